```python
import math
import jax, jax.numpy as jnp
from jax import lax
import numpy as np

D_MODEL = 1024
BATCH = 8
SEQ = 2048
DEPTH = 1

MEM_LEN = 256
NSA_HEADS = 8
NSA_KV_GROUPS = 2
NSA_HPG = NSA_HEADS // NSA_KV_GROUPS
HEAD_DIM = 64
CMP_LEN = 32
CMP_STRIDE = 16
CMP_HIDDEN = 256
SLC_BLOCK = 64
SLC_TOPN = 8
WINDOW = 512
Q_BLOCK = 128
N_BAND = WINDOW // Q_BLOCK
ROPE_THETA = 500000.0
ROPE_DIM = HEAD_DIM // 4
SGU_CHUNK = 128
SGU_GROUPS = 8
SGU_WIDTH = 512
SGU_GROUP_DIM = SGU_WIDTH // SGU_GROUPS
MEM_HEADS = 4
MEM_HEAD_DIM = 128
MEM_WIDTH = MEM_HEADS * MEM_HEAD_DIM
N_BRANCH = 3
N_NSA_BRANCH = 3
N_GROUPS = 4
EXPERTS_PER_GROUP = 8
EXPERT_FF = 256
TOP_K = 2
DN_ALPHA = (2.0 * DEPTH) ** 0.25
DN_BETA = (8.0 * DEPTH) ** -0.25
LN_EPS = 1e-5
NEG = -1e30

Q_W = NSA_HEADS * HEAD_DIM
KV_W = NSA_KV_GROUPS * HEAD_DIM
COL_SIZES = [Q_W, KV_W, KV_W, KV_W, KV_W, KV_W, KV_W,
             NSA_HEADS * N_NSA_BRANCH, 2 * SGU_WIDTH, MEM_WIDTH, N_BRANCH * D_MODEL]
COL_IS_VALUE = [False, False, True, False, True, False, True, False, False, False, False]
SPLIT_POINTS = [int(v) for v in np.cumsum(COL_SIZES)[:-1]]
IN_WIDTH = int(sum(COL_SIZES))

kernel_name = "hybrid_nsa_sgu_mem_hmoe_deepnorm"


def layer_norm(x, g, b):
    xf = x.astype(jnp.float32)
    mu = xf.mean(-1, keepdims=True)
    var = jnp.mean(jnp.square(xf - mu), -1, keepdims=True)
    return ((xf - mu) * lax.rsqrt(var + LN_EPS) * g + b).astype(x.dtype)


def masked_softmax(s, mask):
    s = jnp.where(mask, s.astype(jnp.float32), NEG)
    m = s.max(-1, keepdims=True)
    e = jnp.where(mask, jnp.exp(s - m), 0.0)
    d = e.sum(-1, keepdims=True)
    return e / jnp.where(d > 0, d, 1.0)


def partial_rope(x, pos):
    half = ROPE_DIM // 2
    inv = ROPE_THETA ** (-jnp.arange(0, ROPE_DIM, 2, dtype=jnp.float32) / ROPE_DIM)
    ang = pos.astype(jnp.float32)[:, None] * inv[None, :]
    cos = jnp.cos(ang)[:, None, :]
    sin = jnp.sin(ang)[:, None, :]
    xf = x.astype(jnp.float32)
    x1, x2, xp = xf[..., :half], xf[..., half:ROPE_DIM], xf[..., ROPE_DIM:]
    out = jnp.concatenate([x1 * cos - x2 * sin, x2 * cos + x1 * sin, xp], axis=-1)
    return out.astype(x.dtype)


def compress_blocks(k, pe, w1, w2):
    B, S, G, D = k.shape
    n_cmp = (S - CMP_LEN) // CMP_STRIDE + 1
    idx = jnp.arange(n_cmp)[:, None] * CMP_STRIDE + jnp.arange(CMP_LEN)[None, :]
    blk = k[:, idx] + pe[None, None, :, None, :]
    blk = blk.transpose(0, 1, 3, 2, 4).reshape(B, n_cmp, G, CMP_LEN * D)
    return jax.nn.gelu(blk @ w1, approximate=False) @ w2


def band_blocks(k, nqb):
    B, S, G, D = k.shape
    kb = k.reshape(B, nqb, Q_BLOCK, G, D)
    kp = jnp.pad(kb, ((0, 0), (N_BAND, 0), (0, 0), (0, 0), (0, 0)))
    return jnp.concatenate([kp[:, i:i + nqb] for i in range(N_BAND + 1)], axis=2)


def nsa_mixer(q, kc, vc, ks, vs, kw, vw, gate_logits, pos,
              cmp_pe_k, cmp_w1_k, cmp_w2_k, cmp_pe_v, cmp_w1_v, cmp_w2_v):
    B, S = q.shape[:2]
    G, HPG, D = NSA_KV_GROUPS, NSA_HPG, HEAD_DIM
    dt = q.dtype
    scale = D ** -0.5
    qg = q.reshape(B, S, G, HPG, D)

    k_cmp = compress_blocks(kc, cmp_pe_k, cmp_w1_k, cmp_w2_k)
    v_cmp = compress_blocks(vc, cmp_pe_v, cmp_w1_v, cmp_w2_v)
    n_cmp = k_cmp.shape[1]
    s_cmp = jnp.einsum('bsghd,bngd->bghsn', qg, k_cmp) * scale
    cmp_end = jnp.arange(n_cmp) * CMP_STRIDE + CMP_LEN - 1
    cmp_mask = cmp_end[None, :] <= pos[:, None]
    p_cmp = masked_softmax(s_cmp, cmp_mask)
    o_cmp = jnp.einsum('bghsn,bngd->bsghd', p_cmp.astype(dt), v_cmp)

    n_sel = S // SLC_BLOCK
    top_n = min(SLC_TOPN, n_sel)
    ci = jnp.arange(n_cmp)
    sj = jnp.arange(n_sel)
    overlap = ((ci[:, None] * CMP_STRIDE + CMP_LEN - 1 >= sj[None, :] * SLC_BLOCK) &
               (ci[:, None] * CMP_STRIDE <= sj[None, :] * SLC_BLOCK + SLC_BLOCK - 1)).astype(jnp.float32)
    imp = jnp.einsum('bghsn,nj->bgsj', p_cmp, overlap)
    cur = pos // SLC_BLOCK
    future = sj[None, :] > cur[:, None]
    forced = (sj[None, :] == 0) | (sj[None, :] == cur[:, None]) | (sj[None, :] == cur[:, None] - 1)
    imp = jnp.where(future, NEG, jnp.where(forced, -NEG, imp))
    _, sel = lax.top_k(imp, top_n)

    ks_blk = ks.reshape(B, n_sel, SLC_BLOCK, G, D).transpose(0, 3, 1, 2, 4)
    vs_blk = vs.reshape(B, n_sel, SLC_BLOCK, G, D).transpose(0, 3, 1, 2, 4)
    nqb = S // Q_BLOCK
    q_blocks = qg.reshape(B, nqb, Q_BLOCK, G, HPG, D).transpose(1, 0, 3, 4, 2, 5)
    sel_blocks = sel.reshape(B, G, nqb, Q_BLOCK, top_n).transpose(2, 0, 1, 3, 4)
    pos_blocks = pos.reshape(nqb, Q_BLOCK)
    bi = jnp.arange(B)[:, None, None, None]
    gi = jnp.arange(G)[None, :, None, None]

    def sel_attend(args):
        qb, sb, tb = args
        kk = ks_blk[bi, gi, sb]
        vv = vs_blk[bi, gi, sb]
        s = jnp.einsum('bghqd,bgqnld->bghqnl', qb, kk) * scale
        kpos = sb[..., None] * SLC_BLOCK + jnp.arange(SLC_BLOCK)
        m = kpos <= tb[None, None, :, None, None]
        s = s.reshape(B, G, HPG, Q_BLOCK, top_n * SLC_BLOCK)
        m = m.reshape(B, G, 1, Q_BLOCK, top_n * SLC_BLOCK)
        pr = masked_softmax(s, m)
        return jnp.einsum('bghqk,bgqkd->bghqd', pr.astype(dt),
                          vv.reshape(B, G, Q_BLOCK, top_n * SLC_BLOCK, D))

    o_sel = lax.map(sel_attend, (q_blocks, sel_blocks, pos_blocks))
    o_sel = o_sel.transpose(1, 0, 4, 2, 3, 5).reshape(B, S, G, HPG, D)

    k_band = band_blocks(kw, nqb)
    v_band = band_blocks(vw, nqb)
    qw = qg.reshape(B, nqb, Q_BLOCK, G, HPG, D)
    s_win = jnp.einsum('bcqghd,bckgd->bcghqk', qw, k_band) * scale
    kw_len = (N_BAND + 1) * Q_BLOCK
    kpos = (jnp.arange(nqb)[:, None] - N_BAND) * Q_BLOCK + jnp.arange(kw_len)[None, :]
    diff = pos_blocks[:, :, None] - kpos[:, None, :]
    win_mask = (diff >= 0) & (diff < WINDOW) & (kpos[:, None, :] >= 0)
    p_win = masked_softmax(s_win, win_mask[None, :, None, None])
    o_win = jnp.einsum('bcghqk,bckgd->bcqghd', p_win.astype(dt), v_band).reshape(B, S, G, HPG, D)

    g = jax.nn.sigmoid(gate_logits).reshape(B, S, G, HPG, N_NSA_BRANCH)
    o = g[..., 0:1] * o_cmp + g[..., 1:2] * o_sel + g[..., 2:3] * o_win
    return o.reshape(B, S, NSA_HEADS * D)


def sgu_mixer(z, ln_g, ln_b, w_s, b_s):
    B, S, _ = z.shape
    u, v = z[..., :SGU_WIDTH], z[..., SGU_WIDTH:]
    v = layer_norm(v, ln_g, ln_b)
    nc = S // SGU_CHUNK
    v = v.reshape(B, nc, SGU_CHUNK, SGU_GROUPS, SGU_GROUP_DIM)
    tril = jnp.tril(jnp.ones((SGU_CHUNK, SGU_CHUNK), dtype=bool))
    ws = jnp.where(tril[None], w_s, 0.0).astype(z.dtype)
    sv = jnp.einsum('gts,bcsgd->bctgd', ws, v) + b_s.T[None, None, :, :, None]
    return u * sv.reshape(B, S, SGU_WIDTH)


def memory_attend(mq, mem, w_mem_kv):
    B, S, _ = mq.shape
    M = mem.shape[1]
    q = mq.reshape(B, S, MEM_HEADS, MEM_HEAD_DIM)
    kv = (mem @ w_mem_kv).reshape(B, M, 2, MEM_HEADS, MEM_HEAD_DIM)
    s = jnp.einsum('bshd,bmhd->bhsm', q, kv[:, :, 0]) * (MEM_HEAD_DIM ** -0.5)
    p = jax.nn.softmax(s.astype(jnp.float32), axis=-1).astype(mq.dtype)
    return jnp.einsum('bhsm,bmhd->bshd', p, kv[:, :, 1]).reshape(B, S, MEM_WIDTH)


def hier_moe(x, w_rg, b_rg, w_re, b_re, w_eg, w_eu, w_ed):
    B, S, Dm = x.shape
    T = B * S
    xt = x.reshape(T, Dm)
    gl = (xt @ w_rg + b_rg).astype(jnp.float32)
    gp = jax.nn.softmax(gl, axis=-1)
    gidx = jnp.argmax(gl, axis=-1)
    gprob = jnp.take_along_axis(gp, gidx[:, None], axis=-1)[:, 0]
    el = (xt @ w_re + b_re).astype(jnp.float32).reshape(T, N_GROUPS, EXPERTS_PER_GROUP)
    el = jnp.take_along_axis(el, gidx[:, None, None], axis=1)[:, 0]
    ep = jax.nn.softmax(el, axis=-1)
    tv, ti = lax.top_k(ep, TOP_K)
    tv = tv / tv.sum(-1, keepdims=True)
    ew = (jax.nn.one_hot(ti, EXPERTS_PER_GROUP, dtype=jnp.float32) * tv[..., None]).sum(1)
    cw = (jax.nn.one_hot(gidx, N_GROUPS, dtype=jnp.float32)[:, :, None]
          * ew[:, None, :] * gprob[:, None, None])
    cw = cw.astype(x.dtype).transpose(1, 0, 2)

    def group_ffn(args):
        wg, wu, wd, c = args
        h = jax.nn.silu(jnp.einsum('td,edf->tef', xt, wg)) * jnp.einsum('td,edf->tef', xt, wu)
        return jnp.einsum('tef,te,efd->td', h, c, wd)

    out = lax.map(group_ffn, (w_eg, w_eu, w_ed, cw)).sum(0)
    return out.reshape(B, S, Dm)


def hybrid_layer(x, mem, w_in, cmp_pe_k, cmp_w1_k, cmp_w2_k, cmp_pe_v, cmp_w1_v, cmp_w2_v,
                 sgu_ln_g, sgu_ln_b, sgu_w_s, sgu_b_s, w_mem_kv, w_br_nsa, w_br_sgu, w_br_mem,
                 w_o, ln1_g, ln1_b, w_router_group, b_router_group, w_router_expert,
                 b_router_expert, w_exp_gate, w_exp_up, w_exp_down, ln2_g, ln2_b):
    B, S, _ = x.shape
    pos = jnp.arange(S)
    h = x @ w_in
    (q, kc, vc, ksl, vsl, kwn, vwn, nsa_g, sgu_z, mq, merge_g) = jnp.split(h, SPLIT_POINTS, axis=-1)
    kv_shape = (B, S, NSA_KV_GROUPS, HEAD_DIM)
    q = partial_rope(q.reshape(B, S, NSA_HEADS, HEAD_DIM), pos)
    kc = partial_rope(kc.reshape(kv_shape), pos)
    ksl = partial_rope(ksl.reshape(kv_shape), pos)
    kwn = partial_rope(kwn.reshape(kv_shape), pos)
    o_nsa = nsa_mixer(q, kc, vc.reshape(kv_shape), ksl, vsl.reshape(kv_shape), kwn,
                      vwn.reshape(kv_shape), nsa_g, pos,
                      cmp_pe_k, cmp_w1_k, cmp_w2_k, cmp_pe_v, cmp_w1_v, cmp_w2_v)
    o_sgu = sgu_mixer(jax.nn.gelu(sgu_z, approximate=False), sgu_ln_g, sgu_ln_b, sgu_w_s, sgu_b_s)
    o_mem = memory_attend(mq, mem, w_mem_kv)
    g = jax.nn.sigmoid(merge_g).reshape(B, S, N_BRANCH, D_MODEL)
    y = (g[:, :, 0] * (o_nsa @ w_br_nsa) + g[:, :, 1] * (o_sgu @ w_br_sgu)
         + g[:, :, 2] * (o_mem @ w_br_mem))
    x = layer_norm(DN_ALPHA * x + y @ w_o, ln1_g, ln1_b)
    f = hier_moe(x, w_router_group, b_router_group, w_router_expert, b_router_expert,
                 w_exp_gate, w_exp_up, w_exp_down)
    return layer_norm(DN_ALPHA * x + f, ln2_g, ln2_b)


def setup_inputs(seed: int = 0) -> dict:
    key = jax.random.key(seed)
    ks = jax.random.split(key, 32)
    L = DEPTH

    def nrm(k, shape, scale):
        return jax.random.normal(k, shape, jnp.float32) * scale

    col_scale = jnp.asarray(np.concatenate(
        [np.full((n,), DN_BETA if v else 1.0, np.float32) for n, v in zip(COL_SIZES, COL_IS_VALUE)]))
    mem_scale = jnp.asarray(np.concatenate(
        [np.ones((MEM_WIDTH,), np.float32), np.full((MEM_WIDTH,), DN_BETA, np.float32)]))
    cmp_in = CMP_LEN * HEAD_DIM
    return {
        "x": nrm(ks[0], (BATCH, SEQ, D_MODEL), 1.0),
        "mem": nrm(ks[1], (BATCH, MEM_LEN, D_MODEL), 1.0),
        "w_in": nrm(ks[2], (L, D_MODEL, IN_WIDTH), D_MODEL ** -0.5) * col_scale,
        "cmp_pe_k": nrm(ks[3], (L, CMP_LEN, HEAD_DIM), 0.1),
        "cmp_w1_k": nrm(ks[4], (L, cmp_in, CMP_HIDDEN), cmp_in ** -0.5),
        "cmp_w2_k": nrm(ks[5], (L, CMP_HIDDEN, HEAD_DIM), CMP_HIDDEN ** -0.5),
        "cmp_pe_v": nrm(ks[6], (L, CMP_LEN, HEAD_DIM), 0.1),
        "cmp_w1_v": nrm(ks[7], (L, cmp_in, CMP_HIDDEN), cmp_in ** -0.5),
        "cmp_w2_v": nrm(ks[8], (L, CMP_HIDDEN, HEAD_DIM), CMP_HIDDEN ** -0.5),
        "sgu_ln_g": 1.0 + nrm(ks[9], (L, SGU_WIDTH), 0.02),
        "sgu_ln_b": nrm(ks[10], (L, SGU_WIDTH), 0.02),
        "sgu_w_s": nrm(ks[11], (L, SGU_GROUPS, SGU_CHUNK, SGU_CHUNK), SGU_CHUNK ** -0.5),
        "sgu_b_s": 1.0 + nrm(ks[12], (L, SGU_GROUPS, SGU_CHUNK), 0.02),
        "w_mem_kv": nrm(ks[13], (L, D_MODEL, 2 * MEM_WIDTH), D_MODEL ** -0.5) * mem_scale,
        "w_br_nsa": nrm(ks[14], (L, NSA_HEADS * HEAD_DIM, D_MODEL), (NSA_HEADS * HEAD_DIM) ** -0.5),
        "w_br_sgu": nrm(ks[15], (L, SGU_WIDTH, D_MODEL), SGU_WIDTH ** -0.5),
        "w_br_mem": nrm(ks[16], (L, MEM_WIDTH, D_MODEL), MEM_WIDTH ** -0.5),
        "w_o": nrm(ks[17], (L, D_MODEL, D_MODEL), D_MODEL ** -0.5) * DN_BETA,
        "ln1_g": 1.0 + nrm(ks[18], (L, D_MODEL), 0.02),
        "ln1_b": nrm(ks[19], (L, D_MODEL), 0.02),
        "w_router_group": nrm(ks[20], (L, D_MODEL, N_GROUPS), D_MODEL ** -0.5),
        "b_router_group": nrm(ks[21], (L, N_GROUPS), 0.01),
        "w_router_expert": nrm(ks[22], (L, D_MODEL, N_GROUPS * EXPERTS_PER_GROUP), D_MODEL ** -0.5),
        "b_router_expert": nrm(ks[23], (L, N_GROUPS * EXPERTS_PER_GROUP), 0.01),
        "w_exp_gate": nrm(ks[24], (L, N_GROUPS, EXPERTS_PER_GROUP, D_MODEL, EXPERT_FF), D_MODEL ** -0.5),
        "w_exp_up": nrm(ks[25], (L, N_GROUPS, EXPERTS_PER_GROUP, D_MODEL, EXPERT_FF), D_MODEL ** -0.5),
        "w_exp_down": nrm(ks[26], (L, N_GROUPS, EXPERTS_PER_GROUP, EXPERT_FF, D_MODEL),
                          EXPERT_FF ** -0.5) * DN_BETA,
        "ln2_g": 1.0 + nrm(ks[27], (L, D_MODEL), 0.02),
        "ln2_b": nrm(ks[28], (L, D_MODEL), 0.02),
    }


def reference(x, mem, w_in, cmp_pe_k, cmp_w1_k, cmp_w2_k, cmp_pe_v, cmp_w1_v, cmp_w2_v,
              sgu_ln_g, sgu_ln_b, sgu_w_s, sgu_b_s, w_mem_kv, w_br_nsa, w_br_sgu, w_br_mem,
              w_o, ln1_g, ln1_b, w_router_group, b_router_group, w_router_expert,
              b_router_expert, w_exp_gate, w_exp_up, w_exp_down, ln2_g, ln2_b):
    for l in range(DEPTH):
        x = hybrid_layer(x, mem, w_in[l], cmp_pe_k[l], cmp_w1_k[l], cmp_w2_k[l], cmp_pe_v[l],
                         cmp_w1_v[l], cmp_w2_v[l], sgu_ln_g[l], sgu_ln_b[l], sgu_w_s[l],
                         sgu_b_s[l], w_mem_kv[l], w_br_nsa[l], w_br_sgu[l], w_br_mem[l], w_o[l],
                         ln1_g[l], ln1_b[l], w_router_group[l], b_router_group[l],
                         w_router_expert[l], b_router_expert[l], w_exp_gate[l], w_exp_up[l],
                         w_exp_down[l], ln2_g[l], ln2_b[l])
    return x
```

```python
import functools

import numpy as np
import jax
import jax.numpy as jnp
from jax import lax
from jax.experimental import pallas as pl
from jax.experimental.pallas import tpu as pltpu

NSA_HEADS = 8
NSA_KV_GROUPS = 2
NSA_HPG = NSA_HEADS // NSA_KV_GROUPS
HEAD_DIM = 64
CMP_LEN = 32
CMP_STRIDE = 16
CMP_HIDDEN = 256
SLC_BLOCK = 64
SLC_TOPN = 8
WINDOW = 512
Q_BLOCK = 128
N_BAND = WINDOW // Q_BLOCK
ROPE_THETA = 500000.0
ROPE_DIM = HEAD_DIM // 4
SGU_CHUNK = 128
SGU_GROUPS = 8
SGU_WIDTH = 512
MEM_HEADS = 4
MEM_HEAD_DIM = 128
MEM_WIDTH = MEM_HEADS * MEM_HEAD_DIM
N_GROUPS = 4
EXPERTS_PER_GROUP = 8
EXPERT_FF = 256
DEPTH = 1
DN_ALPHA = (2.0 * DEPTH) ** 0.25
LN_EPS = 1e-5
NEG = -1e30

LANES = 128
Q_W = NSA_HEADS * HEAD_DIM
KV_W = NSA_KV_GROUPS * HEAD_DIM
GATE_W = NSA_HEADS * 3
VMEM_LIMIT = 56 * 1024 * 1024

BF16 = jnp.bfloat16
F32 = jnp.float32


def _dot(a, b):
    return jnp.dot(a, b, preferred_element_type=F32)


def _dot_nt(a, b):
    return lax.dot_general(a, b, (((1,), (1,)), ((), ())), preferred_element_type=F32)


def _sigmoid(x):
    return 1.0 / (1.0 + jnp.exp(-x))


def _gelu(x):
    return 0.5 * x * (1.0 + lax.erf(x * (2.0 ** -0.5)))


def _layer_norm(x, g, b):
    mu = jnp.mean(x, axis=-1, keepdims=True)
    xc = x - mu
    var = jnp.mean(xc * xc, axis=-1, keepdims=True)
    return xc * lax.rsqrt(var + LN_EPS) * g + b


PROJ_TM = 512
_ROPE_COLS = Q_W + 3 * KV_W
_V_OFF = _ROPE_COLS
_G_OFF = _V_OFF + 3 * KV_W
_SGU_OFF = _G_OFF + LANES
_MQ_OFF = _SGU_OFF + 2 * SGU_WIDTH
_PROJ_COLS = _MQ_OFF + MEM_WIDTH


def _proj_kernel(x_ref, w_ref, rope_ref, lng_ref, lnb_ref, ws_ref, bs_ref,
                 q_ref, kc_ref, ksl_ref, kwn_ref, vc_ref, vsl_ref, vwn_ref, gate_ref, osgu_ref, mq_ref):
    xb = x_ref[...].astype(BF16)
    lane = lax.broadcasted_iota(jnp.int32, (PROJ_TM, LANES), 1)
    low = lane < HEAD_DIM
    cos = rope_ref[:, 0:LANES]
    s_dn = rope_ref[:, LANES:2 * LANES]
    s_up = rope_ref[:, 2 * LANES:3 * LANES]

    h = _dot(xb, w_ref[:, 0:_ROPE_COLS])
    k_refs = (kc_ref, ksl_ref, kwn_ref)
    for j in range(_ROPE_COLS // LANES):
        blk = h[:, j * LANES:(j + 1) * LANES]
        r = (blk * cos + pltpu.roll(blk, LANES - ROPE_DIM // 2, 1) * s_dn
             + pltpu.roll(blk, ROPE_DIM // 2, 1) * s_up)
        if j < Q_W // LANES:
            r = r * (HEAD_DIM ** -0.5)
            sw = pltpu.roll(r, HEAD_DIM, 1)
            g = (2 * j) // NSA_HPG
            if g == 0:
                h0 = jnp.where(low, r, 0.0)
                h1 = jnp.where(low, sw, 0.0)
            else:
                h0 = jnp.where(low, 0.0, sw)
                h1 = jnp.where(low, 0.0, r)
            q_ref[:, (2 * j) * LANES:(2 * j + 1) * LANES] = h0.astype(BF16)
            q_ref[:, (2 * j + 1) * LANES:(2 * j + 2) * LANES] = h1.astype(BF16)
        else:
            k_refs[j - Q_W // LANES][...] = r.astype(BF16)

    hv = _dot(xb, w_ref[:, _V_OFF:_G_OFF])
    vc_ref[...] = hv[:, 0:LANES].astype(BF16)
    vsl_ref[...] = hv[:, LANES:2 * LANES].astype(BF16)
    vwn_ref[...] = hv[:, 2 * LANES:3 * LANES].astype(BF16)

    gate_ref[...] = _sigmoid(_dot(xb, w_ref[:, _G_OFF:_SGU_OFF]))

    mq_ref[...] = _dot(xb, w_ref[:, _MQ_OFF:_PROJ_COLS]).astype(BF16)

    z = _gelu(_dot(xb, w_ref[:, _SGU_OFF:_MQ_OFF]))
    u = z[:, 0:SGU_WIDTH]
    v = _layer_norm(z[:, SGU_WIDTH:2 * SGU_WIDTH], lng_ref[...], lnb_ref[...]).astype(BF16)
    lane_c = lax.broadcasted_iota(jnp.int32, (SGU_CHUNK, LANES), 1)
    low_c = lane_c < (SGU_WIDTH // SGU_GROUPS)
    for ci in range(PROJ_TM // SGU_CHUNK):
        rows = slice(ci * SGU_CHUNK, (ci + 1) * SGU_CHUNK)
        for gp in range(SGU_WIDTH // LANES):
            cols = slice(gp * LANES, (gp + 1) * LANES)
            vblk = v[rows, cols]
            sv = jnp.where(low_c, _dot(ws_ref[2 * gp], vblk), _dot(ws_ref[2 * gp + 1], vblk))
            sv = sv + bs_ref[:, cols]
            osgu_ref[rows, cols] = (u[rows, cols] * sv).astype(BF16)


def _proj_call(x2, wp, rope, lng, lnb, ws, bs, seq):
    t = x2.shape[0]
    d = x2.shape[1]
    nt = t // PROJ_TM
    per_seq = seq // PROJ_TM
    row = lambda i: (i, 0)
    const2 = lambda i: (0, 0)
    out_shapes = (
        jax.ShapeDtypeStruct((t, NSA_HEADS * LANES), BF16),
        *[jax.ShapeDtypeStruct((t, KV_W), BF16) for _ in range(6)],
        jax.ShapeDtypeStruct((t, LANES), F32),
        jax.ShapeDtypeStruct((t, SGU_WIDTH), BF16),
        jax.ShapeDtypeStruct((t, MEM_WIDTH), BF16),
    )
    out_specs = (
        pl.BlockSpec((PROJ_TM, NSA_HEADS * LANES), row),
        *[pl.BlockSpec((PROJ_TM, KV_W), row) for _ in range(6)],
        pl.BlockSpec((PROJ_TM, LANES), row),
        pl.BlockSpec((PROJ_TM, SGU_WIDTH), row),
        pl.BlockSpec((PROJ_TM, MEM_WIDTH), row),
    )
    return pl.pallas_call(
        _proj_kernel,
        grid=(nt,),
        in_specs=[
            pl.BlockSpec((PROJ_TM, d), row),
            pl.BlockSpec((d, _PROJ_COLS), const2),
            pl.BlockSpec((PROJ_TM, 3 * LANES), lambda i: (i % per_seq, 0)),
            pl.BlockSpec((1, SGU_WIDTH), const2),
            pl.BlockSpec((1, SGU_WIDTH), const2),
            pl.BlockSpec((SGU_GROUPS, SGU_CHUNK, SGU_CHUNK), lambda i: (0, 0, 0)),
            pl.BlockSpec((SGU_CHUNK, SGU_WIDTH), const2),
        ],
        out_specs=out_specs,
        out_shape=out_shapes,
        compiler_params=pltpu.CompilerParams(
            dimension_semantics=("parallel",), vmem_limit_bytes=VMEM_LIMIT),
        name="proj",
    )(x2, wp, rope, lng, lnb, ws, bs)


N_A = 128
A_W = CMP_STRIDE * KV_W


def _compress_kernel(ka_ref, va_ref, pek_ref, pev_ref, w1k_ref, w2k_ref, w1v_ref, w2v_ref,
                     kcmp_ref, vcmp_ref):
    def one(a_ref, pe_ref, w1_ref, w2_ref, out_ref):
        a = a_ref[...].astype(F32)
        top = (a + pe_ref[0:1, :]).astype(BF16)
        bot = (a + pe_ref[1:2, :]).astype(BF16)
        h1 = _dot(top, w1_ref[0])
        h2 = _dot(bot, w1_ref[1])
        pre = h1 + pltpu.roll(h2, N_A - 1, 0)
        act = _gelu(pre).astype(BF16)
        out_ref[...] = _dot(act, w2_ref[...]).astype(BF16)

    one(ka_ref, pek_ref, w1k_ref, w2k_ref, kcmp_ref)
    one(va_ref, pev_ref, w1v_ref, w2v_ref, vcmp_ref)


def _compress_call(ka, va, pek, pev, w1k, w2k, w1v, w2v, batch):
    row = lambda b: (b, 0)
    c2 = lambda b: (0, 0)
    c3 = lambda b: (0, 0, 0)
    hid2 = NSA_KV_GROUPS * CMP_HIDDEN
    return pl.pallas_call(
        _compress_kernel,
        grid=(batch,),
        in_specs=[
            pl.BlockSpec((N_A, A_W), row),
            pl.BlockSpec((N_A, A_W), row),
            pl.BlockSpec((8, A_W), c2),
            pl.BlockSpec((8, A_W), c2),
            pl.BlockSpec((2, A_W, hid2), c3),
            pl.BlockSpec((hid2, KV_W), c2),
            pl.BlockSpec((2, A_W, hid2), c3),
            pl.BlockSpec((hid2, KV_W), c2),
        ],
        out_specs=(pl.BlockSpec((N_A, KV_W), row), pl.BlockSpec((N_A, KV_W), row)),
        out_shape=(jax.ShapeDtypeStruct((batch * N_A, KV_W), BF16),
                   jax.ShapeDtypeStruct((batch * N_A, KV_W), BF16)),
        compiler_params=pltpu.CompilerParams(
            dimension_semantics=("parallel",), vmem_limit_bytes=VMEM_LIMIT),
        name="compress",
    )(ka, va, pek, pev, w1k, w2k, w1v, w2v)


N_SEL = 32
SEL_CHUNK = 512
HQ = NSA_HPG * Q_BLOCK


def _nsa_kernel(q_ref, ksl_ref, kwn_ref, vslt_ref, vwnt_ref, kcmp_ref, vcmpt_ref, gt_ref,
                et_ref, ot_ref, o_ref, m_s, l_s, acc_s, out_s):
    c = pl.program_id(1)
    lane_hq = lax.broadcasted_iota(jnp.int32, (1, HQ), 1)
    pos_hq = c * Q_BLOCK + (lane_hq & (Q_BLOCK - 1))
    pos_q = c * Q_BLOCK + lax.broadcasted_iota(jnp.int32, (1, Q_BLOCK), 1)
    n_seq_blocks = kwn_ref.shape[0] // Q_BLOCK

    for g in range(NSA_KV_GROUPS):
        qg = jnp.concatenate(
            [q_ref[:, (g * NSA_HPG + hh) * LANES:(g * NSA_HPG + hh + 1) * LANES]
             for hh in range(NSA_HPG)], axis=0)

        s_c = _dot_nt(kcmp_ref[...], qg)
        n_idx = lax.broadcasted_iota(jnp.int32, (N_A, HQ), 0)
        valid_c = (n_idx * CMP_STRIDE + (CMP_LEN - 1)) <= pos_hq
        sm_c = jnp.where(valid_c, s_c, NEG)
        m_c = jnp.max(sm_c, axis=0, keepdims=True)
        e_c = jnp.where(valid_c, jnp.exp(sm_c - m_c), 0.0)
        d_c = jnp.sum(e_c, axis=0, keepdims=True)
        p_c = e_c / jnp.where(d_c > 0, d_c, 1.0)
        o_cmp = _dot(vcmpt_ref[...], p_c.astype(BF16))

        ps = (p_c[:, 0:Q_BLOCK] + p_c[:, Q_BLOCK:2 * Q_BLOCK]
              + p_c[:, 2 * Q_BLOCK:3 * Q_BLOCK] + p_c[:, 3 * Q_BLOCK:4 * Q_BLOCK])
        p_hi = ps.astype(BF16)
        r1 = ps - p_hi.astype(F32)
        p_lo = r1.astype(BF16)
        p_lo2 = (r1 - p_lo.astype(F32)).astype(BF16)
        ot = ot_ref[...]
        imp = _dot(ot, p_hi) + _dot(ot, p_lo) + _dot(ot, p_lo2)

        j_idx = lax.broadcasted_iota(jnp.int32, (N_SEL, Q_BLOCK), 0)
        cur = pos_q // SLC_BLOCK
        future = j_idx > cur
        forced = (j_idx == 0) | (j_idx == cur) | (j_idx == cur - 1)
        imp = jnp.where(future, NEG, jnp.where(forced, -NEG, imp))
        rank = jnp.zeros((N_SEL, Q_BLOCK), F32)
        for i in range(N_SEL):
            row = imp[i:i + 1, :]
            beats = (row > imp) | ((row == imp) & (j_idx > i))
            rank = rank + jnp.where(beats, 1.0, 0.0)
        sel = jnp.where(rank < float(SLC_TOPN), 1.0, 0.0).astype(BF16)

        m_s[...] = jnp.full((1, HQ), NEG, F32)
        l_s[...] = jnp.zeros((1, HQ), F32)
        acc_s[...] = jnp.zeros((LANES, HQ), F32)
        for kc in range(ksl_ref.shape[0] // SEL_CHUNK):
            @pl.when(kc * (SEL_CHUNK // Q_BLOCK) <= c)
            def _sel_chunk(kc=kc, qg=qg, sel=sel):
                ks = slice(kc * SEL_CHUNK, (kc + 1) * SEL_CHUNK)
                s = _dot_nt(ksl_ref[ks, :], qg)
                picked = _dot(et_ref[ks, :], sel)
                kpos = kc * SEL_CHUNK + lax.broadcasted_iota(jnp.int32, (SEL_CHUNK, Q_BLOCK), 0)
                cap = jnp.where((picked > 0.5) & (kpos <= pos_q), -NEG, NEG)
                sm = jnp.minimum(s, jnp.concatenate([cap] * NSA_HPG, axis=1))
                m_old = m_s[...]
                m_new = jnp.maximum(m_old, jnp.max(sm, axis=0, keepdims=True))
                alpha = jnp.exp(m_old - m_new)
                e = jnp.exp(sm - m_new)
                l_s[...] = alpha * l_s[...] + jnp.sum(e, axis=0, keepdims=True)
                acc_s[...] = alpha * acc_s[...] + _dot(vslt_ref[:, ks], e.astype(BF16))
                m_s[...] = m_new
        o_sel = acc_s[...] / l_s[...]

        parts = []
        blks = []
        for i in range(N_BAND + 1):
            blk = c - N_BAND + i
            blk_c = jnp.maximum(blk, 0)
            blks.append(blk_c)
            k_i = kwn_ref[pl.ds(pl.multiple_of(blk_c * Q_BLOCK, Q_BLOCK), Q_BLOCK), :]
            s_i = _dot_nt(k_i, qg)
            kpos = blk * Q_BLOCK + lax.broadcasted_iota(jnp.int32, (Q_BLOCK, HQ), 0)
            diff = pos_hq - kpos
            valid = (diff >= 0) & (diff < WINDOW) & (kpos >= 0)
            parts.append(jnp.where(valid, s_i, NEG))
        sm_w = jnp.concatenate(parts, axis=0)
        m_w = jnp.max(sm_w, axis=0, keepdims=True)
        e_w = jnp.exp(sm_w - m_w)
        l_w = jnp.sum(e_w, axis=0, keepdims=True)
        e_wb = e_w.astype(BF16)
        o_win = _dot(vwnt_ref[blks[0]], e_wb[0:Q_BLOCK, :])
        for i in range(1, N_BAND + 1):
            o_win = o_win + _dot(vwnt_ref[blks[i]], e_wb[i * Q_BLOCK:(i + 1) * Q_BLOCK, :])
        o_win = o_win / l_w

        def gate_row(br):
            return jnp.concatenate(
                [gt_ref[(g * NSA_HPG + hh) * 3 + br:(g * NSA_HPG + hh) * 3 + br + 1, :]
                 for hh in range(NSA_HPG)], axis=1)
        o_t = gate_row(0) * o_cmp + gate_row(1) * o_sel + gate_row(2) * o_win
        for hh in range(NSA_HPG):
            h = g * NSA_HPG + hh
            out_s[h * HEAD_DIM:(h + 1) * HEAD_DIM, :] = (
                o_t[g * HEAD_DIM:(g + 1) * HEAD_DIM, hh * Q_BLOCK:(hh + 1) * Q_BLOCK])

    o_ref[...] = out_s[...].T.astype(BF16)


def _nsa_call(qpad, ksl, kwn, vslt, vwnt, kcmp, vcmpt, gt, et, ot, batch, seq):
    nqb = seq // Q_BLOCK
    qrow = lambda b, c: (b * nqb + c, 0)
    brow = lambda b, c: (b, 0)
    c2 = lambda b, c: (0, 0)
    return pl.pallas_call(
        _nsa_kernel,
        grid=(batch, nqb),
        in_specs=[
            pl.BlockSpec((Q_BLOCK, NSA_HEADS * LANES), qrow),
            pl.BlockSpec((seq, KV_W), brow),
            pl.BlockSpec((seq, KV_W), brow),
            pl.BlockSpec((KV_W, seq), brow),
            pl.BlockSpec((nqb, KV_W, Q_BLOCK), lambda b, c: (b, 0, 0)),
            pl.BlockSpec((N_A, KV_W), brow),
            pl.BlockSpec((KV_W, N_A), brow),
            pl.BlockSpec((LANES, Q_BLOCK), lambda b, c: (0, b * nqb + c)),
            pl.BlockSpec((seq, N_SEL), c2),
            pl.BlockSpec((N_SEL, N_A), c2),
        ],
        out_specs=pl.BlockSpec((Q_BLOCK, Q_W), qrow),
        out_shape=jax.ShapeDtypeStruct((batch * seq, Q_W), BF16),
        scratch_shapes=[
            pltpu.VMEM((1, HQ), F32),
            pltpu.VMEM((1, HQ), F32),
            pltpu.VMEM((LANES, HQ), F32),
            pltpu.VMEM((Q_W, Q_BLOCK), F32),
        ],
        compiler_params=pltpu.CompilerParams(
            dimension_semantics=("parallel", "arbitrary"), vmem_limit_bytes=VMEM_LIMIT),
        name="nsa",
    )(qpad, ksl, kwn, vslt, vwnt, kcmp, vcmpt, gt, et, ot)


MEM_TM = 512


def _memattn_kernel(mq_ref, mem_ref, wkv_ref, o_ref, kv_s):
    @pl.when(pl.program_id(1) == 0)
    def _():
        kv_s[...] = _dot(mem_ref[...].astype(BF16), wkv_ref[...]).astype(BF16)

    for h in range(MEM_HEADS):
        cols = slice(h * MEM_HEAD_DIM, (h + 1) * MEM_HEAD_DIM)
        s = _dot_nt(mq_ref[:, cols], kv_s[:, cols]) * (MEM_HEAD_DIM ** -0.5)
        m = jnp.max(s, axis=-1, keepdims=True)
        e = jnp.exp(s - m)
        p = e / jnp.sum(e, axis=-1, keepdims=True)
        vcols = slice(MEM_WIDTH + h * MEM_HEAD_DIM, MEM_WIDTH + (h + 1) * MEM_HEAD_DIM)
        o_ref[:, cols] = _dot(p.astype(BF16), kv_s[:, vcols]).astype(BF16)


def _memattn_call(mq, mem2, wkv, batch, seq):
    m_len = mem2.shape[0] // batch
    d = mem2.shape[1]
    per = seq // MEM_TM
    return pl.pallas_call(
        _memattn_kernel,
        grid=(batch, per),
        in_specs=[
            pl.BlockSpec((MEM_TM, MEM_WIDTH), lambda b, i: (b * per + i, 0)),
            pl.BlockSpec((m_len, d), lambda b, i: (b, 0)),
            pl.BlockSpec((d, 2 * MEM_WIDTH), lambda b, i: (0, 0)),
        ],
        out_specs=pl.BlockSpec((MEM_TM, MEM_WIDTH), lambda b, i: (b * per + i, 0)),
        out_shape=jax.ShapeDtypeStruct((batch * seq, MEM_WIDTH), BF16),
        scratch_shapes=[pltpu.VMEM((m_len, 2 * MEM_WIDTH), BF16)],
        compiler_params=pltpu.CompilerParams(
            dimension_semantics=("parallel", "arbitrary"), vmem_limit_bytes=VMEM_LIMIT),
        name="memattn",
    )(mq, mem2, wkv)


MERGE_TM = 512


def _merge_kernel(x_ref, onsa_ref, osgu_ref, omem_ref, wg_ref, wbn_ref, wbs_ref, wbm_ref, wo_ref,
                  g_ref, b_ref, out_ref):
    x = x_ref[...]
    xb = x.astype(BF16)
    d = x.shape[1]
    y = None
    for br, (o_r, w_r) in enumerate(((onsa_ref, wbn_ref), (osgu_ref, wbs_ref), (omem_ref, wbm_ref))):
        gate = _sigmoid(_dot(xb, wg_ref[:, br * d:(br + 1) * d]))
        term = gate * _dot(o_r[...], w_r[...])
        y = term if y is None else y + term
    z = DN_ALPHA * x + _dot(y.astype(BF16), wo_ref[...])
    out_ref[...] = _layer_norm(z, g_ref[...], b_ref[...])


def _merge_call(x2, onsa, osgu, omem, wg, wbn, wbs, wbm, wo, g1, b1):
    t, d = x2.shape
    row = lambda i: (i, 0)
    c2 = lambda i: (0, 0)
    full = lambda a: pl.BlockSpec(a.shape, c2)
    return pl.pallas_call(
        _merge_kernel,
        grid=(t // MERGE_TM,),
        in_specs=[
            pl.BlockSpec((MERGE_TM, d), row),
            pl.BlockSpec((MERGE_TM, onsa.shape[1]), row),
            pl.BlockSpec((MERGE_TM, osgu.shape[1]), row),
            pl.BlockSpec((MERGE_TM, omem.shape[1]), row),
            full(wg), full(wbn), full(wbs), full(wbm), full(wo), full(g1), full(b1),
        ],
        out_specs=pl.BlockSpec((MERGE_TM, d), row),
        out_shape=jax.ShapeDtypeStruct((t, d), F32),
        compiler_params=pltpu.CompilerParams(
            dimension_semantics=("parallel",), vmem_limit_bytes=VMEM_LIMIT),
        name="merge",
    )(x2, onsa, osgu, omem, wg, wbn, wbs, wbm, wo, g1, b1)


MOE_TM = 512
_E_OFF = N_GROUPS


def _moe_kernel(x_ref, wr_ref, br_ref, wg_ref, wu_ref, wd_ref, g_ref, b_ref, out_ref,
                xb_s, cw_s, acc_s):
    grp = pl.program_id(1)

    @pl.when(grp == 0)
    def _route():
        xb = x_ref[...].astype(BF16)
        xb_s[...] = xb
        logits = _dot(xb, wr_ref[...]) + br_ref[...]
        lane = lax.broadcasted_iota(jnp.int32, logits.shape, 1).astype(F32)
        is_g = lane < float(N_GROUPS)
        gl = jnp.where(is_g, logits, NEG)
        gmax = jnp.max(gl, axis=-1, keepdims=True)
        gidx = jnp.min(jnp.where(is_g & (gl == gmax), lane, float(LANES)), axis=-1, keepdims=True)
        gsum = jnp.sum(jnp.where(is_g, jnp.exp(gl - gmax), 0.0), axis=-1, keepdims=True)
        gprob = 1.0 / gsum
        lo = float(_E_OFF) + gidx * float(EXPERTS_PER_GROUP)
        in_e = (lane >= lo) & (lane < lo + float(EXPERTS_PER_GROUP))
        el = jnp.where(in_e, logits, NEG)
        emax = jnp.max(el, axis=-1, keepdims=True)
        ee = jnp.where(in_e, jnp.exp(el - emax), 0.0)
        ep = ee / jnp.sum(ee, axis=-1, keepdims=True)
        t1 = jnp.max(jnp.where(in_e, ep, -1.0), axis=-1, keepdims=True)
        i1 = jnp.min(jnp.where(in_e & (ep == t1), lane, float(LANES)), axis=-1, keepdims=True)
        rest = in_e & (lane != i1)
        t2 = jnp.max(jnp.where(rest, ep, -1.0), axis=-1, keepdims=True)
        i2 = jnp.min(jnp.where(rest & (ep == t2), lane, float(LANES)), axis=-1, keepdims=True)
        den = t1 + t2
        ew = jnp.where(lane == i1, t1 / den, 0.0) + jnp.where(lane == i2, t2 / den, 0.0)
        cw_s[...] = ew * gprob
        acc_s[...] = jnp.zeros(acc_s.shape, F32)

    xb = xb_s[...]
    acc = acc_s[...]
    for e in range(EXPERTS_PER_GROUP):
        hg = _dot(xb, wg_ref[0, e])
        hu = _dot(xb, wu_ref[0, e])
        hid = (hg * _sigmoid(hg)) * hu
        lane = lax.broadcasted_iota(jnp.int32, cw_s.shape, 1)
        want = _E_OFF + grp * EXPERTS_PER_GROUP + e
        cw = jnp.sum(jnp.where(lane == want, cw_s[...], 0.0), axis=-1, keepdims=True)
        acc = acc + _dot((hid * cw).astype(BF16), wd_ref[0, e])
    acc_s[...] = acc

    @pl.when(grp == N_GROUPS - 1)
    def _fin():
        z = DN_ALPHA * x_ref[...] + acc_s[...]
        out_ref[...] = _layer_norm(z, g_ref[...], b_ref[...])


def _moe_call(x1, wr, br, wg, wu, wd, g2, b2):
    t, d = x1.shape
    row = lambda i, g: (i, 0)
    c2 = lambda i, g: (0, 0)
    return pl.pallas_call(
        _moe_kernel,
        grid=(t // MOE_TM, N_GROUPS),
        in_specs=[
            pl.BlockSpec((MOE_TM, d), row),
            pl.BlockSpec(wr.shape, c2),
            pl.BlockSpec(br.shape, c2),
            pl.BlockSpec((1, EXPERTS_PER_GROUP, d, EXPERT_FF), lambda i, g: (g, 0, 0, 0)),
            pl.BlockSpec((1, EXPERTS_PER_GROUP, d, EXPERT_FF), lambda i, g: (g, 0, 0, 0)),
            pl.BlockSpec((1, EXPERTS_PER_GROUP, EXPERT_FF, d), lambda i, g: (g, 0, 0, 0)),
            pl.BlockSpec(g2.shape, c2),
            pl.BlockSpec(b2.shape, c2),
        ],
        out_specs=pl.BlockSpec((MOE_TM, d), row),
        out_shape=jax.ShapeDtypeStruct((t, d), F32),
        scratch_shapes=[
            pltpu.VMEM((MOE_TM, d), BF16),
            pltpu.VMEM((MOE_TM, LANES), F32),
            pltpu.VMEM((MOE_TM, d), F32),
        ],
        compiler_params=pltpu.CompilerParams(
            dimension_semantics=("parallel", "arbitrary"), vmem_limit_bytes=VMEM_LIMIT),
        name="moe",
    )(x1, wr, br, wg, wu, wd, g2, b2)


def _rope_table(seq):
    half = ROPE_DIM // 2
    inv = ROPE_THETA ** (-jnp.arange(0, ROPE_DIM, 2, dtype=F32) / ROPE_DIM)
    ang = jnp.arange(seq, dtype=F32)[:, None] * inv[None, :]
    cos, sin = jnp.cos(ang), jnp.sin(ang)
    rest = HEAD_DIM - ROPE_DIM
    one = jnp.ones((seq, rest), F32)
    zero = jnp.zeros((seq, rest), F32)
    zh = jnp.zeros((seq, half), F32)
    c_h = jnp.concatenate([cos, cos, one], axis=1)
    dn_h = jnp.concatenate([-sin, zh, zero], axis=1)
    up_h = jnp.concatenate([zh, sin, zero], axis=1)
    rep = LANES // HEAD_DIM
    return jnp.concatenate([jnp.tile(c_h, (1, rep)), jnp.tile(dn_h, (1, rep)), jnp.tile(up_h, (1, rep))],
                           axis=1)


def _expand_cmp_weights(pe, w1, w2):
    g_n = NSA_KV_GROUPS
    w1r = w1.reshape(2, CMP_STRIDE, HEAD_DIM, CMP_HIDDEN)
    eye = jnp.eye(g_n, dtype=w1.dtype)
    w1e = jnp.einsum('pldh,ge->plgdeh', w1r, eye).reshape(2, A_W, g_n * CMP_HIDDEN)
    w2e = jnp.einsum('hd,ge->ghed', w2, eye).reshape(g_n * CMP_HIDDEN, g_n * HEAD_DIM)
    per = pe.reshape(2, CMP_STRIDE, 1, HEAD_DIM)
    pee = jnp.broadcast_to(per, (2, CMP_STRIDE, g_n, HEAD_DIM)).reshape(2, A_W)
    pee = jnp.concatenate([pee, jnp.zeros((6, A_W), pe.dtype)], axis=0)
    return pee.astype(F32), w1e.astype(BF16), w2e.astype(BF16)


def kernel(x, mem, w_in, cmp_pe_k, cmp_w1_k, cmp_w2_k, cmp_pe_v, cmp_w1_v, cmp_w2_v, sgu_ln_g, sgu_ln_b,
           sgu_w_s, sgu_b_s, w_mem_kv, w_br_nsa, w_br_sgu, w_br_mem, w_o, ln1_g, ln1_b, w_router_group,
           b_router_group, w_router_expert, b_router_expert, w_exp_gate, w_exp_up, w_exp_down, ln2_g, ln2_b):
    batch, seq, d = x.shape
    t = batch * seq
    assert w_in.shape[0] == DEPTH == 1
    assert seq % PROJ_TM == 0 and seq // CMP_STRIDE == N_A and seq // SLC_BLOCK == N_SEL

    w = w_in[0]
    offs = np.cumsum([0, Q_W, KV_W, KV_W, KV_W, KV_W, KV_W, KV_W, GATE_W, 2 * SGU_WIDTH, MEM_WIDTH, 3 * d])
    seg = lambda i: w[:, int(offs[i]):int(offs[i + 1])]
    gate_w = jnp.pad(seg(7), ((0, 0), (0, LANES - GATE_W)))
    wp = jnp.concatenate([seg(0), seg(1), seg(3), seg(5), seg(2), seg(4), seg(6), gate_w, seg(8), seg(9)],
                         axis=1).astype(BF16)
    w_merge = seg(10).astype(BF16)
    rope = _rope_table(seq)
    tril = jnp.tril(jnp.ones((SGU_CHUNK, SGU_CHUNK), dtype=bool))
    ws = jnp.where(tril[None], sgu_w_s[0], 0.0).astype(BF16)
    bs = jnp.repeat(sgu_b_s[0].T, SGU_WIDTH // SGU_GROUPS, axis=1)

    x2 = x.reshape(t, d)
    (qpad, kc, ksl, kwn, vc, vsl, vwn, gate, osgu, mq) = _proj_call(
        x2, wp, rope, sgu_ln_g[0][None], sgu_ln_b[0][None], ws, bs, seq)

    pek, w1k, w2k = _expand_cmp_weights(cmp_pe_k[0], cmp_w1_k[0], cmp_w2_k[0])
    pev, w1v, w2v = _expand_cmp_weights(cmp_pe_v[0], cmp_w1_v[0], cmp_w2_v[0])
    kcmp, vcmp = _compress_call(kc.reshape(batch * N_A, A_W), vc.reshape(batch * N_A, A_W),
                                pek, pev, w1k, w2k, w1v, w2v, batch)

    nqb = seq // Q_BLOCK
    vslt = vsl.reshape(batch, seq, KV_W).transpose(0, 2, 1).reshape(batch * KV_W, seq)
    vwnt = vwn.reshape(batch, nqb, Q_BLOCK, KV_W).transpose(0, 1, 3, 2).reshape(batch * nqb, KV_W, Q_BLOCK)
    vcmpt = vcmp.reshape(batch, N_A, KV_W).transpose(0, 2, 1).reshape(batch * KV_W, N_A)
    gt = gate.T
    key = np.arange(seq)
    et = jnp.asarray((key[:, None] // SLC_BLOCK) == np.arange(N_SEL)[None, :], dtype=BF16)
    ci = np.arange(N_A)
    sj = np.arange(N_SEL)
    overlap = ((ci[None, :] * CMP_STRIDE + CMP_LEN - 1 >= sj[:, None] * SLC_BLOCK)
               & (ci[None, :] * CMP_STRIDE <= sj[:, None] * SLC_BLOCK + SLC_BLOCK - 1)
               & (ci[None, :] < (seq - CMP_LEN) // CMP_STRIDE + 1))
    ot = jnp.asarray(overlap, dtype=BF16)
    onsa = _nsa_call(qpad, ksl, kwn, vslt, vwnt, kcmp, vcmpt, gt, et, ot, batch, seq)

    omem = _memattn_call(mq, mem.reshape(batch * mem.shape[1], d), w_mem_kv[0].astype(BF16), batch, seq)

    x1 = _merge_call(x2, onsa, osgu, omem, w_merge, w_br_nsa[0].astype(BF16), w_br_sgu[0].astype(BF16),
                     w_br_mem[0].astype(BF16), w_o[0].astype(BF16), ln1_g[0][None], ln1_b[0][None])

    n_r = N_GROUPS + N_GROUPS * EXPERTS_PER_GROUP
    wr = jnp.pad(jnp.concatenate([w_router_group[0], w_router_expert[0]], axis=1),
                 ((0, 0), (0, LANES - n_r))).astype(BF16)
    br = jnp.pad(jnp.concatenate([b_router_group[0], b_router_expert[0]]), (0, LANES - n_r))[None]
    out = _moe_call(x1, wr, br, w_exp_gate[0].astype(BF16), w_exp_up[0].astype(BF16),
                    w_exp_down[0].astype(BF16), ln2_g[0][None], ln2_b[0][None])
    return out.reshape(batch, seq, d)
```

```python
import functools

import numpy as np
import jax
import jax.numpy as jnp
from jax import lax
from jax.experimental import pallas as pl
from jax.experimental.pallas import tpu as pltpu

NSA_HEADS = 8
NSA_KV_GROUPS = 2
NSA_HPG = NSA_HEADS // NSA_KV_GROUPS
HEAD_DIM = 64
CMP_LEN = 32
CMP_STRIDE = 16
CMP_HIDDEN = 256
SLC_BLOCK = 64
SLC_TOPN = 8
WINDOW = 512
Q_BLOCK = 128
N_BAND = WINDOW // Q_BLOCK
ROPE_THETA = 500000.0
ROPE_DIM = HEAD_DIM // 4
SGU_CHUNK = 128
SGU_GROUPS = 8
SGU_WIDTH = 512
MEM_HEADS = 4
MEM_HEAD_DIM = 128
MEM_WIDTH = MEM_HEADS * MEM_HEAD_DIM
N_GROUPS = 4
EXPERTS_PER_GROUP = 8
EXPERT_FF = 256
DEPTH = 1
DN_ALPHA = (2.0 * DEPTH) ** 0.25
LN_EPS = 1e-5
NEG = -1e30

LANES = 128
Q_W = NSA_HEADS * HEAD_DIM
KV_W = NSA_KV_GROUPS * HEAD_DIM
GATE_W = NSA_HEADS * 3
VMEM_LIMIT = 56 * 1024 * 1024

BF16 = jnp.bfloat16
F32 = jnp.float32


def _dot(a, b):
    return jnp.dot(a, b, preferred_element_type=F32)


def _dot_nt(a, b):
    return lax.dot_general(a, b, (((1,), (1,)), ((), ())), preferred_element_type=F32)


def _sigmoid(x):
    return 1.0 / (1.0 + jnp.exp(-x))


def _gelu(x):
    return 0.5 * x * (1.0 + lax.erf(x * (2.0 ** -0.5)))


def _layer_norm(x, g, b):
    mu = jnp.mean(x, axis=-1, keepdims=True)
    xc = x - mu
    var = jnp.mean(xc * xc, axis=-1, keepdims=True)
    return xc * lax.rsqrt(var + LN_EPS) * g + b


PROJ_TM = 512
_ROPE_COLS = Q_W + 3 * KV_W
_V_OFF = _ROPE_COLS
_G_OFF = _V_OFF + 3 * KV_W
_SGU_OFF = _G_OFF + LANES
_MQ_OFF = _SGU_OFF + 2 * SGU_WIDTH
_PROJ_COLS = _MQ_OFF + MEM_WIDTH


def _proj_kernel(x_ref, w_ref, rope_ref, lng_ref, lnb_ref, ws_ref, bs_ref,
                 q_ref, kc_ref, ksl_ref, kwn_ref, vc_ref, vsl_ref, vwn_ref, gate_ref, osgu_ref, mq_ref):
    xb = x_ref[...].astype(BF16)
    lane = lax.broadcasted_iota(jnp.int32, (PROJ_TM, LANES), 1)
    low = lane < HEAD_DIM
    cos = rope_ref[:, 0:LANES]
    s_dn = rope_ref[:, LANES:2 * LANES]
    s_up = rope_ref[:, 2 * LANES:3 * LANES]

    h = _dot(xb, w_ref[:, 0:_ROPE_COLS])
    k_refs = (kc_ref, ksl_ref, kwn_ref)
    for j in range(_ROPE_COLS // LANES):
        blk = h[:, j * LANES:(j + 1) * LANES]
        r = (blk * cos + pltpu.roll(blk, LANES - ROPE_DIM // 2, 1) * s_dn
             + pltpu.roll(blk, ROPE_DIM // 2, 1) * s_up)
        if j < Q_W // LANES:
            r = r * (HEAD_DIM ** -0.5)
            sw = pltpu.roll(r, HEAD_DIM, 1)
            g = (2 * j) // NSA_HPG
            if g == 0:
                h0 = jnp.where(low, r, 0.0)
                h1 = jnp.where(low, sw, 0.0)
            else:
                h0 = jnp.where(low, 0.0, sw)
                h1 = jnp.where(low, 0.0, r)
            q_ref[:, (2 * j) * LANES:(2 * j + 1) * LANES] = h0.astype(BF16)
            q_ref[:, (2 * j + 1) * LANES:(2 * j + 2) * LANES] = h1.astype(BF16)
        else:
            k_refs[j - Q_W // LANES][...] = r.astype(BF16)

    hv = _dot(xb, w_ref[:, _V_OFF:_G_OFF])
    vc_ref[...] = hv[:, 0:LANES].astype(BF16)
    vsl_ref[...] = hv[:, LANES:2 * LANES].astype(BF16)
    vwn_ref[...] = hv[:, 2 * LANES:3 * LANES].astype(BF16)

    gate_ref[...] = _sigmoid(_dot(xb, w_ref[:, _G_OFF:_SGU_OFF]))

    mq_ref[...] = _dot(xb, w_ref[:, _MQ_OFF:_PROJ_COLS]).astype(BF16)

    z = _gelu(_dot(xb, w_ref[:, _SGU_OFF:_MQ_OFF]))
    u = z[:, 0:SGU_WIDTH]
    v = _layer_norm(z[:, SGU_WIDTH:2 * SGU_WIDTH], lng_ref[...], lnb_ref[...]).astype(BF16)
    lane_c = lax.broadcasted_iota(jnp.int32, (SGU_CHUNK, LANES), 1)
    low_c = lane_c < (SGU_WIDTH // SGU_GROUPS)
    for ci in range(PROJ_TM // SGU_CHUNK):
        rows = slice(ci * SGU_CHUNK, (ci + 1) * SGU_CHUNK)
        for gp in range(SGU_WIDTH // LANES):
            cols = slice(gp * LANES, (gp + 1) * LANES)
            vblk = v[rows, cols]
            sv = jnp.where(low_c, _dot(ws_ref[2 * gp], vblk), _dot(ws_ref[2 * gp + 1], vblk))
            sv = sv + bs_ref[:, cols]
            osgu_ref[rows, cols] = (u[rows, cols] * sv).astype(BF16)


def _proj_call(x2, wp, rope, lng, lnb, ws, bs, seq):
    t = x2.shape[0]
    d = x2.shape[1]
    nt = t // PROJ_TM
    per_seq = seq // PROJ_TM
    row = lambda i: (i, 0)
    const2 = lambda i: (0, 0)
    out_shapes = (
        jax.ShapeDtypeStruct((t, NSA_HEADS * LANES), BF16),
        *[jax.ShapeDtypeStruct((t, KV_W), BF16) for _ in range(6)],
        jax.ShapeDtypeStruct((t, LANES), F32),
        jax.ShapeDtypeStruct((t, SGU_WIDTH), BF16),
        jax.ShapeDtypeStruct((t, MEM_WIDTH), BF16),
    )
    out_specs = (
        pl.BlockSpec((PROJ_TM, NSA_HEADS * LANES), row),
        *[pl.BlockSpec((PROJ_TM, KV_W), row) for _ in range(6)],
        pl.BlockSpec((PROJ_TM, LANES), row),
        pl.BlockSpec((PROJ_TM, SGU_WIDTH), row),
        pl.BlockSpec((PROJ_TM, MEM_WIDTH), row),
    )
    return pl.pallas_call(
        _proj_kernel,
        grid=(nt,),
        in_specs=[
            pl.BlockSpec((PROJ_TM, d), row),
            pl.BlockSpec((d, _PROJ_COLS), const2),
            pl.BlockSpec((PROJ_TM, 3 * LANES), lambda i: (i % per_seq, 0)),
            pl.BlockSpec((1, SGU_WIDTH), const2),
            pl.BlockSpec((1, SGU_WIDTH), const2),
            pl.BlockSpec((SGU_GROUPS, SGU_CHUNK, SGU_CHUNK), lambda i: (0, 0, 0)),
            pl.BlockSpec((SGU_CHUNK, SGU_WIDTH), const2),
        ],
        out_specs=out_specs,
        out_shape=out_shapes,
        compiler_params=pltpu.CompilerParams(
            dimension_semantics=("parallel",), vmem_limit_bytes=VMEM_LIMIT),
        name="proj",
    )(x2, wp, rope, lng, lnb, ws, bs)


N_A = 128
A_W = CMP_STRIDE * KV_W


def _compress_kernel(ka_ref, va_ref, pek_ref, pev_ref, w1k_ref, w2k_ref, w1v_ref, w2v_ref,
                     kcmp_ref, vcmp_ref):
    def one(a_ref, pe_ref, w1_ref, w2_ref, out_ref):
        a = a_ref[...].astype(F32)
        top = (a + pe_ref[0:1, :]).astype(BF16)
        bot = (a + pe_ref[1:2, :]).astype(BF16)
        h1 = _dot(top, w1_ref[0])
        h2 = _dot(bot, w1_ref[1])
        pre = h1 + pltpu.roll(h2, N_A - 1, 0)
        act = _gelu(pre).astype(BF16)
        out_ref[...] = _dot(act, w2_ref[...]).astype(BF16)

    one(ka_ref, pek_ref, w1k_ref, w2k_ref, kcmp_ref)
    one(va_ref, pev_ref, w1v_ref, w2v_ref, vcmp_ref)


def _compress_call(ka, va, pek, pev, w1k, w2k, w1v, w2v, batch):
    row = lambda b: (b, 0)
    c2 = lambda b: (0, 0)
    c3 = lambda b: (0, 0, 0)
    hid2 = NSA_KV_GROUPS * CMP_HIDDEN
    return pl.pallas_call(
        _compress_kernel,
        grid=(batch,),
        in_specs=[
            pl.BlockSpec((N_A, A_W), row),
            pl.BlockSpec((N_A, A_W), row),
            pl.BlockSpec((8, A_W), c2),
            pl.BlockSpec((8, A_W), c2),
            pl.BlockSpec((2, A_W, hid2), c3),
            pl.BlockSpec((hid2, KV_W), c2),
            pl.BlockSpec((2, A_W, hid2), c3),
            pl.BlockSpec((hid2, KV_W), c2),
        ],
        out_specs=(pl.BlockSpec((N_A, KV_W), row), pl.BlockSpec((N_A, KV_W), row)),
        out_shape=(jax.ShapeDtypeStruct((batch * N_A, KV_W), BF16),
                   jax.ShapeDtypeStruct((batch * N_A, KV_W), BF16)),
        compiler_params=pltpu.CompilerParams(
            dimension_semantics=("parallel",), vmem_limit_bytes=VMEM_LIMIT),
        name="compress",
    )(ka, va, pek, pev, w1k, w2k, w1v, w2v)


N_SEL = 32
SEL_CHUNK = 512
HQ = NSA_HPG * Q_BLOCK


def _nsa_kernel(q_ref, ksl_ref, kwn_ref, vslt_ref, vwnt_ref, kcmp_ref, vcmpt_ref, gt_ref,
                et_ref, ot_ref, o_ref, m_s, l_s, acc_s, out_s):
    c = pl.program_id(1)
    lane_hq = lax.broadcasted_iota(jnp.int32, (1, HQ), 1)
    pos_hq = c * Q_BLOCK + (lane_hq & (Q_BLOCK - 1))
    pos_q = c * Q_BLOCK + lax.broadcasted_iota(jnp.int32, (1, Q_BLOCK), 1)
    n_seq_blocks = kwn_ref.shape[0] // Q_BLOCK

    for g in range(NSA_KV_GROUPS):
        qg = jnp.concatenate(
            [q_ref[:, (g * NSA_HPG + hh) * LANES:(g * NSA_HPG + hh + 1) * LANES]
             for hh in range(NSA_HPG)], axis=0)

        s_c = _dot_nt(kcmp_ref[...], qg)
        n_idx = lax.broadcasted_iota(jnp.int32, (N_A, HQ), 0)
        valid_c = (n_idx * CMP_STRIDE + (CMP_LEN - 1)) <= pos_hq
        sm_c = jnp.where(valid_c, s_c, NEG)
        m_c = jnp.max(sm_c, axis=0, keepdims=True)
        e_c = jnp.where(valid_c, jnp.exp(sm_c - m_c), 0.0)
        d_c = jnp.sum(e_c, axis=0, keepdims=True)
        p_c = e_c / jnp.where(d_c > 0, d_c, 1.0)
        o_cmp = _dot(vcmpt_ref[...], p_c.astype(BF16))

        ps = (p_c[:, 0:Q_BLOCK] + p_c[:, Q_BLOCK:2 * Q_BLOCK]
              + p_c[:, 2 * Q_BLOCK:3 * Q_BLOCK] + p_c[:, 3 * Q_BLOCK:4 * Q_BLOCK])
        p_hi = ps.astype(BF16)
        r1 = ps - p_hi.astype(F32)
        p_lo = r1.astype(BF16)
        p_lo2 = (r1 - p_lo.astype(F32)).astype(BF16)
        ot = ot_ref[...]
        imp = _dot(ot, p_hi) + _dot(ot, p_lo) + _dot(ot, p_lo2)

        j_idx = lax.broadcasted_iota(jnp.int32, (N_SEL, Q_BLOCK), 0)
        cur = pos_q // SLC_BLOCK
        future = j_idx > cur
        forced = (j_idx == 0) | (j_idx == cur) | (j_idx == cur - 1)
        imp = jnp.where(future, NEG, jnp.where(forced, -NEG, imp))
        rank = jnp.zeros((N_SEL, Q_BLOCK), F32)
        for i in range(N_SEL):
            row = imp[i:i + 1, :]
            beats = (row > imp) | ((row == imp) & (j_idx > i))
            rank = rank + jnp.where(beats, 1.0, 0.0)
        sel = jnp.where(rank < float(SLC_TOPN), 1.0, 0.0).astype(BF16)

        m_s[...] = jnp.full((1, HQ), NEG, F32)
        l_s[...] = jnp.zeros((1, HQ), F32)
        acc_s[...] = jnp.zeros((LANES, HQ), F32)
        for kc in range(ksl_ref.shape[0] // SEL_CHUNK):
            @pl.when(kc * (SEL_CHUNK // Q_BLOCK) <= c)
            def _sel_chunk(kc=kc, qg=qg, sel=sel):
                ks = slice(kc * SEL_CHUNK, (kc + 1) * SEL_CHUNK)
                s = _dot_nt(ksl_ref[ks, :], qg)
                picked = _dot(et_ref[ks, :], sel)
                kpos = kc * SEL_CHUNK + lax.broadcasted_iota(jnp.int32, (SEL_CHUNK, Q_BLOCK), 0)
                cap = jnp.where((picked > 0.5) & (kpos <= pos_q), -NEG, NEG)
                sm = jnp.minimum(s, jnp.concatenate([cap] * NSA_HPG, axis=1))
                m_old = m_s[...]
                m_new = jnp.maximum(m_old, jnp.max(sm, axis=0, keepdims=True))
                alpha = jnp.exp(m_old - m_new)
                e = jnp.exp(sm - m_new)
                l_s[...] = alpha * l_s[...] + jnp.sum(e, axis=0, keepdims=True)
                acc_s[...] = alpha * acc_s[...] + _dot(vslt_ref[:, ks], e.astype(BF16))
                m_s[...] = m_new
        o_sel = acc_s[...] / l_s[...]

        parts = []
        blks = []
        for i in range(N_BAND + 1):
            blk = c - N_BAND + i
            blk_c = jnp.maximum(blk, 0)
            blks.append(blk_c)
            k_i = kwn_ref[pl.ds(pl.multiple_of(blk_c * Q_BLOCK, Q_BLOCK), Q_BLOCK), :]
            s_i = _dot_nt(k_i, qg)
            kpos = blk * Q_BLOCK + lax.broadcasted_iota(jnp.int32, (Q_BLOCK, HQ), 0)
            diff = pos_hq - kpos
            valid = (diff >= 0) & (diff < WINDOW) & (kpos >= 0)
            parts.append(jnp.where(valid, s_i, NEG))
        sm_w = jnp.concatenate(parts, axis=0)
        m_w = jnp.max(sm_w, axis=0, keepdims=True)
        e_w = jnp.exp(sm_w - m_w)
        l_w = jnp.sum(e_w, axis=0, keepdims=True)
        e_wb = e_w.astype(BF16)
        o_win = _dot(vwnt_ref[blks[0]], e_wb[0:Q_BLOCK, :])
        for i in range(1, N_BAND + 1):
            o_win = o_win + _dot(vwnt_ref[blks[i]], e_wb[i * Q_BLOCK:(i + 1) * Q_BLOCK, :])
        o_win = o_win / l_w

        def gate_row(br):
            return jnp.concatenate(
                [gt_ref[(g * NSA_HPG + hh) * 3 + br:(g * NSA_HPG + hh) * 3 + br + 1, :]
                 for hh in range(NSA_HPG)], axis=1)
        o_t = gate_row(0) * o_cmp + gate_row(1) * o_sel + gate_row(2) * o_win
        for hh in range(NSA_HPG):
            h = g * NSA_HPG + hh
            out_s[h * HEAD_DIM:(h + 1) * HEAD_DIM, :] = (
                o_t[g * HEAD_DIM:(g + 1) * HEAD_DIM, hh * Q_BLOCK:(hh + 1) * Q_BLOCK])

    o_ref[...] = out_s[...].T.astype(BF16)


def _nsa_call(qpad, ksl, kwn, vslt, vwnt, kcmp, vcmpt, gt, et, ot, batch, seq):
    nqb = seq // Q_BLOCK
    qrow = lambda b, c: (b * nqb + c, 0)
    brow = lambda b, c: (b, 0)
    c2 = lambda b, c: (0, 0)
    return pl.pallas_call(
        _nsa_kernel,
        grid=(batch, nqb),
        in_specs=[
            pl.BlockSpec((Q_BLOCK, NSA_HEADS * LANES), qrow),
            pl.BlockSpec((seq, KV_W), brow),
            pl.BlockSpec((seq, KV_W), brow),
            pl.BlockSpec((KV_W, seq), brow),
            pl.BlockSpec((nqb, KV_W, Q_BLOCK), lambda b, c: (b, 0, 0)),
            pl.BlockSpec((N_A, KV_W), brow),
            pl.BlockSpec((KV_W, N_A), brow),
            pl.BlockSpec((LANES, Q_BLOCK), lambda b, c: (0, b * nqb + c)),
            pl.BlockSpec((seq, N_SEL), c2),
            pl.BlockSpec((N_SEL, N_A), c2),
        ],
        out_specs=pl.BlockSpec((Q_BLOCK, Q_W), qrow),
        out_shape=jax.ShapeDtypeStruct((batch * seq, Q_W), BF16),
        scratch_shapes=[
            pltpu.VMEM((1, HQ), F32),
            pltpu.VMEM((1, HQ), F32),
            pltpu.VMEM((LANES, HQ), F32),
            pltpu.VMEM((Q_W, Q_BLOCK), F32),
        ],
        compiler_params=pltpu.CompilerParams(
            dimension_semantics=("parallel", "arbitrary"), vmem_limit_bytes=VMEM_LIMIT),
        name="nsa",
    )(qpad, ksl, kwn, vslt, vwnt, kcmp, vcmpt, gt, et, ot)


MEM_TM = 512


def _memattn_kernel(mq_ref, mem_ref, wkv_ref, o_ref, kv_s):
    @pl.when(pl.program_id(1) == 0)
    def _():
        kv_s[...] = _dot(mem_ref[...].astype(BF16), wkv_ref[...]).astype(BF16)

    for h in range(MEM_HEADS):
        cols = slice(h * MEM_HEAD_DIM, (h + 1) * MEM_HEAD_DIM)
        s = _dot_nt(mq_ref[:, cols], kv_s[:, cols]) * (MEM_HEAD_DIM ** -0.5)
        m = jnp.max(s, axis=-1, keepdims=True)
        e = jnp.exp(s - m)
        p = e / jnp.sum(e, axis=-1, keepdims=True)
        vcols = slice(MEM_WIDTH + h * MEM_HEAD_DIM, MEM_WIDTH + (h + 1) * MEM_HEAD_DIM)
        o_ref[:, cols] = _dot(p.astype(BF16), kv_s[:, vcols]).astype(BF16)


def _memattn_call(mq, mem2, wkv, batch, seq):
    m_len = mem2.shape[0] // batch
    d = mem2.shape[1]
    per = seq // MEM_TM
    return pl.pallas_call(
        _memattn_kernel,
        grid=(batch, per),
        in_specs=[
            pl.BlockSpec((MEM_TM, MEM_WIDTH), lambda b, i: (b * per + i, 0)),
            pl.BlockSpec((m_len, d), lambda b, i: (b, 0)),
            pl.BlockSpec((d, 2 * MEM_WIDTH), lambda b, i: (0, 0)),
        ],
        out_specs=pl.BlockSpec((MEM_TM, MEM_WIDTH), lambda b, i: (b * per + i, 0)),
        out_shape=jax.ShapeDtypeStruct((batch * seq, MEM_WIDTH), BF16),
        scratch_shapes=[pltpu.VMEM((m_len, 2 * MEM_WIDTH), BF16)],
        compiler_params=pltpu.CompilerParams(
            dimension_semantics=("parallel", "arbitrary"), vmem_limit_bytes=VMEM_LIMIT),
        name="memattn",
    )(mq, mem2, wkv)


MERGE_TM = 512


def _merge_kernel(x_ref, onsa_ref, osgu_ref, omem_ref, wg_ref, wbn_ref, wbs_ref, wbm_ref, wo_ref,
                  g_ref, b_ref, out_ref):
    x = x_ref[...]
    xb = x.astype(BF16)
    d = x.shape[1]
    y = None
    for br, (o_r, w_r) in enumerate(((onsa_ref, wbn_ref), (osgu_ref, wbs_ref), (omem_ref, wbm_ref))):
        gate = _sigmoid(_dot(xb, wg_ref[:, br * d:(br + 1) * d]))
        term = gate * _dot(o_r[...], w_r[...])
        y = term if y is None else y + term
    z = DN_ALPHA * x + _dot(y.astype(BF16), wo_ref[...])
    out_ref[...] = _layer_norm(z, g_ref[...], b_ref[...])


def _merge_call(x2, onsa, osgu, omem, wg, wbn, wbs, wbm, wo, g1, b1):
    t, d = x2.shape
    row = lambda i: (i, 0)
    c2 = lambda i: (0, 0)
    full = lambda a: pl.BlockSpec(a.shape, c2)
    return pl.pallas_call(
        _merge_kernel,
        grid=(t // MERGE_TM,),
        in_specs=[
            pl.BlockSpec((MERGE_TM, d), row),
            pl.BlockSpec((MERGE_TM, onsa.shape[1]), row),
            pl.BlockSpec((MERGE_TM, osgu.shape[1]), row),
            pl.BlockSpec((MERGE_TM, omem.shape[1]), row),
            full(wg), full(wbn), full(wbs), full(wbm), full(wo), full(g1), full(b1),
        ],
        out_specs=pl.BlockSpec((MERGE_TM, d), row),
        out_shape=jax.ShapeDtypeStruct((t, d), F32),
        compiler_params=pltpu.CompilerParams(
            dimension_semantics=("parallel",), vmem_limit_bytes=VMEM_LIMIT),
        name="merge",
    )(x2, onsa, osgu, omem, wg, wbn, wbs, wbm, wo, g1, b1)


MOE_TM = 1024
MOE_CH = 128
_E_OFF = N_GROUPS
_XA_W = 1024 + LANES


def _moe_kernel(x_ref, tri_ref, wr_ref, br_ref, wg_ref, wu_ref, wd_ref, g_ref, b_ref, out_ref,
                xa_s, keyc_s, keyr_s, acc_s, cnt_s):
    grp = pl.program_id(1)
    tm = x_ref.shape[0]
    d = x_ref.shape[1]

    @pl.when(grp == 0)
    def _route():
        xb = x_ref[...].astype(BF16)
        xa_s[:, 0:d] = xb
        logits = _dot(xb, wr_ref[...]) + br_ref[...]
        lane = lax.broadcasted_iota(jnp.int32, logits.shape, 1).astype(F32)
        is_g = lane < float(N_GROUPS)
        gl = jnp.where(is_g, logits, NEG)
        gmax = jnp.max(gl, axis=-1, keepdims=True)
        gidx = jnp.min(jnp.where(is_g & (gl == gmax), lane, float(LANES)), axis=-1, keepdims=True)
        gsum = jnp.sum(jnp.where(is_g, jnp.exp(gl - gmax), 0.0), axis=-1, keepdims=True)
        gprob = 1.0 / gsum
        lo = float(_E_OFF) + gidx * float(EXPERTS_PER_GROUP)
        in_e = (lane >= lo) & (lane < lo + float(EXPERTS_PER_GROUP))
        el = jnp.where(in_e, logits, NEG)
        emax = jnp.max(el, axis=-1, keepdims=True)
        ee = jnp.where(in_e, jnp.exp(el - emax), 0.0)
        ep = ee / jnp.sum(ee, axis=-1, keepdims=True)
        t1 = jnp.max(jnp.where(in_e, ep, -1.0), axis=-1, keepdims=True)
        i1 = jnp.min(jnp.where(in_e & (ep == t1), lane, float(LANES)), axis=-1, keepdims=True)
        rest = in_e & (lane != i1)
        t2 = jnp.max(jnp.where(rest, ep, -1.0), axis=-1, keepdims=True)
        i2 = jnp.min(jnp.where(rest & (ep == t2), lane, float(LANES)), axis=-1, keepdims=True)
        den = t1 + t2
        ew = jnp.where(lane == i1, t1 / den, 0.0) + jnp.where(lane == i2, t2 / den, 0.0)
        cw = ew * gprob

        ilane = lax.broadcasted_iota(jnp.int32, logits.shape, 1)
        cwf = pltpu.roll(cw, LANES - _E_OFF, 1)
        for k in range(1, N_GROUPS):
            cwf = cwf + pltpu.roll(cw, LANES - _E_OFF - k * EXPERTS_PER_GROUP, 1)
        cwf = jnp.where(ilane < EXPERTS_PER_GROUP, cwf, 0.0)
        hi = cwf.astype(BF16).astype(F32)
        r1 = cwf - hi
        lo = r1.astype(BF16).astype(F32)
        lo2 = (r1 - lo).astype(BF16).astype(F32)
        pieces = jnp.where(ilane < EXPERTS_PER_GROUP, hi,
                           jnp.where(ilane < 2 * EXPERTS_PER_GROUP,
                                     pltpu.roll(lo, EXPERTS_PER_GROUP, 1),
                                     pltpu.roll(lo2, 2 * EXPERTS_PER_GROUP, 1)))
        xa_s[:, d:d + LANES] = pieces.astype(BF16)

        onehot = jnp.where(lane == gidx, 1.0, 0.0)
        tri = tri_ref[...]
        rank_c = _dot(tri, onehot.astype(BF16))
        keyc_s[...] = jnp.where(onehot > 0.5, rank_c, -1.0)
        onehot_t = onehot.T
        rank_r = _dot_nt(onehot_t.astype(BF16), tri)
        keyr_s[...] = jnp.where(onehot_t > 0.5, rank_r, -1.0)[0:8, :]
        counts = jnp.sum(onehot, axis=0, keepdims=True)
        lane_r = lax.broadcasted_iota(jnp.int32, counts.shape, 1)
        for k in range(N_GROUPS):
            cnt_s[k] = jnp.sum(jnp.where(lane_r == k, counts, 0.0)).astype(jnp.int32)
        acc_s[...] = jnp.zeros(acc_s.shape, F32)

    n_rows = cnt_s[grp]
    keyr = keyr_s[pl.ds(grp, 1), :]
    lane_c = lax.broadcasted_iota(jnp.int32, keyc_s.shape, 1)
    keyc = jnp.sum(jnp.where(lane_c == grp, keyc_s[...], 0.0), axis=-1, keepdims=True)

    def sweep(k, carry):
        base = (k * MOE_CH).astype(F32)
        r_row = lax.broadcasted_iota(jnp.int32, (MOE_CH, tm), 0).astype(F32) + base
        pick = jnp.where(keyr == r_row, 1.0, 0.0).astype(BF16)
        ga = _dot(pick, xa_s[...])
        xg = ga[:, 0:d].astype(BF16)
        cwg = ga[:, d:d + LANES]
        cs = (cwg + pltpu.roll(cwg, LANES - EXPERTS_PER_GROUP, 1)
              + pltpu.roll(cwg, LANES - 2 * EXPERTS_PER_GROUP, 1))
        y = None
        for e in range(EXPERTS_PER_GROUP):
            hg = _dot(xg, wg_ref[0, e])
            hu = _dot(xg, wu_ref[0, e])
            hid = ((hg * _sigmoid(hg)) * hu) * cs[:, e:e + 1]
            term = _dot(hid.astype(BF16), wd_ref[0, e])
            y = term if y is None else y + term
        r_col = lax.broadcasted_iota(jnp.int32, (tm, MOE_CH), 1).astype(F32) + base
        put = jnp.where(keyc == r_col, 1.0, 0.0).astype(BF16)
        acc_s[...] += _dot(put, y.astype(BF16))
        return carry

    lax.fori_loop(0, (n_rows + MOE_CH - 1) // MOE_CH, sweep, 0)

    @pl.when(grp == N_GROUPS - 1)
    def _fin():
        z = DN_ALPHA * x_ref[...] + acc_s[...]
        out_ref[...] = _layer_norm(z, g_ref[...], b_ref[...])


def _moe_call(x1, wr, br, wg, wu, wd, g2, b2):
    t, d = x1.shape
    assert d + LANES == _XA_W
    row = lambda i, g: (i, 0)
    c2 = lambda i, g: (0, 0)
    idx = np.arange(MOE_TM)
    tri = jnp.asarray(idx[None, :] < idx[:, None], dtype=BF16)
    return pl.pallas_call(
        _moe_kernel,
        grid=(t // MOE_TM, N_GROUPS),
        in_specs=[
            pl.BlockSpec((MOE_TM, d), row, pipeline_mode=pl.Buffered(1)),
            pl.BlockSpec((MOE_TM, MOE_TM), c2, pipeline_mode=pl.Buffered(1)),
            pl.BlockSpec(wr.shape, c2),
            pl.BlockSpec(br.shape, c2),
            pl.BlockSpec((1, EXPERTS_PER_GROUP, d, EXPERT_FF), lambda i, g: (g, 0, 0, 0)),
            pl.BlockSpec((1, EXPERTS_PER_GROUP, d, EXPERT_FF), lambda i, g: (g, 0, 0, 0)),
            pl.BlockSpec((1, EXPERTS_PER_GROUP, EXPERT_FF, d), lambda i, g: (g, 0, 0, 0)),
            pl.BlockSpec(g2.shape, c2),
            pl.BlockSpec(b2.shape, c2),
        ],
        out_specs=pl.BlockSpec((MOE_TM, d), row),
        out_shape=jax.ShapeDtypeStruct((t, d), F32),
        scratch_shapes=[
            pltpu.VMEM((MOE_TM, _XA_W), BF16),
            pltpu.VMEM((MOE_TM, LANES), F32),
            pltpu.VMEM((8, MOE_TM), F32),
            pltpu.VMEM((MOE_TM, d), F32),
            pltpu.SMEM((N_GROUPS,), jnp.int32),
        ],
        compiler_params=pltpu.CompilerParams(
            dimension_semantics=("parallel", "arbitrary"), vmem_limit_bytes=VMEM_LIMIT),
        name="moe",
    )(x1, tri, wr, br, wg, wu, wd, g2, b2)


def _rope_table(seq):
    half = ROPE_DIM // 2
    inv = ROPE_THETA ** (-jnp.arange(0, ROPE_DIM, 2, dtype=F32) / ROPE_DIM)
    ang = jnp.arange(seq, dtype=F32)[:, None] * inv[None, :]
    cos, sin = jnp.cos(ang), jnp.sin(ang)
    rest = HEAD_DIM - ROPE_DIM
    one = jnp.ones((seq, rest), F32)
    zero = jnp.zeros((seq, rest), F32)
    zh = jnp.zeros((seq, half), F32)
    c_h = jnp.concatenate([cos, cos, one], axis=1)
    dn_h = jnp.concatenate([-sin, zh, zero], axis=1)
    up_h = jnp.concatenate([zh, sin, zero], axis=1)
    rep = LANES // HEAD_DIM
    return jnp.concatenate([jnp.tile(c_h, (1, rep)), jnp.tile(dn_h, (1, rep)), jnp.tile(up_h, (1, rep))],
                           axis=1)


def _expand_cmp_weights(pe, w1, w2):
    g_n = NSA_KV_GROUPS
    w1r = w1.reshape(2, CMP_STRIDE, HEAD_DIM, CMP_HIDDEN)
    eye = jnp.eye(g_n, dtype=w1.dtype)
    w1e = jnp.einsum('pldh,ge->plgdeh', w1r, eye).reshape(2, A_W, g_n * CMP_HIDDEN)
    w2e = jnp.einsum('hd,ge->ghed', w2, eye).reshape(g_n * CMP_HIDDEN, g_n * HEAD_DIM)
    per = pe.reshape(2, CMP_STRIDE, 1, HEAD_DIM)
    pee = jnp.broadcast_to(per, (2, CMP_STRIDE, g_n, HEAD_DIM)).reshape(2, A_W)
    pee = jnp.concatenate([pee, jnp.zeros((6, A_W), pe.dtype)], axis=0)
    return pee.astype(F32), w1e.astype(BF16), w2e.astype(BF16)


def kernel(x, mem, w_in, cmp_pe_k, cmp_w1_k, cmp_w2_k, cmp_pe_v, cmp_w1_v, cmp_w2_v, sgu_ln_g, sgu_ln_b,
           sgu_w_s, sgu_b_s, w_mem_kv, w_br_nsa, w_br_sgu, w_br_mem, w_o, ln1_g, ln1_b, w_router_group,
           b_router_group, w_router_expert, b_router_expert, w_exp_gate, w_exp_up, w_exp_down, ln2_g, ln2_b):
    batch, seq, d = x.shape
    t = batch * seq
    assert w_in.shape[0] == DEPTH == 1
    assert seq % PROJ_TM == 0 and seq // CMP_STRIDE == N_A and seq // SLC_BLOCK == N_SEL

    w = w_in[0]
    offs = np.cumsum([0, Q_W, KV_W, KV_W, KV_W, KV_W, KV_W, KV_W, GATE_W, 2 * SGU_WIDTH, MEM_WIDTH, 3 * d])
    seg = lambda i: w[:, int(offs[i]):int(offs[i + 1])]
    gate_w = jnp.pad(seg(7), ((0, 0), (0, LANES - GATE_W)))
    wp = jnp.concatenate([seg(0), seg(1), seg(3), seg(5), seg(2), seg(4), seg(6), gate_w, seg(8), seg(9)],
                         axis=1).astype(BF16)
    w_merge = seg(10).astype(BF16)
    rope = _rope_table(seq)
    tril = jnp.tril(jnp.ones((SGU_CHUNK, SGU_CHUNK), dtype=bool))
    ws = jnp.where(tril[None], sgu_w_s[0], 0.0).astype(BF16)
    bs = jnp.repeat(sgu_b_s[0].T, SGU_WIDTH // SGU_GROUPS, axis=1)

    x2 = x.reshape(t, d)
    (qpad, kc, ksl, kwn, vc, vsl, vwn, gate, osgu, mq) = _proj_call(
        x2, wp, rope, sgu_ln_g[0][None], sgu_ln_b[0][None], ws, bs, seq)

    pek, w1k, w2k = _expand_cmp_weights(cmp_pe_k[0], cmp_w1_k[0], cmp_w2_k[0])
    pev, w1v, w2v = _expand_cmp_weights(cmp_pe_v[0], cmp_w1_v[0], cmp_w2_v[0])
    kcmp, vcmp = _compress_call(kc.reshape(batch * N_A, A_W), vc.reshape(batch * N_A, A_W),
                                pek, pev, w1k, w2k, w1v, w2v, batch)

    nqb = seq // Q_BLOCK
    vslt = vsl.reshape(batch, seq, KV_W).transpose(0, 2, 1).reshape(batch * KV_W, seq)
    vwnt = vwn.reshape(batch, nqb, Q_BLOCK, KV_W).transpose(0, 1, 3, 2).reshape(batch * nqb, KV_W, Q_BLOCK)
    vcmpt = vcmp.reshape(batch, N_A, KV_W).transpose(0, 2, 1).reshape(batch * KV_W, N_A)
    gt = gate.T
    key = np.arange(seq)
    et = jnp.asarray((key[:, None] // SLC_BLOCK) == np.arange(N_SEL)[None, :], dtype=BF16)
    ci = np.arange(N_A)
    sj = np.arange(N_SEL)
    overlap = ((ci[None, :] * CMP_STRIDE + CMP_LEN - 1 >= sj[:, None] * SLC_BLOCK)
               & (ci[None, :] * CMP_STRIDE <= sj[:, None] * SLC_BLOCK + SLC_BLOCK - 1)
               & (ci[None, :] < (seq - CMP_LEN) // CMP_STRIDE + 1))
    ot = jnp.asarray(overlap, dtype=BF16)
    onsa = _nsa_call(qpad, ksl, kwn, vslt, vwnt, kcmp, vcmpt, gt, et, ot, batch, seq)

    omem = _memattn_call(mq, mem.reshape(batch * mem.shape[1], d), w_mem_kv[0].astype(BF16), batch, seq)

    x1 = _merge_call(x2, onsa, osgu, omem, w_merge, w_br_nsa[0].astype(BF16), w_br_sgu[0].astype(BF16),
                     w_br_mem[0].astype(BF16), w_o[0].astype(BF16), ln1_g[0][None], ln1_b[0][None])

    n_r = N_GROUPS + N_GROUPS * EXPERTS_PER_GROUP
    wr = jnp.pad(jnp.concatenate([w_router_group[0], w_router_expert[0]], axis=1),
                 ((0, 0), (0, LANES - n_r))).astype(BF16)
    br = jnp.pad(jnp.concatenate([b_router_group[0], b_router_expert[0]]), (0, LANES - n_r))[None]
    out = _moe_call(x1, wr, br, w_exp_gate[0].astype(BF16), w_exp_up[0].astype(BF16),
                    w_exp_down[0].astype(BF16), ln2_g[0][None], ln2_b[0][None])
    return out.reshape(batch, seq, d)
```

```python
import functools

import numpy as np
import jax
import jax.numpy as jnp
from jax import lax
from jax.experimental import pallas as pl
from jax.experimental.pallas import tpu as pltpu

NSA_HEADS = 8
NSA_KV_GROUPS = 2
NSA_HPG = NSA_HEADS // NSA_KV_GROUPS
HEAD_DIM = 64
CMP_LEN = 32
CMP_STRIDE = 16
CMP_HIDDEN = 256
SLC_BLOCK = 64
SLC_TOPN = 8
WINDOW = 512
Q_BLOCK = 128
N_BAND = WINDOW // Q_BLOCK
ROPE_THETA = 500000.0
ROPE_DIM = HEAD_DIM // 4
SGU_CHUNK = 128
SGU_GROUPS = 8
SGU_WIDTH = 512
MEM_HEADS = 4
MEM_HEAD_DIM = 128
MEM_WIDTH = MEM_HEADS * MEM_HEAD_DIM
N_GROUPS = 4
EXPERTS_PER_GROUP = 8
EXPERT_FF = 256
DEPTH = 1
DN_ALPHA = (2.0 * DEPTH) ** 0.25
LN_EPS = 1e-5
NEG = -1e30
LOG2E = 1.4426950408889634

LANES = 128
Q_W = NSA_HEADS * HEAD_DIM
KV_W = NSA_KV_GROUPS * HEAD_DIM
GATE_W = NSA_HEADS * 3
VMEM_LIMIT = 56 * 1024 * 1024

BF16 = jnp.bfloat16
F32 = jnp.float32


def _dot(a, b):
    return jnp.dot(a, b, preferred_element_type=F32)


def _dot_nt(a, b):
    return lax.dot_general(a, b, (((1,), (1,)), ((), ())), preferred_element_type=F32)


def _sigmoid(x):
    return 1.0 / (1.0 + jnp.exp(-x))


def _gelu(x):
    return 0.5 * x * (1.0 + lax.erf(x * (2.0 ** -0.5)))


def _layer_norm(x, g, b):
    mu = jnp.mean(x, axis=-1, keepdims=True)
    xc = x - mu
    var = jnp.mean(xc * xc, axis=-1, keepdims=True)
    return xc * lax.rsqrt(var + LN_EPS) * g + b


PROJ_TM = 512
_ROPE_COLS = Q_W + 3 * KV_W
_V_OFF = _ROPE_COLS
_G_OFF = _V_OFF + 3 * KV_W
_SGU_OFF = _G_OFF + LANES
_MQ_OFF = _SGU_OFF + 2 * SGU_WIDTH
_PROJ_COLS = _MQ_OFF + MEM_WIDTH


def _proj_kernel(x_ref, w_ref, rope_ref, lng_ref, lnb_ref, ws_ref, bs_ref,
                 q_ref, kc_ref, ksl_ref, kwn_ref, vc_ref, vsl_ref, vwn_ref, gate_ref, osgu_ref, mq_ref):
    xb = x_ref[...].astype(BF16)
    lane = lax.broadcasted_iota(jnp.int32, (PROJ_TM, LANES), 1)
    low = lane < HEAD_DIM
    cos = rope_ref[:, 0:LANES]
    s_dn = rope_ref[:, LANES:2 * LANES]
    s_up = rope_ref[:, 2 * LANES:3 * LANES]

    h = _dot(xb, w_ref[:, 0:_ROPE_COLS])
    k_refs = (kc_ref, ksl_ref, kwn_ref)
    for j in range(_ROPE_COLS // LANES):
        blk = h[:, j * LANES:(j + 1) * LANES]
        r = (blk * cos + pltpu.roll(blk, LANES - ROPE_DIM // 2, 1) * s_dn
             + pltpu.roll(blk, ROPE_DIM // 2, 1) * s_up)
        if j < Q_W // LANES:
            r = r * (HEAD_DIM ** -0.5 * LOG2E)
            sw = pltpu.roll(r, HEAD_DIM, 1)
            g = (2 * j) // NSA_HPG
            if g == 0:
                h0 = jnp.where(low, r, 0.0)
                h1 = jnp.where(low, sw, 0.0)
            else:
                h0 = jnp.where(low, 0.0, sw)
                h1 = jnp.where(low, 0.0, r)
            q_ref[:, (2 * j) * LANES:(2 * j + 1) * LANES] = h0.astype(BF16)
            q_ref[:, (2 * j + 1) * LANES:(2 * j + 2) * LANES] = h1.astype(BF16)
        else:
            k_refs[j - Q_W // LANES][...] = r.astype(BF16)

    hv = _dot(xb, w_ref[:, _V_OFF:_G_OFF])
    vc_ref[...] = hv[:, 0:LANES].astype(BF16)
    vsl_ref[...] = hv[:, LANES:2 * LANES].astype(BF16)
    vwn_ref[...] = hv[:, 2 * LANES:3 * LANES].astype(BF16)

    gate_ref[...] = _sigmoid(_dot(xb, w_ref[:, _G_OFF:_SGU_OFF]))

    mq_ref[...] = _dot(xb, w_ref[:, _MQ_OFF:_PROJ_COLS]).astype(BF16)

    z = _gelu(_dot(xb, w_ref[:, _SGU_OFF:_MQ_OFF]))
    u = z[:, 0:SGU_WIDTH]
    v = _layer_norm(z[:, SGU_WIDTH:2 * SGU_WIDTH], lng_ref[...], lnb_ref[...]).astype(BF16)
    lane_c = lax.broadcasted_iota(jnp.int32, (SGU_CHUNK, LANES), 1)
    low_c = lane_c < (SGU_WIDTH // SGU_GROUPS)
    for ci in range(PROJ_TM // SGU_CHUNK):
        rows = slice(ci * SGU_CHUNK, (ci + 1) * SGU_CHUNK)
        for gp in range(SGU_WIDTH // LANES):
            cols = slice(gp * LANES, (gp + 1) * LANES)
            vblk = v[rows, cols]
            sv = jnp.where(low_c, _dot(ws_ref[2 * gp], vblk), _dot(ws_ref[2 * gp + 1], vblk))
            sv = sv + bs_ref[:, cols]
            osgu_ref[rows, cols] = (u[rows, cols] * sv).astype(BF16)


def _proj_call(x2, wp, rope, lng, lnb, ws, bs, seq):
    t = x2.shape[0]
    d = x2.shape[1]
    nt = t // PROJ_TM
    per_seq = seq // PROJ_TM
    row = lambda i: (i, 0)
    const2 = lambda i: (0, 0)
    out_shapes = (
        jax.ShapeDtypeStruct((t, NSA_HEADS * LANES), BF16),
        *[jax.ShapeDtypeStruct((t, KV_W), BF16) for _ in range(6)],
        jax.ShapeDtypeStruct((t, LANES), F32),
        jax.ShapeDtypeStruct((t, SGU_WIDTH), BF16),
        jax.ShapeDtypeStruct((t, MEM_WIDTH), BF16),
    )
    out_specs = (
        pl.BlockSpec((PROJ_TM, NSA_HEADS * LANES), row),
        *[pl.BlockSpec((PROJ_TM, KV_W), row) for _ in range(6)],
        pl.BlockSpec((PROJ_TM, LANES), row),
        pl.BlockSpec((PROJ_TM, SGU_WIDTH), row),
        pl.BlockSpec((PROJ_TM, MEM_WIDTH), row),
    )
    return pl.pallas_call(
        _proj_kernel,
        grid=(nt,),
        in_specs=[
            pl.BlockSpec((PROJ_TM, d), row),
            pl.BlockSpec((d, _PROJ_COLS), const2),
            pl.BlockSpec((PROJ_TM, 3 * LANES), lambda i: (i % per_seq, 0)),
            pl.BlockSpec((1, SGU_WIDTH), const2),
            pl.BlockSpec((1, SGU_WIDTH), const2),
            pl.BlockSpec((SGU_GROUPS, SGU_CHUNK, SGU_CHUNK), lambda i: (0, 0, 0)),
            pl.BlockSpec((SGU_CHUNK, SGU_WIDTH), const2),
        ],
        out_specs=out_specs,
        out_shape=out_shapes,
        compiler_params=pltpu.CompilerParams(
            dimension_semantics=("parallel",), vmem_limit_bytes=VMEM_LIMIT),
        name="proj",
    )(x2, wp, rope, lng, lnb, ws, bs)


N_A = 128
A_W = CMP_STRIDE * KV_W


def _compress_kernel(ka_ref, va_ref, pek_ref, pev_ref, w1k_ref, w2k_ref, w1v_ref, w2v_ref,
                     kcmp_ref, vcmp_ref):
    def one(a_ref, pe_ref, w1_ref, w2_ref, out_ref):
        a = a_ref[...].astype(F32)
        top = (a + pe_ref[0:1, :]).astype(BF16)
        bot = (a + pe_ref[1:2, :]).astype(BF16)
        h1 = _dot(top, w1_ref[0])
        h2 = _dot(bot, w1_ref[1])
        pre = h1 + pltpu.roll(h2, N_A - 1, 0)
        act = _gelu(pre).astype(BF16)
        out_ref[...] = _dot(act, w2_ref[...]).astype(BF16)

    one(ka_ref, pek_ref, w1k_ref, w2k_ref, kcmp_ref)
    one(va_ref, pev_ref, w1v_ref, w2v_ref, vcmp_ref)


def _compress_call(ka, va, pek, pev, w1k, w2k, w1v, w2v, batch):
    row = lambda b: (b, 0)
    c2 = lambda b: (0, 0)
    c3 = lambda b: (0, 0, 0)
    hid2 = NSA_KV_GROUPS * CMP_HIDDEN
    return pl.pallas_call(
        _compress_kernel,
        grid=(batch,),
        in_specs=[
            pl.BlockSpec((N_A, A_W), row),
            pl.BlockSpec((N_A, A_W), row),
            pl.BlockSpec((8, A_W), c2),
            pl.BlockSpec((8, A_W), c2),
            pl.BlockSpec((2, A_W, hid2), c3),
            pl.BlockSpec((hid2, KV_W), c2),
            pl.BlockSpec((2, A_W, hid2), c3),
            pl.BlockSpec((hid2, KV_W), c2),
        ],
        out_specs=(pl.BlockSpec((N_A, KV_W), row), pl.BlockSpec((N_A, KV_W), row)),
        out_shape=(jax.ShapeDtypeStruct((batch * N_A, KV_W), BF16),
                   jax.ShapeDtypeStruct((batch * N_A, KV_W), BF16)),
        compiler_params=pltpu.CompilerParams(
            dimension_semantics=("parallel",), vmem_limit_bytes=VMEM_LIMIT),
        name="compress",
    )(ka, va, pek, pev, w1k, w2k, w1v, w2v)


N_SEL = 32
SEL_CHUNK = 512
HQ = NSA_HPG * Q_BLOCK


def _nsa_kernel(q_ref, ksl_ref, kwn_ref, vslt_ref, vwnt_ref, kcmp_ref, vcmpt_ref, gt_ref,
                ot_ref, o_ref, m_s, acc_s, ocmp_s, owin_s, cap_s, out_s):
    c = pl.program_id(1)
    lane_hq = lax.broadcasted_iota(jnp.int32, (1, HQ), 1)
    pos_hq = c * Q_BLOCK + (lane_hq & (Q_BLOCK - 1))
    pos_q = c * Q_BLOCK + lax.broadcasted_iota(jnp.int32, (1, Q_BLOCK), 1)
    groups = range(NSA_KV_GROUPS)
    qgs = [jnp.concatenate(
        [q_ref[:, (g * NSA_HPG + hh) * LANES:(g * NSA_HPG + hh + 1) * LANES]
         for hh in range(NSA_HPG)], axis=0) for g in groups]

    _nsa_window(c, qgs, kwn_ref, vwnt_ref, owin_s)

    kcmp = kcmp_ref[...]
    s_c = [_dot_nt(kcmp, qgs[g]) for g in groups]
    n_idx = lax.broadcasted_iota(jnp.int32, (N_A, HQ), 0)
    valid_c = (n_idx * CMP_STRIDE + (CMP_LEN - 1)) <= pos_hq
    p_c = []
    for g in groups:
        sm_c = jnp.where(valid_c, s_c[g], NEG)
        m_c = jnp.max(sm_c, axis=0, keepdims=True)
        e_c = jnp.where(valid_c, jnp.exp2(sm_c - m_c), 0.0)
        d_c = jnp.sum(e_c, axis=0, keepdims=True)
        p_c.append(e_c / jnp.where(d_c > 0, d_c, 1.0))
    for g in groups:
        ocmp_s[g] = _dot(vcmpt_ref[g * HEAD_DIM:(g + 1) * HEAD_DIM, :], p_c[g].astype(BF16))

    ot = ot_ref[...]
    imp = []
    for g in groups:
        ps = (p_c[g][:, 0:Q_BLOCK] + p_c[g][:, Q_BLOCK:2 * Q_BLOCK]
              + p_c[g][:, 2 * Q_BLOCK:3 * Q_BLOCK] + p_c[g][:, 3 * Q_BLOCK:4 * Q_BLOCK])
        p_hi = ps.astype(BF16)
        r1 = ps - p_hi.astype(F32)
        p_lo = r1.astype(BF16)
        p_lo2 = (r1 - p_lo.astype(F32)).astype(BF16)
        imp.append(_dot(ot, p_hi) + _dot(ot, p_lo) + _dot(ot, p_lo2))

    j_idx = lax.broadcasted_iota(jnp.int32, (N_SEL, Q_BLOCK), 0)
    cur = pos_q // SLC_BLOCK
    future = j_idx > cur
    forced = (j_idx == 0) | (j_idx == cur) | (j_idx == cur - 1)
    imp = [jnp.where(future, NEG, jnp.where(forced, -NEG, imp[g])) for g in groups]
    rank = [jnp.zeros((N_SEL, Q_BLOCK), F32) for g in groups]
    for i in range(N_SEL):
        for g in groups:
            row = imp[g][i:i + 1, :]
            beats = (row > imp[g]) | ((row == imp[g]) & (j_idx > i))
            rank[g] = rank[g] + jnp.where(beats, 1.0, 0.0)
    for g in groups:
        cap_s[g] = jnp.where(rank[g] < float(SLC_TOPN), -NEG, NEG)
        m_s[g] = jnp.full((1, HQ), NEG, F32)
        acc_s[g] = jnp.zeros((LANES, HQ), F32)

    blocks_per_chunk = SEL_CHUNK // SLC_BLOCK

    def sel_chunk(kc):
        ks = slice(kc * SEL_CHUNK, (kc + 1) * SEL_CHUNK)
        kpos = kc * SEL_CHUNK + lax.broadcasted_iota(jnp.int32, (SEL_CHUNK, Q_BLOCK), 0)
        causal = kpos <= pos_q
        k_chunk = ksl_ref[ks, :]
        s = [_dot_nt(k_chunk, qgs[g]) for g in groups]
        for g in groups:
            cap = jnp.concatenate(
                [jnp.broadcast_to(cap_s[g, j:j + 1, :], (SLC_BLOCK, Q_BLOCK))
                 for j in range(kc * blocks_per_chunk, (kc + 1) * blocks_per_chunk)], axis=0)
            cap = jnp.where(causal, cap, NEG)
            sm = jnp.minimum(s[g], jnp.concatenate([cap] * NSA_HPG, axis=1))
            m_old = m_s[g]
            m_new = jnp.maximum(m_old, jnp.max(sm, axis=0, keepdims=True))
            alpha = jnp.exp2(m_old - m_new)
            e = jnp.exp2(sm - m_new)
            acc_s[g] = alpha * acc_s[g] + _dot(vslt_ref[g, :, ks], e.astype(BF16))
            m_s[g] = m_new

    sel_chunk(0)
    for kc in range(1, ksl_ref.shape[0] // SEL_CHUNK):
        pl.when(kc * (SEL_CHUNK // Q_BLOCK) <= c)(functools.partial(sel_chunk, kc))

    for g in groups:
        acc = acc_s[g]
        o_sel = acc[0:HEAD_DIM, :] * (1.0 / acc[HEAD_DIM:HEAD_DIM + 1, :])

        def gate_row(br):
            return jnp.concatenate(
                [gt_ref[(g * NSA_HPG + hh) * 3 + br:(g * NSA_HPG + hh) * 3 + br + 1, :]
                 for hh in range(NSA_HPG)], axis=1)
        o_t = gate_row(0) * ocmp_s[g] + gate_row(1) * o_sel + gate_row(2) * owin_s[g]
        for hh in range(NSA_HPG):
            h = g * NSA_HPG + hh
            out_s[h * HEAD_DIM:(h + 1) * HEAD_DIM, :] = o_t[:, hh * Q_BLOCK:(hh + 1) * Q_BLOCK]

    o_ref[...] = out_s[...].T.astype(BF16)


def _nsa_window(c, qgs, kwn_ref, vwnt_ref, owin_s):
    groups = range(NSA_KV_GROUPS)
    q_i = lax.broadcasted_iota(jnp.int32, (Q_BLOCK, Q_BLOCK), 1)
    k_i = lax.broadcasted_iota(jnp.int32, (Q_BLOCK, Q_BLOCK), 0)
    blks, caps = [], []
    for i in range(N_BAND + 1):
        blk = c - N_BAND + i
        blks.append(jnp.maximum(blk, 0))
        off = (N_BAND - i) * Q_BLOCK
        if off - (Q_BLOCK - 1) >= 0 and off + (Q_BLOCK - 1) < WINDOW:
            caps.append(jnp.where(blk >= 0, -NEG, NEG))
        else:
            diff = off + q_i - k_i
            ok = (diff >= 0) & (diff < WINDOW) & (blk >= 0)
            caps.append(jnp.concatenate([jnp.where(ok, -NEG, NEG)] * NSA_HPG, axis=1))
    k_blocks = [kwn_ref[pl.ds(pl.multiple_of(blks[i] * Q_BLOCK, Q_BLOCK), Q_BLOCK), :]
                for i in range(N_BAND + 1)]
    s_w = [[_dot_nt(k_blocks[i], qgs[g]) for i in range(N_BAND + 1)] for g in groups]
    for g in groups:
        sm_w = jnp.concatenate([jnp.minimum(s_w[g][i], caps[i]) for i in range(N_BAND + 1)], axis=0)
        m_w = jnp.max(sm_w, axis=0, keepdims=True)
        e_wb = jnp.exp2(sm_w - m_w).astype(BF16)
        o_win = _dot(vwnt_ref[g, blks[0]], e_wb[0:Q_BLOCK, :])
        for i in range(1, N_BAND + 1):
            o_win = o_win + _dot(vwnt_ref[g, blks[i]], e_wb[i * Q_BLOCK:(i + 1) * Q_BLOCK, :])
        owin_s[g] = o_win[0:HEAD_DIM, :] * (1.0 / o_win[HEAD_DIM:HEAD_DIM + 1, :])


def _nsa_call(qpad, ksl, kwn, vslt, vwnt, kcmp, vcmpt, gt, ot, batch, seq):
    nqb = seq // Q_BLOCK
    n_g = NSA_KV_GROUPS
    qrow = lambda b, c: (b * nqb + c, 0)
    brow = lambda b, c: (b, 0)
    c2 = lambda b, c: (0, 0)
    return pl.pallas_call(
        _nsa_kernel,
        grid=(batch, nqb),
        in_specs=[
            pl.BlockSpec((Q_BLOCK, NSA_HEADS * LANES), qrow),
            pl.BlockSpec((seq, KV_W), brow),
            pl.BlockSpec((seq, KV_W), brow),
            pl.BlockSpec((n_g, LANES, seq), lambda b, c: (b, 0, 0)),
            pl.BlockSpec((n_g, nqb, LANES, Q_BLOCK), lambda b, c: (b, 0, 0, 0)),
            pl.BlockSpec((N_A, KV_W), brow),
            pl.BlockSpec((KV_W, N_A), brow),
            pl.BlockSpec((LANES, Q_BLOCK), lambda b, c: (0, b * nqb + c)),
            pl.BlockSpec((N_SEL, N_A), c2),
        ],
        out_specs=pl.BlockSpec((Q_BLOCK, Q_W), qrow),
        out_shape=jax.ShapeDtypeStruct((batch * seq, Q_W), BF16),
        scratch_shapes=[
            pltpu.VMEM((n_g, 1, HQ), F32),
            pltpu.VMEM((n_g, LANES, HQ), F32),
            pltpu.VMEM((n_g, HEAD_DIM, HQ), F32),
            pltpu.VMEM((n_g, HEAD_DIM, HQ), F32),
            pltpu.VMEM((n_g, N_SEL, Q_BLOCK), F32),
            pltpu.VMEM((Q_W, Q_BLOCK), F32),
        ],
        compiler_params=pltpu.CompilerParams(
            dimension_semantics=("parallel", "arbitrary"), vmem_limit_bytes=VMEM_LIMIT),
        name="nsa",
    )(qpad, ksl, kwn, vslt, vwnt, kcmp, vcmpt, gt, ot)


MEM_TM = 512


def _memattn_kernel(mq_ref, mem_ref, wkv_ref, o_ref, kv_s):
    @pl.when(pl.program_id(1) == 0)
    def _():
        kv_s[...] = _dot(mem_ref[...].astype(BF16), wkv_ref[...]).astype(BF16)

    for h in range(MEM_HEADS):
        cols = slice(h * MEM_HEAD_DIM, (h + 1) * MEM_HEAD_DIM)
        s = _dot_nt(mq_ref[:, cols], kv_s[:, cols]) * (MEM_HEAD_DIM ** -0.5)
        m = jnp.max(s, axis=-1, keepdims=True)
        e = jnp.exp(s - m)
        p = e / jnp.sum(e, axis=-1, keepdims=True)
        vcols = slice(MEM_WIDTH + h * MEM_HEAD_DIM, MEM_WIDTH + (h + 1) * MEM_HEAD_DIM)
        o_ref[:, cols] = _dot(p.astype(BF16), kv_s[:, vcols]).astype(BF16)


def _memattn_call(mq, mem2, wkv, batch, seq):
    m_len = mem2.shape[0] // batch
    d = mem2.shape[1]
    per = seq // MEM_TM
    return pl.pallas_call(
        _memattn_kernel,
        grid=(batch, per),
        in_specs=[
            pl.BlockSpec((MEM_TM, MEM_WIDTH), lambda b, i: (b * per + i, 0)),
            pl.BlockSpec((m_len, d), lambda b, i: (b, 0)),
            pl.BlockSpec((d, 2 * MEM_WIDTH), lambda b, i: (0, 0)),
        ],
        out_specs=pl.BlockSpec((MEM_TM, MEM_WIDTH), lambda b, i: (b * per + i, 0)),
        out_shape=jax.ShapeDtypeStruct((batch * seq, MEM_WIDTH), BF16),
        scratch_shapes=[pltpu.VMEM((m_len, 2 * MEM_WIDTH), BF16)],
        compiler_params=pltpu.CompilerParams(
            dimension_semantics=("parallel", "arbitrary"), vmem_limit_bytes=VMEM_LIMIT),
        name="memattn",
    )(mq, mem2, wkv)


MERGE_TM = 512


def _merge_kernel(x_ref, onsa_ref, osgu_ref, omem_ref, wg_ref, wbn_ref, wbs_ref, wbm_ref, wo_ref,
                  g_ref, b_ref, out_ref):
    x = x_ref[...]
    xb = x.astype(BF16)
    d = x.shape[1]
    y = None
    for br, (o_r, w_r) in enumerate(((onsa_ref, wbn_ref), (osgu_ref, wbs_ref), (omem_ref, wbm_ref))):
        gate = _sigmoid(_dot(xb, wg_ref[:, br * d:(br + 1) * d]))
        term = gate * _dot(o_r[...], w_r[...])
        y = term if y is None else y + term
    z = DN_ALPHA * x + _dot(y.astype(BF16), wo_ref[...])
    out_ref[...] = _layer_norm(z, g_ref[...], b_ref[...])


def _merge_call(x2, onsa, osgu, omem, wg, wbn, wbs, wbm, wo, g1, b1):
    t, d = x2.shape
    row = lambda i: (i, 0)
    c2 = lambda i: (0, 0)
    full = lambda a: pl.BlockSpec(a.shape, c2)
    return pl.pallas_call(
        _merge_kernel,
        grid=(t // MERGE_TM,),
        in_specs=[
            pl.BlockSpec((MERGE_TM, d), row),
            pl.BlockSpec((MERGE_TM, onsa.shape[1]), row),
            pl.BlockSpec((MERGE_TM, osgu.shape[1]), row),
            pl.BlockSpec((MERGE_TM, omem.shape[1]), row),
            full(wg), full(wbn), full(wbs), full(wbm), full(wo), full(g1), full(b1),
        ],
        out_specs=pl.BlockSpec((MERGE_TM, d), row),
        out_shape=jax.ShapeDtypeStruct((t, d), F32),
        compiler_params=pltpu.CompilerParams(
            dimension_semantics=("parallel",), vmem_limit_bytes=VMEM_LIMIT),
        name="merge",
    )(x2, onsa, osgu, omem, wg, wbn, wbs, wbm, wo, g1, b1)


MOE_TM = 1024
MOE_CH = 128
_E_OFF = N_GROUPS
_XA_W = 1024 + LANES


def _moe_kernel(x_ref, tri_ref, wr_ref, br_ref, wg_ref, wu_ref, wd_ref, g_ref, b_ref, out_ref,
                xa_s, keyc_s, keyr_s, acc_s, cnt_s):
    grp = pl.program_id(1)
    tm = x_ref.shape[0]
    d = x_ref.shape[1]

    @pl.when(grp == 0)
    def _route():
        xb = x_ref[...].astype(BF16)
        xa_s[:, 0:d] = xb
        logits = _dot(xb, wr_ref[...]) + br_ref[...]
        lane = lax.broadcasted_iota(jnp.int32, logits.shape, 1).astype(F32)
        is_g = lane < float(N_GROUPS)
        gl = jnp.where(is_g, logits, NEG)
        gmax = jnp.max(gl, axis=-1, keepdims=True)
        gidx = jnp.min(jnp.where(is_g & (gl == gmax), lane, float(LANES)), axis=-1, keepdims=True)
        gsum = jnp.sum(jnp.where(is_g, jnp.exp(gl - gmax), 0.0), axis=-1, keepdims=True)
        gprob = 1.0 / gsum
        lo = float(_E_OFF) + gidx * float(EXPERTS_PER_GROUP)
        in_e = (lane >= lo) & (lane < lo + float(EXPERTS_PER_GROUP))
        el = jnp.where(in_e, logits, NEG)
        emax = jnp.max(el, axis=-1, keepdims=True)
        ee = jnp.where(in_e, jnp.exp(el - emax), 0.0)
        ep = ee / jnp.sum(ee, axis=-1, keepdims=True)
        t1 = jnp.max(jnp.where(in_e, ep, -1.0), axis=-1, keepdims=True)
        i1 = jnp.min(jnp.where(in_e & (ep == t1), lane, float(LANES)), axis=-1, keepdims=True)
        rest = in_e & (lane != i1)
        t2 = jnp.max(jnp.where(rest, ep, -1.0), axis=-1, keepdims=True)
        i2 = jnp.min(jnp.where(rest & (ep == t2), lane, float(LANES)), axis=-1, keepdims=True)
        den = t1 + t2
        ew = jnp.where(lane == i1, t1 / den, 0.0) + jnp.where(lane == i2, t2 / den, 0.0)
        cw = ew * gprob

        ilane = lax.broadcasted_iota(jnp.int32, logits.shape, 1)
        cwf = pltpu.roll(cw, LANES - _E_OFF, 1)
        for k in range(1, N_GROUPS):
            cwf = cwf + pltpu.roll(cw, LANES - _E_OFF - k * EXPERTS_PER_GROUP, 1)
        cwf = jnp.where(ilane < EXPERTS_PER_GROUP, cwf, 0.0)
        hi = cwf.astype(BF16).astype(F32)
        r1 = cwf - hi
        lo = r1.astype(BF16).astype(F32)
        lo2 = (r1 - lo).astype(BF16).astype(F32)
        pieces = jnp.where(ilane < EXPERTS_PER_GROUP, hi,
                           jnp.where(ilane < 2 * EXPERTS_PER_GROUP,
                                     pltpu.roll(lo, EXPERTS_PER_GROUP, 1),
                                     pltpu.roll(lo2, 2 * EXPERTS_PER_GROUP, 1)))
        xa_s[:, d:d + LANES] = pieces.astype(BF16)

        onehot = jnp.where(lane == gidx, 1.0, 0.0)
        tri = tri_ref[...]
        rank_c = _dot(tri, onehot.astype(BF16))
        keyc_s[...] = jnp.where(onehot > 0.5, rank_c, -1.0)
        onehot_t = onehot.T
        rank_r = _dot_nt(onehot_t.astype(BF16), tri)
        keyr_s[...] = jnp.where(onehot_t > 0.5, rank_r, -1.0)[0:8, :]
        counts = jnp.sum(onehot, axis=0, keepdims=True)
        lane_r = lax.broadcasted_iota(jnp.int32, counts.shape, 1)
        for k in range(N_GROUPS):
            cnt_s[k] = jnp.sum(jnp.where(lane_r == k, counts, 0.0)).astype(jnp.int32)
        acc_s[...] = jnp.zeros(acc_s.shape, F32)

    n_rows = cnt_s[grp]
    keyr = keyr_s[pl.ds(grp, 1), :]
    lane_c = lax.broadcasted_iota(jnp.int32, keyc_s.shape, 1)
    keyc = jnp.sum(jnp.where(lane_c == grp, keyc_s[...], 0.0), axis=-1, keepdims=True)

    def sweep(k, carry):
        base = (k * MOE_CH).astype(F32)
        r_row = lax.broadcasted_iota(jnp.int32, (MOE_CH, tm), 0).astype(F32) + base
        pick = jnp.where(keyr == r_row, 1.0, 0.0).astype(BF16)
        ga = _dot(pick, xa_s[...])
        xg = ga[:, 0:d].astype(BF16)
        cwg = ga[:, d:d + LANES]
        cs = (cwg + pltpu.roll(cwg, LANES - EXPERTS_PER_GROUP, 1)
              + pltpu.roll(cwg, LANES - 2 * EXPERTS_PER_GROUP, 1))
        y = None
        for e in range(EXPERTS_PER_GROUP):
            hg = _dot(xg, wg_ref[0, e])
            hu = _dot(xg, wu_ref[0, e])
            hid = ((hg * _sigmoid(hg)) * hu) * cs[:, e:e + 1]
            term = _dot(hid.astype(BF16), wd_ref[0, e])
            y = term if y is None else y + term
        r_col = lax.broadcasted_iota(jnp.int32, (tm, MOE_CH), 1).astype(F32) + base
        put = jnp.where(keyc == r_col, 1.0, 0.0).astype(BF16)
        acc_s[...] += _dot(put, y.astype(BF16))
        return carry

    lax.fori_loop(0, (n_rows + MOE_CH - 1) // MOE_CH, sweep, 0)

    @pl.when(grp == N_GROUPS - 1)
    def _fin():
        z = DN_ALPHA * x_ref[...] + acc_s[...]
        out_ref[...] = _layer_norm(z, g_ref[...], b_ref[...])


def _moe_call(x1, wr, br, wg, wu, wd, g2, b2):
    t, d = x1.shape
    assert d + LANES == _XA_W
    row = lambda i, g: (i, 0)
    c2 = lambda i, g: (0, 0)
    idx = np.arange(MOE_TM)
    tri = jnp.asarray(idx[None, :] < idx[:, None], dtype=BF16)
    return pl.pallas_call(
        _moe_kernel,
        grid=(t // MOE_TM, N_GROUPS),
        in_specs=[
            pl.BlockSpec((MOE_TM, d), row, pipeline_mode=pl.Buffered(1)),
            pl.BlockSpec((MOE_TM, MOE_TM), c2, pipeline_mode=pl.Buffered(1)),
            pl.BlockSpec(wr.shape, c2),
            pl.BlockSpec(br.shape, c2),
            pl.BlockSpec((1, EXPERTS_PER_GROUP, d, EXPERT_FF), lambda i, g: (g, 0, 0, 0)),
            pl.BlockSpec((1, EXPERTS_PER_GROUP, d, EXPERT_FF), lambda i, g: (g, 0, 0, 0)),
            pl.BlockSpec((1, EXPERTS_PER_GROUP, EXPERT_FF, d), lambda i, g: (g, 0, 0, 0)),
            pl.BlockSpec(g2.shape, c2),
            pl.BlockSpec(b2.shape, c2),
        ],
        out_specs=pl.BlockSpec((MOE_TM, d), row),
        out_shape=jax.ShapeDtypeStruct((t, d), F32),
        scratch_shapes=[
            pltpu.VMEM((MOE_TM, _XA_W), BF16),
            pltpu.VMEM((MOE_TM, LANES), F32),
            pltpu.VMEM((8, MOE_TM), F32),
            pltpu.VMEM((MOE_TM, d), F32),
            pltpu.SMEM((N_GROUPS,), jnp.int32),
        ],
        compiler_params=pltpu.CompilerParams(
            dimension_semantics=("parallel", "arbitrary"), vmem_limit_bytes=VMEM_LIMIT),
        name="moe",
    )(x1, tri, wr, br, wg, wu, wd, g2, b2)


def _rope_table(seq):
    half = ROPE_DIM // 2
    inv = ROPE_THETA ** (-jnp.arange(0, ROPE_DIM, 2, dtype=F32) / ROPE_DIM)
    ang = jnp.arange(seq, dtype=F32)[:, None] * inv[None, :]
    cos, sin = jnp.cos(ang), jnp.sin(ang)
    rest = HEAD_DIM - ROPE_DIM
    one = jnp.ones((seq, rest), F32)
    zero = jnp.zeros((seq, rest), F32)
    zh = jnp.zeros((seq, half), F32)
    c_h = jnp.concatenate([cos, cos, one], axis=1)
    dn_h = jnp.concatenate([-sin, zh, zero], axis=1)
    up_h = jnp.concatenate([zh, sin, zero], axis=1)
    rep = LANES // HEAD_DIM
    return jnp.concatenate([jnp.tile(c_h, (1, rep)), jnp.tile(dn_h, (1, rep)), jnp.tile(up_h, (1, rep))],
                           axis=1)


def _expand_cmp_weights(pe, w1, w2):
    g_n = NSA_KV_GROUPS
    w1r = w1.reshape(2, CMP_STRIDE, HEAD_DIM, CMP_HIDDEN)
    eye = jnp.eye(g_n, dtype=w1.dtype)
    w1e = jnp.einsum('pldh,ge->plgdeh', w1r, eye).reshape(2, A_W, g_n * CMP_HIDDEN)
    w2e = jnp.einsum('hd,ge->ghed', w2, eye).reshape(g_n * CMP_HIDDEN, g_n * HEAD_DIM)
    per = pe.reshape(2, CMP_STRIDE, 1, HEAD_DIM)
    pee = jnp.broadcast_to(per, (2, CMP_STRIDE, g_n, HEAD_DIM)).reshape(2, A_W)
    pee = jnp.concatenate([pee, jnp.zeros((6, A_W), pe.dtype)], axis=0)
    return pee.astype(F32), w1e.astype(BF16), w2e.astype(BF16)


def kernel(x, mem, w_in, cmp_pe_k, cmp_w1_k, cmp_w2_k, cmp_pe_v, cmp_w1_v, cmp_w2_v, sgu_ln_g, sgu_ln_b,
           sgu_w_s, sgu_b_s, w_mem_kv, w_br_nsa, w_br_sgu, w_br_mem, w_o, ln1_g, ln1_b, w_router_group,
           b_router_group, w_router_expert, b_router_expert, w_exp_gate, w_exp_up, w_exp_down, ln2_g, ln2_b):
    batch, seq, d = x.shape
    t = batch * seq
    assert w_in.shape[0] == DEPTH == 1
    assert seq % PROJ_TM == 0 and seq // CMP_STRIDE == N_A and seq // SLC_BLOCK == N_SEL

    w = w_in[0]
    offs = np.cumsum([0, Q_W, KV_W, KV_W, KV_W, KV_W, KV_W, KV_W, GATE_W, 2 * SGU_WIDTH, MEM_WIDTH, 3 * d])
    seg = lambda i: w[:, int(offs[i]):int(offs[i + 1])]
    gate_w = jnp.pad(seg(7), ((0, 0), (0, LANES - GATE_W)))
    wp = jnp.concatenate([seg(0), seg(1), seg(3), seg(5), seg(2), seg(4), seg(6), gate_w, seg(8), seg(9)],
                         axis=1).astype(BF16)
    w_merge = seg(10).astype(BF16)
    rope = _rope_table(seq)
    tril = jnp.tril(jnp.ones((SGU_CHUNK, SGU_CHUNK), dtype=bool))
    ws = jnp.where(tril[None], sgu_w_s[0], 0.0).astype(BF16)
    bs = jnp.repeat(sgu_b_s[0].T, SGU_WIDTH // SGU_GROUPS, axis=1)

    x2 = x.reshape(t, d)
    (qpad, kc, ksl, kwn, vc, vsl, vwn, gate, osgu, mq) = _proj_call(
        x2, wp, rope, sgu_ln_g[0][None], sgu_ln_b[0][None], ws, bs, seq)

    pek, w1k, w2k = _expand_cmp_weights(cmp_pe_k[0], cmp_w1_k[0], cmp_w2_k[0])
    pev, w1v, w2v = _expand_cmp_weights(cmp_pe_v[0], cmp_w1_v[0], cmp_w2_v[0])
    kcmp, vcmp = _compress_call(kc.reshape(batch * N_A, A_W), vc.reshape(batch * N_A, A_W),
                                pek, pev, w1k, w2k, w1v, w2v, batch)

    nqb = seq // Q_BLOCK
    n_g = NSA_KV_GROUPS
    vslt = vsl.reshape(batch, seq, n_g, HEAD_DIM).transpose(0, 2, 3, 1)
    vslt = jnp.concatenate([vslt, jnp.ones_like(vslt)], axis=2).reshape(batch * n_g, LANES, seq)
    vwnt = vwn.reshape(batch, nqb, Q_BLOCK, n_g, HEAD_DIM).transpose(0, 3, 1, 4, 2)
    vwnt = jnp.concatenate([vwnt, jnp.ones_like(vwnt)], axis=3).reshape(batch * n_g, nqb, LANES, Q_BLOCK)
    vcmpt = vcmp.reshape(batch, N_A, KV_W).transpose(0, 2, 1).reshape(batch * KV_W, N_A)
    gt = gate.T
    ci = np.arange(N_A)
    sj = np.arange(N_SEL)
    overlap = ((ci[None, :] * CMP_STRIDE + CMP_LEN - 1 >= sj[:, None] * SLC_BLOCK)
               & (ci[None, :] * CMP_STRIDE <= sj[:, None] * SLC_BLOCK + SLC_BLOCK - 1)
               & (ci[None, :] < (seq - CMP_LEN) // CMP_STRIDE + 1))
    ot = jnp.asarray(overlap, dtype=BF16)
    onsa = _nsa_call(qpad, ksl, kwn, vslt, vwnt, kcmp, vcmpt, gt, ot, batch, seq)

    omem = _memattn_call(mq, mem.reshape(batch * mem.shape[1], d), w_mem_kv[0].astype(BF16), batch, seq)

    x1 = _merge_call(x2, onsa, osgu, omem, w_merge, w_br_nsa[0].astype(BF16), w_br_sgu[0].astype(BF16),
                     w_br_mem[0].astype(BF16), w_o[0].astype(BF16), ln1_g[0][None], ln1_b[0][None])

    n_r = N_GROUPS + N_GROUPS * EXPERTS_PER_GROUP
    wr = jnp.pad(jnp.concatenate([w_router_group[0], w_router_expert[0]], axis=1),
                 ((0, 0), (0, LANES - n_r))).astype(BF16)
    br = jnp.pad(jnp.concatenate([b_router_group[0], b_router_expert[0]]), (0, LANES - n_r))[None]
    out = _moe_call(x1, wr, br, w_exp_gate[0].astype(BF16), w_exp_up[0].astype(BF16),
                    w_exp_down[0].astype(BF16), ln2_g[0][None], ln2_b[0][None])
    return out.reshape(batch, seq, d)
```

```python
import functools

import numpy as np
import jax
import jax.numpy as jnp
from jax import lax
from jax.experimental import pallas as pl
from jax.experimental.pallas import tpu as pltpu

NSA_HEADS = 8
NSA_KV_GROUPS = 2
NSA_HPG = NSA_HEADS // NSA_KV_GROUPS
HEAD_DIM = 64
CMP_LEN = 32
CMP_STRIDE = 16
CMP_HIDDEN = 256
SLC_BLOCK = 64
SLC_TOPN = 8
WINDOW = 512
Q_BLOCK = 128
N_BAND = WINDOW // Q_BLOCK
ROPE_THETA = 500000.0
ROPE_DIM = HEAD_DIM // 4
SGU_CHUNK = 128
SGU_GROUPS = 8
SGU_WIDTH = 512
MEM_HEADS = 4
MEM_HEAD_DIM = 128
MEM_WIDTH = MEM_HEADS * MEM_HEAD_DIM
N_GROUPS = 4
EXPERTS_PER_GROUP = 8
EXPERT_FF = 256
DEPTH = 1
DN_ALPHA = (2.0 * DEPTH) ** 0.25
LN_EPS = 1e-5
NEG = -1e30
LOG2E = 1.4426950408889634

LANES = 128
Q_W = NSA_HEADS * HEAD_DIM
KV_W = NSA_KV_GROUPS * HEAD_DIM
GATE_W = NSA_HEADS * 3
VMEM_LIMIT = 56 * 1024 * 1024
MOE_VMEM_LIMIT = 60 * 1024 * 1024

BF16 = jnp.bfloat16
F32 = jnp.float32


def _dot(a, b):
    return jnp.dot(a, b, preferred_element_type=F32)


def _dot_nt(a, b):
    return lax.dot_general(a, b, (((1,), (1,)), ((), ())), preferred_element_type=F32)


def _sigmoid(x):
    return 1.0 / (1.0 + jnp.exp(-x))


def _gelu(x):
    return 0.5 * x * (1.0 + lax.erf(x * (2.0 ** -0.5)))


def _layer_norm(x, g, b):
    mu = jnp.mean(x, axis=-1, keepdims=True)
    xc = x - mu
    var = jnp.mean(xc * xc, axis=-1, keepdims=True)
    return xc * lax.rsqrt(var + LN_EPS) * g + b


PROJ_TM = 512
_ROPE_COLS = Q_W + 3 * KV_W
_V_OFF = _ROPE_COLS
_G_OFF = _V_OFF + 3 * KV_W
_SGU_OFF = _G_OFF + LANES
_MQ_OFF = _SGU_OFF + 2 * SGU_WIDTH
_PROJ_COLS = _MQ_OFF + MEM_WIDTH


def _proj_kernel(x_ref, w_ref, rope_ref, lng_ref, lnb_ref, ws_ref, bs_ref,
                 q_ref, kc_ref, ksl_ref, kwn_ref, vc_ref, vsl_ref, vwn_ref, gate_ref, osgu_ref, mq_ref):
    xb = x_ref[...].astype(BF16)
    lane = lax.broadcasted_iota(jnp.int32, (PROJ_TM, LANES), 1)
    low = lane < HEAD_DIM
    cos = rope_ref[:, 0:LANES]
    s_dn = rope_ref[:, LANES:2 * LANES]
    s_up = rope_ref[:, 2 * LANES:3 * LANES]

    h = _dot(xb, w_ref[:, 0:_ROPE_COLS])
    k_refs = (kc_ref, ksl_ref, kwn_ref)
    for j in range(_ROPE_COLS // LANES):
        blk = h[:, j * LANES:(j + 1) * LANES]
        r = (blk * cos + pltpu.roll(blk, LANES - ROPE_DIM // 2, 1) * s_dn
             + pltpu.roll(blk, ROPE_DIM // 2, 1) * s_up)
        if j < Q_W // LANES:
            r = r * (HEAD_DIM ** -0.5 * LOG2E)
            sw = pltpu.roll(r, HEAD_DIM, 1)
            g = (2 * j) // NSA_HPG
            if g == 0:
                h0 = jnp.where(low, r, 0.0)
                h1 = jnp.where(low, sw, 0.0)
            else:
                h0 = jnp.where(low, 0.0, sw)
                h1 = jnp.where(low, 0.0, r)
            q_ref[:, (2 * j) * LANES:(2 * j + 1) * LANES] = h0.astype(BF16)
            q_ref[:, (2 * j + 1) * LANES:(2 * j + 2) * LANES] = h1.astype(BF16)
        else:
            k_refs[j - Q_W // LANES][...] = r.astype(BF16)

    hv = _dot(xb, w_ref[:, _V_OFF:_G_OFF])
    vc_ref[...] = hv[:, 0:LANES].astype(BF16)
    vsl_ref[...] = hv[:, LANES:2 * LANES].astype(BF16)
    vwn_ref[...] = hv[:, 2 * LANES:3 * LANES].astype(BF16)

    gate_ref[...] = _sigmoid(_dot(xb, w_ref[:, _G_OFF:_SGU_OFF]))

    mq_ref[...] = _dot(xb, w_ref[:, _MQ_OFF:_PROJ_COLS]).astype(BF16)

    z = _gelu(_dot(xb, w_ref[:, _SGU_OFF:_MQ_OFF]))
    u = z[:, 0:SGU_WIDTH]
    v = _layer_norm(z[:, SGU_WIDTH:2 * SGU_WIDTH], lng_ref[...], lnb_ref[...]).astype(BF16)
    lane_c = lax.broadcasted_iota(jnp.int32, (SGU_CHUNK, LANES), 1)
    low_c = lane_c < (SGU_WIDTH // SGU_GROUPS)
    for ci in range(PROJ_TM // SGU_CHUNK):
        rows = slice(ci * SGU_CHUNK, (ci + 1) * SGU_CHUNK)
        for gp in range(SGU_WIDTH // LANES):
            cols = slice(gp * LANES, (gp + 1) * LANES)
            vblk = v[rows, cols]
            sv = jnp.where(low_c, _dot(ws_ref[2 * gp], vblk), _dot(ws_ref[2 * gp + 1], vblk))
            sv = sv + bs_ref[:, cols]
            osgu_ref[rows, cols] = (u[rows, cols] * sv).astype(BF16)


def _proj_call(x2, wp, rope, lng, lnb, ws, bs, seq):
    t = x2.shape[0]
    d = x2.shape[1]
    nt = t // PROJ_TM
    per_seq = seq // PROJ_TM
    row = lambda i: (i, 0)
    const2 = lambda i: (0, 0)
    out_shapes = (
        jax.ShapeDtypeStruct((t, NSA_HEADS * LANES), BF16),
        *[jax.ShapeDtypeStruct((t, KV_W), BF16) for _ in range(6)],
        jax.ShapeDtypeStruct((t, LANES), F32),
        jax.ShapeDtypeStruct((t, SGU_WIDTH), BF16),
        jax.ShapeDtypeStruct((t, MEM_WIDTH), BF16),
    )
    out_specs = (
        pl.BlockSpec((PROJ_TM, NSA_HEADS * LANES), row),
        *[pl.BlockSpec((PROJ_TM, KV_W), row) for _ in range(6)],
        pl.BlockSpec((PROJ_TM, LANES), row),
        pl.BlockSpec((PROJ_TM, SGU_WIDTH), row),
        pl.BlockSpec((PROJ_TM, MEM_WIDTH), row),
    )
    return pl.pallas_call(
        _proj_kernel,
        grid=(nt,),
        in_specs=[
            pl.BlockSpec((PROJ_TM, d), row),
            pl.BlockSpec((d, _PROJ_COLS), const2),
            pl.BlockSpec((PROJ_TM, 3 * LANES), lambda i: (i % per_seq, 0)),
            pl.BlockSpec((1, SGU_WIDTH), const2),
            pl.BlockSpec((1, SGU_WIDTH), const2),
            pl.BlockSpec((SGU_GROUPS, SGU_CHUNK, SGU_CHUNK), lambda i: (0, 0, 0)),
            pl.BlockSpec((SGU_CHUNK, SGU_WIDTH), const2),
        ],
        out_specs=out_specs,
        out_shape=out_shapes,
        compiler_params=pltpu.CompilerParams(
            dimension_semantics=("parallel",), vmem_limit_bytes=VMEM_LIMIT),
        name="proj",
    )(x2, wp, rope, lng, lnb, ws, bs)


N_A = 128
A_W = CMP_STRIDE * KV_W


def _compress_kernel(ka_ref, va_ref, pek_ref, pev_ref, w1k_ref, w2k_ref, w1v_ref, w2v_ref,
                     kcmp_ref, vcmp_ref):
    def one(a_ref, pe_ref, w1_ref, w2_ref, out_ref):
        a = a_ref[...].astype(F32)
        top = (a + pe_ref[0:1, :]).astype(BF16)
        bot = (a + pe_ref[1:2, :]).astype(BF16)
        h1 = _dot(top, w1_ref[0])
        h2 = _dot(bot, w1_ref[1])
        pre = h1 + pltpu.roll(h2, N_A - 1, 0)
        act = _gelu(pre).astype(BF16)
        out_ref[...] = _dot(act, w2_ref[...]).astype(BF16)

    one(ka_ref, pek_ref, w1k_ref, w2k_ref, kcmp_ref)
    one(va_ref, pev_ref, w1v_ref, w2v_ref, vcmp_ref)


def _compress_call(ka, va, pek, pev, w1k, w2k, w1v, w2v, batch):
    row = lambda b: (b, 0)
    c2 = lambda b: (0, 0)
    c3 = lambda b: (0, 0, 0)
    hid2 = NSA_KV_GROUPS * CMP_HIDDEN
    return pl.pallas_call(
        _compress_kernel,
        grid=(batch,),
        in_specs=[
            pl.BlockSpec((N_A, A_W), row),
            pl.BlockSpec((N_A, A_W), row),
            pl.BlockSpec((8, A_W), c2),
            pl.BlockSpec((8, A_W), c2),
            pl.BlockSpec((2, A_W, hid2), c3),
            pl.BlockSpec((hid2, KV_W), c2),
            pl.BlockSpec((2, A_W, hid2), c3),
            pl.BlockSpec((hid2, KV_W), c2),
        ],
        out_specs=(pl.BlockSpec((N_A, KV_W), row), pl.BlockSpec((N_A, KV_W), row)),
        out_shape=(jax.ShapeDtypeStruct((batch * N_A, KV_W), BF16),
                   jax.ShapeDtypeStruct((batch * N_A, KV_W), BF16)),
        compiler_params=pltpu.CompilerParams(
            dimension_semantics=("parallel",), vmem_limit_bytes=VMEM_LIMIT),
        name="compress",
    )(ka, va, pek, pev, w1k, w2k, w1v, w2v)


N_SEL = 32
SEL_CHUNK = 512
HQ = NSA_HPG * Q_BLOCK


def _nsa_kernel(q_ref, ksl_ref, kwn_ref, vslt_ref, vwnt_ref, kcmp_ref, vcmpt_ref, gt_ref,
                ot_ref, o_ref, m_s, acc_s, ocmp_s, owin_s, cap_s, out_s):
    c = pl.program_id(1)
    lane_hq = lax.broadcasted_iota(jnp.int32, (1, HQ), 1)
    pos_hq = c * Q_BLOCK + (lane_hq & (Q_BLOCK - 1))
    pos_q = c * Q_BLOCK + lax.broadcasted_iota(jnp.int32, (1, Q_BLOCK), 1)
    groups = range(NSA_KV_GROUPS)
    qgs = [jnp.concatenate(
        [q_ref[:, (g * NSA_HPG + hh) * LANES:(g * NSA_HPG + hh + 1) * LANES]
         for hh in range(NSA_HPG)], axis=0) for g in groups]

    _nsa_window(c, qgs, kwn_ref, vwnt_ref, owin_s)

    kcmp = kcmp_ref[...]
    s_c = [_dot_nt(kcmp, qgs[g]) for g in groups]
    n_idx = lax.broadcasted_iota(jnp.int32, (N_A, HQ), 0)
    valid_c = (n_idx * CMP_STRIDE + (CMP_LEN - 1)) <= pos_hq
    p_c = []
    for g in groups:
        sm_c = jnp.where(valid_c, s_c[g], NEG)
        m_c = jnp.max(sm_c, axis=0, keepdims=True)
        e_c = jnp.where(valid_c, jnp.exp2(sm_c - m_c), 0.0)
        d_c = jnp.sum(e_c, axis=0, keepdims=True)
        p_c.append(e_c / jnp.where(d_c > 0, d_c, 1.0))
    for g in groups:
        ocmp_s[g] = _dot(vcmpt_ref[g * HEAD_DIM:(g + 1) * HEAD_DIM, :], p_c[g].astype(BF16))

    ot = ot_ref[...]
    imp = []
    for g in groups:
        ps = (p_c[g][:, 0:Q_BLOCK] + p_c[g][:, Q_BLOCK:2 * Q_BLOCK]
              + p_c[g][:, 2 * Q_BLOCK:3 * Q_BLOCK] + p_c[g][:, 3 * Q_BLOCK:4 * Q_BLOCK])
        p_hi = ps.astype(BF16)
        r1 = ps - p_hi.astype(F32)
        p_lo = r1.astype(BF16)
        p_lo2 = (r1 - p_lo.astype(F32)).astype(BF16)
        imp.append(_dot(ot, p_hi) + _dot(ot, p_lo) + _dot(ot, p_lo2))

    j_idx = lax.broadcasted_iota(jnp.int32, (N_SEL, Q_BLOCK), 0)
    cur = pos_q // SLC_BLOCK
    future = j_idx > cur
    forced = (j_idx == 0) | (j_idx == cur) | (j_idx == cur - 1)
    imp = [jnp.where(future, NEG, jnp.where(forced, -NEG, imp[g])) for g in groups]
    rank = [jnp.zeros((N_SEL, Q_BLOCK), F32) for g in groups]
    for i in range(N_SEL):
        for g in groups:
            row = imp[g][i:i + 1, :]
            beats = (row > imp[g]) | ((row == imp[g]) & (j_idx > i))
            rank[g] = rank[g] + jnp.where(beats, 1.0, 0.0)
    for g in groups:
        cap_s[g] = jnp.where(rank[g] < float(SLC_TOPN), -NEG, NEG)
        m_s[g] = jnp.full((1, HQ), NEG, F32)
        acc_s[g] = jnp.zeros((LANES, HQ), F32)

    blocks_per_chunk = SEL_CHUNK // SLC_BLOCK

    def sel_chunk(kc):
        ks = slice(kc * SEL_CHUNK, (kc + 1) * SEL_CHUNK)
        kpos = kc * SEL_CHUNK + lax.broadcasted_iota(jnp.int32, (SEL_CHUNK, Q_BLOCK), 0)
        causal = kpos <= pos_q
        k_chunk = ksl_ref[ks, :]
        s = [_dot_nt(k_chunk, qgs[g]) for g in groups]
        for g in groups:
            cap = jnp.concatenate(
                [jnp.broadcast_to(cap_s[g, j:j + 1, :], (SLC_BLOCK, Q_BLOCK))
                 for j in range(kc * blocks_per_chunk, (kc + 1) * blocks_per_chunk)], axis=0)
            cap = jnp.where(causal, cap, NEG)
            sm = jnp.minimum(s[g], jnp.concatenate([cap] * NSA_HPG, axis=1))
            m_old = m_s[g]
            m_new = jnp.maximum(m_old, jnp.max(sm, axis=0, keepdims=True))
            alpha = jnp.exp2(m_old - m_new)
            e = jnp.exp2(sm - m_new)
            acc_s[g] = alpha * acc_s[g] + _dot(vslt_ref[g, :, ks], e.astype(BF16))
            m_s[g] = m_new

    sel_chunk(0)
    for kc in range(1, ksl_ref.shape[0] // SEL_CHUNK):
        pl.when(kc * (SEL_CHUNK // Q_BLOCK) <= c)(functools.partial(sel_chunk, kc))

    for g in groups:
        acc = acc_s[g]
        o_sel = acc[0:HEAD_DIM, :] * (1.0 / acc[HEAD_DIM:HEAD_DIM + 1, :])

        def gate_row(br):
            return jnp.concatenate(
                [gt_ref[(g * NSA_HPG + hh) * 3 + br:(g * NSA_HPG + hh) * 3 + br + 1, :]
                 for hh in range(NSA_HPG)], axis=1)
        o_t = gate_row(0) * ocmp_s[g] + gate_row(1) * o_sel + gate_row(2) * owin_s[g]
        for hh in range(NSA_HPG):
            h = g * NSA_HPG + hh
            out_s[h * HEAD_DIM:(h + 1) * HEAD_DIM, :] = o_t[:, hh * Q_BLOCK:(hh + 1) * Q_BLOCK]

    o_ref[...] = out_s[...].T.astype(BF16)


def _nsa_window(c, qgs, kwn_ref, vwnt_ref, owin_s):
    groups = range(NSA_KV_GROUPS)
    q_i = lax.broadcasted_iota(jnp.int32, (Q_BLOCK, Q_BLOCK), 1)
    k_i = lax.broadcasted_iota(jnp.int32, (Q_BLOCK, Q_BLOCK), 0)
    blks, caps = [], []
    for i in range(N_BAND + 1):
        blk = c - N_BAND + i
        blks.append(jnp.maximum(blk, 0))
        off = (N_BAND - i) * Q_BLOCK
        if off - (Q_BLOCK - 1) >= 0 and off + (Q_BLOCK - 1) < WINDOW:
            caps.append(jnp.where(blk >= 0, -NEG, NEG))
        else:
            diff = off + q_i - k_i
            ok = (diff >= 0) & (diff < WINDOW) & (blk >= 0)
            caps.append(jnp.concatenate([jnp.where(ok, -NEG, NEG)] * NSA_HPG, axis=1))
    k_blocks = [kwn_ref[pl.ds(pl.multiple_of(blks[i] * Q_BLOCK, Q_BLOCK), Q_BLOCK), :]
                for i in range(N_BAND + 1)]
    s_w = [[_dot_nt(k_blocks[i], qgs[g]) for i in range(N_BAND + 1)] for g in groups]
    for g in groups:
        sm_w = jnp.concatenate([jnp.minimum(s_w[g][i], caps[i]) for i in range(N_BAND + 1)], axis=0)
        m_w = jnp.max(sm_w, axis=0, keepdims=True)
        e_wb = jnp.exp2(sm_w - m_w).astype(BF16)
        o_win = _dot(vwnt_ref[g, blks[0]], e_wb[0:Q_BLOCK, :])
        for i in range(1, N_BAND + 1):
            o_win = o_win + _dot(vwnt_ref[g, blks[i]], e_wb[i * Q_BLOCK:(i + 1) * Q_BLOCK, :])
        owin_s[g] = o_win[0:HEAD_DIM, :] * (1.0 / o_win[HEAD_DIM:HEAD_DIM + 1, :])


def _nsa_call(qpad, ksl, kwn, vslt, vwnt, kcmp, vcmpt, gt, ot, batch, seq):
    nqb = seq // Q_BLOCK
    n_g = NSA_KV_GROUPS
    qrow = lambda b, c: (b * nqb + c, 0)
    brow = lambda b, c: (b, 0)
    c2 = lambda b, c: (0, 0)
    return pl.pallas_call(
        _nsa_kernel,
        grid=(batch, nqb),
        in_specs=[
            pl.BlockSpec((Q_BLOCK, NSA_HEADS * LANES), qrow),
            pl.BlockSpec((seq, KV_W), brow),
            pl.BlockSpec((seq, KV_W), brow),
            pl.BlockSpec((n_g, LANES, seq), lambda b, c: (b, 0, 0)),
            pl.BlockSpec((n_g, nqb, LANES, Q_BLOCK), lambda b, c: (b, 0, 0, 0)),
            pl.BlockSpec((N_A, KV_W), brow),
            pl.BlockSpec((KV_W, N_A), brow),
            pl.BlockSpec((LANES, Q_BLOCK), lambda b, c: (0, b * nqb + c)),
            pl.BlockSpec((N_SEL, N_A), c2),
        ],
        out_specs=pl.BlockSpec((Q_BLOCK, Q_W), qrow),
        out_shape=jax.ShapeDtypeStruct((batch * seq, Q_W), BF16),
        scratch_shapes=[
            pltpu.VMEM((n_g, 1, HQ), F32),
            pltpu.VMEM((n_g, LANES, HQ), F32),
            pltpu.VMEM((n_g, HEAD_DIM, HQ), F32),
            pltpu.VMEM((n_g, HEAD_DIM, HQ), F32),
            pltpu.VMEM((n_g, N_SEL, Q_BLOCK), F32),
            pltpu.VMEM((Q_W, Q_BLOCK), F32),
        ],
        compiler_params=pltpu.CompilerParams(
            dimension_semantics=("parallel", "arbitrary"), vmem_limit_bytes=VMEM_LIMIT),
        name="nsa",
    )(qpad, ksl, kwn, vslt, vwnt, kcmp, vcmpt, gt, ot)


MEM_TM = 512


def _memattn_kernel(mq_ref, mem_ref, wkv_ref, o_ref, kv_s):
    @pl.when(pl.program_id(1) == 0)
    def _():
        kv_s[...] = _dot(mem_ref[...].astype(BF16), wkv_ref[...]).astype(BF16)

    for h in range(MEM_HEADS):
        cols = slice(h * MEM_HEAD_DIM, (h + 1) * MEM_HEAD_DIM)
        s = _dot_nt(mq_ref[:, cols], kv_s[:, cols]) * (MEM_HEAD_DIM ** -0.5)
        m = jnp.max(s, axis=-1, keepdims=True)
        e = jnp.exp(s - m)
        p = e / jnp.sum(e, axis=-1, keepdims=True)
        vcols = slice(MEM_WIDTH + h * MEM_HEAD_DIM, MEM_WIDTH + (h + 1) * MEM_HEAD_DIM)
        o_ref[:, cols] = _dot(p.astype(BF16), kv_s[:, vcols]).astype(BF16)


def _memattn_call(mq, mem2, wkv, batch, seq):
    m_len = mem2.shape[0] // batch
    d = mem2.shape[1]
    per = seq // MEM_TM
    return pl.pallas_call(
        _memattn_kernel,
        grid=(batch, per),
        in_specs=[
            pl.BlockSpec((MEM_TM, MEM_WIDTH), lambda b, i: (b * per + i, 0)),
            pl.BlockSpec((m_len, d), lambda b, i: (b, 0)),
            pl.BlockSpec((d, 2 * MEM_WIDTH), lambda b, i: (0, 0)),
        ],
        out_specs=pl.BlockSpec((MEM_TM, MEM_WIDTH), lambda b, i: (b * per + i, 0)),
        out_shape=jax.ShapeDtypeStruct((batch * seq, MEM_WIDTH), BF16),
        scratch_shapes=[pltpu.VMEM((m_len, 2 * MEM_WIDTH), BF16)],
        compiler_params=pltpu.CompilerParams(
            dimension_semantics=("parallel", "arbitrary"), vmem_limit_bytes=VMEM_LIMIT),
        name="memattn",
    )(mq, mem2, wkv)


MERGE_TM = 512


def _merge_kernel(x_ref, onsa_ref, osgu_ref, omem_ref, wg_ref, wbn_ref, wbs_ref, wbm_ref, wo_ref,
                  g_ref, b_ref, out_ref):
    x = x_ref[...]
    xb = x.astype(BF16)
    d = x.shape[1]
    y = None
    for br, (o_r, w_r) in enumerate(((onsa_ref, wbn_ref), (osgu_ref, wbs_ref), (omem_ref, wbm_ref))):
        gate = _sigmoid(_dot(xb, wg_ref[:, br * d:(br + 1) * d]))
        term = gate * _dot(o_r[...], w_r[...])
        y = term if y is None else y + term
    z = DN_ALPHA * x + _dot(y.astype(BF16), wo_ref[...])
    out_ref[...] = _layer_norm(z, g_ref[...], b_ref[...])


def _merge_call(x2, onsa, osgu, omem, wg, wbn, wbs, wbm, wo, g1, b1):
    t, d = x2.shape
    row = lambda i: (i, 0)
    c2 = lambda i: (0, 0)
    full = lambda a: pl.BlockSpec(a.shape, c2)
    return pl.pallas_call(
        _merge_kernel,
        grid=(t // MERGE_TM,),
        in_specs=[
            pl.BlockSpec((MERGE_TM, d), row),
            pl.BlockSpec((MERGE_TM, onsa.shape[1]), row),
            pl.BlockSpec((MERGE_TM, osgu.shape[1]), row),
            pl.BlockSpec((MERGE_TM, omem.shape[1]), row),
            full(wg), full(wbn), full(wbs), full(wbm), full(wo), full(g1), full(b1),
        ],
        out_specs=pl.BlockSpec((MERGE_TM, d), row),
        out_shape=jax.ShapeDtypeStruct((t, d), F32),
        compiler_params=pltpu.CompilerParams(
            dimension_semantics=("parallel",), vmem_limit_bytes=VMEM_LIMIT),
        name="merge",
    )(x2, onsa, osgu, omem, wg, wbn, wbs, wbm, wo, g1, b1)


MOE_TM = 1024
MOE_CH = 128
_YS_ROWS = MOE_TM + N_GROUPS * MOE_CH
_R_SLOT = 8
_DEST_LANE = 3 * _R_SLOT
_HID = EXPERTS_PER_GROUP * EXPERT_FF


def _moe_kernel(x_ref, tri_ref, wr_ref, br_ref, wg_ref, wu_ref, wd_ref, g_ref, b_ref, out_ref,
                xa_s, tok_s, keyr_s, ys_s, cnt_s, base_s):
    grp = pl.program_id(1)
    tm = x_ref.shape[0]
    d = x_ref.shape[1]

    @pl.when(grp == 0)
    def _route():
        xb = x_ref[...].astype(BF16)
        xa_s[:, 0:d] = xb
        logits = _dot(xb, wr_ref[...]) + br_ref[...]
        lt = logits.T
        row = lax.broadcasted_iota(jnp.int32, (_R_SLOT, tm), 0)
        gl = jnp.where(row < N_GROUPS, lt[0:_R_SLOT], NEG)
        gmax = jnp.max(gl, axis=0, keepdims=True)
        gidx = jnp.min(jnp.where(gl == gmax, row, _R_SLOT), axis=0, keepdims=True)
        gprob = 1.0 / jnp.sum(jnp.exp(gl - gmax), axis=0, keepdims=True)
        el = lt[_R_SLOT:2 * _R_SLOT]
        for k in range(1, N_GROUPS):
            el = jnp.where(gidx == k, lt[(k + 1) * _R_SLOT:(k + 2) * _R_SLOT], el)
        ee = jnp.exp(el - jnp.max(el, axis=0, keepdims=True))
        ep = ee / jnp.sum(ee, axis=0, keepdims=True)
        t1 = jnp.max(ep, axis=0, keepdims=True)
        i1 = jnp.min(jnp.where(ep == t1, row, _R_SLOT), axis=0, keepdims=True)
        rest = row != i1
        t2 = jnp.max(jnp.where(rest, ep, -1.0), axis=0, keepdims=True)
        i2 = jnp.min(jnp.where(rest & (ep == t2), row, _R_SLOT), axis=0, keepdims=True)
        den = t1 + t2
        cwf = (jnp.where(row == i1, t1 / den, 0.0) + jnp.where(row == i2, t2 / den, 0.0)) * gprob
        hi = cwf.astype(BF16).astype(F32)
        r1 = cwf - hi
        lo = r1.astype(BF16).astype(F32)
        lo2 = (r1 - lo).astype(BF16).astype(F32)

        onehot = jnp.where(row == gidx, 1.0, 0.0)
        rank = _dot_nt(onehot.astype(BF16), tri_ref[...])
        keyr_s[...] = jnp.where(onehot > 0.5, rank, -1.0)
        base = jnp.int32(0)
        basev = jnp.zeros((1, tm), F32)
        for k in range(N_GROUPS):
            n_k = jnp.sum(onehot[k:k + 1, :]).astype(jnp.int32)
            cnt_s[k] = n_k
            base_s[k] = base
            basev = jnp.where(gidx == k, base.astype(F32), basev)
            base = base + ((n_k + MOE_CH - 1) // MOE_CH) * MOE_CH
        dest = jnp.sum(rank * onehot, axis=0, keepdims=True) + basev
        tok = jnp.concatenate(
            [hi, lo, lo2, jnp.broadcast_to(dest, (_R_SLOT, tm)),
             jnp.zeros((LANES - 4 * _R_SLOT, tm), F32)], axis=0).T
        tok_s[...] = tok
        lane = lax.broadcasted_iota(jnp.int32, tok.shape, 1)
        xa_s[:, d:d + LANES] = jnp.where(lane < _DEST_LANE, tok, 0.0).astype(BF16)
        ys_s[...] = jnp.zeros(ys_s.shape, BF16)

    n_rows = cnt_s[grp]
    row0 = base_s[grp]
    keyr = keyr_s[pl.ds(grp, 1), :]
    half = _HID // 2

    def sweep(k, carry):
        r_row = (lax.broadcasted_iota(jnp.int32, (MOE_CH, tm), 0) + k * MOE_CH).astype(F32)
        pick = jnp.where(keyr == r_row, 1.0, 0.0).astype(BF16)
        ga = _dot(pick, xa_s[...])
        xg = ga[:, 0:d].astype(BF16)
        cwg = ga[:, d:d + LANES]
        cs = (cwg + pltpu.roll(cwg, LANES - _R_SLOT, 1)
              + pltpu.roll(cwg, LANES - 2 * _R_SLOT, 1))
        hg = [_dot(xg, wg_ref[0, :, h * half:(h + 1) * half]) for h in range(2)]
        hu = [_dot(xg, wu_ref[0, :, h * half:(h + 1) * half]) for h in range(2)]
        y = None
        for h in range(2):
            per_half = EXPERTS_PER_GROUP // 2
            cexp = jnp.concatenate(
                [jnp.broadcast_to(cs[:, e:e + 1], (MOE_CH, EXPERT_FF))
                 for e in range(h * per_half, (h + 1) * per_half)], axis=1)
            hid = ((hg[h] * _sigmoid(hg[h])) * hu[h]) * cexp
            term = _dot(hid.astype(BF16), wd_ref[0, h * half:(h + 1) * half, :])
            y = term if y is None else y + term
        ys_s[pl.ds(pl.multiple_of(row0 + k * MOE_CH, MOE_CH), MOE_CH), :] = y.astype(BF16)
        return carry

    lax.fori_loop(0, (n_rows + MOE_CH - 1) // MOE_CH, sweep, 0)

    @pl.when(grp == N_GROUPS - 1)
    def _fin():
        r_col = lax.broadcasted_iota(jnp.int32, (tm, _YS_ROWS), 1).astype(F32)
        put = jnp.where(tok_s[:, _DEST_LANE:_DEST_LANE + 1] == r_col, 1.0, 0.0).astype(BF16)
        z = DN_ALPHA * x_ref[...] + _dot(put, ys_s[...])
        out_ref[...] = _layer_norm(z, g_ref[...], b_ref[...])


def _moe_call(x1, wr, br, wg, wu, wd, g2, b2):
    t, d = x1.shape
    row = lambda i, g: (i, 0)
    c2 = lambda i, g: (0, 0)
    idx = np.arange(MOE_TM)
    tri = jnp.asarray(idx[None, :] < idx[:, None], dtype=BF16)
    return pl.pallas_call(
        _moe_kernel,
        grid=(t // MOE_TM, N_GROUPS),
        in_specs=[
            pl.BlockSpec((MOE_TM, d), row, pipeline_mode=pl.Buffered(1)),
            pl.BlockSpec((MOE_TM, MOE_TM), c2, pipeline_mode=pl.Buffered(1)),
            pl.BlockSpec(wr.shape, c2),
            pl.BlockSpec(br.shape, c2),
            pl.BlockSpec((1, d, _HID), lambda i, g: (g, 0, 0)),
            pl.BlockSpec((1, d, _HID), lambda i, g: (g, 0, 0)),
            pl.BlockSpec((1, _HID, d), lambda i, g: (g, 0, 0)),
            pl.BlockSpec(g2.shape, c2),
            pl.BlockSpec(b2.shape, c2),
        ],
        out_specs=pl.BlockSpec((MOE_TM, d), row),
        out_shape=jax.ShapeDtypeStruct((t, d), F32),
        scratch_shapes=[
            pltpu.VMEM((MOE_TM, d + LANES), BF16),
            pltpu.VMEM((MOE_TM, LANES), F32),
            pltpu.VMEM((_R_SLOT, MOE_TM), F32),
            pltpu.VMEM((_YS_ROWS, d), BF16),
            pltpu.SMEM((N_GROUPS,), jnp.int32),
            pltpu.SMEM((N_GROUPS,), jnp.int32),
        ],
        compiler_params=pltpu.CompilerParams(
            dimension_semantics=("parallel", "arbitrary"), vmem_limit_bytes=MOE_VMEM_LIMIT),
        name="moe",
    )(x1, tri, wr, br, wg, wu, wd, g2, b2)


def _rope_table(seq):
    half = ROPE_DIM // 2
    inv = ROPE_THETA ** (-jnp.arange(0, ROPE_DIM, 2, dtype=F32) / ROPE_DIM)
    ang = jnp.arange(seq, dtype=F32)[:, None] * inv[None, :]
    cos, sin = jnp.cos(ang), jnp.sin(ang)
    rest = HEAD_DIM - ROPE_DIM
    one = jnp.ones((seq, rest), F32)
    zero = jnp.zeros((seq, rest), F32)
    zh = jnp.zeros((seq, half), F32)
    c_h = jnp.concatenate([cos, cos, one], axis=1)
    dn_h = jnp.concatenate([-sin, zh, zero], axis=1)
    up_h = jnp.concatenate([zh, sin, zero], axis=1)
    rep = LANES // HEAD_DIM
    return jnp.concatenate([jnp.tile(c_h, (1, rep)), jnp.tile(dn_h, (1, rep)), jnp.tile(up_h, (1, rep))],
                           axis=1)


def _expand_cmp_weights(pe, w1, w2):
    g_n = NSA_KV_GROUPS
    w1r = w1.reshape(2, CMP_STRIDE, HEAD_DIM, CMP_HIDDEN)
    eye = jnp.eye(g_n, dtype=w1.dtype)
    w1e = jnp.einsum('pldh,ge->plgdeh', w1r, eye).reshape(2, A_W, g_n * CMP_HIDDEN)
    w2e = jnp.einsum('hd,ge->ghed', w2, eye).reshape(g_n * CMP_HIDDEN, g_n * HEAD_DIM)
    per = pe.reshape(2, CMP_STRIDE, 1, HEAD_DIM)
    pee = jnp.broadcast_to(per, (2, CMP_STRIDE, g_n, HEAD_DIM)).reshape(2, A_W)
    pee = jnp.concatenate([pee, jnp.zeros((6, A_W), pe.dtype)], axis=0)
    return pee.astype(F32), w1e.astype(BF16), w2e.astype(BF16)


def kernel(x, mem, w_in, cmp_pe_k, cmp_w1_k, cmp_w2_k, cmp_pe_v, cmp_w1_v, cmp_w2_v, sgu_ln_g, sgu_ln_b,
           sgu_w_s, sgu_b_s, w_mem_kv, w_br_nsa, w_br_sgu, w_br_mem, w_o, ln1_g, ln1_b, w_router_group,
           b_router_group, w_router_expert, b_router_expert, w_exp_gate, w_exp_up, w_exp_down, ln2_g, ln2_b):
    batch, seq, d = x.shape
    t = batch * seq
    assert w_in.shape[0] == DEPTH == 1
    assert seq % PROJ_TM == 0 and seq // CMP_STRIDE == N_A and seq // SLC_BLOCK == N_SEL

    w = w_in[0]
    offs = np.cumsum([0, Q_W, KV_W, KV_W, KV_W, KV_W, KV_W, KV_W, GATE_W, 2 * SGU_WIDTH, MEM_WIDTH, 3 * d])
    seg = lambda i: w[:, int(offs[i]):int(offs[i + 1])]
    gate_w = jnp.pad(seg(7), ((0, 0), (0, LANES - GATE_W)))
    wp = jnp.concatenate([seg(0), seg(1), seg(3), seg(5), seg(2), seg(4), seg(6), gate_w, seg(8), seg(9)],
                         axis=1).astype(BF16)
    w_merge = seg(10).astype(BF16)
    rope = _rope_table(seq)
    tril = jnp.tril(jnp.ones((SGU_CHUNK, SGU_CHUNK), dtype=bool))
    ws = jnp.where(tril[None], sgu_w_s[0], 0.0).astype(BF16)
    bs = jnp.repeat(sgu_b_s[0].T, SGU_WIDTH // SGU_GROUPS, axis=1)

    x2 = x.reshape(t, d)
    (qpad, kc, ksl, kwn, vc, vsl, vwn, gate, osgu, mq) = _proj_call(
        x2, wp, rope, sgu_ln_g[0][None], sgu_ln_b[0][None], ws, bs, seq)

    pek, w1k, w2k = _expand_cmp_weights(cmp_pe_k[0], cmp_w1_k[0], cmp_w2_k[0])
    pev, w1v, w2v = _expand_cmp_weights(cmp_pe_v[0], cmp_w1_v[0], cmp_w2_v[0])
    kcmp, vcmp = _compress_call(kc.reshape(batch * N_A, A_W), vc.reshape(batch * N_A, A_W),
                                pek, pev, w1k, w2k, w1v, w2v, batch)

    nqb = seq // Q_BLOCK
    n_g = NSA_KV_GROUPS
    vslt = vsl.reshape(batch, seq, n_g, HEAD_DIM).transpose(0, 2, 3, 1)
    vslt = jnp.concatenate([vslt, jnp.ones_like(vslt)], axis=2).reshape(batch * n_g, LANES, seq)
    vwnt = vwn.reshape(batch, nqb, Q_BLOCK, n_g, HEAD_DIM).transpose(0, 3, 1, 4, 2)
    vwnt = jnp.concatenate([vwnt, jnp.ones_like(vwnt)], axis=3).reshape(batch * n_g, nqb, LANES, Q_BLOCK)
    vcmpt = vcmp.reshape(batch, N_A, KV_W).transpose(0, 2, 1).reshape(batch * KV_W, N_A)
    gt = gate.T
    ci = np.arange(N_A)
    sj = np.arange(N_SEL)
    overlap = ((ci[None, :] * CMP_STRIDE + CMP_LEN - 1 >= sj[:, None] * SLC_BLOCK)
               & (ci[None, :] * CMP_STRIDE <= sj[:, None] * SLC_BLOCK + SLC_BLOCK - 1)
               & (ci[None, :] < (seq - CMP_LEN) // CMP_STRIDE + 1))
    ot = jnp.asarray(overlap, dtype=BF16)
    onsa = _nsa_call(qpad, ksl, kwn, vslt, vwnt, kcmp, vcmpt, gt, ot, batch, seq)

    omem = _memattn_call(mq, mem.reshape(batch * mem.shape[1], d), w_mem_kv[0].astype(BF16), batch, seq)

    x1 = _merge_call(x2, onsa, osgu, omem, w_merge, w_br_nsa[0].astype(BF16), w_br_sgu[0].astype(BF16),
                     w_br_mem[0].astype(BF16), w_o[0].astype(BF16), ln1_g[0][None], ln1_b[0][None])

    assert EXPERTS_PER_GROUP == _R_SLOT and N_GROUPS <= _R_SLOT
    n_r = _R_SLOT + N_GROUPS * EXPERTS_PER_GROUP
    wr = jnp.concatenate([jnp.pad(w_router_group[0], ((0, 0), (0, _R_SLOT - N_GROUPS))),
                          w_router_expert[0]], axis=1)
    wr = jnp.pad(wr, ((0, 0), (0, LANES - n_r))).astype(BF16)
    br = jnp.concatenate([jnp.pad(b_router_group[0], (0, _R_SLOT - N_GROUPS)), b_router_expert[0]])
    br = jnp.pad(br, (0, LANES - n_r))[None]
    wg = w_exp_gate[0].transpose(0, 2, 1, 3).reshape(N_GROUPS, d, _HID).astype(BF16)
    wu = w_exp_up[0].transpose(0, 2, 1, 3).reshape(N_GROUPS, d, _HID).astype(BF16)
    wd = w_exp_down[0].reshape(N_GROUPS, _HID, d).astype(BF16)
    out = _moe_call(x1, wr, br, wg, wu, wd, ln2_g[0][None], ln2_b[0][None])
    return out.reshape(batch, seq, d)
```

```python
import functools

import numpy as np
import jax
import jax.numpy as jnp
from jax import lax
from jax.experimental import pallas as pl
from jax.experimental.pallas import tpu as pltpu

NSA_HEADS = 8
NSA_KV_GROUPS = 2
NSA_HPG = NSA_HEADS // NSA_KV_GROUPS
HEAD_DIM = 64
CMP_LEN = 32
CMP_STRIDE = 16
CMP_HIDDEN = 256
SLC_BLOCK = 64
SLC_TOPN = 8
WINDOW = 512
Q_BLOCK = 128
N_BAND = WINDOW // Q_BLOCK
ROPE_THETA = 500000.0
ROPE_DIM = HEAD_DIM // 4
SGU_CHUNK = 128
SGU_GROUPS = 8
SGU_WIDTH = 512
MEM_HEADS = 4
MEM_HEAD_DIM = 128
MEM_WIDTH = MEM_HEADS * MEM_HEAD_DIM
N_GROUPS = 4
EXPERTS_PER_GROUP = 8
EXPERT_FF = 256
DEPTH = 1
DN_ALPHA = (2.0 * DEPTH) ** 0.25
LN_EPS = 1e-5
NEG = -1e30
LOG2E = 1.4426950408889634

LANES = 128
Q_W = NSA_HEADS * HEAD_DIM
KV_W = NSA_KV_GROUPS * HEAD_DIM
GATE_W = NSA_HEADS * 3
VMEM_LIMIT = 56 * 1024 * 1024
MOE_VMEM_LIMIT = 60 * 1024 * 1024

BF16 = jnp.bfloat16
F32 = jnp.float32


def _dot(a, b):
    return jnp.dot(a, b, preferred_element_type=F32)


def _dot_nt(a, b):
    return lax.dot_general(a, b, (((1,), (1,)), ((), ())), preferred_element_type=F32)


def _sigmoid(x):
    return 1.0 / (1.0 + jnp.exp(-x))


def _gelu(x):
    return 0.5 * x * (1.0 + lax.erf(x * (2.0 ** -0.5)))


def _layer_norm(x, g, b):
    mu = jnp.mean(x, axis=-1, keepdims=True)
    xc = x - mu
    var = jnp.mean(xc * xc, axis=-1, keepdims=True)
    return xc * lax.rsqrt(var + LN_EPS) * g + b


PROJ_TM = 512
_ROPE_COLS = Q_W + 3 * KV_W
_V_OFF = _ROPE_COLS
_G_OFF = _V_OFF + 3 * KV_W
_SGU_OFF = _G_OFF + LANES
_MQ_OFF = _SGU_OFF + 2 * SGU_WIDTH
_PROJ_COLS = _MQ_OFF + MEM_WIDTH


def _proj_kernel(x_ref, w_ref, rope_ref, lng_ref, lnb_ref, ws_ref, bs_ref,
                 q_ref, kc_ref, ksl_ref, kwn_ref, vc_ref, vsl_ref, vwn_ref, gate_ref, osgu_ref, mq_ref):
    xb = x_ref[...].astype(BF16)
    lane = lax.broadcasted_iota(jnp.int32, (PROJ_TM, LANES), 1)
    low = lane < HEAD_DIM
    cos = rope_ref[:, 0:LANES]
    s_dn = rope_ref[:, LANES:2 * LANES]
    s_up = rope_ref[:, 2 * LANES:3 * LANES]

    h = _dot(xb, w_ref[:, 0:_ROPE_COLS])
    k_refs = (kc_ref, ksl_ref, kwn_ref)
    for j in range(_ROPE_COLS // LANES):
        blk = h[:, j * LANES:(j + 1) * LANES]
        r = (blk * cos + pltpu.roll(blk, LANES - ROPE_DIM // 2, 1) * s_dn
             + pltpu.roll(blk, ROPE_DIM // 2, 1) * s_up)
        if j < Q_W // LANES:
            r = r * (HEAD_DIM ** -0.5 * LOG2E)
            sw = pltpu.roll(r, HEAD_DIM, 1)
            g = (2 * j) // NSA_HPG
            if g == 0:
                h0 = jnp.where(low, r, 0.0)
                h1 = jnp.where(low, sw, 0.0)
            else:
                h0 = jnp.where(low, 0.0, sw)
                h1 = jnp.where(low, 0.0, r)
            q_ref[:, (2 * j) * LANES:(2 * j + 1) * LANES] = h0.astype(BF16)
            q_ref[:, (2 * j + 1) * LANES:(2 * j + 2) * LANES] = h1.astype(BF16)
        else:
            k_refs[j - Q_W // LANES][...] = r.astype(BF16)

    hv = _dot(xb, w_ref[:, _V_OFF:_G_OFF])
    vc_ref[...] = hv[:, 0:LANES].astype(BF16)
    ones = jnp.ones((V_ROWS - HEAD_DIM, PROJ_TM), BF16)
    for v_ref, col0, blk in ((vsl_ref, LANES, SEL_CHUNK), (vwn_ref, 2 * LANES, Q_BLOCK)):
        vt = hv[:, col0:col0 + LANES].T.astype(BF16)
        for g in range(NSA_KV_GROUPS):
            for b in range(PROJ_TM // blk):
                v_ref[g, b, 0:HEAD_DIM, :] = vt[g * HEAD_DIM:(g + 1) * HEAD_DIM, b * blk:(b + 1) * blk]
                v_ref[g, b, HEAD_DIM:V_ROWS, :] = ones[:, 0:blk]

    gate_ref[...] = _sigmoid(_dot(xb, w_ref[:, _G_OFF:_SGU_OFF])).T

    mq_ref[...] = _dot(xb, w_ref[:, _MQ_OFF:_PROJ_COLS]).astype(BF16)

    z = _gelu(_dot(xb, w_ref[:, _SGU_OFF:_MQ_OFF]))
    u = z[:, 0:SGU_WIDTH]
    v = _layer_norm(z[:, SGU_WIDTH:2 * SGU_WIDTH], lng_ref[...], lnb_ref[...]).astype(BF16)
    lane_c = lax.broadcasted_iota(jnp.int32, (SGU_CHUNK, LANES), 1)
    low_c = lane_c < (SGU_WIDTH // SGU_GROUPS)
    n_ch = PROJ_TM // SGU_CHUNK
    for gp in range(SGU_WIDTH // LANES):
        cols = slice(gp * LANES, (gp + 1) * LANES)
        vcat = jnp.concatenate([v[ci * SGU_CHUNK:(ci + 1) * SGU_CHUNK, cols] for ci in range(n_ch)], axis=1)
        sv0 = _dot(ws_ref[2 * gp], vcat)
        sv1 = _dot(ws_ref[2 * gp + 1], vcat)
        for ci in range(n_ch):
            rows = slice(ci * SGU_CHUNK, (ci + 1) * SGU_CHUNK)
            lanes = slice(ci * LANES, (ci + 1) * LANES)
            sv = jnp.where(low_c, sv0[:, lanes], sv1[:, lanes]) + bs_ref[:, cols]
            osgu_ref[rows, cols] = (u[rows, cols] * sv).astype(BF16)


def _proj_call(x2, wp, rope, lng, lnb, ws, bs, seq):
    t = x2.shape[0]
    d = x2.shape[1]
    nt = t // PROJ_TM
    per_seq = seq // PROJ_TM
    row = lambda i: (i, 0)
    const2 = lambda i: (0, 0)
    assert PROJ_TM % SEL_CHUNK == 0 and PROJ_TM % Q_BLOCK == 0
    n_g = NSA_KV_GROUPS
    batch = t // seq
    vblock = lambda i: (i // per_seq, i % per_seq, 0, 0)

    def v_shape(blk):
        return jax.ShapeDtypeStruct((batch * n_g, seq // blk, V_ROWS, blk), BF16)

    def v_spec(blk):
        return pl.BlockSpec((n_g, PROJ_TM // blk, V_ROWS, blk), vblock)

    out_shapes = (
        jax.ShapeDtypeStruct((t, NSA_HEADS * LANES), BF16),
        *[jax.ShapeDtypeStruct((t, KV_W), BF16) for _ in range(4)],
        v_shape(SEL_CHUNK), v_shape(Q_BLOCK),
        jax.ShapeDtypeStruct((LANES, t), F32),
        jax.ShapeDtypeStruct((t, SGU_WIDTH), BF16),
        jax.ShapeDtypeStruct((t, MEM_WIDTH), BF16),
    )
    out_specs = (
        pl.BlockSpec((PROJ_TM, NSA_HEADS * LANES), row),
        *[pl.BlockSpec((PROJ_TM, KV_W), row) for _ in range(4)],
        v_spec(SEL_CHUNK), v_spec(Q_BLOCK),
        pl.BlockSpec((LANES, PROJ_TM), lambda i: (0, i)),
        pl.BlockSpec((PROJ_TM, SGU_WIDTH), row),
        pl.BlockSpec((PROJ_TM, MEM_WIDTH), row),
    )
    return pl.pallas_call(
        _proj_kernel,
        grid=(nt,),
        in_specs=[
            pl.BlockSpec((PROJ_TM, d), row),
            pl.BlockSpec((d, _PROJ_COLS), const2),
            pl.BlockSpec((PROJ_TM, 3 * LANES), lambda i: (i % per_seq, 0)),
            pl.BlockSpec((1, SGU_WIDTH), const2),
            pl.BlockSpec((1, SGU_WIDTH), const2),
            pl.BlockSpec((SGU_GROUPS, SGU_CHUNK, SGU_CHUNK), lambda i: (0, 0, 0)),
            pl.BlockSpec((SGU_CHUNK, SGU_WIDTH), const2),
        ],
        out_specs=out_specs,
        out_shape=out_shapes,
        compiler_params=pltpu.CompilerParams(
            dimension_semantics=("parallel",), vmem_limit_bytes=VMEM_LIMIT),
        name="proj",
    )(x2, wp, rope, lng, lnb, ws, bs)


N_A = 128
A_W = CMP_STRIDE * KV_W


def _compress_kernel(ka_ref, va_ref, pek_ref, pev_ref, w1k_ref, w2k_ref, w1v_ref, w2v_ref,
                     kcmp_ref, vcmp_ref):
    def one(a_ref, pe_ref, w1_ref, w2_ref, out_ref):
        a = a_ref[...].astype(F32)
        top = (a + pe_ref[0:1, :]).astype(BF16)
        bot = (a + pe_ref[1:2, :]).astype(BF16)
        h1 = _dot(top, w1_ref[0])
        h2 = _dot(bot, w1_ref[1])
        pre = h1 + pltpu.roll(h2, N_A - 1, 0)
        act = _gelu(pre).astype(BF16)
        out_ref[...] = _dot(act, w2_ref[...]).astype(BF16)

    one(ka_ref, pek_ref, w1k_ref, w2k_ref, kcmp_ref)
    one(va_ref, pev_ref, w1v_ref, w2v_ref, vcmp_ref)


def _compress_call(ka, va, pek, pev, w1k, w2k, w1v, w2v, batch):
    row = lambda b: (b, 0)
    c2 = lambda b: (0, 0)
    c3 = lambda b: (0, 0, 0)
    hid2 = NSA_KV_GROUPS * CMP_HIDDEN
    return pl.pallas_call(
        _compress_kernel,
        grid=(batch,),
        in_specs=[
            pl.BlockSpec((N_A, A_W), row),
            pl.BlockSpec((N_A, A_W), row),
            pl.BlockSpec((8, A_W), c2),
            pl.BlockSpec((8, A_W), c2),
            pl.BlockSpec((2, A_W, hid2), c3),
            pl.BlockSpec((hid2, KV_W), c2),
            pl.BlockSpec((2, A_W, hid2), c3),
            pl.BlockSpec((hid2, KV_W), c2),
        ],
        out_specs=(pl.BlockSpec((N_A, KV_W), row), pl.BlockSpec((N_A, KV_W), row)),
        out_shape=(jax.ShapeDtypeStruct((batch * N_A, KV_W), BF16),
                   jax.ShapeDtypeStruct((batch * N_A, KV_W), BF16)),
        compiler_params=pltpu.CompilerParams(
            dimension_semantics=("parallel",), vmem_limit_bytes=VMEM_LIMIT),
        name="compress",
    )(ka, va, pek, pev, w1k, w2k, w1v, w2v)


N_SEL = 32
SEL_CHUNK = 512
SEL_SUB = 128
HQ = NSA_HPG * Q_BLOCK
V_ROWS = HEAD_DIM + 16


def _nsa_kernel(q_ref, ksl_ref, kwn_ref, vslt_ref, vwnt_ref, kcmp_ref, vcmpt_ref, gt_ref,
                ot_ref, o_ref, m_s, acc_s, ocmp_s, owin_s, cap_s, out_s):
    c = pl.program_id(1)
    lane_hq = lax.broadcasted_iota(jnp.int32, (1, HQ), 1)
    pos_hq = c * Q_BLOCK + (lane_hq & (Q_BLOCK - 1))
    pos_q = c * Q_BLOCK + lax.broadcasted_iota(jnp.int32, (1, Q_BLOCK), 1)
    groups = range(NSA_KV_GROUPS)
    qgs = [jnp.concatenate(
        [q_ref[:, (g * NSA_HPG + hh) * LANES:(g * NSA_HPG + hh + 1) * LANES]
         for hh in range(NSA_HPG)], axis=0) for g in groups]

    _nsa_window(c, qgs, kwn_ref, vwnt_ref, owin_s)

    kcmp = kcmp_ref[...]
    s_c = [_dot_nt(kcmp, qgs[g]) for g in groups]
    n_idx = lax.broadcasted_iota(jnp.int32, (N_A, HQ), 0)
    valid_c = (n_idx * CMP_STRIDE + (CMP_LEN - 1)) <= pos_hq
    p_c = []
    for g in groups:
        sm_c = jnp.where(valid_c, s_c[g], NEG)
        m_c = jnp.max(sm_c, axis=0, keepdims=True)
        e_c = jnp.where(valid_c, jnp.exp2(sm_c - m_c), 0.0)
        d_c = jnp.sum(e_c, axis=0, keepdims=True)
        p_c.append(e_c / jnp.where(d_c > 0, d_c, 1.0))
    for g in groups:
        ocmp_s[g] = _dot(vcmpt_ref[g * HEAD_DIM:(g + 1) * HEAD_DIM, :], p_c[g].astype(BF16))

    ot = ot_ref[...]
    imp = []
    for g in groups:
        ps = (p_c[g][:, 0:Q_BLOCK] + p_c[g][:, Q_BLOCK:2 * Q_BLOCK]
              + p_c[g][:, 2 * Q_BLOCK:3 * Q_BLOCK] + p_c[g][:, 3 * Q_BLOCK:4 * Q_BLOCK])
        p_hi = ps.astype(BF16)
        r1 = ps - p_hi.astype(F32)
        p_lo = r1.astype(BF16)
        p_lo2 = (r1 - p_lo.astype(F32)).astype(BF16)
        imp.append(_dot(ot, p_hi) + _dot(ot, p_lo) + _dot(ot, p_lo2))

    j_idx = lax.broadcasted_iota(jnp.int32, (N_SEL, Q_BLOCK), 0)
    cur = pos_q // SLC_BLOCK
    future = j_idx > cur
    forced = (j_idx == 0) | (j_idx == cur) | (j_idx == cur - 1)
    imp = [jnp.where(future, NEG, jnp.where(forced, -NEG, imp[g])) for g in groups]
    rank = [jnp.zeros((N_SEL, Q_BLOCK), F32) for g in groups]
    for i in range(N_SEL):
        for g in groups:
            row = imp[g][i:i + 1, :]
            beats = (row > imp[g]) | ((row == imp[g]) & (j_idx > i))
            rank[g] = rank[g] + jnp.where(beats, 1.0, 0.0)
    for g in groups:
        cap_s[g] = jnp.where(rank[g] < float(SLC_TOPN), -NEG, NEG)
        m_s[g] = jnp.full((1, HQ), NEG, F32)
        acc_s[g] = jnp.zeros((V_ROWS, HQ), F32)

    blocks_per_sub = SEL_SUB // SLC_BLOCK
    subs_per_chunk = SEL_CHUNK // SEL_SUB

    def sel_chunk(kc):
        spans = [slice(kc * SEL_CHUNK + a * SEL_SUB, kc * SEL_CHUNK + (a + 1) * SEL_SUB)
                 for a in range(subs_per_chunk)]
        s = [[_dot_nt(ksl_ref[ks, :], qgs[g]) for g in groups] for ks in spans]
        for a, ks in enumerate(spans):
            kpos = ks.start + lax.broadcasted_iota(jnp.int32, (SEL_SUB, Q_BLOCK), 0)
            causal = kpos <= pos_q
            for g in groups:
                j0 = ks.start // SLC_BLOCK
                cap = jnp.concatenate(
                    [jnp.broadcast_to(cap_s[g, j:j + 1, :], (SLC_BLOCK, Q_BLOCK))
                     for j in range(j0, j0 + blocks_per_sub)], axis=0)
                cap = jnp.where(causal, cap, NEG)
                sm = jnp.minimum(s[a][g], jnp.concatenate([cap] * NSA_HPG, axis=1))
                m_old = m_s[g]
                m_new = jnp.maximum(m_old, jnp.max(sm, axis=0, keepdims=True))
                alpha = jnp.exp2(m_old - m_new)
                e = jnp.exp2(sm - m_new)
                v_t = vslt_ref[g, kc, :, a * SEL_SUB:(a + 1) * SEL_SUB]
                acc_s[g] = alpha * acc_s[g] + _dot(v_t, e.astype(BF16))
                m_s[g] = m_new

    sel_chunk(0)
    for kc in range(1, ksl_ref.shape[0] // SEL_CHUNK):
        pl.when(kc * (SEL_CHUNK // Q_BLOCK) <= c)(functools.partial(sel_chunk, kc))

    for g in groups:
        acc = acc_s[g]
        o_sel = acc[0:HEAD_DIM, :] * (1.0 / acc[HEAD_DIM:HEAD_DIM + 1, :])

        def gate_row(br):
            return jnp.concatenate(
                [gt_ref[(g * NSA_HPG + hh) * 3 + br:(g * NSA_HPG + hh) * 3 + br + 1, :]
                 for hh in range(NSA_HPG)], axis=1)
        o_t = gate_row(0) * ocmp_s[g] + gate_row(1) * o_sel + gate_row(2) * owin_s[g]
        for hh in range(NSA_HPG):
            h = g * NSA_HPG + hh
            out_s[h * HEAD_DIM:(h + 1) * HEAD_DIM, :] = o_t[:, hh * Q_BLOCK:(hh + 1) * Q_BLOCK]

    o_ref[...] = out_s[...].T.astype(BF16)


def _nsa_window(c, qgs, kwn_ref, vwnt_ref, owin_s):
    groups = range(NSA_KV_GROUPS)
    q_i = lax.broadcasted_iota(jnp.int32, (Q_BLOCK, Q_BLOCK), 1)
    k_i = lax.broadcasted_iota(jnp.int32, (Q_BLOCK, Q_BLOCK), 0)
    blks, caps = [], []
    for i in range(N_BAND + 1):
        blk = c - N_BAND + i
        blks.append(jnp.maximum(blk, 0))
        off = (N_BAND - i) * Q_BLOCK
        if off - (Q_BLOCK - 1) >= 0 and off + (Q_BLOCK - 1) < WINDOW:
            caps.append(jnp.where(blk >= 0, -NEG, NEG))
        else:
            diff = off + q_i - k_i
            ok = (diff >= 0) & (diff < WINDOW) & (blk >= 0)
            caps.append(jnp.concatenate([jnp.where(ok, -NEG, NEG)] * NSA_HPG, axis=1))
    k_blocks = [kwn_ref[pl.ds(pl.multiple_of(blks[i] * Q_BLOCK, Q_BLOCK), Q_BLOCK), :]
                for i in range(N_BAND + 1)]
    s_w = [[_dot_nt(k_blocks[i], qgs[g]) for i in range(N_BAND + 1)] for g in groups]
    for g in groups:
        sm_w = jnp.concatenate([jnp.minimum(s_w[g][i], caps[i]) for i in range(N_BAND + 1)], axis=0)
        m_w = jnp.max(sm_w, axis=0, keepdims=True)
        e_wb = jnp.exp2(sm_w - m_w).astype(BF16)
        o_win = _dot(vwnt_ref[g, blks[0]], e_wb[0:Q_BLOCK, :])
        for i in range(1, N_BAND + 1):
            o_win = o_win + _dot(vwnt_ref[g, blks[i]], e_wb[i * Q_BLOCK:(i + 1) * Q_BLOCK, :])
        owin_s[g] = o_win[0:HEAD_DIM, :] * (1.0 / o_win[HEAD_DIM:HEAD_DIM + 1, :])


def _nsa_call(qpad, ksl, kwn, vslt, vwnt, kcmp, vcmpt, gt, ot, batch, seq):
    nqb = seq // Q_BLOCK
    n_g = NSA_KV_GROUPS
    qrow = lambda b, c: (b * nqb + c, 0)
    brow = lambda b, c: (b, 0)
    c2 = lambda b, c: (0, 0)
    return pl.pallas_call(
        _nsa_kernel,
        grid=(batch, nqb),
        in_specs=[
            pl.BlockSpec((Q_BLOCK, NSA_HEADS * LANES), qrow),
            pl.BlockSpec((seq, KV_W), brow),
            pl.BlockSpec((seq, KV_W), brow),
            pl.BlockSpec((n_g, seq // SEL_CHUNK, V_ROWS, SEL_CHUNK), lambda b, c: (b, 0, 0, 0)),
            pl.BlockSpec((n_g, nqb, V_ROWS, Q_BLOCK), lambda b, c: (b, 0, 0, 0)),
            pl.BlockSpec((N_A, KV_W), brow),
            pl.BlockSpec((KV_W, N_A), brow),
            pl.BlockSpec((LANES, Q_BLOCK), lambda b, c: (0, b * nqb + c)),
            pl.BlockSpec((N_SEL, N_A), c2),
        ],
        out_specs=pl.BlockSpec((Q_BLOCK, Q_W), qrow),
        out_shape=jax.ShapeDtypeStruct((batch * seq, Q_W), BF16),
        scratch_shapes=[
            pltpu.VMEM((n_g, 1, HQ), F32),
            pltpu.VMEM((n_g, V_ROWS, HQ), F32),
            pltpu.VMEM((n_g, HEAD_DIM, HQ), F32),
            pltpu.VMEM((n_g, HEAD_DIM, HQ), F32),
            pltpu.VMEM((n_g, N_SEL, Q_BLOCK), F32),
            pltpu.VMEM((Q_W, Q_BLOCK), F32),
        ],
        compiler_params=pltpu.CompilerParams(
            dimension_semantics=("parallel", "arbitrary"), vmem_limit_bytes=VMEM_LIMIT),
        name="nsa",
    )(qpad, ksl, kwn, vslt, vwnt, kcmp, vcmpt, gt, ot)


MEM_TM = 512


def _memattn_kernel(mq_ref, mem_ref, wkv_ref, o_ref, kv_s):
    @pl.when(pl.program_id(1) == 0)
    def _():
        kv_s[...] = _dot(mem_ref[...].astype(BF16), wkv_ref[...]).astype(BF16)

    for h in range(MEM_HEADS):
        cols = slice(h * MEM_HEAD_DIM, (h + 1) * MEM_HEAD_DIM)
        s = _dot_nt(mq_ref[:, cols], kv_s[:, cols]) * (MEM_HEAD_DIM ** -0.5)
        m = jnp.max(s, axis=-1, keepdims=True)
        e = jnp.exp(s - m)
        p = e / jnp.sum(e, axis=-1, keepdims=True)
        vcols = slice(MEM_WIDTH + h * MEM_HEAD_DIM, MEM_WIDTH + (h + 1) * MEM_HEAD_DIM)
        o_ref[:, cols] = _dot(p.astype(BF16), kv_s[:, vcols]).astype(BF16)


def _memattn_call(mq, mem2, wkv, batch, seq):
    m_len = mem2.shape[0] // batch
    d = mem2.shape[1]
    per = seq // MEM_TM
    return pl.pallas_call(
        _memattn_kernel,
        grid=(batch, per),
        in_specs=[
            pl.BlockSpec((MEM_TM, MEM_WIDTH), lambda b, i: (b * per + i, 0)),
            pl.BlockSpec((m_len, d), lambda b, i: (b, 0)),
            pl.BlockSpec((d, 2 * MEM_WIDTH), lambda b, i: (0, 0)),
        ],
        out_specs=pl.BlockSpec((MEM_TM, MEM_WIDTH), lambda b, i: (b * per + i, 0)),
        out_shape=jax.ShapeDtypeStruct((batch * seq, MEM_WIDTH), BF16),
        scratch_shapes=[pltpu.VMEM((m_len, 2 * MEM_WIDTH), BF16)],
        compiler_params=pltpu.CompilerParams(
            dimension_semantics=("parallel", "arbitrary"), vmem_limit_bytes=VMEM_LIMIT),
        name="memattn",
    )(mq, mem2, wkv)


MERGE_TM = 512


def _merge_kernel(x_ref, onsa_ref, osgu_ref, omem_ref, wg_ref, wbn_ref, wbs_ref, wbm_ref, wo_ref,
                  g_ref, b_ref, out_ref):
    x = x_ref[...]
    xb = x.astype(BF16)
    d = x.shape[1]
    y = None
    for br, (o_r, w_r) in enumerate(((onsa_ref, wbn_ref), (osgu_ref, wbs_ref), (omem_ref, wbm_ref))):
        gate = _sigmoid(_dot(xb, wg_ref[:, br * d:(br + 1) * d]))
        term = gate * _dot(o_r[...], w_r[...])
        y = term if y is None else y + term
    z = DN_ALPHA * x + _dot(y.astype(BF16), wo_ref[...])
    out_ref[...] = _layer_norm(z, g_ref[...], b_ref[...])


def _merge_call(x2, onsa, osgu, omem, wg, wbn, wbs, wbm, wo, g1, b1):
    t, d = x2.shape
    row = lambda i: (i, 0)
    c2 = lambda i: (0, 0)
    full = lambda a: pl.BlockSpec(a.shape, c2)
    return pl.pallas_call(
        _merge_kernel,
        grid=(t // MERGE_TM,),
        in_specs=[
            pl.BlockSpec((MERGE_TM, d), row),
            pl.BlockSpec((MERGE_TM, onsa.shape[1]), row),
            pl.BlockSpec((MERGE_TM, osgu.shape[1]), row),
            pl.BlockSpec((MERGE_TM, omem.shape[1]), row),
            full(wg), full(wbn), full(wbs), full(wbm), full(wo), full(g1), full(b1),
        ],
        out_specs=pl.BlockSpec((MERGE_TM, d), row),
        out_shape=jax.ShapeDtypeStruct((t, d), F32),
        compiler_params=pltpu.CompilerParams(
            dimension_semantics=("parallel",), vmem_limit_bytes=VMEM_LIMIT),
        name="merge",
    )(x2, onsa, osgu, omem, wg, wbn, wbs, wbm, wo, g1, b1)


MOE_TM = 1024
MOE_CH = 128
_YS_ROWS = MOE_TM + N_GROUPS * MOE_CH
_R_SLOT = 8
_DEST_LANE = 3 * _R_SLOT
_HID = EXPERTS_PER_GROUP * EXPERT_FF


def _moe_kernel(x_ref, tri_ref, wr_ref, br_ref, wg_ref, wu_ref, wd_ref, g_ref, b_ref, out_ref,
                xa_s, tok_s, keyr_s, ys_s, cnt_s, base_s):
    grp = pl.program_id(1)
    tm = x_ref.shape[0]
    d = x_ref.shape[1]

    @pl.when(grp == 0)
    def _route():
        xb = x_ref[...].astype(BF16)
        xa_s[:, 0:d] = xb
        logits = _dot(xb, wr_ref[...]) + br_ref[...]
        lt = logits.T
        row = lax.broadcasted_iota(jnp.int32, (_R_SLOT, tm), 0)
        gl = jnp.where(row < N_GROUPS, lt[0:_R_SLOT], NEG)
        gmax = jnp.max(gl, axis=0, keepdims=True)
        gidx = jnp.min(jnp.where(gl == gmax, row, _R_SLOT), axis=0, keepdims=True)
        gprob = 1.0 / jnp.sum(jnp.exp(gl - gmax), axis=0, keepdims=True)
        el = lt[_R_SLOT:2 * _R_SLOT]
        for k in range(1, N_GROUPS):
            el = jnp.where(gidx == k, lt[(k + 1) * _R_SLOT:(k + 2) * _R_SLOT], el)
        ee = jnp.exp(el - jnp.max(el, axis=0, keepdims=True))
        ep = ee / jnp.sum(ee, axis=0, keepdims=True)
        t1 = jnp.max(ep, axis=0, keepdims=True)
        i1 = jnp.min(jnp.where(ep == t1, row, _R_SLOT), axis=0, keepdims=True)
        rest = row != i1
        t2 = jnp.max(jnp.where(rest, ep, -1.0), axis=0, keepdims=True)
        i2 = jnp.min(jnp.where(rest & (ep == t2), row, _R_SLOT), axis=0, keepdims=True)
        den = t1 + t2
        cwf = (jnp.where(row == i1, t1 / den, 0.0) + jnp.where(row == i2, t2 / den, 0.0)) * gprob
        hi = cwf.astype(BF16).astype(F32)
        r1 = cwf - hi
        lo = r1.astype(BF16).astype(F32)
        lo2 = (r1 - lo).astype(BF16).astype(F32)

        onehot = jnp.where(row == gidx, 1.0, 0.0)
        rank = _dot_nt(onehot.astype(BF16), tri_ref[...])
        keyr_s[...] = jnp.where(onehot > 0.5, rank, -1.0)
        base = jnp.int32(0)
        basev = jnp.zeros((1, tm), F32)
        for k in range(N_GROUPS):
            n_k = jnp.sum(onehot[k:k + 1, :]).astype(jnp.int32)
            cnt_s[k] = n_k
            base_s[k] = base
            basev = jnp.where(gidx == k, base.astype(F32), basev)
            base = base + ((n_k + MOE_CH - 1) // MOE_CH) * MOE_CH
        dest = jnp.sum(rank * onehot, axis=0, keepdims=True) + basev
        tok = jnp.concatenate(
            [hi, lo, lo2, jnp.broadcast_to(dest, (_R_SLOT, tm)),
             jnp.zeros((LANES - 4 * _R_SLOT, tm), F32)], axis=0).T
        tok_s[...] = tok
        lane = lax.broadcasted_iota(jnp.int32, tok.shape, 1)
        xa_s[:, d:d + LANES] = jnp.where(lane < _DEST_LANE, tok, 0.0).astype(BF16)
        ys_s[...] = jnp.zeros(ys_s.shape, BF16)

    n_rows = cnt_s[grp]
    row0 = base_s[grp]
    keyr = keyr_s[pl.ds(grp, 1), :]
    half = _HID // 2

    def sweep(k, carry):
        ch = MOE_CH
        r_row = (lax.broadcasted_iota(jnp.int32, (ch, tm), 0) + k * MOE_CH).astype(F32)
        pick = jnp.where(keyr == r_row, 1.0, 0.0).astype(BF16)
        ga = _dot(pick, xa_s[...])
        xg = ga[:, 0:d].astype(BF16)
        cwg = ga[:, d:d + LANES]
        cs = (cwg + pltpu.roll(cwg, LANES - _R_SLOT, 1)
              + pltpu.roll(cwg, LANES - 2 * _R_SLOT, 1))
        hg = [_dot(xg, wg_ref[0, :, h * half:(h + 1) * half]) for h in range(2)]
        hu = [_dot(xg, wu_ref[0, :, h * half:(h + 1) * half]) for h in range(2)]
        y = None
        for h in range(2):
            per_half = EXPERTS_PER_GROUP // 2
            cexp = jnp.concatenate(
                [jnp.broadcast_to(cs[:, e:e + 1], (ch, EXPERT_FF))
                 for e in range(h * per_half, (h + 1) * per_half)], axis=1)
            hid = ((hg[h] * _sigmoid(hg[h])) * hu[h]) * cexp
            term = _dot(hid.astype(BF16), wd_ref[0, h * half:(h + 1) * half, :])
            y = term if y is None else y + term
        ys_s[pl.ds(pl.multiple_of(row0 + k * MOE_CH, MOE_CH), ch), :] = y.astype(BF16)
        return carry

    lax.fori_loop(0, (n_rows + MOE_CH - 1) // MOE_CH, sweep, 0)

    @pl.when(grp == N_GROUPS - 1)
    def _fin():
        r_col = lax.broadcasted_iota(jnp.int32, (tm, _YS_ROWS), 1).astype(F32)
        put = jnp.where(tok_s[:, _DEST_LANE:_DEST_LANE + 1] == r_col, 1.0, 0.0).astype(BF16)
        z = DN_ALPHA * x_ref[...] + _dot(put, ys_s[...])
        out_ref[...] = _layer_norm(z, g_ref[...], b_ref[...])


def _moe_call(x1, wr, br, wg, wu, wd, g2, b2):
    t, d = x1.shape
    row = lambda i, g: (i, 0)
    c2 = lambda i, g: (0, 0)
    idx = np.arange(MOE_TM)
    tri = jnp.asarray(idx[None, :] < idx[:, None], dtype=BF16)
    return pl.pallas_call(
        _moe_kernel,
        grid=(t // MOE_TM, N_GROUPS),
        in_specs=[
            pl.BlockSpec((MOE_TM, d), row, pipeline_mode=pl.Buffered(1)),
            pl.BlockSpec((MOE_TM, MOE_TM), c2, pipeline_mode=pl.Buffered(1)),
            pl.BlockSpec(wr.shape, c2),
            pl.BlockSpec(br.shape, c2),
            pl.BlockSpec((1, d, _HID), lambda i, g: (g, 0, 0)),
            pl.BlockSpec((1, d, _HID), lambda i, g: (g, 0, 0)),
            pl.BlockSpec((1, _HID, d), lambda i, g: (g, 0, 0)),
            pl.BlockSpec(g2.shape, c2),
            pl.BlockSpec(b2.shape, c2),
        ],
        out_specs=pl.BlockSpec((MOE_TM, d), row),
        out_shape=jax.ShapeDtypeStruct((t, d), F32),
        scratch_shapes=[
            pltpu.VMEM((MOE_TM, d + LANES), BF16),
            pltpu.VMEM((MOE_TM, LANES), F32),
            pltpu.VMEM((_R_SLOT, MOE_TM), F32),
            pltpu.VMEM((_YS_ROWS, d), BF16),
            pltpu.SMEM((N_GROUPS,), jnp.int32),
            pltpu.SMEM((N_GROUPS,), jnp.int32),
        ],
        compiler_params=pltpu.CompilerParams(
            dimension_semantics=("parallel", "arbitrary"), vmem_limit_bytes=MOE_VMEM_LIMIT),
        name="moe",
    )(x1, tri, wr, br, wg, wu, wd, g2, b2)


def _rope_table(seq):
    half = ROPE_DIM // 2
    inv = ROPE_THETA ** (-jnp.arange(0, ROPE_DIM, 2, dtype=F32) / ROPE_DIM)
    ang = jnp.arange(seq, dtype=F32)[:, None] * inv[None, :]
    cos, sin = jnp.cos(ang), jnp.sin(ang)
    rest = HEAD_DIM - ROPE_DIM
    one = jnp.ones((seq, rest), F32)
    zero = jnp.zeros((seq, rest), F32)
    zh = jnp.zeros((seq, half), F32)
    c_h = jnp.concatenate([cos, cos, one], axis=1)
    dn_h = jnp.concatenate([-sin, zh, zero], axis=1)
    up_h = jnp.concatenate([zh, sin, zero], axis=1)
    rep = LANES // HEAD_DIM
    return jnp.concatenate([jnp.tile(c_h, (1, rep)), jnp.tile(dn_h, (1, rep)), jnp.tile(up_h, (1, rep))],
                           axis=1)


def _expand_cmp_weights(pe, w1, w2):
    g_n = NSA_KV_GROUPS
    w1r = w1.reshape(2, CMP_STRIDE, HEAD_DIM, CMP_HIDDEN)
    eye = jnp.eye(g_n, dtype=w1.dtype)
    w1e = jnp.einsum('pldh,ge->plgdeh', w1r, eye).reshape(2, A_W, g_n * CMP_HIDDEN)
    w2e = jnp.einsum('hd,ge->ghed', w2, eye).reshape(g_n * CMP_HIDDEN, g_n * HEAD_DIM)
    per = pe.reshape(2, CMP_STRIDE, 1, HEAD_DIM)
    pee = jnp.broadcast_to(per, (2, CMP_STRIDE, g_n, HEAD_DIM)).reshape(2, A_W)
    pee = jnp.concatenate([pee, jnp.zeros((6, A_W), pe.dtype)], axis=0)
    return pee.astype(F32), w1e.astype(BF16), w2e.astype(BF16)


def kernel(x, mem, w_in, cmp_pe_k, cmp_w1_k, cmp_w2_k, cmp_pe_v, cmp_w1_v, cmp_w2_v, sgu_ln_g, sgu_ln_b,
           sgu_w_s, sgu_b_s, w_mem_kv, w_br_nsa, w_br_sgu, w_br_mem, w_o, ln1_g, ln1_b, w_router_group,
           b_router_group, w_router_expert, b_router_expert, w_exp_gate, w_exp_up, w_exp_down, ln2_g, ln2_b):
    batch, seq, d = x.shape
    t = batch * seq
    assert w_in.shape[0] == DEPTH == 1
    assert seq % PROJ_TM == 0 and seq // CMP_STRIDE == N_A and seq // SLC_BLOCK == N_SEL

    w = w_in[0]
    offs = np.cumsum([0, Q_W, KV_W, KV_W, KV_W, KV_W, KV_W, KV_W, GATE_W, 2 * SGU_WIDTH, MEM_WIDTH, 3 * d])
    seg = lambda i: w[:, int(offs[i]):int(offs[i + 1])]
    gate_w = jnp.pad(seg(7), ((0, 0), (0, LANES - GATE_W)))
    wp = jnp.concatenate([seg(0), seg(1), seg(3), seg(5), seg(2), seg(4), seg(6), gate_w, seg(8), seg(9)],
                         axis=1).astype(BF16)
    w_merge = seg(10).astype(BF16)
    rope = _rope_table(seq)
    tril = jnp.tril(jnp.ones((SGU_CHUNK, SGU_CHUNK), dtype=bool))
    ws = jnp.where(tril[None], sgu_w_s[0], 0.0).astype(BF16)
    bs = jnp.repeat(sgu_b_s[0].T, SGU_WIDTH // SGU_GROUPS, axis=1)

    x2 = x.reshape(t, d)
    (qpad, kc, ksl, kwn, vc, vslt, vwnt, gt, osgu, mq) = _proj_call(
        x2, wp, rope, sgu_ln_g[0][None], sgu_ln_b[0][None], ws, bs, seq)

    pek, w1k, w2k = _expand_cmp_weights(cmp_pe_k[0], cmp_w1_k[0], cmp_w2_k[0])
    pev, w1v, w2v = _expand_cmp_weights(cmp_pe_v[0], cmp_w1_v[0], cmp_w2_v[0])
    kcmp, vcmp = _compress_call(kc.reshape(batch * N_A, A_W), vc.reshape(batch * N_A, A_W),
                                pek, pev, w1k, w2k, w1v, w2v, batch)

    nqb = seq // Q_BLOCK
    n_g = NSA_KV_GROUPS
    vcmpt = vcmp.reshape(batch, N_A, KV_W).transpose(0, 2, 1).reshape(batch * KV_W, N_A)
    ci = np.arange(N_A)
    sj = np.arange(N_SEL)
    overlap = ((ci[None, :] * CMP_STRIDE + CMP_LEN - 1 >= sj[:, None] * SLC_BLOCK)
               & (ci[None, :] * CMP_STRIDE <= sj[:, None] * SLC_BLOCK + SLC_BLOCK - 1)
               & (ci[None, :] < (seq - CMP_LEN) // CMP_STRIDE + 1))
    ot = jnp.asarray(overlap, dtype=BF16)
    onsa = _nsa_call(qpad, ksl, kwn, vslt, vwnt, kcmp, vcmpt, gt, ot, batch, seq)

    omem = _memattn_call(mq, mem.reshape(batch * mem.shape[1], d), w_mem_kv[0].astype(BF16), batch, seq)

    x1 = _merge_call(x2, onsa, osgu, omem, w_merge, w_br_nsa[0].astype(BF16), w_br_sgu[0].astype(BF16),
                     w_br_mem[0].astype(BF16), w_o[0].astype(BF16), ln1_g[0][None], ln1_b[0][None])

    assert EXPERTS_PER_GROUP == _R_SLOT and N_GROUPS <= _R_SLOT
    n_r = _R_SLOT + N_GROUPS * EXPERTS_PER_GROUP
    wr = jnp.concatenate([jnp.pad(w_router_group[0], ((0, 0), (0, _R_SLOT - N_GROUPS))),
                          w_router_expert[0]], axis=1)
    wr = jnp.pad(wr, ((0, 0), (0, LANES - n_r))).astype(BF16)
    br = jnp.concatenate([jnp.pad(b_router_group[0], (0, _R_SLOT - N_GROUPS)), b_router_expert[0]])
    br = jnp.pad(br, (0, LANES - n_r))[None]
    wg = w_exp_gate[0].transpose(0, 2, 1, 3).reshape(N_GROUPS, d, _HID).astype(BF16)
    wu = w_exp_up[0].transpose(0, 2, 1, 3).reshape(N_GROUPS, d, _HID).astype(BF16)
    wd = w_exp_down[0].reshape(N_GROUPS, _HID, d).astype(BF16)
    out = _moe_call(x1, wr, br, wg, wu, wd, ln2_g[0][None], ln2_b[0][None])
    return out.reshape(batch, seq, d)
```

```python
import functools

import numpy as np
import jax
import jax.numpy as jnp
from jax import lax
from jax.experimental import pallas as pl
from jax.experimental.pallas import tpu as pltpu

NSA_HEADS = 8
NSA_KV_GROUPS = 2
NSA_HPG = NSA_HEADS // NSA_KV_GROUPS
HEAD_DIM = 64
CMP_LEN = 32
CMP_STRIDE = 16
CMP_HIDDEN = 256
SLC_BLOCK = 64
SLC_TOPN = 8
WINDOW = 512
Q_BLOCK = 128
N_BAND = WINDOW // Q_BLOCK
ROPE_THETA = 500000.0
ROPE_DIM = HEAD_DIM // 4
SGU_CHUNK = 128
SGU_GROUPS = 8
SGU_WIDTH = 512
MEM_HEADS = 4
MEM_HEAD_DIM = 128
MEM_WIDTH = MEM_HEADS * MEM_HEAD_DIM
N_GROUPS = 4
EXPERTS_PER_GROUP = 8
EXPERT_FF = 256
DEPTH = 1
DN_ALPHA = (2.0 * DEPTH) ** 0.25
LN_EPS = 1e-5
NEG = -1e30
LOG2E = 1.4426950408889634

LANES = 128
Q_W = NSA_HEADS * HEAD_DIM
KV_W = NSA_KV_GROUPS * HEAD_DIM
GATE_W = NSA_HEADS * 3
VMEM_LIMIT = 56 * 1024 * 1024
MOE_VMEM_LIMIT = 60 * 1024 * 1024

BF16 = jnp.bfloat16
F32 = jnp.float32


def _dot(a, b):
    return jnp.dot(a, b, preferred_element_type=F32)


def _dot_nt(a, b):
    return lax.dot_general(a, b, (((1,), (1,)), ((), ())), preferred_element_type=F32)


def _sigmoid(x):
    return 1.0 / (1.0 + jnp.exp(-x))


def _gelu(x):
    return 0.5 * x * (1.0 + lax.erf(x * (2.0 ** -0.5)))


def _layer_norm(x, g, b):
    mu = jnp.mean(x, axis=-1, keepdims=True)
    xc = x - mu
    var = jnp.mean(xc * xc, axis=-1, keepdims=True)
    return xc * lax.rsqrt(var + LN_EPS) * g + b


PROJ_TM = 512
_ROPE_COLS = Q_W + 3 * KV_W
_V_OFF = _ROPE_COLS
_G_OFF = _V_OFF + 3 * KV_W
_SGU_OFF = _G_OFF + LANES
_MQ_OFF = _SGU_OFF + 2 * SGU_WIDTH
_PROJ_COLS = _MQ_OFF + MEM_WIDTH


def _proj_kernel(x_ref, w_ref, rope_ref, lng_ref, lnb_ref, ws_ref, bs_ref,
                 q_ref, kc_ref, ksl_ref, kwn_ref, vc_ref, vsl_ref, vwn_ref, gate_ref, osgu_ref, mq_ref):
    xb = x_ref[...].astype(BF16)
    lane = lax.broadcasted_iota(jnp.int32, (PROJ_TM, LANES), 1)
    low = lane < HEAD_DIM
    cos = rope_ref[:, 0:LANES]
    s_dn = rope_ref[:, LANES:2 * LANES]
    s_up = rope_ref[:, 2 * LANES:3 * LANES]

    h = _dot(xb, w_ref[:, 0:_ROPE_COLS])
    k_refs = (kc_ref, ksl_ref, kwn_ref)
    for j in range(_ROPE_COLS // LANES):
        blk = h[:, j * LANES:(j + 1) * LANES]
        r = (blk * cos + pltpu.roll(blk, LANES - ROPE_DIM // 2, 1) * s_dn
             + pltpu.roll(blk, ROPE_DIM // 2, 1) * s_up)
        if j < Q_W // LANES:
            r = r * (HEAD_DIM ** -0.5 * LOG2E)
            sw = pltpu.roll(r, HEAD_DIM, 1)
            g = (2 * j) // NSA_HPG
            if g == 0:
                h0 = jnp.where(low, r, 0.0)
                h1 = jnp.where(low, sw, 0.0)
            else:
                h0 = jnp.where(low, 0.0, sw)
                h1 = jnp.where(low, 0.0, r)
            q_ref[:, (2 * j) * LANES:(2 * j + 1) * LANES] = h0.astype(BF16)
            q_ref[:, (2 * j + 1) * LANES:(2 * j + 2) * LANES] = h1.astype(BF16)
        else:
            k_refs[j - Q_W // LANES][...] = r.astype(BF16)

    hv = _dot(xb, w_ref[:, _V_OFF:_G_OFF])
    vc_ref[...] = hv[:, 0:LANES].astype(BF16)
    ones = jnp.ones((V_ROWS - HEAD_DIM, PROJ_TM), BF16)
    for v_ref, col0, blk in ((vsl_ref, LANES, SEL_CHUNK), (vwn_ref, 2 * LANES, Q_BLOCK)):
        vt = hv[:, col0:col0 + LANES].T.astype(BF16)
        for g in range(NSA_KV_GROUPS):
            for b in range(PROJ_TM // blk):
                v_ref[g, b, 0:HEAD_DIM, :] = vt[g * HEAD_DIM:(g + 1) * HEAD_DIM, b * blk:(b + 1) * blk]
                v_ref[g, b, HEAD_DIM:V_ROWS, :] = ones[:, 0:blk]

    gate_ref[...] = _sigmoid(_dot(xb, w_ref[:, _G_OFF:_SGU_OFF])).T

    mq_ref[...] = _dot(xb, w_ref[:, _MQ_OFF:_PROJ_COLS]).astype(BF16)

    z = _gelu(_dot(xb, w_ref[:, _SGU_OFF:_MQ_OFF]))
    u = z[:, 0:SGU_WIDTH]
    v = _layer_norm(z[:, SGU_WIDTH:2 * SGU_WIDTH], lng_ref[...], lnb_ref[...]).astype(BF16)
    lane_c = lax.broadcasted_iota(jnp.int32, (SGU_CHUNK, LANES), 1)
    low_c = lane_c < (SGU_WIDTH // SGU_GROUPS)
    n_ch = PROJ_TM // SGU_CHUNK
    for gp in range(SGU_WIDTH // LANES):
        cols = slice(gp * LANES, (gp + 1) * LANES)
        vcat = jnp.concatenate([v[ci * SGU_CHUNK:(ci + 1) * SGU_CHUNK, cols] for ci in range(n_ch)], axis=1)
        sv0 = _dot(ws_ref[2 * gp], vcat)
        sv1 = _dot(ws_ref[2 * gp + 1], vcat)
        for ci in range(n_ch):
            rows = slice(ci * SGU_CHUNK, (ci + 1) * SGU_CHUNK)
            lanes = slice(ci * LANES, (ci + 1) * LANES)
            sv = jnp.where(low_c, sv0[:, lanes], sv1[:, lanes]) + bs_ref[:, cols]
            osgu_ref[rows, cols] = (u[rows, cols] * sv).astype(BF16)


def _proj_call(x2, wp, rope, lng, lnb, ws, bs, seq):
    t = x2.shape[0]
    d = x2.shape[1]
    nt = t // PROJ_TM
    per_seq = seq // PROJ_TM
    row = lambda i: (i, 0)
    const2 = lambda i: (0, 0)
    assert PROJ_TM % SEL_CHUNK == 0 and PROJ_TM % Q_BLOCK == 0
    n_g = NSA_KV_GROUPS
    batch = t // seq
    vblock = lambda i: (i // per_seq, i % per_seq, 0, 0)

    def v_shape(blk):
        return jax.ShapeDtypeStruct((batch * n_g, seq // blk, V_ROWS, blk), BF16)

    def v_spec(blk):
        return pl.BlockSpec((n_g, PROJ_TM // blk, V_ROWS, blk), vblock)

    out_shapes = (
        jax.ShapeDtypeStruct((t, NSA_HEADS * LANES), BF16),
        *[jax.ShapeDtypeStruct((t, KV_W), BF16) for _ in range(4)],
        v_shape(SEL_CHUNK), v_shape(Q_BLOCK),
        jax.ShapeDtypeStruct((LANES, t), F32),
        jax.ShapeDtypeStruct((t, SGU_WIDTH), BF16),
        jax.ShapeDtypeStruct((t, MEM_WIDTH), BF16),
    )
    out_specs = (
        pl.BlockSpec((PROJ_TM, NSA_HEADS * LANES), row),
        *[pl.BlockSpec((PROJ_TM, KV_W), row) for _ in range(4)],
        v_spec(SEL_CHUNK), v_spec(Q_BLOCK),
        pl.BlockSpec((LANES, PROJ_TM), lambda i: (0, i)),
        pl.BlockSpec((PROJ_TM, SGU_WIDTH), row),
        pl.BlockSpec((PROJ_TM, MEM_WIDTH), row),
    )
    return pl.pallas_call(
        _proj_kernel,
        grid=(nt,),
        in_specs=[
            pl.BlockSpec((PROJ_TM, d), row),
            pl.BlockSpec((d, 3 * d), lambda i: (0, 1)),
            pl.BlockSpec((PROJ_TM, 3 * LANES), lambda i: (i % per_seq, 0)),
            pl.BlockSpec((1, SGU_WIDTH), const2),
            pl.BlockSpec((1, SGU_WIDTH), const2),
            pl.BlockSpec((SGU_GROUPS, SGU_CHUNK, SGU_CHUNK), lambda i: (0, 0, 0)),
            pl.BlockSpec((SGU_CHUNK, SGU_WIDTH), const2),
        ],
        out_specs=out_specs,
        out_shape=out_shapes,
        compiler_params=pltpu.CompilerParams(
            dimension_semantics=("parallel",), vmem_limit_bytes=VMEM_LIMIT),
        name="proj",
    )(x2, wp, rope, lng, lnb, ws, bs)


N_A = 128
A_W = CMP_STRIDE * KV_W


def _compress_kernel(ka_ref, va_ref, pek_ref, pev_ref, w1k_ref, w2k_ref, w1v_ref, w2v_ref,
                     kcmp_ref, vcmp_ref):
    def one(a_ref, pe_ref, w1_ref, w2_ref, out_ref):
        a = a_ref[...].astype(F32)
        top = (a + pe_ref[0:1, :]).astype(BF16)
        bot = (a + pe_ref[1:2, :]).astype(BF16)
        h1 = _dot(top, w1_ref[0])
        h2 = _dot(bot, w1_ref[1])
        pre = h1 + pltpu.roll(h2, N_A - 1, 0)
        act = _gelu(pre).astype(BF16)
        out_ref[...] = _dot(act, w2_ref[...]).astype(BF16)

    one(ka_ref, pek_ref, w1k_ref, w2k_ref, kcmp_ref)
    one(va_ref, pev_ref, w1v_ref, w2v_ref, vcmp_ref)


def _compress_call(ka, va, pek, pev, w1k, w2k, w1v, w2v, batch):
    row = lambda b: (b, 0)
    c2 = lambda b: (0, 0)
    c3 = lambda b: (0, 0, 0)
    hid2 = NSA_KV_GROUPS * CMP_HIDDEN
    return pl.pallas_call(
        _compress_kernel,
        grid=(batch,),
        in_specs=[
            pl.BlockSpec((N_A, A_W), row),
            pl.BlockSpec((N_A, A_W), row),
            pl.BlockSpec((8, A_W), c2),
            pl.BlockSpec((8, A_W), c2),
            pl.BlockSpec((2, A_W, hid2), c3),
            pl.BlockSpec((hid2, KV_W), c2),
            pl.BlockSpec((2, A_W, hid2), c3),
            pl.BlockSpec((hid2, KV_W), c2),
        ],
        out_specs=(pl.BlockSpec((N_A, KV_W), row), pl.BlockSpec((N_A, KV_W), row)),
        out_shape=(jax.ShapeDtypeStruct((batch * N_A, KV_W), BF16),
                   jax.ShapeDtypeStruct((batch * N_A, KV_W), BF16)),
        compiler_params=pltpu.CompilerParams(
            dimension_semantics=("parallel",), vmem_limit_bytes=VMEM_LIMIT),
        name="compress",
    )(ka, va, pek, pev, w1k, w2k, w1v, w2v)


N_SEL = 32
SEL_CHUNK = 512
SEL_SUB = 128
HQ = NSA_HPG * Q_BLOCK
V_ROWS = HEAD_DIM + 16


def _nsa_kernel(q_ref, ksl_ref, kwn_ref, vslt_ref, vwnt_ref, kcmp_ref, vcmpt_ref, gt_ref,
                ot_ref, o_ref, m_s, acc_s, ocmp_s, owin_s, cap_s, out_s):
    c = pl.program_id(1)
    lane_hq = lax.broadcasted_iota(jnp.int32, (1, HQ), 1)
    pos_hq = c * Q_BLOCK + (lane_hq & (Q_BLOCK - 1))
    pos_q = c * Q_BLOCK + lax.broadcasted_iota(jnp.int32, (1, Q_BLOCK), 1)
    groups = range(NSA_KV_GROUPS)
    qgs = [jnp.concatenate(
        [q_ref[:, (g * NSA_HPG + hh) * LANES:(g * NSA_HPG + hh + 1) * LANES]
         for hh in range(NSA_HPG)], axis=0) for g in groups]

    _nsa_window(c, qgs, kwn_ref, vwnt_ref, owin_s)

    kcmp = kcmp_ref[...]
    s_c = [_dot_nt(kcmp, qgs[g]) for g in groups]
    n_idx = lax.broadcasted_iota(jnp.int32, (N_A, HQ), 0)
    valid_c = (n_idx * CMP_STRIDE + (CMP_LEN - 1)) <= pos_hq
    p_c = []
    for g in groups:
        sm_c = jnp.where(valid_c, s_c[g], NEG)
        m_c = jnp.max(sm_c, axis=0, keepdims=True)
        e_c = jnp.where(valid_c, jnp.exp2(sm_c - m_c), 0.0)
        d_c = jnp.sum(e_c, axis=0, keepdims=True)
        p_c.append(e_c / jnp.where(d_c > 0, d_c, 1.0))
    for g in groups:
        ocmp_s[g] = _dot(vcmpt_ref[g * HEAD_DIM:(g + 1) * HEAD_DIM, :], p_c[g].astype(BF16))

    ot = ot_ref[...]
    imp = []
    for g in groups:
        ps = (p_c[g][:, 0:Q_BLOCK] + p_c[g][:, Q_BLOCK:2 * Q_BLOCK]
              + p_c[g][:, 2 * Q_BLOCK:3 * Q_BLOCK] + p_c[g][:, 3 * Q_BLOCK:4 * Q_BLOCK])
        p_hi = ps.astype(BF16)
        r1 = ps - p_hi.astype(F32)
        p_lo = r1.astype(BF16)
        p_lo2 = (r1 - p_lo.astype(F32)).astype(BF16)
        imp.append(_dot(ot, p_hi) + _dot(ot, p_lo) + _dot(ot, p_lo2))

    j_idx = lax.broadcasted_iota(jnp.int32, (N_SEL, Q_BLOCK), 0)
    cur = pos_q // SLC_BLOCK
    future = j_idx > cur
    forced = (j_idx == 0) | (j_idx == cur) | (j_idx == cur - 1)
    imp = [jnp.where(future, NEG, jnp.where(forced, -NEG, imp[g])) for g in groups]
    rank = [jnp.zeros((N_SEL, Q_BLOCK), F32) for g in groups]
    for i in range(N_SEL):
        for g in groups:
            row = imp[g][i:i + 1, :]
            beats = (row > imp[g]) | ((row == imp[g]) & (j_idx > i))
            rank[g] = rank[g] + jnp.where(beats, 1.0, 0.0)
    for g in groups:
        cap_s[g] = jnp.where(rank[g] < float(SLC_TOPN), -NEG, NEG)
        m_s[g] = jnp.full((1, HQ), NEG, F32)
        acc_s[g] = jnp.zeros((V_ROWS, HQ), F32)

    blocks_per_sub = SEL_SUB // SLC_BLOCK
    subs_per_chunk = SEL_CHUNK // SEL_SUB

    def sel_chunk(kc):
        spans = [slice(kc * SEL_CHUNK + a * SEL_SUB, kc * SEL_CHUNK + (a + 1) * SEL_SUB)
                 for a in range(subs_per_chunk)]
        s = [[_dot_nt(ksl_ref[ks, :], qgs[g]) for g in groups] for ks in spans]
        for a, ks in enumerate(spans):
            kpos = ks.start + lax.broadcasted_iota(jnp.int32, (SEL_SUB, Q_BLOCK), 0)
            causal = kpos <= pos_q
            for g in groups:
                j0 = ks.start // SLC_BLOCK
                cap = jnp.concatenate(
                    [jnp.broadcast_to(cap_s[g, j:j + 1, :], (SLC_BLOCK, Q_BLOCK))
                     for j in range(j0, j0 + blocks_per_sub)], axis=0)
                cap = jnp.where(causal, cap, NEG)
                sm = jnp.minimum(s[a][g], jnp.concatenate([cap] * NSA_HPG, axis=1))
                m_old = m_s[g]
                m_new = jnp.maximum(m_old, jnp.max(sm, axis=0, keepdims=True))
                alpha = jnp.exp2(m_old - m_new)
                e = jnp.exp2(sm - m_new)
                v_t = vslt_ref[g, kc, :, a * SEL_SUB:(a + 1) * SEL_SUB]
                acc_s[g] = alpha * acc_s[g] + _dot(v_t, e.astype(BF16))
                m_s[g] = m_new

    sel_chunk(0)
    for kc in range(1, ksl_ref.shape[0] // SEL_CHUNK):
        pl.when(kc * (SEL_CHUNK // Q_BLOCK) <= c)(functools.partial(sel_chunk, kc))

    for g in groups:
        acc = acc_s[g]
        o_sel = acc[0:HEAD_DIM, :] * (1.0 / acc[HEAD_DIM:HEAD_DIM + 1, :])

        def gate_row(br):
            return jnp.concatenate(
                [gt_ref[(g * NSA_HPG + hh) * 3 + br:(g * NSA_HPG + hh) * 3 + br + 1, :]
                 for hh in range(NSA_HPG)], axis=1)
        o_t = gate_row(0) * ocmp_s[g] + gate_row(1) * o_sel + gate_row(2) * owin_s[g]
        for hh in range(NSA_HPG):
            h = g * NSA_HPG + hh
            out_s[h * HEAD_DIM:(h + 1) * HEAD_DIM, :] = o_t[:, hh * Q_BLOCK:(hh + 1) * Q_BLOCK]

    o_ref[...] = out_s[...].T.astype(BF16)


def _nsa_window(c, qgs, kwn_ref, vwnt_ref, owin_s):
    groups = range(NSA_KV_GROUPS)
    q_i = lax.broadcasted_iota(jnp.int32, (Q_BLOCK, Q_BLOCK), 1)
    k_i = lax.broadcasted_iota(jnp.int32, (Q_BLOCK, Q_BLOCK), 0)
    blks, caps = [], []
    for i in range(N_BAND + 1):
        blk = c - N_BAND + i
        blks.append(jnp.maximum(blk, 0))
        off = (N_BAND - i) * Q_BLOCK
        if off - (Q_BLOCK - 1) >= 0 and off + (Q_BLOCK - 1) < WINDOW:
            caps.append(jnp.where(blk >= 0, -NEG, NEG))
        else:
            diff = off + q_i - k_i
            ok = (diff >= 0) & (diff < WINDOW) & (blk >= 0)
            caps.append(jnp.concatenate([jnp.where(ok, -NEG, NEG)] * NSA_HPG, axis=1))
    k_blocks = [kwn_ref[pl.ds(pl.multiple_of(blks[i] * Q_BLOCK, Q_BLOCK), Q_BLOCK), :]
                for i in range(N_BAND + 1)]
    s_w = [[_dot_nt(k_blocks[i], qgs[g]) for i in range(N_BAND + 1)] for g in groups]
    for g in groups:
        sm_w = jnp.concatenate([jnp.minimum(s_w[g][i], caps[i]) for i in range(N_BAND + 1)], axis=0)
        m_w = jnp.max(sm_w, axis=0, keepdims=True)
        e_wb = jnp.exp2(sm_w - m_w).astype(BF16)
        o_win = _dot(vwnt_ref[g, blks[0]], e_wb[0:Q_BLOCK, :])
        for i in range(1, N_BAND + 1):
            o_win = o_win + _dot(vwnt_ref[g, blks[i]], e_wb[i * Q_BLOCK:(i + 1) * Q_BLOCK, :])
        owin_s[g] = o_win[0:HEAD_DIM, :] * (1.0 / o_win[HEAD_DIM:HEAD_DIM + 1, :])


def _nsa_call(qpad, ksl, kwn, vslt, vwnt, kcmp, vcmpt, gt, ot, batch, seq):
    nqb = seq // Q_BLOCK
    n_g = NSA_KV_GROUPS
    qrow = lambda b, c: (b * nqb + c, 0)
    brow = lambda b, c: (b, 0)
    c2 = lambda b, c: (0, 0)
    return pl.pallas_call(
        _nsa_kernel,
        grid=(batch, nqb),
        in_specs=[
            pl.BlockSpec((Q_BLOCK, NSA_HEADS * LANES), qrow),
            pl.BlockSpec((seq, KV_W), brow),
            pl.BlockSpec((seq, KV_W), brow),
            pl.BlockSpec((n_g, seq // SEL_CHUNK, V_ROWS, SEL_CHUNK), lambda b, c: (b, 0, 0, 0)),
            pl.BlockSpec((n_g, nqb, V_ROWS, Q_BLOCK), lambda b, c: (b, 0, 0, 0)),
            pl.BlockSpec((N_A, KV_W), brow),
            pl.BlockSpec((KV_W, N_A), brow),
            pl.BlockSpec((LANES, Q_BLOCK), lambda b, c: (0, b * nqb + c)),
            pl.BlockSpec((N_SEL, N_A), c2),
        ],
        out_specs=pl.BlockSpec((Q_BLOCK, Q_W), qrow),
        out_shape=jax.ShapeDtypeStruct((batch * seq, Q_W), BF16),
        scratch_shapes=[
            pltpu.VMEM((n_g, 1, HQ), F32),
            pltpu.VMEM((n_g, V_ROWS, HQ), F32),
            pltpu.VMEM((n_g, HEAD_DIM, HQ), F32),
            pltpu.VMEM((n_g, HEAD_DIM, HQ), F32),
            pltpu.VMEM((n_g, N_SEL, Q_BLOCK), F32),
            pltpu.VMEM((Q_W, Q_BLOCK), F32),
        ],
        compiler_params=pltpu.CompilerParams(
            dimension_semantics=("parallel", "arbitrary"), vmem_limit_bytes=VMEM_LIMIT),
        name="nsa",
    )(qpad, ksl, kwn, vslt, vwnt, kcmp, vcmpt, gt, ot)


MEM_TM = 512
MEM_V_ROWS = MEM_HEAD_DIM + 16


def _memattn_kernel(mq_ref, mem_ref, wkv_ref, o_ref, k_s, vt_s, out_s):
    @pl.when(pl.program_id(1) == 0)
    def _():
        kv = _dot(mem_ref[...].astype(BF16), wkv_ref[...])
        k_s[...] = kv[:, 0:MEM_WIDTH].astype(BF16)
        vt = kv[:, MEM_WIDTH:2 * MEM_WIDTH].T.astype(BF16)
        for h in range(MEM_HEADS):
            vt_s[h, 0:MEM_HEAD_DIM, :] = vt[h * MEM_HEAD_DIM:(h + 1) * MEM_HEAD_DIM, :]
            vt_s[h, MEM_HEAD_DIM:, :] = jnp.ones((MEM_V_ROWS - MEM_HEAD_DIM, vt.shape[1]), BF16)

    heads = range(MEM_HEADS)
    cols = [slice(h * MEM_HEAD_DIM, (h + 1) * MEM_HEAD_DIM) for h in heads]
    s_t = [_dot_nt(k_s[:, cols[h]], mq_ref[:, cols[h]]) for h in heads]
    e = []
    for h in heads:
        m = jnp.max(s_t[h], axis=0, keepdims=True)
        e.append(jnp.exp2((s_t[h] - m) * (MEM_HEAD_DIM ** -0.5 * LOG2E)).astype(BF16))
    o_t = [_dot(vt_s[h], e[h]) for h in heads]
    for h in heads:
        out_s[cols[h], :] = o_t[h][0:MEM_HEAD_DIM, :] * (1.0 / o_t[h][MEM_HEAD_DIM:MEM_HEAD_DIM + 1, :])
    o_ref[...] = out_s[...].T.astype(BF16)


def _memattn_call(mq, mem2, wkv, batch, seq):
    m_len = mem2.shape[0] // batch
    d = mem2.shape[1]
    per = seq // MEM_TM
    return pl.pallas_call(
        _memattn_kernel,
        grid=(batch, per),
        in_specs=[
            pl.BlockSpec((MEM_TM, MEM_WIDTH), lambda b, i: (b * per + i, 0)),
            pl.BlockSpec((m_len, d), lambda b, i: (b, 0)),
            pl.BlockSpec((d, 2 * MEM_WIDTH), lambda b, i: (0, 0)),
        ],
        out_specs=pl.BlockSpec((MEM_TM, MEM_WIDTH), lambda b, i: (b * per + i, 0)),
        out_shape=jax.ShapeDtypeStruct((batch * seq, MEM_WIDTH), BF16),
        scratch_shapes=[
            pltpu.VMEM((m_len, MEM_WIDTH), BF16),
            pltpu.VMEM((MEM_HEADS, MEM_V_ROWS, m_len), BF16),
            pltpu.VMEM((MEM_WIDTH, MEM_TM), F32),
        ],
        compiler_params=pltpu.CompilerParams(
            dimension_semantics=("parallel", "arbitrary"), vmem_limit_bytes=VMEM_LIMIT),
        name="memattn",
    )(mq, mem2, wkv)


MERGE_TM = 512


def _merge_kernel(x_ref, onsa_ref, osgu_ref, omem_ref, wg_ref, wbn_ref, wbs_ref, wbm_ref, wo_ref,
                  g_ref, b_ref, out_ref):
    x = x_ref[...]
    xb = x.astype(BF16)
    d = x.shape[1]
    y = None
    for br, (o_r, w_r) in enumerate(((onsa_ref, wbn_ref), (osgu_ref, wbs_ref), (omem_ref, wbm_ref))):
        gate = _sigmoid(_dot(xb, wg_ref[:, br * d:(br + 1) * d]))
        term = gate * _dot(o_r[...], w_r[...])
        y = term if y is None else y + term
    z = DN_ALPHA * x + _dot(y.astype(BF16), wo_ref[...])
    out_ref[...] = _layer_norm(z, g_ref[...], b_ref[...])


def _merge_call(x2, onsa, osgu, omem, wg, wbn, wbs, wbm, wo, g1, b1):
    t, d = x2.shape
    row = lambda i: (i, 0)
    c2 = lambda i: (0, 0)
    full = lambda a: pl.BlockSpec(a.shape, c2)
    return pl.pallas_call(
        _merge_kernel,
        grid=(t // MERGE_TM,),
        in_specs=[
            pl.BlockSpec((MERGE_TM, d), row),
            pl.BlockSpec((MERGE_TM, onsa.shape[1]), row),
            pl.BlockSpec((MERGE_TM, osgu.shape[1]), row),
            pl.BlockSpec((MERGE_TM, omem.shape[1]), row),
            pl.BlockSpec((d, 3 * d), c2),
            full(wbn), full(wbs), full(wbm), full(wo), full(g1), full(b1),
        ],
        out_specs=pl.BlockSpec((MERGE_TM, d), row),
        out_shape=jax.ShapeDtypeStruct((t, d), F32),
        compiler_params=pltpu.CompilerParams(
            dimension_semantics=("parallel",), vmem_limit_bytes=VMEM_LIMIT),
        name="merge",
    )(x2, onsa, osgu, omem, wg, wbn, wbs, wbm, wo, g1, b1)


MOE_TM = 1024
MOE_CH = 128
_YS_ROWS = MOE_TM + N_GROUPS * MOE_CH
_R_SLOT = 8
_DEST_LANE = 3 * _R_SLOT
_HID = EXPERTS_PER_GROUP * EXPERT_FF


def _moe_kernel(x_ref, tri_ref, wr_ref, br_ref, wg_ref, wu_ref, wd_ref, g_ref, b_ref, out_ref,
                xa_s, tok_s, keyr_s, ys_s, cnt_s, base_s):
    grp = pl.program_id(1)
    tm = x_ref.shape[0]
    d = x_ref.shape[1]

    @pl.when(grp == 0)
    def _route():
        xb = x_ref[...].astype(BF16)
        xa_s[:, 0:d] = xb
        logits = _dot(xb, wr_ref[...]) + br_ref[...]
        lt = logits.T
        row = lax.broadcasted_iota(jnp.int32, (_R_SLOT, tm), 0)
        gl = jnp.where(row < N_GROUPS, lt[0:_R_SLOT], NEG)
        gmax = jnp.max(gl, axis=0, keepdims=True)
        gidx = jnp.min(jnp.where(gl == gmax, row, _R_SLOT), axis=0, keepdims=True)
        gprob = 1.0 / jnp.sum(jnp.exp(gl - gmax), axis=0, keepdims=True)
        el = lt[_R_SLOT:2 * _R_SLOT]
        for k in range(1, N_GROUPS):
            el = jnp.where(gidx == k, lt[(k + 1) * _R_SLOT:(k + 2) * _R_SLOT], el)
        ee = jnp.exp(el - jnp.max(el, axis=0, keepdims=True))
        ep = ee / jnp.sum(ee, axis=0, keepdims=True)
        t1 = jnp.max(ep, axis=0, keepdims=True)
        i1 = jnp.min(jnp.where(ep == t1, row, _R_SLOT), axis=0, keepdims=True)
        rest = row != i1
        t2 = jnp.max(jnp.where(rest, ep, -1.0), axis=0, keepdims=True)
        i2 = jnp.min(jnp.where(rest & (ep == t2), row, _R_SLOT), axis=0, keepdims=True)
        den = t1 + t2
        cwf = (jnp.where(row == i1, t1 / den, 0.0) + jnp.where(row == i2, t2 / den, 0.0)) * gprob
        hi = cwf.astype(BF16).astype(F32)
        r1 = cwf - hi
        lo = r1.astype(BF16).astype(F32)
        lo2 = (r1 - lo).astype(BF16).astype(F32)

        onehot = jnp.where(row == gidx, 1.0, 0.0)
        rank = _dot_nt(onehot.astype(BF16), tri_ref[...])
        keyr_s[...] = jnp.where(onehot > 0.5, rank, -1.0)
        base = jnp.int32(0)
        basev = jnp.zeros((1, tm), F32)
        for k in range(N_GROUPS):
            n_k = jnp.sum(onehot[k:k + 1, :]).astype(jnp.int32)
            cnt_s[k] = n_k
            base_s[k] = base
            basev = jnp.where(gidx == k, base.astype(F32), basev)
            base = base + ((n_k + MOE_CH - 1) // MOE_CH) * MOE_CH
        dest = jnp.sum(rank * onehot, axis=0, keepdims=True) + basev
        tok = jnp.concatenate(
            [hi, lo, lo2, jnp.broadcast_to(dest, (_R_SLOT, tm)),
             jnp.zeros((LANES - 4 * _R_SLOT, tm), F32)], axis=0).T
        tok_s[...] = tok
        lane = lax.broadcasted_iota(jnp.int32, tok.shape, 1)
        xa_s[:, d:d + LANES] = jnp.where(lane < _DEST_LANE, tok, 0.0).astype(BF16)
        ys_s[...] = jnp.zeros(ys_s.shape, BF16)

    n_rows = cnt_s[grp]
    row0 = base_s[grp]
    keyr = keyr_s[pl.ds(grp, 1), :]
    half = _HID // 2

    def sweep(k, carry):
        ch = MOE_CH
        r_row = (lax.broadcasted_iota(jnp.int32, (ch, tm), 0) + k * MOE_CH).astype(F32)
        pick = jnp.where(keyr == r_row, 1.0, 0.0).astype(BF16)
        ga = _dot(pick, xa_s[...])
        xg = ga[:, 0:d].astype(BF16)
        cwg = ga[:, d:d + LANES]
        cs = (cwg + pltpu.roll(cwg, LANES - _R_SLOT, 1)
              + pltpu.roll(cwg, LANES - 2 * _R_SLOT, 1))
        hg = [_dot(xg, wg_ref[0, :, h * half:(h + 1) * half]) for h in range(2)]
        hu = [_dot(xg, wu_ref[0, :, h * half:(h + 1) * half]) for h in range(2)]
        y = None
        for h in range(2):
            per_half = EXPERTS_PER_GROUP // 2
            cexp = jnp.concatenate(
                [jnp.broadcast_to(cs[:, e:e + 1], (ch, EXPERT_FF))
                 for e in range(h * per_half, (h + 1) * per_half)], axis=1)
            hid = ((hg[h] * _sigmoid(hg[h])) * hu[h]) * cexp
            term = _dot(hid.astype(BF16), wd_ref[0, h * half:(h + 1) * half, :])
            y = term if y is None else y + term
        ys_s[pl.ds(pl.multiple_of(row0 + k * MOE_CH, MOE_CH), ch), :] = y.astype(BF16)
        return carry

    lax.fori_loop(0, (n_rows + MOE_CH - 1) // MOE_CH, sweep, 0)

    @pl.when(grp == N_GROUPS - 1)
    def _fin():
        r_col = lax.broadcasted_iota(jnp.int32, (tm, _YS_ROWS), 1).astype(F32)
        put = jnp.where(tok_s[:, _DEST_LANE:_DEST_LANE + 1] == r_col, 1.0, 0.0).astype(BF16)
        z = DN_ALPHA * x_ref[...] + _dot(put, ys_s[...])
        out_ref[...] = _layer_norm(z, g_ref[...], b_ref[...])


def _moe_call(x1, wr, br, wg, wu, wd, g2, b2):
    t, d = x1.shape
    row = lambda i, g: (i, 0)
    c2 = lambda i, g: (0, 0)
    idx = np.arange(MOE_TM)
    tri = jnp.asarray(idx[None, :] < idx[:, None], dtype=BF16)
    return pl.pallas_call(
        _moe_kernel,
        grid=(t // MOE_TM, N_GROUPS),
        in_specs=[
            pl.BlockSpec((MOE_TM, d), row, pipeline_mode=pl.Buffered(1)),
            pl.BlockSpec((MOE_TM, MOE_TM), c2, pipeline_mode=pl.Buffered(1)),
            pl.BlockSpec(wr.shape, c2),
            pl.BlockSpec(br.shape, c2),
            pl.BlockSpec((1, d, _HID), lambda i, g: (g, 0, 0)),
            pl.BlockSpec((1, d, _HID), lambda i, g: (g, 0, 0)),
            pl.BlockSpec((1, _HID, d), lambda i, g: (g, 0, 0)),
            pl.BlockSpec(g2.shape, c2),
            pl.BlockSpec(b2.shape, c2),
        ],
        out_specs=pl.BlockSpec((MOE_TM, d), row),
        out_shape=jax.ShapeDtypeStruct((t, d), F32),
        scratch_shapes=[
            pltpu.VMEM((MOE_TM, d + LANES), BF16),
            pltpu.VMEM((MOE_TM, LANES), F32),
            pltpu.VMEM((_R_SLOT, MOE_TM), F32),
            pltpu.VMEM((_YS_ROWS, d), BF16),
            pltpu.SMEM((N_GROUPS,), jnp.int32),
            pltpu.SMEM((N_GROUPS,), jnp.int32),
        ],
        compiler_params=pltpu.CompilerParams(
            dimension_semantics=("parallel", "arbitrary"), vmem_limit_bytes=MOE_VMEM_LIMIT),
        name="moe",
    )(x1, tri, wr, br, wg, wu, wd, g2, b2)


def _rope_table(seq):
    half = ROPE_DIM // 2
    inv = ROPE_THETA ** (-jnp.arange(0, ROPE_DIM, 2, dtype=F32) / ROPE_DIM)
    ang = jnp.arange(seq, dtype=F32)[:, None] * inv[None, :]
    cos, sin = jnp.cos(ang), jnp.sin(ang)
    rest = HEAD_DIM - ROPE_DIM
    one = jnp.ones((seq, rest), F32)
    zero = jnp.zeros((seq, rest), F32)
    zh = jnp.zeros((seq, half), F32)
    c_h = jnp.concatenate([cos, cos, one], axis=1)
    dn_h = jnp.concatenate([-sin, zh, zero], axis=1)
    up_h = jnp.concatenate([zh, sin, zero], axis=1)
    rep = LANES // HEAD_DIM
    return jnp.concatenate([jnp.tile(c_h, (1, rep)), jnp.tile(dn_h, (1, rep)), jnp.tile(up_h, (1, rep))],
                           axis=1)


def _expand_cmp_weights(pe, w1, w2):
    g_n = NSA_KV_GROUPS
    w1r = w1.astype(BF16).reshape(2, CMP_STRIDE, HEAD_DIM, CMP_HIDDEN)
    w2b = w2.astype(BF16)
    w1e = jnp.concatenate(
        [jnp.stack([w1r if g == e else jnp.zeros_like(w1r) for g in range(g_n)], axis=2)
         .reshape(2, A_W, CMP_HIDDEN) for e in range(g_n)], axis=2)
    w2e = jnp.concatenate(
        [jnp.concatenate([w2b if g == e else jnp.zeros_like(w2b) for e in range(g_n)], axis=1)
         for g in range(g_n)], axis=0)
    per = pe.reshape(2, CMP_STRIDE, 1, HEAD_DIM)
    pee = jnp.broadcast_to(per, (2, CMP_STRIDE, g_n, HEAD_DIM)).reshape(2, A_W)
    pee = jnp.concatenate([pee, jnp.zeros((6, A_W), pe.dtype)], axis=0)
    return pee.astype(F32), w1e.astype(BF16), w2e.astype(BF16)


def kernel(x, mem, w_in, cmp_pe_k, cmp_w1_k, cmp_w2_k, cmp_pe_v, cmp_w1_v, cmp_w2_v, sgu_ln_g, sgu_ln_b,
           sgu_w_s, sgu_b_s, w_mem_kv, w_br_nsa, w_br_sgu, w_br_mem, w_o, ln1_g, ln1_b, w_router_group,
           b_router_group, w_router_expert, b_router_expert, w_exp_gate, w_exp_up, w_exp_down, ln2_g, ln2_b):
    batch, seq, d = x.shape
    t = batch * seq
    assert w_in.shape[0] == DEPTH == 1
    assert seq % PROJ_TM == 0 and seq // CMP_STRIDE == N_A and seq // SLC_BLOCK == N_SEL

    w = w_in[0]
    offs = np.cumsum([0, Q_W, KV_W, KV_W, KV_W, KV_W, KV_W, KV_W, GATE_W, 2 * SGU_WIDTH, MEM_WIDTH, 3 * d])
    seg = lambda i: w[:, int(offs[i]):int(offs[i + 1])]
    gate_w = jnp.pad(seg(7), ((0, 0), (0, LANES - GATE_W)))
    w_all = jnp.concatenate(
        [seg(10), seg(0), seg(1), seg(3), seg(5), seg(2), seg(4), seg(6), gate_w, seg(8), seg(9),
         jnp.zeros((d, 3 * d - _PROJ_COLS), w.dtype)], axis=1).astype(BF16)
    rope = _rope_table(seq)
    tril = jnp.tril(jnp.ones((SGU_CHUNK, SGU_CHUNK), dtype=bool))
    ws = jnp.where(tril[None], sgu_w_s[0], 0.0).astype(BF16)
    bs = jnp.repeat(sgu_b_s[0].T, SGU_WIDTH // SGU_GROUPS, axis=1)

    x2 = x.reshape(t, d)
    (qpad, kc, ksl, kwn, vc, vslt, vwnt, gt, osgu, mq) = _proj_call(
        x2, w_all, rope, sgu_ln_g[0][None], sgu_ln_b[0][None], ws, bs, seq)

    pek, w1k, w2k = _expand_cmp_weights(cmp_pe_k[0], cmp_w1_k[0], cmp_w2_k[0])
    pev, w1v, w2v = _expand_cmp_weights(cmp_pe_v[0], cmp_w1_v[0], cmp_w2_v[0])
    kcmp, vcmp = _compress_call(kc.reshape(batch * N_A, A_W), vc.reshape(batch * N_A, A_W),
                                pek, pev, w1k, w2k, w1v, w2v, batch)

    nqb = seq // Q_BLOCK
    n_g = NSA_KV_GROUPS
    vcmpt = vcmp.reshape(batch, N_A, KV_W).transpose(0, 2, 1).reshape(batch * KV_W, N_A)
    ci = np.arange(N_A)
    sj = np.arange(N_SEL)
    overlap = ((ci[None, :] * CMP_STRIDE + CMP_LEN - 1 >= sj[:, None] * SLC_BLOCK)
               & (ci[None, :] * CMP_STRIDE <= sj[:, None] * SLC_BLOCK + SLC_BLOCK - 1)
               & (ci[None, :] < (seq - CMP_LEN) // CMP_STRIDE + 1))
    ot = jnp.asarray(overlap, dtype=BF16)
    onsa = _nsa_call(qpad, ksl, kwn, vslt, vwnt, kcmp, vcmpt, gt, ot, batch, seq)

    omem = _memattn_call(mq, mem.reshape(batch * mem.shape[1], d), w_mem_kv[0].astype(BF16), batch, seq)

    x1 = _merge_call(x2, onsa, osgu, omem, w_all, w_br_nsa[0].astype(BF16), w_br_sgu[0].astype(BF16),
                     w_br_mem[0].astype(BF16), w_o[0].astype(BF16), ln1_g[0][None], ln1_b[0][None])

    assert EXPERTS_PER_GROUP == _R_SLOT and N_GROUPS <= _R_SLOT
    n_r = _R_SLOT + N_GROUPS * EXPERTS_PER_GROUP
    wr = jnp.concatenate([jnp.pad(w_router_group[0], ((0, 0), (0, _R_SLOT - N_GROUPS))),
                          w_router_expert[0]], axis=1)
    wr = jnp.pad(wr, ((0, 0), (0, LANES - n_r))).astype(BF16)
    br = jnp.concatenate([jnp.pad(b_router_group[0], (0, _R_SLOT - N_GROUPS)), b_router_expert[0]])
    br = jnp.pad(br, (0, LANES - n_r))[None]
    wg = w_exp_gate[0].transpose(0, 2, 1, 3).reshape(N_GROUPS, d, _HID).astype(BF16)
    wu = w_exp_up[0].transpose(0, 2, 1, 3).reshape(N_GROUPS, d, _HID).astype(BF16)
    wd = w_exp_down[0].reshape(N_GROUPS, _HID, d).astype(BF16)
    out = _moe_call(x1, wr, br, wg, wu, wd, ln2_g[0][None], ln2_b[0][None])
    return out.reshape(batch, seq, d)
```

```python
import functools

import numpy as np
import jax
import jax.numpy as jnp
from jax import lax
from jax.experimental import pallas as pl
from jax.experimental.pallas import tpu as pltpu

NSA_HEADS = 8
NSA_KV_GROUPS = 2
NSA_HPG = NSA_HEADS // NSA_KV_GROUPS
HEAD_DIM = 64
CMP_LEN = 32
CMP_STRIDE = 16
CMP_HIDDEN = 256
SLC_BLOCK = 64
SLC_TOPN = 8
WINDOW = 512
Q_BLOCK = 128
N_BAND = WINDOW // Q_BLOCK
ROPE_THETA = 500000.0
ROPE_DIM = HEAD_DIM // 4
SGU_CHUNK = 128
SGU_GROUPS = 8
SGU_WIDTH = 512
MEM_HEADS = 4
MEM_HEAD_DIM = 128
MEM_WIDTH = MEM_HEADS * MEM_HEAD_DIM
N_GROUPS = 4
EXPERTS_PER_GROUP = 8
EXPERT_FF = 256
DEPTH = 1
DN_ALPHA = (2.0 * DEPTH) ** 0.25
LN_EPS = 1e-5
NEG = -1e30
LOG2E = 1.4426950408889634

LANES = 128
Q_W = NSA_HEADS * HEAD_DIM
KV_W = NSA_KV_GROUPS * HEAD_DIM
GATE_W = NSA_HEADS * 3
VMEM_LIMIT = 56 * 1024 * 1024
MOE_VMEM_LIMIT = 60 * 1024 * 1024

BF16 = jnp.bfloat16
F32 = jnp.float32


def _dot(a, b):
    return jnp.dot(a, b, preferred_element_type=F32)


def _dot_nt(a, b):
    return lax.dot_general(a, b, (((1,), (1,)), ((), ())), preferred_element_type=F32)


def _sigmoid(x):
    return 1.0 / (1.0 + jnp.exp(-x))


def _gelu(x):
    return 0.5 * x * (1.0 + lax.erf(x * (2.0 ** -0.5)))


def _layer_norm(x, g, b):
    mu = jnp.mean(x, axis=-1, keepdims=True)
    xc = x - mu
    var = jnp.mean(xc * xc, axis=-1, keepdims=True)
    return xc * lax.rsqrt(var + LN_EPS) * g + b


PROJ_TM = 512
_ROPE_COLS = Q_W + 3 * KV_W
_V_OFF = _ROPE_COLS
_G_OFF = _V_OFF + 3 * KV_W
_SGU_OFF = _G_OFF + LANES
_MQ_OFF = _SGU_OFF + 2 * SGU_WIDTH
_PROJ_COLS = _MQ_OFF + MEM_WIDTH


def _proj_kernel(x_ref, w32_ref, rope_ref, lng_ref, lnb_ref, ws_ref, bs_ref,
                 q_ref, kc_ref, ksl_ref, kwn_ref, vc_ref, vsl_ref, vwn_ref, gate_ref, osgu_ref, mq_ref,
                 w_ref):
    @pl.when(pl.program_id(0) == 0)
    def _():
        w_ref[...] = w32_ref[...].astype(BF16)

    xb = x_ref[...].astype(BF16)
    lane = lax.broadcasted_iota(jnp.int32, (PROJ_TM, LANES), 1)
    low = lane < HEAD_DIM
    cos = rope_ref[:, 0:LANES]
    s_dn = rope_ref[:, LANES:2 * LANES]
    s_up = rope_ref[:, 2 * LANES:3 * LANES]

    h = _dot(xb, w_ref[:, 0:_ROPE_COLS])
    k_refs = (kc_ref, ksl_ref, kwn_ref)
    for j in range(_ROPE_COLS // LANES):
        blk = h[:, j * LANES:(j + 1) * LANES]
        r = (blk * cos + pltpu.roll(blk, LANES - ROPE_DIM // 2, 1) * s_dn
             + pltpu.roll(blk, ROPE_DIM // 2, 1) * s_up)
        if j < Q_W // LANES:
            r = r * (HEAD_DIM ** -0.5 * LOG2E)
            sw = pltpu.roll(r, HEAD_DIM, 1)
            g = (2 * j) // NSA_HPG
            if g == 0:
                h0 = jnp.where(low, r, 0.0)
                h1 = jnp.where(low, sw, 0.0)
            else:
                h0 = jnp.where(low, 0.0, sw)
                h1 = jnp.where(low, 0.0, r)
            q_ref[:, (2 * j) * LANES:(2 * j + 1) * LANES] = h0.astype(BF16)
            q_ref[:, (2 * j + 1) * LANES:(2 * j + 2) * LANES] = h1.astype(BF16)
        else:
            k_refs[j - Q_W // LANES][...] = r.astype(BF16)

    hv = _dot(xb, w_ref[:, _V_OFF:_G_OFF])
    vc_ref[...] = hv[:, 0:LANES].astype(BF16)
    ones = jnp.ones((V_ROWS - HEAD_DIM, PROJ_TM), BF16)
    for v_ref, col0, blk in ((vsl_ref, LANES, SEL_CHUNK), (vwn_ref, 2 * LANES, Q_BLOCK)):
        vt = hv[:, col0:col0 + LANES].T.astype(BF16)
        for g in range(NSA_KV_GROUPS):
            for b in range(PROJ_TM // blk):
                v_ref[g, b, 0:HEAD_DIM, :] = vt[g * HEAD_DIM:(g + 1) * HEAD_DIM, b * blk:(b + 1) * blk]
                v_ref[g, b, HEAD_DIM:V_ROWS, :] = ones[:, 0:blk]

    gate_ref[...] = _sigmoid(_dot(xb, w_ref[:, _G_OFF:_SGU_OFF])).T

    mq_ref[...] = _dot(xb, w_ref[:, _MQ_OFF:_PROJ_COLS]).astype(BF16)

    z = _gelu(_dot(xb, w_ref[:, _SGU_OFF:_MQ_OFF]))
    u = z[:, 0:SGU_WIDTH]
    v = _layer_norm(z[:, SGU_WIDTH:2 * SGU_WIDTH], lng_ref[...], lnb_ref[...]).astype(BF16)
    lane_c = lax.broadcasted_iota(jnp.int32, (SGU_CHUNK, LANES), 1)
    low_c = lane_c < (SGU_WIDTH // SGU_GROUPS)
    n_ch = PROJ_TM // SGU_CHUNK
    for gp in range(SGU_WIDTH // LANES):
        cols = slice(gp * LANES, (gp + 1) * LANES)
        vcat = jnp.concatenate([v[ci * SGU_CHUNK:(ci + 1) * SGU_CHUNK, cols] for ci in range(n_ch)], axis=1)
        sv0 = _dot(ws_ref[2 * gp], vcat)
        sv1 = _dot(ws_ref[2 * gp + 1], vcat)
        for ci in range(n_ch):
            rows = slice(ci * SGU_CHUNK, (ci + 1) * SGU_CHUNK)
            lanes = slice(ci * LANES, (ci + 1) * LANES)
            sv = jnp.where(low_c, sv0[:, lanes], sv1[:, lanes]) + bs_ref[:, cols]
            osgu_ref[rows, cols] = (u[rows, cols] * sv).astype(BF16)


def _proj_call(x2, wp, rope, lng, lnb, ws, bs, seq):
    t = x2.shape[0]
    d = x2.shape[1]
    nt = t // PROJ_TM
    per_seq = seq // PROJ_TM
    row = lambda i: (i, 0)
    const2 = lambda i: (0, 0)
    assert PROJ_TM % SEL_CHUNK == 0 and PROJ_TM % Q_BLOCK == 0
    n_g = NSA_KV_GROUPS
    batch = t // seq
    vblock = lambda i: (i // per_seq, i % per_seq, 0, 0)

    def v_shape(blk):
        return jax.ShapeDtypeStruct((batch * n_g, seq // blk, V_ROWS, blk), BF16)

    def v_spec(blk):
        return pl.BlockSpec((n_g, PROJ_TM // blk, V_ROWS, blk), vblock)

    out_shapes = (
        jax.ShapeDtypeStruct((t, NSA_HEADS * LANES), BF16),
        *[jax.ShapeDtypeStruct((t, KV_W), BF16) for _ in range(4)],
        v_shape(SEL_CHUNK), v_shape(Q_BLOCK),
        jax.ShapeDtypeStruct((LANES, t), F32),
        jax.ShapeDtypeStruct((t, SGU_WIDTH), BF16),
        jax.ShapeDtypeStruct((t, MEM_WIDTH), BF16),
    )
    out_specs = (
        pl.BlockSpec((PROJ_TM, NSA_HEADS * LANES), row),
        *[pl.BlockSpec((PROJ_TM, KV_W), row) for _ in range(4)],
        v_spec(SEL_CHUNK), v_spec(Q_BLOCK),
        pl.BlockSpec((LANES, PROJ_TM), lambda i: (0, i)),
        pl.BlockSpec((PROJ_TM, SGU_WIDTH), row),
        pl.BlockSpec((PROJ_TM, MEM_WIDTH), row),
    )
    return pl.pallas_call(
        _proj_kernel,
        grid=(nt,),
        in_specs=[
            pl.BlockSpec((PROJ_TM, d), row),
            pl.BlockSpec((d, 3 * d), lambda i: (0, 1), pipeline_mode=pl.Buffered(1)),
            pl.BlockSpec((PROJ_TM, 3 * LANES), lambda i: (i % per_seq, 0)),
            pl.BlockSpec((1, SGU_WIDTH), const2),
            pl.BlockSpec((1, SGU_WIDTH), const2),
            pl.BlockSpec((SGU_GROUPS, SGU_CHUNK, SGU_CHUNK), lambda i: (0, 0, 0)),
            pl.BlockSpec((SGU_CHUNK, SGU_WIDTH), const2),
        ],
        out_specs=out_specs,
        out_shape=out_shapes,
        scratch_shapes=[pltpu.VMEM((d, 3 * d), BF16)],
        compiler_params=pltpu.CompilerParams(
            dimension_semantics=("arbitrary",), vmem_limit_bytes=VMEM_LIMIT),
        name="proj",
    )(x2, wp, rope, lng, lnb, ws, bs)


N_A = 128
A_W = CMP_STRIDE * KV_W


def _compress_kernel(ka_ref, va_ref, pek_ref, pev_ref, w1k_ref, w2k_ref, w1v_ref, w2v_ref,
                     kcmp_ref, vcmp_ref):
    def one(a_ref, pe_ref, w1_ref, w2_ref, out_ref):
        a = a_ref[...].astype(F32)
        top = (a + pe_ref[0:1, :]).astype(BF16)
        bot = (a + pe_ref[1:2, :]).astype(BF16)
        h1 = _dot(top, w1_ref[0])
        h2 = _dot(bot, w1_ref[1])
        pre = h1 + pltpu.roll(h2, N_A - 1, 0)
        act = _gelu(pre).astype(BF16)
        out_ref[...] = _dot(act, w2_ref[...]).astype(BF16)

    one(ka_ref, pek_ref, w1k_ref, w2k_ref, kcmp_ref)
    one(va_ref, pev_ref, w1v_ref, w2v_ref, vcmp_ref)


def _compress_call(ka, va, pek, pev, w1k, w2k, w1v, w2v, batch):
    row = lambda b: (b, 0)
    c2 = lambda b: (0, 0)
    c3 = lambda b: (0, 0, 0)
    hid2 = NSA_KV_GROUPS * CMP_HIDDEN
    return pl.pallas_call(
        _compress_kernel,
        grid=(batch,),
        in_specs=[
            pl.BlockSpec((N_A, A_W), row),
            pl.BlockSpec((N_A, A_W), row),
            pl.BlockSpec((8, A_W), c2),
            pl.BlockSpec((8, A_W), c2),
            pl.BlockSpec((2, A_W, hid2), c3),
            pl.BlockSpec((hid2, KV_W), c2),
            pl.BlockSpec((2, A_W, hid2), c3),
            pl.BlockSpec((hid2, KV_W), c2),
        ],
        out_specs=(pl.BlockSpec((N_A, KV_W), row), pl.BlockSpec((N_A, KV_W), row)),
        out_shape=(jax.ShapeDtypeStruct((batch * N_A, KV_W), BF16),
                   jax.ShapeDtypeStruct((batch * N_A, KV_W), BF16)),
        compiler_params=pltpu.CompilerParams(
            dimension_semantics=("parallel",), vmem_limit_bytes=VMEM_LIMIT),
        name="compress",
    )(ka, va, pek, pev, w1k, w2k, w1v, w2v)


N_SEL = 32
SEL_CHUNK = 512
SEL_SUB = 128
HQ = NSA_HPG * Q_BLOCK
V_ROWS = HEAD_DIM + 16


def _nsa_kernel(q_ref, ksl_ref, kwn_ref, vslt_ref, vwnt_ref, kcmp_ref, vcmpt_ref, gt_ref,
                ot_ref, o_ref, m_s, acc_s, ocmp_s, owin_s, cap_s, out_s):
    c = pl.program_id(1)
    lane_hq = lax.broadcasted_iota(jnp.int32, (1, HQ), 1)
    pos_hq = c * Q_BLOCK + (lane_hq & (Q_BLOCK - 1))
    pos_q = c * Q_BLOCK + lax.broadcasted_iota(jnp.int32, (1, Q_BLOCK), 1)
    groups = range(NSA_KV_GROUPS)
    qgs = [jnp.concatenate(
        [q_ref[:, (g * NSA_HPG + hh) * LANES:(g * NSA_HPG + hh + 1) * LANES]
         for hh in range(NSA_HPG)], axis=0) for g in groups]

    kcmp = kcmp_ref[...]
    s_c = [_dot_nt(kcmp, qgs[g]) for g in groups]
    n_idx = lax.broadcasted_iota(jnp.int32, (N_A, HQ), 0)
    valid_c = (n_idx * CMP_STRIDE + (CMP_LEN - 1)) <= pos_hq
    p_c = []
    for g in groups:
        sm_c = jnp.where(valid_c, s_c[g], NEG)
        m_c = jnp.max(sm_c, axis=0, keepdims=True)
        e_c = jnp.where(valid_c, jnp.exp2(sm_c - m_c), 0.0)
        d_c = jnp.sum(e_c, axis=0, keepdims=True)
        p_c.append(e_c / jnp.where(d_c > 0, d_c, 1.0))
    for g in groups:
        ocmp_s[g] = _dot(vcmpt_ref[g * HEAD_DIM:(g + 1) * HEAD_DIM, :], p_c[g].astype(BF16))

    ot = ot_ref[...]
    imp = []
    for g in groups:
        ps = (p_c[g][:, 0:Q_BLOCK] + p_c[g][:, Q_BLOCK:2 * Q_BLOCK]
              + p_c[g][:, 2 * Q_BLOCK:3 * Q_BLOCK] + p_c[g][:, 3 * Q_BLOCK:4 * Q_BLOCK])
        p_hi = ps.astype(BF16)
        r1 = ps - p_hi.astype(F32)
        p_lo = r1.astype(BF16)
        p_lo2 = (r1 - p_lo.astype(F32)).astype(BF16)
        imp.append(_dot(ot, p_hi) + _dot(ot, p_lo) + _dot(ot, p_lo2))

    j_idx = lax.broadcasted_iota(jnp.int32, (N_SEL, Q_BLOCK), 0)
    cur = pos_q // SLC_BLOCK
    future = j_idx > cur
    forced = (j_idx == 0) | (j_idx == cur) | (j_idx == cur - 1)
    imp = [jnp.where(future, NEG, jnp.where(forced, -NEG, imp[g])) for g in groups]
    rank = [jnp.zeros((N_SEL, Q_BLOCK), F32) for g in groups]
    for i in range(N_SEL):
        for g in groups:
            row = imp[g][i:i + 1, :]
            beats = (row > imp[g]) | ((row == imp[g]) & (j_idx > i))
            rank[g] = rank[g] + jnp.where(beats, 1.0, 0.0)
    for g in groups:
        cap_s[g] = jnp.where(rank[g] < float(SLC_TOPN), -NEG, NEG)
        m_s[g] = jnp.full((1, HQ), NEG, F32)
        acc_s[g] = jnp.zeros((V_ROWS, HQ), F32)

    _nsa_window(c, qgs, kwn_ref, vwnt_ref, owin_s)

    blocks_per_sub = SEL_SUB // SLC_BLOCK
    subs_per_chunk = SEL_CHUNK // SEL_SUB

    def sel_chunk(kc):
        spans = [slice(kc * SEL_CHUNK + a * SEL_SUB, kc * SEL_CHUNK + (a + 1) * SEL_SUB)
                 for a in range(subs_per_chunk)]
        s = [[_dot_nt(ksl_ref[ks, :], qgs[g]) for g in groups] for ks in spans]
        for a, ks in enumerate(spans):
            kpos = ks.start + lax.broadcasted_iota(jnp.int32, (SEL_SUB, Q_BLOCK), 0)
            causal = kpos <= pos_q
            for g in groups:
                j0 = ks.start // SLC_BLOCK
                cap = jnp.concatenate(
                    [jnp.broadcast_to(cap_s[g, j:j + 1, :], (SLC_BLOCK, Q_BLOCK))
                     for j in range(j0, j0 + blocks_per_sub)], axis=0)
                cap = jnp.where(causal, cap, NEG)
                sm = jnp.minimum(s[a][g], jnp.concatenate([cap] * NSA_HPG, axis=1))
                m_old = m_s[g]
                m_new = jnp.maximum(m_old, jnp.max(sm, axis=0, keepdims=True))
                alpha = jnp.exp2(m_old - m_new)
                e = jnp.exp2((sm - m_new).astype(BF16))
                v_t = vslt_ref[g, kc, :, a * SEL_SUB:(a + 1) * SEL_SUB]
                acc_s[g] = alpha * acc_s[g] + _dot(v_t, e)
                m_s[g] = m_new

    sel_chunk(0)
    for kc in range(1, ksl_ref.shape[0] // SEL_CHUNK):
        pl.when(kc * (SEL_CHUNK // Q_BLOCK) <= c)(functools.partial(sel_chunk, kc))

    for g in groups:
        acc = acc_s[g]
        o_sel = acc[0:HEAD_DIM, :] * (1.0 / acc[HEAD_DIM:HEAD_DIM + 1, :])

        def gate_row(br):
            return jnp.concatenate(
                [gt_ref[(g * NSA_HPG + hh) * 3 + br:(g * NSA_HPG + hh) * 3 + br + 1, :]
                 for hh in range(NSA_HPG)], axis=1)
        o_t = gate_row(0) * ocmp_s[g] + gate_row(1) * o_sel + gate_row(2) * owin_s[g]
        for hh in range(NSA_HPG):
            h = g * NSA_HPG + hh
            out_s[h * HEAD_DIM:(h + 1) * HEAD_DIM, :] = o_t[:, hh * Q_BLOCK:(hh + 1) * Q_BLOCK]

    o_ref[...] = out_s[...].T.astype(BF16)


def _nsa_window(c, qgs, kwn_ref, vwnt_ref, owin_s):
    groups = range(NSA_KV_GROUPS)
    q_i = lax.broadcasted_iota(jnp.int32, (Q_BLOCK, Q_BLOCK), 1)
    k_i = lax.broadcasted_iota(jnp.int32, (Q_BLOCK, Q_BLOCK), 0)
    blks, caps = [], []
    for i in range(N_BAND + 1):
        blk = c - N_BAND + i
        blks.append(jnp.maximum(blk, 0))
        off = (N_BAND - i) * Q_BLOCK
        if off - (Q_BLOCK - 1) >= 0 and off + (Q_BLOCK - 1) < WINDOW:
            caps.append(jnp.where(blk >= 0, -NEG, NEG))
        else:
            diff = off + q_i - k_i
            ok = (diff >= 0) & (diff < WINDOW) & (blk >= 0)
            caps.append(jnp.concatenate([jnp.where(ok, -NEG, NEG)] * NSA_HPG, axis=1))
    k_blocks = [kwn_ref[pl.ds(pl.multiple_of(blks[i] * Q_BLOCK, Q_BLOCK), Q_BLOCK), :]
                for i in range(N_BAND + 1)]
    s_w = [[None] * (N_BAND + 1) for g in groups]
    for i in reversed(range(N_BAND + 1)):
        for g in groups:
            s_w[g][i] = _dot_nt(k_blocks[i], qgs[g])
    m_w = [None for g in groups]
    o_win = [None for g in groups]
    for i in reversed(range(N_BAND + 1)):
        for g in groups:
            sm = jnp.minimum(s_w[g][i], caps[i])
            m_blk = jnp.max(sm, axis=0, keepdims=True)
            m_new = m_blk if m_w[g] is None else jnp.maximum(m_w[g], m_blk)
            pv = _dot(vwnt_ref[g, blks[i]], jnp.exp2((sm - m_new).astype(BF16)))
            o_win[g] = pv if m_w[g] is None else jnp.exp2(m_w[g] - m_new) * o_win[g] + pv
            m_w[g] = m_new
    for g in groups:
        owin_s[g] = o_win[g][0:HEAD_DIM, :] * (1.0 / o_win[g][HEAD_DIM:HEAD_DIM + 1, :])


def _nsa_call(qpad, ksl, kwn, vslt, vwnt, kcmp, vcmpt, gt, ot, batch, seq):
    nqb = seq // Q_BLOCK
    n_g = NSA_KV_GROUPS
    qrow = lambda b, c: (b * nqb + c, 0)
    brow = lambda b, c: (b, 0)
    c2 = lambda b, c: (0, 0)
    return pl.pallas_call(
        _nsa_kernel,
        grid=(batch, nqb),
        in_specs=[
            pl.BlockSpec((Q_BLOCK, NSA_HEADS * LANES), qrow),
            pl.BlockSpec((seq, KV_W), brow),
            pl.BlockSpec((seq, KV_W), brow),
            pl.BlockSpec((n_g, seq // SEL_CHUNK, V_ROWS, SEL_CHUNK), lambda b, c: (b, 0, 0, 0)),
            pl.BlockSpec((n_g, nqb, V_ROWS, Q_BLOCK), lambda b, c: (b, 0, 0, 0)),
            pl.BlockSpec((N_A, KV_W), brow),
            pl.BlockSpec((KV_W, N_A), brow),
            pl.BlockSpec((LANES, Q_BLOCK), lambda b, c: (0, b * nqb + c)),
            pl.BlockSpec((N_SEL, N_A), c2),
        ],
        out_specs=pl.BlockSpec((Q_BLOCK, Q_W), qrow),
        out_shape=jax.ShapeDtypeStruct((batch * seq, Q_W), BF16),
        scratch_shapes=[
            pltpu.VMEM((n_g, 1, HQ), F32),
            pltpu.VMEM((n_g, V_ROWS, HQ), F32),
            pltpu.VMEM((n_g, HEAD_DIM, HQ), F32),
            pltpu.VMEM((n_g, HEAD_DIM, HQ), F32),
            pltpu.VMEM((n_g, N_SEL, Q_BLOCK), F32),
            pltpu.VMEM((Q_W, Q_BLOCK), F32),
        ],
        compiler_params=pltpu.CompilerParams(
            dimension_semantics=("parallel", "arbitrary"), vmem_limit_bytes=VMEM_LIMIT),
        name="nsa",
    )(qpad, ksl, kwn, vslt, vwnt, kcmp, vcmpt, gt, ot)


MEM_TM = 512
MEM_V_ROWS = MEM_HEAD_DIM + 16


def _memattn_kernel(mq_ref, mem_ref, wkv_ref, o_ref, k_s, vt_s, out_s):
    @pl.when(pl.program_id(1) == 0)
    def _():
        kv = _dot(mem_ref[...].astype(BF16), wkv_ref[...].astype(BF16))
        k_s[...] = kv[:, 0:MEM_WIDTH].astype(BF16)
        vt = kv[:, MEM_WIDTH:2 * MEM_WIDTH].T.astype(BF16)
        for h in range(MEM_HEADS):
            vt_s[h, 0:MEM_HEAD_DIM, :] = vt[h * MEM_HEAD_DIM:(h + 1) * MEM_HEAD_DIM, :]
            vt_s[h, MEM_HEAD_DIM:, :] = jnp.ones((MEM_V_ROWS - MEM_HEAD_DIM, vt.shape[1]), BF16)

    heads = range(MEM_HEADS)
    cols = [slice(h * MEM_HEAD_DIM, (h + 1) * MEM_HEAD_DIM) for h in heads]
    s_t = [_dot_nt(k_s[:, cols[h]], mq_ref[:, cols[h]]) for h in heads]
    e = []
    for h in heads:
        m = jnp.max(s_t[h], axis=0, keepdims=True)
        e.append(jnp.exp2((s_t[h] - m) * (MEM_HEAD_DIM ** -0.5 * LOG2E)).astype(BF16))
    o_t = [_dot(vt_s[h], e[h]) for h in heads]
    for h in heads:
        out_s[cols[h], :] = o_t[h][0:MEM_HEAD_DIM, :] * (1.0 / o_t[h][MEM_HEAD_DIM:MEM_HEAD_DIM + 1, :])
    o_ref[...] = out_s[...].T.astype(BF16)


def _memattn_call(mq, mem2, wkv, batch, seq):
    m_len = mem2.shape[0] // batch
    d = mem2.shape[1]
    per = seq // MEM_TM
    return pl.pallas_call(
        _memattn_kernel,
        grid=(batch, per),
        in_specs=[
            pl.BlockSpec((MEM_TM, MEM_WIDTH), lambda b, i: (b * per + i, 0)),
            pl.BlockSpec((m_len, d), lambda b, i: (b, 0)),
            pl.BlockSpec((d, 2 * MEM_WIDTH), lambda b, i: (0, 0)),
        ],
        out_specs=pl.BlockSpec((MEM_TM, MEM_WIDTH), lambda b, i: (b * per + i, 0)),
        out_shape=jax.ShapeDtypeStruct((batch * seq, MEM_WIDTH), BF16),
        scratch_shapes=[
            pltpu.VMEM((m_len, MEM_WIDTH), BF16),
            pltpu.VMEM((MEM_HEADS, MEM_V_ROWS, m_len), BF16),
            pltpu.VMEM((MEM_WIDTH, MEM_TM), F32),
        ],
        compiler_params=pltpu.CompilerParams(
            dimension_semantics=("parallel", "arbitrary"), vmem_limit_bytes=VMEM_LIMIT),
        name="memattn",
    )(mq, mem2, wkv)


MERGE_TM = 512


def _merge_kernel(x_ref, onsa_ref, osgu_ref, omem_ref, wg32_ref, wbn32_ref, wbs32_ref, wbm32_ref, wo32_ref,
                  g_ref, b_ref, out_ref, wg_ref, wbn_ref, wbs_ref, wbm_ref, wo_ref):
    @pl.when(pl.program_id(0) == 0)
    def _():
        for dst, src in ((wg_ref, wg32_ref), (wbn_ref, wbn32_ref), (wbs_ref, wbs32_ref),
                         (wbm_ref, wbm32_ref), (wo_ref, wo32_ref)):
            dst[...] = src[...].astype(BF16)

    x = x_ref[...]
    xb = x.astype(BF16)
    d = x.shape[1]
    y = None
    for br, (o_r, w_r) in enumerate(((onsa_ref, wbn_ref), (osgu_ref, wbs_ref), (omem_ref, wbm_ref))):
        gate = _sigmoid(_dot(xb, wg_ref[:, br * d:(br + 1) * d]))
        term = gate * _dot(o_r[...], w_r[...])
        y = term if y is None else y + term
    z = DN_ALPHA * x + _dot(y.astype(BF16), wo_ref[...])
    out_ref[...] = _layer_norm(z, g_ref[...], b_ref[...])


def _merge_call(x2, onsa, osgu, omem, wg, wbn, wbs, wbm, wo, g1, b1):
    t, d = x2.shape
    row = lambda i: (i, 0)
    c2 = lambda i: (0, 0)
    full = lambda a: pl.BlockSpec(a.shape, c2)
    once = lambda a: pl.BlockSpec(a.shape, c2, pipeline_mode=pl.Buffered(1))
    return pl.pallas_call(
        _merge_kernel,
        grid=(t // MERGE_TM,),
        in_specs=[
            pl.BlockSpec((MERGE_TM, d), row),
            pl.BlockSpec((MERGE_TM, onsa.shape[1]), row),
            pl.BlockSpec((MERGE_TM, osgu.shape[1]), row),
            pl.BlockSpec((MERGE_TM, omem.shape[1]), row),
            pl.BlockSpec((d, 3 * d), c2, pipeline_mode=pl.Buffered(1)),
            once(wbn), once(wbs), once(wbm), once(wo), full(g1), full(b1),
        ],
        out_specs=pl.BlockSpec((MERGE_TM, d), row),
        out_shape=jax.ShapeDtypeStruct((t, d), F32),
        scratch_shapes=[pltpu.VMEM((d, 3 * d), BF16)]
        + [pltpu.VMEM(a.shape, BF16) for a in (wbn, wbs, wbm, wo)],
        compiler_params=pltpu.CompilerParams(
            dimension_semantics=("arbitrary",), vmem_limit_bytes=VMEM_LIMIT),
        name="merge",
    )(x2, onsa, osgu, omem, wg, wbn, wbs, wbm, wo, g1, b1)


MOE_TM = 1024
MOE_CH = 128
_YS_ROWS = MOE_TM + N_GROUPS * MOE_CH
_R_SLOT = 8
_DEST_LANE = 3 * _R_SLOT
_HID = EXPERTS_PER_GROUP * EXPERT_FF


def _moe_kernel(x_ref, tri_ref, wr_ref, br_ref, wg_ref, wu_ref, wd_ref, g_ref, b_ref, out_ref,
                xa_s, tok_s, keyr_s, ys_s, cnt_s, base_s):
    grp = pl.program_id(1)
    tm = x_ref.shape[0]
    d = x_ref.shape[1]

    @pl.when(grp == 0)
    def _route():
        xb = x_ref[...].astype(BF16)
        xa_s[:, 0:d] = xb
        logits = _dot(xb, wr_ref[...]) + br_ref[...]
        lt = logits.T
        row = lax.broadcasted_iota(jnp.int32, (_R_SLOT, tm), 0)
        gl = jnp.where(row < N_GROUPS, lt[0:_R_SLOT], NEG)
        gmax = jnp.max(gl, axis=0, keepdims=True)
        gidx = jnp.min(jnp.where(gl == gmax, row, _R_SLOT), axis=0, keepdims=True)
        gprob = 1.0 / jnp.sum(jnp.exp(gl - gmax), axis=0, keepdims=True)
        el = lt[_R_SLOT:2 * _R_SLOT]
        for k in range(1, N_GROUPS):
            el = jnp.where(gidx == k, lt[(k + 1) * _R_SLOT:(k + 2) * _R_SLOT], el)
        ee = jnp.exp(el - jnp.max(el, axis=0, keepdims=True))
        ep = ee / jnp.sum(ee, axis=0, keepdims=True)
        t1 = jnp.max(ep, axis=0, keepdims=True)
        i1 = jnp.min(jnp.where(ep == t1, row, _R_SLOT), axis=0, keepdims=True)
        rest = row != i1
        t2 = jnp.max(jnp.where(rest, ep, -1.0), axis=0, keepdims=True)
        i2 = jnp.min(jnp.where(rest & (ep == t2), row, _R_SLOT), axis=0, keepdims=True)
        den = t1 + t2
        cwf = (jnp.where(row == i1, t1 / den, 0.0) + jnp.where(row == i2, t2 / den, 0.0)) * gprob
        hi = cwf.astype(BF16).astype(F32)
        r1 = cwf - hi
        lo = r1.astype(BF16).astype(F32)
        lo2 = (r1 - lo).astype(BF16).astype(F32)

        onehot = jnp.where(row == gidx, 1.0, 0.0)
        rank = _dot_nt(onehot.astype(BF16), tri_ref[...])
        keyr_s[...] = jnp.where(onehot > 0.5, rank, -1.0)
        base = jnp.int32(0)
        basev = jnp.zeros((1, tm), F32)
        for k in range(N_GROUPS):
            n_k = jnp.sum(onehot[k:k + 1, :]).astype(jnp.int32)
            cnt_s[k] = n_k
            base_s[k] = base
            basev = jnp.where(gidx == k, base.astype(F32), basev)
            base = base + ((n_k + MOE_CH - 1) // MOE_CH) * MOE_CH
        dest = jnp.sum(rank * onehot, axis=0, keepdims=True) + basev
        tok = jnp.concatenate(
            [hi, lo, lo2, jnp.broadcast_to(dest, (_R_SLOT, tm)),
             jnp.zeros((LANES - 4 * _R_SLOT, tm), F32)], axis=0).T
        tok_s[...] = tok
        lane = lax.broadcasted_iota(jnp.int32, tok.shape, 1)
        xa_s[:, d:d + LANES] = jnp.where(lane < _DEST_LANE, tok, 0.0).astype(BF16)
        ys_s[...] = jnp.zeros(ys_s.shape, BF16)

    n_rows = cnt_s[grp]
    row0 = base_s[grp]
    keyr = keyr_s[pl.ds(grp, 1), :]
    half = _HID // 2

    def sweep(k, carry):
        ch = MOE_CH
        r_row = (lax.broadcasted_iota(jnp.int32, (ch, tm), 0) + k * MOE_CH).astype(F32)
        pick = jnp.where(keyr == r_row, 1.0, 0.0).astype(BF16)
        ga = _dot(pick, xa_s[...])
        xg = ga[:, 0:d].astype(BF16)
        cwg = ga[:, d:d + LANES]
        cs = (cwg + pltpu.roll(cwg, LANES - _R_SLOT, 1)
              + pltpu.roll(cwg, LANES - 2 * _R_SLOT, 1))
        hg = [_dot(xg, wg_ref[0, :, h * half:(h + 1) * half]) for h in range(2)]
        hu = [_dot(xg, wu_ref[0, :, h * half:(h + 1) * half]) for h in range(2)]
        y = None
        for h in range(2):
            per_half = EXPERTS_PER_GROUP // 2
            cexp = jnp.concatenate(
                [jnp.broadcast_to(cs[:, e:e + 1], (ch, EXPERT_FF))
                 for e in range(h * per_half, (h + 1) * per_half)], axis=1)
            hid = ((hg[h] * _sigmoid(hg[h])) * hu[h]) * cexp
            term = _dot(hid.astype(BF16), wd_ref[0, h * half:(h + 1) * half, :])
            y = term if y is None else y + term
        ys_s[pl.ds(pl.multiple_of(row0 + k * MOE_CH, MOE_CH), ch), :] = y.astype(BF16)
        return carry

    lax.fori_loop(0, (n_rows + MOE_CH - 1) // MOE_CH, sweep, 0)

    @pl.when(grp == N_GROUPS - 1)
    def _fin():
        r_col = lax.broadcasted_iota(jnp.int32, (tm, _YS_ROWS), 1).astype(F32)
        put = jnp.where(tok_s[:, _DEST_LANE:_DEST_LANE + 1] == r_col, 1.0, 0.0).astype(BF16)
        z = DN_ALPHA * x_ref[...] + _dot(put, ys_s[...])
        out_ref[...] = _layer_norm(z, g_ref[...], b_ref[...])


def _moe_call(x1, wr, br, wg, wu, wd, g2, b2):
    t, d = x1.shape
    row = lambda i, g: (i, 0)
    c2 = lambda i, g: (0, 0)
    idx = np.arange(MOE_TM)
    tri = jnp.asarray(idx[None, :] < idx[:, None], dtype=BF16)
    return pl.pallas_call(
        _moe_kernel,
        grid=(t // MOE_TM, N_GROUPS),
        in_specs=[
            pl.BlockSpec((MOE_TM, d), row, pipeline_mode=pl.Buffered(1)),
            pl.BlockSpec((MOE_TM, MOE_TM), c2, pipeline_mode=pl.Buffered(1)),
            pl.BlockSpec(wr.shape, c2),
            pl.BlockSpec(br.shape, c2),
            pl.BlockSpec((1, d, _HID), lambda i, g: (g, 0, 0)),
            pl.BlockSpec((1, d, _HID), lambda i, g: (g, 0, 0)),
            pl.BlockSpec((1, _HID, d), lambda i, g: (g, 0, 0)),
            pl.BlockSpec(g2.shape, c2),
            pl.BlockSpec(b2.shape, c2),
        ],
        out_specs=pl.BlockSpec((MOE_TM, d), row),
        out_shape=jax.ShapeDtypeStruct((t, d), F32),
        scratch_shapes=[
            pltpu.VMEM((MOE_TM, d + LANES), BF16),
            pltpu.VMEM((MOE_TM, LANES), F32),
            pltpu.VMEM((_R_SLOT, MOE_TM), F32),
            pltpu.VMEM((_YS_ROWS, d), BF16),
            pltpu.SMEM((N_GROUPS,), jnp.int32),
            pltpu.SMEM((N_GROUPS,), jnp.int32),
        ],
        compiler_params=pltpu.CompilerParams(
            dimension_semantics=("parallel", "arbitrary"), vmem_limit_bytes=MOE_VMEM_LIMIT),
        name="moe",
    )(x1, tri, wr, br, wg, wu, wd, g2, b2)


def _rope_table(seq):
    half = ROPE_DIM // 2
    inv = ROPE_THETA ** (-jnp.arange(0, ROPE_DIM, 2, dtype=F32) / ROPE_DIM)
    ang = jnp.arange(seq, dtype=F32)[:, None] * inv[None, :]
    cos, sin = jnp.cos(ang), jnp.sin(ang)
    rest = HEAD_DIM - ROPE_DIM
    one = jnp.ones((seq, rest), F32)
    zero = jnp.zeros((seq, rest), F32)
    zh = jnp.zeros((seq, half), F32)
    c_h = jnp.concatenate([cos, cos, one], axis=1)
    dn_h = jnp.concatenate([-sin, zh, zero], axis=1)
    up_h = jnp.concatenate([zh, sin, zero], axis=1)
    rep = LANES // HEAD_DIM
    return jnp.concatenate([jnp.tile(c_h, (1, rep)), jnp.tile(dn_h, (1, rep)), jnp.tile(up_h, (1, rep))],
                           axis=1)


def _expand_cmp_weights(pe, w1, w2):
    g_n = NSA_KV_GROUPS
    w1r = w1.astype(BF16).reshape(2, CMP_STRIDE, HEAD_DIM, CMP_HIDDEN)
    w2b = w2.astype(BF16)
    w1e = jnp.concatenate(
        [jnp.stack([w1r if g == e else jnp.zeros_like(w1r) for g in range(g_n)], axis=2)
         .reshape(2, A_W, CMP_HIDDEN) for e in range(g_n)], axis=2)
    w2e = jnp.concatenate(
        [jnp.concatenate([w2b if g == e else jnp.zeros_like(w2b) for e in range(g_n)], axis=1)
         for g in range(g_n)], axis=0)
    per = pe.reshape(2, CMP_STRIDE, 1, HEAD_DIM)
    pee = jnp.broadcast_to(per, (2, CMP_STRIDE, g_n, HEAD_DIM)).reshape(2, A_W)
    pee = jnp.concatenate([pee, jnp.zeros((6, A_W), pe.dtype)], axis=0)
    return pee.astype(F32), w1e.astype(BF16), w2e.astype(BF16)


def kernel(x, mem, w_in, cmp_pe_k, cmp_w1_k, cmp_w2_k, cmp_pe_v, cmp_w1_v, cmp_w2_v, sgu_ln_g, sgu_ln_b,
           sgu_w_s, sgu_b_s, w_mem_kv, w_br_nsa, w_br_sgu, w_br_mem, w_o, ln1_g, ln1_b, w_router_group,
           b_router_group, w_router_expert, b_router_expert, w_exp_gate, w_exp_up, w_exp_down, ln2_g, ln2_b):
    batch, seq, d = x.shape
    t = batch * seq
    assert w_in.shape[0] == DEPTH == 1
    assert seq % PROJ_TM == 0 and seq // CMP_STRIDE == N_A and seq // SLC_BLOCK == N_SEL

    w = w_in[0]
    offs = np.cumsum([0, Q_W, KV_W, KV_W, KV_W, KV_W, KV_W, KV_W, GATE_W, 2 * SGU_WIDTH, MEM_WIDTH, 3 * d])
    seg = lambda i: w[:, int(offs[i]):int(offs[i + 1])]
    gate_w = jnp.pad(seg(7), ((0, 0), (0, LANES - GATE_W)))
    assert _PROJ_COLS <= 3 * d
    w_all = jnp.concatenate(
        [seg(10), seg(0), seg(1), seg(3), seg(5), seg(2), seg(4), seg(6), gate_w, seg(8), seg(9),
         jnp.zeros((d, 3 * d - _PROJ_COLS), w.dtype)], axis=1)
    rope = _rope_table(seq)
    tril = jnp.tril(jnp.ones((SGU_CHUNK, SGU_CHUNK), dtype=bool))
    ws = jnp.where(tril[None], sgu_w_s[0], 0.0).astype(BF16)
    bs = jnp.repeat(sgu_b_s[0].T, SGU_WIDTH // SGU_GROUPS, axis=1)

    x2 = x.reshape(t, d)
    (qpad, kc, ksl, kwn, vc, vslt, vwnt, gt, osgu, mq) = _proj_call(
        x2, w_all, rope, sgu_ln_g[0][None], sgu_ln_b[0][None], ws, bs, seq)

    pek, w1k, w2k = _expand_cmp_weights(cmp_pe_k[0], cmp_w1_k[0], cmp_w2_k[0])
    pev, w1v, w2v = _expand_cmp_weights(cmp_pe_v[0], cmp_w1_v[0], cmp_w2_v[0])
    kcmp, vcmp = _compress_call(kc.reshape(batch * N_A, A_W), vc.reshape(batch * N_A, A_W),
                                pek, pev, w1k, w2k, w1v, w2v, batch)

    nqb = seq // Q_BLOCK
    n_g = NSA_KV_GROUPS
    vcmpt = vcmp.reshape(batch, N_A, KV_W).transpose(0, 2, 1).reshape(batch * KV_W, N_A)
    ci = np.arange(N_A)
    sj = np.arange(N_SEL)
    overlap = ((ci[None, :] * CMP_STRIDE + CMP_LEN - 1 >= sj[:, None] * SLC_BLOCK)
               & (ci[None, :] * CMP_STRIDE <= sj[:, None] * SLC_BLOCK + SLC_BLOCK - 1)
               & (ci[None, :] < (seq - CMP_LEN) // CMP_STRIDE + 1))
    ot = jnp.asarray(overlap, dtype=BF16)
    onsa = _nsa_call(qpad, ksl, kwn, vslt, vwnt, kcmp, vcmpt, gt, ot, batch, seq)

    omem = _memattn_call(mq, mem.reshape(batch * mem.shape[1], d), w_mem_kv[0], batch, seq)

    x1 = _merge_call(x2, onsa, osgu, omem, w_all, w_br_nsa[0], w_br_sgu[0], w_br_mem[0], w_o[0],
                     ln1_g[0][None], ln1_b[0][None])

    assert EXPERTS_PER_GROUP == _R_SLOT and N_GROUPS <= _R_SLOT
    n_r = _R_SLOT + N_GROUPS * EXPERTS_PER_GROUP
    wr = jnp.concatenate([jnp.pad(w_router_group[0], ((0, 0), (0, _R_SLOT - N_GROUPS))),
                          w_router_expert[0]], axis=1)
    wr = jnp.pad(wr, ((0, 0), (0, LANES - n_r))).astype(BF16)
    br = jnp.concatenate([jnp.pad(b_router_group[0], (0, _R_SLOT - N_GROUPS)), b_router_expert[0]])
    br = jnp.pad(br, (0, LANES - n_r))[None]
    wg = w_exp_gate[0].transpose(0, 2, 1, 3).reshape(N_GROUPS, d, _HID).astype(BF16)
    wu = w_exp_up[0].transpose(0, 2, 1, 3).reshape(N_GROUPS, d, _HID).astype(BF16)
    wd = w_exp_down[0].reshape(N_GROUPS, _HID, d).astype(BF16)
    out = _moe_call(x1, wr, br, wg, wu, wd, ln2_g[0][None], ln2_b[0][None])
    return out.reshape(batch, seq, d)
```

```python
import functools

import numpy as np
import jax
import jax.numpy as jnp
from jax import lax
from jax.experimental import pallas as pl
from jax.experimental.pallas import tpu as pltpu

NSA_HEADS = 8
NSA_KV_GROUPS = 2
NSA_HPG = NSA_HEADS // NSA_KV_GROUPS
HEAD_DIM = 64
CMP_LEN = 32
CMP_STRIDE = 16
CMP_HIDDEN = 256
SLC_BLOCK = 64
SLC_TOPN = 8
WINDOW = 512
Q_BLOCK = 128
N_BAND = WINDOW // Q_BLOCK
ROPE_THETA = 500000.0
ROPE_DIM = HEAD_DIM // 4
SGU_CHUNK = 128
SGU_GROUPS = 8
SGU_WIDTH = 512
MEM_HEADS = 4
MEM_HEAD_DIM = 128
MEM_WIDTH = MEM_HEADS * MEM_HEAD_DIM
N_GROUPS = 4
EXPERTS_PER_GROUP = 8
EXPERT_FF = 256
DEPTH = 1
DN_ALPHA = (2.0 * DEPTH) ** 0.25
LN_EPS = 1e-5
NEG = -1e30
LOG2E = 1.4426950408889634

LANES = 128
Q_W = NSA_HEADS * HEAD_DIM
KV_W = NSA_KV_GROUPS * HEAD_DIM
GATE_W = NSA_HEADS * 3
VMEM_LIMIT = 56 * 1024 * 1024
MOE_VMEM_LIMIT = 60 * 1024 * 1024

BF16 = jnp.bfloat16
F32 = jnp.float32


def _dot(a, b):
    return jnp.dot(a, b, preferred_element_type=F32)


def _dot_nt(a, b):
    return lax.dot_general(a, b, (((1,), (1,)), ((), ())), preferred_element_type=F32)


def _sigmoid(x):
    return 1.0 / (1.0 + jnp.exp(-x))


def _gelu(x):
    return 0.5 * x * (1.0 + lax.erf(x * (2.0 ** -0.5)))


def _layer_norm(x, g, b):
    mu = jnp.mean(x, axis=-1, keepdims=True)
    xc = x - mu
    var = jnp.mean(xc * xc, axis=-1, keepdims=True)
    return xc * lax.rsqrt(var + LN_EPS) * g + b


PACK_TR = 256


def _pack_kernel(segments, n_full, w_ref, tail_ref, o_ref):
    lane = lax.broadcasted_iota(jnp.int32, (PACK_TR, LANES), 1)

    def src_block(a):
        return w_ref[:, a * LANES:(a + 1) * LANES] if a < n_full else tail_ref[...]

    for dst0, src0, width, valid in segments:
        for j in range(width // LANES):
            dst = slice(dst0 + j * LANES, dst0 + (j + 1) * LANES)
            if src0 is None:
                o_ref[:, dst] = jnp.zeros((PACK_TR, LANES), BF16)
                continue
            a, sh = divmod(src0 + j * LANES, LANES)
            if sh == 0:
                blk = src_block(a)
            else:
                blk = jnp.where(lane < LANES - sh, pltpu.roll(src_block(a), LANES - sh, 1),
                                pltpu.roll(src_block(a + 1), LANES - sh, 1))
            if valid < LANES:
                blk = jnp.where(lane < valid, blk, 0.0)
            o_ref[:, dst] = blk.astype(BF16)


def _pack_call(w, segments, out_cols):
    d, n_in = w.shape
    n_full = n_in // LANES
    tail = jnp.pad(w[:, n_full * LANES:], ((0, 0), (0, (n_full + 1) * LANES - n_in)))
    return pl.pallas_call(
        functools.partial(_pack_kernel, segments, n_full),
        grid=(d // PACK_TR,),
        in_specs=[pl.BlockSpec((PACK_TR, n_full * LANES), lambda i: (i, 0)),
                  pl.BlockSpec((PACK_TR, LANES), lambda i: (i, 0))],
        out_specs=pl.BlockSpec((PACK_TR, out_cols), lambda i: (i, 0)),
        out_shape=jax.ShapeDtypeStruct((d, out_cols), BF16),
        compiler_params=pltpu.CompilerParams(
            dimension_semantics=("parallel",), vmem_limit_bytes=VMEM_LIMIT),
        name="pack",
    )(w, tail)


PROJ_TM = 512
_ROPE_COLS = Q_W + 3 * KV_W
_V_OFF = _ROPE_COLS
_G_OFF = _V_OFF + 3 * KV_W
_SGU_OFF = _G_OFF + LANES
_MQ_OFF = _SGU_OFF + 2 * SGU_WIDTH
_PROJ_COLS = _MQ_OFF + MEM_WIDTH


def _proj_kernel(x_ref, w_ref, rope_ref, lng_ref, lnb_ref, ws_ref, bs_ref,
                 q_ref, kc_ref, ksl_ref, kwn_ref, vc_ref, vsl_ref, vwn_ref, gate_ref, osgu_ref, mq_ref):
    xb = x_ref[...].astype(BF16)
    lane = lax.broadcasted_iota(jnp.int32, (PROJ_TM, LANES), 1)
    low = lane < HEAD_DIM
    cos = rope_ref[:, 0:LANES]
    s_dn = rope_ref[:, LANES:2 * LANES]
    s_up = rope_ref[:, 2 * LANES:3 * LANES]

    h = _dot(xb, w_ref[:, 0:_ROPE_COLS])
    k_refs = (kc_ref, ksl_ref, kwn_ref)
    for j in range(_ROPE_COLS // LANES):
        blk = h[:, j * LANES:(j + 1) * LANES]
        r = (blk * cos + pltpu.roll(blk, LANES - ROPE_DIM // 2, 1) * s_dn
             + pltpu.roll(blk, ROPE_DIM // 2, 1) * s_up)
        if j < Q_W // LANES:
            r = r * (HEAD_DIM ** -0.5 * LOG2E)
            sw = pltpu.roll(r, HEAD_DIM, 1)
            g = (2 * j) // NSA_HPG
            if g == 0:
                h0 = jnp.where(low, r, 0.0)
                h1 = jnp.where(low, sw, 0.0)
            else:
                h0 = jnp.where(low, 0.0, sw)
                h1 = jnp.where(low, 0.0, r)
            q_ref[:, (2 * j) * LANES:(2 * j + 1) * LANES] = h0.astype(BF16)
            q_ref[:, (2 * j + 1) * LANES:(2 * j + 2) * LANES] = h1.astype(BF16)
        else:
            k_refs[j - Q_W // LANES][...] = r.astype(BF16)

    hv = _dot(xb, w_ref[:, _V_OFF:_G_OFF])
    vc_ref[...] = hv[:, 0:LANES].astype(BF16)
    ones = jnp.ones((V_ROWS - HEAD_DIM, PROJ_TM), BF16)
    for v_ref, col0, blk in ((vsl_ref, LANES, SEL_CHUNK), (vwn_ref, 2 * LANES, Q_BLOCK)):
        vt = hv[:, col0:col0 + LANES].T.astype(BF16)
        for g in range(NSA_KV_GROUPS):
            for b in range(PROJ_TM // blk):
                v_ref[g, b, 0:HEAD_DIM, :] = vt[g * HEAD_DIM:(g + 1) * HEAD_DIM, b * blk:(b + 1) * blk]
                v_ref[g, b, HEAD_DIM:V_ROWS, :] = ones[:, 0:blk]

    gate_ref[...] = _sigmoid(_dot(xb, w_ref[:, _G_OFF:_SGU_OFF])).T

    mq_ref[...] = _dot(xb, w_ref[:, _MQ_OFF:_PROJ_COLS]).astype(BF16)

    z = _gelu(_dot(xb, w_ref[:, _SGU_OFF:_MQ_OFF]))
    u = z[:, 0:SGU_WIDTH]
    v = _layer_norm(z[:, SGU_WIDTH:2 * SGU_WIDTH], lng_ref[...], lnb_ref[...]).astype(BF16)
    lane_c = lax.broadcasted_iota(jnp.int32, (SGU_CHUNK, LANES), 1)
    low_c = lane_c < (SGU_WIDTH // SGU_GROUPS)
    n_ch = PROJ_TM // SGU_CHUNK
    for gp in range(SGU_WIDTH // LANES):
        cols = slice(gp * LANES, (gp + 1) * LANES)
        vcat = jnp.concatenate([v[ci * SGU_CHUNK:(ci + 1) * SGU_CHUNK, cols] for ci in range(n_ch)], axis=1)
        sv0 = _dot(ws_ref[2 * gp], vcat)
        sv1 = _dot(ws_ref[2 * gp + 1], vcat)
        for ci in range(n_ch):
            rows = slice(ci * SGU_CHUNK, (ci + 1) * SGU_CHUNK)
            lanes = slice(ci * LANES, (ci + 1) * LANES)
            sv = jnp.where(low_c, sv0[:, lanes], sv1[:, lanes]) + bs_ref[:, cols]
            osgu_ref[rows, cols] = (u[rows, cols] * sv).astype(BF16)


def _proj_call(x2, wp, rope, lng, lnb, ws, bs, seq):
    t = x2.shape[0]
    d = x2.shape[1]
    nt = t // PROJ_TM
    per_seq = seq // PROJ_TM
    row = lambda i: (i, 0)
    const2 = lambda i: (0, 0)
    assert PROJ_TM % SEL_CHUNK == 0 and PROJ_TM % Q_BLOCK == 0
    n_g = NSA_KV_GROUPS
    batch = t // seq
    vblock = lambda i: (i // per_seq, i % per_seq, 0, 0)

    def v_shape(blk):
        return jax.ShapeDtypeStruct((batch * n_g, seq // blk, V_ROWS, blk), BF16)

    def v_spec(blk):
        return pl.BlockSpec((n_g, PROJ_TM // blk, V_ROWS, blk), vblock)

    out_shapes = (
        jax.ShapeDtypeStruct((t, NSA_HEADS * LANES), BF16),
        *[jax.ShapeDtypeStruct((t, KV_W), BF16) for _ in range(4)],
        v_shape(SEL_CHUNK), v_shape(Q_BLOCK),
        jax.ShapeDtypeStruct((LANES, t), F32),
        jax.ShapeDtypeStruct((t, SGU_WIDTH), BF16),
        jax.ShapeDtypeStruct((t, MEM_WIDTH), BF16),
    )
    out_specs = (
        pl.BlockSpec((PROJ_TM, NSA_HEADS * LANES), row),
        *[pl.BlockSpec((PROJ_TM, KV_W), row) for _ in range(4)],
        v_spec(SEL_CHUNK), v_spec(Q_BLOCK),
        pl.BlockSpec((LANES, PROJ_TM), lambda i: (0, i)),
        pl.BlockSpec((PROJ_TM, SGU_WIDTH), row),
        pl.BlockSpec((PROJ_TM, MEM_WIDTH), row),
    )
    return pl.pallas_call(
        _proj_kernel,
        grid=(nt,),
        in_specs=[
            pl.BlockSpec((PROJ_TM, d), row),
            pl.BlockSpec((d, 3 * d), lambda i: (0, 1)),
            pl.BlockSpec((PROJ_TM, 3 * LANES), lambda i: (i % per_seq, 0)),
            pl.BlockSpec((1, SGU_WIDTH), const2),
            pl.BlockSpec((1, SGU_WIDTH), const2),
            pl.BlockSpec((SGU_GROUPS, SGU_CHUNK, SGU_CHUNK), lambda i: (0, 0, 0)),
            pl.BlockSpec((SGU_CHUNK, SGU_WIDTH), const2),
        ],
        out_specs=out_specs,
        out_shape=out_shapes,
        compiler_params=pltpu.CompilerParams(
            dimension_semantics=("parallel",), vmem_limit_bytes=VMEM_LIMIT),
        name="proj",
    )(x2, wp, rope, lng, lnb, ws, bs)


N_A = 128
A_W = CMP_STRIDE * KV_W


def _compress_kernel(ka_ref, va_ref, pek_ref, pev_ref, w1k_ref, w2k_ref, w1v_ref, w2v_ref,
                     kcmp_ref, vcmp_ref):
    def one(a_ref, pe_ref, w1_ref, w2_ref, out_ref):
        a = a_ref[...].astype(F32)
        top = (a + pe_ref[0:1, :]).astype(BF16)
        bot = (a + pe_ref[1:2, :]).astype(BF16)
        h1 = _dot(top, w1_ref[0])
        h2 = _dot(bot, w1_ref[1])
        pre = h1 + pltpu.roll(h2, N_A - 1, 0)
        act = _gelu(pre).astype(BF16)
        out_ref[...] = _dot(act, w2_ref[...]).astype(BF16)

    one(ka_ref, pek_ref, w1k_ref, w2k_ref, kcmp_ref)
    one(va_ref, pev_ref, w1v_ref, w2v_ref, vcmp_ref)


def _compress_call(ka, va, pek, pev, w1k, w2k, w1v, w2v, batch):
    row = lambda b: (b, 0)
    c2 = lambda b: (0, 0)
    c3 = lambda b: (0, 0, 0)
    hid2 = NSA_KV_GROUPS * CMP_HIDDEN
    return pl.pallas_call(
        _compress_kernel,
        grid=(batch,),
        in_specs=[
            pl.BlockSpec((N_A, A_W), row),
            pl.BlockSpec((N_A, A_W), row),
            pl.BlockSpec((8, A_W), c2),
            pl.BlockSpec((8, A_W), c2),
            pl.BlockSpec((2, A_W, hid2), c3),
            pl.BlockSpec((hid2, KV_W), c2),
            pl.BlockSpec((2, A_W, hid2), c3),
            pl.BlockSpec((hid2, KV_W), c2),
        ],
        out_specs=(pl.BlockSpec((N_A, KV_W), row), pl.BlockSpec((N_A, KV_W), row)),
        out_shape=(jax.ShapeDtypeStruct((batch * N_A, KV_W), BF16),
                   jax.ShapeDtypeStruct((batch * N_A, KV_W), BF16)),
        compiler_params=pltpu.CompilerParams(
            dimension_semantics=("parallel",), vmem_limit_bytes=VMEM_LIMIT),
        name="compress",
    )(ka, va, pek, pev, w1k, w2k, w1v, w2v)


N_SEL = 32
SEL_CHUNK = 512
SEL_SUB = 128
HQ = NSA_HPG * Q_BLOCK
V_ROWS = HEAD_DIM + 16


def _nsa_kernel(q_ref, ksl_ref, kwn_ref, vslt_ref, vwnt_ref, kcmp_ref, vcmpt_ref, gt_ref,
                ot_ref, o_ref, m_s, acc_s, ocmp_s, owin_s, cap_s, out_s):
    c = pl.program_id(1)
    lane_hq = lax.broadcasted_iota(jnp.int32, (1, HQ), 1)
    pos_hq = c * Q_BLOCK + (lane_hq & (Q_BLOCK - 1))
    pos_q = c * Q_BLOCK + lax.broadcasted_iota(jnp.int32, (1, Q_BLOCK), 1)
    groups = range(NSA_KV_GROUPS)
    qgs = [jnp.concatenate(
        [q_ref[:, (g * NSA_HPG + hh) * LANES:(g * NSA_HPG + hh + 1) * LANES]
         for hh in range(NSA_HPG)], axis=0) for g in groups]

    kcmp = kcmp_ref[...]
    s_c = [_dot_nt(kcmp, qgs[g]) for g in groups]
    n_idx = lax.broadcasted_iota(jnp.int32, (N_A, HQ), 0)
    valid_c = (n_idx * CMP_STRIDE + (CMP_LEN - 1)) <= pos_hq
    p_c = []
    for g in groups:
        sm_c = jnp.where(valid_c, s_c[g], NEG)
        m_c = jnp.max(sm_c, axis=0, keepdims=True)
        e_c = jnp.where(valid_c, jnp.exp2(sm_c - m_c), 0.0)
        d_c = jnp.sum(e_c, axis=0, keepdims=True)
        p_c.append(e_c / jnp.where(d_c > 0, d_c, 1.0))
    for g in groups:
        ocmp_s[g] = _dot(vcmpt_ref[g * HEAD_DIM:(g + 1) * HEAD_DIM, :], p_c[g].astype(BF16))

    ot = ot_ref[...]
    imp = []
    for g in groups:
        ps = (p_c[g][:, 0:Q_BLOCK] + p_c[g][:, Q_BLOCK:2 * Q_BLOCK]
              + p_c[g][:, 2 * Q_BLOCK:3 * Q_BLOCK] + p_c[g][:, 3 * Q_BLOCK:4 * Q_BLOCK])
        p_hi = ps.astype(BF16)
        r1 = ps - p_hi.astype(F32)
        p_lo = r1.astype(BF16)
        p_lo2 = (r1 - p_lo.astype(F32)).astype(BF16)
        imp.append(_dot(ot, p_hi) + _dot(ot, p_lo) + _dot(ot, p_lo2))

    j_idx = lax.broadcasted_iota(jnp.int32, (N_SEL, Q_BLOCK), 0)
    cur = pos_q // SLC_BLOCK
    future = j_idx > cur
    forced = (j_idx == 0) | (j_idx == cur) | (j_idx == cur - 1)
    imp = [jnp.where(future, NEG, jnp.where(forced, -NEG, imp[g])) for g in groups]
    rank = [jnp.zeros((N_SEL, Q_BLOCK), F32) for g in groups]
    for i in range(N_SEL):
        for g in groups:
            row = imp[g][i:i + 1, :]
            beats = (row > imp[g]) | ((row == imp[g]) & (j_idx > i))
            rank[g] = rank[g] + jnp.where(beats, 1.0, 0.0)
    for g in groups:
        cap_s[g] = jnp.where(rank[g] < float(SLC_TOPN), -NEG, NEG)
        m_s[g] = jnp.full((1, HQ), NEG, F32)
        acc_s[g] = jnp.zeros((V_ROWS, HQ), F32)

    _nsa_window(c, qgs, kwn_ref, vwnt_ref, owin_s)

    blocks_per_sub = SEL_SUB // SLC_BLOCK
    subs_per_chunk = SEL_CHUNK // SEL_SUB

    def sel_chunk(kc):
        spans = [slice(kc * SEL_CHUNK + a * SEL_SUB, kc * SEL_CHUNK + (a + 1) * SEL_SUB)
                 for a in range(subs_per_chunk)]
        s = [[_dot_nt(ksl_ref[ks, :], qgs[g]) for g in groups] for ks in spans]
        for a, ks in enumerate(spans):
            kpos = ks.start + lax.broadcasted_iota(jnp.int32, (SEL_SUB, Q_BLOCK), 0)
            causal = kpos <= pos_q
            for g in groups:
                j0 = ks.start // SLC_BLOCK
                cap = jnp.concatenate(
                    [jnp.broadcast_to(cap_s[g, j:j + 1, :], (SLC_BLOCK, Q_BLOCK))
                     for j in range(j0, j0 + blocks_per_sub)], axis=0)
                cap = jnp.where(causal, cap, NEG)
                sm = jnp.minimum(s[a][g], jnp.concatenate([cap] * NSA_HPG, axis=1))
                m_old = m_s[g]
                m_new = jnp.maximum(m_old, jnp.max(sm, axis=0, keepdims=True))
                alpha = jnp.exp2(m_old - m_new)
                e = jnp.exp2((sm - m_new).astype(BF16))
                v_t = vslt_ref[g, kc, :, a * SEL_SUB:(a + 1) * SEL_SUB]
                acc_s[g] = alpha * acc_s[g] + _dot(v_t, e)
                m_s[g] = m_new

    sel_chunk(0)
    for kc in range(1, ksl_ref.shape[0] // SEL_CHUNK):
        pl.when(kc * (SEL_CHUNK // Q_BLOCK) <= c)(functools.partial(sel_chunk, kc))

    for g in groups:
        acc = acc_s[g]
        o_sel = acc[0:HEAD_DIM, :] * (1.0 / acc[HEAD_DIM:HEAD_DIM + 1, :])

        def gate_row(br):
            return jnp.concatenate(
                [gt_ref[(g * NSA_HPG + hh) * 3 + br:(g * NSA_HPG + hh) * 3 + br + 1, :]
                 for hh in range(NSA_HPG)], axis=1)
        o_t = gate_row(0) * ocmp_s[g] + gate_row(1) * o_sel + gate_row(2) * owin_s[g]
        for hh in range(NSA_HPG):
            h = g * NSA_HPG + hh
            out_s[h * HEAD_DIM:(h + 1) * HEAD_DIM, :] = o_t[:, hh * Q_BLOCK:(hh + 1) * Q_BLOCK]

    o_ref[...] = out_s[...].T.astype(BF16)


def _nsa_window(c, qgs, kwn_ref, vwnt_ref, owin_s):
    groups = range(NSA_KV_GROUPS)
    q_i = lax.broadcasted_iota(jnp.int32, (Q_BLOCK, Q_BLOCK), 1)
    k_i = lax.broadcasted_iota(jnp.int32, (Q_BLOCK, Q_BLOCK), 0)
    blks, caps = [], []
    for i in range(N_BAND + 1):
        blk = c - N_BAND + i
        blks.append(jnp.maximum(blk, 0))
        off = (N_BAND - i) * Q_BLOCK
        if off - (Q_BLOCK - 1) >= 0 and off + (Q_BLOCK - 1) < WINDOW:
            caps.append(jnp.where(blk >= 0, -NEG, NEG))
        else:
            diff = off + q_i - k_i
            ok = (diff >= 0) & (diff < WINDOW) & (blk >= 0)
            caps.append(jnp.concatenate([jnp.where(ok, -NEG, NEG)] * NSA_HPG, axis=1))
    k_blocks = [kwn_ref[pl.ds(pl.multiple_of(blks[i] * Q_BLOCK, Q_BLOCK), Q_BLOCK), :]
                for i in range(N_BAND + 1)]
    s_w = [[None] * (N_BAND + 1) for g in groups]
    for i in reversed(range(N_BAND + 1)):
        for g in groups:
            s_w[g][i] = _dot_nt(k_blocks[i], qgs[g])
    m_w = [None for g in groups]
    o_win = [None for g in groups]
    for i in reversed(range(N_BAND + 1)):
        for g in groups:
            sm = jnp.minimum(s_w[g][i], caps[i])
            m_blk = jnp.max(sm, axis=0, keepdims=True)
            m_new = m_blk if m_w[g] is None else jnp.maximum(m_w[g], m_blk)
            pv = _dot(vwnt_ref[g, blks[i]], jnp.exp2((sm - m_new).astype(BF16)))
            o_win[g] = pv if m_w[g] is None else jnp.exp2(m_w[g] - m_new) * o_win[g] + pv
            m_w[g] = m_new
    for g in groups:
        owin_s[g] = o_win[g][0:HEAD_DIM, :] * (1.0 / o_win[g][HEAD_DIM:HEAD_DIM + 1, :])


def _nsa_call(qpad, ksl, kwn, vslt, vwnt, kcmp, vcmpt, gt, ot, batch, seq):
    nqb = seq // Q_BLOCK
    n_g = NSA_KV_GROUPS
    qrow = lambda b, c: (b * nqb + c, 0)
    brow = lambda b, c: (b, 0)
    c2 = lambda b, c: (0, 0)
    return pl.pallas_call(
        _nsa_kernel,
        grid=(batch, nqb),
        in_specs=[
            pl.BlockSpec((Q_BLOCK, NSA_HEADS * LANES), qrow),
            pl.BlockSpec((seq, KV_W), brow),
            pl.BlockSpec((seq, KV_W), brow),
            pl.BlockSpec((n_g, seq // SEL_CHUNK, V_ROWS, SEL_CHUNK), lambda b, c: (b, 0, 0, 0)),
            pl.BlockSpec((n_g, nqb, V_ROWS, Q_BLOCK), lambda b, c: (b, 0, 0, 0)),
            pl.BlockSpec((N_A, KV_W), brow),
            pl.BlockSpec((KV_W, N_A), brow),
            pl.BlockSpec((LANES, Q_BLOCK), lambda b, c: (0, b * nqb + c)),
            pl.BlockSpec((N_SEL, N_A), c2),
        ],
        out_specs=pl.BlockSpec((Q_BLOCK, Q_W), qrow),
        out_shape=jax.ShapeDtypeStruct((batch * seq, Q_W), BF16),
        scratch_shapes=[
            pltpu.VMEM((n_g, 1, HQ), F32),
            pltpu.VMEM((n_g, V_ROWS, HQ), F32),
            pltpu.VMEM((n_g, HEAD_DIM, HQ), F32),
            pltpu.VMEM((n_g, HEAD_DIM, HQ), F32),
            pltpu.VMEM((n_g, N_SEL, Q_BLOCK), F32),
            pltpu.VMEM((Q_W, Q_BLOCK), F32),
        ],
        compiler_params=pltpu.CompilerParams(
            dimension_semantics=("parallel", "arbitrary"), vmem_limit_bytes=VMEM_LIMIT),
        name="nsa",
    )(qpad, ksl, kwn, vslt, vwnt, kcmp, vcmpt, gt, ot)


MEM_TM = 512
MEM_V_ROWS = MEM_HEAD_DIM + 16


def _memattn_kernel(mq_ref, mem_ref, wkv_ref, o_ref, k_s, vt_s, out_s):
    @pl.when(pl.program_id(1) == 0)
    def _():
        kv = _dot(mem_ref[...].astype(BF16), wkv_ref[...].astype(BF16))
        k_s[...] = kv[:, 0:MEM_WIDTH].astype(BF16)
        vt = kv[:, MEM_WIDTH:2 * MEM_WIDTH].T.astype(BF16)
        for h in range(MEM_HEADS):
            vt_s[h, 0:MEM_HEAD_DIM, :] = vt[h * MEM_HEAD_DIM:(h + 1) * MEM_HEAD_DIM, :]
            vt_s[h, MEM_HEAD_DIM:, :] = jnp.ones((MEM_V_ROWS - MEM_HEAD_DIM, vt.shape[1]), BF16)

    heads = range(MEM_HEADS)
    cols = [slice(h * MEM_HEAD_DIM, (h + 1) * MEM_HEAD_DIM) for h in heads]
    s_t = [_dot_nt(k_s[:, cols[h]], mq_ref[:, cols[h]]) for h in heads]
    e = []
    for h in heads:
        m = jnp.max(s_t[h], axis=0, keepdims=True)
        e.append(jnp.exp2((s_t[h] - m) * (MEM_HEAD_DIM ** -0.5 * LOG2E)).astype(BF16))
    o_t = [_dot(vt_s[h], e[h]) for h in heads]
    for h in heads:
        out_s[cols[h], :] = o_t[h][0:MEM_HEAD_DIM, :] * (1.0 / o_t[h][MEM_HEAD_DIM:MEM_HEAD_DIM + 1, :])
    o_ref[...] = out_s[...].T.astype(BF16)


def _memattn_call(mq, mem2, wkv, batch, seq):
    m_len = mem2.shape[0] // batch
    d = mem2.shape[1]
    per = seq // MEM_TM
    return pl.pallas_call(
        _memattn_kernel,
        grid=(batch, per),
        in_specs=[
            pl.BlockSpec((MEM_TM, MEM_WIDTH), lambda b, i: (b * per + i, 0)),
            pl.BlockSpec((m_len, d), lambda b, i: (b, 0)),
            pl.BlockSpec((d, 2 * MEM_WIDTH), lambda b, i: (0, 0)),
        ],
        out_specs=pl.BlockSpec((MEM_TM, MEM_WIDTH), lambda b, i: (b * per + i, 0)),
        out_shape=jax.ShapeDtypeStruct((batch * seq, MEM_WIDTH), BF16),
        scratch_shapes=[
            pltpu.VMEM((m_len, MEM_WIDTH), BF16),
            pltpu.VMEM((MEM_HEADS, MEM_V_ROWS, m_len), BF16),
            pltpu.VMEM((MEM_WIDTH, MEM_TM), F32),
        ],
        compiler_params=pltpu.CompilerParams(
            dimension_semantics=("parallel", "arbitrary"), vmem_limit_bytes=VMEM_LIMIT),
        name="memattn",
    )(mq, mem2, wkv)


MERGE_TM = 512


def _merge_kernel(x_ref, onsa_ref, osgu_ref, omem_ref, wg_ref, wbn32_ref, wbs32_ref, wbm32_ref, wo32_ref,
                  g_ref, b_ref, out_ref, wbn_ref, wbs_ref, wbm_ref, wo_ref):
    @pl.when(pl.program_id(0) == 0)
    def _():
        for dst, src in ((wbn_ref, wbn32_ref), (wbs_ref, wbs32_ref), (wbm_ref, wbm32_ref),
                         (wo_ref, wo32_ref)):
            dst[...] = src[...].astype(BF16)

    x = x_ref[...]
    xb = x.astype(BF16)
    d = x.shape[1]
    y = None
    for br, (o_r, w_r) in enumerate(((onsa_ref, wbn_ref), (osgu_ref, wbs_ref), (omem_ref, wbm_ref))):
        gate = _sigmoid(_dot(xb, wg_ref[:, br * d:(br + 1) * d]))
        term = gate * _dot(o_r[...], w_r[...])
        y = term if y is None else y + term
    z = DN_ALPHA * x + _dot(y.astype(BF16), wo_ref[...])
    out_ref[...] = _layer_norm(z, g_ref[...], b_ref[...])


def _merge_call(x2, onsa, osgu, omem, wg, wbn, wbs, wbm, wo, g1, b1):
    t, d = x2.shape
    row = lambda i: (i, 0)
    c2 = lambda i: (0, 0)
    full = lambda a: pl.BlockSpec(a.shape, c2)
    once = lambda a: pl.BlockSpec(a.shape, c2, pipeline_mode=pl.Buffered(1))
    return pl.pallas_call(
        _merge_kernel,
        grid=(t // MERGE_TM,),
        in_specs=[
            pl.BlockSpec((MERGE_TM, d), row),
            pl.BlockSpec((MERGE_TM, onsa.shape[1]), row),
            pl.BlockSpec((MERGE_TM, osgu.shape[1]), row),
            pl.BlockSpec((MERGE_TM, omem.shape[1]), row),
            pl.BlockSpec((d, 3 * d), c2),
            once(wbn), once(wbs), once(wbm), once(wo), full(g1), full(b1),
        ],
        out_specs=pl.BlockSpec((MERGE_TM, d), row),
        out_shape=jax.ShapeDtypeStruct((t, d), F32),
        scratch_shapes=[pltpu.VMEM(a.shape, BF16) for a in (wbn, wbs, wbm, wo)],
        compiler_params=pltpu.CompilerParams(
            dimension_semantics=("arbitrary",), vmem_limit_bytes=VMEM_LIMIT),
        name="merge",
    )(x2, onsa, osgu, omem, wg, wbn, wbs, wbm, wo, g1, b1)


MOE_TM = 1024
MOE_CH = 128
_YS_ROWS = MOE_TM + N_GROUPS * MOE_CH
_R_SLOT = 8
_DEST_LANE = 3 * _R_SLOT
_HID = EXPERTS_PER_GROUP * EXPERT_FF


def _moe_kernel(x_ref, tri_ref, wr_ref, br_ref, wg_ref, wu_ref, wd_ref, g_ref, b_ref, out_ref,
                xa_s, tok_s, keyr_s, ys_s, cnt_s, base_s):
    grp = pl.program_id(1)
    tm = x_ref.shape[0]
    d = x_ref.shape[1]

    @pl.when(grp == 0)
    def _route():
        xb = x_ref[...].astype(BF16)
        xa_s[:, 0:d] = xb
        logits = _dot(xb, wr_ref[...]) + br_ref[...]
        lt = logits.T
        row = lax.broadcasted_iota(jnp.int32, (_R_SLOT, tm), 0)
        gl = jnp.where(row < N_GROUPS, lt[0:_R_SLOT], NEG)
        gmax = jnp.max(gl, axis=0, keepdims=True)
        gidx = jnp.min(jnp.where(gl == gmax, row, _R_SLOT), axis=0, keepdims=True)
        gprob = 1.0 / jnp.sum(jnp.exp(gl - gmax), axis=0, keepdims=True)
        el = lt[_R_SLOT:2 * _R_SLOT]
        for k in range(1, N_GROUPS):
            el = jnp.where(gidx == k, lt[(k + 1) * _R_SLOT:(k + 2) * _R_SLOT], el)
        ee = jnp.exp(el - jnp.max(el, axis=0, keepdims=True))
        ep = ee / jnp.sum(ee, axis=0, keepdims=True)
        t1 = jnp.max(ep, axis=0, keepdims=True)
        i1 = jnp.min(jnp.where(ep == t1, row, _R_SLOT), axis=0, keepdims=True)
        rest = row != i1
        t2 = jnp.max(jnp.where(rest, ep, -1.0), axis=0, keepdims=True)
        i2 = jnp.min(jnp.where(rest & (ep == t2), row, _R_SLOT), axis=0, keepdims=True)
        den = t1 + t2
        cwf = (jnp.where(row == i1, t1 / den, 0.0) + jnp.where(row == i2, t2 / den, 0.0)) * gprob
        hi = cwf.astype(BF16).astype(F32)
        r1 = cwf - hi
        lo = r1.astype(BF16).astype(F32)
        lo2 = (r1 - lo).astype(BF16).astype(F32)

        onehot = jnp.where(row == gidx, 1.0, 0.0)
        rank = _dot_nt(onehot.astype(BF16), tri_ref[...])
        keyr_s[...] = jnp.where(onehot > 0.5, rank, -1.0)
        base = jnp.int32(0)
        basev = jnp.zeros((1, tm), F32)
        for k in range(N_GROUPS):
            n_k = jnp.sum(onehot[k:k + 1, :]).astype(jnp.int32)
            cnt_s[k] = n_k
            base_s[k] = base
            basev = jnp.where(gidx == k, base.astype(F32), basev)
            base = base + ((n_k + MOE_CH - 1) // MOE_CH) * MOE_CH
        dest = jnp.sum(rank * onehot, axis=0, keepdims=True) + basev
        tok = jnp.concatenate(
            [hi, lo, lo2, jnp.broadcast_to(dest, (_R_SLOT, tm)),
             jnp.zeros((LANES - 4 * _R_SLOT, tm), F32)], axis=0).T
        tok_s[...] = tok
        lane = lax.broadcasted_iota(jnp.int32, tok.shape, 1)
        xa_s[:, d:d + LANES] = jnp.where(lane < _DEST_LANE, tok, 0.0).astype(BF16)
        ys_s[...] = jnp.zeros(ys_s.shape, BF16)

    n_rows = cnt_s[grp]
    row0 = base_s[grp]
    keyr = keyr_s[pl.ds(grp, 1), :]
    half = _HID // 2

    def sweep(k, carry):
        ch = MOE_CH
        r_row = (lax.broadcasted_iota(jnp.int32, (ch, tm), 0) + k * MOE_CH).astype(F32)
        pick = jnp.where(keyr == r_row, 1.0, 0.0).astype(BF16)
        ga = _dot(pick, xa_s[...])
        xg = ga[:, 0:d].astype(BF16)
        cwg = ga[:, d:d + LANES]
        cs = (cwg + pltpu.roll(cwg, LANES - _R_SLOT, 1)
              + pltpu.roll(cwg, LANES - 2 * _R_SLOT, 1))
        hg = [_dot(xg, wg_ref[0, :, h * half:(h + 1) * half]) for h in range(2)]
        hu = [_dot(xg, wu_ref[0, :, h * half:(h + 1) * half]) for h in range(2)]
        y = None
        for h in range(2):
            per_half = EXPERTS_PER_GROUP // 2
            cexp = jnp.concatenate(
                [jnp.broadcast_to(cs[:, e:e + 1], (ch, EXPERT_FF))
                 for e in range(h * per_half, (h + 1) * per_half)], axis=1)
            hid = ((hg[h] * _sigmoid(hg[h])) * hu[h]) * cexp
            term = _dot(hid.astype(BF16), wd_ref[0, h * half:(h + 1) * half, :])
            y = term if y is None else y + term
        ys_s[pl.ds(pl.multiple_of(row0 + k * MOE_CH, MOE_CH), ch), :] = y.astype(BF16)
        return carry

    lax.fori_loop(0, (n_rows + MOE_CH - 1) // MOE_CH, sweep, 0)

    @pl.when(grp == N_GROUPS - 1)
    def _fin():
        r_col = lax.broadcasted_iota(jnp.int32, (tm, _YS_ROWS), 1).astype(F32)
        put = jnp.where(tok_s[:, _DEST_LANE:_DEST_LANE + 1] == r_col, 1.0, 0.0).astype(BF16)
        z = DN_ALPHA * x_ref[...] + _dot(put, ys_s[...])
        out_ref[...] = _layer_norm(z, g_ref[...], b_ref[...])


def _moe_call(x1, wr, br, wg, wu, wd, g2, b2):
    t, d = x1.shape
    row = lambda i, g: (i, 0)
    c2 = lambda i, g: (0, 0)
    idx = np.arange(MOE_TM)
    tri = jnp.asarray(idx[None, :] < idx[:, None], dtype=BF16)
    return pl.pallas_call(
        _moe_kernel,
        grid=(t // MOE_TM, N_GROUPS),
        in_specs=[
            pl.BlockSpec((MOE_TM, d), row, pipeline_mode=pl.Buffered(1)),
            pl.BlockSpec((MOE_TM, MOE_TM), c2, pipeline_mode=pl.Buffered(1)),
            pl.BlockSpec(wr.shape, c2),
            pl.BlockSpec(br.shape, c2),
            pl.BlockSpec((1, d, _HID), lambda i, g: (g, 0, 0)),
            pl.BlockSpec((1, d, _HID), lambda i, g: (g, 0, 0)),
            pl.BlockSpec((1, _HID, d), lambda i, g: (g, 0, 0)),
            pl.BlockSpec(g2.shape, c2),
            pl.BlockSpec(b2.shape, c2),
        ],
        out_specs=pl.BlockSpec((MOE_TM, d), row),
        out_shape=jax.ShapeDtypeStruct((t, d), F32),
        scratch_shapes=[
            pltpu.VMEM((MOE_TM, d + LANES), BF16),
            pltpu.VMEM((MOE_TM, LANES), F32),
            pltpu.VMEM((_R_SLOT, MOE_TM), F32),
            pltpu.VMEM((_YS_ROWS, d), BF16),
            pltpu.SMEM((N_GROUPS,), jnp.int32),
            pltpu.SMEM((N_GROUPS,), jnp.int32),
        ],
        compiler_params=pltpu.CompilerParams(
            dimension_semantics=("parallel", "arbitrary"), vmem_limit_bytes=MOE_VMEM_LIMIT),
        name="moe",
    )(x1, tri, wr, br, wg, wu, wd, g2, b2)


def _rope_table(seq):
    half = ROPE_DIM // 2
    inv = ROPE_THETA ** (-jnp.arange(0, ROPE_DIM, 2, dtype=F32) / ROPE_DIM)
    ang = jnp.arange(seq, dtype=F32)[:, None] * inv[None, :]
    cos, sin = jnp.cos(ang), jnp.sin(ang)
    rest = HEAD_DIM - ROPE_DIM
    one = jnp.ones((seq, rest), F32)
    zero = jnp.zeros((seq, rest), F32)
    zh = jnp.zeros((seq, half), F32)
    c_h = jnp.concatenate([cos, cos, one], axis=1)
    dn_h = jnp.concatenate([-sin, zh, zero], axis=1)
    up_h = jnp.concatenate([zh, sin, zero], axis=1)
    rep = LANES // HEAD_DIM
    return jnp.concatenate([jnp.tile(c_h, (1, rep)), jnp.tile(dn_h, (1, rep)), jnp.tile(up_h, (1, rep))],
                           axis=1)


def _expand_cmp_weights(pe, w1, w2):
    g_n = NSA_KV_GROUPS
    w1r = w1.astype(BF16).reshape(2, CMP_STRIDE, HEAD_DIM, CMP_HIDDEN)
    w2b = w2.astype(BF16)
    w1e = jnp.concatenate(
        [jnp.stack([w1r if g == e else jnp.zeros_like(w1r) for g in range(g_n)], axis=2)
         .reshape(2, A_W, CMP_HIDDEN) for e in range(g_n)], axis=2)
    w2e = jnp.concatenate(
        [jnp.concatenate([w2b if g == e else jnp.zeros_like(w2b) for e in range(g_n)], axis=1)
         for g in range(g_n)], axis=0)
    per = pe.reshape(2, CMP_STRIDE, 1, HEAD_DIM)
    pee = jnp.broadcast_to(per, (2, CMP_STRIDE, g_n, HEAD_DIM)).reshape(2, A_W)
    pee = jnp.concatenate([pee, jnp.zeros((6, A_W), pe.dtype)], axis=0)
    return pee.astype(F32), w1e.astype(BF16), w2e.astype(BF16)


def kernel(x, mem, w_in, cmp_pe_k, cmp_w1_k, cmp_w2_k, cmp_pe_v, cmp_w1_v, cmp_w2_v, sgu_ln_g, sgu_ln_b,
           sgu_w_s, sgu_b_s, w_mem_kv, w_br_nsa, w_br_sgu, w_br_mem, w_o, ln1_g, ln1_b, w_router_group,
           b_router_group, w_router_expert, b_router_expert, w_exp_gate, w_exp_up, w_exp_down, ln2_g, ln2_b):
    batch, seq, d = x.shape
    t = batch * seq
    assert w_in.shape[0] == DEPTH == 1
    assert seq % PROJ_TM == 0 and seq // CMP_STRIDE == N_A and seq // SLC_BLOCK == N_SEL

    offs = [int(v) for v in np.cumsum(
        [0, Q_W, KV_W, KV_W, KV_W, KV_W, KV_W, KV_W, GATE_W, 2 * SGU_WIDTH, MEM_WIDTH, 3 * d])]
    assert _PROJ_COLS <= 3 * d and offs[-1] == w_in.shape[2]
    order = (0, 1, 3, 5, 2, 4, 6, 7, 8, 9)
    segments = [(0, offs[10], 3 * d, LANES)]
    col = 3 * d
    for i in order:
        width = -(-(offs[i + 1] - offs[i]) // LANES) * LANES
        segments.append((col, offs[i], width, min(offs[i + 1] - offs[i], LANES)))
        col += width
    assert col == 3 * d + _PROJ_COLS
    segments.append((col, None, 3 * d - _PROJ_COLS, LANES))
    w_all = _pack_call(w_in[0], tuple(segments), 6 * d)
    rope = _rope_table(seq)
    tril = jnp.tril(jnp.ones((SGU_CHUNK, SGU_CHUNK), dtype=bool))
    ws = jnp.where(tril[None], sgu_w_s[0], 0.0).astype(BF16)
    bs = jnp.repeat(sgu_b_s[0].T, SGU_WIDTH // SGU_GROUPS, axis=1)

    x2 = x.reshape(t, d)
    (qpad, kc, ksl, kwn, vc, vslt, vwnt, gt, osgu, mq) = _proj_call(
        x2, w_all, rope, sgu_ln_g[0][None], sgu_ln_b[0][None], ws, bs, seq)

    pek, w1k, w2k = _expand_cmp_weights(cmp_pe_k[0], cmp_w1_k[0], cmp_w2_k[0])
    pev, w1v, w2v = _expand_cmp_weights(cmp_pe_v[0], cmp_w1_v[0], cmp_w2_v[0])
    kcmp, vcmp = _compress_call(kc.reshape(batch * N_A, A_W), vc.reshape(batch * N_A, A_W),
                                pek, pev, w1k, w2k, w1v, w2v, batch)

    nqb = seq // Q_BLOCK
    n_g = NSA_KV_GROUPS
    vcmpt = vcmp.reshape(batch, N_A, KV_W).transpose(0, 2, 1).reshape(batch * KV_W, N_A)
    ci = np.arange(N_A)
    sj = np.arange(N_SEL)
    overlap = ((ci[None, :] * CMP_STRIDE + CMP_LEN - 1 >= sj[:, None] * SLC_BLOCK)
               & (ci[None, :] * CMP_STRIDE <= sj[:, None] * SLC_BLOCK + SLC_BLOCK - 1)
               & (ci[None, :] < (seq - CMP_LEN) // CMP_STRIDE + 1))
    ot = jnp.asarray(overlap, dtype=BF16)
    onsa = _nsa_call(qpad, ksl, kwn, vslt, vwnt, kcmp, vcmpt, gt, ot, batch, seq)

    omem = _memattn_call(mq, mem.reshape(batch * mem.shape[1], d), w_mem_kv[0], batch, seq)

    x1 = _merge_call(x2, onsa, osgu, omem, w_all, w_br_nsa[0], w_br_sgu[0], w_br_mem[0], w_o[0],
                     ln1_g[0][None], ln1_b[0][None])

    assert EXPERTS_PER_GROUP == _R_SLOT and N_GROUPS <= _R_SLOT
    n_r = _R_SLOT + N_GROUPS * EXPERTS_PER_GROUP
    wr = jnp.concatenate([jnp.pad(w_router_group[0], ((0, 0), (0, _R_SLOT - N_GROUPS))),
                          w_router_expert[0]], axis=1)
    wr = jnp.pad(wr, ((0, 0), (0, LANES - n_r))).astype(BF16)
    br = jnp.concatenate([jnp.pad(b_router_group[0], (0, _R_SLOT - N_GROUPS)), b_router_expert[0]])
    br = jnp.pad(br, (0, LANES - n_r))[None]
    wg = w_exp_gate[0].transpose(0, 2, 1, 3).reshape(N_GROUPS, d, _HID).astype(BF16)
    wu = w_exp_up[0].transpose(0, 2, 1, 3).reshape(N_GROUPS, d, _HID).astype(BF16)
    wd = w_exp_down[0].reshape(N_GROUPS, _HID, d).astype(BF16)
    out = _moe_call(x1, wr, br, wg, wu, wd, ln2_g[0][None], ln2_b[0][None])
    return out.reshape(batch, seq, d)
```

```python
import functools

import numpy as np
import jax
import jax.numpy as jnp
from jax import lax
from jax.experimental import pallas as pl
from jax.experimental.pallas import tpu as pltpu

NSA_HEADS = 8
NSA_KV_GROUPS = 2
NSA_HPG = NSA_HEADS // NSA_KV_GROUPS
HEAD_DIM = 64
CMP_LEN = 32
CMP_STRIDE = 16
CMP_HIDDEN = 256
SLC_BLOCK = 64
SLC_TOPN = 8
WINDOW = 512
Q_BLOCK = 128
N_BAND = WINDOW // Q_BLOCK
ROPE_THETA = 500000.0
ROPE_DIM = HEAD_DIM // 4
SGU_CHUNK = 128
SGU_GROUPS = 8
SGU_WIDTH = 512
MEM_HEADS = 4
MEM_HEAD_DIM = 128
MEM_WIDTH = MEM_HEADS * MEM_HEAD_DIM
N_GROUPS = 4
EXPERTS_PER_GROUP = 8
EXPERT_FF = 256
DEPTH = 1
DN_ALPHA = (2.0 * DEPTH) ** 0.25
LN_EPS = 1e-5
NEG = -1e30
LOG2E = 1.4426950408889634

LANES = 128
Q_W = NSA_HEADS * HEAD_DIM
KV_W = NSA_KV_GROUPS * HEAD_DIM
GATE_W = NSA_HEADS * 3
VMEM_LIMIT = 56 * 1024 * 1024
MOE_VMEM_LIMIT = 60 * 1024 * 1024

BF16 = jnp.bfloat16
F32 = jnp.float32


def _dot(a, b):
    return jnp.dot(a, b, preferred_element_type=F32)


def _dot_nt(a, b):
    return lax.dot_general(a, b, (((1,), (1,)), ((), ())), preferred_element_type=F32)


def _sigmoid(x):
    return 1.0 / (1.0 + jnp.exp(-x))


def _gelu(x):
    return 0.5 * x * (1.0 + lax.erf(x * (2.0 ** -0.5)))


def _layer_norm(x, g, b):
    mu = jnp.mean(x, axis=-1, keepdims=True)
    xc = x - mu
    var = jnp.mean(xc * xc, axis=-1, keepdims=True)
    return xc * lax.rsqrt(var + LN_EPS) * g + b


PACK_TR = 256


def _pack_kernel(segments, n_full, w_ref, tail_ref, o_ref):
    lane = lax.broadcasted_iota(jnp.int32, (PACK_TR, LANES), 1)

    def src_block(a):
        return w_ref[:, a * LANES:(a + 1) * LANES] if a < n_full else tail_ref[...]

    for dst0, src0, width, valid in segments:
        for j in range(width // LANES):
            dst = slice(dst0 + j * LANES, dst0 + (j + 1) * LANES)
            if src0 is None:
                o_ref[:, dst] = jnp.zeros((PACK_TR, LANES), BF16)
                continue
            a, sh = divmod(src0 + j * LANES, LANES)
            if sh == 0:
                blk = src_block(a)
            else:
                blk = jnp.where(lane < LANES - sh, pltpu.roll(src_block(a), LANES - sh, 1),
                                pltpu.roll(src_block(a + 1), LANES - sh, 1))
            if valid < LANES:
                blk = jnp.where(lane < valid, blk, 0.0)
            o_ref[:, dst] = blk.astype(BF16)


def _pack_call(w, segments, out_cols):
    d, n_in = w.shape
    n_full = n_in // LANES
    tail = jnp.pad(w[:, n_full * LANES:], ((0, 0), (0, (n_full + 1) * LANES - n_in)))
    return pl.pallas_call(
        functools.partial(_pack_kernel, segments, n_full),
        grid=(d // PACK_TR,),
        in_specs=[pl.BlockSpec((PACK_TR, n_full * LANES), lambda i: (i, 0)),
                  pl.BlockSpec((PACK_TR, LANES), lambda i: (i, 0))],
        out_specs=pl.BlockSpec((PACK_TR, out_cols), lambda i: (i, 0)),
        out_shape=jax.ShapeDtypeStruct((d, out_cols), BF16),
        compiler_params=pltpu.CompilerParams(
            dimension_semantics=("parallel",), vmem_limit_bytes=VMEM_LIMIT),
        name="pack",
    )(w, tail)


PROJ_TM = 512
_ROPE_COLS = Q_W + 3 * KV_W
_V_OFF = _ROPE_COLS
_G_OFF = _V_OFF + 3 * KV_W
_SGU_OFF = _G_OFF + LANES
_MQ_OFF = _SGU_OFF + 2 * SGU_WIDTH
_PROJ_COLS = _MQ_OFF + MEM_WIDTH


def _store_strided_rows(val, out_ref, tmp_ref):
    tmp_ref[...] = val
    n = val.shape[0] // CMP_STRIDE
    for l in range(CMP_STRIDE):
        out_ref[:, l * LANES:(l + 1) * LANES] = tmp_ref[pl.ds(l, n, stride=CMP_STRIDE), :].astype(BF16)


def _proj_kernel(x_ref, w_ref, rope_ref, lng_ref, lnb_ref, ws_ref, bs_ref,
                 q_ref, kc_ref, ksl_ref, kwn_ref, vc_ref, vsl_ref, vwn_ref, gate_ref, osgu_ref, mq_ref,
                 a_s):
    xb = x_ref[...].astype(BF16)
    lane = lax.broadcasted_iota(jnp.int32, (PROJ_TM, LANES), 1)
    low = lane < HEAD_DIM
    cos = rope_ref[:, 0:LANES]
    s_dn = rope_ref[:, LANES:2 * LANES]
    s_up = rope_ref[:, 2 * LANES:3 * LANES]

    h = _dot(xb, w_ref[:, 0:_ROPE_COLS])
    k_refs = (kc_ref, ksl_ref, kwn_ref)
    for j in range(_ROPE_COLS // LANES):
        blk = h[:, j * LANES:(j + 1) * LANES]
        r = (blk * cos + pltpu.roll(blk, LANES - ROPE_DIM // 2, 1) * s_dn
             + pltpu.roll(blk, ROPE_DIM // 2, 1) * s_up)
        if j < Q_W // LANES:
            r = r * (HEAD_DIM ** -0.5 * LOG2E)
            sw = pltpu.roll(r, HEAD_DIM, 1)
            g = (2 * j) // NSA_HPG
            if g == 0:
                h0 = jnp.where(low, r, 0.0)
                h1 = jnp.where(low, sw, 0.0)
            else:
                h0 = jnp.where(low, 0.0, sw)
                h1 = jnp.where(low, 0.0, r)
            q_ref[:, (2 * j) * LANES:(2 * j + 1) * LANES] = h0.astype(BF16)
            q_ref[:, (2 * j + 1) * LANES:(2 * j + 2) * LANES] = h1.astype(BF16)
        else:
            if j == Q_W // LANES:
                _store_strided_rows(r, kc_ref, a_s)
            else:
                k_refs[j - Q_W // LANES][...] = r.astype(BF16)

    hv = _dot(xb, w_ref[:, _V_OFF:_G_OFF])
    _store_strided_rows(hv[:, 0:LANES], vc_ref, a_s)
    ones = jnp.ones((V_ROWS - HEAD_DIM, PROJ_TM), BF16)
    for v_ref, col0, blk in ((vsl_ref, LANES, SEL_CHUNK), (vwn_ref, 2 * LANES, Q_BLOCK)):
        vt = hv[:, col0:col0 + LANES].T.astype(BF16)
        for g in range(NSA_KV_GROUPS):
            for b in range(PROJ_TM // blk):
                v_ref[g, b, 0:HEAD_DIM, :] = vt[g * HEAD_DIM:(g + 1) * HEAD_DIM, b * blk:(b + 1) * blk]
                v_ref[g, b, HEAD_DIM:V_ROWS, :] = ones[:, 0:blk]

    gate_ref[...] = _sigmoid(_dot(xb, w_ref[:, _G_OFF:_SGU_OFF])).T

    mq_ref[...] = _dot(xb, w_ref[:, _MQ_OFF:_PROJ_COLS]).astype(BF16)

    z = _gelu(_dot(xb, w_ref[:, _SGU_OFF:_MQ_OFF]))
    u = z[:, 0:SGU_WIDTH]
    v = _layer_norm(z[:, SGU_WIDTH:2 * SGU_WIDTH], lng_ref[...], lnb_ref[...]).astype(BF16)
    lane_c = lax.broadcasted_iota(jnp.int32, (SGU_CHUNK, LANES), 1)
    low_c = lane_c < (SGU_WIDTH // SGU_GROUPS)
    n_ch = PROJ_TM // SGU_CHUNK
    for gp in range(SGU_WIDTH // LANES):
        cols = slice(gp * LANES, (gp + 1) * LANES)
        vcat = jnp.concatenate([v[ci * SGU_CHUNK:(ci + 1) * SGU_CHUNK, cols] for ci in range(n_ch)], axis=1)
        sv0 = _dot(ws_ref[2 * gp], vcat)
        sv1 = _dot(ws_ref[2 * gp + 1], vcat)
        for ci in range(n_ch):
            rows = slice(ci * SGU_CHUNK, (ci + 1) * SGU_CHUNK)
            lanes = slice(ci * LANES, (ci + 1) * LANES)
            sv = jnp.where(low_c, sv0[:, lanes], sv1[:, lanes]) + bs_ref[:, cols]
            osgu_ref[rows, cols] = (u[rows, cols] * sv).astype(BF16)


def _proj_call(x2, wp, rope, lng, lnb, ws, bs, seq):
    t = x2.shape[0]
    d = x2.shape[1]
    nt = t // PROJ_TM
    per_seq = seq // PROJ_TM
    row = lambda i: (i, 0)
    const2 = lambda i: (0, 0)
    assert PROJ_TM % SEL_CHUNK == 0 and PROJ_TM % Q_BLOCK == 0
    n_g = NSA_KV_GROUPS
    batch = t // seq
    vblock = lambda i: (i // per_seq, i % per_seq, 0, 0)

    def v_shape(blk):
        return jax.ShapeDtypeStruct((batch * n_g, seq // blk, V_ROWS, blk), BF16)

    def v_spec(blk):
        return pl.BlockSpec((n_g, PROJ_TM // blk, V_ROWS, blk), vblock)

    a_shape = jax.ShapeDtypeStruct((t // CMP_STRIDE, CMP_STRIDE * KV_W), BF16)
    a_spec = pl.BlockSpec((PROJ_TM // CMP_STRIDE, CMP_STRIDE * KV_W), row)
    k_shape = jax.ShapeDtypeStruct((t, KV_W), BF16)
    k_spec = pl.BlockSpec((PROJ_TM, KV_W), row)
    out_shapes = (
        jax.ShapeDtypeStruct((t, NSA_HEADS * LANES), BF16),
        a_shape, k_shape, k_shape, a_shape,
        v_shape(SEL_CHUNK), v_shape(Q_BLOCK),
        jax.ShapeDtypeStruct((LANES, t), F32),
        jax.ShapeDtypeStruct((t, SGU_WIDTH), BF16),
        jax.ShapeDtypeStruct((t, MEM_WIDTH), BF16),
    )
    out_specs = (
        pl.BlockSpec((PROJ_TM, NSA_HEADS * LANES), row),
        a_spec, k_spec, k_spec, a_spec,
        v_spec(SEL_CHUNK), v_spec(Q_BLOCK),
        pl.BlockSpec((LANES, PROJ_TM), lambda i: (0, i)),
        pl.BlockSpec((PROJ_TM, SGU_WIDTH), row),
        pl.BlockSpec((PROJ_TM, MEM_WIDTH), row),
    )
    return pl.pallas_call(
        _proj_kernel,
        grid=(nt,),
        in_specs=[
            pl.BlockSpec((PROJ_TM, d), row),
            pl.BlockSpec((d, 3 * d), lambda i: (0, 1)),
            pl.BlockSpec((PROJ_TM, 3 * LANES), lambda i: (i % per_seq, 0)),
            pl.BlockSpec((1, SGU_WIDTH), const2),
            pl.BlockSpec((1, SGU_WIDTH), const2),
            pl.BlockSpec((SGU_GROUPS, SGU_CHUNK, SGU_CHUNK), lambda i: (0, 0, 0)),
            pl.BlockSpec((SGU_CHUNK, SGU_WIDTH), const2),
        ],
        out_specs=out_specs,
        out_shape=out_shapes,
        scratch_shapes=[pltpu.VMEM((PROJ_TM, KV_W), F32)],
        compiler_params=pltpu.CompilerParams(
            dimension_semantics=("parallel",), vmem_limit_bytes=VMEM_LIMIT),
        name="proj",
    )(x2, wp, rope, lng, lnb, ws, bs)


N_A = 128
A_W = CMP_STRIDE * KV_W


def _compress_kernel(ka_ref, va_ref, pek_ref, pev_ref, w1k_ref, w2k_ref, w1v_ref, w2v_ref,
                     kcmp_ref, vcmp_ref):
    def one(a_ref, pe_ref, w1_ref, w2_ref, out_ref):
        a = a_ref[...].astype(F32)
        top = (a + pe_ref[0:1, :]).astype(BF16)
        bot = (a + pe_ref[1:2, :]).astype(BF16)
        h1 = _dot(top, w1_ref[0])
        h2 = _dot(bot, w1_ref[1])
        pre = h1 + pltpu.roll(h2, N_A - 1, 0)
        act = _gelu(pre).astype(BF16)
        out_ref[...] = _dot(act, w2_ref[...]).astype(BF16)

    one(ka_ref, pek_ref, w1k_ref, w2k_ref, kcmp_ref)
    one(va_ref, pev_ref, w1v_ref, w2v_ref, vcmp_ref)


def _compress_call(ka, va, pek, pev, w1k, w2k, w1v, w2v, batch):
    row = lambda b: (b, 0)
    c2 = lambda b: (0, 0)
    c3 = lambda b: (0, 0, 0)
    hid2 = NSA_KV_GROUPS * CMP_HIDDEN
    return pl.pallas_call(
        _compress_kernel,
        grid=(batch,),
        in_specs=[
            pl.BlockSpec((N_A, A_W), row),
            pl.BlockSpec((N_A, A_W), row),
            pl.BlockSpec((8, A_W), c2),
            pl.BlockSpec((8, A_W), c2),
            pl.BlockSpec((2, A_W, hid2), c3),
            pl.BlockSpec((hid2, KV_W), c2),
            pl.BlockSpec((2, A_W, hid2), c3),
            pl.BlockSpec((hid2, KV_W), c2),
        ],
        out_specs=(pl.BlockSpec((N_A, KV_W), row), pl.BlockSpec((N_A, KV_W), row)),
        out_shape=(jax.ShapeDtypeStruct((batch * N_A, KV_W), BF16),
                   jax.ShapeDtypeStruct((batch * N_A, KV_W), BF16)),
        compiler_params=pltpu.CompilerParams(
            dimension_semantics=("parallel",), vmem_limit_bytes=VMEM_LIMIT),
        name="compress",
    )(ka, va, pek, pev, w1k, w2k, w1v, w2v)


N_SEL = 32
SEL_CHUNK = 512
SEL_SUB = 128
HQ = NSA_HPG * Q_BLOCK
V_ROWS = HEAD_DIM + 16


def _nsa_kernel(q_ref, ksl_ref, kwn_ref, vslt_ref, vwnt_ref, kcmp_ref, vcmpt_ref, gt_ref,
                ot_ref, o_ref, m_s, acc_s, ocmp_s, owin_s, cap_s, out_s):
    c = pl.program_id(1)
    lane_hq = lax.broadcasted_iota(jnp.int32, (1, HQ), 1)
    pos_hq = c * Q_BLOCK + (lane_hq & (Q_BLOCK - 1))
    pos_q = c * Q_BLOCK + lax.broadcasted_iota(jnp.int32, (1, Q_BLOCK), 1)
    groups = range(NSA_KV_GROUPS)
    qgs = [jnp.concatenate(
        [q_ref[:, (g * NSA_HPG + hh) * LANES:(g * NSA_HPG + hh + 1) * LANES]
         for hh in range(NSA_HPG)], axis=0) for g in groups]

    kcmp = kcmp_ref[...]
    s_c = [_dot_nt(kcmp, qgs[g]) for g in groups]
    n_idx = lax.broadcasted_iota(jnp.int32, (N_A, HQ), 0)
    valid_c = (n_idx * CMP_STRIDE + (CMP_LEN - 1)) <= pos_hq
    p_c = []
    for g in groups:
        sm_c = jnp.where(valid_c, s_c[g], NEG)
        m_c = jnp.max(sm_c, axis=0, keepdims=True)
        e_c = jnp.where(valid_c, jnp.exp2(sm_c - m_c), 0.0)
        d_c = jnp.sum(e_c, axis=0, keepdims=True)
        p_c.append(e_c / jnp.where(d_c > 0, d_c, 1.0))
    for g in groups:
        ocmp_s[g] = _dot(vcmpt_ref[g * HEAD_DIM:(g + 1) * HEAD_DIM, :], p_c[g].astype(BF16))

    ot = ot_ref[...]
    imp = []
    for g in groups:
        ps = (p_c[g][:, 0:Q_BLOCK] + p_c[g][:, Q_BLOCK:2 * Q_BLOCK]
              + p_c[g][:, 2 * Q_BLOCK:3 * Q_BLOCK] + p_c[g][:, 3 * Q_BLOCK:4 * Q_BLOCK])
        p_hi = ps.astype(BF16)
        r1 = ps - p_hi.astype(F32)
        p_lo = r1.astype(BF16)
        p_lo2 = (r1 - p_lo.astype(F32)).astype(BF16)
        imp.append(_dot(ot, p_hi) + _dot(ot, p_lo) + _dot(ot, p_lo2))

    j_idx = lax.broadcasted_iota(jnp.int32, (N_SEL, Q_BLOCK), 0)
    cur = pos_q // SLC_BLOCK
    future = j_idx > cur
    forced = (j_idx == 0) | (j_idx == cur) | (j_idx == cur - 1)
    imp = [jnp.where(future, NEG, jnp.where(forced, -NEG, imp[g])) for g in groups]
    rank = [jnp.zeros((N_SEL, Q_BLOCK), F32) for g in groups]
    for i in range(N_SEL):
        for g in groups:
            row = imp[g][i:i + 1, :]
            beats = (row > imp[g]) | ((row == imp[g]) & (j_idx > i))
            rank[g] = rank[g] + jnp.where(beats, 1.0, 0.0)
    for g in groups:
        cap_s[g] = jnp.where(rank[g] < float(SLC_TOPN), -NEG, NEG)
        m_s[g] = jnp.full((1, HQ), NEG, F32)
        acc_s[g] = jnp.zeros((V_ROWS, HQ), F32)

    _nsa_window(c, qgs, kwn_ref, vwnt_ref, owin_s)

    blocks_per_sub = SEL_SUB // SLC_BLOCK
    subs_per_chunk = SEL_CHUNK // SEL_SUB

    def sel_chunk(kc):
        spans = [slice(kc * SEL_CHUNK + a * SEL_SUB, kc * SEL_CHUNK + (a + 1) * SEL_SUB)
                 for a in range(subs_per_chunk)]
        s = [[_dot_nt(ksl_ref[ks, :], qgs[g]) for g in groups] for ks in spans]
        for a, ks in enumerate(spans):
            kpos = ks.start + lax.broadcasted_iota(jnp.int32, (SEL_SUB, Q_BLOCK), 0)
            causal = kpos <= pos_q
            for g in groups:
                j0 = ks.start // SLC_BLOCK
                cap = jnp.concatenate(
                    [jnp.broadcast_to(cap_s[g, j:j + 1, :], (SLC_BLOCK, Q_BLOCK))
                     for j in range(j0, j0 + blocks_per_sub)], axis=0)
                cap = jnp.where(causal, cap, NEG)
                sm = jnp.minimum(s[a][g], jnp.concatenate([cap] * NSA_HPG, axis=1))
                m_old = m_s[g]
                m_new = jnp.maximum(m_old, jnp.max(sm, axis=0, keepdims=True))
                alpha = jnp.exp2(m_old - m_new)
                e = jnp.exp2((sm - m_new).astype(BF16))
                v_t = vslt_ref[g, kc, :, a * SEL_SUB:(a + 1) * SEL_SUB]
                acc_s[g] = alpha * acc_s[g] + _dot(v_t, e)
                m_s[g] = m_new

    sel_chunk(0)
    for kc in range(1, ksl_ref.shape[0] // SEL_CHUNK):
        pl.when(kc * (SEL_CHUNK // Q_BLOCK) <= c)(functools.partial(sel_chunk, kc))

    for g in groups:
        acc = acc_s[g]
        o_sel = acc[0:HEAD_DIM, :] * (1.0 / acc[HEAD_DIM:HEAD_DIM + 1, :])

        def gate_row(br):
            return jnp.concatenate(
                [gt_ref[(g * NSA_HPG + hh) * 3 + br:(g * NSA_HPG + hh) * 3 + br + 1, :]
                 for hh in range(NSA_HPG)], axis=1)
        o_t = gate_row(0) * ocmp_s[g] + gate_row(1) * o_sel + gate_row(2) * owin_s[g]
        for hh in range(NSA_HPG):
            h = g * NSA_HPG + hh
            out_s[h * HEAD_DIM:(h + 1) * HEAD_DIM, :] = o_t[:, hh * Q_BLOCK:(hh + 1) * Q_BLOCK]

    o_ref[...] = out_s[...].T.astype(BF16)


def _nsa_window(c, qgs, kwn_ref, vwnt_ref, owin_s):
    groups = range(NSA_KV_GROUPS)
    q_i = lax.broadcasted_iota(jnp.int32, (Q_BLOCK, Q_BLOCK), 1)
    k_i = lax.broadcasted_iota(jnp.int32, (Q_BLOCK, Q_BLOCK), 0)
    blks, caps = [], []
    for i in range(N_BAND + 1):
        blk = c - N_BAND + i
        blks.append(jnp.maximum(blk, 0))
        off = (N_BAND - i) * Q_BLOCK
        if off - (Q_BLOCK - 1) >= 0 and off + (Q_BLOCK - 1) < WINDOW:
            caps.append(jnp.where(blk >= 0, -NEG, NEG))
        else:
            diff = off + q_i - k_i
            ok = (diff >= 0) & (diff < WINDOW) & (blk >= 0)
            caps.append(jnp.concatenate([jnp.where(ok, -NEG, NEG)] * NSA_HPG, axis=1))
    k_blocks = [kwn_ref[pl.ds(pl.multiple_of(blks[i] * Q_BLOCK, Q_BLOCK), Q_BLOCK), :]
                for i in range(N_BAND + 1)]
    s_w = [[None] * (N_BAND + 1) for g in groups]
    for i in reversed(range(N_BAND + 1)):
        for g in groups:
            s_w[g][i] = _dot_nt(k_blocks[i], qgs[g])
    m_w = [None for g in groups]
    o_win = [None for g in groups]
    for i in reversed(range(N_BAND + 1)):
        for g in groups:
            sm = jnp.minimum(s_w[g][i], caps[i])
            m_blk = jnp.max(sm, axis=0, keepdims=True)
            m_new = m_blk if m_w[g] is None else jnp.maximum(m_w[g], m_blk)
            pv = _dot(vwnt_ref[g, blks[i]], jnp.exp2((sm - m_new).astype(BF16)))
            o_win[g] = pv if m_w[g] is None else jnp.exp2(m_w[g] - m_new) * o_win[g] + pv
            m_w[g] = m_new
    for g in groups:
        owin_s[g] = o_win[g][0:HEAD_DIM, :] * (1.0 / o_win[g][HEAD_DIM:HEAD_DIM + 1, :])


def _nsa_call(qpad, ksl, kwn, vslt, vwnt, kcmp, vcmpt, gt, ot, batch, seq):
    nqb = seq // Q_BLOCK
    n_g = NSA_KV_GROUPS
    qrow = lambda b, c: (b * nqb + c, 0)
    brow = lambda b, c: (b, 0)
    c2 = lambda b, c: (0, 0)
    return pl.pallas_call(
        _nsa_kernel,
        grid=(batch, nqb),
        in_specs=[
            pl.BlockSpec((Q_BLOCK, NSA_HEADS * LANES), qrow),
            pl.BlockSpec((seq, KV_W), brow),
            pl.BlockSpec((seq, KV_W), brow),
            pl.BlockSpec((n_g, seq // SEL_CHUNK, V_ROWS, SEL_CHUNK), lambda b, c: (b, 0, 0, 0)),
            pl.BlockSpec((n_g, nqb, V_ROWS, Q_BLOCK), lambda b, c: (b, 0, 0, 0)),
            pl.BlockSpec((N_A, KV_W), brow),
            pl.BlockSpec((KV_W, N_A), brow),
            pl.BlockSpec((LANES, Q_BLOCK), lambda b, c: (0, b * nqb + c)),
            pl.BlockSpec((N_SEL, N_A), c2),
        ],
        out_specs=pl.BlockSpec((Q_BLOCK, Q_W), qrow),
        out_shape=jax.ShapeDtypeStruct((batch * seq, Q_W), BF16),
        scratch_shapes=[
            pltpu.VMEM((n_g, 1, HQ), F32),
            pltpu.VMEM((n_g, V_ROWS, HQ), F32),
            pltpu.VMEM((n_g, HEAD_DIM, HQ), F32),
            pltpu.VMEM((n_g, HEAD_DIM, HQ), F32),
            pltpu.VMEM((n_g, N_SEL, Q_BLOCK), F32),
            pltpu.VMEM((Q_W, Q_BLOCK), F32),
        ],
        compiler_params=pltpu.CompilerParams(
            dimension_semantics=("parallel", "arbitrary"), vmem_limit_bytes=VMEM_LIMIT),
        name="nsa",
    )(qpad, ksl, kwn, vslt, vwnt, kcmp, vcmpt, gt, ot)


MEM_TM = 512
MEM_V_ROWS = MEM_HEAD_DIM + 16


def _memattn_kernel(mq_ref, mem_ref, wkv_ref, o_ref, k_s, vt_s, out_s):
    @pl.when(pl.program_id(1) == 0)
    def _():
        kv = _dot(mem_ref[...].astype(BF16), wkv_ref[...].astype(BF16))
        k_s[...] = kv[:, 0:MEM_WIDTH].astype(BF16)
        vt = kv[:, MEM_WIDTH:2 * MEM_WIDTH].T.astype(BF16)
        for h in range(MEM_HEADS):
            vt_s[h, 0:MEM_HEAD_DIM, :] = vt[h * MEM_HEAD_DIM:(h + 1) * MEM_HEAD_DIM, :]
            vt_s[h, MEM_HEAD_DIM:, :] = jnp.ones((MEM_V_ROWS - MEM_HEAD_DIM, vt.shape[1]), BF16)

    heads = range(MEM_HEADS)
    cols = [slice(h * MEM_HEAD_DIM, (h + 1) * MEM_HEAD_DIM) for h in heads]
    s_t = [_dot_nt(k_s[:, cols[h]], mq_ref[:, cols[h]]) for h in heads]
    e = []
    for h in heads:
        m = jnp.max(s_t[h], axis=0, keepdims=True)
        e.append(jnp.exp2((s_t[h] - m) * (MEM_HEAD_DIM ** -0.5 * LOG2E)).astype(BF16))
    o_t = [_dot(vt_s[h], e[h]) for h in heads]
    for h in heads:
        out_s[cols[h], :] = o_t[h][0:MEM_HEAD_DIM, :] * (1.0 / o_t[h][MEM_HEAD_DIM:MEM_HEAD_DIM + 1, :])
    o_ref[...] = out_s[...].T.astype(BF16)


def _memattn_call(mq, mem2, wkv, batch, seq):
    m_len = mem2.shape[0] // batch
    d = mem2.shape[1]
    per = seq // MEM_TM
    return pl.pallas_call(
        _memattn_kernel,
        grid=(batch, per),
        in_specs=[
            pl.BlockSpec((MEM_TM, MEM_WIDTH), lambda b, i: (b * per + i, 0)),
            pl.BlockSpec((m_len, d), lambda b, i: (b, 0)),
            pl.BlockSpec((d, 2 * MEM_WIDTH), lambda b, i: (0, 0)),
        ],
        out_specs=pl.BlockSpec((MEM_TM, MEM_WIDTH), lambda b, i: (b * per + i, 0)),
        out_shape=jax.ShapeDtypeStruct((batch * seq, MEM_WIDTH), BF16),
        scratch_shapes=[
            pltpu.VMEM((m_len, MEM_WIDTH), BF16),
            pltpu.VMEM((MEM_HEADS, MEM_V_ROWS, m_len), BF16),
            pltpu.VMEM((MEM_WIDTH, MEM_TM), F32),
        ],
        compiler_params=pltpu.CompilerParams(
            dimension_semantics=("parallel", "arbitrary"), vmem_limit_bytes=VMEM_LIMIT),
        name="memattn",
    )(mq, mem2, wkv)


MERGE_TM = 512


def _merge_kernel(x_ref, onsa_ref, osgu_ref, omem_ref, wg_ref, wbn32_ref, wbs32_ref, wbm32_ref, wo32_ref,
                  g_ref, b_ref, out_ref, wbn_ref, wbs_ref, wbm_ref, wo_ref):
    @pl.when(pl.program_id(0) == 0)
    def _():
        for dst, src in ((wbn_ref, wbn32_ref), (wbs_ref, wbs32_ref), (wbm_ref, wbm32_ref),
                         (wo_ref, wo32_ref)):
            dst[...] = src[...].astype(BF16)

    x = x_ref[...]
    xb = x.astype(BF16)
    d = x.shape[1]
    y = None
    for br, (o_r, w_r) in enumerate(((onsa_ref, wbn_ref), (osgu_ref, wbs_ref), (omem_ref, wbm_ref))):
        gate = _sigmoid(_dot(xb, wg_ref[:, br * d:(br + 1) * d]))
        term = gate * _dot(o_r[...], w_r[...])
        y = term if y is None else y + term
    z = DN_ALPHA * x + _dot(y.astype(BF16), wo_ref[...])
    out_ref[...] = _layer_norm(z, g_ref[...], b_ref[...])


def _merge_call(x2, onsa, osgu, omem, wg, wbn, wbs, wbm, wo, g1, b1):
    t, d = x2.shape
    row = lambda i: (i, 0)
    c2 = lambda i: (0, 0)
    full = lambda a: pl.BlockSpec(a.shape, c2)
    once = lambda a: pl.BlockSpec(a.shape, c2, pipeline_mode=pl.Buffered(1))
    return pl.pallas_call(
        _merge_kernel,
        grid=(t // MERGE_TM,),
        in_specs=[
            pl.BlockSpec((MERGE_TM, d), row),
            pl.BlockSpec((MERGE_TM, onsa.shape[1]), row),
            pl.BlockSpec((MERGE_TM, osgu.shape[1]), row),
            pl.BlockSpec((MERGE_TM, omem.shape[1]), row),
            pl.BlockSpec((d, 3 * d), c2),
            once(wbn), once(wbs), once(wbm), once(wo), full(g1), full(b1),
        ],
        out_specs=pl.BlockSpec((MERGE_TM, d), row),
        out_shape=jax.ShapeDtypeStruct((t, d), F32),
        scratch_shapes=[pltpu.VMEM(a.shape, BF16) for a in (wbn, wbs, wbm, wo)],
        compiler_params=pltpu.CompilerParams(
            dimension_semantics=("arbitrary",), vmem_limit_bytes=VMEM_LIMIT),
        name="merge",
    )(x2, onsa, osgu, omem, wg, wbn, wbs, wbm, wo, g1, b1)


MOE_TM = 1024
MOE_CH = 144
_YS_ROWS = -(-(MOE_TM + N_GROUPS * MOE_CH) // 256) * 256
_R_SLOT = 8
_DEST_LANE = 3 * _R_SLOT
_HID = EXPERTS_PER_GROUP * EXPERT_FF


def _moe_kernel(x_ref, tri_ref, wr_ref, br_ref, wg_ref, wu_ref, wd_ref, g_ref, b_ref, out_ref,
                xa_s, tok_s, keyr_s, ys_s, cnt_s, base_s):
    grp = pl.program_id(1)
    tm = x_ref.shape[0]
    d = x_ref.shape[1]

    @pl.when(grp == 0)
    def _route():
        xb = x_ref[...].astype(BF16)
        xa_s[:, 0:d] = xb
        logits = _dot(xb, wr_ref[...]) + br_ref[...]
        lt = logits.T
        row = lax.broadcasted_iota(jnp.int32, (_R_SLOT, tm), 0)
        gl = jnp.where(row < N_GROUPS, lt[0:_R_SLOT], NEG)
        gmax = jnp.max(gl, axis=0, keepdims=True)
        gidx = jnp.min(jnp.where(gl == gmax, row, _R_SLOT), axis=0, keepdims=True)
        gprob = 1.0 / jnp.sum(jnp.exp(gl - gmax), axis=0, keepdims=True)
        el = lt[_R_SLOT:2 * _R_SLOT]
        for k in range(1, N_GROUPS):
            el = jnp.where(gidx == k, lt[(k + 1) * _R_SLOT:(k + 2) * _R_SLOT], el)
        ee = jnp.exp(el - jnp.max(el, axis=0, keepdims=True))
        ep = ee / jnp.sum(ee, axis=0, keepdims=True)
        t1 = jnp.max(ep, axis=0, keepdims=True)
        i1 = jnp.min(jnp.where(ep == t1, row, _R_SLOT), axis=0, keepdims=True)
        rest = row != i1
        t2 = jnp.max(jnp.where(rest, ep, -1.0), axis=0, keepdims=True)
        i2 = jnp.min(jnp.where(rest & (ep == t2), row, _R_SLOT), axis=0, keepdims=True)
        den = t1 + t2
        cwf = (jnp.where(row == i1, t1 / den, 0.0) + jnp.where(row == i2, t2 / den, 0.0)) * gprob
        hi = cwf.astype(BF16).astype(F32)
        r1 = cwf - hi
        lo = r1.astype(BF16).astype(F32)
        lo2 = (r1 - lo).astype(BF16).astype(F32)

        onehot = jnp.where(row == gidx, 1.0, 0.0)
        rank = _dot_nt(onehot.astype(BF16), tri_ref[...])
        keyr_s[...] = jnp.where(onehot > 0.5, rank, -1.0)
        base = jnp.int32(0)
        basev = jnp.zeros((1, tm), F32)
        for k in range(N_GROUPS):
            n_k = jnp.sum(onehot[k:k + 1, :]).astype(jnp.int32)
            cnt_s[k] = n_k
            base_s[k] = base
            basev = jnp.where(gidx == k, base.astype(F32), basev)
            base = base + ((n_k + MOE_CH - 1) // MOE_CH) * MOE_CH
        dest = jnp.sum(rank * onehot, axis=0, keepdims=True) + basev
        tok = jnp.concatenate(
            [hi, lo, lo2, jnp.broadcast_to(dest, (_R_SLOT, tm)),
             jnp.zeros((LANES - 4 * _R_SLOT, tm), F32)], axis=0).T
        tok_s[...] = tok
        lane = lax.broadcasted_iota(jnp.int32, tok.shape, 1)
        xa_s[:, d:d + LANES] = jnp.where(lane < _DEST_LANE, tok, 0.0).astype(BF16)
        ys_s[...] = jnp.zeros(ys_s.shape, BF16)

    n_rows = cnt_s[grp]
    row0 = base_s[grp]
    keyr = keyr_s[pl.ds(grp, 1), :]
    half = _HID // 2

    def sweep(k, carry):
        ch = MOE_CH
        r_row = (lax.broadcasted_iota(jnp.int32, (ch, tm), 0) + k * MOE_CH).astype(F32)
        pick = jnp.where(keyr == r_row, 1.0, 0.0).astype(BF16)
        ga = _dot(pick, xa_s[...])
        xg = ga[:, 0:d].astype(BF16)
        cwg = ga[:, d:d + LANES]
        cs = (cwg + pltpu.roll(cwg, LANES - _R_SLOT, 1)
              + pltpu.roll(cwg, LANES - 2 * _R_SLOT, 1))
        hg = [_dot(xg, wg_ref[0, :, h * half:(h + 1) * half]) for h in range(2)]
        hu = [_dot(xg, wu_ref[0, :, h * half:(h + 1) * half]) for h in range(2)]
        y = None
        for h in range(2):
            per_half = EXPERTS_PER_GROUP // 2
            cexp = jnp.concatenate(
                [jnp.broadcast_to(cs[:, e:e + 1], (ch, EXPERT_FF))
                 for e in range(h * per_half, (h + 1) * per_half)], axis=1)
            hid = ((hg[h] * _sigmoid(hg[h])) * hu[h]) * cexp
            term = _dot(hid.astype(BF16), wd_ref[0, h * half:(h + 1) * half, :])
            y = term if y is None else y + term
        ys_s[pl.ds(pl.multiple_of(row0 + k * MOE_CH, 16), ch), :] = y.astype(BF16)
        return carry

    lax.fori_loop(0, (n_rows + MOE_CH - 1) // MOE_CH, sweep, 0)

    @pl.when(grp == N_GROUPS - 1)
    def _fin():
        r_col = lax.broadcasted_iota(jnp.int32, (tm, _YS_ROWS), 1).astype(F32)
        put = jnp.where(tok_s[:, _DEST_LANE:_DEST_LANE + 1] == r_col, 1.0, 0.0).astype(BF16)
        z = DN_ALPHA * x_ref[...] + _dot(put, ys_s[...])
        out_ref[...] = _layer_norm(z, g_ref[...], b_ref[...])


def _moe_call(x1, wr, br, wg, wu, wd, g2, b2):
    t, d = x1.shape
    row = lambda i, g: (i, 0)
    c2 = lambda i, g: (0, 0)
    idx = np.arange(MOE_TM)
    tri = jnp.asarray(idx[None, :] < idx[:, None], dtype=BF16)
    return pl.pallas_call(
        _moe_kernel,
        grid=(t // MOE_TM, N_GROUPS),
        in_specs=[
            pl.BlockSpec((MOE_TM, d), row, pipeline_mode=pl.Buffered(1)),
            pl.BlockSpec((MOE_TM, MOE_TM), c2, pipeline_mode=pl.Buffered(1)),
            pl.BlockSpec(wr.shape, c2),
            pl.BlockSpec(br.shape, c2),
            pl.BlockSpec((1, d, _HID), lambda i, g: (g, 0, 0)),
            pl.BlockSpec((1, d, _HID), lambda i, g: (g, 0, 0)),
            pl.BlockSpec((1, _HID, d), lambda i, g: (g, 0, 0)),
            pl.BlockSpec(g2.shape, c2),
            pl.BlockSpec(b2.shape, c2),
        ],
        out_specs=pl.BlockSpec((MOE_TM, d), row),
        out_shape=jax.ShapeDtypeStruct((t, d), F32),
        scratch_shapes=[
            pltpu.VMEM((MOE_TM, d + LANES), BF16),
            pltpu.VMEM((MOE_TM, LANES), F32),
            pltpu.VMEM((_R_SLOT, MOE_TM), F32),
            pltpu.VMEM((_YS_ROWS, d), BF16),
            pltpu.SMEM((N_GROUPS,), jnp.int32),
            pltpu.SMEM((N_GROUPS,), jnp.int32),
        ],
        compiler_params=pltpu.CompilerParams(
            dimension_semantics=("parallel", "arbitrary"), vmem_limit_bytes=MOE_VMEM_LIMIT),
        name="moe",
    )(x1, tri, wr, br, wg, wu, wd, g2, b2)


def _rope_table(seq):
    half = ROPE_DIM // 2
    inv = ROPE_THETA ** (-jnp.arange(0, ROPE_DIM, 2, dtype=F32) / ROPE_DIM)
    ang = jnp.arange(seq, dtype=F32)[:, None] * inv[None, :]
    cos, sin = jnp.cos(ang), jnp.sin(ang)
    rest = HEAD_DIM - ROPE_DIM
    one = jnp.ones((seq, rest), F32)
    zero = jnp.zeros((seq, rest), F32)
    zh = jnp.zeros((seq, half), F32)
    c_h = jnp.concatenate([cos, cos, one], axis=1)
    dn_h = jnp.concatenate([-sin, zh, zero], axis=1)
    up_h = jnp.concatenate([zh, sin, zero], axis=1)
    rep = LANES // HEAD_DIM
    return jnp.concatenate([jnp.tile(c_h, (1, rep)), jnp.tile(dn_h, (1, rep)), jnp.tile(up_h, (1, rep))],
                           axis=1)


def _expand_cmp_weights(pe, w1, w2):
    g_n = NSA_KV_GROUPS
    w1r = w1.astype(BF16).reshape(2, CMP_STRIDE, HEAD_DIM, CMP_HIDDEN)
    w2b = w2.astype(BF16)
    w1e = jnp.concatenate(
        [jnp.stack([w1r if g == e else jnp.zeros_like(w1r) for g in range(g_n)], axis=2)
         .reshape(2, A_W, CMP_HIDDEN) for e in range(g_n)], axis=2)
    w2e = jnp.concatenate(
        [jnp.concatenate([w2b if g == e else jnp.zeros_like(w2b) for e in range(g_n)], axis=1)
         for g in range(g_n)], axis=0)
    per = pe.reshape(2, CMP_STRIDE, 1, HEAD_DIM)
    pee = jnp.broadcast_to(per, (2, CMP_STRIDE, g_n, HEAD_DIM)).reshape(2, A_W)
    pee = jnp.concatenate([pee, jnp.zeros((6, A_W), pe.dtype)], axis=0)
    return pee.astype(F32), w1e.astype(BF16), w2e.astype(BF16)


def kernel(x, mem, w_in, cmp_pe_k, cmp_w1_k, cmp_w2_k, cmp_pe_v, cmp_w1_v, cmp_w2_v, sgu_ln_g, sgu_ln_b,
           sgu_w_s, sgu_b_s, w_mem_kv, w_br_nsa, w_br_sgu, w_br_mem, w_o, ln1_g, ln1_b, w_router_group,
           b_router_group, w_router_expert, b_router_expert, w_exp_gate, w_exp_up, w_exp_down, ln2_g, ln2_b):
    batch, seq, d = x.shape
    t = batch * seq
    assert w_in.shape[0] == DEPTH == 1
    assert seq % PROJ_TM == 0 and seq // CMP_STRIDE == N_A and seq // SLC_BLOCK == N_SEL

    offs = [int(v) for v in np.cumsum(
        [0, Q_W, KV_W, KV_W, KV_W, KV_W, KV_W, KV_W, GATE_W, 2 * SGU_WIDTH, MEM_WIDTH, 3 * d])]
    assert _PROJ_COLS <= 3 * d and offs[-1] == w_in.shape[2]
    order = (0, 1, 3, 5, 2, 4, 6, 7, 8, 9)
    segments = [(0, offs[10], 3 * d, LANES)]
    col = 3 * d
    for i in order:
        width = -(-(offs[i + 1] - offs[i]) // LANES) * LANES
        segments.append((col, offs[i], width, min(offs[i + 1] - offs[i], LANES)))
        col += width
    assert col == 3 * d + _PROJ_COLS
    segments.append((col, None, 3 * d - _PROJ_COLS, LANES))
    w_all = _pack_call(w_in[0], tuple(segments), 6 * d)
    rope = _rope_table(seq)
    tril = jnp.tril(jnp.ones((SGU_CHUNK, SGU_CHUNK), dtype=bool))
    ws = jnp.where(tril[None], sgu_w_s[0], 0.0).astype(BF16)
    bs = jnp.repeat(sgu_b_s[0].T, SGU_WIDTH // SGU_GROUPS, axis=1)

    x2 = x.reshape(t, d)
    (qpad, kc, ksl, kwn, vc, vslt, vwnt, gt, osgu, mq) = _proj_call(
        x2, w_all, rope, sgu_ln_g[0][None], sgu_ln_b[0][None], ws, bs, seq)

    pek, w1k, w2k = _expand_cmp_weights(cmp_pe_k[0], cmp_w1_k[0], cmp_w2_k[0])
    pev, w1v, w2v = _expand_cmp_weights(cmp_pe_v[0], cmp_w1_v[0], cmp_w2_v[0])
    kcmp, vcmp = _compress_call(kc, vc, pek, pev, w1k, w2k, w1v, w2v, batch)

    nqb = seq // Q_BLOCK
    n_g = NSA_KV_GROUPS
    vcmpt = vcmp.reshape(batch, N_A, KV_W).transpose(0, 2, 1).reshape(batch * KV_W, N_A)
    ci = np.arange(N_A)
    sj = np.arange(N_SEL)
    overlap = ((ci[None, :] * CMP_STRIDE + CMP_LEN - 1 >= sj[:, None] * SLC_BLOCK)
               & (ci[None, :] * CMP_STRIDE <= sj[:, None] * SLC_BLOCK + SLC_BLOCK - 1)
               & (ci[None, :] < (seq - CMP_LEN) // CMP_STRIDE + 1))
    ot = jnp.asarray(overlap, dtype=BF16)
    onsa = _nsa_call(qpad, ksl, kwn, vslt, vwnt, kcmp, vcmpt, gt, ot, batch, seq)

    omem = _memattn_call(mq, mem.reshape(batch * mem.shape[1], d), w_mem_kv[0], batch, seq)

    x1 = _merge_call(x2, onsa, osgu, omem, w_all, w_br_nsa[0], w_br_sgu[0], w_br_mem[0], w_o[0],
                     ln1_g[0][None], ln1_b[0][None])

    assert EXPERTS_PER_GROUP == _R_SLOT and N_GROUPS <= _R_SLOT
    n_r = _R_SLOT + N_GROUPS * EXPERTS_PER_GROUP
    wr = jnp.concatenate([jnp.pad(w_router_group[0], ((0, 0), (0, _R_SLOT - N_GROUPS))),
                          w_router_expert[0]], axis=1)
    wr = jnp.pad(wr, ((0, 0), (0, LANES - n_r))).astype(BF16)
    br = jnp.concatenate([jnp.pad(b_router_group[0], (0, _R_SLOT - N_GROUPS)), b_router_expert[0]])
    br = jnp.pad(br, (0, LANES - n_r))[None]
    wg = w_exp_gate[0].transpose(0, 2, 1, 3).reshape(N_GROUPS, d, _HID).astype(BF16)
    wu = w_exp_up[0].transpose(0, 2, 1, 3).reshape(N_GROUPS, d, _HID).astype(BF16)
    wd = w_exp_down[0].reshape(N_GROUPS, _HID, d).astype(BF16)
    out = _moe_call(x1, wr, br, wg, wu, wd, ln2_g[0][None], ln2_b[0][None])
    return out.reshape(batch, seq, d)
```

```python
import functools

import numpy as np
import jax
import jax.numpy as jnp
from jax import lax
from jax.experimental import pallas as pl
from jax.experimental.pallas import tpu as pltpu

NSA_HEADS = 8
NSA_KV_GROUPS = 2
NSA_HPG = NSA_HEADS // NSA_KV_GROUPS
HEAD_DIM = 64
CMP_LEN = 32
CMP_STRIDE = 16
CMP_HIDDEN = 256
SLC_BLOCK = 64
SLC_TOPN = 8
WINDOW = 512
Q_BLOCK = 128
N_BAND = WINDOW // Q_BLOCK
ROPE_THETA = 500000.0
ROPE_DIM = HEAD_DIM // 4
SGU_CHUNK = 128
SGU_GROUPS = 8
SGU_WIDTH = 512
MEM_HEADS = 4
MEM_HEAD_DIM = 128
MEM_WIDTH = MEM_HEADS * MEM_HEAD_DIM
N_GROUPS = 4
EXPERTS_PER_GROUP = 8
EXPERT_FF = 256
DEPTH = 1
DN_ALPHA = (2.0 * DEPTH) ** 0.25
LN_EPS = 1e-5
NEG = -1e30
LOG2E = 1.4426950408889634

LANES = 128
Q_W = NSA_HEADS * HEAD_DIM
KV_W = NSA_KV_GROUPS * HEAD_DIM
GATE_W = NSA_HEADS * 3
VMEM_LIMIT = 56 * 1024 * 1024
MOE_VMEM_LIMIT = 60 * 1024 * 1024

BF16 = jnp.bfloat16
F32 = jnp.float32


def _dot(a, b):
    return jnp.dot(a, b, preferred_element_type=F32)


def _dot_nt(a, b):
    return lax.dot_general(a, b, (((1,), (1,)), ((), ())), preferred_element_type=F32)


def _sigmoid(x):
    return 1.0 / (1.0 + jnp.exp(-x))


def _gelu(x):
    return 0.5 * x * (1.0 + lax.erf(x * (2.0 ** -0.5)))


def _layer_norm(x, g, b):
    mu = jnp.mean(x, axis=-1, keepdims=True)
    xc = x - mu
    var = jnp.mean(xc * xc, axis=-1, keepdims=True)
    return xc * lax.rsqrt(var + LN_EPS) * g + b


PROJ_TM = 512
_ROPE_COLS = Q_W + 3 * KV_W
_V_OFF = _ROPE_COLS
_G_OFF = _V_OFF + 3 * KV_W
_SGU_OFF = _G_OFF + LANES
_MQ_OFF = _SGU_OFF + 2 * SGU_WIDTH
_PROJ_COLS = _MQ_OFF + MEM_WIDTH


def _store_strided_rows(val, out_ref, tmp_ref):
    tmp_ref[...] = val
    n = val.shape[0] // CMP_STRIDE
    for l in range(CMP_STRIDE):
        out_ref[:, l * LANES:(l + 1) * LANES] = tmp_ref[pl.ds(l, n, stride=CMP_STRIDE), :].astype(BF16)


def _proj_kernel(x_ref, w_ref, rope_ref, lng_ref, lnb_ref, ws_ref, bs_ref,
                 q_ref, kc_ref, ksl_ref, kwn_ref, vc_ref, vsl_ref, vwn_ref, gate_ref, osgu_ref, mq_ref,
                 a_s):
    xb = x_ref[...].astype(BF16)
    lane = lax.broadcasted_iota(jnp.int32, (PROJ_TM, LANES), 1)
    low = lane < HEAD_DIM
    cos = rope_ref[:, 0:LANES]
    s_dn = rope_ref[:, LANES:2 * LANES]
    s_up = rope_ref[:, 2 * LANES:3 * LANES]

    h = _dot_nt(xb, w_ref[0:_ROPE_COLS, :])
    k_refs = (kc_ref, ksl_ref, kwn_ref)
    for j in range(_ROPE_COLS // LANES):
        blk = h[:, j * LANES:(j + 1) * LANES]
        r = (blk * cos + pltpu.roll(blk, LANES - ROPE_DIM // 2, 1) * s_dn
             + pltpu.roll(blk, ROPE_DIM // 2, 1) * s_up)
        if j < Q_W // LANES:
            r = r * (HEAD_DIM ** -0.5 * LOG2E)
            sw = pltpu.roll(r, HEAD_DIM, 1)
            g = (2 * j) // NSA_HPG
            if g == 0:
                h0 = jnp.where(low, r, 0.0)
                h1 = jnp.where(low, sw, 0.0)
            else:
                h0 = jnp.where(low, 0.0, sw)
                h1 = jnp.where(low, 0.0, r)
            q_ref[:, (2 * j) * LANES:(2 * j + 1) * LANES] = h0.astype(BF16)
            q_ref[:, (2 * j + 1) * LANES:(2 * j + 2) * LANES] = h1.astype(BF16)
        else:
            if j == Q_W // LANES:
                _store_strided_rows(r, kc_ref, a_s)
            else:
                k_refs[j - Q_W // LANES][...] = r.astype(BF16)

    _store_strided_rows(_dot_nt(xb, w_ref[_V_OFF:_V_OFF + LANES, :]), vc_ref, a_s)
    ones = jnp.ones((V_ROWS - HEAD_DIM, PROJ_TM), BF16)
    for v_ref, row0, blk in ((vsl_ref, _V_OFF + LANES, SEL_CHUNK), (vwn_ref, _V_OFF + 2 * LANES, Q_BLOCK)):
        vt = _dot_nt(w_ref[row0:row0 + LANES, :], xb).astype(BF16)
        for g in range(NSA_KV_GROUPS):
            for b in range(PROJ_TM // blk):
                v_ref[g, b, 0:HEAD_DIM, :] = vt[g * HEAD_DIM:(g + 1) * HEAD_DIM, b * blk:(b + 1) * blk]
                v_ref[g, b, HEAD_DIM:V_ROWS, :] = ones[:, 0:blk]

    gate_ref[...] = _sigmoid(_dot_nt(w_ref[_G_OFF:_SGU_OFF, :], xb))

    mq_ref[...] = _dot_nt(xb, w_ref[_MQ_OFF:_PROJ_COLS, :]).astype(BF16)

    z = _gelu(_dot_nt(xb, w_ref[_SGU_OFF:_MQ_OFF, :]))
    u = z[:, 0:SGU_WIDTH]
    v = _layer_norm(z[:, SGU_WIDTH:2 * SGU_WIDTH], lng_ref[...], lnb_ref[...]).astype(BF16)
    lane_c = lax.broadcasted_iota(jnp.int32, (SGU_CHUNK, LANES), 1)
    low_c = lane_c < (SGU_WIDTH // SGU_GROUPS)
    n_ch = PROJ_TM // SGU_CHUNK
    for gp in range(SGU_WIDTH // LANES):
        cols = slice(gp * LANES, (gp + 1) * LANES)
        vcat = jnp.concatenate([v[ci * SGU_CHUNK:(ci + 1) * SGU_CHUNK, cols] for ci in range(n_ch)], axis=1)
        sv0 = _dot(ws_ref[2 * gp], vcat)
        sv1 = _dot(ws_ref[2 * gp + 1], vcat)
        for ci in range(n_ch):
            rows = slice(ci * SGU_CHUNK, (ci + 1) * SGU_CHUNK)
            lanes = slice(ci * LANES, (ci + 1) * LANES)
            sv = jnp.where(low_c, sv0[:, lanes], sv1[:, lanes]) + bs_ref[:, cols]
            osgu_ref[rows, cols] = (u[rows, cols] * sv).astype(BF16)


def _proj_call(x2, wp, rope, lng, lnb, ws, bs, seq):
    t = x2.shape[0]
    d = x2.shape[1]
    nt = t // PROJ_TM
    per_seq = seq // PROJ_TM
    row = lambda i: (i, 0)
    const2 = lambda i: (0, 0)
    assert PROJ_TM % SEL_CHUNK == 0 and PROJ_TM % Q_BLOCK == 0
    n_g = NSA_KV_GROUPS
    batch = t // seq
    vblock = lambda i: (i // per_seq, i % per_seq, 0, 0)

    def v_shape(blk):
        return jax.ShapeDtypeStruct((batch * n_g, seq // blk, V_ROWS, blk), BF16)

    def v_spec(blk):
        return pl.BlockSpec((n_g, PROJ_TM // blk, V_ROWS, blk), vblock)

    a_shape = jax.ShapeDtypeStruct((t // CMP_STRIDE, CMP_STRIDE * KV_W), BF16)
    a_spec = pl.BlockSpec((PROJ_TM // CMP_STRIDE, CMP_STRIDE * KV_W), row)
    k_shape = jax.ShapeDtypeStruct((t, KV_W), BF16)
    k_spec = pl.BlockSpec((PROJ_TM, KV_W), row)
    out_shapes = (
        jax.ShapeDtypeStruct((t, NSA_HEADS * LANES), BF16),
        a_shape, k_shape, k_shape, a_shape,
        v_shape(SEL_CHUNK), v_shape(Q_BLOCK),
        jax.ShapeDtypeStruct((LANES, t), F32),
        jax.ShapeDtypeStruct((t, SGU_WIDTH), BF16),
        jax.ShapeDtypeStruct((t, MEM_WIDTH), BF16),
    )
    out_specs = (
        pl.BlockSpec((PROJ_TM, NSA_HEADS * LANES), row),
        a_spec, k_spec, k_spec, a_spec,
        v_spec(SEL_CHUNK), v_spec(Q_BLOCK),
        pl.BlockSpec((LANES, PROJ_TM), lambda i: (0, i)),
        pl.BlockSpec((PROJ_TM, SGU_WIDTH), row),
        pl.BlockSpec((PROJ_TM, MEM_WIDTH), row),
    )
    return pl.pallas_call(
        _proj_kernel,
        grid=(nt,),
        in_specs=[
            pl.BlockSpec((PROJ_TM, d), row),
            pl.BlockSpec((_PROJ_COLS, d), const2),
            pl.BlockSpec((PROJ_TM, 3 * LANES), lambda i: (i % per_seq, 0)),
            pl.BlockSpec((1, SGU_WIDTH), const2),
            pl.BlockSpec((1, SGU_WIDTH), const2),
            pl.BlockSpec((SGU_GROUPS, SGU_CHUNK, SGU_CHUNK), lambda i: (0, 0, 0)),
            pl.BlockSpec((SGU_CHUNK, SGU_WIDTH), const2),
        ],
        out_specs=out_specs,
        out_shape=out_shapes,
        scratch_shapes=[pltpu.VMEM((PROJ_TM, KV_W), F32)],
        compiler_params=pltpu.CompilerParams(
            dimension_semantics=("parallel",), vmem_limit_bytes=VMEM_LIMIT),
        name="proj",
    )(x2, wp, rope, lng, lnb, ws, bs)


N_A = 128
A_W = CMP_STRIDE * KV_W


def _compress_kernel(ka_ref, va_ref, pek_ref, pev_ref, w1k_ref, w2k_ref, w1v_ref, w2v_ref,
                     kcmp_ref, vcmp_ref):
    def one(a_ref, pe_ref, w1_ref, w2_ref, out_ref):
        a = a_ref[...].astype(F32)
        top = (a + pe_ref[0:1, :]).astype(BF16)
        bot = (a + pe_ref[1:2, :]).astype(BF16)
        h1 = _dot(top, w1_ref[0])
        h2 = _dot(bot, w1_ref[1])
        pre = h1 + pltpu.roll(h2, N_A - 1, 0)
        act = _gelu(pre).astype(BF16)
        out_ref[...] = _dot(act, w2_ref[...]).astype(BF16)

    one(ka_ref, pek_ref, w1k_ref, w2k_ref, kcmp_ref)
    one(va_ref, pev_ref, w1v_ref, w2v_ref, vcmp_ref)


def _compress_call(ka, va, pek, pev, w1k, w2k, w1v, w2v, batch):
    row = lambda b: (b, 0)
    c2 = lambda b: (0, 0)
    c3 = lambda b: (0, 0, 0)
    hid2 = NSA_KV_GROUPS * CMP_HIDDEN
    return pl.pallas_call(
        _compress_kernel,
        grid=(batch,),
        in_specs=[
            pl.BlockSpec((N_A, A_W), row),
            pl.BlockSpec((N_A, A_W), row),
            pl.BlockSpec((8, A_W), c2),
            pl.BlockSpec((8, A_W), c2),
            pl.BlockSpec((2, A_W, hid2), c3),
            pl.BlockSpec((hid2, KV_W), c2),
            pl.BlockSpec((2, A_W, hid2), c3),
            pl.BlockSpec((hid2, KV_W), c2),
        ],
        out_specs=(pl.BlockSpec((N_A, KV_W), row), pl.BlockSpec((N_A, KV_W), row)),
        out_shape=(jax.ShapeDtypeStruct((batch * N_A, KV_W), BF16),
                   jax.ShapeDtypeStruct((batch * N_A, KV_W), BF16)),
        compiler_params=pltpu.CompilerParams(
            dimension_semantics=("parallel",), vmem_limit_bytes=VMEM_LIMIT),
        name="compress",
    )(ka, va, pek, pev, w1k, w2k, w1v, w2v)


N_SEL = 32
SEL_CHUNK = 512
SEL_SUB = 128
HQ = NSA_HPG * Q_BLOCK
V_ROWS = HEAD_DIM + 16


def _nsa_kernel(q_ref, ksl_ref, kwn_ref, vslt_ref, vwnt_ref, kcmp_ref, vcmpt_ref, gt_ref,
                ot_ref, o_ref, m_s, acc_s, ocmp_s, owin_s, cap_s, out_s):
    c = pl.program_id(1)
    lane_hq = lax.broadcasted_iota(jnp.int32, (1, HQ), 1)
    pos_hq = c * Q_BLOCK + (lane_hq & (Q_BLOCK - 1))
    pos_q = c * Q_BLOCK + lax.broadcasted_iota(jnp.int32, (1, Q_BLOCK), 1)
    groups = range(NSA_KV_GROUPS)
    qgs = [jnp.concatenate(
        [q_ref[:, (g * NSA_HPG + hh) * LANES:(g * NSA_HPG + hh + 1) * LANES]
         for hh in range(NSA_HPG)], axis=0) for g in groups]

    kcmp = kcmp_ref[...]
    s_c = [_dot_nt(kcmp, qgs[g]) for g in groups]
    n_idx = lax.broadcasted_iota(jnp.int32, (N_A, HQ), 0)
    valid_c = (n_idx * CMP_STRIDE + (CMP_LEN - 1)) <= pos_hq
    p_c = []
    for g in groups:
        sm_c = jnp.where(valid_c, s_c[g], NEG)
        m_c = jnp.max(sm_c, axis=0, keepdims=True)
        e_c = jnp.where(valid_c, jnp.exp2(sm_c - m_c), 0.0)
        d_c = jnp.sum(e_c, axis=0, keepdims=True)
        p_c.append(e_c / jnp.where(d_c > 0, d_c, 1.0))
    for g in groups:
        ocmp_s[g] = _dot(vcmpt_ref[g * HEAD_DIM:(g + 1) * HEAD_DIM, :], p_c[g].astype(BF16))

    ot = ot_ref[...]
    imp = []
    for g in groups:
        ps = (p_c[g][:, 0:Q_BLOCK] + p_c[g][:, Q_BLOCK:2 * Q_BLOCK]
              + p_c[g][:, 2 * Q_BLOCK:3 * Q_BLOCK] + p_c[g][:, 3 * Q_BLOCK:4 * Q_BLOCK])
        p_hi = ps.astype(BF16)
        r1 = ps - p_hi.astype(F32)
        p_lo = r1.astype(BF16)
        p_lo2 = (r1 - p_lo.astype(F32)).astype(BF16)
        imp.append(_dot(ot, p_hi) + _dot(ot, p_lo) + _dot(ot, p_lo2))

    j_idx = lax.broadcasted_iota(jnp.int32, (N_SEL, Q_BLOCK), 0)
    cur = pos_q // SLC_BLOCK
    future = j_idx > cur
    forced = (j_idx == 0) | (j_idx == cur) | (j_idx == cur - 1)
    imp = [jnp.where(future, NEG, jnp.where(forced, -NEG, imp[g])) for g in groups]
    rank = [jnp.zeros((N_SEL, Q_BLOCK), F32) for g in groups]
    for i in range(N_SEL):
        for g in groups:
            row = imp[g][i:i + 1, :]
            beats = (row > imp[g]) | ((row == imp[g]) & (j_idx > i))
            rank[g] = rank[g] + jnp.where(beats, 1.0, 0.0)
    for g in groups:
        cap_s[g] = jnp.where(rank[g] < float(SLC_TOPN), -NEG, NEG)
        m_s[g] = jnp.full((1, HQ), NEG, F32)
        acc_s[g] = jnp.zeros((V_ROWS, HQ), F32)

    _nsa_window(c, qgs, kwn_ref, vwnt_ref, owin_s)

    blocks_per_sub = SEL_SUB // SLC_BLOCK
    subs_per_chunk = SEL_CHUNK // SEL_SUB

    def sel_chunk(kc):
        spans = [slice(kc * SEL_CHUNK + a * SEL_SUB, kc * SEL_CHUNK + (a + 1) * SEL_SUB)
                 for a in range(subs_per_chunk)]
        s = [[_dot_nt(ksl_ref[ks, :], qgs[g]) for g in groups] for ks in spans]
        for a, ks in enumerate(spans):
            kpos = ks.start + lax.broadcasted_iota(jnp.int32, (SEL_SUB, Q_BLOCK), 0)
            causal = kpos <= pos_q
            for g in groups:
                j0 = ks.start // SLC_BLOCK
                cap = jnp.concatenate(
                    [jnp.broadcast_to(cap_s[g, j:j + 1, :], (SLC_BLOCK, Q_BLOCK))
                     for j in range(j0, j0 + blocks_per_sub)], axis=0)
                cap = jnp.where(causal, cap, NEG)
                sm = jnp.minimum(s[a][g], jnp.concatenate([cap] * NSA_HPG, axis=1))
                m_old = m_s[g]
                m_new = jnp.maximum(m_old, jnp.max(sm, axis=0, keepdims=True))
                alpha = jnp.exp2(m_old - m_new)
                e = jnp.exp2((sm - m_new).astype(BF16))
                v_t = vslt_ref[g, kc, :, a * SEL_SUB:(a + 1) * SEL_SUB]
                acc_s[g] = alpha * acc_s[g] + _dot(v_t, e)
                m_s[g] = m_new

    sel_chunk(0)
    for kc in range(1, ksl_ref.shape[0] // SEL_CHUNK):
        pl.when(kc * (SEL_CHUNK // Q_BLOCK) <= c)(functools.partial(sel_chunk, kc))

    for g in groups:
        acc = acc_s[g]
        o_sel = acc[0:HEAD_DIM, :] * (1.0 / acc[HEAD_DIM:HEAD_DIM + 1, :])

        def gate_row(br):
            return jnp.concatenate(
                [gt_ref[(g * NSA_HPG + hh) * 3 + br:(g * NSA_HPG + hh) * 3 + br + 1, :]
                 for hh in range(NSA_HPG)], axis=1)
        o_t = gate_row(0) * ocmp_s[g] + gate_row(1) * o_sel + gate_row(2) * owin_s[g]
        for hh in range(NSA_HPG):
            h = g * NSA_HPG + hh
            out_s[h * HEAD_DIM:(h + 1) * HEAD_DIM, :] = o_t[:, hh * Q_BLOCK:(hh + 1) * Q_BLOCK]

    o_ref[...] = out_s[...].T.astype(BF16)


def _nsa_window(c, qgs, kwn_ref, vwnt_ref, owin_s):
    groups = range(NSA_KV_GROUPS)
    q_i = lax.broadcasted_iota(jnp.int32, (Q_BLOCK, Q_BLOCK), 1)
    k_i = lax.broadcasted_iota(jnp.int32, (Q_BLOCK, Q_BLOCK), 0)
    blks, caps = [], []
    for i in range(N_BAND + 1):
        blk = c - N_BAND + i
        blks.append(jnp.maximum(blk, 0))
        off = (N_BAND - i) * Q_BLOCK
        if off - (Q_BLOCK - 1) >= 0 and off + (Q_BLOCK - 1) < WINDOW:
            caps.append(jnp.where(blk >= 0, -NEG, NEG))
        else:
            diff = off + q_i - k_i
            ok = (diff >= 0) & (diff < WINDOW) & (blk >= 0)
            caps.append(jnp.concatenate([jnp.where(ok, -NEG, NEG)] * NSA_HPG, axis=1))
    k_blocks = [kwn_ref[pl.ds(pl.multiple_of(blks[i] * Q_BLOCK, Q_BLOCK), Q_BLOCK), :]
                for i in range(N_BAND + 1)]
    s_w = [[None] * (N_BAND + 1) for g in groups]
    for i in reversed(range(N_BAND + 1)):
        for g in groups:
            s_w[g][i] = _dot_nt(k_blocks[i], qgs[g])
    m_w = [None for g in groups]
    o_win = [None for g in groups]
    for i in reversed(range(N_BAND + 1)):
        for g in groups:
            sm = jnp.minimum(s_w[g][i], caps[i])
            m_blk = jnp.max(sm, axis=0, keepdims=True)
            m_new = m_blk if m_w[g] is None else jnp.maximum(m_w[g], m_blk)
            pv = _dot(vwnt_ref[g, blks[i]], jnp.exp2((sm - m_new).astype(BF16)))
            o_win[g] = pv if m_w[g] is None else jnp.exp2(m_w[g] - m_new) * o_win[g] + pv
            m_w[g] = m_new
    for g in groups:
        owin_s[g] = o_win[g][0:HEAD_DIM, :] * (1.0 / o_win[g][HEAD_DIM:HEAD_DIM + 1, :])


def _nsa_call(qpad, ksl, kwn, vslt, vwnt, kcmp, vcmpt, gt, ot, batch, seq):
    nqb = seq // Q_BLOCK
    n_g = NSA_KV_GROUPS
    qrow = lambda b, c: (b * nqb + c, 0)
    brow = lambda b, c: (b, 0)
    c2 = lambda b, c: (0, 0)
    return pl.pallas_call(
        _nsa_kernel,
        grid=(batch, nqb),
        in_specs=[
            pl.BlockSpec((Q_BLOCK, NSA_HEADS * LANES), qrow),
            pl.BlockSpec((seq, KV_W), brow),
            pl.BlockSpec((seq, KV_W), brow),
            pl.BlockSpec((n_g, seq // SEL_CHUNK, V_ROWS, SEL_CHUNK), lambda b, c: (b, 0, 0, 0)),
            pl.BlockSpec((n_g, nqb, V_ROWS, Q_BLOCK), lambda b, c: (b, 0, 0, 0)),
            pl.BlockSpec((N_A, KV_W), brow),
            pl.BlockSpec((KV_W, N_A), brow),
            pl.BlockSpec((LANES, Q_BLOCK), lambda b, c: (0, b * nqb + c)),
            pl.BlockSpec((N_SEL, N_A), c2),
        ],
        out_specs=pl.BlockSpec((Q_BLOCK, Q_W), qrow),
        out_shape=jax.ShapeDtypeStruct((batch * seq, Q_W), BF16),
        scratch_shapes=[
            pltpu.VMEM((n_g, 1, HQ), F32),
            pltpu.VMEM((n_g, V_ROWS, HQ), F32),
            pltpu.VMEM((n_g, HEAD_DIM, HQ), F32),
            pltpu.VMEM((n_g, HEAD_DIM, HQ), F32),
            pltpu.VMEM((n_g, N_SEL, Q_BLOCK), F32),
            pltpu.VMEM((Q_W, Q_BLOCK), F32),
        ],
        compiler_params=pltpu.CompilerParams(
            dimension_semantics=("parallel", "arbitrary"), vmem_limit_bytes=VMEM_LIMIT),
        name="nsa",
    )(qpad, ksl, kwn, vslt, vwnt, kcmp, vcmpt, gt, ot)


MEM_TM = 512
MEM_V_ROWS = MEM_HEAD_DIM + 16


def _memattn_kernel(mq_ref, mem_ref, wkv_ref, o_ref, k_s, vt_s, out_s):
    @pl.when(pl.program_id(1) == 0)
    def _():
        kv = _dot(mem_ref[...].astype(BF16), wkv_ref[...].astype(BF16))
        k_s[...] = kv[:, 0:MEM_WIDTH].astype(BF16)
        vt = kv[:, MEM_WIDTH:2 * MEM_WIDTH].T.astype(BF16)
        for h in range(MEM_HEADS):
            vt_s[h, 0:MEM_HEAD_DIM, :] = vt[h * MEM_HEAD_DIM:(h + 1) * MEM_HEAD_DIM, :]
            vt_s[h, MEM_HEAD_DIM:, :] = jnp.ones((MEM_V_ROWS - MEM_HEAD_DIM, vt.shape[1]), BF16)

    heads = range(MEM_HEADS)
    cols = [slice(h * MEM_HEAD_DIM, (h + 1) * MEM_HEAD_DIM) for h in heads]
    s_t = [_dot_nt(k_s[:, cols[h]], mq_ref[:, cols[h]]) for h in heads]
    e = []
    for h in heads:
        m = jnp.max(s_t[h], axis=0, keepdims=True)
        e.append(jnp.exp2((s_t[h] - m) * (MEM_HEAD_DIM ** -0.5 * LOG2E)).astype(BF16))
    o_t = [_dot(vt_s[h], e[h]) for h in heads]
    for h in heads:
        out_s[cols[h], :] = o_t[h][0:MEM_HEAD_DIM, :] * (1.0 / o_t[h][MEM_HEAD_DIM:MEM_HEAD_DIM + 1, :])
    o_ref[...] = out_s[...].T.astype(BF16)


def _memattn_call(mq, mem2, wkv, batch, seq):
    m_len = mem2.shape[0] // batch
    d = mem2.shape[1]
    per = seq // MEM_TM
    return pl.pallas_call(
        _memattn_kernel,
        grid=(batch, per),
        in_specs=[
            pl.BlockSpec((MEM_TM, MEM_WIDTH), lambda b, i: (b * per + i, 0)),
            pl.BlockSpec((m_len, d), lambda b, i: (b, 0)),
            pl.BlockSpec((d, 2 * MEM_WIDTH), lambda b, i: (0, 0)),
        ],
        out_specs=pl.BlockSpec((MEM_TM, MEM_WIDTH), lambda b, i: (b * per + i, 0)),
        out_shape=jax.ShapeDtypeStruct((batch * seq, MEM_WIDTH), BF16),
        scratch_shapes=[
            pltpu.VMEM((m_len, MEM_WIDTH), BF16),
            pltpu.VMEM((MEM_HEADS, MEM_V_ROWS, m_len), BF16),
            pltpu.VMEM((MEM_WIDTH, MEM_TM), F32),
        ],
        compiler_params=pltpu.CompilerParams(
            dimension_semantics=("parallel", "arbitrary"), vmem_limit_bytes=VMEM_LIMIT),
        name="memattn",
    )(mq, mem2, wkv)


MERGE_TM = 512


def _merge_kernel(x_ref, onsa_ref, osgu_ref, omem_ref, wg_ref, wbn32_ref, wbs32_ref, wbm32_ref, wo32_ref,
                  g_ref, b_ref, out_ref, wbn_ref, wbs_ref, wbm_ref, wo_ref):
    @pl.when(pl.program_id(0) == 0)
    def _():
        for dst, src in ((wbn_ref, wbn32_ref), (wbs_ref, wbs32_ref), (wbm_ref, wbm32_ref),
                         (wo_ref, wo32_ref)):
            dst[...] = src[...].astype(BF16)

    x = x_ref[...]
    xb = x.astype(BF16)
    d = x.shape[1]
    y = None
    for br, (o_r, w_r) in enumerate(((onsa_ref, wbn_ref), (osgu_ref, wbs_ref), (omem_ref, wbm_ref))):
        gate = _sigmoid(_dot_nt(xb, wg_ref[br * d:(br + 1) * d, :]))
        term = gate * _dot(o_r[...], w_r[...])
        y = term if y is None else y + term
    z = DN_ALPHA * x + _dot(y.astype(BF16), wo_ref[...])
    out_ref[...] = _layer_norm(z, g_ref[...], b_ref[...])


def _merge_call(x2, onsa, osgu, omem, wg, wbn, wbs, wbm, wo, g1, b1):
    t, d = x2.shape
    row = lambda i: (i, 0)
    c2 = lambda i: (0, 0)
    full = lambda a: pl.BlockSpec(a.shape, c2)
    once = lambda a: pl.BlockSpec(a.shape, c2, pipeline_mode=pl.Buffered(1))
    return pl.pallas_call(
        _merge_kernel,
        grid=(t // MERGE_TM,),
        in_specs=[
            pl.BlockSpec((MERGE_TM, d), row),
            pl.BlockSpec((MERGE_TM, onsa.shape[1]), row),
            pl.BlockSpec((MERGE_TM, osgu.shape[1]), row),
            pl.BlockSpec((MERGE_TM, omem.shape[1]), row),
            pl.BlockSpec((3 * d, d), c2),
            once(wbn), once(wbs), once(wbm), once(wo), full(g1), full(b1),
        ],
        out_specs=pl.BlockSpec((MERGE_TM, d), row),
        out_shape=jax.ShapeDtypeStruct((t, d), F32),
        scratch_shapes=[pltpu.VMEM(a.shape, BF16) for a in (wbn, wbs, wbm, wo)],
        compiler_params=pltpu.CompilerParams(
            dimension_semantics=("arbitrary",), vmem_limit_bytes=VMEM_LIMIT),
        name="merge",
    )(x2, onsa, osgu, omem, wg, wbn, wbs, wbm, wo, g1, b1)


MOE_TM = 1024
MOE_CH = 144
_YS_ROWS = -(-(MOE_TM + N_GROUPS * MOE_CH) // 256) * 256
_R_SLOT = 8
_DEST_LANE = 3 * _R_SLOT
_HID = EXPERTS_PER_GROUP * EXPERT_FF


def _moe_kernel(x_ref, tri_ref, wr_ref, br_ref, wg_ref, wu_ref, wd_ref, g_ref, b_ref, out_ref,
                xa_s, tok_s, keyr_s, ys_s, cnt_s, base_s):
    grp = pl.program_id(1)
    tm = x_ref.shape[0]
    d = x_ref.shape[1]

    @pl.when(grp == 0)
    def _route():
        xb = x_ref[...].astype(BF16)
        xa_s[:, 0:d] = xb
        logits = _dot(xb, wr_ref[...]) + br_ref[...]
        lt = logits.T
        row = lax.broadcasted_iota(jnp.int32, (_R_SLOT, tm), 0)
        gl = jnp.where(row < N_GROUPS, lt[0:_R_SLOT], NEG)
        gmax = jnp.max(gl, axis=0, keepdims=True)
        gidx = jnp.min(jnp.where(gl == gmax, row, _R_SLOT), axis=0, keepdims=True)
        gprob = 1.0 / jnp.sum(jnp.exp(gl - gmax), axis=0, keepdims=True)
        el = lt[_R_SLOT:2 * _R_SLOT]
        for k in range(1, N_GROUPS):
            el = jnp.where(gidx == k, lt[(k + 1) * _R_SLOT:(k + 2) * _R_SLOT], el)
        ee = jnp.exp(el - jnp.max(el, axis=0, keepdims=True))
        ep = ee / jnp.sum(ee, axis=0, keepdims=True)
        t1 = jnp.max(ep, axis=0, keepdims=True)
        i1 = jnp.min(jnp.where(ep == t1, row, _R_SLOT), axis=0, keepdims=True)
        rest = row != i1
        t2 = jnp.max(jnp.where(rest, ep, -1.0), axis=0, keepdims=True)
        i2 = jnp.min(jnp.where(rest & (ep == t2), row, _R_SLOT), axis=0, keepdims=True)
        den = t1 + t2
        cwf = (jnp.where(row == i1, t1 / den, 0.0) + jnp.where(row == i2, t2 / den, 0.0)) * gprob
        hi = cwf.astype(BF16).astype(F32)
        r1 = cwf - hi
        lo = r1.astype(BF16).astype(F32)
        lo2 = (r1 - lo).astype(BF16).astype(F32)

        onehot = jnp.where(row == gidx, 1.0, 0.0)
        rank = _dot_nt(onehot.astype(BF16), tri_ref[...])
        keyr_s[...] = jnp.where(onehot > 0.5, rank, -1.0)
        base = jnp.int32(0)
        basev = jnp.zeros((1, tm), F32)
        for k in range(N_GROUPS):
            n_k = jnp.sum(onehot[k:k + 1, :]).astype(jnp.int32)
            cnt_s[k] = n_k
            base_s[k] = base
            basev = jnp.where(gidx == k, base.astype(F32), basev)
            base = base + ((n_k + MOE_CH - 1) // MOE_CH) * MOE_CH
        dest = jnp.sum(rank * onehot, axis=0, keepdims=True) + basev
        tok = jnp.concatenate(
            [hi, lo, lo2, jnp.broadcast_to(dest, (_R_SLOT, tm)),
             jnp.zeros((LANES - 4 * _R_SLOT, tm), F32)], axis=0).T
        tok_s[...] = tok
        lane = lax.broadcasted_iota(jnp.int32, tok.shape, 1)
        xa_s[:, d:d + LANES] = jnp.where(lane < _DEST_LANE, tok, 0.0).astype(BF16)
        ys_s[...] = jnp.zeros(ys_s.shape, BF16)

    n_rows = cnt_s[grp]
    row0 = base_s[grp]
    keyr = keyr_s[pl.ds(grp, 1), :]
    half = _HID // 2

    def sweep(k, carry):
        ch = MOE_CH
        r_row = (lax.broadcasted_iota(jnp.int32, (ch, tm), 0) + k * MOE_CH).astype(F32)
        pick = jnp.where(keyr == r_row, 1.0, 0.0).astype(BF16)
        ga = _dot(pick, xa_s[...])
        xg = ga[:, 0:d].astype(BF16)
        cwg = ga[:, d:d + LANES]
        cs = (cwg + pltpu.roll(cwg, LANES - _R_SLOT, 1)
              + pltpu.roll(cwg, LANES - 2 * _R_SLOT, 1))
        hg = [_dot(xg, wg_ref[0, :, h * half:(h + 1) * half]) for h in range(2)]
        hu = [_dot(xg, wu_ref[0, :, h * half:(h + 1) * half]) for h in range(2)]
        y = None
        for h in range(2):
            per_half = EXPERTS_PER_GROUP // 2
            cexp = jnp.concatenate(
                [jnp.broadcast_to(cs[:, e:e + 1], (ch, EXPERT_FF))
                 for e in range(h * per_half, (h + 1) * per_half)], axis=1)
            hid = ((hg[h] * _sigmoid(hg[h])) * hu[h]) * cexp
            term = _dot(hid.astype(BF16), wd_ref[0, h * half:(h + 1) * half, :])
            y = term if y is None else y + term
        ys_s[pl.ds(pl.multiple_of(row0 + k * MOE_CH, 16), ch), :] = y.astype(BF16)
        return carry

    lax.fori_loop(0, (n_rows + MOE_CH - 1) // MOE_CH, sweep, 0)

    @pl.when(grp == N_GROUPS - 1)
    def _fin():
        r_col = lax.broadcasted_iota(jnp.int32, (tm, _YS_ROWS), 1).astype(F32)
        put = jnp.where(tok_s[:, _DEST_LANE:_DEST_LANE + 1] == r_col, 1.0, 0.0).astype(BF16)
        z = DN_ALPHA * x_ref[...] + _dot(put, ys_s[...])
        out_ref[...] = _layer_norm(z, g_ref[...], b_ref[...])


def _moe_call(x1, wr, br, wg, wu, wd, g2, b2):
    t, d = x1.shape
    row = lambda i, g: (i, 0)
    c2 = lambda i, g: (0, 0)
    idx = np.arange(MOE_TM)
    tri = jnp.asarray(idx[None, :] < idx[:, None], dtype=BF16)
    return pl.pallas_call(
        _moe_kernel,
        grid=(t // MOE_TM, N_GROUPS),
        in_specs=[
            pl.BlockSpec((MOE_TM, d), row, pipeline_mode=pl.Buffered(1)),
            pl.BlockSpec((MOE_TM, MOE_TM), c2, pipeline_mode=pl.Buffered(1)),
            pl.BlockSpec(wr.shape, c2),
            pl.BlockSpec(br.shape, c2),
            pl.BlockSpec((1, d, _HID), lambda i, g: (g, 0, 0)),
            pl.BlockSpec((1, d, _HID), lambda i, g: (g, 0, 0)),
            pl.BlockSpec((1, _HID, d), lambda i, g: (g, 0, 0)),
            pl.BlockSpec(g2.shape, c2),
            pl.BlockSpec(b2.shape, c2),
        ],
        out_specs=pl.BlockSpec((MOE_TM, d), row),
        out_shape=jax.ShapeDtypeStruct((t, d), F32),
        scratch_shapes=[
            pltpu.VMEM((MOE_TM, d + LANES), BF16),
            pltpu.VMEM((MOE_TM, LANES), F32),
            pltpu.VMEM((_R_SLOT, MOE_TM), F32),
            pltpu.VMEM((_YS_ROWS, d), BF16),
            pltpu.SMEM((N_GROUPS,), jnp.int32),
            pltpu.SMEM((N_GROUPS,), jnp.int32),
        ],
        compiler_params=pltpu.CompilerParams(
            dimension_semantics=("parallel", "arbitrary"), vmem_limit_bytes=MOE_VMEM_LIMIT),
        name="moe",
    )(x1, tri, wr, br, wg, wu, wd, g2, b2)


def _rope_table(seq):
    half = ROPE_DIM // 2
    inv = ROPE_THETA ** (-jnp.arange(0, ROPE_DIM, 2, dtype=F32) / ROPE_DIM)
    ang = jnp.arange(seq, dtype=F32)[:, None] * inv[None, :]
    cos, sin = jnp.cos(ang), jnp.sin(ang)
    rest = HEAD_DIM - ROPE_DIM
    one = jnp.ones((seq, rest), F32)
    zero = jnp.zeros((seq, rest), F32)
    zh = jnp.zeros((seq, half), F32)
    c_h = jnp.concatenate([cos, cos, one], axis=1)
    dn_h = jnp.concatenate([-sin, zh, zero], axis=1)
    up_h = jnp.concatenate([zh, sin, zero], axis=1)
    rep = LANES // HEAD_DIM
    return jnp.concatenate([jnp.tile(c_h, (1, rep)), jnp.tile(dn_h, (1, rep)), jnp.tile(up_h, (1, rep))],
                           axis=1)


def _expand_cmp_weights(pe, w1, w2):
    g_n = NSA_KV_GROUPS
    w1r = w1.astype(BF16).reshape(2, CMP_STRIDE, HEAD_DIM, CMP_HIDDEN)
    w2b = w2.astype(BF16)
    w1e = jnp.concatenate(
        [jnp.stack([w1r if g == e else jnp.zeros_like(w1r) for g in range(g_n)], axis=2)
         .reshape(2, A_W, CMP_HIDDEN) for e in range(g_n)], axis=2)
    w2e = jnp.concatenate(
        [jnp.concatenate([w2b if g == e else jnp.zeros_like(w2b) for e in range(g_n)], axis=1)
         for g in range(g_n)], axis=0)
    per = pe.reshape(2, CMP_STRIDE, 1, HEAD_DIM)
    pee = jnp.broadcast_to(per, (2, CMP_STRIDE, g_n, HEAD_DIM)).reshape(2, A_W)
    pee = jnp.concatenate([pee, jnp.zeros((6, A_W), pe.dtype)], axis=0)
    return pee.astype(F32), w1e.astype(BF16), w2e.astype(BF16)


def kernel(x, mem, w_in, cmp_pe_k, cmp_w1_k, cmp_w2_k, cmp_pe_v, cmp_w1_v, cmp_w2_v, sgu_ln_g, sgu_ln_b,
           sgu_w_s, sgu_b_s, w_mem_kv, w_br_nsa, w_br_sgu, w_br_mem, w_o, ln1_g, ln1_b, w_router_group,
           b_router_group, w_router_expert, b_router_expert, w_exp_gate, w_exp_up, w_exp_down, ln2_g, ln2_b):
    batch, seq, d = x.shape
    t = batch * seq
    assert w_in.shape[0] == DEPTH == 1
    assert seq % PROJ_TM == 0 and seq // CMP_STRIDE == N_A and seq // SLC_BLOCK == N_SEL

    offs = [int(v) for v in np.cumsum(
        [0, Q_W, KV_W, KV_W, KV_W, KV_W, KV_W, KV_W, GATE_W, 2 * SGU_WIDTH, MEM_WIDTH, 3 * d])]
    assert offs[-1] == w_in.shape[2]
    w_t = jnp.transpose(w_in[0])
    seg = lambda i: w_t[offs[i]:offs[i + 1]]
    gate_rows = jnp.pad(seg(7), ((0, LANES - GATE_W), (0, 0)))
    wp_t = jnp.concatenate([seg(0), seg(1), seg(3), seg(5), seg(2), seg(4), seg(6), gate_rows, seg(8), seg(9)],
                           axis=0).astype(BF16)
    w_mt = seg(10).astype(BF16)
    assert wp_t.shape[0] == _PROJ_COLS
    rope = _rope_table(seq)
    tril = jnp.tril(jnp.ones((SGU_CHUNK, SGU_CHUNK), dtype=bool))
    ws = jnp.where(tril[None], sgu_w_s[0], 0.0).astype(BF16)
    bs = jnp.repeat(sgu_b_s[0].T, SGU_WIDTH // SGU_GROUPS, axis=1)

    x2 = x.reshape(t, d)
    (qpad, kc, ksl, kwn, vc, vslt, vwnt, gt, osgu, mq) = _proj_call(
        x2, wp_t, rope, sgu_ln_g[0][None], sgu_ln_b[0][None], ws, bs, seq)

    pek, w1k, w2k = _expand_cmp_weights(cmp_pe_k[0], cmp_w1_k[0], cmp_w2_k[0])
    pev, w1v, w2v = _expand_cmp_weights(cmp_pe_v[0], cmp_w1_v[0], cmp_w2_v[0])
    kcmp, vcmp = _compress_call(kc, vc, pek, pev, w1k, w2k, w1v, w2v, batch)

    nqb = seq // Q_BLOCK
    n_g = NSA_KV_GROUPS
    vcmpt = vcmp.reshape(batch, N_A, KV_W).transpose(0, 2, 1).reshape(batch * KV_W, N_A)
    ci = np.arange(N_A)
    sj = np.arange(N_SEL)
    overlap = ((ci[None, :] * CMP_STRIDE + CMP_LEN - 1 >= sj[:, None] * SLC_BLOCK)
               & (ci[None, :] * CMP_STRIDE <= sj[:, None] * SLC_BLOCK + SLC_BLOCK - 1)
               & (ci[None, :] < (seq - CMP_LEN) // CMP_STRIDE + 1))
    ot = jnp.asarray(overlap, dtype=BF16)
    onsa = _nsa_call(qpad, ksl, kwn, vslt, vwnt, kcmp, vcmpt, gt, ot, batch, seq)

    omem = _memattn_call(mq, mem.reshape(batch * mem.shape[1], d), w_mem_kv[0], batch, seq)

    x1 = _merge_call(x2, onsa, osgu, omem, w_mt, w_br_nsa[0], w_br_sgu[0], w_br_mem[0], w_o[0],
                     ln1_g[0][None], ln1_b[0][None])

    assert EXPERTS_PER_GROUP == _R_SLOT and N_GROUPS <= _R_SLOT
    n_r = _R_SLOT + N_GROUPS * EXPERTS_PER_GROUP
    wr = jnp.concatenate([jnp.pad(w_router_group[0], ((0, 0), (0, _R_SLOT - N_GROUPS))),
                          w_router_expert[0]], axis=1)
    wr = jnp.pad(wr, ((0, 0), (0, LANES - n_r))).astype(BF16)
    br = jnp.concatenate([jnp.pad(b_router_group[0], (0, _R_SLOT - N_GROUPS)), b_router_expert[0]])
    br = jnp.pad(br, (0, LANES - n_r))[None]
    wg = w_exp_gate[0].transpose(0, 2, 1, 3).reshape(N_GROUPS, d, _HID).astype(BF16)
    wu = w_exp_up[0].transpose(0, 2, 1, 3).reshape(N_GROUPS, d, _HID).astype(BF16)
    wd = w_exp_down[0].reshape(N_GROUPS, _HID, d).astype(BF16)
    out = _moe_call(x1, wr, br, wg, wu, wd, ln2_g[0][None], ln2_b[0][None])
    return out.reshape(batch, seq, d)
```

```python
import functools

import numpy as np
import jax
import jax.numpy as jnp
from jax import lax
from jax.experimental import pallas as pl
from jax.experimental.pallas import tpu as pltpu

NSA_HEADS = 8
NSA_KV_GROUPS = 2
NSA_HPG = NSA_HEADS // NSA_KV_GROUPS
HEAD_DIM = 64
CMP_LEN = 32
CMP_STRIDE = 16
CMP_HIDDEN = 256
SLC_BLOCK = 64
SLC_TOPN = 8
WINDOW = 512
Q_BLOCK = 128
N_BAND = WINDOW // Q_BLOCK
ROPE_THETA = 500000.0
ROPE_DIM = HEAD_DIM // 4
SGU_CHUNK = 128
SGU_GROUPS = 8
SGU_WIDTH = 512
MEM_HEADS = 4
MEM_HEAD_DIM = 128
MEM_WIDTH = MEM_HEADS * MEM_HEAD_DIM
N_GROUPS = 4
EXPERTS_PER_GROUP = 8
EXPERT_FF = 256
DEPTH = 1
DN_ALPHA = (2.0 * DEPTH) ** 0.25
LN_EPS = 1e-5
NEG = -1e30
LOG2E = 1.4426950408889634

LANES = 128
Q_W = NSA_HEADS * HEAD_DIM
KV_W = NSA_KV_GROUPS * HEAD_DIM
GATE_W = NSA_HEADS * 3
VMEM_LIMIT = 56 * 1024 * 1024
MOE_VMEM_LIMIT = 60 * 1024 * 1024

BF16 = jnp.bfloat16
F32 = jnp.float32


def _dot(a, b):
    return jnp.dot(a, b, preferred_element_type=F32)


def _dot_nt(a, b):
    return lax.dot_general(a, b, (((1,), (1,)), ((), ())), preferred_element_type=F32)


def _sigmoid(x):
    return 1.0 / (1.0 + jnp.exp(-x))


def _gelu(x):
    return 0.5 * x * (1.0 + lax.erf(x * (2.0 ** -0.5)))


def _layer_norm(x, g, b):
    mu = jnp.mean(x, axis=-1, keepdims=True)
    xc = x - mu
    var = jnp.mean(xc * xc, axis=-1, keepdims=True)
    return xc * lax.rsqrt(var + LN_EPS) * g + b


PROJ_TM = 512
_ROPE_COLS = Q_W + 3 * KV_W
_V_OFF = _ROPE_COLS
_G_OFF = _V_OFF + 3 * KV_W
_SGU_OFF = _G_OFF + LANES
_MQ_OFF = _SGU_OFF + 2 * SGU_WIDTH
_PROJ_COLS = _MQ_OFF + MEM_WIDTH


def _store_strided_rows(val, out_ref, tmp_ref):
    tmp_ref[...] = val
    n = val.shape[0] // CMP_STRIDE
    for l in range(CMP_STRIDE):
        out_ref[:, l * LANES:(l + 1) * LANES] = tmp_ref[pl.ds(l, n, stride=CMP_STRIDE), :].astype(BF16)


def _transpose_weight(wt_ref, w_s):
    for j in range(wt_ref.shape[0] // LANES):
        rows = slice(j * LANES, (j + 1) * LANES)
        w_s[:, rows] = wt_ref[rows, :].astype(F32).T.astype(BF16)


def _proj_kernel(x_ref, wt_ref, rope_ref, lng_ref, lnb_ref, ws_ref, bs_ref,
                 q_ref, kc_ref, ksl_ref, kwn_ref, vc_ref, vsl_ref, vwn_ref, gate_ref, osgu_ref, mq_ref,
                 a_s, w_ref):
    pl.when(pl.program_id(0) == 0)(functools.partial(_transpose_weight, wt_ref, w_ref))
    xb = x_ref[...].astype(BF16)
    lane = lax.broadcasted_iota(jnp.int32, (PROJ_TM, LANES), 1)
    low = lane < HEAD_DIM
    cos = rope_ref[:, 0:LANES]
    s_dn = rope_ref[:, LANES:2 * LANES]
    s_up = rope_ref[:, 2 * LANES:3 * LANES]

    h = _dot(xb, w_ref[:, 0:_ROPE_COLS])
    k_refs = (kc_ref, ksl_ref, kwn_ref)
    for j in range(_ROPE_COLS // LANES):
        blk = h[:, j * LANES:(j + 1) * LANES]
        r = (blk * cos + pltpu.roll(blk, LANES - ROPE_DIM // 2, 1) * s_dn
             + pltpu.roll(blk, ROPE_DIM // 2, 1) * s_up)
        if j < Q_W // LANES:
            r = r * (HEAD_DIM ** -0.5 * LOG2E)
            sw = pltpu.roll(r, HEAD_DIM, 1)
            g = (2 * j) // NSA_HPG
            if g == 0:
                h0 = jnp.where(low, r, 0.0)
                h1 = jnp.where(low, sw, 0.0)
            else:
                h0 = jnp.where(low, 0.0, sw)
                h1 = jnp.where(low, 0.0, r)
            q_ref[:, (2 * j) * LANES:(2 * j + 1) * LANES] = h0.astype(BF16)
            q_ref[:, (2 * j + 1) * LANES:(2 * j + 2) * LANES] = h1.astype(BF16)
        else:
            if j == Q_W // LANES:
                _store_strided_rows(r, kc_ref, a_s)
            else:
                k_refs[j - Q_W // LANES][...] = r.astype(BF16)

    _store_strided_rows(_dot(xb, w_ref[:, _V_OFF:_V_OFF + LANES]), vc_ref, a_s)
    ones = jnp.ones((V_ROWS - HEAD_DIM, PROJ_TM), BF16)
    for v_ref, row0, blk in ((vsl_ref, _V_OFF + LANES, SEL_CHUNK), (vwn_ref, _V_OFF + 2 * LANES, Q_BLOCK)):
        vt = _dot_nt(wt_ref[row0:row0 + LANES, :], xb).astype(BF16)
        for g in range(NSA_KV_GROUPS):
            for b in range(PROJ_TM // blk):
                v_ref[g, b, 0:HEAD_DIM, :] = vt[g * HEAD_DIM:(g + 1) * HEAD_DIM, b * blk:(b + 1) * blk]
                v_ref[g, b, HEAD_DIM:V_ROWS, :] = ones[:, 0:blk]

    gate_ref[...] = _sigmoid(_dot_nt(wt_ref[_G_OFF:_SGU_OFF, :], xb))

    mq_ref[...] = _dot(xb, w_ref[:, _MQ_OFF:_PROJ_COLS]).astype(BF16)

    z = _gelu(_dot(xb, w_ref[:, _SGU_OFF:_MQ_OFF]))
    u = z[:, 0:SGU_WIDTH]
    v = _layer_norm(z[:, SGU_WIDTH:2 * SGU_WIDTH], lng_ref[...], lnb_ref[...]).astype(BF16)
    lane_c = lax.broadcasted_iota(jnp.int32, (SGU_CHUNK, LANES), 1)
    low_c = lane_c < (SGU_WIDTH // SGU_GROUPS)
    n_ch = PROJ_TM // SGU_CHUNK
    for gp in range(SGU_WIDTH // LANES):
        cols = slice(gp * LANES, (gp + 1) * LANES)
        vcat = jnp.concatenate([v[ci * SGU_CHUNK:(ci + 1) * SGU_CHUNK, cols] for ci in range(n_ch)], axis=1)
        sv0 = _dot(ws_ref[2 * gp], vcat)
        sv1 = _dot(ws_ref[2 * gp + 1], vcat)
        for ci in range(n_ch):
            rows = slice(ci * SGU_CHUNK, (ci + 1) * SGU_CHUNK)
            lanes = slice(ci * LANES, (ci + 1) * LANES)
            sv = jnp.where(low_c, sv0[:, lanes], sv1[:, lanes]) + bs_ref[:, cols]
            osgu_ref[rows, cols] = (u[rows, cols] * sv).astype(BF16)


def _proj_call(x2, wp, rope, lng, lnb, ws, bs, seq):
    t = x2.shape[0]
    d = x2.shape[1]
    nt = t // PROJ_TM
    per_seq = seq // PROJ_TM
    row = lambda i: (i, 0)
    const2 = lambda i: (0, 0)
    assert PROJ_TM % SEL_CHUNK == 0 and PROJ_TM % Q_BLOCK == 0
    n_g = NSA_KV_GROUPS
    batch = t // seq
    vblock = lambda i: (i // per_seq, i % per_seq, 0, 0)

    def v_shape(blk):
        return jax.ShapeDtypeStruct((batch * n_g, seq // blk, V_ROWS, blk), BF16)

    def v_spec(blk):
        return pl.BlockSpec((n_g, PROJ_TM // blk, V_ROWS, blk), vblock)

    a_shape = jax.ShapeDtypeStruct((t // CMP_STRIDE, CMP_STRIDE * KV_W), BF16)
    a_spec = pl.BlockSpec((PROJ_TM // CMP_STRIDE, CMP_STRIDE * KV_W), row)
    k_shape = jax.ShapeDtypeStruct((t, KV_W), BF16)
    k_spec = pl.BlockSpec((PROJ_TM, KV_W), row)
    out_shapes = (
        jax.ShapeDtypeStruct((t, NSA_HEADS * LANES), BF16),
        a_shape, k_shape, k_shape, a_shape,
        v_shape(SEL_CHUNK), v_shape(Q_BLOCK),
        jax.ShapeDtypeStruct((LANES, t), F32),
        jax.ShapeDtypeStruct((t, SGU_WIDTH), BF16),
        jax.ShapeDtypeStruct((t, MEM_WIDTH), BF16),
    )
    out_specs = (
        pl.BlockSpec((PROJ_TM, NSA_HEADS * LANES), row),
        a_spec, k_spec, k_spec, a_spec,
        v_spec(SEL_CHUNK), v_spec(Q_BLOCK),
        pl.BlockSpec((LANES, PROJ_TM), lambda i: (0, i)),
        pl.BlockSpec((PROJ_TM, SGU_WIDTH), row),
        pl.BlockSpec((PROJ_TM, MEM_WIDTH), row),
    )
    return pl.pallas_call(
        _proj_kernel,
        grid=(nt,),
        in_specs=[
            pl.BlockSpec((PROJ_TM, d), row),
            pl.BlockSpec((_PROJ_COLS, d), const2),
            pl.BlockSpec((PROJ_TM, 3 * LANES), lambda i: (i % per_seq, 0)),
            pl.BlockSpec((1, SGU_WIDTH), const2),
            pl.BlockSpec((1, SGU_WIDTH), const2),
            pl.BlockSpec((SGU_GROUPS, SGU_CHUNK, SGU_CHUNK), lambda i: (0, 0, 0)),
            pl.BlockSpec((SGU_CHUNK, SGU_WIDTH), const2),
        ],
        out_specs=out_specs,
        out_shape=out_shapes,
        scratch_shapes=[pltpu.VMEM((PROJ_TM, KV_W), F32), pltpu.VMEM((d, _PROJ_COLS), BF16)],
        compiler_params=pltpu.CompilerParams(
            dimension_semantics=("arbitrary",), vmem_limit_bytes=VMEM_LIMIT),
        name="proj",
    )(x2, wp, rope, lng, lnb, ws, bs)


N_A = 128
A_W = CMP_STRIDE * KV_W


def _compress_kernel(ka_ref, va_ref, pek_ref, pev_ref, w1k_ref, w2k_ref, w1v_ref, w2v_ref,
                     kcmp_ref, vcmp_ref):
    def one(a_ref, pe_ref, w1_ref, w2_ref, out_ref):
        a = a_ref[...].astype(F32)
        top = (a + pe_ref[0:1, :]).astype(BF16)
        bot = (a + pe_ref[1:2, :]).astype(BF16)
        h1 = _dot(top, w1_ref[0])
        h2 = _dot(bot, w1_ref[1])
        pre = h1 + pltpu.roll(h2, N_A - 1, 0)
        act = _gelu(pre).astype(BF16)
        out_ref[...] = _dot(act, w2_ref[...]).astype(BF16)

    one(ka_ref, pek_ref, w1k_ref, w2k_ref, kcmp_ref)
    one(va_ref, pev_ref, w1v_ref, w2v_ref, vcmp_ref)


def _compress_call(ka, va, pek, pev, w1k, w2k, w1v, w2v, batch):
    row = lambda b: (b, 0)
    c2 = lambda b: (0, 0)
    c3 = lambda b: (0, 0, 0)
    hid2 = NSA_KV_GROUPS * CMP_HIDDEN
    return pl.pallas_call(
        _compress_kernel,
        grid=(batch,),
        in_specs=[
            pl.BlockSpec((N_A, A_W), row),
            pl.BlockSpec((N_A, A_W), row),
            pl.BlockSpec((8, A_W), c2),
            pl.BlockSpec((8, A_W), c2),
            pl.BlockSpec((2, A_W, hid2), c3),
            pl.BlockSpec((hid2, KV_W), c2),
            pl.BlockSpec((2, A_W, hid2), c3),
            pl.BlockSpec((hid2, KV_W), c2),
        ],
        out_specs=(pl.BlockSpec((N_A, KV_W), row), pl.BlockSpec((N_A, KV_W), row)),
        out_shape=(jax.ShapeDtypeStruct((batch * N_A, KV_W), BF16),
                   jax.ShapeDtypeStruct((batch * N_A, KV_W), BF16)),
        compiler_params=pltpu.CompilerParams(
            dimension_semantics=("parallel",), vmem_limit_bytes=VMEM_LIMIT),
        name="compress",
    )(ka, va, pek, pev, w1k, w2k, w1v, w2v)


N_SEL = 32
SEL_CHUNK = 512
SEL_SUB = 128
HQ = NSA_HPG * Q_BLOCK
V_ROWS = HEAD_DIM + 16


def _nsa_kernel(q_ref, ksl_ref, kwn_ref, vslt_ref, vwnt_ref, kcmp_ref, vcmpt_ref, gt_ref,
                ot_ref, o_ref, m_s, acc_s, ocmp_s, owin_s, cap_s, out_s):
    c = pl.program_id(1)
    lane_hq = lax.broadcasted_iota(jnp.int32, (1, HQ), 1)
    pos_hq = c * Q_BLOCK + (lane_hq & (Q_BLOCK - 1))
    pos_q = c * Q_BLOCK + lax.broadcasted_iota(jnp.int32, (1, Q_BLOCK), 1)
    groups = range(NSA_KV_GROUPS)
    qgs = [jnp.concatenate(
        [q_ref[:, (g * NSA_HPG + hh) * LANES:(g * NSA_HPG + hh + 1) * LANES]
         for hh in range(NSA_HPG)], axis=0) for g in groups]

    kcmp = kcmp_ref[...]
    s_c = [_dot_nt(kcmp, qgs[g]) for g in groups]
    n_idx = lax.broadcasted_iota(jnp.int32, (N_A, HQ), 0)
    valid_c = (n_idx * CMP_STRIDE + (CMP_LEN - 1)) <= pos_hq
    p_c = []
    for g in groups:
        sm_c = jnp.where(valid_c, s_c[g], NEG)
        m_c = jnp.max(sm_c, axis=0, keepdims=True)
        e_c = jnp.where(valid_c, jnp.exp2(sm_c - m_c), 0.0)
        d_c = jnp.sum(e_c, axis=0, keepdims=True)
        p_c.append(e_c / jnp.where(d_c > 0, d_c, 1.0))
    for g in groups:
        ocmp_s[g] = _dot(vcmpt_ref[g * HEAD_DIM:(g + 1) * HEAD_DIM, :], p_c[g].astype(BF16))

    ot = ot_ref[...]
    imp = []
    for g in groups:
        ps = (p_c[g][:, 0:Q_BLOCK] + p_c[g][:, Q_BLOCK:2 * Q_BLOCK]
              + p_c[g][:, 2 * Q_BLOCK:3 * Q_BLOCK] + p_c[g][:, 3 * Q_BLOCK:4 * Q_BLOCK])
        p_hi = ps.astype(BF16)
        r1 = ps - p_hi.astype(F32)
        p_lo = r1.astype(BF16)
        p_lo2 = (r1 - p_lo.astype(F32)).astype(BF16)
        imp.append(_dot(ot, p_hi) + _dot(ot, p_lo) + _dot(ot, p_lo2))

    j_idx = lax.broadcasted_iota(jnp.int32, (N_SEL, Q_BLOCK), 0)
    cur = pos_q // SLC_BLOCK
    future = j_idx > cur
    forced = (j_idx == 0) | (j_idx == cur) | (j_idx == cur - 1)
    imp = [jnp.where(future, NEG, jnp.where(forced, -NEG, imp[g])) for g in groups]
    rank = [jnp.zeros((N_SEL, Q_BLOCK), F32) for g in groups]
    for i in range(N_SEL):
        for g in groups:
            row = imp[g][i:i + 1, :]
            beats = (row > imp[g]) | ((row == imp[g]) & (j_idx > i))
            rank[g] = rank[g] + jnp.where(beats, 1.0, 0.0)
    for g in groups:
        cap_s[g] = jnp.where(rank[g] < float(SLC_TOPN), -NEG, NEG)
        m_s[g] = jnp.full((1, HQ), NEG, F32)
        acc_s[g] = jnp.zeros((V_ROWS, HQ), F32)

    _nsa_window(c, qgs, kwn_ref, vwnt_ref, owin_s)

    blocks_per_sub = SEL_SUB // SLC_BLOCK
    subs_per_chunk = SEL_CHUNK // SEL_SUB

    def sel_chunk(kc):
        spans = [slice(kc * SEL_CHUNK + a * SEL_SUB, kc * SEL_CHUNK + (a + 1) * SEL_SUB)
                 for a in range(subs_per_chunk)]
        s = [[_dot_nt(ksl_ref[ks, :], qgs[g]) for g in groups] for ks in spans]
        for a, ks in enumerate(spans):
            kpos = ks.start + lax.broadcasted_iota(jnp.int32, (SEL_SUB, Q_BLOCK), 0)
            causal = kpos <= pos_q
            for g in groups:
                j0 = ks.start // SLC_BLOCK
                cap = jnp.concatenate(
                    [jnp.broadcast_to(cap_s[g, j:j + 1, :], (SLC_BLOCK, Q_BLOCK))
                     for j in range(j0, j0 + blocks_per_sub)], axis=0)
                cap = jnp.where(causal, cap, NEG)
                sm = jnp.minimum(s[a][g], jnp.concatenate([cap] * NSA_HPG, axis=1))
                m_old = m_s[g]
                m_new = jnp.maximum(m_old, jnp.max(sm, axis=0, keepdims=True))
                alpha = jnp.exp2(m_old - m_new)
                e = jnp.exp2((sm - m_new).astype(BF16))
                v_t = vslt_ref[g, kc, :, a * SEL_SUB:(a + 1) * SEL_SUB]
                acc_s[g] = alpha * acc_s[g] + _dot(v_t, e)
                m_s[g] = m_new

    sel_chunk(0)
    for kc in range(1, ksl_ref.shape[0] // SEL_CHUNK):
        pl.when(kc * (SEL_CHUNK // Q_BLOCK) <= c)(functools.partial(sel_chunk, kc))

    for g in groups:
        acc = acc_s[g]
        o_sel = acc[0:HEAD_DIM, :] * (1.0 / acc[HEAD_DIM:HEAD_DIM + 1, :])

        def gate_row(br):
            return jnp.concatenate(
                [gt_ref[(g * NSA_HPG + hh) * 3 + br:(g * NSA_HPG + hh) * 3 + br + 1, :]
                 for hh in range(NSA_HPG)], axis=1)
        o_t = gate_row(0) * ocmp_s[g] + gate_row(1) * o_sel + gate_row(2) * owin_s[g]
        for hh in range(NSA_HPG):
            h = g * NSA_HPG + hh
            out_s[h * HEAD_DIM:(h + 1) * HEAD_DIM, :] = o_t[:, hh * Q_BLOCK:(hh + 1) * Q_BLOCK]

    o_ref[...] = out_s[...].T.astype(BF16)


def _nsa_window(c, qgs, kwn_ref, vwnt_ref, owin_s):
    groups = range(NSA_KV_GROUPS)
    q_i = lax.broadcasted_iota(jnp.int32, (Q_BLOCK, Q_BLOCK), 1)
    k_i = lax.broadcasted_iota(jnp.int32, (Q_BLOCK, Q_BLOCK), 0)
    blks, caps = [], []
    for i in range(N_BAND + 1):
        blk = c - N_BAND + i
        blks.append(jnp.maximum(blk, 0))
        off = (N_BAND - i) * Q_BLOCK
        if off - (Q_BLOCK - 1) >= 0 and off + (Q_BLOCK - 1) < WINDOW:
            caps.append(jnp.where(blk >= 0, -NEG, NEG))
        else:
            diff = off + q_i - k_i
            ok = (diff >= 0) & (diff < WINDOW) & (blk >= 0)
            caps.append(jnp.concatenate([jnp.where(ok, -NEG, NEG)] * NSA_HPG, axis=1))
    k_blocks = [kwn_ref[pl.ds(pl.multiple_of(blks[i] * Q_BLOCK, Q_BLOCK), Q_BLOCK), :]
                for i in range(N_BAND + 1)]
    s_w = [[None] * (N_BAND + 1) for g in groups]
    for i in reversed(range(N_BAND + 1)):
        for g in groups:
            s_w[g][i] = _dot_nt(k_blocks[i], qgs[g])
    m_w = [None for g in groups]
    o_win = [None for g in groups]
    for i in reversed(range(N_BAND + 1)):
        for g in groups:
            sm = jnp.minimum(s_w[g][i], caps[i])
            m_blk = jnp.max(sm, axis=0, keepdims=True)
            m_new = m_blk if m_w[g] is None else jnp.maximum(m_w[g], m_blk)
            pv = _dot(vwnt_ref[g, blks[i]], jnp.exp2((sm - m_new).astype(BF16)))
            o_win[g] = pv if m_w[g] is None else jnp.exp2(m_w[g] - m_new) * o_win[g] + pv
            m_w[g] = m_new
    for g in groups:
        owin_s[g] = o_win[g][0:HEAD_DIM, :] * (1.0 / o_win[g][HEAD_DIM:HEAD_DIM + 1, :])


def _nsa_call(qpad, ksl, kwn, vslt, vwnt, kcmp, vcmpt, gt, ot, batch, seq):
    nqb = seq // Q_BLOCK
    n_g = NSA_KV_GROUPS
    qrow = lambda b, c: (b * nqb + c, 0)
    brow = lambda b, c: (b, 0)
    c2 = lambda b, c: (0, 0)
    return pl.pallas_call(
        _nsa_kernel,
        grid=(batch, nqb),
        in_specs=[
            pl.BlockSpec((Q_BLOCK, NSA_HEADS * LANES), qrow),
            pl.BlockSpec((seq, KV_W), brow),
            pl.BlockSpec((seq, KV_W), brow),
            pl.BlockSpec((n_g, seq // SEL_CHUNK, V_ROWS, SEL_CHUNK), lambda b, c: (b, 0, 0, 0)),
            pl.BlockSpec((n_g, nqb, V_ROWS, Q_BLOCK), lambda b, c: (b, 0, 0, 0)),
            pl.BlockSpec((N_A, KV_W), brow),
            pl.BlockSpec((KV_W, N_A), brow),
            pl.BlockSpec((LANES, Q_BLOCK), lambda b, c: (0, b * nqb + c)),
            pl.BlockSpec((N_SEL, N_A), c2),
        ],
        out_specs=pl.BlockSpec((Q_BLOCK, Q_W), qrow),
        out_shape=jax.ShapeDtypeStruct((batch * seq, Q_W), BF16),
        scratch_shapes=[
            pltpu.VMEM((n_g, 1, HQ), F32),
            pltpu.VMEM((n_g, V_ROWS, HQ), F32),
            pltpu.VMEM((n_g, HEAD_DIM, HQ), F32),
            pltpu.VMEM((n_g, HEAD_DIM, HQ), F32),
            pltpu.VMEM((n_g, N_SEL, Q_BLOCK), F32),
            pltpu.VMEM((Q_W, Q_BLOCK), F32),
        ],
        compiler_params=pltpu.CompilerParams(
            dimension_semantics=("parallel", "arbitrary"), vmem_limit_bytes=VMEM_LIMIT),
        name="nsa",
    )(qpad, ksl, kwn, vslt, vwnt, kcmp, vcmpt, gt, ot)


MEM_TM = 512
MEM_V_ROWS = MEM_HEAD_DIM + 16


def _memattn_kernel(mq_ref, mem_ref, wkv_ref, o_ref, k_s, vt_s, out_s):
    @pl.when(pl.program_id(1) == 0)
    def _():
        kv = _dot(mem_ref[...].astype(BF16), wkv_ref[...].astype(BF16))
        k_s[...] = kv[:, 0:MEM_WIDTH].astype(BF16)
        vt = kv[:, MEM_WIDTH:2 * MEM_WIDTH].T.astype(BF16)
        for h in range(MEM_HEADS):
            vt_s[h, 0:MEM_HEAD_DIM, :] = vt[h * MEM_HEAD_DIM:(h + 1) * MEM_HEAD_DIM, :]
            vt_s[h, MEM_HEAD_DIM:, :] = jnp.ones((MEM_V_ROWS - MEM_HEAD_DIM, vt.shape[1]), BF16)

    heads = range(MEM_HEADS)
    cols = [slice(h * MEM_HEAD_DIM, (h + 1) * MEM_HEAD_DIM) for h in heads]
    s_t = [_dot_nt(k_s[:, cols[h]], mq_ref[:, cols[h]]) for h in heads]
    e = []
    for h in heads:
        m = jnp.max(s_t[h], axis=0, keepdims=True)
        e.append(jnp.exp2((s_t[h] - m) * (MEM_HEAD_DIM ** -0.5 * LOG2E)).astype(BF16))
    o_t = [_dot(vt_s[h], e[h]) for h in heads]
    for h in heads:
        out_s[cols[h], :] = o_t[h][0:MEM_HEAD_DIM, :] * (1.0 / o_t[h][MEM_HEAD_DIM:MEM_HEAD_DIM + 1, :])
    o_ref[...] = out_s[...].T.astype(BF16)


def _memattn_call(mq, mem2, wkv, batch, seq):
    m_len = mem2.shape[0] // batch
    d = mem2.shape[1]
    per = seq // MEM_TM
    return pl.pallas_call(
        _memattn_kernel,
        grid=(batch, per),
        in_specs=[
            pl.BlockSpec((MEM_TM, MEM_WIDTH), lambda b, i: (b * per + i, 0)),
            pl.BlockSpec((m_len, d), lambda b, i: (b, 0)),
            pl.BlockSpec((d, 2 * MEM_WIDTH), lambda b, i: (0, 0)),
        ],
        out_specs=pl.BlockSpec((MEM_TM, MEM_WIDTH), lambda b, i: (b * per + i, 0)),
        out_shape=jax.ShapeDtypeStruct((batch * seq, MEM_WIDTH), BF16),
        scratch_shapes=[
            pltpu.VMEM((m_len, MEM_WIDTH), BF16),
            pltpu.VMEM((MEM_HEADS, MEM_V_ROWS, m_len), BF16),
            pltpu.VMEM((MEM_WIDTH, MEM_TM), F32),
        ],
        compiler_params=pltpu.CompilerParams(
            dimension_semantics=("parallel", "arbitrary"), vmem_limit_bytes=VMEM_LIMIT),
        name="memattn",
    )(mq, mem2, wkv)


MERGE_TM = 512


def _merge_kernel(x_ref, onsa_ref, osgu_ref, omem_ref, wgt_ref, wbn32_ref, wbs32_ref, wbm32_ref, wo32_ref,
                  g_ref, b_ref, out_ref, wbn_ref, wbs_ref, wbm_ref, wo_ref, wg_ref):
    @pl.when(pl.program_id(0) == 0)
    def _():
        for dst, src in ((wbn_ref, wbn32_ref), (wbs_ref, wbs32_ref), (wbm_ref, wbm32_ref),
                         (wo_ref, wo32_ref)):
            dst[...] = src[...].astype(BF16)
        _transpose_weight(wgt_ref, wg_ref)

    x = x_ref[...]
    xb = x.astype(BF16)
    d = x.shape[1]
    y = None
    for br, (o_r, w_r) in enumerate(((onsa_ref, wbn_ref), (osgu_ref, wbs_ref), (omem_ref, wbm_ref))):
        gate = _sigmoid(_dot(xb, wg_ref[:, br * d:(br + 1) * d]))
        term = gate * _dot(o_r[...], w_r[...])
        y = term if y is None else y + term
    z = DN_ALPHA * x + _dot(y.astype(BF16), wo_ref[...])
    out_ref[...] = _layer_norm(z, g_ref[...], b_ref[...])


def _merge_call(x2, onsa, osgu, omem, wg, wbn, wbs, wbm, wo, g1, b1):
    t, d = x2.shape
    row = lambda i: (i, 0)
    c2 = lambda i: (0, 0)
    full = lambda a: pl.BlockSpec(a.shape, c2)
    once = lambda a: pl.BlockSpec(a.shape, c2, pipeline_mode=pl.Buffered(1))
    return pl.pallas_call(
        _merge_kernel,
        grid=(t // MERGE_TM,),
        in_specs=[
            pl.BlockSpec((MERGE_TM, d), row),
            pl.BlockSpec((MERGE_TM, onsa.shape[1]), row),
            pl.BlockSpec((MERGE_TM, osgu.shape[1]), row),
            pl.BlockSpec((MERGE_TM, omem.shape[1]), row),
            pl.BlockSpec((3 * d, d), c2, pipeline_mode=pl.Buffered(1)),
            once(wbn), once(wbs), once(wbm), once(wo), full(g1), full(b1),
        ],
        out_specs=pl.BlockSpec((MERGE_TM, d), row),
        out_shape=jax.ShapeDtypeStruct((t, d), F32),
        scratch_shapes=[pltpu.VMEM(a.shape, BF16) for a in (wbn, wbs, wbm, wo)]
        + [pltpu.VMEM((d, 3 * d), BF16)],
        compiler_params=pltpu.CompilerParams(
            dimension_semantics=("arbitrary",), vmem_limit_bytes=VMEM_LIMIT),
        name="merge",
    )(x2, onsa, osgu, omem, wg, wbn, wbs, wbm, wo, g1, b1)


MOE_TM = 1024
MOE_CH = 144
_YS_ROWS = -(-(MOE_TM + N_GROUPS * MOE_CH) // 256) * 256
_R_SLOT = 8
_DEST_LANE = 3 * _R_SLOT
_HID = EXPERTS_PER_GROUP * EXPERT_FF


def _moe_kernel(x_ref, tri_ref, wr_ref, br_ref, wg_ref, wu_ref, wd_ref, g_ref, b_ref, out_ref,
                xa_s, tok_s, keyr_s, ys_s, cnt_s, base_s):
    grp = pl.program_id(1)
    tm = x_ref.shape[0]
    d = x_ref.shape[1]

    @pl.when(grp == 0)
    def _route():
        xb = x_ref[...].astype(BF16)
        xa_s[:, 0:d] = xb
        logits = _dot(xb, wr_ref[...]) + br_ref[...]
        lt = logits.T
        row = lax.broadcasted_iota(jnp.int32, (_R_SLOT, tm), 0)
        gl = jnp.where(row < N_GROUPS, lt[0:_R_SLOT], NEG)
        gmax = jnp.max(gl, axis=0, keepdims=True)
        gidx = jnp.min(jnp.where(gl == gmax, row, _R_SLOT), axis=0, keepdims=True)
        gprob = 1.0 / jnp.sum(jnp.exp(gl - gmax), axis=0, keepdims=True)
        el = lt[_R_SLOT:2 * _R_SLOT]
        for k in range(1, N_GROUPS):
            el = jnp.where(gidx == k, lt[(k + 1) * _R_SLOT:(k + 2) * _R_SLOT], el)
        ee = jnp.exp(el - jnp.max(el, axis=0, keepdims=True))
        ep = ee / jnp.sum(ee, axis=0, keepdims=True)
        t1 = jnp.max(ep, axis=0, keepdims=True)
        i1 = jnp.min(jnp.where(ep == t1, row, _R_SLOT), axis=0, keepdims=True)
        rest = row != i1
        t2 = jnp.max(jnp.where(rest, ep, -1.0), axis=0, keepdims=True)
        i2 = jnp.min(jnp.where(rest & (ep == t2), row, _R_SLOT), axis=0, keepdims=True)
        den = t1 + t2
        cwf = (jnp.where(row == i1, t1 / den, 0.0) + jnp.where(row == i2, t2 / den, 0.0)) * gprob
        hi = cwf.astype(BF16).astype(F32)
        r1 = cwf - hi
        lo = r1.astype(BF16).astype(F32)
        lo2 = (r1 - lo).astype(BF16).astype(F32)

        onehot = jnp.where(row == gidx, 1.0, 0.0)
        rank = _dot_nt(onehot.astype(BF16), tri_ref[...])
        keyr_s[...] = jnp.where(onehot > 0.5, rank, -1.0)
        base = jnp.int32(0)
        basev = jnp.zeros((1, tm), F32)
        for k in range(N_GROUPS):
            n_k = jnp.sum(onehot[k:k + 1, :]).astype(jnp.int32)
            cnt_s[k] = n_k
            base_s[k] = base
            basev = jnp.where(gidx == k, base.astype(F32), basev)
            base = base + ((n_k + MOE_CH - 1) // MOE_CH) * MOE_CH
        dest = jnp.sum(rank * onehot, axis=0, keepdims=True) + basev
        tok = jnp.concatenate(
            [hi, lo, lo2, jnp.broadcast_to(dest, (_R_SLOT, tm)),
             jnp.zeros((LANES - 4 * _R_SLOT, tm), F32)], axis=0).T
        tok_s[...] = tok
        lane = lax.broadcasted_iota(jnp.int32, tok.shape, 1)
        xa_s[:, d:d + LANES] = jnp.where(lane < _DEST_LANE, tok, 0.0).astype(BF16)
        ys_s[...] = jnp.zeros(ys_s.shape, BF16)

    n_rows = cnt_s[grp]
    row0 = base_s[grp]
    keyr = keyr_s[pl.ds(grp, 1), :]
    half = _HID // 2

    def sweep(k, carry):
        ch = MOE_CH
        r_row = (lax.broadcasted_iota(jnp.int32, (ch, tm), 0) + k * MOE_CH).astype(F32)
        pick = jnp.where(keyr == r_row, 1.0, 0.0).astype(BF16)
        ga = _dot(pick, xa_s[...])
        xg = ga[:, 0:d].astype(BF16)
        cwg = ga[:, d:d + LANES]
        cs = (cwg + pltpu.roll(cwg, LANES - _R_SLOT, 1)
              + pltpu.roll(cwg, LANES - 2 * _R_SLOT, 1))
        hg = [_dot(xg, wg_ref[0, :, h * half:(h + 1) * half]) for h in range(2)]
        hu = [_dot(xg, wu_ref[0, :, h * half:(h + 1) * half]) for h in range(2)]
        y = None
        for h in range(2):
            per_half = EXPERTS_PER_GROUP // 2
            cexp = jnp.concatenate(
                [jnp.broadcast_to(cs[:, e:e + 1], (ch, EXPERT_FF))
                 for e in range(h * per_half, (h + 1) * per_half)], axis=1)
            hid = ((hg[h] * _sigmoid(hg[h])) * hu[h]) * cexp
            term = _dot(hid.astype(BF16), wd_ref[0, h * half:(h + 1) * half, :])
            y = term if y is None else y + term
        ys_s[pl.ds(pl.multiple_of(row0 + k * MOE_CH, 16), ch), :] = y.astype(BF16)
        return carry

    lax.fori_loop(0, (n_rows + MOE_CH - 1) // MOE_CH, sweep, 0)

    @pl.when(grp == N_GROUPS - 1)
    def _fin():
        r_col = lax.broadcasted_iota(jnp.int32, (tm, _YS_ROWS), 1).astype(F32)
        put = jnp.where(tok_s[:, _DEST_LANE:_DEST_LANE + 1] == r_col, 1.0, 0.0).astype(BF16)
        z = DN_ALPHA * x_ref[...] + _dot(put, ys_s[...])
        out_ref[...] = _layer_norm(z, g_ref[...], b_ref[...])


def _moe_call(x1, wr, br, wg, wu, wd, g2, b2):
    t, d = x1.shape
    row = lambda i, g: (i, 0)
    c2 = lambda i, g: (0, 0)
    idx = np.arange(MOE_TM)
    tri = jnp.asarray(idx[None, :] < idx[:, None], dtype=BF16)
    return pl.pallas_call(
        _moe_kernel,
        grid=(t // MOE_TM, N_GROUPS),
        in_specs=[
            pl.BlockSpec((MOE_TM, d), row, pipeline_mode=pl.Buffered(1)),
            pl.BlockSpec((MOE_TM, MOE_TM), c2, pipeline_mode=pl.Buffered(1)),
            pl.BlockSpec(wr.shape, c2),
            pl.BlockSpec(br.shape, c2),
            pl.BlockSpec((1, d, _HID), lambda i, g: (g, 0, 0)),
            pl.BlockSpec((1, d, _HID), lambda i, g: (g, 0, 0)),
            pl.BlockSpec((1, _HID, d), lambda i, g: (g, 0, 0)),
            pl.BlockSpec(g2.shape, c2),
            pl.BlockSpec(b2.shape, c2),
        ],
        out_specs=pl.BlockSpec((MOE_TM, d), row),
        out_shape=jax.ShapeDtypeStruct((t, d), F32),
        scratch_shapes=[
            pltpu.VMEM((MOE_TM, d + LANES), BF16),
            pltpu.VMEM((MOE_TM, LANES), F32),
            pltpu.VMEM((_R_SLOT, MOE_TM), F32),
            pltpu.VMEM((_YS_ROWS, d), BF16),
            pltpu.SMEM((N_GROUPS,), jnp.int32),
            pltpu.SMEM((N_GROUPS,), jnp.int32),
        ],
        compiler_params=pltpu.CompilerParams(
            dimension_semantics=("parallel", "arbitrary"), vmem_limit_bytes=MOE_VMEM_LIMIT),
        name="moe",
    )(x1, tri, wr, br, wg, wu, wd, g2, b2)


def _rope_table(seq):
    half = ROPE_DIM // 2
    inv = ROPE_THETA ** (-jnp.arange(0, ROPE_DIM, 2, dtype=F32) / ROPE_DIM)
    ang = jnp.arange(seq, dtype=F32)[:, None] * inv[None, :]
    cos, sin = jnp.cos(ang), jnp.sin(ang)
    rest = HEAD_DIM - ROPE_DIM
    one = jnp.ones((seq, rest), F32)
    zero = jnp.zeros((seq, rest), F32)
    zh = jnp.zeros((seq, half), F32)
    c_h = jnp.concatenate([cos, cos, one], axis=1)
    dn_h = jnp.concatenate([-sin, zh, zero], axis=1)
    up_h = jnp.concatenate([zh, sin, zero], axis=1)
    rep = LANES // HEAD_DIM
    return jnp.concatenate([jnp.tile(c_h, (1, rep)), jnp.tile(dn_h, (1, rep)), jnp.tile(up_h, (1, rep))],
                           axis=1)


def _expand_cmp_weights(pe, w1, w2):
    g_n = NSA_KV_GROUPS
    w1r = w1.astype(BF16).reshape(2, CMP_STRIDE, HEAD_DIM, CMP_HIDDEN)
    w2b = w2.astype(BF16)
    w1e = jnp.concatenate(
        [jnp.stack([w1r if g == e else jnp.zeros_like(w1r) for g in range(g_n)], axis=2)
         .reshape(2, A_W, CMP_HIDDEN) for e in range(g_n)], axis=2)
    w2e = jnp.concatenate(
        [jnp.concatenate([w2b if g == e else jnp.zeros_like(w2b) for e in range(g_n)], axis=1)
         for g in range(g_n)], axis=0)
    per = pe.reshape(2, CMP_STRIDE, 1, HEAD_DIM)
    pee = jnp.broadcast_to(per, (2, CMP_STRIDE, g_n, HEAD_DIM)).reshape(2, A_W)
    pee = jnp.concatenate([pee, jnp.zeros((6, A_W), pe.dtype)], axis=0)
    return pee.astype(F32), w1e.astype(BF16), w2e.astype(BF16)


def kernel(x, mem, w_in, cmp_pe_k, cmp_w1_k, cmp_w2_k, cmp_pe_v, cmp_w1_v, cmp_w2_v, sgu_ln_g, sgu_ln_b,
           sgu_w_s, sgu_b_s, w_mem_kv, w_br_nsa, w_br_sgu, w_br_mem, w_o, ln1_g, ln1_b, w_router_group,
           b_router_group, w_router_expert, b_router_expert, w_exp_gate, w_exp_up, w_exp_down, ln2_g, ln2_b):
    batch, seq, d = x.shape
    t = batch * seq
    assert w_in.shape[0] == DEPTH == 1
    assert seq % PROJ_TM == 0 and seq // CMP_STRIDE == N_A and seq // SLC_BLOCK == N_SEL

    offs = [int(v) for v in np.cumsum(
        [0, Q_W, KV_W, KV_W, KV_W, KV_W, KV_W, KV_W, GATE_W, 2 * SGU_WIDTH, MEM_WIDTH, 3 * d])]
    assert offs[-1] == w_in.shape[2]
    w_t = jnp.transpose(w_in[0])
    seg = lambda i: w_t[offs[i]:offs[i + 1]]
    gate_rows = jnp.pad(seg(7), ((0, LANES - GATE_W), (0, 0)))
    wp_t = jnp.concatenate([seg(0), seg(1), seg(3), seg(5), seg(2), seg(4), seg(6), gate_rows, seg(8), seg(9)],
                           axis=0).astype(BF16)
    w_mt = seg(10).astype(BF16)
    assert wp_t.shape[0] == _PROJ_COLS
    rope = _rope_table(seq)
    tril = jnp.tril(jnp.ones((SGU_CHUNK, SGU_CHUNK), dtype=bool))
    ws = jnp.where(tril[None], sgu_w_s[0], 0.0).astype(BF16)
    bs = jnp.repeat(sgu_b_s[0].T, SGU_WIDTH // SGU_GROUPS, axis=1)

    x2 = x.reshape(t, d)
    (qpad, kc, ksl, kwn, vc, vslt, vwnt, gt, osgu, mq) = _proj_call(
        x2, wp_t, rope, sgu_ln_g[0][None], sgu_ln_b[0][None], ws, bs, seq)

    pek, w1k, w2k = _expand_cmp_weights(cmp_pe_k[0], cmp_w1_k[0], cmp_w2_k[0])
    pev, w1v, w2v = _expand_cmp_weights(cmp_pe_v[0], cmp_w1_v[0], cmp_w2_v[0])
    kcmp, vcmp = _compress_call(kc, vc, pek, pev, w1k, w2k, w1v, w2v, batch)

    nqb = seq // Q_BLOCK
    n_g = NSA_KV_GROUPS
    vcmpt = vcmp.reshape(batch, N_A, KV_W).transpose(0, 2, 1).reshape(batch * KV_W, N_A)
    ci = np.arange(N_A)
    sj = np.arange(N_SEL)
    overlap = ((ci[None, :] * CMP_STRIDE + CMP_LEN - 1 >= sj[:, None] * SLC_BLOCK)
               & (ci[None, :] * CMP_STRIDE <= sj[:, None] * SLC_BLOCK + SLC_BLOCK - 1)
               & (ci[None, :] < (seq - CMP_LEN) // CMP_STRIDE + 1))
    ot = jnp.asarray(overlap, dtype=BF16)
    onsa = _nsa_call(qpad, ksl, kwn, vslt, vwnt, kcmp, vcmpt, gt, ot, batch, seq)

    omem = _memattn_call(mq, mem.reshape(batch * mem.shape[1], d), w_mem_kv[0], batch, seq)

    x1 = _merge_call(x2, onsa, osgu, omem, w_mt, w_br_nsa[0], w_br_sgu[0], w_br_mem[0], w_o[0],
                     ln1_g[0][None], ln1_b[0][None])

    assert EXPERTS_PER_GROUP == _R_SLOT and N_GROUPS <= _R_SLOT
    n_r = _R_SLOT + N_GROUPS * EXPERTS_PER_GROUP
    wr = jnp.concatenate([jnp.pad(w_router_group[0], ((0, 0), (0, _R_SLOT - N_GROUPS))),
                          w_router_expert[0]], axis=1)
    wr = jnp.pad(wr, ((0, 0), (0, LANES - n_r))).astype(BF16)
    br = jnp.concatenate([jnp.pad(b_router_group[0], (0, _R_SLOT - N_GROUPS)), b_router_expert[0]])
    br = jnp.pad(br, (0, LANES - n_r))[None]
    wg = w_exp_gate[0].transpose(0, 2, 1, 3).reshape(N_GROUPS, d, _HID).astype(BF16)
    wu = w_exp_up[0].transpose(0, 2, 1, 3).reshape(N_GROUPS, d, _HID).astype(BF16)
    wd = w_exp_down[0].reshape(N_GROUPS, _HID, d).astype(BF16)
    out = _moe_call(x1, wr, br, wg, wu, wd, ln2_g[0][None], ln2_b[0][None])
    return out.reshape(batch, seq, d)
```

```python
import functools

import numpy as np
import jax
import jax.numpy as jnp
from jax import lax
from jax.experimental import pallas as pl
from jax.experimental.pallas import tpu as pltpu

NSA_HEADS = 8
NSA_KV_GROUPS = 2
NSA_HPG = NSA_HEADS // NSA_KV_GROUPS
HEAD_DIM = 64
CMP_LEN = 32
CMP_STRIDE = 16
CMP_HIDDEN = 256
SLC_BLOCK = 64
SLC_TOPN = 8
WINDOW = 512
Q_BLOCK = 128
N_BAND = WINDOW // Q_BLOCK
ROPE_THETA = 500000.0
ROPE_DIM = HEAD_DIM // 4
SGU_CHUNK = 128
SGU_GROUPS = 8
SGU_WIDTH = 512
MEM_HEADS = 4
MEM_HEAD_DIM = 128
MEM_WIDTH = MEM_HEADS * MEM_HEAD_DIM
N_GROUPS = 4
EXPERTS_PER_GROUP = 8
EXPERT_FF = 256
DEPTH = 1
DN_ALPHA = (2.0 * DEPTH) ** 0.25
LN_EPS = 1e-5
NEG = -1e30
LOG2E = 1.4426950408889634

LANES = 128
Q_W = NSA_HEADS * HEAD_DIM
KV_W = NSA_KV_GROUPS * HEAD_DIM
GATE_W = NSA_HEADS * 3
VMEM_LIMIT = 56 * 1024 * 1024
MOE_VMEM_LIMIT = 60 * 1024 * 1024

BF16 = jnp.bfloat16
F32 = jnp.float32


def _dot(a, b):
    return jnp.dot(a, b, preferred_element_type=F32)


def _dot_nt(a, b):
    return lax.dot_general(a, b, (((1,), (1,)), ((), ())), preferred_element_type=F32)


def _sigmoid(x):
    return 1.0 / (1.0 + jnp.exp(-x))


def _gelu(x):
    return 0.5 * x * (1.0 + lax.erf(x * (2.0 ** -0.5)))


def _layer_norm(x, g, b):
    mu = jnp.mean(x, axis=-1, keepdims=True)
    xc = x - mu
    var = jnp.mean(xc * xc, axis=-1, keepdims=True)
    return xc * lax.rsqrt(var + LN_EPS) * g + b


PROJ_TM = 512
_ROPE_COLS = Q_W + 3 * KV_W
_V_OFF = _ROPE_COLS
_G_OFF = _V_OFF + 3 * KV_W
_SGU_OFF = _G_OFF + LANES
_MQ_OFF = _SGU_OFF + 2 * SGU_WIDTH
_PROJ_COLS = _MQ_OFF + MEM_WIDTH


def _store_strided_rows(val, out_ref, tmp_ref):
    tmp_ref[...] = val
    n = val.shape[0] // CMP_STRIDE
    for l in range(CMP_STRIDE):
        out_ref[:, l * LANES:(l + 1) * LANES] = tmp_ref[pl.ds(l, n, stride=CMP_STRIDE), :].astype(BF16)


def _transpose_weight(wt_ref, w_s):
    for j in range(wt_ref.shape[0] // LANES):
        rows = slice(j * LANES, (j + 1) * LANES)
        w_s[:, rows] = wt_ref[rows, :].astype(F32).T.astype(BF16)


def _proj_kernel(x_ref, wt_ref, rope_ref, lng_ref, lnb_ref, ws_ref, bs_ref,
                 q_ref, kc_ref, ksl_ref, kwn_ref, vc_ref, vsl_ref, vwn_ref, gate_ref, osgu_ref, mq_ref,
                 a_s, w_ref):
    pl.when(pl.program_id(0) == 0)(functools.partial(_transpose_weight, wt_ref, w_ref))
    xb = x_ref[...].astype(BF16)
    lane = lax.broadcasted_iota(jnp.int32, (PROJ_TM, LANES), 1)
    low = lane < HEAD_DIM
    cos = rope_ref[:, 0:LANES]
    s_dn = rope_ref[:, LANES:2 * LANES]
    s_up = rope_ref[:, 2 * LANES:3 * LANES]

    hz = _dot(xb, w_ref[:, _SGU_OFF:_MQ_OFF])
    h = _dot(xb, w_ref[:, 0:_ROPE_COLS])
    k_refs = (kc_ref, ksl_ref, kwn_ref)
    for j in range(_ROPE_COLS // LANES):
        blk = h[:, j * LANES:(j + 1) * LANES]
        r = (blk * cos + pltpu.roll(blk, LANES - ROPE_DIM // 2, 1) * s_dn
             + pltpu.roll(blk, ROPE_DIM // 2, 1) * s_up)
        if j < Q_W // LANES:
            r = r * (HEAD_DIM ** -0.5 * LOG2E)
            sw = pltpu.roll(r, HEAD_DIM, 1)
            g = (2 * j) // NSA_HPG
            if g == 0:
                h0 = jnp.where(low, r, 0.0)
                h1 = jnp.where(low, sw, 0.0)
            else:
                h0 = jnp.where(low, 0.0, sw)
                h1 = jnp.where(low, 0.0, r)
            q_ref[:, (2 * j) * LANES:(2 * j + 1) * LANES] = h0.astype(BF16)
            q_ref[:, (2 * j + 1) * LANES:(2 * j + 2) * LANES] = h1.astype(BF16)
        else:
            if j == Q_W // LANES:
                _store_strided_rows(r, kc_ref, a_s)
            else:
                k_refs[j - Q_W // LANES][...] = r.astype(BF16)

    _store_strided_rows(_dot(xb, w_ref[:, _V_OFF:_V_OFF + LANES]), vc_ref, a_s)
    ones = jnp.ones((V_ROWS - HEAD_DIM, PROJ_TM), BF16)
    for v_ref, row0, blk in ((vsl_ref, _V_OFF + LANES, SEL_CHUNK), (vwn_ref, _V_OFF + 2 * LANES, Q_BLOCK)):
        vt = _dot_nt(wt_ref[row0:row0 + LANES, :], xb).astype(BF16)
        for g in range(NSA_KV_GROUPS):
            for b in range(PROJ_TM // blk):
                v_ref[g, b, 0:HEAD_DIM, :] = vt[g * HEAD_DIM:(g + 1) * HEAD_DIM, b * blk:(b + 1) * blk]
                v_ref[g, b, HEAD_DIM:V_ROWS, :] = ones[:, 0:blk]

    gate_ref[...] = _sigmoid(_dot_nt(wt_ref[_G_OFF:_SGU_OFF, :], xb))

    mq_ref[...] = _dot(xb, w_ref[:, _MQ_OFF:_PROJ_COLS]).astype(BF16)

    z = _gelu(hz)
    u = z[:, 0:SGU_WIDTH]
    v = _layer_norm(z[:, SGU_WIDTH:2 * SGU_WIDTH], lng_ref[...], lnb_ref[...]).astype(BF16)
    lane_c = lax.broadcasted_iota(jnp.int32, (SGU_CHUNK, LANES), 1)
    low_c = lane_c < (SGU_WIDTH // SGU_GROUPS)
    n_ch = PROJ_TM // SGU_CHUNK
    for gp in range(SGU_WIDTH // LANES):
        cols = slice(gp * LANES, (gp + 1) * LANES)
        vcat = jnp.concatenate([v[ci * SGU_CHUNK:(ci + 1) * SGU_CHUNK, cols] for ci in range(n_ch)], axis=1)
        sv0 = _dot(ws_ref[2 * gp], vcat)
        sv1 = _dot(ws_ref[2 * gp + 1], vcat)
        for ci in range(n_ch):
            rows = slice(ci * SGU_CHUNK, (ci + 1) * SGU_CHUNK)
            lanes = slice(ci * LANES, (ci + 1) * LANES)
            sv = jnp.where(low_c, sv0[:, lanes], sv1[:, lanes]) + bs_ref[:, cols]
            osgu_ref[rows, cols] = (u[rows, cols] * sv).astype(BF16)


def _proj_call(x2, wp, rope, lng, lnb, ws, bs, seq):
    t = x2.shape[0]
    d = x2.shape[1]
    nt = t // PROJ_TM
    per_seq = seq // PROJ_TM
    row = lambda i: (i, 0)
    const2 = lambda i: (0, 0)
    assert PROJ_TM % SEL_CHUNK == 0 and PROJ_TM % Q_BLOCK == 0
    n_g = NSA_KV_GROUPS
    batch = t // seq
    vblock = lambda i: (i // per_seq, i % per_seq, 0, 0)

    def v_shape(blk):
        return jax.ShapeDtypeStruct((batch * n_g, seq // blk, V_ROWS, blk), BF16)

    def v_spec(blk):
        return pl.BlockSpec((n_g, PROJ_TM // blk, V_ROWS, blk), vblock)

    a_shape = jax.ShapeDtypeStruct((t // CMP_STRIDE, CMP_STRIDE * KV_W), BF16)
    a_spec = pl.BlockSpec((PROJ_TM // CMP_STRIDE, CMP_STRIDE * KV_W), row)
    k_shape = jax.ShapeDtypeStruct((t, KV_W), BF16)
    k_spec = pl.BlockSpec((PROJ_TM, KV_W), row)
    out_shapes = (
        jax.ShapeDtypeStruct((t, NSA_HEADS * LANES), BF16),
        a_shape, k_shape, k_shape, a_shape,
        v_shape(SEL_CHUNK), v_shape(Q_BLOCK),
        jax.ShapeDtypeStruct((LANES, t), F32),
        jax.ShapeDtypeStruct((t, SGU_WIDTH), BF16),
        jax.ShapeDtypeStruct((t, MEM_WIDTH), BF16),
    )
    out_specs = (
        pl.BlockSpec((PROJ_TM, NSA_HEADS * LANES), row),
        a_spec, k_spec, k_spec, a_spec,
        v_spec(SEL_CHUNK), v_spec(Q_BLOCK),
        pl.BlockSpec((LANES, PROJ_TM), lambda i: (0, i)),
        pl.BlockSpec((PROJ_TM, SGU_WIDTH), row),
        pl.BlockSpec((PROJ_TM, MEM_WIDTH), row),
    )
    return pl.pallas_call(
        _proj_kernel,
        grid=(nt,),
        in_specs=[
            pl.BlockSpec((PROJ_TM, d), row),
            pl.BlockSpec((_PROJ_COLS, d), const2),
            pl.BlockSpec((PROJ_TM, 3 * LANES), lambda i: (i % per_seq, 0)),
            pl.BlockSpec((1, SGU_WIDTH), const2),
            pl.BlockSpec((1, SGU_WIDTH), const2),
            pl.BlockSpec((SGU_GROUPS, SGU_CHUNK, SGU_CHUNK), lambda i: (0, 0, 0)),
            pl.BlockSpec((SGU_CHUNK, SGU_WIDTH), const2),
        ],
        out_specs=out_specs,
        out_shape=out_shapes,
        scratch_shapes=[pltpu.VMEM((PROJ_TM, KV_W), F32), pltpu.VMEM((d, _PROJ_COLS), BF16)],
        compiler_params=pltpu.CompilerParams(
            dimension_semantics=("arbitrary",), vmem_limit_bytes=VMEM_LIMIT),
        name="proj",
    )(x2, wp, rope, lng, lnb, ws, bs)


N_A = 128
A_W = CMP_STRIDE * KV_W


def _compress_kernel(ka_ref, va_ref, pek_ref, pev_ref, w1k_ref, w2k_ref, w1v_ref, w2v_ref,
                     kcmp_ref, vcmp_ref):
    def one(a_ref, pe_ref, w1_ref, w2_ref, out_ref):
        a = a_ref[...].astype(F32)
        top = (a + pe_ref[0:1, :]).astype(BF16)
        bot = (a + pe_ref[1:2, :]).astype(BF16)
        h1 = _dot(top, w1_ref[0])
        h2 = _dot(bot, w1_ref[1])
        pre = h1 + pltpu.roll(h2, N_A - 1, 0)
        act = _gelu(pre).astype(BF16)
        out_ref[...] = _dot(act, w2_ref[...]).astype(BF16)

    one(ka_ref, pek_ref, w1k_ref, w2k_ref, kcmp_ref)
    one(va_ref, pev_ref, w1v_ref, w2v_ref, vcmp_ref)


def _compress_call(ka, va, pek, pev, w1k, w2k, w1v, w2v, batch):
    row = lambda b: (b, 0)
    c2 = lambda b: (0, 0)
    c3 = lambda b: (0, 0, 0)
    hid2 = NSA_KV_GROUPS * CMP_HIDDEN
    return pl.pallas_call(
        _compress_kernel,
        grid=(batch,),
        in_specs=[
            pl.BlockSpec((N_A, A_W), row),
            pl.BlockSpec((N_A, A_W), row),
            pl.BlockSpec((8, A_W), c2),
            pl.BlockSpec((8, A_W), c2),
            pl.BlockSpec((2, A_W, hid2), c3),
            pl.BlockSpec((hid2, KV_W), c2),
            pl.BlockSpec((2, A_W, hid2), c3),
            pl.BlockSpec((hid2, KV_W), c2),
        ],
        out_specs=(pl.BlockSpec((N_A, KV_W), row), pl.BlockSpec((N_A, KV_W), row)),
        out_shape=(jax.ShapeDtypeStruct((batch * N_A, KV_W), BF16),
                   jax.ShapeDtypeStruct((batch * N_A, KV_W), BF16)),
        compiler_params=pltpu.CompilerParams(
            dimension_semantics=("parallel",), vmem_limit_bytes=VMEM_LIMIT),
        name="compress",
    )(ka, va, pek, pev, w1k, w2k, w1v, w2v)


N_SEL = 32
SEL_CHUNK = 512
SEL_SUB = 128
HQ = NSA_HPG * Q_BLOCK
V_ROWS = HEAD_DIM + 16


def _nsa_kernel(q_ref, ksl_ref, kwn_ref, vslt_ref, vwnt_ref, kcmp_ref, vcmpt_ref, gt_ref,
                ot_ref, o_ref, m_s, acc_s, ocmp_s, owin_s, cap_s, out_s):
    c = pl.program_id(1)
    lane_hq = lax.broadcasted_iota(jnp.int32, (1, HQ), 1)
    pos_hq = c * Q_BLOCK + (lane_hq & (Q_BLOCK - 1))
    pos_q = c * Q_BLOCK + lax.broadcasted_iota(jnp.int32, (1, Q_BLOCK), 1)
    groups = range(NSA_KV_GROUPS)
    qgs = [jnp.concatenate(
        [q_ref[:, (g * NSA_HPG + hh) * LANES:(g * NSA_HPG + hh + 1) * LANES]
         for hh in range(NSA_HPG)], axis=0) for g in groups]

    kcmp = kcmp_ref[...]
    s_c = [_dot_nt(kcmp, qgs[g]) for g in groups]
    n_idx = lax.broadcasted_iota(jnp.int32, (N_A, HQ), 0)
    valid_c = (n_idx * CMP_STRIDE + (CMP_LEN - 1)) <= pos_hq
    p_c = []
    for g in groups:
        sm_c = jnp.where(valid_c, s_c[g], NEG)
        m_c = jnp.max(sm_c, axis=0, keepdims=True)
        e_c = jnp.where(valid_c, jnp.exp2(sm_c - m_c), 0.0)
        d_c = jnp.sum(e_c, axis=0, keepdims=True)
        p_c.append(e_c / jnp.where(d_c > 0, d_c, 1.0))
    for g in groups:
        ocmp_s[g] = _dot(vcmpt_ref[g * HEAD_DIM:(g + 1) * HEAD_DIM, :], p_c[g].astype(BF16))

    ot = ot_ref[...]
    imp = []
    for g in groups:
        ps = (p_c[g][:, 0:Q_BLOCK] + p_c[g][:, Q_BLOCK:2 * Q_BLOCK]
              + p_c[g][:, 2 * Q_BLOCK:3 * Q_BLOCK] + p_c[g][:, 3 * Q_BLOCK:4 * Q_BLOCK])
        p_hi = ps.astype(BF16)
        r1 = ps - p_hi.astype(F32)
        p_lo = r1.astype(BF16)
        p_lo2 = (r1 - p_lo.astype(F32)).astype(BF16)
        imp.append(_dot(ot, p_hi) + _dot(ot, p_lo) + _dot(ot, p_lo2))

    j_idx = lax.broadcasted_iota(jnp.int32, (N_SEL, Q_BLOCK), 0)
    cur = pos_q // SLC_BLOCK
    future = j_idx > cur
    forced = (j_idx == 0) | (j_idx == cur) | (j_idx == cur - 1)
    imp = [jnp.where(future, NEG, jnp.where(forced, -NEG, imp[g])) for g in groups]
    rank = [jnp.zeros((N_SEL, Q_BLOCK), F32) for g in groups]
    for i in range(N_SEL):
        for g in groups:
            row = imp[g][i:i + 1, :]
            beats = (row > imp[g]) | ((row == imp[g]) & (j_idx > i))
            rank[g] = rank[g] + jnp.where(beats, 1.0, 0.0)
    for g in groups:
        cap_s[g] = jnp.where(rank[g] < float(SLC_TOPN), -NEG, NEG)
        m_s[g] = jnp.full((1, HQ), NEG, F32)
        acc_s[g] = jnp.zeros((V_ROWS, HQ), F32)

    _nsa_window(c, qgs, kwn_ref, vwnt_ref, owin_s)

    blocks_per_sub = SEL_SUB // SLC_BLOCK
    subs_per_chunk = SEL_CHUNK // SEL_SUB

    def sel_chunk(kc):
        spans = [slice(kc * SEL_CHUNK + a * SEL_SUB, kc * SEL_CHUNK + (a + 1) * SEL_SUB)
                 for a in range(subs_per_chunk)]
        s = [[_dot_nt(ksl_ref[ks, :], qgs[g]) for g in groups] for ks in spans]
        for a, ks in enumerate(spans):
            kpos = ks.start + lax.broadcasted_iota(jnp.int32, (SEL_SUB, Q_BLOCK), 0)
            causal = kpos <= pos_q
            for g in groups:
                j0 = ks.start // SLC_BLOCK
                cap = jnp.concatenate(
                    [jnp.broadcast_to(cap_s[g, j:j + 1, :], (SLC_BLOCK, Q_BLOCK))
                     for j in range(j0, j0 + blocks_per_sub)], axis=0)
                cap = jnp.where(causal, cap, NEG)
                sm = jnp.minimum(s[a][g], jnp.concatenate([cap] * NSA_HPG, axis=1))
                m_old = m_s[g]
                m_new = jnp.maximum(m_old, jnp.max(sm, axis=0, keepdims=True))
                alpha = jnp.exp2(m_old - m_new)
                e = jnp.exp2((sm - m_new).astype(BF16))
                v_t = vslt_ref[g, kc, :, a * SEL_SUB:(a + 1) * SEL_SUB]
                acc_s[g] = alpha * acc_s[g] + _dot(v_t, e)
                m_s[g] = m_new

    sel_chunk(0)
    for kc in range(1, ksl_ref.shape[0] // SEL_CHUNK):
        pl.when(kc * (SEL_CHUNK // Q_BLOCK) <= c)(functools.partial(sel_chunk, kc))

    for g in groups:
        acc = acc_s[g]
        o_sel = acc[0:HEAD_DIM, :] * (1.0 / acc[HEAD_DIM:HEAD_DIM + 1, :])

        def gate_row(br):
            return jnp.concatenate(
                [gt_ref[(g * NSA_HPG + hh) * 3 + br:(g * NSA_HPG + hh) * 3 + br + 1, :]
                 for hh in range(NSA_HPG)], axis=1)
        o_t = gate_row(0) * ocmp_s[g] + gate_row(1) * o_sel + gate_row(2) * owin_s[g]
        for hh in range(NSA_HPG):
            h = g * NSA_HPG + hh
            out_s[h * HEAD_DIM:(h + 1) * HEAD_DIM, :] = o_t[:, hh * Q_BLOCK:(hh + 1) * Q_BLOCK]

    o_ref[...] = out_s[...].T.astype(BF16)


def _nsa_window(c, qgs, kwn_ref, vwnt_ref, owin_s):
    groups = range(NSA_KV_GROUPS)
    q_i = lax.broadcasted_iota(jnp.int32, (Q_BLOCK, Q_BLOCK), 1)
    k_i = lax.broadcasted_iota(jnp.int32, (Q_BLOCK, Q_BLOCK), 0)
    blks, caps = [], []
    for i in range(N_BAND + 1):
        blk = c - N_BAND + i
        blks.append(jnp.maximum(blk, 0))
        off = (N_BAND - i) * Q_BLOCK
        if off - (Q_BLOCK - 1) >= 0 and off + (Q_BLOCK - 1) < WINDOW:
            caps.append(jnp.where(blk >= 0, -NEG, NEG))
        else:
            diff = off + q_i - k_i
            ok = (diff >= 0) & (diff < WINDOW) & (blk >= 0)
            caps.append(jnp.concatenate([jnp.where(ok, -NEG, NEG)] * NSA_HPG, axis=1))
    k_blocks = [kwn_ref[pl.ds(pl.multiple_of(blks[i] * Q_BLOCK, Q_BLOCK), Q_BLOCK), :]
                for i in range(N_BAND + 1)]
    s_w = [[None] * (N_BAND + 1) for g in groups]
    for i in reversed(range(N_BAND + 1)):
        for g in groups:
            s_w[g][i] = _dot_nt(k_blocks[i], qgs[g])
    m_w = [None for g in groups]
    o_win = [None for g in groups]
    for i in reversed(range(N_BAND + 1)):
        for g in groups:
            sm = jnp.minimum(s_w[g][i], caps[i])
            m_blk = jnp.max(sm, axis=0, keepdims=True)
            m_new = m_blk if m_w[g] is None else jnp.maximum(m_w[g], m_blk)
            pv = _dot(vwnt_ref[g, blks[i]], jnp.exp2((sm - m_new).astype(BF16)))
            o_win[g] = pv if m_w[g] is None else jnp.exp2(m_w[g] - m_new) * o_win[g] + pv
            m_w[g] = m_new
    for g in groups:
        owin_s[g] = o_win[g][0:HEAD_DIM, :] * (1.0 / o_win[g][HEAD_DIM:HEAD_DIM + 1, :])


def _nsa_call(qpad, ksl, kwn, vslt, vwnt, kcmp, vcmpt, gt, ot, batch, seq):
    nqb = seq // Q_BLOCK
    n_g = NSA_KV_GROUPS
    qrow = lambda b, c: (b * nqb + c, 0)
    brow = lambda b, c: (b, 0)
    c2 = lambda b, c: (0, 0)
    return pl.pallas_call(
        _nsa_kernel,
        grid=(batch, nqb),
        in_specs=[
            pl.BlockSpec((Q_BLOCK, NSA_HEADS * LANES), qrow),
            pl.BlockSpec((seq, KV_W), brow),
            pl.BlockSpec((seq, KV_W), brow),
            pl.BlockSpec((n_g, seq // SEL_CHUNK, V_ROWS, SEL_CHUNK), lambda b, c: (b, 0, 0, 0)),
            pl.BlockSpec((n_g, nqb, V_ROWS, Q_BLOCK), lambda b, c: (b, 0, 0, 0)),
            pl.BlockSpec((N_A, KV_W), brow),
            pl.BlockSpec((KV_W, N_A), brow),
            pl.BlockSpec((LANES, Q_BLOCK), lambda b, c: (0, b * nqb + c)),
            pl.BlockSpec((N_SEL, N_A), c2),
        ],
        out_specs=pl.BlockSpec((Q_BLOCK, Q_W), qrow),
        out_shape=jax.ShapeDtypeStruct((batch * seq, Q_W), BF16),
        scratch_shapes=[
            pltpu.VMEM((n_g, 1, HQ), F32),
            pltpu.VMEM((n_g, V_ROWS, HQ), F32),
            pltpu.VMEM((n_g, HEAD_DIM, HQ), F32),
            pltpu.VMEM((n_g, HEAD_DIM, HQ), F32),
            pltpu.VMEM((n_g, N_SEL, Q_BLOCK), F32),
            pltpu.VMEM((Q_W, Q_BLOCK), F32),
        ],
        compiler_params=pltpu.CompilerParams(
            dimension_semantics=("parallel", "arbitrary"), vmem_limit_bytes=VMEM_LIMIT),
        name="nsa",
    )(qpad, ksl, kwn, vslt, vwnt, kcmp, vcmpt, gt, ot)


MEM_TM = 512
MEM_V_ROWS = MEM_HEAD_DIM + 16


def _memattn_kernel(mq_ref, mem_ref, wkv_ref, o_ref, k_s, vt_s, out_s):
    @pl.when(pl.program_id(1) == 0)
    def _():
        kv = _dot(mem_ref[...].astype(BF16), wkv_ref[...].astype(BF16))
        k_s[...] = kv[:, 0:MEM_WIDTH].astype(BF16)
        vt = kv[:, MEM_WIDTH:2 * MEM_WIDTH].T.astype(BF16)
        for h in range(MEM_HEADS):
            vt_s[h, 0:MEM_HEAD_DIM, :] = vt[h * MEM_HEAD_DIM:(h + 1) * MEM_HEAD_DIM, :]
            vt_s[h, MEM_HEAD_DIM:, :] = jnp.ones((MEM_V_ROWS - MEM_HEAD_DIM, vt.shape[1]), BF16)

    heads = range(MEM_HEADS)
    cols = [slice(h * MEM_HEAD_DIM, (h + 1) * MEM_HEAD_DIM) for h in heads]
    s_t = [_dot_nt(k_s[:, cols[h]], mq_ref[:, cols[h]]) for h in heads]
    e = []
    for h in heads:
        m = jnp.max(s_t[h], axis=0, keepdims=True)
        e.append(jnp.exp2((s_t[h] - m) * (MEM_HEAD_DIM ** -0.5 * LOG2E)).astype(BF16))
    o_t = [_dot(vt_s[h], e[h]) for h in heads]
    for h in heads:
        out_s[cols[h], :] = o_t[h][0:MEM_HEAD_DIM, :] * (1.0 / o_t[h][MEM_HEAD_DIM:MEM_HEAD_DIM + 1, :])
    o_ref[...] = out_s[...].T.astype(BF16)


def _memattn_call(mq, mem2, wkv, batch, seq):
    m_len = mem2.shape[0] // batch
    d = mem2.shape[1]
    per = seq // MEM_TM
    return pl.pallas_call(
        _memattn_kernel,
        grid=(batch, per),
        in_specs=[
            pl.BlockSpec((MEM_TM, MEM_WIDTH), lambda b, i: (b * per + i, 0)),
            pl.BlockSpec((m_len, d), lambda b, i: (b, 0)),
            pl.BlockSpec((d, 2 * MEM_WIDTH), lambda b, i: (0, 0)),
        ],
        out_specs=pl.BlockSpec((MEM_TM, MEM_WIDTH), lambda b, i: (b * per + i, 0)),
        out_shape=jax.ShapeDtypeStruct((batch * seq, MEM_WIDTH), BF16),
        scratch_shapes=[
            pltpu.VMEM((m_len, MEM_WIDTH), BF16),
            pltpu.VMEM((MEM_HEADS, MEM_V_ROWS, m_len), BF16),
            pltpu.VMEM((MEM_WIDTH, MEM_TM), F32),
        ],
        compiler_params=pltpu.CompilerParams(
            dimension_semantics=("parallel", "arbitrary"), vmem_limit_bytes=VMEM_LIMIT),
        name="memattn",
    )(mq, mem2, wkv)


MERGE_TM = 512


def _merge_kernel(x_ref, onsa_ref, osgu_ref, omem_ref, wgt_ref, wbn32_ref, wbs32_ref, wbm32_ref, wo32_ref,
                  g_ref, b_ref, out_ref, wbn_ref, wbs_ref, wbm_ref, wo_ref, wg_ref):
    @pl.when(pl.program_id(0) == 0)
    def _():
        for dst, src in ((wbn_ref, wbn32_ref), (wbs_ref, wbs32_ref), (wbm_ref, wbm32_ref),
                         (wo_ref, wo32_ref)):
            dst[...] = src[...].astype(BF16)
        _transpose_weight(wgt_ref, wg_ref)

    d = x_ref.shape[1]
    branches = ((onsa_ref, wbn_ref), (osgu_ref, wbs_ref), (omem_ref, wbm_ref))
    halves = [slice(h * (MERGE_TM // 2), (h + 1) * (MERGE_TM // 2)) for h in range(2)]
    xs = [x_ref[rows, :] for rows in halves]
    xbs = [x.astype(BF16) for x in xs]
    logits = [[_dot(xb, wg_ref[:, br * d:(br + 1) * d]) for br in range(len(branches))] for xb in xbs]
    projs = [[_dot(o_r[rows, :], w_r[...]) for o_r, w_r in branches] for rows in halves]
    ys = []
    for h in range(2):
        y = None
        for br in range(len(branches)):
            term = _sigmoid(logits[h][br]) * projs[h][br]
            y = term if y is None else y + term
        ys.append(y.astype(BF16))
    outs = [_dot(y, wo_ref[...]) for y in ys]
    for h, rows in enumerate(halves):
        out_ref[rows, :] = _layer_norm(DN_ALPHA * xs[h] + outs[h], g_ref[...], b_ref[...])


def _merge_call(x2, onsa, osgu, omem, wg, wbn, wbs, wbm, wo, g1, b1):
    t, d = x2.shape
    row = lambda i: (i, 0)
    c2 = lambda i: (0, 0)
    full = lambda a: pl.BlockSpec(a.shape, c2)
    once = lambda a: pl.BlockSpec(a.shape, c2, pipeline_mode=pl.Buffered(1))
    return pl.pallas_call(
        _merge_kernel,
        grid=(t // MERGE_TM,),
        in_specs=[
            pl.BlockSpec((MERGE_TM, d), row),
            pl.BlockSpec((MERGE_TM, onsa.shape[1]), row),
            pl.BlockSpec((MERGE_TM, osgu.shape[1]), row),
            pl.BlockSpec((MERGE_TM, omem.shape[1]), row),
            pl.BlockSpec((3 * d, d), c2, pipeline_mode=pl.Buffered(1)),
            once(wbn), once(wbs), once(wbm), once(wo), full(g1), full(b1),
        ],
        out_specs=pl.BlockSpec((MERGE_TM, d), row),
        out_shape=jax.ShapeDtypeStruct((t, d), F32),
        scratch_shapes=[pltpu.VMEM(a.shape, BF16) for a in (wbn, wbs, wbm, wo)]
        + [pltpu.VMEM((d, 3 * d), BF16)],
        compiler_params=pltpu.CompilerParams(
            dimension_semantics=("arbitrary",), vmem_limit_bytes=VMEM_LIMIT),
        name="merge",
    )(x2, onsa, osgu, omem, wg, wbn, wbs, wbm, wo, g1, b1)


MOE_TM = 1024
MOE_CH = 144
_YS_ROWS = -(-(MOE_TM + N_GROUPS * MOE_CH) // 256) * 256
_R_SLOT = 8
_DEST_LANE = 3 * _R_SLOT
_HID = EXPERTS_PER_GROUP * EXPERT_FF


def _moe_kernel(x_ref, tri_ref, wr_ref, br_ref, wg_ref, wu_ref, wd_ref, g_ref, b_ref, out_ref,
                xa_s, tok_s, keyr_s, ys_s, cnt_s, base_s):
    grp = pl.program_id(1)
    tm = x_ref.shape[0]
    d = x_ref.shape[1]

    @pl.when(grp == 0)
    def _route():
        xb = x_ref[...].astype(BF16)
        xa_s[:, 0:d] = xb
        logits = _dot(xb, wr_ref[...]) + br_ref[...]
        lt = logits.T
        row = lax.broadcasted_iota(jnp.int32, (_R_SLOT, tm), 0)
        gl = jnp.where(row < N_GROUPS, lt[0:_R_SLOT], NEG)
        gmax = jnp.max(gl, axis=0, keepdims=True)
        gidx = jnp.min(jnp.where(gl == gmax, row, _R_SLOT), axis=0, keepdims=True)
        gprob = 1.0 / jnp.sum(jnp.exp(gl - gmax), axis=0, keepdims=True)
        el = lt[_R_SLOT:2 * _R_SLOT]
        for k in range(1, N_GROUPS):
            el = jnp.where(gidx == k, lt[(k + 1) * _R_SLOT:(k + 2) * _R_SLOT], el)
        ee = jnp.exp(el - jnp.max(el, axis=0, keepdims=True))
        ep = ee / jnp.sum(ee, axis=0, keepdims=True)
        t1 = jnp.max(ep, axis=0, keepdims=True)
        i1 = jnp.min(jnp.where(ep == t1, row, _R_SLOT), axis=0, keepdims=True)
        rest = row != i1
        t2 = jnp.max(jnp.where(rest, ep, -1.0), axis=0, keepdims=True)
        i2 = jnp.min(jnp.where(rest & (ep == t2), row, _R_SLOT), axis=0, keepdims=True)
        den = t1 + t2
        cwf = (jnp.where(row == i1, t1 / den, 0.0) + jnp.where(row == i2, t2 / den, 0.0)) * gprob
        hi = cwf.astype(BF16).astype(F32)
        r1 = cwf - hi
        lo = r1.astype(BF16).astype(F32)
        lo2 = (r1 - lo).astype(BF16).astype(F32)

        onehot = jnp.where(row == gidx, 1.0, 0.0)
        rank = _dot_nt(onehot.astype(BF16), tri_ref[...])
        keyr_s[...] = jnp.where(onehot > 0.5, rank, -1.0)
        base = jnp.int32(0)
        basev = jnp.zeros((1, tm), F32)
        for k in range(N_GROUPS):
            n_k = jnp.sum(onehot[k:k + 1, :]).astype(jnp.int32)
            cnt_s[k] = n_k
            base_s[k] = base
            basev = jnp.where(gidx == k, base.astype(F32), basev)
            base = base + ((n_k + MOE_CH - 1) // MOE_CH) * MOE_CH
        dest = jnp.sum(rank * onehot, axis=0, keepdims=True) + basev
        tok = jnp.concatenate(
            [hi, lo, lo2, jnp.broadcast_to(dest, (_R_SLOT, tm)),
             jnp.zeros((LANES - 4 * _R_SLOT, tm), F32)], axis=0).T
        tok_s[...] = tok
        lane = lax.broadcasted_iota(jnp.int32, tok.shape, 1)
        xa_s[:, d:d + LANES] = jnp.where(lane < _DEST_LANE, tok, 0.0).astype(BF16)
        ys_s[...] = jnp.zeros(ys_s.shape, BF16)

    n_rows = cnt_s[grp]
    row0 = base_s[grp]
    keyr = keyr_s[pl.ds(grp, 1), :]
    half = _HID // 2

    def sweep(k, carry):
        ch = MOE_CH
        r_row = (lax.broadcasted_iota(jnp.int32, (ch, tm), 0) + k * MOE_CH).astype(F32)
        pick = jnp.where(keyr == r_row, 1.0, 0.0).astype(BF16)
        ga = _dot(pick, xa_s[...])
        xg = ga[:, 0:d].astype(BF16)
        cwg = ga[:, d:d + LANES]
        cs = (cwg + pltpu.roll(cwg, LANES - _R_SLOT, 1)
              + pltpu.roll(cwg, LANES - 2 * _R_SLOT, 1))
        hg = [_dot(xg, wg_ref[0, :, h * half:(h + 1) * half]) for h in range(2)]
        hu = [_dot(xg, wu_ref[0, :, h * half:(h + 1) * half]) for h in range(2)]
        y = None
        for h in range(2):
            per_half = EXPERTS_PER_GROUP // 2
            cexp = jnp.concatenate(
                [jnp.broadcast_to(cs[:, e:e + 1], (ch, EXPERT_FF))
                 for e in range(h * per_half, (h + 1) * per_half)], axis=1)
            hid = ((hg[h] * _sigmoid(hg[h])) * hu[h]) * cexp
            term = _dot(hid.astype(BF16), wd_ref[0, h * half:(h + 1) * half, :])
            y = term if y is None else y + term
        ys_s[pl.ds(pl.multiple_of(row0 + k * MOE_CH, 16), ch), :] = y.astype(BF16)
        return carry

    lax.fori_loop(0, (n_rows + MOE_CH - 1) // MOE_CH, sweep, 0)

    @pl.when(grp == N_GROUPS - 1)
    def _fin():
        r_col = lax.broadcasted_iota(jnp.int32, (tm, _YS_ROWS), 1).astype(F32)
        put = jnp.where(tok_s[:, _DEST_LANE:_DEST_LANE + 1] == r_col, 1.0, 0.0).astype(BF16)
        z = DN_ALPHA * x_ref[...] + _dot(put, ys_s[...])
        out_ref[...] = _layer_norm(z, g_ref[...], b_ref[...])


def _moe_call(x1, wr, br, wg, wu, wd, g2, b2):
    t, d = x1.shape
    row = lambda i, g: (i, 0)
    c2 = lambda i, g: (0, 0)
    idx = np.arange(MOE_TM)
    tri = jnp.asarray(idx[None, :] < idx[:, None], dtype=BF16)
    return pl.pallas_call(
        _moe_kernel,
        grid=(t // MOE_TM, N_GROUPS),
        in_specs=[
            pl.BlockSpec((MOE_TM, d), row, pipeline_mode=pl.Buffered(1)),
            pl.BlockSpec((MOE_TM, MOE_TM), c2, pipeline_mode=pl.Buffered(1)),
            pl.BlockSpec(wr.shape, c2),
            pl.BlockSpec(br.shape, c2),
            pl.BlockSpec((1, d, _HID), lambda i, g: (g, 0, 0)),
            pl.BlockSpec((1, d, _HID), lambda i, g: (g, 0, 0)),
            pl.BlockSpec((1, _HID, d), lambda i, g: (g, 0, 0)),
            pl.BlockSpec(g2.shape, c2),
            pl.BlockSpec(b2.shape, c2),
        ],
        out_specs=pl.BlockSpec((MOE_TM, d), row),
        out_shape=jax.ShapeDtypeStruct((t, d), F32),
        scratch_shapes=[
            pltpu.VMEM((MOE_TM, d + LANES), BF16),
            pltpu.VMEM((MOE_TM, LANES), F32),
            pltpu.VMEM((_R_SLOT, MOE_TM), F32),
            pltpu.VMEM((_YS_ROWS, d), BF16),
            pltpu.SMEM((N_GROUPS,), jnp.int32),
            pltpu.SMEM((N_GROUPS,), jnp.int32),
        ],
        compiler_params=pltpu.CompilerParams(
            dimension_semantics=("parallel", "arbitrary"), vmem_limit_bytes=MOE_VMEM_LIMIT),
        name="moe",
    )(x1, tri, wr, br, wg, wu, wd, g2, b2)


def _rope_table(seq):
    half = ROPE_DIM // 2
    inv = ROPE_THETA ** (-jnp.arange(0, ROPE_DIM, 2, dtype=F32) / ROPE_DIM)
    ang = jnp.arange(seq, dtype=F32)[:, None] * inv[None, :]
    cos, sin = jnp.cos(ang), jnp.sin(ang)
    rest = HEAD_DIM - ROPE_DIM
    one = jnp.ones((seq, rest), F32)
    zero = jnp.zeros((seq, rest), F32)
    zh = jnp.zeros((seq, half), F32)
    c_h = jnp.concatenate([cos, cos, one], axis=1)
    dn_h = jnp.concatenate([-sin, zh, zero], axis=1)
    up_h = jnp.concatenate([zh, sin, zero], axis=1)
    rep = LANES // HEAD_DIM
    return jnp.concatenate([jnp.tile(c_h, (1, rep)), jnp.tile(dn_h, (1, rep)), jnp.tile(up_h, (1, rep))],
                           axis=1)


def _expand_cmp_weights(pe, w1, w2):
    g_n = NSA_KV_GROUPS
    w1r = w1.astype(BF16).reshape(2, CMP_STRIDE, HEAD_DIM, CMP_HIDDEN)
    w2b = w2.astype(BF16)
    w1e = jnp.concatenate(
        [jnp.stack([w1r if g == e else jnp.zeros_like(w1r) for g in range(g_n)], axis=2)
         .reshape(2, A_W, CMP_HIDDEN) for e in range(g_n)], axis=2)
    w2e = jnp.concatenate(
        [jnp.concatenate([w2b if g == e else jnp.zeros_like(w2b) for e in range(g_n)], axis=1)
         for g in range(g_n)], axis=0)
    per = pe.reshape(2, CMP_STRIDE, 1, HEAD_DIM)
    pee = jnp.broadcast_to(per, (2, CMP_STRIDE, g_n, HEAD_DIM)).reshape(2, A_W)
    pee = jnp.concatenate([pee, jnp.zeros((6, A_W), pe.dtype)], axis=0)
    return pee.astype(F32), w1e.astype(BF16), w2e.astype(BF16)


def kernel(x, mem, w_in, cmp_pe_k, cmp_w1_k, cmp_w2_k, cmp_pe_v, cmp_w1_v, cmp_w2_v, sgu_ln_g, sgu_ln_b,
           sgu_w_s, sgu_b_s, w_mem_kv, w_br_nsa, w_br_sgu, w_br_mem, w_o, ln1_g, ln1_b, w_router_group,
           b_router_group, w_router_expert, b_router_expert, w_exp_gate, w_exp_up, w_exp_down, ln2_g, ln2_b):
    batch, seq, d = x.shape
    t = batch * seq
    assert w_in.shape[0] == DEPTH == 1
    assert seq % PROJ_TM == 0 and seq // CMP_STRIDE == N_A and seq // SLC_BLOCK == N_SEL

    offs = [int(v) for v in np.cumsum(
        [0, Q_W, KV_W, KV_W, KV_W, KV_W, KV_W, KV_W, GATE_W, 2 * SGU_WIDTH, MEM_WIDTH, 3 * d])]
    assert offs[-1] == w_in.shape[2]
    w_t = jnp.transpose(w_in[0])
    seg = lambda i: w_t[offs[i]:offs[i + 1]]
    gate_rows = jnp.pad(seg(7), ((0, LANES - GATE_W), (0, 0)))
    wp_t = jnp.concatenate([seg(0), seg(1), seg(3), seg(5), seg(2), seg(4), seg(6), gate_rows, seg(8), seg(9)],
                           axis=0).astype(BF16)
    w_mt = seg(10).astype(BF16)
    assert wp_t.shape[0] == _PROJ_COLS
    rope = _rope_table(seq)
    tril = jnp.tril(jnp.ones((SGU_CHUNK, SGU_CHUNK), dtype=bool))
    ws = jnp.where(tril[None], sgu_w_s[0], 0.0).astype(BF16)
    bs = jnp.repeat(sgu_b_s[0].T, SGU_WIDTH // SGU_GROUPS, axis=1)

    x2 = x.reshape(t, d)
    (qpad, kc, ksl, kwn, vc, vslt, vwnt, gt, osgu, mq) = _proj_call(
        x2, wp_t, rope, sgu_ln_g[0][None], sgu_ln_b[0][None], ws, bs, seq)

    pek, w1k, w2k = _expand_cmp_weights(cmp_pe_k[0], cmp_w1_k[0], cmp_w2_k[0])
    pev, w1v, w2v = _expand_cmp_weights(cmp_pe_v[0], cmp_w1_v[0], cmp_w2_v[0])
    kcmp, vcmp = _compress_call(kc, vc, pek, pev, w1k, w2k, w1v, w2v, batch)

    nqb = seq // Q_BLOCK
    n_g = NSA_KV_GROUPS
    vcmpt = vcmp.reshape(batch, N_A, KV_W).transpose(0, 2, 1).reshape(batch * KV_W, N_A)
    ci = np.arange(N_A)
    sj = np.arange(N_SEL)
    overlap = ((ci[None, :] * CMP_STRIDE + CMP_LEN - 1 >= sj[:, None] * SLC_BLOCK)
               & (ci[None, :] * CMP_STRIDE <= sj[:, None] * SLC_BLOCK + SLC_BLOCK - 1)
               & (ci[None, :] < (seq - CMP_LEN) // CMP_STRIDE + 1))
    ot = jnp.asarray(overlap, dtype=BF16)
    onsa = _nsa_call(qpad, ksl, kwn, vslt, vwnt, kcmp, vcmpt, gt, ot, batch, seq)

    omem = _memattn_call(mq, mem.reshape(batch * mem.shape[1], d), w_mem_kv[0], batch, seq)

    x1 = _merge_call(x2, onsa, osgu, omem, w_mt, w_br_nsa[0], w_br_sgu[0], w_br_mem[0], w_o[0],
                     ln1_g[0][None], ln1_b[0][None])

    assert EXPERTS_PER_GROUP == _R_SLOT and N_GROUPS <= _R_SLOT
    n_r = _R_SLOT + N_GROUPS * EXPERTS_PER_GROUP
    wr = jnp.concatenate([jnp.pad(w_router_group[0], ((0, 0), (0, _R_SLOT - N_GROUPS))),
                          w_router_expert[0]], axis=1)
    wr = jnp.pad(wr, ((0, 0), (0, LANES - n_r))).astype(BF16)
    br = jnp.concatenate([jnp.pad(b_router_group[0], (0, _R_SLOT - N_GROUPS)), b_router_expert[0]])
    br = jnp.pad(br, (0, LANES - n_r))[None]
    wg = w_exp_gate[0].transpose(0, 2, 1, 3).reshape(N_GROUPS, d, _HID).astype(BF16)
    wu = w_exp_up[0].transpose(0, 2, 1, 3).reshape(N_GROUPS, d, _HID).astype(BF16)
    wd = w_exp_down[0].reshape(N_GROUPS, _HID, d).astype(BF16)
    out = _moe_call(x1, wr, br, wg, wu, wd, ln2_g[0][None], ln2_b[0][None])
    return out.reshape(batch, seq, d)
```

```python
import functools

import numpy as np
import jax
import jax.numpy as jnp
from jax import lax
from jax.experimental import pallas as pl
from jax.experimental.pallas import tpu as pltpu

NSA_HEADS = 8
NSA_KV_GROUPS = 2
NSA_HPG = NSA_HEADS // NSA_KV_GROUPS
HEAD_DIM = 64
CMP_LEN = 32
CMP_STRIDE = 16
CMP_HIDDEN = 256
SLC_BLOCK = 64
SLC_TOPN = 8
WINDOW = 512
Q_BLOCK = 128
N_BAND = WINDOW // Q_BLOCK
ROPE_THETA = 500000.0
ROPE_DIM = HEAD_DIM // 4
SGU_CHUNK = 128
SGU_GROUPS = 8
SGU_WIDTH = 512
MEM_HEADS = 4
MEM_HEAD_DIM = 128
MEM_WIDTH = MEM_HEADS * MEM_HEAD_DIM
N_GROUPS = 4
EXPERTS_PER_GROUP = 8
EXPERT_FF = 256
DEPTH = 1
DN_ALPHA = (2.0 * DEPTH) ** 0.25
LN_EPS = 1e-5
NEG = -1e30
LOG2E = 1.4426950408889634

LANES = 128
Q_W = NSA_HEADS * HEAD_DIM
KV_W = NSA_KV_GROUPS * HEAD_DIM
GATE_W = NSA_HEADS * 3
VMEM_LIMIT = 56 * 1024 * 1024
MOE_VMEM_LIMIT = 60 * 1024 * 1024

BF16 = jnp.bfloat16
F32 = jnp.float32


def _dot(a, b):
    return jnp.dot(a, b, preferred_element_type=F32)


def _dot_nt(a, b):
    return lax.dot_general(a, b, (((1,), (1,)), ((), ())), preferred_element_type=F32)


def _sigmoid(x):
    return 1.0 / (1.0 + jnp.exp(-x))


def _gelu(x):
    return 0.5 * x * (1.0 + lax.erf(x * (2.0 ** -0.5)))


def _layer_norm(x, g, b):
    mu = jnp.mean(x, axis=-1, keepdims=True)
    xc = x - mu
    var = jnp.mean(xc * xc, axis=-1, keepdims=True)
    return xc * lax.rsqrt(var + LN_EPS) * g + b


PROJ_TM = 512
_ROPE_COLS = Q_W + 3 * KV_W
_V_OFF = _ROPE_COLS
_G_OFF = _V_OFF + 3 * KV_W
_SGU_OFF = _G_OFF + LANES
_MQ_OFF = _SGU_OFF + 2 * SGU_WIDTH
_PROJ_COLS = _MQ_OFF + MEM_WIDTH


def _store_strided_rows(val, out_ref, tmp_ref):
    tmp_ref[...] = val
    n = val.shape[0] // CMP_STRIDE
    for l in range(CMP_STRIDE):
        out_ref[:, l * LANES:(l + 1) * LANES] = tmp_ref[pl.ds(l, n, stride=CMP_STRIDE), :].astype(BF16)


def _transpose_weight(wt_ref, w_s):
    for j in range(wt_ref.shape[0] // LANES):
        rows = slice(j * LANES, (j + 1) * LANES)
        w_s[:, rows] = wt_ref[rows, :].astype(F32).T.astype(BF16)


def _proj_kernel(x_ref, wt_ref, rope_ref, lng_ref, lnb_ref, ws_ref, bs_ref,
                 q_ref, kc_ref, ksl_ref, kwn_ref, vc_ref, vsl_ref, vwn_ref, gate_ref, osgu_ref, mq_ref,
                 a_s, w_ref):
    pl.when(pl.program_id(0) == 0)(functools.partial(_transpose_weight, wt_ref, w_ref))
    xb = x_ref[...].astype(BF16)
    lane = lax.broadcasted_iota(jnp.int32, (PROJ_TM, LANES), 1)
    low = lane < HEAD_DIM
    cos = rope_ref[:, 0:LANES]
    s_dn = rope_ref[:, LANES:2 * LANES]
    s_up = rope_ref[:, 2 * LANES:3 * LANES]

    hz = _dot(xb, w_ref[:, _SGU_OFF:_MQ_OFF])
    h = _dot(xb, w_ref[:, 0:_ROPE_COLS])
    k_refs = (kc_ref, ksl_ref, kwn_ref)
    for j in range(_ROPE_COLS // LANES):
        blk = h[:, j * LANES:(j + 1) * LANES]
        r = (blk * cos + pltpu.roll(blk, LANES - ROPE_DIM // 2, 1) * s_dn
             + pltpu.roll(blk, ROPE_DIM // 2, 1) * s_up)
        if j < Q_W // LANES:
            r = r * (HEAD_DIM ** -0.5 * LOG2E)
            sw = pltpu.roll(r, HEAD_DIM, 1)
            g = (2 * j) // NSA_HPG
            if g == 0:
                h0 = jnp.where(low, r, 0.0)
                h1 = jnp.where(low, sw, 0.0)
            else:
                h0 = jnp.where(low, 0.0, sw)
                h1 = jnp.where(low, 0.0, r)
            q_ref[:, (2 * j) * LANES:(2 * j + 1) * LANES] = h0.astype(BF16)
            q_ref[:, (2 * j + 1) * LANES:(2 * j + 2) * LANES] = h1.astype(BF16)
        else:
            if j == Q_W // LANES:
                _store_strided_rows(r, kc_ref, a_s)
            else:
                k_refs[j - Q_W // LANES][...] = r.astype(BF16)

    _store_strided_rows(_dot(xb, w_ref[:, _V_OFF:_V_OFF + LANES]), vc_ref, a_s)
    ones = jnp.ones((V_ROWS - HEAD_DIM, PROJ_TM), BF16)
    for v_ref, row0, blk in ((vsl_ref, _V_OFF + LANES, SEL_CHUNK), (vwn_ref, _V_OFF + 2 * LANES, Q_BLOCK)):
        vt = _dot_nt(wt_ref[row0:row0 + LANES, :], xb).astype(BF16)
        for g in range(NSA_KV_GROUPS):
            for b in range(PROJ_TM // blk):
                v_ref[g, b, 0:HEAD_DIM, :] = vt[g * HEAD_DIM:(g + 1) * HEAD_DIM, b * blk:(b + 1) * blk]
                v_ref[g, b, HEAD_DIM:V_ROWS, :] = ones[:, 0:blk]

    gate_ref[...] = _sigmoid(_dot_nt(wt_ref[_G_OFF:_SGU_OFF, :], xb))

    mq_ref[...] = _dot(xb, w_ref[:, _MQ_OFF:_PROJ_COLS]).astype(BF16)

    z = _gelu(hz)
    u = z[:, 0:SGU_WIDTH]
    v = _layer_norm(z[:, SGU_WIDTH:2 * SGU_WIDTH], lng_ref[...], lnb_ref[...]).astype(BF16)
    lane_c = lax.broadcasted_iota(jnp.int32, (SGU_CHUNK, LANES), 1)
    low_c = lane_c < (SGU_WIDTH // SGU_GROUPS)
    n_ch = PROJ_TM // SGU_CHUNK
    for gp in range(SGU_WIDTH // LANES):
        cols = slice(gp * LANES, (gp + 1) * LANES)
        vcat = jnp.concatenate([v[ci * SGU_CHUNK:(ci + 1) * SGU_CHUNK, cols] for ci in range(n_ch)], axis=1)
        sv0 = _dot(ws_ref[2 * gp], vcat)
        sv1 = _dot(ws_ref[2 * gp + 1], vcat)
        for ci in range(n_ch):
            rows = slice(ci * SGU_CHUNK, (ci + 1) * SGU_CHUNK)
            lanes = slice(ci * LANES, (ci + 1) * LANES)
            sv = jnp.where(low_c, sv0[:, lanes], sv1[:, lanes]) + bs_ref[:, cols]
            osgu_ref[rows, cols] = (u[rows, cols] * sv).astype(BF16)


def _proj_call(x2, wp, rope, lng, lnb, ws, bs, seq):
    t = x2.shape[0]
    d = x2.shape[1]
    nt = t // PROJ_TM
    per_seq = seq // PROJ_TM
    row = lambda i: (i, 0)
    const2 = lambda i: (0, 0)
    assert PROJ_TM % SEL_CHUNK == 0 and PROJ_TM % Q_BLOCK == 0
    n_g = NSA_KV_GROUPS
    batch = t // seq
    vblock = lambda i: (i // per_seq, i % per_seq, 0, 0)

    def v_shape(blk):
        return jax.ShapeDtypeStruct((batch * n_g, seq // blk, V_ROWS, blk), BF16)

    def v_spec(blk):
        return pl.BlockSpec((n_g, PROJ_TM // blk, V_ROWS, blk), vblock)

    a_shape = jax.ShapeDtypeStruct((t // CMP_STRIDE, CMP_STRIDE * KV_W), BF16)
    a_spec = pl.BlockSpec((PROJ_TM // CMP_STRIDE, CMP_STRIDE * KV_W), row)
    k_shape = jax.ShapeDtypeStruct((t, KV_W), BF16)
    k_spec = pl.BlockSpec((PROJ_TM, KV_W), row)
    out_shapes = (
        jax.ShapeDtypeStruct((t, NSA_HEADS * LANES), BF16),
        a_shape, k_shape, k_shape, a_shape,
        v_shape(SEL_CHUNK), v_shape(Q_BLOCK),
        jax.ShapeDtypeStruct((LANES, t), F32),
        jax.ShapeDtypeStruct((t, SGU_WIDTH), BF16),
        jax.ShapeDtypeStruct((t, MEM_WIDTH), BF16),
    )
    out_specs = (
        pl.BlockSpec((PROJ_TM, NSA_HEADS * LANES), row),
        a_spec, k_spec, k_spec, a_spec,
        v_spec(SEL_CHUNK), v_spec(Q_BLOCK),
        pl.BlockSpec((LANES, PROJ_TM), lambda i: (0, i)),
        pl.BlockSpec((PROJ_TM, SGU_WIDTH), row),
        pl.BlockSpec((PROJ_TM, MEM_WIDTH), row),
    )
    return pl.pallas_call(
        _proj_kernel,
        grid=(nt,),
        in_specs=[
            pl.BlockSpec((PROJ_TM, d), row),
            pl.BlockSpec((_PROJ_COLS, d), const2),
            pl.BlockSpec((PROJ_TM, 3 * LANES), lambda i: (i % per_seq, 0)),
            pl.BlockSpec((1, SGU_WIDTH), const2),
            pl.BlockSpec((1, SGU_WIDTH), const2),
            pl.BlockSpec((SGU_GROUPS, SGU_CHUNK, SGU_CHUNK), lambda i: (0, 0, 0)),
            pl.BlockSpec((SGU_CHUNK, SGU_WIDTH), const2),
        ],
        out_specs=out_specs,
        out_shape=out_shapes,
        scratch_shapes=[pltpu.VMEM((PROJ_TM, KV_W), F32), pltpu.VMEM((d, _PROJ_COLS), BF16)],
        compiler_params=pltpu.CompilerParams(
            dimension_semantics=("arbitrary",), vmem_limit_bytes=VMEM_LIMIT),
        name="proj",
    )(x2, wp, rope, lng, lnb, ws, bs)


N_A = 128
A_W = CMP_STRIDE * KV_W


def _compress_kernel(ka_ref, va_ref, pek_ref, pev_ref, w1k_ref, w2k_ref, w1v_ref, w2v_ref,
                     kcmp_ref, vcmp_ref):
    def one(a_ref, pe_ref, w1_ref, w2_ref, out_ref):
        a = a_ref[...].astype(F32)
        top = (a + pe_ref[0:1, :]).astype(BF16)
        bot = (a + pe_ref[1:2, :]).astype(BF16)
        h1 = _dot(top, w1_ref[0])
        h2 = _dot(bot, w1_ref[1])
        pre = h1 + pltpu.roll(h2, N_A - 1, 0)
        act = _gelu(pre).astype(BF16)
        out_ref[...] = _dot(act, w2_ref[...]).astype(BF16)

    one(ka_ref, pek_ref, w1k_ref, w2k_ref, kcmp_ref)
    one(va_ref, pev_ref, w1v_ref, w2v_ref, vcmp_ref)


def _compress_call(ka, va, pek, pev, w1k, w2k, w1v, w2v, batch):
    row = lambda b: (b, 0)
    c2 = lambda b: (0, 0)
    c3 = lambda b: (0, 0, 0)
    hid2 = NSA_KV_GROUPS * CMP_HIDDEN
    return pl.pallas_call(
        _compress_kernel,
        grid=(batch,),
        in_specs=[
            pl.BlockSpec((N_A, A_W), row),
            pl.BlockSpec((N_A, A_W), row),
            pl.BlockSpec((8, A_W), c2),
            pl.BlockSpec((8, A_W), c2),
            pl.BlockSpec((2, A_W, hid2), c3),
            pl.BlockSpec((hid2, KV_W), c2),
            pl.BlockSpec((2, A_W, hid2), c3),
            pl.BlockSpec((hid2, KV_W), c2),
        ],
        out_specs=(pl.BlockSpec((N_A, KV_W), row), pl.BlockSpec((N_A, KV_W), row)),
        out_shape=(jax.ShapeDtypeStruct((batch * N_A, KV_W), BF16),
                   jax.ShapeDtypeStruct((batch * N_A, KV_W), BF16)),
        compiler_params=pltpu.CompilerParams(
            dimension_semantics=("parallel",), vmem_limit_bytes=VMEM_LIMIT),
        name="compress",
    )(ka, va, pek, pev, w1k, w2k, w1v, w2v)


N_SEL = 32
SEL_CHUNK = 512
SEL_SUB = 128
HQ = NSA_HPG * Q_BLOCK
V_ROWS = HEAD_DIM + 16


def _nsa_kernel(q_ref, ksl_ref, kwn_ref, vslt_ref, vwnt_ref, kcmp_ref, vcmpt_ref, gt_ref,
                ot_ref, o_ref, m_s, acc_s, ocmp_s, owin_s, cap_s, out_s):
    c = pl.program_id(1)
    lane_hq = lax.broadcasted_iota(jnp.int32, (1, HQ), 1)
    pos_hq = c * Q_BLOCK + (lane_hq & (Q_BLOCK - 1))
    pos_q = c * Q_BLOCK + lax.broadcasted_iota(jnp.int32, (1, Q_BLOCK), 1)
    groups = range(NSA_KV_GROUPS)
    qgs = [jnp.concatenate(
        [q_ref[:, (g * NSA_HPG + hh) * LANES:(g * NSA_HPG + hh + 1) * LANES]
         for hh in range(NSA_HPG)], axis=0) for g in groups]

    kcmp = kcmp_ref[...]
    s_c = [_dot_nt(kcmp, qgs[g]) for g in groups]
    n_idx = lax.broadcasted_iota(jnp.int32, (N_A, HQ), 0)
    valid_c = (n_idx * CMP_STRIDE + (CMP_LEN - 1)) <= pos_hq
    p_c = []
    for g in groups:
        sm_c = jnp.where(valid_c, s_c[g], NEG)
        m_c = jnp.max(sm_c, axis=0, keepdims=True)
        e_c = jnp.where(valid_c, jnp.exp2(sm_c - m_c), 0.0)
        d_c = jnp.sum(e_c, axis=0, keepdims=True)
        p_c.append(e_c / jnp.where(d_c > 0, d_c, 1.0))
    for g in groups:
        ocmp_s[g] = _dot(vcmpt_ref[g * HEAD_DIM:(g + 1) * HEAD_DIM, :], p_c[g].astype(BF16))

    ot = ot_ref[...]
    imp = []
    for g in groups:
        ps = (p_c[g][:, 0:Q_BLOCK] + p_c[g][:, Q_BLOCK:2 * Q_BLOCK]
              + p_c[g][:, 2 * Q_BLOCK:3 * Q_BLOCK] + p_c[g][:, 3 * Q_BLOCK:4 * Q_BLOCK])
        p_hi = ps.astype(BF16)
        r1 = ps - p_hi.astype(F32)
        p_lo = r1.astype(BF16)
        p_lo2 = (r1 - p_lo.astype(F32)).astype(BF16)
        imp.append(_dot(ot, p_hi) + _dot(ot, p_lo) + _dot(ot, p_lo2))

    j_idx = lax.broadcasted_iota(jnp.int32, (N_SEL, Q_BLOCK), 0)
    cur = pos_q // SLC_BLOCK
    future = j_idx > cur
    forced = (j_idx == 0) | (j_idx == cur) | (j_idx == cur - 1)
    imp = [jnp.where(future, NEG, jnp.where(forced, -NEG, imp[g])) for g in groups]
    rank = [jnp.zeros((N_SEL, Q_BLOCK), F32) for g in groups]
    for i in range(N_SEL):
        for g in groups:
            row = imp[g][i:i + 1, :]
            beats = (row > imp[g]) | ((row == imp[g]) & (j_idx > i))
            rank[g] = rank[g] + jnp.where(beats, 1.0, 0.0)
    for g in groups:
        cap_s[g] = jnp.where(rank[g] < float(SLC_TOPN), -NEG, NEG)
        m_s[g] = jnp.full((1, HQ), NEG, F32)
        acc_s[g] = jnp.zeros((V_ROWS, HQ), F32)

    blocks_per_sub = SEL_SUB // SLC_BLOCK
    subs_per_chunk = SEL_CHUNK // SEL_SUB

    def sel_spans(kc):
        return [slice(kc * SEL_CHUNK + a * SEL_SUB, kc * SEL_CHUNK + (a + 1) * SEL_SUB)
                for a in range(subs_per_chunk)]

    def sel_scores(kc):
        return [[_dot_nt(ksl_ref[ks, :], qgs[g]) for g in groups] for ks in sel_spans(kc)]

    def sel_chunk(kc, s=None):
        spans = sel_spans(kc)
        s = sel_scores(kc) if s is None else s
        for a, ks in enumerate(spans):
            kpos = ks.start + lax.broadcasted_iota(jnp.int32, (SEL_SUB, Q_BLOCK), 0)
            causal = kpos <= pos_q
            for g in groups:
                j0 = ks.start // SLC_BLOCK
                cap = jnp.concatenate(
                    [jnp.broadcast_to(cap_s[g, j:j + 1, :], (SLC_BLOCK, Q_BLOCK))
                     for j in range(j0, j0 + blocks_per_sub)], axis=0)
                cap = jnp.where(causal, cap, NEG)
                sm = jnp.minimum(s[a][g], jnp.concatenate([cap] * NSA_HPG, axis=1))
                m_old = m_s[g]
                m_new = jnp.maximum(m_old, jnp.max(sm, axis=0, keepdims=True))
                alpha = jnp.exp2(m_old - m_new)
                e = jnp.exp2((sm - m_new).astype(BF16))
                v_t = vslt_ref[g, kc, :, a * SEL_SUB:(a + 1) * SEL_SUB]
                acc_s[g] = alpha * acc_s[g] + _dot(v_t, e)
                m_s[g] = m_new

    s0 = sel_scores(0)
    _nsa_window(c, qgs, kwn_ref, vwnt_ref, owin_s)
    sel_chunk(0, s0)
    def later_chunks(n):
        s_next = sel_scores(1)
        for kc in range(1, n + 1):
            s_cur = s_next
            if kc < n:
                s_next = sel_scores(kc + 1)
            sel_chunk(kc, s_cur)

    for n in range(1, ksl_ref.shape[0] // SEL_CHUNK):
        pl.when(c // (SEL_CHUNK // Q_BLOCK) == n)(functools.partial(later_chunks, n))

    for g in groups:
        acc = acc_s[g]
        o_sel = acc[0:HEAD_DIM, :] * (1.0 / acc[HEAD_DIM:HEAD_DIM + 1, :])

        def gate_row(br):
            return jnp.concatenate(
                [gt_ref[(g * NSA_HPG + hh) * 3 + br:(g * NSA_HPG + hh) * 3 + br + 1, :]
                 for hh in range(NSA_HPG)], axis=1)
        o_t = gate_row(0) * ocmp_s[g] + gate_row(1) * o_sel + gate_row(2) * owin_s[g]
        for hh in range(NSA_HPG):
            h = g * NSA_HPG + hh
            out_s[h * HEAD_DIM:(h + 1) * HEAD_DIM, :] = o_t[:, hh * Q_BLOCK:(hh + 1) * Q_BLOCK]

    o_ref[...] = out_s[...].T.astype(BF16)


def _nsa_window(c, qgs, kwn_ref, vwnt_ref, owin_s):
    groups = range(NSA_KV_GROUPS)
    q_i = lax.broadcasted_iota(jnp.int32, (Q_BLOCK, Q_BLOCK), 1)
    k_i = lax.broadcasted_iota(jnp.int32, (Q_BLOCK, Q_BLOCK), 0)
    blks, caps = [], []
    for i in range(N_BAND + 1):
        blk = c - N_BAND + i
        blks.append(jnp.maximum(blk, 0))
        off = (N_BAND - i) * Q_BLOCK
        if off - (Q_BLOCK - 1) >= 0 and off + (Q_BLOCK - 1) < WINDOW:
            caps.append(jnp.where(blk >= 0, -NEG, NEG))
        else:
            diff = off + q_i - k_i
            ok = (diff >= 0) & (diff < WINDOW) & (blk >= 0)
            caps.append(jnp.concatenate([jnp.where(ok, -NEG, NEG)] * NSA_HPG, axis=1))
    k_blocks = [kwn_ref[pl.ds(pl.multiple_of(blks[i] * Q_BLOCK, Q_BLOCK), Q_BLOCK), :]
                for i in range(N_BAND + 1)]
    s_w = [[None] * (N_BAND + 1) for g in groups]
    for i in reversed(range(N_BAND + 1)):
        for g in groups:
            s_w[g][i] = _dot_nt(k_blocks[i], qgs[g])
    m_w = [None for g in groups]
    o_win = [None for g in groups]
    for i in reversed(range(N_BAND + 1)):
        for g in groups:
            sm = jnp.minimum(s_w[g][i], caps[i])
            m_blk = jnp.max(sm, axis=0, keepdims=True)
            m_new = m_blk if m_w[g] is None else jnp.maximum(m_w[g], m_blk)
            pv = _dot(vwnt_ref[g, blks[i]], jnp.exp2((sm - m_new).astype(BF16)))
            o_win[g] = pv if m_w[g] is None else jnp.exp2(m_w[g] - m_new) * o_win[g] + pv
            m_w[g] = m_new
    for g in groups:
        owin_s[g] = o_win[g][0:HEAD_DIM, :] * (1.0 / o_win[g][HEAD_DIM:HEAD_DIM + 1, :])


def _nsa_call(qpad, ksl, kwn, vslt, vwnt, kcmp, vcmpt, gt, ot, batch, seq):
    nqb = seq // Q_BLOCK
    n_g = NSA_KV_GROUPS
    qrow = lambda b, c: (b * nqb + c, 0)
    brow = lambda b, c: (b, 0)
    c2 = lambda b, c: (0, 0)
    return pl.pallas_call(
        _nsa_kernel,
        grid=(batch, nqb),
        in_specs=[
            pl.BlockSpec((Q_BLOCK, NSA_HEADS * LANES), qrow),
            pl.BlockSpec((seq, KV_W), brow),
            pl.BlockSpec((seq, KV_W), brow),
            pl.BlockSpec((n_g, seq // SEL_CHUNK, V_ROWS, SEL_CHUNK), lambda b, c: (b, 0, 0, 0)),
            pl.BlockSpec((n_g, nqb, V_ROWS, Q_BLOCK), lambda b, c: (b, 0, 0, 0)),
            pl.BlockSpec((N_A, KV_W), brow),
            pl.BlockSpec((KV_W, N_A), brow),
            pl.BlockSpec((LANES, Q_BLOCK), lambda b, c: (0, b * nqb + c)),
            pl.BlockSpec((N_SEL, N_A), c2),
        ],
        out_specs=pl.BlockSpec((Q_BLOCK, Q_W), qrow),
        out_shape=jax.ShapeDtypeStruct((batch * seq, Q_W), BF16),
        scratch_shapes=[
            pltpu.VMEM((n_g, 1, HQ), F32),
            pltpu.VMEM((n_g, V_ROWS, HQ), F32),
            pltpu.VMEM((n_g, HEAD_DIM, HQ), F32),
            pltpu.VMEM((n_g, HEAD_DIM, HQ), F32),
            pltpu.VMEM((n_g, N_SEL, Q_BLOCK), F32),
            pltpu.VMEM((Q_W, Q_BLOCK), F32),
        ],
        compiler_params=pltpu.CompilerParams(
            dimension_semantics=("parallel", "arbitrary"), vmem_limit_bytes=VMEM_LIMIT),
        name="nsa",
    )(qpad, ksl, kwn, vslt, vwnt, kcmp, vcmpt, gt, ot)


MEM_TM = 512
MEM_V_ROWS = MEM_HEAD_DIM + 16


def _memattn_kernel(mq_ref, mem_ref, wkv_ref, o_ref, k_s, vt_s, out_s):
    @pl.when(pl.program_id(1) == 0)
    def _():
        kv = _dot(mem_ref[...].astype(BF16), wkv_ref[...].astype(BF16))
        k_s[...] = kv[:, 0:MEM_WIDTH].astype(BF16)
        vt = kv[:, MEM_WIDTH:2 * MEM_WIDTH].T.astype(BF16)
        for h in range(MEM_HEADS):
            vt_s[h, 0:MEM_HEAD_DIM, :] = vt[h * MEM_HEAD_DIM:(h + 1) * MEM_HEAD_DIM, :]
            vt_s[h, MEM_HEAD_DIM:, :] = jnp.ones((MEM_V_ROWS - MEM_HEAD_DIM, vt.shape[1]), BF16)

    heads = range(MEM_HEADS)
    cols = [slice(h * MEM_HEAD_DIM, (h + 1) * MEM_HEAD_DIM) for h in heads]
    s_t = [_dot_nt(k_s[:, cols[h]], mq_ref[:, cols[h]]) for h in heads]
    e = []
    for h in heads:
        m = jnp.max(s_t[h], axis=0, keepdims=True)
        e.append(jnp.exp2((s_t[h] - m) * (MEM_HEAD_DIM ** -0.5 * LOG2E)).astype(BF16))
    o_t = [_dot(vt_s[h], e[h]) for h in heads]
    for h in heads:
        out_s[cols[h], :] = o_t[h][0:MEM_HEAD_DIM, :] * (1.0 / o_t[h][MEM_HEAD_DIM:MEM_HEAD_DIM + 1, :])
    o_ref[...] = out_s[...].T.astype(BF16)


def _memattn_call(mq, mem2, wkv, batch, seq):
    m_len = mem2.shape[0] // batch
    d = mem2.shape[1]
    per = seq // MEM_TM
    return pl.pallas_call(
        _memattn_kernel,
        grid=(batch, per),
        in_specs=[
            pl.BlockSpec((MEM_TM, MEM_WIDTH), lambda b, i: (b * per + i, 0)),
            pl.BlockSpec((m_len, d), lambda b, i: (b, 0)),
            pl.BlockSpec((d, 2 * MEM_WIDTH), lambda b, i: (0, 0)),
        ],
        out_specs=pl.BlockSpec((MEM_TM, MEM_WIDTH), lambda b, i: (b * per + i, 0)),
        out_shape=jax.ShapeDtypeStruct((batch * seq, MEM_WIDTH), BF16),
        scratch_shapes=[
            pltpu.VMEM((m_len, MEM_WIDTH), BF16),
            pltpu.VMEM((MEM_HEADS, MEM_V_ROWS, m_len), BF16),
            pltpu.VMEM((MEM_WIDTH, MEM_TM), F32),
        ],
        compiler_params=pltpu.CompilerParams(
            dimension_semantics=("parallel", "arbitrary"), vmem_limit_bytes=VMEM_LIMIT),
        name="memattn",
    )(mq, mem2, wkv)


MERGE_TM = 512


def _merge_kernel(x_ref, onsa_ref, osgu_ref, omem_ref, wgt_ref, wbn32_ref, wbs32_ref, wbm32_ref, wo32_ref,
                  g_ref, b_ref, out_ref, wbn_ref, wbs_ref, wbm_ref, wo_ref, wg_ref):
    @pl.when(pl.program_id(0) == 0)
    def _():
        for dst, src in ((wbn_ref, wbn32_ref), (wbs_ref, wbs32_ref), (wbm_ref, wbm32_ref),
                         (wo_ref, wo32_ref)):
            dst[...] = src[...].astype(BF16)
        _transpose_weight(wgt_ref, wg_ref)

    d = x_ref.shape[1]
    branches = ((onsa_ref, wbn_ref), (osgu_ref, wbs_ref), (omem_ref, wbm_ref))
    halves = [slice(h * (MERGE_TM // 2), (h + 1) * (MERGE_TM // 2)) for h in range(2)]
    xs = [x_ref[rows, :] for rows in halves]
    xbs = [x.astype(BF16) for x in xs]
    logits = [[_dot(xb, wg_ref[:, br * d:(br + 1) * d]) for br in range(len(branches))] for xb in xbs]
    projs = [[_dot(o_r[rows, :], w_r[...]) for o_r, w_r in branches] for rows in halves]
    ys = []
    for h in range(2):
        y = None
        for br in range(len(branches)):
            term = _sigmoid(logits[h][br]) * projs[h][br]
            y = term if y is None else y + term
        ys.append(y.astype(BF16))
    outs = [_dot(y, wo_ref[...]) for y in ys]
    for h, rows in enumerate(halves):
        out_ref[rows, :] = _layer_norm(DN_ALPHA * xs[h] + outs[h], g_ref[...], b_ref[...])


def _merge_call(x2, onsa, osgu, omem, wg, wbn, wbs, wbm, wo, g1, b1):
    t, d = x2.shape
    row = lambda i: (i, 0)
    c2 = lambda i: (0, 0)
    full = lambda a: pl.BlockSpec(a.shape, c2)
    once = lambda a: pl.BlockSpec(a.shape, c2, pipeline_mode=pl.Buffered(1))
    return pl.pallas_call(
        _merge_kernel,
        grid=(t // MERGE_TM,),
        in_specs=[
            pl.BlockSpec((MERGE_TM, d), row),
            pl.BlockSpec((MERGE_TM, onsa.shape[1]), row),
            pl.BlockSpec((MERGE_TM, osgu.shape[1]), row),
            pl.BlockSpec((MERGE_TM, omem.shape[1]), row),
            pl.BlockSpec((3 * d, d), c2, pipeline_mode=pl.Buffered(1)),
            once(wbn), once(wbs), once(wbm), once(wo), full(g1), full(b1),
        ],
        out_specs=pl.BlockSpec((MERGE_TM, d), row),
        out_shape=jax.ShapeDtypeStruct((t, d), F32),
        scratch_shapes=[pltpu.VMEM(a.shape, BF16) for a in (wbn, wbs, wbm, wo)]
        + [pltpu.VMEM((d, 3 * d), BF16)],
        compiler_params=pltpu.CompilerParams(
            dimension_semantics=("arbitrary",), vmem_limit_bytes=VMEM_LIMIT),
        name="merge",
    )(x2, onsa, osgu, omem, wg, wbn, wbs, wbm, wo, g1, b1)


MOE_TM = 1024
MOE_CH = 144
_YS_ROWS = -(-(MOE_TM + N_GROUPS * MOE_CH) // 256) * 256
_R_SLOT = 8
_DEST_LANE = 3 * _R_SLOT
_HID = EXPERTS_PER_GROUP * EXPERT_FF


def _moe_kernel(x_ref, tri_ref, wr_ref, br_ref, wg_ref, wu_ref, wd_ref, g_ref, b_ref, out_ref,
                xa_s, tok_s, keyr_s, ys_s, cnt_s, base_s):
    grp = pl.program_id(1)
    tm = x_ref.shape[0]
    d = x_ref.shape[1]

    @pl.when(grp == 0)
    def _route():
        xb = x_ref[...].astype(BF16)
        xa_s[:, 0:d] = xb
        logits = _dot(xb, wr_ref[...]) + br_ref[...]
        lt = logits.T
        row = lax.broadcasted_iota(jnp.int32, (_R_SLOT, tm), 0)
        gl = jnp.where(row < N_GROUPS, lt[0:_R_SLOT], NEG)
        gmax = jnp.max(gl, axis=0, keepdims=True)
        gidx = jnp.min(jnp.where(gl == gmax, row, _R_SLOT), axis=0, keepdims=True)
        gprob = 1.0 / jnp.sum(jnp.exp(gl - gmax), axis=0, keepdims=True)
        el = lt[_R_SLOT:2 * _R_SLOT]
        for k in range(1, N_GROUPS):
            el = jnp.where(gidx == k, lt[(k + 1) * _R_SLOT:(k + 2) * _R_SLOT], el)
        ee = jnp.exp(el - jnp.max(el, axis=0, keepdims=True))
        ep = ee / jnp.sum(ee, axis=0, keepdims=True)
        t1 = jnp.max(ep, axis=0, keepdims=True)
        i1 = jnp.min(jnp.where(ep == t1, row, _R_SLOT), axis=0, keepdims=True)
        rest = row != i1
        t2 = jnp.max(jnp.where(rest, ep, -1.0), axis=0, keepdims=True)
        i2 = jnp.min(jnp.where(rest & (ep == t2), row, _R_SLOT), axis=0, keepdims=True)
        den = t1 + t2
        cwf = (jnp.where(row == i1, t1 / den, 0.0) + jnp.where(row == i2, t2 / den, 0.0)) * gprob
        hi = cwf.astype(BF16).astype(F32)
        r1 = cwf - hi
        lo = r1.astype(BF16).astype(F32)
        lo2 = (r1 - lo).astype(BF16).astype(F32)

        onehot = jnp.where(row == gidx, 1.0, 0.0)
        rank = _dot_nt(onehot.astype(BF16), tri_ref[...])
        keyr_s[...] = jnp.where(onehot > 0.5, rank, -1.0)
        base = jnp.int32(0)
        basev = jnp.zeros((1, tm), F32)
        for k in range(N_GROUPS):
            n_k = jnp.sum(onehot[k:k + 1, :]).astype(jnp.int32)
            cnt_s[k] = n_k
            base_s[k] = base
            basev = jnp.where(gidx == k, base.astype(F32), basev)
            base = base + ((n_k + MOE_CH - 1) // MOE_CH) * MOE_CH
        dest = jnp.sum(rank * onehot, axis=0, keepdims=True) + basev
        tok = jnp.concatenate(
            [hi, lo, lo2, jnp.broadcast_to(dest, (_R_SLOT, tm)),
             jnp.zeros((LANES - 4 * _R_SLOT, tm), F32)], axis=0).T
        tok_s[...] = tok
        lane = lax.broadcasted_iota(jnp.int32, tok.shape, 1)
        xa_s[:, d:d + LANES] = jnp.where(lane < _DEST_LANE, tok, 0.0).astype(BF16)
        ys_s[...] = jnp.zeros(ys_s.shape, BF16)

    n_rows = cnt_s[grp]
    row0 = base_s[grp]
    keyr = keyr_s[pl.ds(grp, 1), :]
    half = _HID // 2

    def sweep(k, carry):
        ch = MOE_CH
        r_row = (lax.broadcasted_iota(jnp.int32, (ch, tm), 0) + k * MOE_CH).astype(F32)
        pick = jnp.where(keyr == r_row, 1.0, 0.0).astype(BF16)
        ga = _dot(pick, xa_s[...])
        xg = ga[:, 0:d].astype(BF16)
        cwg = ga[:, d:d + LANES]
        cs = (cwg + pltpu.roll(cwg, LANES - _R_SLOT, 1)
              + pltpu.roll(cwg, LANES - 2 * _R_SLOT, 1))
        hg = [_dot(xg, wg_ref[0, :, h * half:(h + 1) * half]) for h in range(2)]
        hu = [_dot(xg, wu_ref[0, :, h * half:(h + 1) * half]) for h in range(2)]
        y = None
        for h in range(2):
            per_half = EXPERTS_PER_GROUP // 2
            cexp = jnp.concatenate(
                [jnp.broadcast_to(cs[:, e:e + 1], (ch, EXPERT_FF))
                 for e in range(h * per_half, (h + 1) * per_half)], axis=1)
            hid = ((hg[h] * _sigmoid(hg[h])) * hu[h]) * cexp
            term = _dot(hid.astype(BF16), wd_ref[0, h * half:(h + 1) * half, :])
            y = term if y is None else y + term
        ys_s[pl.ds(pl.multiple_of(row0 + k * MOE_CH, 16), ch), :] = y.astype(BF16)
        return carry

    lax.fori_loop(0, (n_rows + MOE_CH - 1) // MOE_CH, sweep, 0)

    @pl.when(grp == N_GROUPS - 1)
    def _fin():
        n_q = 4
        q_rows = tm // n_q
        r_col = lax.broadcasted_iota(jnp.int32, (q_rows, _YS_ROWS), 1).astype(F32)
        parts = [slice(i * q_rows, (i + 1) * q_rows) for i in range(n_q)]
        puts = [jnp.where(tok_s[rows, _DEST_LANE:_DEST_LANE + 1] == r_col, 1.0, 0.0).astype(BF16)
                for rows in parts]
        fs = [_dot(put, ys_s[...]) for put in puts]
        for rows, f in zip(parts, fs):
            out_ref[rows, :] = _layer_norm(DN_ALPHA * x_ref[rows, :] + f, g_ref[...], b_ref[...])


def _moe_call(x1, wr, br, wg, wu, wd, g2, b2):
    t, d = x1.shape
    row = lambda i, g: (i, 0)
    c2 = lambda i, g: (0, 0)
    idx = np.arange(MOE_TM)
    tri = jnp.asarray(idx[None, :] < idx[:, None], dtype=BF16)
    return pl.pallas_call(
        _moe_kernel,
        grid=(t // MOE_TM, N_GROUPS),
        in_specs=[
            pl.BlockSpec((MOE_TM, d), row, pipeline_mode=pl.Buffered(1)),
            pl.BlockSpec((MOE_TM, MOE_TM), c2, pipeline_mode=pl.Buffered(1)),
            pl.BlockSpec(wr.shape, c2),
            pl.BlockSpec(br.shape, c2),
            pl.BlockSpec((1, d, _HID), lambda i, g: (g, 0, 0)),
            pl.BlockSpec((1, d, _HID), lambda i, g: (g, 0, 0)),
            pl.BlockSpec((1, _HID, d), lambda i, g: (g, 0, 0)),
            pl.BlockSpec(g2.shape, c2),
            pl.BlockSpec(b2.shape, c2),
        ],
        out_specs=pl.BlockSpec((MOE_TM, d), row),
        out_shape=jax.ShapeDtypeStruct((t, d), F32),
        scratch_shapes=[
            pltpu.VMEM((MOE_TM, d + LANES), BF16),
            pltpu.VMEM((MOE_TM, LANES), F32),
            pltpu.VMEM((_R_SLOT, MOE_TM), F32),
            pltpu.VMEM((_YS_ROWS, d), BF16),
            pltpu.SMEM((N_GROUPS,), jnp.int32),
            pltpu.SMEM((N_GROUPS,), jnp.int32),
        ],
        compiler_params=pltpu.CompilerParams(
            dimension_semantics=("parallel", "arbitrary"), vmem_limit_bytes=MOE_VMEM_LIMIT),
        name="moe",
    )(x1, tri, wr, br, wg, wu, wd, g2, b2)


def _rope_table(seq):
    half = ROPE_DIM // 2
    inv = ROPE_THETA ** (-jnp.arange(0, ROPE_DIM, 2, dtype=F32) / ROPE_DIM)
    ang = jnp.arange(seq, dtype=F32)[:, None] * inv[None, :]
    cos, sin = jnp.cos(ang), jnp.sin(ang)
    rest = HEAD_DIM - ROPE_DIM
    one = jnp.ones((seq, rest), F32)
    zero = jnp.zeros((seq, rest), F32)
    zh = jnp.zeros((seq, half), F32)
    c_h = jnp.concatenate([cos, cos, one], axis=1)
    dn_h = jnp.concatenate([-sin, zh, zero], axis=1)
    up_h = jnp.concatenate([zh, sin, zero], axis=1)
    rep = LANES // HEAD_DIM
    return jnp.concatenate([jnp.tile(c_h, (1, rep)), jnp.tile(dn_h, (1, rep)), jnp.tile(up_h, (1, rep))],
                           axis=1)


def _expand_cmp_weights(pe, w1, w2):
    g_n = NSA_KV_GROUPS
    w1r = w1.astype(BF16).reshape(2, CMP_STRIDE, HEAD_DIM, CMP_HIDDEN)
    w2b = w2.astype(BF16)
    w1e = jnp.concatenate(
        [jnp.stack([w1r if g == e else jnp.zeros_like(w1r) for g in range(g_n)], axis=2)
         .reshape(2, A_W, CMP_HIDDEN) for e in range(g_n)], axis=2)
    w2e = jnp.concatenate(
        [jnp.concatenate([w2b if g == e else jnp.zeros_like(w2b) for e in range(g_n)], axis=1)
         for g in range(g_n)], axis=0)
    per = pe.reshape(2, CMP_STRIDE, 1, HEAD_DIM)
    pee = jnp.broadcast_to(per, (2, CMP_STRIDE, g_n, HEAD_DIM)).reshape(2, A_W)
    pee = jnp.concatenate([pee, jnp.zeros((6, A_W), pe.dtype)], axis=0)
    return pee.astype(F32), w1e.astype(BF16), w2e.astype(BF16)


def kernel(x, mem, w_in, cmp_pe_k, cmp_w1_k, cmp_w2_k, cmp_pe_v, cmp_w1_v, cmp_w2_v, sgu_ln_g, sgu_ln_b,
           sgu_w_s, sgu_b_s, w_mem_kv, w_br_nsa, w_br_sgu, w_br_mem, w_o, ln1_g, ln1_b, w_router_group,
           b_router_group, w_router_expert, b_router_expert, w_exp_gate, w_exp_up, w_exp_down, ln2_g, ln2_b):
    batch, seq, d = x.shape
    t = batch * seq
    assert w_in.shape[0] == DEPTH == 1
    assert seq % PROJ_TM == 0 and seq // CMP_STRIDE == N_A and seq // SLC_BLOCK == N_SEL

    offs = [int(v) for v in np.cumsum(
        [0, Q_W, KV_W, KV_W, KV_W, KV_W, KV_W, KV_W, GATE_W, 2 * SGU_WIDTH, MEM_WIDTH, 3 * d])]
    assert offs[-1] == w_in.shape[2]
    w_t = jnp.transpose(w_in[0])
    seg = lambda i: w_t[offs[i]:offs[i + 1]]
    gate_rows = jnp.pad(seg(7), ((0, LANES - GATE_W), (0, 0)))
    wp_t = jnp.concatenate([seg(0), seg(1), seg(3), seg(5), seg(2), seg(4), seg(6), gate_rows, seg(8), seg(9)],
                           axis=0).astype(BF16)
    w_mt = seg(10).astype(BF16)
    assert wp_t.shape[0] == _PROJ_COLS
    rope = _rope_table(seq)
    tril = jnp.tril(jnp.ones((SGU_CHUNK, SGU_CHUNK), dtype=bool))
    ws = jnp.where(tril[None], sgu_w_s[0], 0.0).astype(BF16)
    bs = jnp.repeat(sgu_b_s[0].T, SGU_WIDTH // SGU_GROUPS, axis=1)

    x2 = x.reshape(t, d)
    (qpad, kc, ksl, kwn, vc, vslt, vwnt, gt, osgu, mq) = _proj_call(
        x2, wp_t, rope, sgu_ln_g[0][None], sgu_ln_b[0][None], ws, bs, seq)

    pek, w1k, w2k = _expand_cmp_weights(cmp_pe_k[0], cmp_w1_k[0], cmp_w2_k[0])
    pev, w1v, w2v = _expand_cmp_weights(cmp_pe_v[0], cmp_w1_v[0], cmp_w2_v[0])
    kcmp, vcmp = _compress_call(kc, vc, pek, pev, w1k, w2k, w1v, w2v, batch)

    nqb = seq // Q_BLOCK
    n_g = NSA_KV_GROUPS
    vcmpt = vcmp.reshape(batch, N_A, KV_W).transpose(0, 2, 1).reshape(batch * KV_W, N_A)
    ci = np.arange(N_A)
    sj = np.arange(N_SEL)
    overlap = ((ci[None, :] * CMP_STRIDE + CMP_LEN - 1 >= sj[:, None] * SLC_BLOCK)
               & (ci[None, :] * CMP_STRIDE <= sj[:, None] * SLC_BLOCK + SLC_BLOCK - 1)
               & (ci[None, :] < (seq - CMP_LEN) // CMP_STRIDE + 1))
    ot = jnp.asarray(overlap, dtype=BF16)
    onsa = _nsa_call(qpad, ksl, kwn, vslt, vwnt, kcmp, vcmpt, gt, ot, batch, seq)

    omem = _memattn_call(mq, mem.reshape(batch * mem.shape[1], d), w_mem_kv[0], batch, seq)

    x1 = _merge_call(x2, onsa, osgu, omem, w_mt, w_br_nsa[0], w_br_sgu[0], w_br_mem[0], w_o[0],
                     ln1_g[0][None], ln1_b[0][None])

    assert EXPERTS_PER_GROUP == _R_SLOT and N_GROUPS <= _R_SLOT
    n_r = _R_SLOT + N_GROUPS * EXPERTS_PER_GROUP
    wr = jnp.concatenate([jnp.pad(w_router_group[0], ((0, 0), (0, _R_SLOT - N_GROUPS))),
                          w_router_expert[0]], axis=1)
    wr = jnp.pad(wr, ((0, 0), (0, LANES - n_r))).astype(BF16)
    br = jnp.concatenate([jnp.pad(b_router_group[0], (0, _R_SLOT - N_GROUPS)), b_router_expert[0]])
    br = jnp.pad(br, (0, LANES - n_r))[None]
    wg = w_exp_gate[0].transpose(0, 2, 1, 3).reshape(N_GROUPS, d, _HID).astype(BF16)
    wu = w_exp_up[0].transpose(0, 2, 1, 3).reshape(N_GROUPS, d, _HID).astype(BF16)
    wd = w_exp_down[0].reshape(N_GROUPS, _HID, d).astype(BF16)
    out = _moe_call(x1, wr, br, wg, wu, wd, ln2_g[0][None], ln2_b[0][None])
    return out.reshape(batch, seq, d)
```

```python
import functools

import numpy as np
import jax
import jax.numpy as jnp
from jax import lax
from jax.experimental import pallas as pl
from jax.experimental.pallas import tpu as pltpu

NSA_HEADS = 8
NSA_KV_GROUPS = 2
NSA_HPG = NSA_HEADS // NSA_KV_GROUPS
HEAD_DIM = 64
CMP_LEN = 32
CMP_STRIDE = 16
CMP_HIDDEN = 256
SLC_BLOCK = 64
SLC_TOPN = 8
WINDOW = 512
Q_BLOCK = 128
N_BAND = WINDOW // Q_BLOCK
ROPE_THETA = 500000.0
ROPE_DIM = HEAD_DIM // 4
SGU_CHUNK = 128
SGU_GROUPS = 8
SGU_WIDTH = 512
MEM_HEADS = 4
MEM_HEAD_DIM = 128
MEM_WIDTH = MEM_HEADS * MEM_HEAD_DIM
N_GROUPS = 4
EXPERTS_PER_GROUP = 8
EXPERT_FF = 256
DEPTH = 1
DN_ALPHA = (2.0 * DEPTH) ** 0.25
LN_EPS = 1e-5
NEG = -1e30
LOG2E = 1.4426950408889634

LANES = 128
Q_W = NSA_HEADS * HEAD_DIM
KV_W = NSA_KV_GROUPS * HEAD_DIM
GATE_W = NSA_HEADS * 3
VMEM_LIMIT = 56 * 1024 * 1024
MOE_VMEM_LIMIT = 60 * 1024 * 1024

BF16 = jnp.bfloat16
F32 = jnp.float32


def _dot(a, b):
    return jnp.dot(a, b, preferred_element_type=F32)


def _dot_nt(a, b):
    return lax.dot_general(a, b, (((1,), (1,)), ((), ())), preferred_element_type=F32)


def _sigmoid(x):
    return 1.0 / (1.0 + jnp.exp(-x))


def _gelu(x):
    return 0.5 * x * (1.0 + lax.erf(x * (2.0 ** -0.5)))


def _layer_norm(x, g, b):
    mu = jnp.mean(x, axis=-1, keepdims=True)
    xc = x - mu
    var = jnp.mean(xc * xc, axis=-1, keepdims=True)
    return xc * lax.rsqrt(var + LN_EPS) * g + b


PROJ_TM = 512
_ROPE_COLS = Q_W + 3 * KV_W
_V_OFF = _ROPE_COLS
_G_OFF = _V_OFF + 3 * KV_W
_SGU_OFF = _G_OFF + LANES
_MQ_OFF = _SGU_OFF + 2 * SGU_WIDTH
_PROJ_COLS = _MQ_OFF + MEM_WIDTH


def _store_strided_rows(val, out_ref, tmp_ref):
    tmp_ref[...] = val
    n = val.shape[0] // CMP_STRIDE
    for l in range(CMP_STRIDE):
        out_ref[:, l * LANES:(l + 1) * LANES] = tmp_ref[pl.ds(l, n, stride=CMP_STRIDE), :].astype(BF16)


def _transpose_weight(wt_ref, w_s):
    for j in range(wt_ref.shape[0] // LANES):
        rows = slice(j * LANES, (j + 1) * LANES)
        w_s[:, rows] = wt_ref[rows, :].astype(F32).T.astype(BF16)


def _proj_kernel(x_ref, wt_ref, rope_ref, lng_ref, lnb_ref, ws_ref, bs_ref,
                 q_ref, kc_ref, ksl_ref, kwn_ref, vc_ref, vsl_ref, vwn_ref, gate_ref, osgu_ref, mq_ref,
                 a_s, w_ref):
    pl.when(pl.program_id(0) == 0)(functools.partial(_transpose_weight, wt_ref, w_ref))
    xb = x_ref[...].astype(BF16)
    lane = lax.broadcasted_iota(jnp.int32, (PROJ_TM, LANES), 1)
    low = lane < HEAD_DIM
    cos = rope_ref[:, 0:LANES]
    s_dn = rope_ref[:, LANES:2 * LANES]
    s_up = rope_ref[:, 2 * LANES:3 * LANES]

    hz = _dot(xb, w_ref[:, _SGU_OFF:_MQ_OFF])
    h = _dot(xb, w_ref[:, 0:_ROPE_COLS])
    k_refs = (kc_ref, ksl_ref, kwn_ref)
    for j in range(_ROPE_COLS // LANES):
        blk = h[:, j * LANES:(j + 1) * LANES]
        r = (blk * cos + pltpu.roll(blk, LANES - ROPE_DIM // 2, 1) * s_dn
             + pltpu.roll(blk, ROPE_DIM // 2, 1) * s_up)
        if j < Q_W // LANES:
            r = r * (HEAD_DIM ** -0.5 * LOG2E)
            sw = pltpu.roll(r, HEAD_DIM, 1)
            g = (2 * j) // NSA_HPG
            if g == 0:
                h0 = jnp.where(low, r, 0.0)
                h1 = jnp.where(low, sw, 0.0)
            else:
                h0 = jnp.where(low, 0.0, sw)
                h1 = jnp.where(low, 0.0, r)
            q_ref[:, (2 * j) * LANES:(2 * j + 1) * LANES] = h0.astype(BF16)
            q_ref[:, (2 * j + 1) * LANES:(2 * j + 2) * LANES] = h1.astype(BF16)
        else:
            if j == Q_W // LANES:
                _store_strided_rows(r, kc_ref, a_s)
            else:
                k_refs[j - Q_W // LANES][...] = r.astype(BF16)

    _store_strided_rows(_dot(xb, w_ref[:, _V_OFF:_V_OFF + LANES]), vc_ref, a_s)
    ones = jnp.ones((V_ROWS - HEAD_DIM, PROJ_TM), BF16)
    for v_ref, row0, blk in ((vsl_ref, _V_OFF + LANES, SEL_CHUNK), (vwn_ref, _V_OFF + 2 * LANES, Q_BLOCK)):
        vt = _dot_nt(wt_ref[row0:row0 + LANES, :], xb).astype(BF16)
        for g in range(NSA_KV_GROUPS):
            for b in range(PROJ_TM // blk):
                v_ref[g, b, 0:HEAD_DIM, :] = vt[g * HEAD_DIM:(g + 1) * HEAD_DIM, b * blk:(b + 1) * blk]
                v_ref[g, b, HEAD_DIM:V_ROWS, :] = ones[:, 0:blk]

    gate_ref[...] = _sigmoid(_dot_nt(wt_ref[_G_OFF:_SGU_OFF, :], xb))

    mq_ref[...] = _dot(xb, w_ref[:, _MQ_OFF:_PROJ_COLS]).astype(BF16)

    z = _gelu(hz)
    u = z[:, 0:SGU_WIDTH]
    v = _layer_norm(z[:, SGU_WIDTH:2 * SGU_WIDTH], lng_ref[...], lnb_ref[...]).astype(BF16)
    lane_c = lax.broadcasted_iota(jnp.int32, (SGU_CHUNK, LANES), 1)
    low_c = lane_c < (SGU_WIDTH // SGU_GROUPS)
    n_ch = PROJ_TM // SGU_CHUNK
    for gp in range(SGU_WIDTH // LANES):
        cols = slice(gp * LANES, (gp + 1) * LANES)
        vcat = jnp.concatenate([v[ci * SGU_CHUNK:(ci + 1) * SGU_CHUNK, cols] for ci in range(n_ch)], axis=1)
        sv0 = _dot(ws_ref[2 * gp], vcat)
        sv1 = _dot(ws_ref[2 * gp + 1], vcat)
        for ci in range(n_ch):
            rows = slice(ci * SGU_CHUNK, (ci + 1) * SGU_CHUNK)
            lanes = slice(ci * LANES, (ci + 1) * LANES)
            sv = jnp.where(low_c, sv0[:, lanes], sv1[:, lanes]) + bs_ref[:, cols]
            osgu_ref[rows, cols] = (u[rows, cols] * sv).astype(BF16)


def _proj_call(x2, wp, rope, lng, lnb, ws, bs, seq):
    t = x2.shape[0]
    d = x2.shape[1]
    nt = t // PROJ_TM
    per_seq = seq // PROJ_TM
    row = lambda i: (i, 0)
    const2 = lambda i: (0, 0)
    assert PROJ_TM % SEL_CHUNK == 0 and PROJ_TM % Q_BLOCK == 0
    n_g = NSA_KV_GROUPS
    batch = t // seq
    vblock = lambda i: (i // per_seq, i % per_seq, 0, 0)

    def v_shape(blk):
        return jax.ShapeDtypeStruct((batch * n_g, seq // blk, V_ROWS, blk), BF16)

    def v_spec(blk):
        return pl.BlockSpec((n_g, PROJ_TM // blk, V_ROWS, blk), vblock)

    a_shape = jax.ShapeDtypeStruct((t // CMP_STRIDE, CMP_STRIDE * KV_W), BF16)
    a_spec = pl.BlockSpec((PROJ_TM // CMP_STRIDE, CMP_STRIDE * KV_W), row)
    k_shape = jax.ShapeDtypeStruct((t, KV_W), BF16)
    k_spec = pl.BlockSpec((PROJ_TM, KV_W), row)
    out_shapes = (
        jax.ShapeDtypeStruct((t, NSA_HEADS * LANES), BF16),
        a_shape, k_shape, k_shape, a_shape,
        v_shape(SEL_CHUNK), v_shape(Q_BLOCK),
        jax.ShapeDtypeStruct((LANES, t), F32),
        jax.ShapeDtypeStruct((t, SGU_WIDTH), BF16),
        jax.ShapeDtypeStruct((t, MEM_WIDTH), BF16),
    )
    out_specs = (
        pl.BlockSpec((PROJ_TM, NSA_HEADS * LANES), row),
        a_spec, k_spec, k_spec, a_spec,
        v_spec(SEL_CHUNK), v_spec(Q_BLOCK),
        pl.BlockSpec((LANES, PROJ_TM), lambda i: (0, i)),
        pl.BlockSpec((PROJ_TM, SGU_WIDTH), row),
        pl.BlockSpec((PROJ_TM, MEM_WIDTH), row),
    )
    return pl.pallas_call(
        _proj_kernel,
        grid=(nt,),
        in_specs=[
            pl.BlockSpec((PROJ_TM, d), row),
            pl.BlockSpec((_PROJ_COLS, d), const2),
            pl.BlockSpec((PROJ_TM, 3 * LANES), lambda i: (i % per_seq, 0)),
            pl.BlockSpec((1, SGU_WIDTH), const2),
            pl.BlockSpec((1, SGU_WIDTH), const2),
            pl.BlockSpec((SGU_GROUPS, SGU_CHUNK, SGU_CHUNK), lambda i: (0, 0, 0)),
            pl.BlockSpec((SGU_CHUNK, SGU_WIDTH), const2),
        ],
        out_specs=out_specs,
        out_shape=out_shapes,
        scratch_shapes=[pltpu.VMEM((PROJ_TM, KV_W), F32), pltpu.VMEM((d, _PROJ_COLS), BF16)],
        compiler_params=pltpu.CompilerParams(
            dimension_semantics=("arbitrary",), vmem_limit_bytes=VMEM_LIMIT),
        name="proj",
    )(x2, wp, rope, lng, lnb, ws, bs)


N_A = 128
A_W = CMP_STRIDE * KV_W


def _compress_kernel(ka_ref, va_ref, pek_ref, pev_ref, w1k_ref, w2k_ref, w1v_ref, w2v_ref,
                     kcmp_ref, vcmp_ref):
    def one(a_ref, pe_ref, w1_ref, w2_ref, out_ref):
        a = a_ref[...].astype(F32)
        top = (a + pe_ref[0:1, :]).astype(BF16)
        bot = (a + pe_ref[1:2, :]).astype(BF16)
        h1 = _dot(top, w1_ref[0])
        h2 = _dot(bot, w1_ref[1])
        pre = h1 + pltpu.roll(h2, N_A - 1, 0)
        act = _gelu(pre).astype(BF16)
        out_ref[...] = _dot(act, w2_ref[...]).astype(BF16)

    one(ka_ref, pek_ref, w1k_ref, w2k_ref, kcmp_ref)
    one(va_ref, pev_ref, w1v_ref, w2v_ref, vcmp_ref)


def _compress_call(ka, va, pek, pev, w1k, w2k, w1v, w2v, batch):
    row = lambda b: (b, 0)
    c2 = lambda b: (0, 0)
    c3 = lambda b: (0, 0, 0)
    hid2 = NSA_KV_GROUPS * CMP_HIDDEN
    return pl.pallas_call(
        _compress_kernel,
        grid=(batch,),
        in_specs=[
            pl.BlockSpec((N_A, A_W), row),
            pl.BlockSpec((N_A, A_W), row),
            pl.BlockSpec((8, A_W), c2),
            pl.BlockSpec((8, A_W), c2),
            pl.BlockSpec((2, A_W, hid2), c3),
            pl.BlockSpec((hid2, KV_W), c2),
            pl.BlockSpec((2, A_W, hid2), c3),
            pl.BlockSpec((hid2, KV_W), c2),
        ],
        out_specs=(pl.BlockSpec((N_A, KV_W), row), pl.BlockSpec((N_A, KV_W), row)),
        out_shape=(jax.ShapeDtypeStruct((batch * N_A, KV_W), BF16),
                   jax.ShapeDtypeStruct((batch * N_A, KV_W), BF16)),
        compiler_params=pltpu.CompilerParams(
            dimension_semantics=("parallel",), vmem_limit_bytes=VMEM_LIMIT),
        name="compress",
    )(ka, va, pek, pev, w1k, w2k, w1v, w2v)


N_SEL = 32
SEL_CHUNK = 512
SEL_SUB = 128
HQ = NSA_HPG * Q_BLOCK
V_ROWS = HEAD_DIM + 16


def _nsa_kernel(*refs):
    ksl_ref = refs[1]
    reach = pl.program_id(1) // (SEL_CHUNK // Q_BLOCK)
    for n in range(ksl_ref.shape[0] // SEL_CHUNK):
        pl.when(reach == n)(functools.partial(_nsa_body, n, *refs))


def _nsa_body(n_later, q_ref, ksl_ref, kwn_ref, vslt_ref, vwnt_ref, kcmp_ref, vcmpt_ref, gt_ref,
              ot_ref, o_ref, m_s, acc_s, ocmp_s, owin_s, cap_s, out_s):
    c = pl.program_id(1)
    lane_hq = lax.broadcasted_iota(jnp.int32, (1, HQ), 1)
    pos_hq = c * Q_BLOCK + (lane_hq & (Q_BLOCK - 1))
    pos_q = c * Q_BLOCK + lax.broadcasted_iota(jnp.int32, (1, Q_BLOCK), 1)
    groups = range(NSA_KV_GROUPS)
    qgs = [jnp.concatenate(
        [q_ref[:, (g * NSA_HPG + hh) * LANES:(g * NSA_HPG + hh + 1) * LANES]
         for hh in range(NSA_HPG)], axis=0) for g in groups]

    kcmp = kcmp_ref[...]
    s_c = [_dot_nt(kcmp, qgs[g]) for g in groups]
    n_idx = lax.broadcasted_iota(jnp.int32, (N_A, HQ), 0)
    valid_c = (n_idx * CMP_STRIDE + (CMP_LEN - 1)) <= pos_hq
    p_c = []
    for g in groups:
        sm_c = jnp.where(valid_c, s_c[g], NEG)
        m_c = jnp.max(sm_c, axis=0, keepdims=True)
        e_c = jnp.where(valid_c, jnp.exp2(sm_c - m_c), 0.0)
        d_c = jnp.sum(e_c, axis=0, keepdims=True)
        p_c.append(e_c / jnp.where(d_c > 0, d_c, 1.0))
    for g in groups:
        ocmp_s[g] = _dot(vcmpt_ref[g * HEAD_DIM:(g + 1) * HEAD_DIM, :], p_c[g].astype(BF16))

    ot = ot_ref[...]
    imp = []
    for g in groups:
        ps = (p_c[g][:, 0:Q_BLOCK] + p_c[g][:, Q_BLOCK:2 * Q_BLOCK]
              + p_c[g][:, 2 * Q_BLOCK:3 * Q_BLOCK] + p_c[g][:, 3 * Q_BLOCK:4 * Q_BLOCK])
        p_hi = ps.astype(BF16)
        r1 = ps - p_hi.astype(F32)
        p_lo = r1.astype(BF16)
        p_lo2 = (r1 - p_lo.astype(F32)).astype(BF16)
        imp.append(_dot(ot, p_hi) + _dot(ot, p_lo) + _dot(ot, p_lo2))

    j_idx = lax.broadcasted_iota(jnp.int32, (N_SEL, Q_BLOCK), 0)
    cur = pos_q // SLC_BLOCK
    future = j_idx > cur
    forced = (j_idx == 0) | (j_idx == cur) | (j_idx == cur - 1)
    imp = [jnp.where(future, NEG, jnp.where(forced, -NEG, imp[g])) for g in groups]
    rank = [jnp.zeros((N_SEL, Q_BLOCK), F32) for g in groups]
    for i in range(N_SEL):
        for g in groups:
            row = imp[g][i:i + 1, :]
            beats = (row > imp[g]) | ((row == imp[g]) & (j_idx > i))
            rank[g] = rank[g] + jnp.where(beats, 1.0, 0.0)
    for g in groups:
        cap_s[g] = jnp.where(rank[g] < float(SLC_TOPN), -NEG, NEG)
        m_s[g] = jnp.full((1, HQ), NEG, F32)
        acc_s[g] = jnp.zeros((V_ROWS, HQ), F32)

    blocks_per_sub = SEL_SUB // SLC_BLOCK
    subs_per_chunk = SEL_CHUNK // SEL_SUB

    def sel_spans(kc):
        return [slice(kc * SEL_CHUNK + a * SEL_SUB, kc * SEL_CHUNK + (a + 1) * SEL_SUB)
                for a in range(subs_per_chunk)]

    def sel_scores(kc):
        return [[_dot_nt(ksl_ref[ks, :], qgs[g]) for g in groups] for ks in sel_spans(kc)]

    def sel_chunk(kc, s=None):
        spans = sel_spans(kc)
        s = sel_scores(kc) if s is None else s
        for a, ks in enumerate(spans):
            kpos = ks.start + lax.broadcasted_iota(jnp.int32, (SEL_SUB, Q_BLOCK), 0)
            causal = kpos <= pos_q
            for g in groups:
                j0 = ks.start // SLC_BLOCK
                cap = jnp.concatenate(
                    [jnp.broadcast_to(cap_s[g, j:j + 1, :], (SLC_BLOCK, Q_BLOCK))
                     for j in range(j0, j0 + blocks_per_sub)], axis=0)
                cap = jnp.where(causal, cap, NEG)
                sm = jnp.minimum(s[a][g], jnp.concatenate([cap] * NSA_HPG, axis=1))
                m_old = m_s[g]
                m_new = jnp.maximum(m_old, jnp.max(sm, axis=0, keepdims=True))
                alpha = jnp.exp2(m_old - m_new)
                e = jnp.exp2((sm - m_new).astype(BF16))
                v_t = vslt_ref[g, kc, :, a * SEL_SUB:(a + 1) * SEL_SUB]
                acc_s[g] = alpha * acc_s[g] + _dot(v_t, e)
                m_s[g] = m_new

    s0 = sel_scores(0)
    _nsa_window(c, qgs, kwn_ref, vwnt_ref, owin_s)
    s_next = sel_scores(1) if n_later else None
    sel_chunk(0, s0)
    for kc in range(1, n_later + 1):
        s_cur = s_next
        if kc < n_later:
            s_next = sel_scores(kc + 1)
        sel_chunk(kc, s_cur)

    for g in groups:
        acc = acc_s[g]
        o_sel = acc[0:HEAD_DIM, :] * (1.0 / acc[HEAD_DIM:HEAD_DIM + 1, :])

        def gate_row(br):
            return jnp.concatenate(
                [gt_ref[(g * NSA_HPG + hh) * 3 + br:(g * NSA_HPG + hh) * 3 + br + 1, :]
                 for hh in range(NSA_HPG)], axis=1)
        o_t = gate_row(0) * ocmp_s[g] + gate_row(1) * o_sel + gate_row(2) * owin_s[g]
        for hh in range(NSA_HPG):
            h = g * NSA_HPG + hh
            out_s[h * HEAD_DIM:(h + 1) * HEAD_DIM, :] = o_t[:, hh * Q_BLOCK:(hh + 1) * Q_BLOCK]

    o_ref[...] = out_s[...].T.astype(BF16)


def _nsa_window(c, qgs, kwn_ref, vwnt_ref, owin_s):
    groups = range(NSA_KV_GROUPS)
    q_i = lax.broadcasted_iota(jnp.int32, (Q_BLOCK, Q_BLOCK), 1)
    k_i = lax.broadcasted_iota(jnp.int32, (Q_BLOCK, Q_BLOCK), 0)
    blks, caps = [], []
    for i in range(N_BAND + 1):
        blk = c - N_BAND + i
        blks.append(jnp.maximum(blk, 0))
        off = (N_BAND - i) * Q_BLOCK
        if off - (Q_BLOCK - 1) >= 0 and off + (Q_BLOCK - 1) < WINDOW:
            caps.append(jnp.where(blk >= 0, -NEG, NEG))
        else:
            diff = off + q_i - k_i
            ok = (diff >= 0) & (diff < WINDOW) & (blk >= 0)
            caps.append(jnp.concatenate([jnp.where(ok, -NEG, NEG)] * NSA_HPG, axis=1))
    k_blocks = [kwn_ref[pl.ds(pl.multiple_of(blks[i] * Q_BLOCK, Q_BLOCK), Q_BLOCK), :]
                for i in range(N_BAND + 1)]
    s_w = [[None] * (N_BAND + 1) for g in groups]
    for i in reversed(range(N_BAND + 1)):
        for g in groups:
            s_w[g][i] = _dot_nt(k_blocks[i], qgs[g])
    m_w = [None for g in groups]
    o_win = [None for g in groups]
    for i in reversed(range(N_BAND + 1)):
        for g in groups:
            sm = jnp.minimum(s_w[g][i], caps[i])
            m_blk = jnp.max(sm, axis=0, keepdims=True)
            m_new = m_blk if m_w[g] is None else jnp.maximum(m_w[g], m_blk)
            pv = _dot(vwnt_ref[g, blks[i]], jnp.exp2((sm - m_new).astype(BF16)))
            o_win[g] = pv if m_w[g] is None else jnp.exp2(m_w[g] - m_new) * o_win[g] + pv
            m_w[g] = m_new
    for g in groups:
        owin_s[g] = o_win[g][0:HEAD_DIM, :] * (1.0 / o_win[g][HEAD_DIM:HEAD_DIM + 1, :])


def _nsa_call(qpad, ksl, kwn, vslt, vwnt, kcmp, vcmpt, gt, ot, batch, seq):
    nqb = seq // Q_BLOCK
    n_g = NSA_KV_GROUPS
    qrow = lambda b, c: (b * nqb + c, 0)
    brow = lambda b, c: (b, 0)
    c2 = lambda b, c: (0, 0)
    return pl.pallas_call(
        _nsa_kernel,
        grid=(batch, nqb),
        in_specs=[
            pl.BlockSpec((Q_BLOCK, NSA_HEADS * LANES), qrow),
            pl.BlockSpec((seq, KV_W), brow),
            pl.BlockSpec((seq, KV_W), brow),
            pl.BlockSpec((n_g, seq // SEL_CHUNK, V_ROWS, SEL_CHUNK), lambda b, c: (b, 0, 0, 0)),
            pl.BlockSpec((n_g, nqb, V_ROWS, Q_BLOCK), lambda b, c: (b, 0, 0, 0)),
            pl.BlockSpec((N_A, KV_W), brow),
            pl.BlockSpec((KV_W, N_A), brow),
            pl.BlockSpec((LANES, Q_BLOCK), lambda b, c: (0, b * nqb + c)),
            pl.BlockSpec((N_SEL, N_A), c2),
        ],
        out_specs=pl.BlockSpec((Q_BLOCK, Q_W), qrow),
        out_shape=jax.ShapeDtypeStruct((batch * seq, Q_W), BF16),
        scratch_shapes=[
            pltpu.VMEM((n_g, 1, HQ), F32),
            pltpu.VMEM((n_g, V_ROWS, HQ), F32),
            pltpu.VMEM((n_g, HEAD_DIM, HQ), F32),
            pltpu.VMEM((n_g, HEAD_DIM, HQ), F32),
            pltpu.VMEM((n_g, N_SEL, Q_BLOCK), F32),
            pltpu.VMEM((Q_W, Q_BLOCK), F32),
        ],
        compiler_params=pltpu.CompilerParams(
            dimension_semantics=("parallel", "arbitrary"), vmem_limit_bytes=VMEM_LIMIT),
        name="nsa",
    )(qpad, ksl, kwn, vslt, vwnt, kcmp, vcmpt, gt, ot)


MEM_TM = 512
MEM_V_ROWS = MEM_HEAD_DIM + 16


def _memattn_kernel(mq_ref, mem_ref, wkv_ref, o_ref, k_s, vt_s, out_s):
    @pl.when(pl.program_id(1) == 0)
    def _():
        kv = _dot(mem_ref[...].astype(BF16), wkv_ref[...].astype(BF16))
        k_s[...] = kv[:, 0:MEM_WIDTH].astype(BF16)
        vt = kv[:, MEM_WIDTH:2 * MEM_WIDTH].T.astype(BF16)
        for h in range(MEM_HEADS):
            vt_s[h, 0:MEM_HEAD_DIM, :] = vt[h * MEM_HEAD_DIM:(h + 1) * MEM_HEAD_DIM, :]
            vt_s[h, MEM_HEAD_DIM:, :] = jnp.ones((MEM_V_ROWS - MEM_HEAD_DIM, vt.shape[1]), BF16)

    heads = range(MEM_HEADS)
    cols = [slice(h * MEM_HEAD_DIM, (h + 1) * MEM_HEAD_DIM) for h in heads]
    s_t = [_dot_nt(k_s[:, cols[h]], mq_ref[:, cols[h]]) for h in heads]
    e = []
    for h in heads:
        m = jnp.max(s_t[h], axis=0, keepdims=True)
        e.append(jnp.exp2((s_t[h] - m) * (MEM_HEAD_DIM ** -0.5 * LOG2E)).astype(BF16))
    o_t = [_dot(vt_s[h], e[h]) for h in heads]
    for h in heads:
        out_s[cols[h], :] = o_t[h][0:MEM_HEAD_DIM, :] * (1.0 / o_t[h][MEM_HEAD_DIM:MEM_HEAD_DIM + 1, :])
    o_ref[...] = out_s[...].T.astype(BF16)


def _memattn_call(mq, mem2, wkv, batch, seq):
    m_len = mem2.shape[0] // batch
    d = mem2.shape[1]
    per = seq // MEM_TM
    return pl.pallas_call(
        _memattn_kernel,
        grid=(batch, per),
        in_specs=[
            pl.BlockSpec((MEM_TM, MEM_WIDTH), lambda b, i: (b * per + i, 0)),
            pl.BlockSpec((m_len, d), lambda b, i: (b, 0)),
            pl.BlockSpec((d, 2 * MEM_WIDTH), lambda b, i: (0, 0)),
        ],
        out_specs=pl.BlockSpec((MEM_TM, MEM_WIDTH), lambda b, i: (b * per + i, 0)),
        out_shape=jax.ShapeDtypeStruct((batch * seq, MEM_WIDTH), BF16),
        scratch_shapes=[
            pltpu.VMEM((m_len, MEM_WIDTH), BF16),
            pltpu.VMEM((MEM_HEADS, MEM_V_ROWS, m_len), BF16),
            pltpu.VMEM((MEM_WIDTH, MEM_TM), F32),
        ],
        compiler_params=pltpu.CompilerParams(
            dimension_semantics=("parallel", "arbitrary"), vmem_limit_bytes=VMEM_LIMIT),
        name="memattn",
    )(mq, mem2, wkv)


MERGE_TM = 512


def _merge_kernel(x_ref, onsa_ref, osgu_ref, omem_ref, wgt_ref, wbn32_ref, wbs32_ref, wbm32_ref, wo32_ref,
                  g_ref, b_ref, out_ref, wbn_ref, wbs_ref, wbm_ref, wo_ref, wg_ref):
    @pl.when(pl.program_id(0) == 0)
    def _():
        for dst, src in ((wbn_ref, wbn32_ref), (wbs_ref, wbs32_ref), (wbm_ref, wbm32_ref),
                         (wo_ref, wo32_ref)):
            dst[...] = src[...].astype(BF16)
        _transpose_weight(wgt_ref, wg_ref)

    d = x_ref.shape[1]
    branches = ((onsa_ref, wbn_ref), (osgu_ref, wbs_ref), (omem_ref, wbm_ref))
    halves = [slice(h * (MERGE_TM // 2), (h + 1) * (MERGE_TM // 2)) for h in range(2)]
    xs = [x_ref[rows, :] for rows in halves]
    xbs = [x.astype(BF16) for x in xs]
    logits = [[_dot(xb, wg_ref[:, br * d:(br + 1) * d]) for br in range(len(branches))] for xb in xbs]
    projs = [[_dot(o_r[rows, :], w_r[...]) for o_r, w_r in branches] for rows in halves]
    ys = []
    for h in range(2):
        y = None
        for br in range(len(branches)):
            term = _sigmoid(logits[h][br]) * projs[h][br]
            y = term if y is None else y + term
        ys.append(y.astype(BF16))
    outs = [_dot(y, wo_ref[...]) for y in ys]
    for h, rows in enumerate(halves):
        out_ref[rows, :] = _layer_norm(DN_ALPHA * xs[h] + outs[h], g_ref[...], b_ref[...])


def _merge_call(x2, onsa, osgu, omem, wg, wbn, wbs, wbm, wo, g1, b1):
    t, d = x2.shape
    row = lambda i: (i, 0)
    c2 = lambda i: (0, 0)
    full = lambda a: pl.BlockSpec(a.shape, c2)
    once = lambda a: pl.BlockSpec(a.shape, c2, pipeline_mode=pl.Buffered(1))
    return pl.pallas_call(
        _merge_kernel,
        grid=(t // MERGE_TM,),
        in_specs=[
            pl.BlockSpec((MERGE_TM, d), row),
            pl.BlockSpec((MERGE_TM, onsa.shape[1]), row),
            pl.BlockSpec((MERGE_TM, osgu.shape[1]), row),
            pl.BlockSpec((MERGE_TM, omem.shape[1]), row),
            pl.BlockSpec((3 * d, d), c2, pipeline_mode=pl.Buffered(1)),
            once(wbn), once(wbs), once(wbm), once(wo), full(g1), full(b1),
        ],
        out_specs=pl.BlockSpec((MERGE_TM, d), row),
        out_shape=jax.ShapeDtypeStruct((t, d), F32),
        scratch_shapes=[pltpu.VMEM(a.shape, BF16) for a in (wbn, wbs, wbm, wo)]
        + [pltpu.VMEM((d, 3 * d), BF16)],
        compiler_params=pltpu.CompilerParams(
            dimension_semantics=("arbitrary",), vmem_limit_bytes=VMEM_LIMIT),
        name="merge",
    )(x2, onsa, osgu, omem, wg, wbn, wbs, wbm, wo, g1, b1)


MOE_TM = 1024
MOE_CH = 144
_YS_ROWS = -(-(MOE_TM + N_GROUPS * MOE_CH) // 256) * 256
_R_SLOT = 8
_DEST_LANE = 3 * _R_SLOT
_HID = EXPERTS_PER_GROUP * EXPERT_FF


def _moe_kernel(x_ref, tri_ref, wr_ref, br_ref, wg_ref, wu_ref, wd_ref, g_ref, b_ref, out_ref,
                xa_s, tok_s, keyr_s, ys_s, cnt_s, base_s):
    grp = pl.program_id(1)
    tm = x_ref.shape[0]
    d = x_ref.shape[1]

    @pl.when(grp == 0)
    def _route():
        xb = x_ref[...].astype(BF16)
        xa_s[:, 0:d] = xb
        logits = _dot(xb, wr_ref[...]) + br_ref[...]
        lt = logits.T
        row = lax.broadcasted_iota(jnp.int32, (_R_SLOT, tm), 0)
        gl = jnp.where(row < N_GROUPS, lt[0:_R_SLOT], NEG)
        gmax = jnp.max(gl, axis=0, keepdims=True)
        gidx = jnp.min(jnp.where(gl == gmax, row, _R_SLOT), axis=0, keepdims=True)
        gprob = 1.0 / jnp.sum(jnp.exp(gl - gmax), axis=0, keepdims=True)
        el = lt[_R_SLOT:2 * _R_SLOT]
        for k in range(1, N_GROUPS):
            el = jnp.where(gidx == k, lt[(k + 1) * _R_SLOT:(k + 2) * _R_SLOT], el)
        ee = jnp.exp(el - jnp.max(el, axis=0, keepdims=True))
        ep = ee / jnp.sum(ee, axis=0, keepdims=True)
        t1 = jnp.max(ep, axis=0, keepdims=True)
        i1 = jnp.min(jnp.where(ep == t1, row, _R_SLOT), axis=0, keepdims=True)
        rest = row != i1
        t2 = jnp.max(jnp.where(rest, ep, -1.0), axis=0, keepdims=True)
        i2 = jnp.min(jnp.where(rest & (ep == t2), row, _R_SLOT), axis=0, keepdims=True)
        den = t1 + t2
        cwf = (jnp.where(row == i1, t1 / den, 0.0) + jnp.where(row == i2, t2 / den, 0.0)) * gprob
        hi = cwf.astype(BF16).astype(F32)
        r1 = cwf - hi
        lo = r1.astype(BF16).astype(F32)
        lo2 = (r1 - lo).astype(BF16).astype(F32)

        onehot = jnp.where(row == gidx, 1.0, 0.0)
        rank = _dot_nt(onehot.astype(BF16), tri_ref[...])
        keyr_s[...] = jnp.where(onehot > 0.5, rank, -1.0)
        base = jnp.int32(0)
        basev = jnp.zeros((1, tm), F32)
        for k in range(N_GROUPS):
            n_k = jnp.sum(onehot[k:k + 1, :]).astype(jnp.int32)
            cnt_s[k] = n_k
            base_s[k] = base
            basev = jnp.where(gidx == k, base.astype(F32), basev)
            base = base + ((n_k + MOE_CH - 1) // MOE_CH) * MOE_CH
        dest = jnp.sum(rank * onehot, axis=0, keepdims=True) + basev
        tok = jnp.concatenate(
            [hi, lo, lo2, jnp.broadcast_to(dest, (_R_SLOT, tm)),
             jnp.zeros((LANES - 4 * _R_SLOT, tm), F32)], axis=0).T
        tok_s[...] = tok
        lane = lax.broadcasted_iota(jnp.int32, tok.shape, 1)
        xa_s[:, d:d + LANES] = jnp.where(lane < _DEST_LANE, tok, 0.0).astype(BF16)
        ys_s[...] = jnp.zeros(ys_s.shape, BF16)

    n_rows = cnt_s[grp]
    row0 = base_s[grp]
    keyr = keyr_s[pl.ds(grp, 1), :]
    half = _HID // 2

    def sweep(k, carry):
        ch = MOE_CH
        r_row = (lax.broadcasted_iota(jnp.int32, (ch, tm), 0) + k * MOE_CH).astype(F32)
        pick = jnp.where(keyr == r_row, 1.0, 0.0).astype(BF16)
        ga = _dot(pick, xa_s[...])
        xg = ga[:, 0:d].astype(BF16)
        cwg = ga[:, d:d + LANES]
        cs = (cwg + pltpu.roll(cwg, LANES - _R_SLOT, 1)
              + pltpu.roll(cwg, LANES - 2 * _R_SLOT, 1))
        hg = [_dot(xg, wg_ref[0, :, h * half:(h + 1) * half]) for h in range(2)]
        hu = [_dot(xg, wu_ref[0, :, h * half:(h + 1) * half]) for h in range(2)]
        y = None
        for h in range(2):
            per_half = EXPERTS_PER_GROUP // 2
            cexp = jnp.concatenate(
                [jnp.broadcast_to(cs[:, e:e + 1], (ch, EXPERT_FF))
                 for e in range(h * per_half, (h + 1) * per_half)], axis=1)
            hid = ((hg[h] * _sigmoid(hg[h])) * hu[h]) * cexp
            term = _dot(hid.astype(BF16), wd_ref[0, h * half:(h + 1) * half, :])
            y = term if y is None else y + term
        ys_s[pl.ds(pl.multiple_of(row0 + k * MOE_CH, 16), ch), :] = y.astype(BF16)
        return carry

    lax.fori_loop(0, (n_rows + MOE_CH - 1) // MOE_CH, sweep, 0)

    @pl.when(grp == N_GROUPS - 1)
    def _fin():
        n_q = 4
        q_rows = tm // n_q
        r_col = lax.broadcasted_iota(jnp.int32, (q_rows, _YS_ROWS), 1).astype(F32)
        parts = [slice(i * q_rows, (i + 1) * q_rows) for i in range(n_q)]
        puts = [jnp.where(tok_s[rows, _DEST_LANE:_DEST_LANE + 1] == r_col, 1.0, 0.0).astype(BF16)
                for rows in parts]
        fs = [_dot(put, ys_s[...]) for put in puts]
        for rows, f in zip(parts, fs):
            out_ref[rows, :] = _layer_norm(DN_ALPHA * x_ref[rows, :] + f, g_ref[...], b_ref[...])


def _moe_call(x1, wr, br, wg, wu, wd, g2, b2):
    t, d = x1.shape
    row = lambda i, g: (i, 0)
    c2 = lambda i, g: (0, 0)
    idx = np.arange(MOE_TM)
    tri = jnp.asarray(idx[None, :] < idx[:, None], dtype=BF16)
    return pl.pallas_call(
        _moe_kernel,
        grid=(t // MOE_TM, N_GROUPS),
        in_specs=[
            pl.BlockSpec((MOE_TM, d), row, pipeline_mode=pl.Buffered(1)),
            pl.BlockSpec((MOE_TM, MOE_TM), c2, pipeline_mode=pl.Buffered(1)),
            pl.BlockSpec(wr.shape, c2),
            pl.BlockSpec(br.shape, c2),
            pl.BlockSpec((1, d, _HID), lambda i, g: (g, 0, 0)),
            pl.BlockSpec((1, d, _HID), lambda i, g: (g, 0, 0)),
            pl.BlockSpec((1, _HID, d), lambda i, g: (g, 0, 0)),
            pl.BlockSpec(g2.shape, c2),
            pl.BlockSpec(b2.shape, c2),
        ],
        out_specs=pl.BlockSpec((MOE_TM, d), row),
        out_shape=jax.ShapeDtypeStruct((t, d), F32),
        scratch_shapes=[
            pltpu.VMEM((MOE_TM, d + LANES), BF16),
            pltpu.VMEM((MOE_TM, LANES), F32),
            pltpu.VMEM((_R_SLOT, MOE_TM), F32),
            pltpu.VMEM((_YS_ROWS, d), BF16),
            pltpu.SMEM((N_GROUPS,), jnp.int32),
            pltpu.SMEM((N_GROUPS,), jnp.int32),
        ],
        compiler_params=pltpu.CompilerParams(
            dimension_semantics=("parallel", "arbitrary"), vmem_limit_bytes=MOE_VMEM_LIMIT),
        name="moe",
    )(x1, tri, wr, br, wg, wu, wd, g2, b2)


def _rope_table(seq):
    half = ROPE_DIM // 2
    inv = ROPE_THETA ** (-jnp.arange(0, ROPE_DIM, 2, dtype=F32) / ROPE_DIM)
    ang = jnp.arange(seq, dtype=F32)[:, None] * inv[None, :]
    cos, sin = jnp.cos(ang), jnp.sin(ang)
    rest = HEAD_DIM - ROPE_DIM
    one = jnp.ones((seq, rest), F32)
    zero = jnp.zeros((seq, rest), F32)
    zh = jnp.zeros((seq, half), F32)
    c_h = jnp.concatenate([cos, cos, one], axis=1)
    dn_h = jnp.concatenate([-sin, zh, zero], axis=1)
    up_h = jnp.concatenate([zh, sin, zero], axis=1)
    rep = LANES // HEAD_DIM
    return jnp.concatenate([jnp.tile(c_h, (1, rep)), jnp.tile(dn_h, (1, rep)), jnp.tile(up_h, (1, rep))],
                           axis=1)


def _expand_cmp_weights(pe, w1, w2):
    g_n = NSA_KV_GROUPS
    w1r = w1.astype(BF16).reshape(2, CMP_STRIDE, HEAD_DIM, CMP_HIDDEN)
    w2b = w2.astype(BF16)
    w1e = jnp.concatenate(
        [jnp.stack([w1r if g == e else jnp.zeros_like(w1r) for g in range(g_n)], axis=2)
         .reshape(2, A_W, CMP_HIDDEN) for e in range(g_n)], axis=2)
    w2e = jnp.concatenate(
        [jnp.concatenate([w2b if g == e else jnp.zeros_like(w2b) for e in range(g_n)], axis=1)
         for g in range(g_n)], axis=0)
    per = pe.reshape(2, CMP_STRIDE, 1, HEAD_DIM)
    pee = jnp.broadcast_to(per, (2, CMP_STRIDE, g_n, HEAD_DIM)).reshape(2, A_W)
    pee = jnp.concatenate([pee, jnp.zeros((6, A_W), pe.dtype)], axis=0)
    return pee.astype(F32), w1e.astype(BF16), w2e.astype(BF16)


def kernel(x, mem, w_in, cmp_pe_k, cmp_w1_k, cmp_w2_k, cmp_pe_v, cmp_w1_v, cmp_w2_v, sgu_ln_g, sgu_ln_b,
           sgu_w_s, sgu_b_s, w_mem_kv, w_br_nsa, w_br_sgu, w_br_mem, w_o, ln1_g, ln1_b, w_router_group,
           b_router_group, w_router_expert, b_router_expert, w_exp_gate, w_exp_up, w_exp_down, ln2_g, ln2_b):
    batch, seq, d = x.shape
    t = batch * seq
    assert w_in.shape[0] == DEPTH == 1
    assert seq % PROJ_TM == 0 and seq // CMP_STRIDE == N_A and seq // SLC_BLOCK == N_SEL

    offs = [int(v) for v in np.cumsum(
        [0, Q_W, KV_W, KV_W, KV_W, KV_W, KV_W, KV_W, GATE_W, 2 * SGU_WIDTH, MEM_WIDTH, 3 * d])]
    assert offs[-1] == w_in.shape[2]
    w_t = jnp.transpose(w_in[0])
    seg = lambda i: w_t[offs[i]:offs[i + 1]]
    gate_rows = jnp.pad(seg(7), ((0, LANES - GATE_W), (0, 0)))
    wp_t = jnp.concatenate([seg(0), seg(1), seg(3), seg(5), seg(2), seg(4), seg(6), gate_rows, seg(8), seg(9)],
                           axis=0).astype(BF16)
    w_mt = seg(10).astype(BF16)
    assert wp_t.shape[0] == _PROJ_COLS
    rope = _rope_table(seq)
    tril = jnp.tril(jnp.ones((SGU_CHUNK, SGU_CHUNK), dtype=bool))
    ws = jnp.where(tril[None], sgu_w_s[0], 0.0).astype(BF16)
    bs = jnp.repeat(sgu_b_s[0].T, SGU_WIDTH // SGU_GROUPS, axis=1)

    x2 = x.reshape(t, d)
    (qpad, kc, ksl, kwn, vc, vslt, vwnt, gt, osgu, mq) = _proj_call(
        x2, wp_t, rope, sgu_ln_g[0][None], sgu_ln_b[0][None], ws, bs, seq)

    pek, w1k, w2k = _expand_cmp_weights(cmp_pe_k[0], cmp_w1_k[0], cmp_w2_k[0])
    pev, w1v, w2v = _expand_cmp_weights(cmp_pe_v[0], cmp_w1_v[0], cmp_w2_v[0])
    kcmp, vcmp = _compress_call(kc, vc, pek, pev, w1k, w2k, w1v, w2v, batch)

    nqb = seq // Q_BLOCK
    n_g = NSA_KV_GROUPS
    vcmpt = vcmp.reshape(batch, N_A, KV_W).transpose(0, 2, 1).reshape(batch * KV_W, N_A)
    ci = np.arange(N_A)
    sj = np.arange(N_SEL)
    overlap = ((ci[None, :] * CMP_STRIDE + CMP_LEN - 1 >= sj[:, None] * SLC_BLOCK)
               & (ci[None, :] * CMP_STRIDE <= sj[:, None] * SLC_BLOCK + SLC_BLOCK - 1)
               & (ci[None, :] < (seq - CMP_LEN) // CMP_STRIDE + 1))
    ot = jnp.asarray(overlap, dtype=BF16)
    onsa = _nsa_call(qpad, ksl, kwn, vslt, vwnt, kcmp, vcmpt, gt, ot, batch, seq)

    omem = _memattn_call(mq, mem.reshape(batch * mem.shape[1], d), w_mem_kv[0], batch, seq)

    x1 = _merge_call(x2, onsa, osgu, omem, w_mt, w_br_nsa[0], w_br_sgu[0], w_br_mem[0], w_o[0],
                     ln1_g[0][None], ln1_b[0][None])

    assert EXPERTS_PER_GROUP == _R_SLOT and N_GROUPS <= _R_SLOT
    n_r = _R_SLOT + N_GROUPS * EXPERTS_PER_GROUP
    wr = jnp.concatenate([jnp.pad(w_router_group[0], ((0, 0), (0, _R_SLOT - N_GROUPS))),
                          w_router_expert[0]], axis=1)
    wr = jnp.pad(wr, ((0, 0), (0, LANES - n_r))).astype(BF16)
    br = jnp.concatenate([jnp.pad(b_router_group[0], (0, _R_SLOT - N_GROUPS)), b_router_expert[0]])
    br = jnp.pad(br, (0, LANES - n_r))[None]
    wg = w_exp_gate[0].transpose(0, 2, 1, 3).reshape(N_GROUPS, d, _HID).astype(BF16)
    wu = w_exp_up[0].transpose(0, 2, 1, 3).reshape(N_GROUPS, d, _HID).astype(BF16)
    wd = w_exp_down[0].reshape(N_GROUPS, _HID, d).astype(BF16)
    out = _moe_call(x1, wr, br, wg, wu, wd, ln2_g[0][None], ln2_b[0][None])
    return out.reshape(batch, seq, d)
```

```python
import functools

import numpy as np
import jax
import jax.numpy as jnp
from jax import lax
from jax.experimental import pallas as pl
from jax.experimental.pallas import tpu as pltpu

NSA_HEADS = 8
NSA_KV_GROUPS = 2
NSA_HPG = NSA_HEADS // NSA_KV_GROUPS
HEAD_DIM = 64
CMP_LEN = 32
CMP_STRIDE = 16
CMP_HIDDEN = 256
SLC_BLOCK = 64
SLC_TOPN = 8
WINDOW = 512
Q_BLOCK = 128
N_BAND = WINDOW // Q_BLOCK
ROPE_THETA = 500000.0
ROPE_DIM = HEAD_DIM // 4
SGU_CHUNK = 128
SGU_GROUPS = 8
SGU_WIDTH = 512
MEM_HEADS = 4
MEM_HEAD_DIM = 128
MEM_WIDTH = MEM_HEADS * MEM_HEAD_DIM
N_GROUPS = 4
EXPERTS_PER_GROUP = 8
EXPERT_FF = 256
DEPTH = 1
DN_ALPHA = (2.0 * DEPTH) ** 0.25
LN_EPS = 1e-5
NEG = -1e30
LOG2E = 1.4426950408889634

LANES = 128
Q_W = NSA_HEADS * HEAD_DIM
KV_W = NSA_KV_GROUPS * HEAD_DIM
GATE_W = NSA_HEADS * 3
VMEM_LIMIT = 56 * 1024 * 1024
MOE_VMEM_LIMIT = 60 * 1024 * 1024

BF16 = jnp.bfloat16
F32 = jnp.float32


def _dot(a, b):
    return jnp.dot(a, b, preferred_element_type=F32)


def _dot_nt(a, b):
    return lax.dot_general(a, b, (((1,), (1,)), ((), ())), preferred_element_type=F32)


def _sigmoid(x):
    return 1.0 / (1.0 + jnp.exp(-x))


def _gelu(x):
    return 0.5 * x * (1.0 + lax.erf(x * (2.0 ** -0.5)))


def _layer_norm(x, g, b):
    mu = jnp.mean(x, axis=-1, keepdims=True)
    xc = x - mu
    var = jnp.mean(xc * xc, axis=-1, keepdims=True)
    return xc * lax.rsqrt(var + LN_EPS) * g + b


PROJ_TM = 512
_ROPE_COLS = Q_W + 3 * KV_W
_V_OFF = _ROPE_COLS
_G_OFF = _V_OFF + 3 * KV_W
_SGU_OFF = _G_OFF + LANES
_MQ_OFF = _SGU_OFF + 2 * SGU_WIDTH
_PROJ_COLS = _MQ_OFF + MEM_WIDTH


def _store_strided_rows(val, out_ref, tmp_ref):
    tmp_ref[...] = val
    n = val.shape[0] // CMP_STRIDE
    for l in range(CMP_STRIDE):
        out_ref[:, l * LANES:(l + 1) * LANES] = tmp_ref[pl.ds(l, n, stride=CMP_STRIDE), :].astype(BF16)


def _transpose_weight(wt_ref, w_s):
    for j in range(wt_ref.shape[0] // LANES):
        rows = slice(j * LANES, (j + 1) * LANES)
        w_s[:, rows] = wt_ref[rows, :].astype(F32).T.astype(BF16)


def _proj_kernel(x_ref, wt_ref, rope_ref, lng_ref, lnb_ref, ws_ref, bs_ref,
                 q_ref, kc_ref, ksl_ref, kwn_ref, vc_ref, vsl_ref, vwn_ref, gate_ref, osgu_ref, mq_ref,
                 a_s, w_ref):
    pl.when(pl.program_id(0) == 0)(functools.partial(_transpose_weight, wt_ref, w_ref))
    xb = x_ref[...].astype(BF16)
    lane = lax.broadcasted_iota(jnp.int32, (PROJ_TM, LANES), 1)
    low = lane < HEAD_DIM
    cos = rope_ref[:, 0:LANES]
    s_dn = rope_ref[:, LANES:2 * LANES]
    s_up = rope_ref[:, 2 * LANES:3 * LANES]

    hz = _dot(xb, w_ref[:, _SGU_OFF:_MQ_OFF])
    h = _dot(xb, w_ref[:, 0:_ROPE_COLS])
    k_refs = (kc_ref, ksl_ref, kwn_ref)
    for j in range(_ROPE_COLS // LANES):
        blk = h[:, j * LANES:(j + 1) * LANES]
        r = (blk * cos + pltpu.roll(blk, LANES - ROPE_DIM // 2, 1) * s_dn
             + pltpu.roll(blk, ROPE_DIM // 2, 1) * s_up)
        if j < Q_W // LANES:
            r = r * (HEAD_DIM ** -0.5 * LOG2E)
            sw = pltpu.roll(r, HEAD_DIM, 1)
            g = (2 * j) // NSA_HPG
            if g == 0:
                h0 = jnp.where(low, r, 0.0)
                h1 = jnp.where(low, sw, 0.0)
            else:
                h0 = jnp.where(low, 0.0, sw)
                h1 = jnp.where(low, 0.0, r)
            q_ref[:, (2 * j) * LANES:(2 * j + 1) * LANES] = h0.astype(BF16)
            q_ref[:, (2 * j + 1) * LANES:(2 * j + 2) * LANES] = h1.astype(BF16)
        else:
            if j == Q_W // LANES:
                _store_strided_rows(r, kc_ref, a_s)
            else:
                k_refs[j - Q_W // LANES][...] = r.astype(BF16)

    _store_strided_rows(_dot(xb, w_ref[:, _V_OFF:_V_OFF + LANES]), vc_ref, a_s)
    ones = jnp.ones((V_ROWS - HEAD_DIM, PROJ_TM), BF16)
    for v_ref, row0, blk in ((vsl_ref, _V_OFF + LANES, SEL_CHUNK), (vwn_ref, _V_OFF + 2 * LANES, Q_BLOCK)):
        vt = _dot_nt(wt_ref[row0:row0 + LANES, :], xb).astype(BF16)
        for g in range(NSA_KV_GROUPS):
            for b in range(PROJ_TM // blk):
                v_ref[g, b, 0:HEAD_DIM, :] = vt[g * HEAD_DIM:(g + 1) * HEAD_DIM, b * blk:(b + 1) * blk]
                v_ref[g, b, HEAD_DIM:V_ROWS, :] = ones[:, 0:blk]

    gate_ref[...] = _sigmoid(_dot_nt(wt_ref[_G_OFF:_SGU_OFF, :], xb))

    mq_ref[...] = _dot(xb, w_ref[:, _MQ_OFF:_PROJ_COLS]).astype(BF16)

    z = _gelu(hz)
    u = z[:, 0:SGU_WIDTH]
    v = _layer_norm(z[:, SGU_WIDTH:2 * SGU_WIDTH], lng_ref[...], lnb_ref[...]).astype(BF16)
    lane_c = lax.broadcasted_iota(jnp.int32, (SGU_CHUNK, LANES), 1)
    low_c = lane_c < (SGU_WIDTH // SGU_GROUPS)
    n_ch = PROJ_TM // SGU_CHUNK
    for gp in range(SGU_WIDTH // LANES):
        cols = slice(gp * LANES, (gp + 1) * LANES)
        vcat = jnp.concatenate([v[ci * SGU_CHUNK:(ci + 1) * SGU_CHUNK, cols] for ci in range(n_ch)], axis=1)
        sv0 = _dot(ws_ref[2 * gp], vcat)
        sv1 = _dot(ws_ref[2 * gp + 1], vcat)
        for ci in range(n_ch):
            rows = slice(ci * SGU_CHUNK, (ci + 1) * SGU_CHUNK)
            lanes = slice(ci * LANES, (ci + 1) * LANES)
            sv = jnp.where(low_c, sv0[:, lanes], sv1[:, lanes]) + bs_ref[:, cols]
            osgu_ref[rows, cols] = (u[rows, cols] * sv).astype(BF16)


def _proj_call(x2, wp, rope, lng, lnb, ws, bs, seq):
    t = x2.shape[0]
    d = x2.shape[1]
    nt = t // PROJ_TM
    per_seq = seq // PROJ_TM
    row = lambda i: (i, 0)
    const2 = lambda i: (0, 0)
    assert PROJ_TM % SEL_CHUNK == 0 and PROJ_TM % Q_BLOCK == 0
    n_g = NSA_KV_GROUPS
    batch = t // seq
    vblock = lambda i: (i // per_seq, i % per_seq, 0, 0)

    def v_shape(blk):
        return jax.ShapeDtypeStruct((batch * n_g, seq // blk, V_ROWS, blk), BF16)

    def v_spec(blk):
        return pl.BlockSpec((n_g, PROJ_TM // blk, V_ROWS, blk), vblock)

    a_shape = jax.ShapeDtypeStruct((t // CMP_STRIDE, CMP_STRIDE * KV_W), BF16)
    a_spec = pl.BlockSpec((PROJ_TM // CMP_STRIDE, CMP_STRIDE * KV_W), row)
    k_shape = jax.ShapeDtypeStruct((t, KV_W), BF16)
    k_spec = pl.BlockSpec((PROJ_TM, KV_W), row)
    out_shapes = (
        jax.ShapeDtypeStruct((t, NSA_HEADS * LANES), BF16),
        a_shape, k_shape, k_shape, a_shape,
        v_shape(SEL_CHUNK), v_shape(Q_BLOCK),
        jax.ShapeDtypeStruct((LANES, t), F32),
        jax.ShapeDtypeStruct((t, SGU_WIDTH), BF16),
        jax.ShapeDtypeStruct((t, MEM_WIDTH), BF16),
    )
    out_specs = (
        pl.BlockSpec((PROJ_TM, NSA_HEADS * LANES), row),
        a_spec, k_spec, k_spec, a_spec,
        v_spec(SEL_CHUNK), v_spec(Q_BLOCK),
        pl.BlockSpec((LANES, PROJ_TM), lambda i: (0, i)),
        pl.BlockSpec((PROJ_TM, SGU_WIDTH), row),
        pl.BlockSpec((PROJ_TM, MEM_WIDTH), row),
    )
    return pl.pallas_call(
        _proj_kernel,
        grid=(nt,),
        in_specs=[
            pl.BlockSpec((PROJ_TM, d), row),
            pl.BlockSpec((_PROJ_COLS, d), const2),
            pl.BlockSpec((PROJ_TM, 3 * LANES), lambda i: (i % per_seq, 0)),
            pl.BlockSpec((1, SGU_WIDTH), const2),
            pl.BlockSpec((1, SGU_WIDTH), const2),
            pl.BlockSpec((SGU_GROUPS, SGU_CHUNK, SGU_CHUNK), lambda i: (0, 0, 0)),
            pl.BlockSpec((SGU_CHUNK, SGU_WIDTH), const2),
        ],
        out_specs=out_specs,
        out_shape=out_shapes,
        scratch_shapes=[pltpu.VMEM((PROJ_TM, KV_W), F32), pltpu.VMEM((d, _PROJ_COLS), BF16)],
        compiler_params=pltpu.CompilerParams(
            dimension_semantics=("arbitrary",), vmem_limit_bytes=VMEM_LIMIT),
        name="proj",
    )(x2, wp, rope, lng, lnb, ws, bs)


N_A = 128
A_W = CMP_STRIDE * KV_W


def _compress_kernel(ka_ref, va_ref, pek_ref, pev_ref, w1k_ref, w2k_ref, w1v_ref, w2v_ref,
                     kcmp_ref, vcmp_ref):
    def one(a_ref, pe_ref, w1_ref, w2_ref, out_ref):
        a = a_ref[...].astype(F32)
        top = (a + pe_ref[0:1, :]).astype(BF16)
        bot = (a + pe_ref[1:2, :]).astype(BF16)
        h1 = _dot(top, w1_ref[0])
        h2 = _dot(bot, w1_ref[1])
        pre = h1 + pltpu.roll(h2, N_A - 1, 0)
        act = _gelu(pre).astype(BF16)
        out_ref[...] = _dot(act, w2_ref[...]).astype(BF16)

    one(ka_ref, pek_ref, w1k_ref, w2k_ref, kcmp_ref)
    one(va_ref, pev_ref, w1v_ref, w2v_ref, vcmp_ref)


def _compress_call(ka, va, pek, pev, w1k, w2k, w1v, w2v, batch):
    row = lambda b: (b, 0)
    c2 = lambda b: (0, 0)
    c3 = lambda b: (0, 0, 0)
    hid2 = NSA_KV_GROUPS * CMP_HIDDEN
    return pl.pallas_call(
        _compress_kernel,
        grid=(batch,),
        in_specs=[
            pl.BlockSpec((N_A, A_W), row),
            pl.BlockSpec((N_A, A_W), row),
            pl.BlockSpec((8, A_W), c2),
            pl.BlockSpec((8, A_W), c2),
            pl.BlockSpec((2, A_W, hid2), c3),
            pl.BlockSpec((hid2, KV_W), c2),
            pl.BlockSpec((2, A_W, hid2), c3),
            pl.BlockSpec((hid2, KV_W), c2),
        ],
        out_specs=(pl.BlockSpec((N_A, KV_W), row), pl.BlockSpec((N_A, KV_W), row)),
        out_shape=(jax.ShapeDtypeStruct((batch * N_A, KV_W), BF16),
                   jax.ShapeDtypeStruct((batch * N_A, KV_W), BF16)),
        compiler_params=pltpu.CompilerParams(
            dimension_semantics=("parallel",), vmem_limit_bytes=VMEM_LIMIT),
        name="compress",
    )(ka, va, pek, pev, w1k, w2k, w1v, w2v)


N_SEL = 32
SEL_CHUNK = 256
SEL_SUB = 128
HQ = NSA_HPG * Q_BLOCK
V_ROWS = HEAD_DIM + 16


def _nsa_kernel(*refs):
    ksl_ref = refs[1]
    reach = pl.program_id(1) // (SEL_CHUNK // Q_BLOCK)
    for n in range(ksl_ref.shape[0] // SEL_CHUNK):
        pl.when(reach == n)(functools.partial(_nsa_body, n, *refs))


def _nsa_body(n_later, q_ref, ksl_ref, kwn_ref, vslt_ref, vwnt_ref, kcmp_ref, vcmpt_ref, gt_ref,
              ot_ref, o_ref, m_s, acc_s, ocmp_s, owin_s, cap_s, out_s):
    c = pl.program_id(1)
    lane_hq = lax.broadcasted_iota(jnp.int32, (1, HQ), 1)
    pos_hq = c * Q_BLOCK + (lane_hq & (Q_BLOCK - 1))
    pos_q = c * Q_BLOCK + lax.broadcasted_iota(jnp.int32, (1, Q_BLOCK), 1)
    groups = range(NSA_KV_GROUPS)
    qgs = [jnp.concatenate(
        [q_ref[:, (g * NSA_HPG + hh) * LANES:(g * NSA_HPG + hh + 1) * LANES]
         for hh in range(NSA_HPG)], axis=0) for g in groups]

    kcmp = kcmp_ref[...]
    s_c = [_dot_nt(kcmp, qgs[g]) for g in groups]
    n_idx = lax.broadcasted_iota(jnp.int32, (N_A, HQ), 0)
    valid_c = (n_idx * CMP_STRIDE + (CMP_LEN - 1)) <= pos_hq
    p_c = []
    for g in groups:
        sm_c = jnp.where(valid_c, s_c[g], NEG)
        m_c = jnp.max(sm_c, axis=0, keepdims=True)
        e_c = jnp.where(valid_c, jnp.exp2(sm_c - m_c), 0.0)
        d_c = jnp.sum(e_c, axis=0, keepdims=True)
        p_c.append(e_c / jnp.where(d_c > 0, d_c, 1.0))
    for g in groups:
        ocmp_s[g] = _dot(vcmpt_ref[g * HEAD_DIM:(g + 1) * HEAD_DIM, :], p_c[g].astype(BF16))

    ot = ot_ref[...]
    imp = []
    for g in groups:
        ps = (p_c[g][:, 0:Q_BLOCK] + p_c[g][:, Q_BLOCK:2 * Q_BLOCK]
              + p_c[g][:, 2 * Q_BLOCK:3 * Q_BLOCK] + p_c[g][:, 3 * Q_BLOCK:4 * Q_BLOCK])
        p_hi = ps.astype(BF16)
        r1 = ps - p_hi.astype(F32)
        p_lo = r1.astype(BF16)
        p_lo2 = (r1 - p_lo.astype(F32)).astype(BF16)
        imp.append(_dot(ot, p_hi) + _dot(ot, p_lo) + _dot(ot, p_lo2))

    j_idx = lax.broadcasted_iota(jnp.int32, (N_SEL, Q_BLOCK), 0)
    cur = pos_q // SLC_BLOCK
    future = j_idx > cur
    forced = (j_idx == 0) | (j_idx == cur) | (j_idx == cur - 1)
    imp = [jnp.where(future, NEG, jnp.where(forced, -NEG, imp[g])) for g in groups]
    rank = [jnp.zeros((N_SEL, Q_BLOCK), F32) for g in groups]
    for i in range(N_SEL):
        for g in groups:
            row = imp[g][i:i + 1, :]
            beats = (row > imp[g]) | ((row == imp[g]) & (j_idx > i))
            rank[g] = rank[g] + jnp.where(beats, 1.0, 0.0)
    for g in groups:
        cap_s[g] = jnp.where(rank[g] < float(SLC_TOPN), -NEG, NEG)
        m_s[g] = jnp.full((1, HQ), NEG, F32)
        acc_s[g] = jnp.zeros((V_ROWS, HQ), F32)

    blocks_per_sub = SEL_SUB // SLC_BLOCK
    subs_per_chunk = SEL_CHUNK // SEL_SUB

    def sel_spans(kc):
        return [slice(kc * SEL_CHUNK + a * SEL_SUB, kc * SEL_CHUNK + (a + 1) * SEL_SUB)
                for a in range(subs_per_chunk)]

    def sel_scores(kc):
        return [[_dot_nt(ksl_ref[ks, :], qgs[g]) for g in groups] for ks in sel_spans(kc)]

    def sel_chunk(kc, s=None):
        spans = sel_spans(kc)
        s = sel_scores(kc) if s is None else s
        for a, ks in enumerate(spans):
            kpos = ks.start + lax.broadcasted_iota(jnp.int32, (SEL_SUB, Q_BLOCK), 0)
            causal = kpos <= pos_q
            for g in groups:
                j0 = ks.start // SLC_BLOCK
                cap = jnp.concatenate(
                    [jnp.broadcast_to(cap_s[g, j:j + 1, :], (SLC_BLOCK, Q_BLOCK))
                     for j in range(j0, j0 + blocks_per_sub)], axis=0)
                cap = jnp.where(causal, cap, NEG)
                sm = jnp.minimum(s[a][g], jnp.concatenate([cap] * NSA_HPG, axis=1))
                m_old = m_s[g]
                m_new = jnp.maximum(m_old, jnp.max(sm, axis=0, keepdims=True))
                alpha = jnp.exp2(m_old - m_new)
                e = jnp.exp2((sm - m_new).astype(BF16))
                v_t = vslt_ref[g, kc, :, a * SEL_SUB:(a + 1) * SEL_SUB]
                acc_s[g] = alpha * acc_s[g] + _dot(v_t, e)
                m_s[g] = m_new

    s0 = sel_scores(0)
    _nsa_window(c, qgs, kwn_ref, vwnt_ref, owin_s)
    s_next = sel_scores(1) if n_later else None
    sel_chunk(0, s0)
    for kc in range(1, n_later + 1):
        s_cur = s_next
        if kc < n_later:
            s_next = sel_scores(kc + 1)
        sel_chunk(kc, s_cur)

    for g in groups:
        acc = acc_s[g]
        o_sel = acc[0:HEAD_DIM, :] * (1.0 / acc[HEAD_DIM:HEAD_DIM + 1, :])

        def gate_row(br):
            return jnp.concatenate(
                [gt_ref[(g * NSA_HPG + hh) * 3 + br:(g * NSA_HPG + hh) * 3 + br + 1, :]
                 for hh in range(NSA_HPG)], axis=1)
        o_t = gate_row(0) * ocmp_s[g] + gate_row(1) * o_sel + gate_row(2) * owin_s[g]
        for hh in range(NSA_HPG):
            h = g * NSA_HPG + hh
            out_s[h * HEAD_DIM:(h + 1) * HEAD_DIM, :] = o_t[:, hh * Q_BLOCK:(hh + 1) * Q_BLOCK]

    o_ref[...] = out_s[...].T.astype(BF16)


def _nsa_window(c, qgs, kwn_ref, vwnt_ref, owin_s):
    groups = range(NSA_KV_GROUPS)
    q_i = lax.broadcasted_iota(jnp.int32, (Q_BLOCK, Q_BLOCK), 1)
    k_i = lax.broadcasted_iota(jnp.int32, (Q_BLOCK, Q_BLOCK), 0)
    blks, caps = [], []
    for i in range(N_BAND + 1):
        blk = c - N_BAND + i
        blks.append(jnp.maximum(blk, 0))
        off = (N_BAND - i) * Q_BLOCK
        if off - (Q_BLOCK - 1) >= 0 and off + (Q_BLOCK - 1) < WINDOW:
            caps.append(jnp.where(blk >= 0, -NEG, NEG))
        else:
            diff = off + q_i - k_i
            ok = (diff >= 0) & (diff < WINDOW) & (blk >= 0)
            caps.append(jnp.concatenate([jnp.where(ok, -NEG, NEG)] * NSA_HPG, axis=1))
    k_blocks = [kwn_ref[pl.ds(pl.multiple_of(blks[i] * Q_BLOCK, Q_BLOCK), Q_BLOCK), :]
                for i in range(N_BAND + 1)]
    s_w = [[None] * (N_BAND + 1) for g in groups]
    for i in reversed(range(N_BAND + 1)):
        for g in groups:
            s_w[g][i] = _dot_nt(k_blocks[i], qgs[g])
    m_w = [None for g in groups]
    o_win = [None for g in groups]
    for i in reversed(range(N_BAND + 1)):
        for g in groups:
            sm = jnp.minimum(s_w[g][i], caps[i])
            m_blk = jnp.max(sm, axis=0, keepdims=True)
            m_new = m_blk if m_w[g] is None else jnp.maximum(m_w[g], m_blk)
            pv = _dot(vwnt_ref[g, blks[i]], jnp.exp2((sm - m_new).astype(BF16)))
            o_win[g] = pv if m_w[g] is None else jnp.exp2(m_w[g] - m_new) * o_win[g] + pv
            m_w[g] = m_new
    for g in groups:
        owin_s[g] = o_win[g][0:HEAD_DIM, :] * (1.0 / o_win[g][HEAD_DIM:HEAD_DIM + 1, :])


def _nsa_call(qpad, ksl, kwn, vslt, vwnt, kcmp, vcmpt, gt, ot, batch, seq):
    nqb = seq // Q_BLOCK
    n_g = NSA_KV_GROUPS
    qrow = lambda b, c: (b * nqb + c, 0)
    brow = lambda b, c: (b, 0)
    c2 = lambda b, c: (0, 0)
    return pl.pallas_call(
        _nsa_kernel,
        grid=(batch, nqb),
        in_specs=[
            pl.BlockSpec((Q_BLOCK, NSA_HEADS * LANES), qrow),
            pl.BlockSpec((seq, KV_W), brow),
            pl.BlockSpec((seq, KV_W), brow),
            pl.BlockSpec((n_g, seq // SEL_CHUNK, V_ROWS, SEL_CHUNK), lambda b, c: (b, 0, 0, 0)),
            pl.BlockSpec((n_g, nqb, V_ROWS, Q_BLOCK), lambda b, c: (b, 0, 0, 0)),
            pl.BlockSpec((N_A, KV_W), brow),
            pl.BlockSpec((KV_W, N_A), brow),
            pl.BlockSpec((LANES, Q_BLOCK), lambda b, c: (0, b * nqb + c)),
            pl.BlockSpec((N_SEL, N_A), c2),
        ],
        out_specs=pl.BlockSpec((Q_BLOCK, Q_W), qrow),
        out_shape=jax.ShapeDtypeStruct((batch * seq, Q_W), BF16),
        scratch_shapes=[
            pltpu.VMEM((n_g, 1, HQ), F32),
            pltpu.VMEM((n_g, V_ROWS, HQ), F32),
            pltpu.VMEM((n_g, HEAD_DIM, HQ), F32),
            pltpu.VMEM((n_g, HEAD_DIM, HQ), F32),
            pltpu.VMEM((n_g, N_SEL, Q_BLOCK), F32),
            pltpu.VMEM((Q_W, Q_BLOCK), F32),
        ],
        compiler_params=pltpu.CompilerParams(
            dimension_semantics=("parallel", "arbitrary"), vmem_limit_bytes=VMEM_LIMIT),
        name="nsa",
    )(qpad, ksl, kwn, vslt, vwnt, kcmp, vcmpt, gt, ot)


MEM_TM = 512
MEM_V_ROWS = MEM_HEAD_DIM + 16


def _memattn_kernel(mq_ref, mem_ref, wkv_ref, o_ref, k_s, vt_s, out_s):
    @pl.when(pl.program_id(1) == 0)
    def _():
        kv = _dot(mem_ref[...].astype(BF16), wkv_ref[...].astype(BF16))
        k_s[...] = kv[:, 0:MEM_WIDTH].astype(BF16)
        vt = kv[:, MEM_WIDTH:2 * MEM_WIDTH].T.astype(BF16)
        for h in range(MEM_HEADS):
            vt_s[h, 0:MEM_HEAD_DIM, :] = vt[h * MEM_HEAD_DIM:(h + 1) * MEM_HEAD_DIM, :]
            vt_s[h, MEM_HEAD_DIM:, :] = jnp.ones((MEM_V_ROWS - MEM_HEAD_DIM, vt.shape[1]), BF16)

    heads = range(MEM_HEADS)
    cols = [slice(h * MEM_HEAD_DIM, (h + 1) * MEM_HEAD_DIM) for h in heads]
    s_t = [_dot_nt(k_s[:, cols[h]], mq_ref[:, cols[h]]) for h in heads]
    e = []
    for h in heads:
        m = jnp.max(s_t[h], axis=0, keepdims=True)
        e.append(jnp.exp2((s_t[h] - m) * (MEM_HEAD_DIM ** -0.5 * LOG2E)).astype(BF16))
    o_t = [_dot(vt_s[h], e[h]) for h in heads]
    for h in heads:
        out_s[cols[h], :] = o_t[h][0:MEM_HEAD_DIM, :] * (1.0 / o_t[h][MEM_HEAD_DIM:MEM_HEAD_DIM + 1, :])
    o_ref[...] = out_s[...].T.astype(BF16)


def _memattn_call(mq, mem2, wkv, batch, seq):
    m_len = mem2.shape[0] // batch
    d = mem2.shape[1]
    per = seq // MEM_TM
    return pl.pallas_call(
        _memattn_kernel,
        grid=(batch, per),
        in_specs=[
            pl.BlockSpec((MEM_TM, MEM_WIDTH), lambda b, i: (b * per + i, 0)),
            pl.BlockSpec((m_len, d), lambda b, i: (b, 0)),
            pl.BlockSpec((d, 2 * MEM_WIDTH), lambda b, i: (0, 0)),
        ],
        out_specs=pl.BlockSpec((MEM_TM, MEM_WIDTH), lambda b, i: (b * per + i, 0)),
        out_shape=jax.ShapeDtypeStruct((batch * seq, MEM_WIDTH), BF16),
        scratch_shapes=[
            pltpu.VMEM((m_len, MEM_WIDTH), BF16),
            pltpu.VMEM((MEM_HEADS, MEM_V_ROWS, m_len), BF16),
            pltpu.VMEM((MEM_WIDTH, MEM_TM), F32),
        ],
        compiler_params=pltpu.CompilerParams(
            dimension_semantics=("parallel", "arbitrary"), vmem_limit_bytes=VMEM_LIMIT),
        name="memattn",
    )(mq, mem2, wkv)


MERGE_TM = 512


def _merge_kernel(x_ref, onsa_ref, osgu_ref, omem_ref, wgt_ref, wbn32_ref, wbs32_ref, wbm32_ref, wo32_ref,
                  g_ref, b_ref, out_ref, wbn_ref, wbs_ref, wbm_ref, wo_ref, wg_ref):
    @pl.when(pl.program_id(0) == 0)
    def _():
        for dst, src in ((wbn_ref, wbn32_ref), (wbs_ref, wbs32_ref), (wbm_ref, wbm32_ref),
                         (wo_ref, wo32_ref)):
            dst[...] = src[...].astype(BF16)
        _transpose_weight(wgt_ref, wg_ref)

    d = x_ref.shape[1]
    branches = ((onsa_ref, wbn_ref), (osgu_ref, wbs_ref), (omem_ref, wbm_ref))
    halves = [slice(h * (MERGE_TM // 2), (h + 1) * (MERGE_TM // 2)) for h in range(2)]
    xs = [x_ref[rows, :] for rows in halves]
    xbs = [x.astype(BF16) for x in xs]
    logits = [[_dot(xb, wg_ref[:, br * d:(br + 1) * d]) for br in range(len(branches))] for xb in xbs]
    projs = [[_dot(o_r[rows, :], w_r[...]) for o_r, w_r in branches] for rows in halves]
    ys = []
    for h in range(2):
        y = None
        for br in range(len(branches)):
            term = _sigmoid(logits[h][br]) * projs[h][br]
            y = term if y is None else y + term
        ys.append(y.astype(BF16))
    outs = [_dot(y, wo_ref[...]) for y in ys]
    for h, rows in enumerate(halves):
        out_ref[rows, :] = _layer_norm(DN_ALPHA * xs[h] + outs[h], g_ref[...], b_ref[...])


def _merge_call(x2, onsa, osgu, omem, wg, wbn, wbs, wbm, wo, g1, b1):
    t, d = x2.shape
    row = lambda i: (i, 0)
    c2 = lambda i: (0, 0)
    full = lambda a: pl.BlockSpec(a.shape, c2)
    once = lambda a: pl.BlockSpec(a.shape, c2, pipeline_mode=pl.Buffered(1))
    return pl.pallas_call(
        _merge_kernel,
        grid=(t // MERGE_TM,),
        in_specs=[
            pl.BlockSpec((MERGE_TM, d), row),
            pl.BlockSpec((MERGE_TM, onsa.shape[1]), row),
            pl.BlockSpec((MERGE_TM, osgu.shape[1]), row),
            pl.BlockSpec((MERGE_TM, omem.shape[1]), row),
            pl.BlockSpec((3 * d, d), c2, pipeline_mode=pl.Buffered(1)),
            once(wbn), once(wbs), once(wbm), once(wo), full(g1), full(b1),
        ],
        out_specs=pl.BlockSpec((MERGE_TM, d), row),
        out_shape=jax.ShapeDtypeStruct((t, d), F32),
        scratch_shapes=[pltpu.VMEM(a.shape, BF16) for a in (wbn, wbs, wbm, wo)]
        + [pltpu.VMEM((d, 3 * d), BF16)],
        compiler_params=pltpu.CompilerParams(
            dimension_semantics=("arbitrary",), vmem_limit_bytes=VMEM_LIMIT),
        name="merge",
    )(x2, onsa, osgu, omem, wg, wbn, wbs, wbm, wo, g1, b1)


MOE_TM = 1024
MOE_CH = 144
_YS_ROWS = -(-(MOE_TM + N_GROUPS * MOE_CH) // 256) * 256
_R_SLOT = 8
_DEST_LANE = 3 * _R_SLOT
_HID = EXPERTS_PER_GROUP * EXPERT_FF


def _moe_kernel(x_ref, tri_ref, wr_ref, br_ref, wg_ref, wu_ref, wd_ref, g_ref, b_ref, out_ref,
                xa_s, tok_s, keyr_s, ys_s, cnt_s, base_s):
    grp = pl.program_id(1)
    tm = x_ref.shape[0]
    d = x_ref.shape[1]

    @pl.when(grp == 0)
    def _route():
        xb = x_ref[...].astype(BF16)
        xa_s[:, 0:d] = xb
        logits = _dot(xb, wr_ref[...]) + br_ref[...]
        lt = logits.T
        row = lax.broadcasted_iota(jnp.int32, (_R_SLOT, tm), 0)
        gl = jnp.where(row < N_GROUPS, lt[0:_R_SLOT], NEG)
        gmax = jnp.max(gl, axis=0, keepdims=True)
        gidx = jnp.min(jnp.where(gl == gmax, row, _R_SLOT), axis=0, keepdims=True)
        gprob = 1.0 / jnp.sum(jnp.exp(gl - gmax), axis=0, keepdims=True)
        el = lt[_R_SLOT:2 * _R_SLOT]
        for k in range(1, N_GROUPS):
            el = jnp.where(gidx == k, lt[(k + 1) * _R_SLOT:(k + 2) * _R_SLOT], el)
        ee = jnp.exp(el - jnp.max(el, axis=0, keepdims=True))
        ep = ee / jnp.sum(ee, axis=0, keepdims=True)
        t1 = jnp.max(ep, axis=0, keepdims=True)
        i1 = jnp.min(jnp.where(ep == t1, row, _R_SLOT), axis=0, keepdims=True)
        rest = row != i1
        t2 = jnp.max(jnp.where(rest, ep, -1.0), axis=0, keepdims=True)
        i2 = jnp.min(jnp.where(rest & (ep == t2), row, _R_SLOT), axis=0, keepdims=True)
        den = t1 + t2
        cwf = (jnp.where(row == i1, t1 / den, 0.0) + jnp.where(row == i2, t2 / den, 0.0)) * gprob
        hi = cwf.astype(BF16).astype(F32)
        r1 = cwf - hi
        lo = r1.astype(BF16).astype(F32)
        lo2 = (r1 - lo).astype(BF16).astype(F32)

        onehot = jnp.where(row == gidx, 1.0, 0.0)
        rank = _dot_nt(onehot.astype(BF16), tri_ref[...])
        keyr_s[...] = jnp.where(onehot > 0.5, rank, -1.0)
        base = jnp.int32(0)
        basev = jnp.zeros((1, tm), F32)
        for k in range(N_GROUPS):
            n_k = jnp.sum(onehot[k:k + 1, :]).astype(jnp.int32)
            cnt_s[k] = n_k
            base_s[k] = base
            basev = jnp.where(gidx == k, base.astype(F32), basev)
            base = base + ((n_k + MOE_CH - 1) // MOE_CH) * MOE_CH
        dest = jnp.sum(rank * onehot, axis=0, keepdims=True) + basev
        tok = jnp.concatenate(
            [hi, lo, lo2, jnp.broadcast_to(dest, (_R_SLOT, tm)),
             jnp.zeros((LANES - 4 * _R_SLOT, tm), F32)], axis=0).T
        tok_s[...] = tok
        lane = lax.broadcasted_iota(jnp.int32, tok.shape, 1)
        xa_s[:, d:d + LANES] = jnp.where(lane < _DEST_LANE, tok, 0.0).astype(BF16)
        ys_s[...] = jnp.zeros(ys_s.shape, BF16)

    n_rows = cnt_s[grp]
    row0 = base_s[grp]
    keyr = keyr_s[pl.ds(grp, 1), :]
    half = _HID // 2

    def sweep(k, carry):
        ch = MOE_CH
        r_row = (lax.broadcasted_iota(jnp.int32, (ch, tm), 0) + k * MOE_CH).astype(F32)
        pick = jnp.where(keyr == r_row, 1.0, 0.0).astype(BF16)
        ga = _dot(pick, xa_s[...])
        xg = ga[:, 0:d].astype(BF16)
        cwg = ga[:, d:d + LANES]
        cs = (cwg + pltpu.roll(cwg, LANES - _R_SLOT, 1)
              + pltpu.roll(cwg, LANES - 2 * _R_SLOT, 1))
        hg = [_dot(xg, wg_ref[0, :, h * half:(h + 1) * half]) for h in range(2)]
        hu = [_dot(xg, wu_ref[0, :, h * half:(h + 1) * half]) for h in range(2)]
        y = None
        for h in range(2):
            per_half = EXPERTS_PER_GROUP // 2
            cexp = jnp.concatenate(
                [jnp.broadcast_to(cs[:, e:e + 1], (ch, EXPERT_FF))
                 for e in range(h * per_half, (h + 1) * per_half)], axis=1)
            hid = ((hg[h] * _sigmoid(hg[h])) * hu[h]) * cexp
            term = _dot(hid.astype(BF16), wd_ref[0, h * half:(h + 1) * half, :])
            y = term if y is None else y + term
        ys_s[pl.ds(pl.multiple_of(row0 + k * MOE_CH, 16), ch), :] = y.astype(BF16)
        return carry

    lax.fori_loop(0, (n_rows + MOE_CH - 1) // MOE_CH, sweep, 0)

    @pl.when(grp == N_GROUPS - 1)
    def _fin():
        n_q = 4
        q_rows = tm // n_q
        r_col = lax.broadcasted_iota(jnp.int32, (q_rows, _YS_ROWS), 1).astype(F32)
        parts = [slice(i * q_rows, (i + 1) * q_rows) for i in range(n_q)]
        puts = [jnp.where(tok_s[rows, _DEST_LANE:_DEST_LANE + 1] == r_col, 1.0, 0.0).astype(BF16)
                for rows in parts]
        fs = [_dot(put, ys_s[...]) for put in puts]
        for rows, f in zip(parts, fs):
            out_ref[rows, :] = _layer_norm(DN_ALPHA * x_ref[rows, :] + f, g_ref[...], b_ref[...])


def _moe_call(x1, wr, br, wg, wu, wd, g2, b2):
    t, d = x1.shape
    row = lambda i, g: (i, 0)
    c2 = lambda i, g: (0, 0)
    idx = np.arange(MOE_TM)
    tri = jnp.asarray(idx[None, :] < idx[:, None], dtype=BF16)
    return pl.pallas_call(
        _moe_kernel,
        grid=(t // MOE_TM, N_GROUPS),
        in_specs=[
            pl.BlockSpec((MOE_TM, d), row, pipeline_mode=pl.Buffered(1)),
            pl.BlockSpec((MOE_TM, MOE_TM), c2, pipeline_mode=pl.Buffered(1)),
            pl.BlockSpec(wr.shape, c2),
            pl.BlockSpec(br.shape, c2),
            pl.BlockSpec((1, d, _HID), lambda i, g: (g, 0, 0)),
            pl.BlockSpec((1, d, _HID), lambda i, g: (g, 0, 0)),
            pl.BlockSpec((1, _HID, d), lambda i, g: (g, 0, 0)),
            pl.BlockSpec(g2.shape, c2),
            pl.BlockSpec(b2.shape, c2),
        ],
        out_specs=pl.BlockSpec((MOE_TM, d), row),
        out_shape=jax.ShapeDtypeStruct((t, d), F32),
        scratch_shapes=[
            pltpu.VMEM((MOE_TM, d + LANES), BF16),
            pltpu.VMEM((MOE_TM, LANES), F32),
            pltpu.VMEM((_R_SLOT, MOE_TM), F32),
            pltpu.VMEM((_YS_ROWS, d), BF16),
            pltpu.SMEM((N_GROUPS,), jnp.int32),
            pltpu.SMEM((N_GROUPS,), jnp.int32),
        ],
        compiler_params=pltpu.CompilerParams(
            dimension_semantics=("parallel", "arbitrary"), vmem_limit_bytes=MOE_VMEM_LIMIT),
        name="moe",
    )(x1, tri, wr, br, wg, wu, wd, g2, b2)


def _rope_table(seq):
    half = ROPE_DIM // 2
    inv = ROPE_THETA ** (-jnp.arange(0, ROPE_DIM, 2, dtype=F32) / ROPE_DIM)
    ang = jnp.arange(seq, dtype=F32)[:, None] * inv[None, :]
    cos, sin = jnp.cos(ang), jnp.sin(ang)
    rest = HEAD_DIM - ROPE_DIM
    one = jnp.ones((seq, rest), F32)
    zero = jnp.zeros((seq, rest), F32)
    zh = jnp.zeros((seq, half), F32)
    c_h = jnp.concatenate([cos, cos, one], axis=1)
    dn_h = jnp.concatenate([-sin, zh, zero], axis=1)
    up_h = jnp.concatenate([zh, sin, zero], axis=1)
    rep = LANES // HEAD_DIM
    return jnp.concatenate([jnp.tile(c_h, (1, rep)), jnp.tile(dn_h, (1, rep)), jnp.tile(up_h, (1, rep))],
                           axis=1)


def _expand_cmp_weights(pe, w1, w2):
    g_n = NSA_KV_GROUPS
    w1r = w1.astype(BF16).reshape(2, CMP_STRIDE, HEAD_DIM, CMP_HIDDEN)
    w2b = w2.astype(BF16)
    w1e = jnp.concatenate(
        [jnp.stack([w1r if g == e else jnp.zeros_like(w1r) for g in range(g_n)], axis=2)
         .reshape(2, A_W, CMP_HIDDEN) for e in range(g_n)], axis=2)
    w2e = jnp.concatenate(
        [jnp.concatenate([w2b if g == e else jnp.zeros_like(w2b) for e in range(g_n)], axis=1)
         for g in range(g_n)], axis=0)
    per = pe.reshape(2, CMP_STRIDE, 1, HEAD_DIM)
    pee = jnp.broadcast_to(per, (2, CMP_STRIDE, g_n, HEAD_DIM)).reshape(2, A_W)
    pee = jnp.concatenate([pee, jnp.zeros((6, A_W), pe.dtype)], axis=0)
    return pee.astype(F32), w1e.astype(BF16), w2e.astype(BF16)


def kernel(x, mem, w_in, cmp_pe_k, cmp_w1_k, cmp_w2_k, cmp_pe_v, cmp_w1_v, cmp_w2_v, sgu_ln_g, sgu_ln_b,
           sgu_w_s, sgu_b_s, w_mem_kv, w_br_nsa, w_br_sgu, w_br_mem, w_o, ln1_g, ln1_b, w_router_group,
           b_router_group, w_router_expert, b_router_expert, w_exp_gate, w_exp_up, w_exp_down, ln2_g, ln2_b):
    batch, seq, d = x.shape
    t = batch * seq
    assert w_in.shape[0] == DEPTH == 1
    assert seq % PROJ_TM == 0 and seq // CMP_STRIDE == N_A and seq // SLC_BLOCK == N_SEL

    offs = [int(v) for v in np.cumsum(
        [0, Q_W, KV_W, KV_W, KV_W, KV_W, KV_W, KV_W, GATE_W, 2 * SGU_WIDTH, MEM_WIDTH, 3 * d])]
    assert offs[-1] == w_in.shape[2]
    w_t = jnp.transpose(w_in[0])
    seg = lambda i: w_t[offs[i]:offs[i + 1]]
    gate_rows = jnp.pad(seg(7), ((0, LANES - GATE_W), (0, 0)))
    wp_t = jnp.concatenate([seg(0), seg(1), seg(3), seg(5), seg(2), seg(4), seg(6), gate_rows, seg(8), seg(9)],
                           axis=0).astype(BF16)
    w_mt = seg(10).astype(BF16)
    assert wp_t.shape[0] == _PROJ_COLS
    rope = _rope_table(seq)
    tril = jnp.tril(jnp.ones((SGU_CHUNK, SGU_CHUNK), dtype=bool))
    ws = jnp.where(tril[None], sgu_w_s[0], 0.0).astype(BF16)
    bs = jnp.repeat(sgu_b_s[0].T, SGU_WIDTH // SGU_GROUPS, axis=1)

    x2 = x.reshape(t, d)
    (qpad, kc, ksl, kwn, vc, vslt, vwnt, gt, osgu, mq) = _proj_call(
        x2, wp_t, rope, sgu_ln_g[0][None], sgu_ln_b[0][None], ws, bs, seq)

    pek, w1k, w2k = _expand_cmp_weights(cmp_pe_k[0], cmp_w1_k[0], cmp_w2_k[0])
    pev, w1v, w2v = _expand_cmp_weights(cmp_pe_v[0], cmp_w1_v[0], cmp_w2_v[0])
    kcmp, vcmp = _compress_call(kc, vc, pek, pev, w1k, w2k, w1v, w2v, batch)

    nqb = seq // Q_BLOCK
    n_g = NSA_KV_GROUPS
    vcmpt = vcmp.reshape(batch, N_A, KV_W).transpose(0, 2, 1).reshape(batch * KV_W, N_A)
    ci = np.arange(N_A)
    sj = np.arange(N_SEL)
    overlap = ((ci[None, :] * CMP_STRIDE + CMP_LEN - 1 >= sj[:, None] * SLC_BLOCK)
               & (ci[None, :] * CMP_STRIDE <= sj[:, None] * SLC_BLOCK + SLC_BLOCK - 1)
               & (ci[None, :] < (seq - CMP_LEN) // CMP_STRIDE + 1))
    ot = jnp.asarray(overlap, dtype=BF16)
    onsa = _nsa_call(qpad, ksl, kwn, vslt, vwnt, kcmp, vcmpt, gt, ot, batch, seq)

    omem = _memattn_call(mq, mem.reshape(batch * mem.shape[1], d), w_mem_kv[0], batch, seq)

    x1 = _merge_call(x2, onsa, osgu, omem, w_mt, w_br_nsa[0], w_br_sgu[0], w_br_mem[0], w_o[0],
                     ln1_g[0][None], ln1_b[0][None])

    assert EXPERTS_PER_GROUP == _R_SLOT and N_GROUPS <= _R_SLOT
    n_r = _R_SLOT + N_GROUPS * EXPERTS_PER_GROUP
    wr = jnp.concatenate([jnp.pad(w_router_group[0], ((0, 0), (0, _R_SLOT - N_GROUPS))),
                          w_router_expert[0]], axis=1)
    wr = jnp.pad(wr, ((0, 0), (0, LANES - n_r))).astype(BF16)
    br = jnp.concatenate([jnp.pad(b_router_group[0], (0, _R_SLOT - N_GROUPS)), b_router_expert[0]])
    br = jnp.pad(br, (0, LANES - n_r))[None]
    wg = w_exp_gate[0].transpose(0, 2, 1, 3).reshape(N_GROUPS, d, _HID).astype(BF16)
    wu = w_exp_up[0].transpose(0, 2, 1, 3).reshape(N_GROUPS, d, _HID).astype(BF16)
    wd = w_exp_down[0].reshape(N_GROUPS, _HID, d).astype(BF16)
    out = _moe_call(x1, wr, br, wg, wu, wd, ln2_g[0][None], ln2_b[0][None])
    return out.reshape(batch, seq, d)
```

```python
import functools

import numpy as np
import jax
import jax.numpy as jnp
from jax import lax
from jax.experimental import pallas as pl
from jax.experimental.pallas import tpu as pltpu

NSA_HEADS = 8
NSA_KV_GROUPS = 2
NSA_HPG = NSA_HEADS // NSA_KV_GROUPS
HEAD_DIM = 64
CMP_LEN = 32
CMP_STRIDE = 16
CMP_HIDDEN = 256
SLC_BLOCK = 64
SLC_TOPN = 8
WINDOW = 512
Q_BLOCK = 128
N_BAND = WINDOW // Q_BLOCK
ROPE_THETA = 500000.0
ROPE_DIM = HEAD_DIM // 4
SGU_CHUNK = 128
SGU_GROUPS = 8
SGU_WIDTH = 512
MEM_HEADS = 4
MEM_HEAD_DIM = 128
MEM_WIDTH = MEM_HEADS * MEM_HEAD_DIM
N_GROUPS = 4
EXPERTS_PER_GROUP = 8
EXPERT_FF = 256
DEPTH = 1
DN_ALPHA = (2.0 * DEPTH) ** 0.25
LN_EPS = 1e-5
NEG = -1e30
LOG2E = 1.4426950408889634

LANES = 128
Q_W = NSA_HEADS * HEAD_DIM
KV_W = NSA_KV_GROUPS * HEAD_DIM
GATE_W = NSA_HEADS * 3
VMEM_LIMIT = 56 * 1024 * 1024
MOE_VMEM_LIMIT = 60 * 1024 * 1024

BF16 = jnp.bfloat16
F32 = jnp.float32


def _dot(a, b):
    return jnp.dot(a, b, preferred_element_type=F32)


def _dot_nt(a, b):
    return lax.dot_general(a, b, (((1,), (1,)), ((), ())), preferred_element_type=F32)


def _sigmoid(x):
    return 1.0 / (1.0 + jnp.exp(-x))


def _gelu(x):
    return 0.5 * x * (1.0 + lax.erf(x * (2.0 ** -0.5)))


def _layer_norm(x, g, b):
    mu = jnp.mean(x, axis=-1, keepdims=True)
    xc = x - mu
    var = jnp.mean(xc * xc, axis=-1, keepdims=True)
    return xc * lax.rsqrt(var + LN_EPS) * g + b


PROJ_TM = 512
_ROPE_COLS = Q_W + 3 * KV_W
_V_OFF = _ROPE_COLS
_G_OFF = _V_OFF + 3 * KV_W
_SGU_OFF = _G_OFF + LANES
_MQ_OFF = _SGU_OFF + 2 * SGU_WIDTH
_PROJ_COLS = _MQ_OFF + MEM_WIDTH


def _store_strided_rows(val, out_ref, tmp_ref):
    tmp_ref[...] = val
    n = val.shape[0] // CMP_STRIDE
    for l in range(CMP_STRIDE):
        out_ref[:, l * LANES:(l + 1) * LANES] = tmp_ref[pl.ds(l, n, stride=CMP_STRIDE), :].astype(BF16)


def _transpose_weight(wt_ref, w_s):
    for j in range(wt_ref.shape[0] // LANES):
        rows = slice(j * LANES, (j + 1) * LANES)
        w_s[:, rows] = wt_ref[rows, :].astype(F32).T.astype(BF16)


def _proj_kernel(x_ref, wt_ref, rope_ref, lng_ref, lnb_ref, ws_ref, bs_ref,
                 q_ref, kc_ref, ksl_ref, kwn_ref, vc_ref, vsl_ref, vwn_ref, gate_ref, osgu_ref, mq_ref,
                 a_s, w_ref):
    pl.when(pl.program_id(0) == 0)(functools.partial(_transpose_weight, wt_ref, w_ref))
    xb = x_ref[...].astype(BF16)
    lane = lax.broadcasted_iota(jnp.int32, (PROJ_TM, LANES), 1)
    low = lane < HEAD_DIM
    cos = rope_ref[:, 0:LANES]
    s_dn = rope_ref[:, LANES:2 * LANES]
    s_up = rope_ref[:, 2 * LANES:3 * LANES]

    hz = _dot(xb, w_ref[:, _SGU_OFF:_MQ_OFF])
    h = _dot(xb, w_ref[:, 0:_ROPE_COLS])
    k_refs = (kc_ref, ksl_ref, kwn_ref)
    for j in range(_ROPE_COLS // LANES):
        blk = h[:, j * LANES:(j + 1) * LANES]
        r = (blk * cos + pltpu.roll(blk, LANES - ROPE_DIM // 2, 1) * s_dn
             + pltpu.roll(blk, ROPE_DIM // 2, 1) * s_up)
        if j < Q_W // LANES:
            r = r * (HEAD_DIM ** -0.5 * LOG2E)
            sw = pltpu.roll(r, HEAD_DIM, 1)
            g = (2 * j) // NSA_HPG
            if g == 0:
                h0 = jnp.where(low, r, 0.0)
                h1 = jnp.where(low, sw, 0.0)
            else:
                h0 = jnp.where(low, 0.0, sw)
                h1 = jnp.where(low, 0.0, r)
            q_ref[:, (2 * j) * LANES:(2 * j + 1) * LANES] = h0.astype(BF16)
            q_ref[:, (2 * j + 1) * LANES:(2 * j + 2) * LANES] = h1.astype(BF16)
        else:
            if j == Q_W // LANES:
                _store_strided_rows(r, kc_ref, a_s)
            else:
                k_refs[j - Q_W // LANES][...] = r.astype(BF16)

    _store_strided_rows(_dot(xb, w_ref[:, _V_OFF:_V_OFF + LANES]), vc_ref, a_s)
    ones = jnp.ones((V_ROWS - HEAD_DIM, PROJ_TM), BF16)
    for v_ref, row0, blk in ((vsl_ref, _V_OFF + LANES, SEL_CHUNK), (vwn_ref, _V_OFF + 2 * LANES, Q_BLOCK)):
        vt = _dot_nt(wt_ref[row0:row0 + LANES, :], xb).astype(BF16)
        for g in range(NSA_KV_GROUPS):
            for b in range(PROJ_TM // blk):
                v_ref[g, b, 0:HEAD_DIM, :] = vt[g * HEAD_DIM:(g + 1) * HEAD_DIM, b * blk:(b + 1) * blk]
                v_ref[g, b, HEAD_DIM:V_ROWS, :] = ones[:, 0:blk]

    gate_ref[...] = _sigmoid(_dot_nt(wt_ref[_G_OFF:_SGU_OFF, :], xb))

    mq_ref[...] = _dot(xb, w_ref[:, _MQ_OFF:_PROJ_COLS]).astype(BF16)

    z = _gelu(hz)
    u = z[:, 0:SGU_WIDTH]
    v = _layer_norm(z[:, SGU_WIDTH:2 * SGU_WIDTH], lng_ref[...], lnb_ref[...]).astype(BF16)
    lane_c = lax.broadcasted_iota(jnp.int32, (SGU_CHUNK, LANES), 1)
    low_c = lane_c < (SGU_WIDTH // SGU_GROUPS)
    n_ch = PROJ_TM // SGU_CHUNK
    for gp in range(SGU_WIDTH // LANES):
        cols = slice(gp * LANES, (gp + 1) * LANES)
        vcat = jnp.concatenate([v[ci * SGU_CHUNK:(ci + 1) * SGU_CHUNK, cols] for ci in range(n_ch)], axis=1)
        sv0 = _dot(ws_ref[2 * gp], vcat)
        sv1 = _dot(ws_ref[2 * gp + 1], vcat)
        for ci in range(n_ch):
            rows = slice(ci * SGU_CHUNK, (ci + 1) * SGU_CHUNK)
            lanes = slice(ci * LANES, (ci + 1) * LANES)
            sv = jnp.where(low_c, sv0[:, lanes], sv1[:, lanes]) + bs_ref[:, cols]
            osgu_ref[rows, cols] = (u[rows, cols] * sv).astype(BF16)


def _proj_call(x2, wp, rope, lng, lnb, ws, bs, seq):
    t = x2.shape[0]
    d = x2.shape[1]
    nt = t // PROJ_TM
    per_seq = seq // PROJ_TM
    row = lambda i: (i, 0)
    const2 = lambda i: (0, 0)
    assert PROJ_TM % SEL_CHUNK == 0 and PROJ_TM % Q_BLOCK == 0
    n_g = NSA_KV_GROUPS
    batch = t // seq
    vblock = lambda i: (i // per_seq, i % per_seq, 0, 0)

    def v_shape(blk):
        return jax.ShapeDtypeStruct((batch * n_g, seq // blk, V_ROWS, blk), BF16)

    def v_spec(blk):
        return pl.BlockSpec((n_g, PROJ_TM // blk, V_ROWS, blk), vblock)

    a_shape = jax.ShapeDtypeStruct((t // CMP_STRIDE, CMP_STRIDE * KV_W), BF16)
    a_spec = pl.BlockSpec((PROJ_TM // CMP_STRIDE, CMP_STRIDE * KV_W), row)
    k_shape = jax.ShapeDtypeStruct((t, KV_W), BF16)
    k_spec = pl.BlockSpec((PROJ_TM, KV_W), row)
    out_shapes = (
        jax.ShapeDtypeStruct((t, NSA_HEADS * LANES), BF16),
        a_shape, k_shape, k_shape, a_shape,
        v_shape(SEL_CHUNK), v_shape(Q_BLOCK),
        jax.ShapeDtypeStruct((LANES, t), F32),
        jax.ShapeDtypeStruct((t, SGU_WIDTH), BF16),
        jax.ShapeDtypeStruct((t, MEM_WIDTH), BF16),
    )
    out_specs = (
        pl.BlockSpec((PROJ_TM, NSA_HEADS * LANES), row),
        a_spec, k_spec, k_spec, a_spec,
        v_spec(SEL_CHUNK), v_spec(Q_BLOCK),
        pl.BlockSpec((LANES, PROJ_TM), lambda i: (0, i)),
        pl.BlockSpec((PROJ_TM, SGU_WIDTH), row),
        pl.BlockSpec((PROJ_TM, MEM_WIDTH), row),
    )
    return pl.pallas_call(
        _proj_kernel,
        grid=(nt,),
        in_specs=[
            pl.BlockSpec((PROJ_TM, d), row),
            pl.BlockSpec((_PROJ_COLS, d), const2),
            pl.BlockSpec((PROJ_TM, 3 * LANES), lambda i: (i % per_seq, 0)),
            pl.BlockSpec((1, SGU_WIDTH), const2),
            pl.BlockSpec((1, SGU_WIDTH), const2),
            pl.BlockSpec((SGU_GROUPS, SGU_CHUNK, SGU_CHUNK), lambda i: (0, 0, 0)),
            pl.BlockSpec((SGU_CHUNK, SGU_WIDTH), const2),
        ],
        out_specs=out_specs,
        out_shape=out_shapes,
        scratch_shapes=[pltpu.VMEM((PROJ_TM, KV_W), F32), pltpu.VMEM((d, _PROJ_COLS), BF16)],
        compiler_params=pltpu.CompilerParams(
            dimension_semantics=("arbitrary",), vmem_limit_bytes=VMEM_LIMIT),
        name="proj",
    )(x2, wp, rope, lng, lnb, ws, bs)


N_A = 128
A_W = CMP_STRIDE * KV_W


def _compress_kernel(ka_ref, va_ref, pek_ref, pev_ref, w1k_ref, w2k_ref, w1v_ref, w2v_ref,
                     kcmp_ref, vcmp_ref):
    def one(a_ref, pe_ref, w1_ref, w2_ref, out_ref):
        a = a_ref[...].astype(F32)
        top = (a + pe_ref[0:1, :]).astype(BF16)
        bot = (a + pe_ref[1:2, :]).astype(BF16)
        h1 = _dot(top, w1_ref[0])
        h2 = _dot(bot, w1_ref[1])
        pre = h1 + pltpu.roll(h2, N_A - 1, 0)
        act = _gelu(pre).astype(BF16)
        out_ref[...] = _dot(act, w2_ref[...]).astype(BF16)

    one(ka_ref, pek_ref, w1k_ref, w2k_ref, kcmp_ref)
    one(va_ref, pev_ref, w1v_ref, w2v_ref, vcmp_ref)


def _compress_call(ka, va, pek, pev, w1k, w2k, w1v, w2v, batch):
    row = lambda b: (b, 0)
    c2 = lambda b: (0, 0)
    c3 = lambda b: (0, 0, 0)
    hid2 = NSA_KV_GROUPS * CMP_HIDDEN
    return pl.pallas_call(
        _compress_kernel,
        grid=(batch,),
        in_specs=[
            pl.BlockSpec((N_A, A_W), row),
            pl.BlockSpec((N_A, A_W), row),
            pl.BlockSpec((8, A_W), c2),
            pl.BlockSpec((8, A_W), c2),
            pl.BlockSpec((2, A_W, hid2), c3),
            pl.BlockSpec((hid2, KV_W), c2),
            pl.BlockSpec((2, A_W, hid2), c3),
            pl.BlockSpec((hid2, KV_W), c2),
        ],
        out_specs=(pl.BlockSpec((N_A, KV_W), row), pl.BlockSpec((N_A, KV_W), row)),
        out_shape=(jax.ShapeDtypeStruct((batch * N_A, KV_W), BF16),
                   jax.ShapeDtypeStruct((batch * N_A, KV_W), BF16)),
        compiler_params=pltpu.CompilerParams(
            dimension_semantics=("parallel",), vmem_limit_bytes=VMEM_LIMIT),
        name="compress",
    )(ka, va, pek, pev, w1k, w2k, w1v, w2v)


N_SEL = 32
SEL_CHUNK = 256
SEL_SUB = 128
HQ = NSA_HPG * Q_BLOCK
V_ROWS = HEAD_DIM + 16


def _nsa_kernel(*refs):
    ksl_ref = refs[1]
    reach = pl.program_id(1) // (SEL_CHUNK // Q_BLOCK)
    for n in range(ksl_ref.shape[0] // SEL_CHUNK):
        pl.when(reach == n)(functools.partial(_nsa_body, n, *refs))


def _nsa_body(n_later, q_ref, ksl_ref, kwn_ref, vslt_ref, vwnt_ref, kcmp_ref, vcmpt_ref, gt_ref,
              ot_ref, o_ref, m_s, acc_s, ocmp_s, owin_s, cap_s, out_s):
    c = pl.program_id(1)
    lane_hq = lax.broadcasted_iota(jnp.int32, (1, HQ), 1)
    pos_hq = c * Q_BLOCK + (lane_hq & (Q_BLOCK - 1))
    pos_q = c * Q_BLOCK + lax.broadcasted_iota(jnp.int32, (1, Q_BLOCK), 1)
    groups = range(NSA_KV_GROUPS)
    qgs = [jnp.concatenate(
        [q_ref[:, (g * NSA_HPG + hh) * LANES:(g * NSA_HPG + hh + 1) * LANES]
         for hh in range(NSA_HPG)], axis=0) for g in groups]

    kcmp = kcmp_ref[...]
    s_c = [_dot_nt(kcmp, qgs[g]) for g in groups]
    n_idx = lax.broadcasted_iota(jnp.int32, (N_A, HQ), 0)
    valid_c = (n_idx * CMP_STRIDE + (CMP_LEN - 1)) <= pos_hq
    p_c = []
    for g in groups:
        sm_c = jnp.where(valid_c, s_c[g], NEG)
        m_c = jnp.max(sm_c, axis=0, keepdims=True)
        e_c = jnp.where(valid_c, jnp.exp2(sm_c - m_c), 0.0)
        d_c = jnp.sum(e_c, axis=0, keepdims=True)
        p_c.append(e_c / jnp.where(d_c > 0, d_c, 1.0))
    for g in groups:
        ocmp_s[g] = _dot(vcmpt_ref[g * HEAD_DIM:(g + 1) * HEAD_DIM, :], p_c[g].astype(BF16))

    ot = ot_ref[...]
    imp = []
    for g in groups:
        ps = (p_c[g][:, 0:Q_BLOCK] + p_c[g][:, Q_BLOCK:2 * Q_BLOCK]
              + p_c[g][:, 2 * Q_BLOCK:3 * Q_BLOCK] + p_c[g][:, 3 * Q_BLOCK:4 * Q_BLOCK])
        p_hi = ps.astype(BF16)
        r1 = ps - p_hi.astype(F32)
        p_lo = r1.astype(BF16)
        p_lo2 = (r1 - p_lo.astype(F32)).astype(BF16)
        imp.append(_dot(ot, p_hi) + _dot(ot, p_lo) + _dot(ot, p_lo2))

    j_idx = lax.broadcasted_iota(jnp.int32, (N_SEL, Q_BLOCK), 0)
    cur = pos_q // SLC_BLOCK
    future = j_idx > cur
    forced = (j_idx == 0) | (j_idx == cur) | (j_idx == cur - 1)
    imp = [jnp.where(future, NEG, jnp.where(forced, -NEG, imp[g])) for g in groups]
    rank = [jnp.zeros((N_SEL, Q_BLOCK), F32) for g in groups]
    for i in range(N_SEL):
        for g in groups:
            row = imp[g][i:i + 1, :]
            beats = (row > imp[g]) | ((row == imp[g]) & (j_idx > i))
            rank[g] = rank[g] + jnp.where(beats, 1.0, 0.0)
    for g in groups:
        cap_s[g] = jnp.where(rank[g] < float(SLC_TOPN), -NEG, NEG)
        m_s[g] = jnp.full((1, HQ), NEG, F32)
        acc_s[g] = jnp.zeros((V_ROWS, HQ), F32)

    blocks_per_sub = SEL_SUB // SLC_BLOCK
    subs_per_chunk = SEL_CHUNK // SEL_SUB

    def sel_spans(kc):
        return [slice(kc * SEL_CHUNK + a * SEL_SUB, kc * SEL_CHUNK + (a + 1) * SEL_SUB)
                for a in range(subs_per_chunk)]

    def sel_scores(kc):
        return [[_dot_nt(ksl_ref[ks, :], qgs[g]) for g in groups] for ks in sel_spans(kc)]

    def sel_chunk(kc, s=None):
        spans = sel_spans(kc)
        s = sel_scores(kc) if s is None else s
        for a, ks in enumerate(spans):
            kpos = ks.start + lax.broadcasted_iota(jnp.int32, (SEL_SUB, Q_BLOCK), 0)
            causal = kpos <= pos_q
            for g in groups:
                j0 = ks.start // SLC_BLOCK
                cap = jnp.concatenate(
                    [jnp.broadcast_to(cap_s[g, j:j + 1, :], (SLC_BLOCK, Q_BLOCK))
                     for j in range(j0, j0 + blocks_per_sub)], axis=0)
                cap = jnp.where(causal, cap, NEG)
                sm = jnp.minimum(s[a][g], jnp.concatenate([cap] * NSA_HPG, axis=1))
                m_old = m_s[g]
                m_new = jnp.maximum(m_old, jnp.max(sm, axis=0, keepdims=True))
                alpha = jnp.exp2(m_old - m_new)
                e = jnp.exp2((sm - m_new).astype(BF16))
                v_t = vslt_ref[g, kc, :, a * SEL_SUB:(a + 1) * SEL_SUB]
                acc_s[g] = alpha * acc_s[g] + _dot(v_t, e)
                m_s[g] = m_new

    s0 = sel_scores(0)
    _nsa_window(c, qgs, kwn_ref, vwnt_ref, owin_s)
    s_next = sel_scores(1) if n_later else None
    sel_chunk(0, s0)
    for kc in range(1, n_later + 1):
        s_cur = s_next
        if kc < n_later:
            s_next = sel_scores(kc + 1)
        sel_chunk(kc, s_cur)

    for g in groups:
        acc = acc_s[g]
        o_sel = acc[0:HEAD_DIM, :] * (1.0 / acc[HEAD_DIM:HEAD_DIM + 1, :])

        def gate_row(br):
            return jnp.concatenate(
                [gt_ref[(g * NSA_HPG + hh) * 3 + br:(g * NSA_HPG + hh) * 3 + br + 1, :]
                 for hh in range(NSA_HPG)], axis=1)
        o_t = gate_row(0) * ocmp_s[g] + gate_row(1) * o_sel + gate_row(2) * owin_s[g]
        for hh in range(NSA_HPG):
            h = g * NSA_HPG + hh
            out_s[h * HEAD_DIM:(h + 1) * HEAD_DIM, :] = o_t[:, hh * Q_BLOCK:(hh + 1) * Q_BLOCK]

    o_ref[...] = out_s[...].T.astype(BF16)


def _nsa_window(c, qgs, kwn_ref, vwnt_ref, owin_s):
    groups = range(NSA_KV_GROUPS)
    q_i = lax.broadcasted_iota(jnp.int32, (Q_BLOCK, Q_BLOCK), 1)
    k_i = lax.broadcasted_iota(jnp.int32, (Q_BLOCK, Q_BLOCK), 0)
    blks, caps = [], []
    for i in range(N_BAND + 1):
        blk = c - N_BAND + i
        blks.append(jnp.maximum(blk, 0))
        off = (N_BAND - i) * Q_BLOCK
        if off - (Q_BLOCK - 1) >= 0 and off + (Q_BLOCK - 1) < WINDOW:
            caps.append(jnp.where(blk >= 0, -NEG, NEG))
        else:
            diff = off + q_i - k_i
            ok = (diff >= 0) & (diff < WINDOW) & (blk >= 0)
            caps.append(jnp.concatenate([jnp.where(ok, -NEG, NEG)] * NSA_HPG, axis=1))
    k_blocks = [kwn_ref[pl.ds(pl.multiple_of(blks[i] * Q_BLOCK, Q_BLOCK), Q_BLOCK), :]
                for i in range(N_BAND + 1)]
    s_w = [[None] * (N_BAND + 1) for g in groups]
    for i in reversed(range(N_BAND + 1)):
        for g in groups:
            s_w[g][i] = _dot_nt(k_blocks[i], qgs[g])
    m_w = [None for g in groups]
    o_win = [None for g in groups]
    for i in reversed(range(N_BAND + 1)):
        for g in groups:
            sm = jnp.minimum(s_w[g][i], caps[i])
            m_blk = jnp.max(sm, axis=0, keepdims=True)
            m_new = m_blk if m_w[g] is None else jnp.maximum(m_w[g], m_blk)
            pv = _dot(vwnt_ref[g, blks[i]], jnp.exp2((sm - m_new).astype(BF16)))
            o_win[g] = pv if m_w[g] is None else jnp.exp2(m_w[g] - m_new) * o_win[g] + pv
            m_w[g] = m_new
    for g in groups:
        owin_s[g] = o_win[g][0:HEAD_DIM, :] * (1.0 / o_win[g][HEAD_DIM:HEAD_DIM + 1, :])


def _nsa_call(qpad, ksl, kwn, vslt, vwnt, kcmp, vcmpt, gt, ot, batch, seq):
    nqb = seq // Q_BLOCK
    n_g = NSA_KV_GROUPS
    qrow = lambda b, c: (b * nqb + c, 0)
    brow = lambda b, c: (b, 0)
    c2 = lambda b, c: (0, 0)
    return pl.pallas_call(
        _nsa_kernel,
        grid=(batch, nqb),
        in_specs=[
            pl.BlockSpec((Q_BLOCK, NSA_HEADS * LANES), qrow),
            pl.BlockSpec((seq, KV_W), brow),
            pl.BlockSpec((seq, KV_W), brow),
            pl.BlockSpec((n_g, seq // SEL_CHUNK, V_ROWS, SEL_CHUNK), lambda b, c: (b, 0, 0, 0)),
            pl.BlockSpec((n_g, nqb, V_ROWS, Q_BLOCK), lambda b, c: (b, 0, 0, 0)),
            pl.BlockSpec((N_A, KV_W), brow),
            pl.BlockSpec((KV_W, N_A), brow),
            pl.BlockSpec((LANES, Q_BLOCK), lambda b, c: (0, b * nqb + c)),
            pl.BlockSpec((N_SEL, N_A), c2),
        ],
        out_specs=pl.BlockSpec((Q_BLOCK, Q_W), qrow),
        out_shape=jax.ShapeDtypeStruct((batch * seq, Q_W), BF16),
        scratch_shapes=[
            pltpu.VMEM((n_g, 1, HQ), F32),
            pltpu.VMEM((n_g, V_ROWS, HQ), F32),
            pltpu.VMEM((n_g, HEAD_DIM, HQ), F32),
            pltpu.VMEM((n_g, HEAD_DIM, HQ), F32),
            pltpu.VMEM((n_g, N_SEL, Q_BLOCK), F32),
            pltpu.VMEM((Q_W, Q_BLOCK), F32),
        ],
        compiler_params=pltpu.CompilerParams(
            dimension_semantics=("parallel", "arbitrary"), vmem_limit_bytes=VMEM_LIMIT),
        name="nsa",
    )(qpad, ksl, kwn, vslt, vwnt, kcmp, vcmpt, gt, ot)


MEM_TM = 512
MEM_V_ROWS = MEM_HEAD_DIM + 16


def _memattn_kernel(mq_ref, mem_ref, wkv_ref, o_ref, k_s, vt_s, out_s):
    @pl.when(pl.program_id(1) == 0)
    def _():
        kv = _dot(mem_ref[...].astype(BF16), wkv_ref[...].astype(BF16))
        k_s[...] = kv[:, 0:MEM_WIDTH].astype(BF16)
        vt = kv[:, MEM_WIDTH:2 * MEM_WIDTH].T.astype(BF16)
        for h in range(MEM_HEADS):
            vt_s[h, 0:MEM_HEAD_DIM, :] = vt[h * MEM_HEAD_DIM:(h + 1) * MEM_HEAD_DIM, :]
            vt_s[h, MEM_HEAD_DIM:, :] = jnp.ones((MEM_V_ROWS - MEM_HEAD_DIM, vt.shape[1]), BF16)

    heads = range(MEM_HEADS)
    cols = [slice(h * MEM_HEAD_DIM, (h + 1) * MEM_HEAD_DIM) for h in heads]
    s_t = [_dot_nt(k_s[:, cols[h]], mq_ref[:, cols[h]]) for h in heads]
    e = []
    for h in heads:
        m = jnp.max(s_t[h], axis=0, keepdims=True)
        e.append(jnp.exp2((s_t[h] - m) * (MEM_HEAD_DIM ** -0.5 * LOG2E)).astype(BF16))
    o_t = [_dot(vt_s[h], e[h]) for h in heads]
    for h in heads:
        out_s[cols[h], :] = o_t[h][0:MEM_HEAD_DIM, :] * (1.0 / o_t[h][MEM_HEAD_DIM:MEM_HEAD_DIM + 1, :])
    o_ref[...] = out_s[...].T.astype(BF16)


def _memattn_call(mq, mem2, wkv, batch, seq):
    m_len = mem2.shape[0] // batch
    d = mem2.shape[1]
    per = seq // MEM_TM
    return pl.pallas_call(
        _memattn_kernel,
        grid=(batch, per),
        in_specs=[
            pl.BlockSpec((MEM_TM, MEM_WIDTH), lambda b, i: (b * per + i, 0)),
            pl.BlockSpec((m_len, d), lambda b, i: (b, 0)),
            pl.BlockSpec((d, 2 * MEM_WIDTH), lambda b, i: (0, 0)),
        ],
        out_specs=pl.BlockSpec((MEM_TM, MEM_WIDTH), lambda b, i: (b * per + i, 0)),
        out_shape=jax.ShapeDtypeStruct((batch * seq, MEM_WIDTH), BF16),
        scratch_shapes=[
            pltpu.VMEM((m_len, MEM_WIDTH), BF16),
            pltpu.VMEM((MEM_HEADS, MEM_V_ROWS, m_len), BF16),
            pltpu.VMEM((MEM_WIDTH, MEM_TM), F32),
        ],
        compiler_params=pltpu.CompilerParams(
            dimension_semantics=("parallel", "arbitrary"), vmem_limit_bytes=VMEM_LIMIT),
        name="memattn",
    )(mq, mem2, wkv)


MERGE_TM = 512


def _merge_kernel(x_ref, onsa_ref, osgu_ref, omem_ref, wgt_ref, wbn32_ref, wbs32_ref, wbm32_ref, wo32_ref,
                  g_ref, b_ref, out_ref, wbn_ref, wbs_ref, wbm_ref, wo_ref, wg_ref):
    @pl.when(pl.program_id(0) == 0)
    def _():
        for dst, src in ((wbn_ref, wbn32_ref), (wbs_ref, wbs32_ref), (wbm_ref, wbm32_ref),
                         (wo_ref, wo32_ref)):
            dst[...] = src[...].astype(BF16)
        _transpose_weight(wgt_ref, wg_ref)

    d = x_ref.shape[1]
    branches = ((onsa_ref, wbn_ref), (osgu_ref, wbs_ref), (omem_ref, wbm_ref))
    halves = [slice(h * (MERGE_TM // 2), (h + 1) * (MERGE_TM // 2)) for h in range(2)]
    xs = [x_ref[rows, :] for rows in halves]
    xbs = [x.astype(BF16) for x in xs]
    logits = [[_dot(xb, wg_ref[:, br * d:(br + 1) * d]) for br in range(len(branches))] for xb in xbs]
    projs = [[_dot(o_r[rows, :], w_r[...]) for o_r, w_r in branches] for rows in halves]
    ys = []
    for h in range(2):
        y = None
        for br in range(len(branches)):
            term = _sigmoid(logits[h][br]) * projs[h][br]
            y = term if y is None else y + term
        ys.append(y.astype(BF16))
    outs = [_dot(y, wo_ref[...]) for y in ys]
    for h, rows in enumerate(halves):
        out_ref[rows, :] = _layer_norm(DN_ALPHA * xs[h] + outs[h], g_ref[...], b_ref[...])


def _merge_call(x2, onsa, osgu, omem, wg, wbn, wbs, wbm, wo, g1, b1):
    t, d = x2.shape
    row = lambda i: (i, 0)
    c2 = lambda i: (0, 0)
    full = lambda a: pl.BlockSpec(a.shape, c2)
    once = lambda a: pl.BlockSpec(a.shape, c2, pipeline_mode=pl.Buffered(1))
    return pl.pallas_call(
        _merge_kernel,
        grid=(t // MERGE_TM,),
        in_specs=[
            pl.BlockSpec((MERGE_TM, d), row),
            pl.BlockSpec((MERGE_TM, onsa.shape[1]), row),
            pl.BlockSpec((MERGE_TM, osgu.shape[1]), row),
            pl.BlockSpec((MERGE_TM, omem.shape[1]), row),
            pl.BlockSpec((3 * d, d), c2, pipeline_mode=pl.Buffered(1)),
            once(wbn), once(wbs), once(wbm), once(wo), full(g1), full(b1),
        ],
        out_specs=pl.BlockSpec((MERGE_TM, d), row),
        out_shape=jax.ShapeDtypeStruct((t, d), F32),
        scratch_shapes=[pltpu.VMEM(a.shape, BF16) for a in (wbn, wbs, wbm, wo)]
        + [pltpu.VMEM((d, 3 * d), BF16)],
        compiler_params=pltpu.CompilerParams(
            dimension_semantics=("arbitrary",), vmem_limit_bytes=VMEM_LIMIT),
        name="merge",
    )(x2, onsa, osgu, omem, wg, wbn, wbs, wbm, wo, g1, b1)


MOE_TM = 1024
MOE_CH = 144
_YS_ROWS = -(-(MOE_TM + N_GROUPS * MOE_CH) // 256) * 256
_R_SLOT = 8
_DEST_LANE = 3 * _R_SLOT
_HID = EXPERTS_PER_GROUP * EXPERT_FF


def _moe_kernel(x_ref, tri_ref, wr_ref, br_ref, wg_ref, wu_ref, wd_ref, g_ref, b_ref, out_ref,
                xa_s, tok_s, keyr_s, ys_s, cnt_s, base_s):
    grp = pl.program_id(1)
    tm = x_ref.shape[0]
    d = x_ref.shape[1]

    @pl.when(grp == 0)
    def _route():
        xb = x_ref[...].astype(BF16)
        xa_s[:, 0:d] = xb
        lt = _dot_nt(wr_ref[...], xb) + br_ref[...]
        row = lax.broadcasted_iota(jnp.int32, (_R_SLOT, tm), 0)
        gl = jnp.where(row < N_GROUPS, lt[0:_R_SLOT], NEG)
        gmax = jnp.max(gl, axis=0, keepdims=True)
        gidx = jnp.min(jnp.where(gl == gmax, row, _R_SLOT), axis=0, keepdims=True)
        gprob = 1.0 / jnp.sum(jnp.exp(gl - gmax), axis=0, keepdims=True)
        el = lt[_R_SLOT:2 * _R_SLOT]
        for k in range(1, N_GROUPS):
            el = jnp.where(gidx == k, lt[(k + 1) * _R_SLOT:(k + 2) * _R_SLOT], el)
        ee = jnp.exp(el - jnp.max(el, axis=0, keepdims=True))
        ep = ee / jnp.sum(ee, axis=0, keepdims=True)
        t1 = jnp.max(ep, axis=0, keepdims=True)
        i1 = jnp.min(jnp.where(ep == t1, row, _R_SLOT), axis=0, keepdims=True)
        rest = row != i1
        t2 = jnp.max(jnp.where(rest, ep, -1.0), axis=0, keepdims=True)
        i2 = jnp.min(jnp.where(rest & (ep == t2), row, _R_SLOT), axis=0, keepdims=True)
        den = t1 + t2
        cwf = (jnp.where(row == i1, t1 / den, 0.0) + jnp.where(row == i2, t2 / den, 0.0)) * gprob
        hi = cwf.astype(BF16).astype(F32)
        r1 = cwf - hi
        lo = r1.astype(BF16).astype(F32)
        lo2 = (r1 - lo).astype(BF16).astype(F32)

        onehot = jnp.where(row == gidx, 1.0, 0.0)
        tri = tri_ref[...]
        onehot_b = onehot.astype(BF16)
        ranks, before = [], jnp.zeros((_R_SLOT, 1), F32)
        for b in range(tm // LANES):
            blk = slice(b * LANES, (b + 1) * LANES)
            ranks.append(_dot_nt(onehot_b[:, blk], tri) + before)
            before = before + jnp.sum(onehot[:, blk], axis=1, keepdims=True)
        rank = jnp.concatenate(ranks, axis=1)
        keyr_s[...] = jnp.where(onehot > 0.5, rank, -1.0)
        base = jnp.int32(0)
        basev = jnp.zeros((1, tm), F32)
        for k in range(N_GROUPS):
            n_k = jnp.sum(onehot[k:k + 1, :]).astype(jnp.int32)
            cnt_s[k] = n_k
            base_s[k] = base
            basev = jnp.where(gidx == k, base.astype(F32), basev)
            base = base + ((n_k + MOE_CH - 1) // MOE_CH) * MOE_CH
        dest = jnp.sum(rank * onehot, axis=0, keepdims=True) + basev
        tok = jnp.concatenate(
            [hi, lo, lo2, jnp.broadcast_to(dest, (_R_SLOT, tm)),
             jnp.zeros((LANES - 4 * _R_SLOT, tm), F32)], axis=0).T
        tok_s[...] = tok
        lane = lax.broadcasted_iota(jnp.int32, tok.shape, 1)
        xa_s[:, d:d + LANES] = jnp.where(lane < _DEST_LANE, tok, 0.0).astype(BF16)
        ys_s[...] = jnp.zeros(ys_s.shape, BF16)

    n_rows = cnt_s[grp]
    row0 = base_s[grp]
    keyr = keyr_s[pl.ds(grp, 1), :]
    half = _HID // 2

    def sweep(k, carry):
        ch = MOE_CH
        r_row = (lax.broadcasted_iota(jnp.int32, (ch, tm), 0) + k * MOE_CH).astype(F32)
        pick = jnp.where(keyr == r_row, 1.0, 0.0).astype(BF16)
        ga = _dot(pick, xa_s[...])
        xg = ga[:, 0:d].astype(BF16)
        cwg = ga[:, d:d + LANES]
        cs = (cwg + pltpu.roll(cwg, LANES - _R_SLOT, 1)
              + pltpu.roll(cwg, LANES - 2 * _R_SLOT, 1))
        hg = [_dot(xg, wg_ref[0, :, h * half:(h + 1) * half]) for h in range(2)]
        hu = [_dot(xg, wu_ref[0, :, h * half:(h + 1) * half]) for h in range(2)]
        y = None
        for h in range(2):
            per_half = EXPERTS_PER_GROUP // 2
            cexp = jnp.concatenate(
                [jnp.broadcast_to(cs[:, e:e + 1], (ch, EXPERT_FF))
                 for e in range(h * per_half, (h + 1) * per_half)], axis=1)
            hid = ((hg[h] * _sigmoid(hg[h])) * hu[h]) * cexp
            term = _dot(hid.astype(BF16), wd_ref[0, h * half:(h + 1) * half, :])
            y = term if y is None else y + term
        ys_s[pl.ds(pl.multiple_of(row0 + k * MOE_CH, 16), ch), :] = y.astype(BF16)
        return carry

    lax.fori_loop(0, (n_rows + MOE_CH - 1) // MOE_CH, sweep, 0)

    @pl.when(grp == N_GROUPS - 1)
    def _fin():
        n_q = 4
        q_rows = tm // n_q
        r_col = lax.broadcasted_iota(jnp.int32, (q_rows, _YS_ROWS), 1).astype(F32)
        parts = [slice(i * q_rows, (i + 1) * q_rows) for i in range(n_q)]
        puts = [jnp.where(tok_s[rows, _DEST_LANE:_DEST_LANE + 1] == r_col, 1.0, 0.0).astype(BF16)
                for rows in parts]
        fs = [_dot(put, ys_s[...]) for put in puts]
        for rows, f in zip(parts, fs):
            out_ref[rows, :] = _layer_norm(DN_ALPHA * x_ref[rows, :] + f, g_ref[...], b_ref[...])


def _moe_call(x1, wr, br, wg, wu, wd, g2, b2):
    t, d = x1.shape
    row = lambda i, g: (i, 0)
    c2 = lambda i, g: (0, 0)
    idx = np.arange(LANES)
    tri = jnp.asarray(idx[None, :] < idx[:, None], dtype=BF16)
    return pl.pallas_call(
        _moe_kernel,
        grid=(t // MOE_TM, N_GROUPS),
        in_specs=[
            pl.BlockSpec((MOE_TM, d), row, pipeline_mode=pl.Buffered(1)),
            pl.BlockSpec((LANES, LANES), c2),
            pl.BlockSpec(wr.shape, c2),
            pl.BlockSpec(br.shape, c2),
            pl.BlockSpec((1, d, _HID), lambda i, g: (g, 0, 0)),
            pl.BlockSpec((1, d, _HID), lambda i, g: (g, 0, 0)),
            pl.BlockSpec((1, _HID, d), lambda i, g: (g, 0, 0)),
            pl.BlockSpec(g2.shape, c2),
            pl.BlockSpec(b2.shape, c2),
        ],
        out_specs=pl.BlockSpec((MOE_TM, d), row),
        out_shape=jax.ShapeDtypeStruct((t, d), F32),
        scratch_shapes=[
            pltpu.VMEM((MOE_TM, d + LANES), BF16),
            pltpu.VMEM((MOE_TM, LANES), F32),
            pltpu.VMEM((_R_SLOT, MOE_TM), F32),
            pltpu.VMEM((_YS_ROWS, d), BF16),
            pltpu.SMEM((N_GROUPS,), jnp.int32),
            pltpu.SMEM((N_GROUPS,), jnp.int32),
        ],
        compiler_params=pltpu.CompilerParams(
            dimension_semantics=("parallel", "arbitrary"), vmem_limit_bytes=MOE_VMEM_LIMIT),
        name="moe",
    )(x1, tri, wr, br, wg, wu, wd, g2, b2)


def _rope_table(seq):
    half = ROPE_DIM // 2
    inv = ROPE_THETA ** (-jnp.arange(0, ROPE_DIM, 2, dtype=F32) / ROPE_DIM)
    ang = jnp.arange(seq, dtype=F32)[:, None] * inv[None, :]
    cos, sin = jnp.cos(ang), jnp.sin(ang)
    rest = HEAD_DIM - ROPE_DIM
    one = jnp.ones((seq, rest), F32)
    zero = jnp.zeros((seq, rest), F32)
    zh = jnp.zeros((seq, half), F32)
    c_h = jnp.concatenate([cos, cos, one], axis=1)
    dn_h = jnp.concatenate([-sin, zh, zero], axis=1)
    up_h = jnp.concatenate([zh, sin, zero], axis=1)
    rep = LANES // HEAD_DIM
    return jnp.concatenate([jnp.tile(c_h, (1, rep)), jnp.tile(dn_h, (1, rep)), jnp.tile(up_h, (1, rep))],
                           axis=1)


def _expand_cmp_weights(pe, w1, w2):
    g_n = NSA_KV_GROUPS
    w1r = w1.astype(BF16).reshape(2, CMP_STRIDE, HEAD_DIM, CMP_HIDDEN)
    w2b = w2.astype(BF16)
    w1e = jnp.concatenate(
        [jnp.stack([w1r if g == e else jnp.zeros_like(w1r) for g in range(g_n)], axis=2)
         .reshape(2, A_W, CMP_HIDDEN) for e in range(g_n)], axis=2)
    w2e = jnp.concatenate(
        [jnp.concatenate([w2b if g == e else jnp.zeros_like(w2b) for e in range(g_n)], axis=1)
         for g in range(g_n)], axis=0)
    per = pe.reshape(2, CMP_STRIDE, 1, HEAD_DIM)
    pee = jnp.broadcast_to(per, (2, CMP_STRIDE, g_n, HEAD_DIM)).reshape(2, A_W)
    pee = jnp.concatenate([pee, jnp.zeros((6, A_W), pe.dtype)], axis=0)
    return pee.astype(F32), w1e.astype(BF16), w2e.astype(BF16)


def kernel(x, mem, w_in, cmp_pe_k, cmp_w1_k, cmp_w2_k, cmp_pe_v, cmp_w1_v, cmp_w2_v, sgu_ln_g, sgu_ln_b,
           sgu_w_s, sgu_b_s, w_mem_kv, w_br_nsa, w_br_sgu, w_br_mem, w_o, ln1_g, ln1_b, w_router_group,
           b_router_group, w_router_expert, b_router_expert, w_exp_gate, w_exp_up, w_exp_down, ln2_g, ln2_b):
    batch, seq, d = x.shape
    t = batch * seq
    assert w_in.shape[0] == DEPTH == 1
    assert seq % PROJ_TM == 0 and seq // CMP_STRIDE == N_A and seq // SLC_BLOCK == N_SEL

    offs = [int(v) for v in np.cumsum(
        [0, Q_W, KV_W, KV_W, KV_W, KV_W, KV_W, KV_W, GATE_W, 2 * SGU_WIDTH, MEM_WIDTH, 3 * d])]
    assert offs[-1] == w_in.shape[2]
    w_t = jnp.transpose(w_in[0])
    seg = lambda i: w_t[offs[i]:offs[i + 1]]
    gate_rows = jnp.pad(seg(7), ((0, LANES - GATE_W), (0, 0)))
    wp_t = jnp.concatenate([seg(0), seg(1), seg(3), seg(5), seg(2), seg(4), seg(6), gate_rows, seg(8), seg(9)],
                           axis=0).astype(BF16)
    w_mt = seg(10).astype(BF16)
    assert wp_t.shape[0] == _PROJ_COLS
    rope = _rope_table(seq)
    tril = jnp.tril(jnp.ones((SGU_CHUNK, SGU_CHUNK), dtype=bool))
    ws = jnp.where(tril[None], sgu_w_s[0], 0.0).astype(BF16)
    bs = jnp.repeat(sgu_b_s[0].T, SGU_WIDTH // SGU_GROUPS, axis=1)

    x2 = x.reshape(t, d)
    (qpad, kc, ksl, kwn, vc, vslt, vwnt, gt, osgu, mq) = _proj_call(
        x2, wp_t, rope, sgu_ln_g[0][None], sgu_ln_b[0][None], ws, bs, seq)

    pek, w1k, w2k = _expand_cmp_weights(cmp_pe_k[0], cmp_w1_k[0], cmp_w2_k[0])
    pev, w1v, w2v = _expand_cmp_weights(cmp_pe_v[0], cmp_w1_v[0], cmp_w2_v[0])
    kcmp, vcmp = _compress_call(kc, vc, pek, pev, w1k, w2k, w1v, w2v, batch)

    nqb = seq // Q_BLOCK
    n_g = NSA_KV_GROUPS
    vcmpt = vcmp.reshape(batch, N_A, KV_W).transpose(0, 2, 1).reshape(batch * KV_W, N_A)
    ci = np.arange(N_A)
    sj = np.arange(N_SEL)
    overlap = ((ci[None, :] * CMP_STRIDE + CMP_LEN - 1 >= sj[:, None] * SLC_BLOCK)
               & (ci[None, :] * CMP_STRIDE <= sj[:, None] * SLC_BLOCK + SLC_BLOCK - 1)
               & (ci[None, :] < (seq - CMP_LEN) // CMP_STRIDE + 1))
    ot = jnp.asarray(overlap, dtype=BF16)
    onsa = _nsa_call(qpad, ksl, kwn, vslt, vwnt, kcmp, vcmpt, gt, ot, batch, seq)

    omem = _memattn_call(mq, mem.reshape(batch * mem.shape[1], d), w_mem_kv[0], batch, seq)

    x1 = _merge_call(x2, onsa, osgu, omem, w_mt, w_br_nsa[0], w_br_sgu[0], w_br_mem[0], w_o[0],
                     ln1_g[0][None], ln1_b[0][None])

    assert EXPERTS_PER_GROUP == _R_SLOT and N_GROUPS <= _R_SLOT
    n_r = _R_SLOT + N_GROUPS * EXPERTS_PER_GROUP
    wr = jnp.concatenate([jnp.pad(w_router_group[0], ((0, 0), (0, _R_SLOT - N_GROUPS))),
                          w_router_expert[0]], axis=1)
    wr = jnp.pad(wr, ((0, 0), (0, LANES - n_r))).astype(BF16).T
    br = jnp.concatenate([jnp.pad(b_router_group[0], (0, _R_SLOT - N_GROUPS)), b_router_expert[0]])
    br = jnp.pad(br, (0, LANES - n_r))[:, None]
    wg = w_exp_gate[0].transpose(0, 2, 1, 3).reshape(N_GROUPS, d, _HID).astype(BF16)
    wu = w_exp_up[0].transpose(0, 2, 1, 3).reshape(N_GROUPS, d, _HID).astype(BF16)
    wd = w_exp_down[0].reshape(N_GROUPS, _HID, d).astype(BF16)
    out = _moe_call(x1, wr, br, wg, wu, wd, ln2_g[0][None], ln2_b[0][None])
    return out.reshape(batch, seq, d)
```

```python
import functools

import numpy as np
import jax
import jax.numpy as jnp
from jax import lax
from jax.experimental import pallas as pl
from jax.experimental.pallas import tpu as pltpu

NSA_HEADS = 8
NSA_KV_GROUPS = 2
NSA_HPG = NSA_HEADS // NSA_KV_GROUPS
HEAD_DIM = 64
CMP_LEN = 32
CMP_STRIDE = 16
CMP_HIDDEN = 256
SLC_BLOCK = 64
SLC_TOPN = 8
WINDOW = 512
Q_BLOCK = 128
N_BAND = WINDOW // Q_BLOCK
ROPE_THETA = 500000.0
ROPE_DIM = HEAD_DIM // 4
SGU_CHUNK = 128
SGU_GROUPS = 8
SGU_WIDTH = 512
MEM_HEADS = 4
MEM_HEAD_DIM = 128
MEM_WIDTH = MEM_HEADS * MEM_HEAD_DIM
N_GROUPS = 4
EXPERTS_PER_GROUP = 8
EXPERT_FF = 256
DEPTH = 1
DN_ALPHA = (2.0 * DEPTH) ** 0.25
LN_EPS = 1e-5
NEG = -1e30
LOG2E = 1.4426950408889634

LANES = 128
Q_W = NSA_HEADS * HEAD_DIM
KV_W = NSA_KV_GROUPS * HEAD_DIM
GATE_W = NSA_HEADS * 3
VMEM_LIMIT = 56 * 1024 * 1024
MOE_VMEM_LIMIT = 60 * 1024 * 1024

BF16 = jnp.bfloat16
F32 = jnp.float32


def _dot(a, b):
    return jnp.dot(a, b, preferred_element_type=F32)


def _dot_nt(a, b):
    return lax.dot_general(a, b, (((1,), (1,)), ((), ())), preferred_element_type=F32)


def _sigmoid(x):
    return 1.0 / (1.0 + jnp.exp(-x))


def _gelu(x):
    return 0.5 * x * (1.0 + lax.erf(x * (2.0 ** -0.5)))


def _layer_norm(x, g, b):
    mu = jnp.mean(x, axis=-1, keepdims=True)
    xc = x - mu
    var = jnp.mean(xc * xc, axis=-1, keepdims=True)
    return xc * lax.rsqrt(var + LN_EPS) * g + b


PROJ_TM = 512
_ROPE_COLS = Q_W + 3 * KV_W
_V_OFF = _ROPE_COLS
_G_OFF = _V_OFF + 3 * KV_W
_SGU_OFF = _G_OFF + LANES
_MQ_OFF = _SGU_OFF + 2 * SGU_WIDTH
_PROJ_COLS = _MQ_OFF + MEM_WIDTH


def _store_strided_rows(val, out_ref, tmp_ref):
    tmp_ref[...] = val
    n = val.shape[0] // CMP_STRIDE
    for l in range(CMP_STRIDE):
        out_ref[:, l * LANES:(l + 1) * LANES] = tmp_ref[pl.ds(l, n, stride=CMP_STRIDE), :].astype(BF16)


def _transpose_weight(wt_ref, w_s):
    for j in range(wt_ref.shape[0] // LANES):
        rows = slice(j * LANES, (j + 1) * LANES)
        w_s[:, rows] = wt_ref[rows, :].astype(F32).T.astype(BF16)


def _proj_kernel(x_ref, wt_ref, rope_ref, lng_ref, lnb_ref, ws_ref, bs_ref,
                 q_ref, kc_ref, ksl_ref, kwn_ref, vc_ref, vsl_ref, vwn_ref, gate_ref, osgu_ref, mq_ref,
                 a_s, w_ref):
    pl.when(pl.program_id(0) == 0)(functools.partial(_transpose_weight, wt_ref, w_ref))
    xb = x_ref[...].astype(BF16)
    lane = lax.broadcasted_iota(jnp.int32, (PROJ_TM, LANES), 1)
    low = lane < HEAD_DIM
    cos = rope_ref[:, 0:LANES]
    s_dn = rope_ref[:, LANES:2 * LANES]
    s_up = rope_ref[:, 2 * LANES:3 * LANES]

    hz = _dot(xb, w_ref[:, _SGU_OFF:_MQ_OFF])
    h = _dot(xb, w_ref[:, 0:_ROPE_COLS])
    k_refs = (kc_ref, ksl_ref, kwn_ref)
    for j in range(_ROPE_COLS // LANES):
        blk = h[:, j * LANES:(j + 1) * LANES]
        r = (blk * cos + pltpu.roll(blk, LANES - ROPE_DIM // 2, 1) * s_dn
             + pltpu.roll(blk, ROPE_DIM // 2, 1) * s_up)
        if j < Q_W // LANES:
            r = r * (HEAD_DIM ** -0.5 * LOG2E)
            sw = pltpu.roll(r, HEAD_DIM, 1)
            g = (2 * j) // NSA_HPG
            if g == 0:
                h0 = jnp.where(low, r, 0.0)
                h1 = jnp.where(low, sw, 0.0)
            else:
                h0 = jnp.where(low, 0.0, sw)
                h1 = jnp.where(low, 0.0, r)
            q_ref[:, (2 * j) * LANES:(2 * j + 1) * LANES] = h0.astype(BF16)
            q_ref[:, (2 * j + 1) * LANES:(2 * j + 2) * LANES] = h1.astype(BF16)
        else:
            if j == Q_W // LANES:
                _store_strided_rows(r, kc_ref, a_s)
            else:
                k_refs[j - Q_W // LANES][...] = r.astype(BF16)

    _store_strided_rows(_dot(xb, w_ref[:, _V_OFF:_V_OFF + LANES]), vc_ref, a_s)
    ones = jnp.ones((V_ROWS - HEAD_DIM, PROJ_TM), BF16)
    for v_ref, row0, blk in ((vsl_ref, _V_OFF + LANES, SEL_CHUNK), (vwn_ref, _V_OFF + 2 * LANES, Q_BLOCK)):
        vt = _dot_nt(wt_ref[row0:row0 + LANES, :], xb).astype(BF16)
        for g in range(NSA_KV_GROUPS):
            for b in range(PROJ_TM // blk):
                v_ref[g, b, 0:HEAD_DIM, :] = vt[g * HEAD_DIM:(g + 1) * HEAD_DIM, b * blk:(b + 1) * blk]
                v_ref[g, b, HEAD_DIM:V_ROWS, :] = ones[:, 0:blk]

    gate_ref[...] = _sigmoid(_dot_nt(wt_ref[_G_OFF:_SGU_OFF, :], xb))

    mq_ref[...] = _dot(xb, w_ref[:, _MQ_OFF:_PROJ_COLS]).astype(BF16)

    z = _gelu(hz)
    u = z[:, 0:SGU_WIDTH]
    v = _layer_norm(z[:, SGU_WIDTH:2 * SGU_WIDTH], lng_ref[...], lnb_ref[...]).astype(BF16)
    lane_c = lax.broadcasted_iota(jnp.int32, (SGU_CHUNK, LANES), 1)
    low_c = lane_c < (SGU_WIDTH // SGU_GROUPS)
    n_ch = PROJ_TM // SGU_CHUNK
    for gp in range(SGU_WIDTH // LANES):
        cols = slice(gp * LANES, (gp + 1) * LANES)
        vcat = jnp.concatenate([v[ci * SGU_CHUNK:(ci + 1) * SGU_CHUNK, cols] for ci in range(n_ch)], axis=1)
        sv0 = _dot(ws_ref[2 * gp], vcat)
        sv1 = _dot(ws_ref[2 * gp + 1], vcat)
        for ci in range(n_ch):
            rows = slice(ci * SGU_CHUNK, (ci + 1) * SGU_CHUNK)
            lanes = slice(ci * LANES, (ci + 1) * LANES)
            sv = jnp.where(low_c, sv0[:, lanes], sv1[:, lanes]) + bs_ref[:, cols]
            osgu_ref[rows, cols] = (u[rows, cols] * sv).astype(BF16)


def _proj_call(x2, wp, rope, lng, lnb, ws, bs, seq):
    t = x2.shape[0]
    d = x2.shape[1]
    nt = t // PROJ_TM
    per_seq = seq // PROJ_TM
    row = lambda i: (i, 0)
    const2 = lambda i: (0, 0)
    assert PROJ_TM % SEL_CHUNK == 0 and PROJ_TM % Q_BLOCK == 0
    n_g = NSA_KV_GROUPS
    batch = t // seq
    vblock = lambda i: (i // per_seq, i % per_seq, 0, 0)

    def v_shape(blk):
        return jax.ShapeDtypeStruct((batch * n_g, seq // blk, V_ROWS, blk), BF16)

    def v_spec(blk):
        return pl.BlockSpec((n_g, PROJ_TM // blk, V_ROWS, blk), vblock)

    a_shape = jax.ShapeDtypeStruct((t // CMP_STRIDE, CMP_STRIDE * KV_W), BF16)
    a_spec = pl.BlockSpec((PROJ_TM // CMP_STRIDE, CMP_STRIDE * KV_W), row)
    k_shape = jax.ShapeDtypeStruct((t, KV_W), BF16)
    k_spec = pl.BlockSpec((PROJ_TM, KV_W), row)
    out_shapes = (
        jax.ShapeDtypeStruct((t, NSA_HEADS * LANES), BF16),
        a_shape, k_shape, k_shape, a_shape,
        v_shape(SEL_CHUNK), v_shape(Q_BLOCK),
        jax.ShapeDtypeStruct((LANES, t), F32),
        jax.ShapeDtypeStruct((t, SGU_WIDTH), BF16),
        jax.ShapeDtypeStruct((t, MEM_WIDTH), BF16),
    )
    out_specs = (
        pl.BlockSpec((PROJ_TM, NSA_HEADS * LANES), row),
        a_spec, k_spec, k_spec, a_spec,
        v_spec(SEL_CHUNK), v_spec(Q_BLOCK),
        pl.BlockSpec((LANES, PROJ_TM), lambda i: (0, i)),
        pl.BlockSpec((PROJ_TM, SGU_WIDTH), row),
        pl.BlockSpec((PROJ_TM, MEM_WIDTH), row),
    )
    return pl.pallas_call(
        _proj_kernel,
        grid=(nt,),
        in_specs=[
            pl.BlockSpec((PROJ_TM, d), row),
            pl.BlockSpec((_PROJ_COLS, d), const2),
            pl.BlockSpec((PROJ_TM, 3 * LANES), lambda i: (i % per_seq, 0)),
            pl.BlockSpec((1, SGU_WIDTH), const2),
            pl.BlockSpec((1, SGU_WIDTH), const2),
            pl.BlockSpec((SGU_GROUPS, SGU_CHUNK, SGU_CHUNK), lambda i: (0, 0, 0)),
            pl.BlockSpec((SGU_CHUNK, SGU_WIDTH), const2),
        ],
        out_specs=out_specs,
        out_shape=out_shapes,
        scratch_shapes=[pltpu.VMEM((PROJ_TM, KV_W), F32), pltpu.VMEM((d, _PROJ_COLS), BF16)],
        compiler_params=pltpu.CompilerParams(
            dimension_semantics=("arbitrary",), vmem_limit_bytes=VMEM_LIMIT),
        name="proj",
    )(x2, wp, rope, lng, lnb, ws, bs)


N_A = 128
A_W = CMP_STRIDE * KV_W


def _compress_kernel(ka_ref, va_ref, pek_ref, pev_ref, w1k_ref, w2k_ref, w1v_ref, w2v_ref,
                     kcmp_ref, vcmp_ref):
    def one(a_ref, pe_ref, w1_ref, w2_ref, out_ref):
        a = a_ref[...].astype(F32)
        top = (a + pe_ref[0:1, :]).astype(BF16)
        bot = (a + pe_ref[1:2, :]).astype(BF16)
        h1 = _dot(top, w1_ref[0])
        h2 = _dot(bot, w1_ref[1])
        pre = h1 + pltpu.roll(h2, N_A - 1, 0)
        act = _gelu(pre).astype(BF16)
        out_ref[...] = _dot(act, w2_ref[...]).astype(BF16)

    one(ka_ref, pek_ref, w1k_ref, w2k_ref, kcmp_ref)
    one(va_ref, pev_ref, w1v_ref, w2v_ref, vcmp_ref)


def _compress_call(ka, va, pek, pev, w1k, w2k, w1v, w2v, batch):
    row = lambda b: (b, 0)
    c2 = lambda b: (0, 0)
    c3 = lambda b: (0, 0, 0)
    hid2 = NSA_KV_GROUPS * CMP_HIDDEN
    return pl.pallas_call(
        _compress_kernel,
        grid=(batch,),
        in_specs=[
            pl.BlockSpec((N_A, A_W), row),
            pl.BlockSpec((N_A, A_W), row),
            pl.BlockSpec((8, A_W), c2),
            pl.BlockSpec((8, A_W), c2),
            pl.BlockSpec((2, A_W, hid2), c3),
            pl.BlockSpec((hid2, KV_W), c2),
            pl.BlockSpec((2, A_W, hid2), c3),
            pl.BlockSpec((hid2, KV_W), c2),
        ],
        out_specs=(pl.BlockSpec((N_A, KV_W), row), pl.BlockSpec((N_A, KV_W), row)),
        out_shape=(jax.ShapeDtypeStruct((batch * N_A, KV_W), BF16),
                   jax.ShapeDtypeStruct((batch * N_A, KV_W), BF16)),
        compiler_params=pltpu.CompilerParams(
            dimension_semantics=("parallel",), vmem_limit_bytes=VMEM_LIMIT),
        name="compress",
    )(ka, va, pek, pev, w1k, w2k, w1v, w2v)


N_SEL = 32
SEL_CHUNK = 256
SEL_SUB = 128
HQ = NSA_HPG * Q_BLOCK
V_ROWS = HEAD_DIM + 16


def _nsa_kernel(*refs):
    ksl_ref = refs[1]
    reach = pl.program_id(1) // (SEL_CHUNK // Q_BLOCK)
    for n in range(ksl_ref.shape[0] // SEL_CHUNK):
        pl.when(reach == n)(functools.partial(_nsa_body, n, *refs))


def _nsa_body(n_later, q_ref, ksl_ref, kwn_ref, vslt_ref, vwnt_ref, kcmp_ref, vcmpt_ref, gt_ref,
              ot_ref, o_ref, m_s, acc_s, ocmp_s, owin_s, cap_s, out_s):
    c = pl.program_id(1)
    lane_hq = lax.broadcasted_iota(jnp.int32, (1, HQ), 1)
    pos_hq = c * Q_BLOCK + (lane_hq & (Q_BLOCK - 1))
    pos_q = c * Q_BLOCK + lax.broadcasted_iota(jnp.int32, (1, Q_BLOCK), 1)
    groups = range(NSA_KV_GROUPS)
    qgs = [jnp.concatenate(
        [q_ref[:, (g * NSA_HPG + hh) * LANES:(g * NSA_HPG + hh + 1) * LANES]
         for hh in range(NSA_HPG)], axis=0) for g in groups]

    kcmp = kcmp_ref[...]
    s_c = [_dot_nt(kcmp, qgs[g]) for g in groups]
    n_idx = lax.broadcasted_iota(jnp.int32, (N_A, HQ), 0)
    valid_c = (n_idx * CMP_STRIDE + (CMP_LEN - 1)) <= pos_hq
    p_c = []
    for g in groups:
        sm_c = jnp.where(valid_c, s_c[g], NEG)
        m_c = jnp.max(sm_c, axis=0, keepdims=True)
        e_c = jnp.where(valid_c, jnp.exp2(sm_c - m_c), 0.0)
        d_c = jnp.sum(e_c, axis=0, keepdims=True)
        p_c.append(e_c / jnp.where(d_c > 0, d_c, 1.0))
    for g in groups:
        ocmp_s[g] = _dot(vcmpt_ref[g * HEAD_DIM:(g + 1) * HEAD_DIM, :], p_c[g].astype(BF16))

    ot = ot_ref[...]
    imp = []
    for g in groups:
        ps = (p_c[g][:, 0:Q_BLOCK] + p_c[g][:, Q_BLOCK:2 * Q_BLOCK]
              + p_c[g][:, 2 * Q_BLOCK:3 * Q_BLOCK] + p_c[g][:, 3 * Q_BLOCK:4 * Q_BLOCK])
        p_hi = ps.astype(BF16)
        r1 = ps - p_hi.astype(F32)
        p_lo = r1.astype(BF16)
        p_lo2 = (r1 - p_lo.astype(F32)).astype(BF16)
        imp.append(_dot(ot, p_hi) + _dot(ot, p_lo) + _dot(ot, p_lo2))

    j_idx = lax.broadcasted_iota(jnp.int32, (N_SEL, Q_BLOCK), 0)
    cur = pos_q // SLC_BLOCK
    future = j_idx > cur
    forced = (j_idx == 0) | (j_idx == cur) | (j_idx == cur - 1)
    imp = [jnp.where(future, NEG, jnp.where(forced, -NEG, imp[g])) for g in groups]
    rank = [jnp.zeros((N_SEL, Q_BLOCK), F32) for g in groups]
    for i in range(N_SEL):
        for g in groups:
            row = imp[g][i:i + 1, :]
            beats = (row > imp[g]) | ((row == imp[g]) & (j_idx > i))
            rank[g] = rank[g] + jnp.where(beats, 1.0, 0.0)
    for g in groups:
        cap_s[g] = jnp.where(rank[g] < float(SLC_TOPN), -NEG, NEG)
        m_s[g] = jnp.full((1, HQ), NEG, F32)
        acc_s[g] = jnp.zeros((V_ROWS, HQ), F32)

    blocks_per_sub = SEL_SUB // SLC_BLOCK
    subs_per_chunk = SEL_CHUNK // SEL_SUB

    def sel_spans(kc):
        return [slice(kc * SEL_CHUNK + a * SEL_SUB, kc * SEL_CHUNK + (a + 1) * SEL_SUB)
                for a in range(subs_per_chunk)]

    def sel_scores(kc):
        return [[_dot_nt(ksl_ref[ks, :], qgs[g]) for g in groups] for ks in sel_spans(kc)]

    def sel_chunk(kc, s=None):
        spans = sel_spans(kc)
        s = sel_scores(kc) if s is None else s
        for a, ks in enumerate(spans):
            kpos = ks.start + lax.broadcasted_iota(jnp.int32, (SEL_SUB, Q_BLOCK), 0)
            causal = kpos <= pos_q
            for g in groups:
                j0 = ks.start // SLC_BLOCK
                cap = jnp.concatenate(
                    [jnp.broadcast_to(cap_s[g, j:j + 1, :], (SLC_BLOCK, Q_BLOCK))
                     for j in range(j0, j0 + blocks_per_sub)], axis=0)
                cap = jnp.where(causal, cap, NEG)
                sm = jnp.minimum(s[a][g], jnp.concatenate([cap] * NSA_HPG, axis=1))
                m_old = m_s[g]
                m_new = jnp.maximum(m_old, jnp.max(sm, axis=0, keepdims=True))
                alpha = jnp.exp2(m_old - m_new)
                e = jnp.exp2((sm - m_new).astype(BF16))
                v_t = vslt_ref[g, kc, :, a * SEL_SUB:(a + 1) * SEL_SUB]
                acc_s[g] = alpha * acc_s[g] + _dot(v_t, e)
                m_s[g] = m_new

    s0 = sel_scores(0)
    _nsa_window(c, qgs, kwn_ref, vwnt_ref, owin_s)
    s_next = sel_scores(1) if n_later else None
    sel_chunk(0, s0)
    for kc in range(1, n_later + 1):
        s_cur = s_next
        if kc < n_later:
            s_next = sel_scores(kc + 1)
        sel_chunk(kc, s_cur)

    for g in groups:
        acc = acc_s[g]
        o_sel = acc[0:HEAD_DIM, :] * (1.0 / acc[HEAD_DIM:HEAD_DIM + 1, :])

        def gate_row(br):
            return jnp.concatenate(
                [gt_ref[(g * NSA_HPG + hh) * 3 + br:(g * NSA_HPG + hh) * 3 + br + 1, :]
                 for hh in range(NSA_HPG)], axis=1)
        o_t = gate_row(0) * ocmp_s[g] + gate_row(1) * o_sel + gate_row(2) * owin_s[g]
        for hh in range(NSA_HPG):
            h = g * NSA_HPG + hh
            out_s[h * HEAD_DIM:(h + 1) * HEAD_DIM, :] = o_t[:, hh * Q_BLOCK:(hh + 1) * Q_BLOCK]

    o_ref[...] = out_s[...].T.astype(BF16)


def _nsa_window(c, qgs, kwn_ref, vwnt_ref, owin_s):
    groups = range(NSA_KV_GROUPS)
    q_i = lax.broadcasted_iota(jnp.int32, (Q_BLOCK, Q_BLOCK), 1)
    k_i = lax.broadcasted_iota(jnp.int32, (Q_BLOCK, Q_BLOCK), 0)
    blks, caps = [], []
    for i in range(N_BAND + 1):
        blk = c - N_BAND + i
        blks.append(jnp.maximum(blk, 0))
        off = (N_BAND - i) * Q_BLOCK
        if off - (Q_BLOCK - 1) >= 0 and off + (Q_BLOCK - 1) < WINDOW:
            caps.append(jnp.where(blk >= 0, -NEG, NEG))
        else:
            diff = off + q_i - k_i
            ok = (diff >= 0) & (diff < WINDOW) & (blk >= 0)
            caps.append(jnp.concatenate([jnp.where(ok, -NEG, NEG)] * NSA_HPG, axis=1))
    k_blocks = [kwn_ref[pl.ds(pl.multiple_of(blks[i] * Q_BLOCK, Q_BLOCK), Q_BLOCK), :]
                for i in range(N_BAND + 1)]
    s_w = [[None] * (N_BAND + 1) for g in groups]
    for i in reversed(range(N_BAND + 1)):
        for g in groups:
            s_w[g][i] = _dot_nt(k_blocks[i], qgs[g])
    m_w = [None for g in groups]
    o_win = [None for g in groups]
    for i in reversed(range(N_BAND + 1)):
        for g in groups:
            sm = jnp.minimum(s_w[g][i], caps[i])
            m_blk = jnp.max(sm, axis=0, keepdims=True)
            m_new = m_blk if m_w[g] is None else jnp.maximum(m_w[g], m_blk)
            pv = _dot(vwnt_ref[g, blks[i]], jnp.exp2((sm - m_new).astype(BF16)))
            o_win[g] = pv if m_w[g] is None else jnp.exp2(m_w[g] - m_new) * o_win[g] + pv
            m_w[g] = m_new
    for g in groups:
        owin_s[g] = o_win[g][0:HEAD_DIM, :] * (1.0 / o_win[g][HEAD_DIM:HEAD_DIM + 1, :])


def _nsa_call(qpad, ksl, kwn, vslt, vwnt, kcmp, vcmpt, gt, ot, batch, seq):
    nqb = seq // Q_BLOCK
    n_g = NSA_KV_GROUPS
    qrow = lambda b, c: (b * nqb + c, 0)
    brow = lambda b, c: (b, 0)
    c2 = lambda b, c: (0, 0)
    return pl.pallas_call(
        _nsa_kernel,
        grid=(batch, nqb),
        in_specs=[
            pl.BlockSpec((Q_BLOCK, NSA_HEADS * LANES), qrow),
            pl.BlockSpec((seq, KV_W), brow),
            pl.BlockSpec((seq, KV_W), brow),
            pl.BlockSpec((n_g, seq // SEL_CHUNK, V_ROWS, SEL_CHUNK), lambda b, c: (b, 0, 0, 0)),
            pl.BlockSpec((n_g, nqb, V_ROWS, Q_BLOCK), lambda b, c: (b, 0, 0, 0)),
            pl.BlockSpec((N_A, KV_W), brow),
            pl.BlockSpec((KV_W, N_A), brow),
            pl.BlockSpec((LANES, Q_BLOCK), lambda b, c: (0, b * nqb + c)),
            pl.BlockSpec((N_SEL, N_A), c2),
        ],
        out_specs=pl.BlockSpec((Q_BLOCK, Q_W), qrow),
        out_shape=jax.ShapeDtypeStruct((batch * seq, Q_W), BF16),
        scratch_shapes=[
            pltpu.VMEM((n_g, 1, HQ), F32),
            pltpu.VMEM((n_g, V_ROWS, HQ), F32),
            pltpu.VMEM((n_g, HEAD_DIM, HQ), F32),
            pltpu.VMEM((n_g, HEAD_DIM, HQ), F32),
            pltpu.VMEM((n_g, N_SEL, Q_BLOCK), F32),
            pltpu.VMEM((Q_W, Q_BLOCK), F32),
        ],
        compiler_params=pltpu.CompilerParams(
            dimension_semantics=("parallel", "arbitrary"), vmem_limit_bytes=VMEM_LIMIT),
        name="nsa",
    )(qpad, ksl, kwn, vslt, vwnt, kcmp, vcmpt, gt, ot)


MEM_TM = 512
MEM_V_ROWS = MEM_HEAD_DIM + 16


def _memattn_kernel(mq_ref, mem_ref, wkv_ref, o_ref, k_s, vt_s, out_s):
    @pl.when(pl.program_id(1) == 0)
    def _():
        kv = _dot(mem_ref[...].astype(BF16), wkv_ref[...].astype(BF16))
        k_s[...] = kv[:, 0:MEM_WIDTH].astype(BF16)
        vt = kv[:, MEM_WIDTH:2 * MEM_WIDTH].T.astype(BF16)
        for h in range(MEM_HEADS):
            vt_s[h, 0:MEM_HEAD_DIM, :] = vt[h * MEM_HEAD_DIM:(h + 1) * MEM_HEAD_DIM, :]
            vt_s[h, MEM_HEAD_DIM:, :] = jnp.ones((MEM_V_ROWS - MEM_HEAD_DIM, vt.shape[1]), BF16)

    heads = range(MEM_HEADS)
    cols = [slice(h * MEM_HEAD_DIM, (h + 1) * MEM_HEAD_DIM) for h in heads]
    s_t = [_dot_nt(k_s[:, cols[h]], mq_ref[:, cols[h]]) for h in heads]
    e = []
    for h in heads:
        m = jnp.max(s_t[h], axis=0, keepdims=True)
        e.append(jnp.exp2((s_t[h] - m) * (MEM_HEAD_DIM ** -0.5 * LOG2E)).astype(BF16))
    o_t = [_dot(vt_s[h], e[h]) for h in heads]
    for h in heads:
        out_s[cols[h], :] = o_t[h][0:MEM_HEAD_DIM, :] * (1.0 / o_t[h][MEM_HEAD_DIM:MEM_HEAD_DIM + 1, :])
    o_ref[...] = out_s[...].T.astype(BF16)


def _memattn_call(mq, mem2, wkv, batch, seq):
    m_len = mem2.shape[0] // batch
    d = mem2.shape[1]
    per = seq // MEM_TM
    return pl.pallas_call(
        _memattn_kernel,
        grid=(batch, per),
        in_specs=[
            pl.BlockSpec((MEM_TM, MEM_WIDTH), lambda b, i: (b * per + i, 0)),
            pl.BlockSpec((m_len, d), lambda b, i: (b, 0)),
            pl.BlockSpec((d, 2 * MEM_WIDTH), lambda b, i: (0, 0)),
        ],
        out_specs=pl.BlockSpec((MEM_TM, MEM_WIDTH), lambda b, i: (b * per + i, 0)),
        out_shape=jax.ShapeDtypeStruct((batch * seq, MEM_WIDTH), BF16),
        scratch_shapes=[
            pltpu.VMEM((m_len, MEM_WIDTH), BF16),
            pltpu.VMEM((MEM_HEADS, MEM_V_ROWS, m_len), BF16),
            pltpu.VMEM((MEM_WIDTH, MEM_TM), F32),
        ],
        compiler_params=pltpu.CompilerParams(
            dimension_semantics=("parallel", "arbitrary"), vmem_limit_bytes=VMEM_LIMIT),
        name="memattn",
    )(mq, mem2, wkv)


MERGE_TM = 512


def _merge_kernel(x_ref, onsa_ref, osgu_ref, omem_ref, wgt_ref, wbn32_ref, wbs32_ref, wbm32_ref, wo32_ref,
                  g_ref, b_ref, out_ref, wbn_ref, wbs_ref, wbm_ref, wo_ref, wg_ref):
    @pl.when(pl.program_id(0) == 0)
    def _():
        for dst, src in ((wbn_ref, wbn32_ref), (wbs_ref, wbs32_ref), (wbm_ref, wbm32_ref),
                         (wo_ref, wo32_ref)):
            dst[...] = src[...].astype(BF16)
        _transpose_weight(wgt_ref, wg_ref)

    d = x_ref.shape[1]
    branches = ((onsa_ref, wbn_ref), (osgu_ref, wbs_ref), (omem_ref, wbm_ref))
    halves = [slice(h * (MERGE_TM // 2), (h + 1) * (MERGE_TM // 2)) for h in range(2)]
    xs = [x_ref[rows, :] for rows in halves]
    xbs = [x.astype(BF16) for x in xs]
    logits = [[_dot(xb, wg_ref[:, br * d:(br + 1) * d]) for br in range(len(branches))] for xb in xbs]
    projs = [[_dot(o_r[rows, :], w_r[...]) for o_r, w_r in branches] for rows in halves]
    ys = []
    for h in range(2):
        y = None
        for br in range(len(branches)):
            term = _sigmoid(logits[h][br]) * projs[h][br]
            y = term if y is None else y + term
        ys.append(y.astype(BF16))
    outs = [_dot(y, wo_ref[...]) for y in ys]
    for h, rows in enumerate(halves):
        out_ref[rows, :] = _layer_norm(DN_ALPHA * xs[h] + outs[h], g_ref[...], b_ref[...])


def _merge_call(x2, onsa, osgu, omem, wg, wbn, wbs, wbm, wo, g1, b1):
    t, d = x2.shape
    row = lambda i: (i, 0)
    c2 = lambda i: (0, 0)
    full = lambda a: pl.BlockSpec(a.shape, c2)
    once = lambda a: pl.BlockSpec(a.shape, c2, pipeline_mode=pl.Buffered(1))
    return pl.pallas_call(
        _merge_kernel,
        grid=(t // MERGE_TM,),
        in_specs=[
            pl.BlockSpec((MERGE_TM, d), row),
            pl.BlockSpec((MERGE_TM, onsa.shape[1]), row),
            pl.BlockSpec((MERGE_TM, osgu.shape[1]), row),
            pl.BlockSpec((MERGE_TM, omem.shape[1]), row),
            pl.BlockSpec((3 * d, d), c2, pipeline_mode=pl.Buffered(1)),
            once(wbn), once(wbs), once(wbm), once(wo), full(g1), full(b1),
        ],
        out_specs=pl.BlockSpec((MERGE_TM, d), row),
        out_shape=jax.ShapeDtypeStruct((t, d), F32),
        scratch_shapes=[pltpu.VMEM(a.shape, BF16) for a in (wbn, wbs, wbm, wo)]
        + [pltpu.VMEM((d, 3 * d), BF16)],
        compiler_params=pltpu.CompilerParams(
            dimension_semantics=("arbitrary",), vmem_limit_bytes=VMEM_LIMIT),
        name="merge",
    )(x2, onsa, osgu, omem, wg, wbn, wbs, wbm, wo, g1, b1)


MOE_TM = 1024
MOE_CH = 144
_YS_ROWS = -(-(MOE_TM + N_GROUPS * MOE_CH) // 256) * 256
_R_SLOT = 8
_DEST_LANE = 3 * _R_SLOT
_HID = EXPERTS_PER_GROUP * EXPERT_FF


def _moe_kernel(x_ref, tri_ref, wr_ref, br_ref, wg_ref, wu_ref, wd_ref, g_ref, b_ref, out_ref,
                xa_s, tok_s, keyr_s, ys_s, cnt_s, base_s):
    grp = pl.program_id(1)
    tm = x_ref.shape[0]
    d = x_ref.shape[1]

    @pl.when(grp == 0)
    def _route():
        xb = x_ref[...].astype(BF16)
        xa_s[:, 0:d] = xb
        lt = _dot_nt(wr_ref[...], xb) + br_ref[...]
        row = lax.broadcasted_iota(jnp.int32, (_R_SLOT, tm), 0)
        gl = jnp.where(row < N_GROUPS, lt[0:_R_SLOT], NEG)
        gmax = jnp.max(gl, axis=0, keepdims=True)
        gidx = jnp.min(jnp.where(gl == gmax, row, _R_SLOT), axis=0, keepdims=True)
        gprob = 1.0 / jnp.sum(jnp.exp(gl - gmax), axis=0, keepdims=True)
        el = lt[_R_SLOT:2 * _R_SLOT]
        for k in range(1, N_GROUPS):
            el = jnp.where(gidx == k, lt[(k + 1) * _R_SLOT:(k + 2) * _R_SLOT], el)
        ee = jnp.exp(el - jnp.max(el, axis=0, keepdims=True))
        ep = ee / jnp.sum(ee, axis=0, keepdims=True)
        t1 = jnp.max(ep, axis=0, keepdims=True)
        i1 = jnp.min(jnp.where(ep == t1, row, _R_SLOT), axis=0, keepdims=True)
        rest = row != i1
        t2 = jnp.max(jnp.where(rest, ep, -1.0), axis=0, keepdims=True)
        i2 = jnp.min(jnp.where(rest & (ep == t2), row, _R_SLOT), axis=0, keepdims=True)
        den = t1 + t2
        cwf = (jnp.where(row == i1, t1 / den, 0.0) + jnp.where(row == i2, t2 / den, 0.0)) * gprob
        hi = cwf.astype(BF16).astype(F32)
        r1 = cwf - hi
        lo = r1.astype(BF16).astype(F32)
        lo2 = (r1 - lo).astype(BF16).astype(F32)

        onehot = jnp.where(row == gidx, 1.0, 0.0)
        tri = tri_ref[...]
        onehot_b = onehot.astype(BF16)
        ranks, before = [], jnp.zeros((_R_SLOT, 1), F32)
        for b in range(tm // LANES):
            blk = slice(b * LANES, (b + 1) * LANES)
            ranks.append(_dot_nt(onehot_b[:, blk], tri) + before)
            before = before + jnp.sum(onehot[:, blk], axis=1, keepdims=True)
        rank = jnp.concatenate(ranks, axis=1)
        keyr_s[...] = jnp.where(onehot > 0.5, rank, -1.0)
        base = jnp.int32(0)
        basev = jnp.zeros((1, tm), F32)
        for k in range(N_GROUPS):
            n_k = jnp.sum(onehot[k:k + 1, :]).astype(jnp.int32)
            cnt_s[k] = n_k
            base_s[k] = base
            basev = jnp.where(gidx == k, base.astype(F32), basev)
            base = base + ((n_k + MOE_CH - 1) // MOE_CH) * MOE_CH
        dest = jnp.sum(rank * onehot, axis=0, keepdims=True) + basev
        tok = jnp.concatenate(
            [hi, lo, lo2, jnp.broadcast_to(dest, (_R_SLOT, tm)),
             jnp.zeros((LANES - 4 * _R_SLOT, tm), F32)], axis=0).T
        tok_s[...] = tok
        lane = lax.broadcasted_iota(jnp.int32, tok.shape, 1)
        xa_s[:, d:d + LANES] = jnp.where(lane < _DEST_LANE, tok, 0.0).astype(BF16)
        ys_s[...] = jnp.zeros(ys_s.shape, BF16)

    n_rows = cnt_s[grp]
    row0 = base_s[grp]
    keyr = keyr_s[pl.ds(grp, 1), :]
    half = _HID // 2

    def sweep(k, carry):
        ch = MOE_CH
        r_row = (lax.broadcasted_iota(jnp.int32, (ch, tm), 0) + k * MOE_CH).astype(F32)
        pick = jnp.where(keyr == r_row, 1.0, 0.0).astype(BF16)
        ga = _dot(pick, xa_s[...])
        xg = ga[:, 0:d].astype(BF16)
        cwg = ga[:, d:d + LANES]
        cs = (cwg + pltpu.roll(cwg, LANES - _R_SLOT, 1)
              + pltpu.roll(cwg, LANES - 2 * _R_SLOT, 1))
        experts = range(EXPERTS_PER_GROUP)
        per_half = EXPERTS_PER_GROUP // 2
        hg = [_dot(xg, wg_ref[0, e]) for e in experts]
        hu = [_dot(xg, wu_ref[0, e]) for e in experts]
        y = None
        for h in range(2):
            hid = jnp.concatenate(
                [(((hg[e] * _sigmoid(hg[e])) * hu[e]) * cs[:, e:e + 1]).astype(BF16)
                 for e in range(h * per_half, (h + 1) * per_half)], axis=1)
            term = _dot(hid, wd_ref[0, h * half:(h + 1) * half, :])
            y = term if y is None else y + term
        ys_s[pl.ds(pl.multiple_of(row0 + k * MOE_CH, 16), ch), :] = y.astype(BF16)
        return carry

    lax.fori_loop(0, (n_rows + MOE_CH - 1) // MOE_CH, sweep, 0)

    @pl.when(grp == N_GROUPS - 1)
    def _fin():
        n_q = 4
        q_rows = tm // n_q
        r_col = lax.broadcasted_iota(jnp.int32, (q_rows, _YS_ROWS), 1).astype(F32)
        parts = [slice(i * q_rows, (i + 1) * q_rows) for i in range(n_q)]
        puts = [jnp.where(tok_s[rows, _DEST_LANE:_DEST_LANE + 1] == r_col, 1.0, 0.0).astype(BF16)
                for rows in parts]
        fs = [_dot(put, ys_s[...]) for put in puts]
        for rows, f in zip(parts, fs):
            out_ref[rows, :] = _layer_norm(DN_ALPHA * x_ref[rows, :] + f, g_ref[...], b_ref[...])


def _moe_call(x1, wr, br, wg, wu, wd, g2, b2):
    t, d = x1.shape
    row = lambda i, g: (i, 0)
    c2 = lambda i, g: (0, 0)
    idx = np.arange(LANES)
    tri = jnp.asarray(idx[None, :] < idx[:, None], dtype=BF16)
    return pl.pallas_call(
        _moe_kernel,
        grid=(t // MOE_TM, N_GROUPS),
        in_specs=[
            pl.BlockSpec((MOE_TM, d), row, pipeline_mode=pl.Buffered(1)),
            pl.BlockSpec((LANES, LANES), c2),
            pl.BlockSpec(wr.shape, c2),
            pl.BlockSpec(br.shape, c2),
            pl.BlockSpec((1, EXPERTS_PER_GROUP, d, EXPERT_FF), lambda i, g: (g, 0, 0, 0)),
            pl.BlockSpec((1, EXPERTS_PER_GROUP, d, EXPERT_FF), lambda i, g: (g, 0, 0, 0)),
            pl.BlockSpec((1, _HID, d), lambda i, g: (g, 0, 0)),
            pl.BlockSpec(g2.shape, c2),
            pl.BlockSpec(b2.shape, c2),
        ],
        out_specs=pl.BlockSpec((MOE_TM, d), row),
        out_shape=jax.ShapeDtypeStruct((t, d), F32),
        scratch_shapes=[
            pltpu.VMEM((MOE_TM, d + LANES), BF16),
            pltpu.VMEM((MOE_TM, LANES), F32),
            pltpu.VMEM((_R_SLOT, MOE_TM), F32),
            pltpu.VMEM((_YS_ROWS, d), BF16),
            pltpu.SMEM((N_GROUPS,), jnp.int32),
            pltpu.SMEM((N_GROUPS,), jnp.int32),
        ],
        compiler_params=pltpu.CompilerParams(
            dimension_semantics=("parallel", "arbitrary"), vmem_limit_bytes=MOE_VMEM_LIMIT),
        name="moe",
    )(x1, tri, wr, br, wg, wu, wd, g2, b2)


def _rope_table(seq):
    half = ROPE_DIM // 2
    inv = ROPE_THETA ** (-jnp.arange(0, ROPE_DIM, 2, dtype=F32) / ROPE_DIM)
    ang = jnp.arange(seq, dtype=F32)[:, None] * inv[None, :]
    cos, sin = jnp.cos(ang), jnp.sin(ang)
    rest = HEAD_DIM - ROPE_DIM
    one = jnp.ones((seq, rest), F32)
    zero = jnp.zeros((seq, rest), F32)
    zh = jnp.zeros((seq, half), F32)
    c_h = jnp.concatenate([cos, cos, one], axis=1)
    dn_h = jnp.concatenate([-sin, zh, zero], axis=1)
    up_h = jnp.concatenate([zh, sin, zero], axis=1)
    rep = LANES // HEAD_DIM
    return jnp.concatenate([jnp.tile(c_h, (1, rep)), jnp.tile(dn_h, (1, rep)), jnp.tile(up_h, (1, rep))],
                           axis=1)


def _expand_cmp_weights(pe, w1, w2):
    g_n = NSA_KV_GROUPS
    w1r = w1.astype(BF16).reshape(2, CMP_STRIDE, HEAD_DIM, CMP_HIDDEN)
    w2b = w2.astype(BF16)
    w1e = jnp.concatenate(
        [jnp.stack([w1r if g == e else jnp.zeros_like(w1r) for g in range(g_n)], axis=2)
         .reshape(2, A_W, CMP_HIDDEN) for e in range(g_n)], axis=2)
    w2e = jnp.concatenate(
        [jnp.concatenate([w2b if g == e else jnp.zeros_like(w2b) for e in range(g_n)], axis=1)
         for g in range(g_n)], axis=0)
    per = pe.reshape(2, CMP_STRIDE, 1, HEAD_DIM)
    pee = jnp.broadcast_to(per, (2, CMP_STRIDE, g_n, HEAD_DIM)).reshape(2, A_W)
    pee = jnp.concatenate([pee, jnp.zeros((6, A_W), pe.dtype)], axis=0)
    return pee.astype(F32), w1e.astype(BF16), w2e.astype(BF16)


def kernel(x, mem, w_in, cmp_pe_k, cmp_w1_k, cmp_w2_k, cmp_pe_v, cmp_w1_v, cmp_w2_v, sgu_ln_g, sgu_ln_b,
           sgu_w_s, sgu_b_s, w_mem_kv, w_br_nsa, w_br_sgu, w_br_mem, w_o, ln1_g, ln1_b, w_router_group,
           b_router_group, w_router_expert, b_router_expert, w_exp_gate, w_exp_up, w_exp_down, ln2_g, ln2_b):
    batch, seq, d = x.shape
    t = batch * seq
    assert w_in.shape[0] == DEPTH == 1
    assert seq % PROJ_TM == 0 and seq // CMP_STRIDE == N_A and seq // SLC_BLOCK == N_SEL

    offs = [int(v) for v in np.cumsum(
        [0, Q_W, KV_W, KV_W, KV_W, KV_W, KV_W, KV_W, GATE_W, 2 * SGU_WIDTH, MEM_WIDTH, 3 * d])]
    assert offs[-1] == w_in.shape[2]
    w_t = jnp.transpose(w_in[0])
    seg = lambda i: w_t[offs[i]:offs[i + 1]]
    gate_rows = jnp.pad(seg(7), ((0, LANES - GATE_W), (0, 0)))
    wp_t = jnp.concatenate([seg(0), seg(1), seg(3), seg(5), seg(2), seg(4), seg(6), gate_rows, seg(8), seg(9)],
                           axis=0).astype(BF16)
    w_mt = seg(10).astype(BF16)
    assert wp_t.shape[0] == _PROJ_COLS
    rope = _rope_table(seq)
    tril = jnp.tril(jnp.ones((SGU_CHUNK, SGU_CHUNK), dtype=bool))
    ws = jnp.where(tril[None], sgu_w_s[0], 0.0).astype(BF16)
    bs = jnp.repeat(sgu_b_s[0].T, SGU_WIDTH // SGU_GROUPS, axis=1)

    x2 = x.reshape(t, d)
    (qpad, kc, ksl, kwn, vc, vslt, vwnt, gt, osgu, mq) = _proj_call(
        x2, wp_t, rope, sgu_ln_g[0][None], sgu_ln_b[0][None], ws, bs, seq)

    pek, w1k, w2k = _expand_cmp_weights(cmp_pe_k[0], cmp_w1_k[0], cmp_w2_k[0])
    pev, w1v, w2v = _expand_cmp_weights(cmp_pe_v[0], cmp_w1_v[0], cmp_w2_v[0])
    kcmp, vcmp = _compress_call(kc, vc, pek, pev, w1k, w2k, w1v, w2v, batch)

    nqb = seq // Q_BLOCK
    n_g = NSA_KV_GROUPS
    vcmpt = vcmp.reshape(batch, N_A, KV_W).transpose(0, 2, 1).reshape(batch * KV_W, N_A)
    ci = np.arange(N_A)
    sj = np.arange(N_SEL)
    overlap = ((ci[None, :] * CMP_STRIDE + CMP_LEN - 1 >= sj[:, None] * SLC_BLOCK)
               & (ci[None, :] * CMP_STRIDE <= sj[:, None] * SLC_BLOCK + SLC_BLOCK - 1)
               & (ci[None, :] < (seq - CMP_LEN) // CMP_STRIDE + 1))
    ot = jnp.asarray(overlap, dtype=BF16)
    onsa = _nsa_call(qpad, ksl, kwn, vslt, vwnt, kcmp, vcmpt, gt, ot, batch, seq)

    omem = _memattn_call(mq, mem.reshape(batch * mem.shape[1], d), w_mem_kv[0], batch, seq)

    x1 = _merge_call(x2, onsa, osgu, omem, w_mt, w_br_nsa[0], w_br_sgu[0], w_br_mem[0], w_o[0],
                     ln1_g[0][None], ln1_b[0][None])

    assert EXPERTS_PER_GROUP == _R_SLOT and N_GROUPS <= _R_SLOT
    n_r = _R_SLOT + N_GROUPS * EXPERTS_PER_GROUP
    wr = jnp.concatenate([jnp.pad(w_router_group[0], ((0, 0), (0, _R_SLOT - N_GROUPS))),
                          w_router_expert[0]], axis=1)
    wr = jnp.pad(wr, ((0, 0), (0, LANES - n_r))).astype(BF16).T
    br = jnp.concatenate([jnp.pad(b_router_group[0], (0, _R_SLOT - N_GROUPS)), b_router_expert[0]])
    br = jnp.pad(br, (0, LANES - n_r))[:, None]
    wg = w_exp_gate[0].astype(BF16)
    wu = w_exp_up[0].astype(BF16)
    wd = w_exp_down[0].reshape(N_GROUPS, _HID, d).astype(BF16)
    out = _moe_call(x1, wr, br, wg, wu, wd, ln2_g[0][None], ln2_b[0][None])
    return out.reshape(batch, seq, d)
```

```python
import functools

import numpy as np
import jax
import jax.numpy as jnp
from jax import lax
from jax.experimental import pallas as pl
from jax.experimental.pallas import tpu as pltpu

NSA_HEADS = 8
NSA_KV_GROUPS = 2
NSA_HPG = NSA_HEADS // NSA_KV_GROUPS
HEAD_DIM = 64
CMP_LEN = 32
CMP_STRIDE = 16
CMP_HIDDEN = 256
SLC_BLOCK = 64
SLC_TOPN = 8
WINDOW = 512
Q_BLOCK = 128
N_BAND = WINDOW // Q_BLOCK
ROPE_THETA = 500000.0
ROPE_DIM = HEAD_DIM // 4
SGU_CHUNK = 128
SGU_GROUPS = 8
SGU_WIDTH = 512
MEM_HEADS = 4
MEM_HEAD_DIM = 128
MEM_WIDTH = MEM_HEADS * MEM_HEAD_DIM
N_GROUPS = 4
EXPERTS_PER_GROUP = 8
EXPERT_FF = 256
DEPTH = 1
DN_ALPHA = (2.0 * DEPTH) ** 0.25
LN_EPS = 1e-5
NEG = -1e30
LOG2E = 1.4426950408889634

LANES = 128
Q_W = NSA_HEADS * HEAD_DIM
KV_W = NSA_KV_GROUPS * HEAD_DIM
GATE_W = NSA_HEADS * 3
VMEM_LIMIT = 56 * 1024 * 1024
MOE_VMEM_LIMIT = 60 * 1024 * 1024

BF16 = jnp.bfloat16
F32 = jnp.float32


def _dot(a, b):
    return jnp.dot(a, b, preferred_element_type=F32)


def _dot_nt(a, b):
    return lax.dot_general(a, b, (((1,), (1,)), ((), ())), preferred_element_type=F32)


def _sigmoid(x):
    return 1.0 / (1.0 + jnp.exp(-x))


def _gelu(x):
    return 0.5 * x * (1.0 + lax.erf(x * (2.0 ** -0.5)))


def _layer_norm(x, g, b):
    mu = jnp.mean(x, axis=-1, keepdims=True)
    xc = x - mu
    var = jnp.mean(xc * xc, axis=-1, keepdims=True)
    return xc * lax.rsqrt(var + LN_EPS) * g + b


PROJ_TM = 512
_ROPE_COLS = Q_W + 3 * KV_W
_V_OFF = _ROPE_COLS
_G_OFF = _V_OFF + 3 * KV_W
_SGU_OFF = _G_OFF + LANES
_MQ_OFF = _SGU_OFF + 2 * SGU_WIDTH
_PROJ_COLS = _MQ_OFF + MEM_WIDTH


def _store_strided_rows(val, out_ref, tmp_ref):
    tmp_ref[...] = val
    n = val.shape[0] // CMP_STRIDE
    for l in range(CMP_STRIDE):
        out_ref[:, l * LANES:(l + 1) * LANES] = tmp_ref[pl.ds(l, n, stride=CMP_STRIDE), :].astype(BF16)


def _transpose_weight(wt_ref, w_s):
    for j in range(wt_ref.shape[0] // LANES):
        rows = slice(j * LANES, (j + 1) * LANES)
        w_s[:, rows] = wt_ref[rows, :].astype(F32).T.astype(BF16)


def _proj_kernel(x_ref, wt_ref, rope_ref, lng_ref, lnb_ref, ws_ref, bs_ref,
                 q_ref, kc_ref, ksl_ref, kwn_ref, vc_ref, vsl_ref, vwn_ref, gate_ref, osgu_ref, mq_ref,
                 a_s, w_ref):
    pl.when(pl.program_id(0) == 0)(functools.partial(_transpose_weight, wt_ref, w_ref))
    xb = x_ref[...].astype(BF16)
    lane = lax.broadcasted_iota(jnp.int32, (PROJ_TM, LANES), 1)
    low = lane < HEAD_DIM
    cos = rope_ref[:, 0:LANES]
    s_dn = rope_ref[:, LANES:2 * LANES]
    s_up = rope_ref[:, 2 * LANES:3 * LANES]

    hz = _dot(xb, w_ref[:, _SGU_OFF:_MQ_OFF])
    h = _dot(xb, w_ref[:, 0:_ROPE_COLS])
    k_refs = (kc_ref, ksl_ref, kwn_ref)
    for j in range(_ROPE_COLS // LANES):
        blk = h[:, j * LANES:(j + 1) * LANES]
        r = (blk * cos + pltpu.roll(blk, LANES - ROPE_DIM // 2, 1) * s_dn
             + pltpu.roll(blk, ROPE_DIM // 2, 1) * s_up)
        if j < Q_W // LANES:
            r = r * (HEAD_DIM ** -0.5 * LOG2E)
            sw = pltpu.roll(r, HEAD_DIM, 1)
            g = (2 * j) // NSA_HPG
            if g == 0:
                h0 = jnp.where(low, r, 0.0)
                h1 = jnp.where(low, sw, 0.0)
            else:
                h0 = jnp.where(low, 0.0, sw)
                h1 = jnp.where(low, 0.0, r)
            q_ref[:, (2 * j) * LANES:(2 * j + 1) * LANES] = h0.astype(BF16)
            q_ref[:, (2 * j + 1) * LANES:(2 * j + 2) * LANES] = h1.astype(BF16)
        else:
            if j == Q_W // LANES:
                _store_strided_rows(r, kc_ref, a_s)
            else:
                k_refs[j - Q_W // LANES][...] = r.astype(BF16)

    _store_strided_rows(_dot(xb, w_ref[:, _V_OFF:_V_OFF + LANES]), vc_ref, a_s)
    ones = jnp.ones((V_ROWS - HEAD_DIM, PROJ_TM), BF16)
    for v_ref, row0, blk in ((vsl_ref, _V_OFF + LANES, SEL_CHUNK), (vwn_ref, _V_OFF + 2 * LANES, Q_BLOCK)):
        vt = _dot_nt(wt_ref[row0:row0 + LANES, :], xb).astype(BF16)
        for g in range(NSA_KV_GROUPS):
            for b in range(PROJ_TM // blk):
                v_ref[g, b, 0:HEAD_DIM, :] = vt[g * HEAD_DIM:(g + 1) * HEAD_DIM, b * blk:(b + 1) * blk]
                v_ref[g, b, HEAD_DIM:V_ROWS, :] = ones[:, 0:blk]

    gate_ref[...] = _sigmoid(_dot_nt(wt_ref[_G_OFF:_SGU_OFF, :], xb))

    mq_ref[...] = _dot(xb, w_ref[:, _MQ_OFF:_PROJ_COLS]).astype(BF16)

    z = _gelu(hz)
    u = z[:, 0:SGU_WIDTH]
    v = _layer_norm(z[:, SGU_WIDTH:2 * SGU_WIDTH], lng_ref[...], lnb_ref[...]).astype(BF16)
    lane_c = lax.broadcasted_iota(jnp.int32, (SGU_CHUNK, LANES), 1)
    low_c = lane_c < (SGU_WIDTH // SGU_GROUPS)
    n_ch = PROJ_TM // SGU_CHUNK
    for gp in range(SGU_WIDTH // LANES):
        cols = slice(gp * LANES, (gp + 1) * LANES)
        vcat = jnp.concatenate([v[ci * SGU_CHUNK:(ci + 1) * SGU_CHUNK, cols] for ci in range(n_ch)], axis=1)
        sv0 = _dot(ws_ref[2 * gp], vcat)
        sv1 = _dot(ws_ref[2 * gp + 1], vcat)
        for ci in range(n_ch):
            rows = slice(ci * SGU_CHUNK, (ci + 1) * SGU_CHUNK)
            lanes = slice(ci * LANES, (ci + 1) * LANES)
            sv = jnp.where(low_c, sv0[:, lanes], sv1[:, lanes]) + bs_ref[:, cols]
            osgu_ref[rows, cols] = (u[rows, cols] * sv).astype(BF16)


def _proj_call(x2, wp, rope, lng, lnb, ws, bs, seq):
    t = x2.shape[0]
    d = x2.shape[1]
    nt = t // PROJ_TM
    per_seq = seq // PROJ_TM
    row = lambda i: (i, 0)
    const2 = lambda i: (0, 0)
    assert PROJ_TM % SEL_CHUNK == 0 and PROJ_TM % Q_BLOCK == 0
    n_g = NSA_KV_GROUPS
    batch = t // seq
    vblock = lambda i: (i // per_seq, i % per_seq, 0, 0)

    def v_shape(blk):
        return jax.ShapeDtypeStruct((batch * n_g, seq // blk, V_ROWS, blk), BF16)

    def v_spec(blk):
        return pl.BlockSpec((n_g, PROJ_TM // blk, V_ROWS, blk), vblock)

    a_shape = jax.ShapeDtypeStruct((t // CMP_STRIDE, CMP_STRIDE * KV_W), BF16)
    a_spec = pl.BlockSpec((PROJ_TM // CMP_STRIDE, CMP_STRIDE * KV_W), row)
    k_shape = jax.ShapeDtypeStruct((t, KV_W), BF16)
    k_spec = pl.BlockSpec((PROJ_TM, KV_W), row)
    out_shapes = (
        jax.ShapeDtypeStruct((t, NSA_HEADS * LANES), BF16),
        a_shape, k_shape, k_shape, a_shape,
        v_shape(SEL_CHUNK), v_shape(Q_BLOCK),
        jax.ShapeDtypeStruct((LANES, t), F32),
        jax.ShapeDtypeStruct((t, SGU_WIDTH), BF16),
        jax.ShapeDtypeStruct((t, MEM_WIDTH), BF16),
    )
    out_specs = (
        pl.BlockSpec((PROJ_TM, NSA_HEADS * LANES), row),
        a_spec, k_spec, k_spec, a_spec,
        v_spec(SEL_CHUNK), v_spec(Q_BLOCK),
        pl.BlockSpec((LANES, PROJ_TM), lambda i: (0, i)),
        pl.BlockSpec((PROJ_TM, SGU_WIDTH), row),
        pl.BlockSpec((PROJ_TM, MEM_WIDTH), row),
    )
    return pl.pallas_call(
        _proj_kernel,
        grid=(nt,),
        in_specs=[
            pl.BlockSpec((PROJ_TM, d), row),
            pl.BlockSpec((_PROJ_COLS, d), const2),
            pl.BlockSpec((PROJ_TM, 3 * LANES), lambda i: (i % per_seq, 0)),
            pl.BlockSpec((1, SGU_WIDTH), const2),
            pl.BlockSpec((1, SGU_WIDTH), const2),
            pl.BlockSpec((SGU_GROUPS, SGU_CHUNK, SGU_CHUNK), lambda i: (0, 0, 0)),
            pl.BlockSpec((SGU_CHUNK, SGU_WIDTH), const2),
        ],
        out_specs=out_specs,
        out_shape=out_shapes,
        scratch_shapes=[pltpu.VMEM((PROJ_TM, KV_W), F32), pltpu.VMEM((d, _PROJ_COLS), BF16)],
        compiler_params=pltpu.CompilerParams(
            dimension_semantics=("arbitrary",), vmem_limit_bytes=VMEM_LIMIT),
        name="proj",
    )(x2, wp, rope, lng, lnb, ws, bs)


N_A = 128
A_W = CMP_STRIDE * KV_W


def _compress_kernel(ka_ref, va_ref, pek_ref, pev_ref, w1k_ref, w2k_ref, w1v_ref, w2v_ref,
                     kcmp_ref, vcmp_ref):
    def one(a_ref, pe_ref, w1_ref, w2_ref, out_ref):
        a = a_ref[...].astype(F32)
        top = (a + pe_ref[0:1, :]).astype(BF16)
        bot = (a + pe_ref[1:2, :]).astype(BF16)
        h1 = _dot(top, w1_ref[0])
        h2 = _dot(bot, w1_ref[1])
        pre = h1 + pltpu.roll(h2, N_A - 1, 0)
        act = _gelu(pre).astype(BF16)
        out_ref[...] = _dot(act, w2_ref[...]).astype(BF16)

    one(ka_ref, pek_ref, w1k_ref, w2k_ref, kcmp_ref)
    one(va_ref, pev_ref, w1v_ref, w2v_ref, vcmp_ref)


def _compress_call(ka, va, pek, pev, w1k, w2k, w1v, w2v, batch):
    row = lambda b: (b, 0)
    c2 = lambda b: (0, 0)
    c3 = lambda b: (0, 0, 0)
    hid2 = NSA_KV_GROUPS * CMP_HIDDEN
    return pl.pallas_call(
        _compress_kernel,
        grid=(batch,),
        in_specs=[
            pl.BlockSpec((N_A, A_W), row),
            pl.BlockSpec((N_A, A_W), row),
            pl.BlockSpec((8, A_W), c2),
            pl.BlockSpec((8, A_W), c2),
            pl.BlockSpec((2, A_W, hid2), c3),
            pl.BlockSpec((hid2, KV_W), c2),
            pl.BlockSpec((2, A_W, hid2), c3),
            pl.BlockSpec((hid2, KV_W), c2),
        ],
        out_specs=(pl.BlockSpec((N_A, KV_W), row), pl.BlockSpec((N_A, KV_W), row)),
        out_shape=(jax.ShapeDtypeStruct((batch * N_A, KV_W), BF16),
                   jax.ShapeDtypeStruct((batch * N_A, KV_W), BF16)),
        compiler_params=pltpu.CompilerParams(
            dimension_semantics=("parallel",), vmem_limit_bytes=VMEM_LIMIT),
        name="compress",
    )(ka, va, pek, pev, w1k, w2k, w1v, w2v)


N_SEL = 32
SEL_CHUNK = 256
SEL_SUB = 128
HQ = NSA_HPG * Q_BLOCK
V_ROWS = HEAD_DIM + 16


def _nsa_kernel(*refs):
    ksl_ref = refs[1]
    reach = pl.program_id(1) // (SEL_CHUNK // Q_BLOCK)
    for n in range(ksl_ref.shape[0] // SEL_CHUNK):
        pl.when(reach == n)(functools.partial(_nsa_body, n, *refs))


def _nsa_body(n_later, q_ref, ksl_ref, kwn_ref, vslt_ref, vwnt_ref, kcmp_ref, vcmpt_ref, gt_ref,
              ot_ref, o_ref, m_s, acc_s, ocmp_s, owin_s, cap_s, out_s):
    c = pl.program_id(1)
    lane_hq = lax.broadcasted_iota(jnp.int32, (1, HQ), 1)
    pos_hq = c * Q_BLOCK + (lane_hq & (Q_BLOCK - 1))
    pos_q = c * Q_BLOCK + lax.broadcasted_iota(jnp.int32, (1, Q_BLOCK), 1)
    groups = range(NSA_KV_GROUPS)
    qgs = [jnp.concatenate(
        [q_ref[:, (g * NSA_HPG + hh) * LANES:(g * NSA_HPG + hh + 1) * LANES]
         for hh in range(NSA_HPG)], axis=0) for g in groups]

    kcmp = kcmp_ref[...]
    s_c = [_dot_nt(kcmp, qgs[g]) for g in groups]
    n_idx = lax.broadcasted_iota(jnp.int32, (N_A, HQ), 0)
    valid_c = (n_idx * CMP_STRIDE + (CMP_LEN - 1)) <= pos_hq
    p_c = []
    for g in groups:
        sm_c = jnp.where(valid_c, s_c[g], NEG)
        m_c = jnp.max(sm_c, axis=0, keepdims=True)
        e_c = jnp.where(valid_c, jnp.exp2(sm_c - m_c), 0.0)
        d_c = jnp.sum(e_c, axis=0, keepdims=True)
        p_c.append(e_c / jnp.where(d_c > 0, d_c, 1.0))
    for g in groups:
        ocmp_s[g] = _dot(vcmpt_ref[g * HEAD_DIM:(g + 1) * HEAD_DIM, :], p_c[g].astype(BF16))

    ot = ot_ref[...]
    imp = []
    for g in groups:
        ps = (p_c[g][:, 0:Q_BLOCK] + p_c[g][:, Q_BLOCK:2 * Q_BLOCK]
              + p_c[g][:, 2 * Q_BLOCK:3 * Q_BLOCK] + p_c[g][:, 3 * Q_BLOCK:4 * Q_BLOCK])
        p_hi = ps.astype(BF16)
        r1 = ps - p_hi.astype(F32)
        p_lo = r1.astype(BF16)
        p_lo2 = (r1 - p_lo.astype(F32)).astype(BF16)
        imp.append(_dot(ot, p_hi) + _dot(ot, p_lo) + _dot(ot, p_lo2))

    j_idx = lax.broadcasted_iota(jnp.int32, (N_SEL, Q_BLOCK), 0)
    cur = pos_q // SLC_BLOCK
    future = j_idx > cur
    forced = (j_idx == 0) | (j_idx == cur) | (j_idx == cur - 1)
    imp = [jnp.where(future, NEG, jnp.where(forced, -NEG, imp[g])) for g in groups]
    rank = [jnp.zeros((N_SEL, Q_BLOCK), F32) for g in groups]
    for i in range(N_SEL):
        for g in groups:
            row = imp[g][i:i + 1, :]
            beats = (row > imp[g]) | ((row == imp[g]) & (j_idx > i))
            rank[g] = rank[g] + jnp.where(beats, 1.0, 0.0)
    for g in groups:
        cap_s[g] = jnp.where(rank[g] < float(SLC_TOPN), -NEG, NEG)
        m_s[g] = jnp.full((1, HQ), NEG, F32)
        acc_s[g] = jnp.zeros((V_ROWS, HQ), F32)

    blocks_per_sub = SEL_SUB // SLC_BLOCK
    subs_per_chunk = SEL_CHUNK // SEL_SUB

    def sel_spans(kc):
        return [slice(kc * SEL_CHUNK + a * SEL_SUB, kc * SEL_CHUNK + (a + 1) * SEL_SUB)
                for a in range(subs_per_chunk)]

    def sel_scores(kc):
        return [[_dot_nt(ksl_ref[ks, :], qgs[g]) for g in groups] for ks in sel_spans(kc)]

    def sel_chunk(kc, s=None):
        spans = sel_spans(kc)
        s = sel_scores(kc) if s is None else s
        for a, ks in enumerate(spans):
            kpos = ks.start + lax.broadcasted_iota(jnp.int32, (SEL_SUB, Q_BLOCK), 0)
            causal = kpos <= pos_q
            for g in groups:
                j0 = ks.start // SLC_BLOCK
                cap = jnp.concatenate(
                    [jnp.broadcast_to(cap_s[g, j:j + 1, :], (SLC_BLOCK, Q_BLOCK))
                     for j in range(j0, j0 + blocks_per_sub)], axis=0)
                cap = jnp.where(causal, cap, NEG)
                sm = jnp.minimum(s[a][g], jnp.concatenate([cap] * NSA_HPG, axis=1))
                m_old = m_s[g]
                m_new = jnp.maximum(m_old, jnp.max(sm, axis=0, keepdims=True))
                alpha = jnp.exp2(m_old - m_new)
                e = jnp.exp2((sm - m_new).astype(BF16))
                v_t = vslt_ref[g, kc, :, a * SEL_SUB:(a + 1) * SEL_SUB]
                acc_s[g] = alpha * acc_s[g] + _dot(v_t, e)
                m_s[g] = m_new

    s0 = sel_scores(0)
    _nsa_window(c, qgs, kwn_ref, vwnt_ref, owin_s)
    s_next = sel_scores(1) if n_later else None
    sel_chunk(0, s0)
    for kc in range(1, n_later + 1):
        s_cur = s_next
        if kc < n_later:
            s_next = sel_scores(kc + 1)
        sel_chunk(kc, s_cur)

    for g in groups:
        acc = acc_s[g]
        o_sel = acc[0:HEAD_DIM, :] * (1.0 / acc[HEAD_DIM:HEAD_DIM + 1, :])

        def gate_row(br):
            return jnp.concatenate(
                [gt_ref[(g * NSA_HPG + hh) * 3 + br:(g * NSA_HPG + hh) * 3 + br + 1, :]
                 for hh in range(NSA_HPG)], axis=1)
        o_t = gate_row(0) * ocmp_s[g] + gate_row(1) * o_sel + gate_row(2) * owin_s[g]
        for hh in range(NSA_HPG):
            h = g * NSA_HPG + hh
            out_s[h * HEAD_DIM:(h + 1) * HEAD_DIM, :] = o_t[:, hh * Q_BLOCK:(hh + 1) * Q_BLOCK]

    o_ref[...] = out_s[...].T.astype(BF16)


def _nsa_window(c, qgs, kwn_ref, vwnt_ref, owin_s):
    groups = range(NSA_KV_GROUPS)
    q_i = lax.broadcasted_iota(jnp.int32, (Q_BLOCK, Q_BLOCK), 1)
    k_i = lax.broadcasted_iota(jnp.int32, (Q_BLOCK, Q_BLOCK), 0)
    blks, caps = [], []
    for i in range(N_BAND + 1):
        blk = c - N_BAND + i
        blks.append(jnp.maximum(blk, 0))
        off = (N_BAND - i) * Q_BLOCK
        if off - (Q_BLOCK - 1) >= 0 and off + (Q_BLOCK - 1) < WINDOW:
            caps.append(jnp.where(blk >= 0, -NEG, NEG))
        else:
            diff = off + q_i - k_i
            ok = (diff >= 0) & (diff < WINDOW) & (blk >= 0)
            caps.append(jnp.concatenate([jnp.where(ok, -NEG, NEG)] * NSA_HPG, axis=1))
    k_blocks = [kwn_ref[pl.ds(pl.multiple_of(blks[i] * Q_BLOCK, Q_BLOCK), Q_BLOCK), :]
                for i in range(N_BAND + 1)]
    s_w = [[None] * (N_BAND + 1) for g in groups]
    for i in reversed(range(N_BAND + 1)):
        for g in groups:
            s_w[g][i] = _dot_nt(k_blocks[i], qgs[g])
    m_w = [None for g in groups]
    o_win = [None for g in groups]
    for i in reversed(range(N_BAND + 1)):
        for g in groups:
            sm = jnp.minimum(s_w[g][i], caps[i])
            m_blk = jnp.max(sm, axis=0, keepdims=True)
            m_new = m_blk if m_w[g] is None else jnp.maximum(m_w[g], m_blk)
            pv = _dot(vwnt_ref[g, blks[i]], jnp.exp2((sm - m_new).astype(BF16)))
            o_win[g] = pv if m_w[g] is None else jnp.exp2(m_w[g] - m_new) * o_win[g] + pv
            m_w[g] = m_new
    for g in groups:
        owin_s[g] = o_win[g][0:HEAD_DIM, :] * (1.0 / o_win[g][HEAD_DIM:HEAD_DIM + 1, :])


def _nsa_call(qpad, ksl, kwn, vslt, vwnt, kcmp, vcmpt, gt, ot, batch, seq):
    nqb = seq // Q_BLOCK
    n_g = NSA_KV_GROUPS
    qrow = lambda b, c: (b * nqb + c, 0)
    brow = lambda b, c: (b, 0)
    c2 = lambda b, c: (0, 0)
    return pl.pallas_call(
        _nsa_kernel,
        grid=(batch, nqb),
        in_specs=[
            pl.BlockSpec((Q_BLOCK, NSA_HEADS * LANES), qrow),
            pl.BlockSpec((seq, KV_W), brow),
            pl.BlockSpec((seq, KV_W), brow),
            pl.BlockSpec((n_g, seq // SEL_CHUNK, V_ROWS, SEL_CHUNK), lambda b, c: (b, 0, 0, 0)),
            pl.BlockSpec((n_g, nqb, V_ROWS, Q_BLOCK), lambda b, c: (b, 0, 0, 0)),
            pl.BlockSpec((N_A, KV_W), brow),
            pl.BlockSpec((KV_W, N_A), brow),
            pl.BlockSpec((LANES, Q_BLOCK), lambda b, c: (0, b * nqb + c)),
            pl.BlockSpec((N_SEL, N_A), c2),
        ],
        out_specs=pl.BlockSpec((Q_BLOCK, Q_W), qrow),
        out_shape=jax.ShapeDtypeStruct((batch * seq, Q_W), BF16),
        scratch_shapes=[
            pltpu.VMEM((n_g, 1, HQ), F32),
            pltpu.VMEM((n_g, V_ROWS, HQ), F32),
            pltpu.VMEM((n_g, HEAD_DIM, HQ), F32),
            pltpu.VMEM((n_g, HEAD_DIM, HQ), F32),
            pltpu.VMEM((n_g, N_SEL, Q_BLOCK), F32),
            pltpu.VMEM((Q_W, Q_BLOCK), F32),
        ],
        compiler_params=pltpu.CompilerParams(
            dimension_semantics=("parallel", "arbitrary"), vmem_limit_bytes=VMEM_LIMIT),
        name="nsa",
    )(qpad, ksl, kwn, vslt, vwnt, kcmp, vcmpt, gt, ot)


MEM_TM = 512
MEM_V_ROWS = MEM_HEAD_DIM + 16


def _memattn_kernel(mq_ref, mem_ref, wkv_ref, o_ref, k_s, vt_s, out_s):
    @pl.when(pl.program_id(1) == 0)
    def _():
        kv = _dot(mem_ref[...].astype(BF16), wkv_ref[...].astype(BF16))
        k_s[...] = kv[:, 0:MEM_WIDTH].astype(BF16)
        vt = kv[:, MEM_WIDTH:2 * MEM_WIDTH].T.astype(BF16)
        for h in range(MEM_HEADS):
            vt_s[h, 0:MEM_HEAD_DIM, :] = vt[h * MEM_HEAD_DIM:(h + 1) * MEM_HEAD_DIM, :]
            vt_s[h, MEM_HEAD_DIM:, :] = jnp.ones((MEM_V_ROWS - MEM_HEAD_DIM, vt.shape[1]), BF16)

    heads = range(MEM_HEADS)
    cols = [slice(h * MEM_HEAD_DIM, (h + 1) * MEM_HEAD_DIM) for h in heads]
    s_t = [_dot_nt(k_s[:, cols[h]], mq_ref[:, cols[h]]) for h in heads]
    e = []
    for h in heads:
        m = jnp.max(s_t[h], axis=0, keepdims=True)
        e.append(jnp.exp2((s_t[h] - m) * (MEM_HEAD_DIM ** -0.5 * LOG2E)).astype(BF16))
    o_t = [_dot(vt_s[h], e[h]) for h in heads]
    for h in heads:
        out_s[cols[h], :] = o_t[h][0:MEM_HEAD_DIM, :] * (1.0 / o_t[h][MEM_HEAD_DIM:MEM_HEAD_DIM + 1, :])
    o_ref[...] = out_s[...].T.astype(BF16)


def _memattn_call(mq, mem2, wkv, batch, seq):
    m_len = mem2.shape[0] // batch
    d = mem2.shape[1]
    per = seq // MEM_TM
    return pl.pallas_call(
        _memattn_kernel,
        grid=(batch, per),
        in_specs=[
            pl.BlockSpec((MEM_TM, MEM_WIDTH), lambda b, i: (b * per + i, 0)),
            pl.BlockSpec((m_len, d), lambda b, i: (b, 0)),
            pl.BlockSpec((d, 2 * MEM_WIDTH), lambda b, i: (0, 0)),
        ],
        out_specs=pl.BlockSpec((MEM_TM, MEM_WIDTH), lambda b, i: (b * per + i, 0)),
        out_shape=jax.ShapeDtypeStruct((batch * seq, MEM_WIDTH), BF16),
        scratch_shapes=[
            pltpu.VMEM((m_len, MEM_WIDTH), BF16),
            pltpu.VMEM((MEM_HEADS, MEM_V_ROWS, m_len), BF16),
            pltpu.VMEM((MEM_WIDTH, MEM_TM), F32),
        ],
        compiler_params=pltpu.CompilerParams(
            dimension_semantics=("parallel", "arbitrary"), vmem_limit_bytes=VMEM_LIMIT),
        name="memattn",
    )(mq, mem2, wkv)


MERGE_TM = 512


def _merge_kernel(x_ref, onsa_ref, osgu_ref, omem_ref, wgt_ref, wbn32_ref, wbs32_ref, wbm32_ref, wo32_ref,
                  g_ref, b_ref, out_ref, wbn_ref, wbs_ref, wbm_ref, wo_ref, wg_ref):
    @pl.when(pl.program_id(0) == 0)
    def _():
        for dst, src in ((wbn_ref, wbn32_ref), (wbs_ref, wbs32_ref), (wbm_ref, wbm32_ref),
                         (wo_ref, wo32_ref)):
            dst[...] = src[...].astype(BF16)
        _transpose_weight(wgt_ref, wg_ref)

    d = x_ref.shape[1]
    branches = ((onsa_ref, wbn_ref), (osgu_ref, wbs_ref), (omem_ref, wbm_ref))
    halves = [slice(h * (MERGE_TM // 2), (h + 1) * (MERGE_TM // 2)) for h in range(2)]
    xs = [x_ref[rows, :] for rows in halves]
    xbs = [x.astype(BF16) for x in xs]
    logits = [[_dot(xb, wg_ref[:, br * d:(br + 1) * d]) for br in range(len(branches))] for xb in xbs]
    projs = [[_dot(o_r[rows, :], w_r[...]) for o_r, w_r in branches] for rows in halves]
    ys = []
    for h in range(2):
        y = None
        for br in range(len(branches)):
            term = _sigmoid(logits[h][br]) * projs[h][br]
            y = term if y is None else y + term
        ys.append(y.astype(BF16))
    outs = [_dot(y, wo_ref[...]) for y in ys]
    for h, rows in enumerate(halves):
        out_ref[rows, :] = _layer_norm(DN_ALPHA * xs[h] + outs[h], g_ref[...], b_ref[...])


def _merge_call(x2, onsa, osgu, omem, wg, wbn, wbs, wbm, wo, g1, b1):
    t, d = x2.shape
    row = lambda i: (i, 0)
    c2 = lambda i: (0, 0)
    full = lambda a: pl.BlockSpec(a.shape, c2)
    once = lambda a: pl.BlockSpec(a.shape, c2, pipeline_mode=pl.Buffered(1))
    return pl.pallas_call(
        _merge_kernel,
        grid=(t // MERGE_TM,),
        in_specs=[
            pl.BlockSpec((MERGE_TM, d), row),
            pl.BlockSpec((MERGE_TM, onsa.shape[1]), row),
            pl.BlockSpec((MERGE_TM, osgu.shape[1]), row),
            pl.BlockSpec((MERGE_TM, omem.shape[1]), row),
            pl.BlockSpec((3 * d, d), c2, pipeline_mode=pl.Buffered(1)),
            once(wbn), once(wbs), once(wbm), once(wo), full(g1), full(b1),
        ],
        out_specs=pl.BlockSpec((MERGE_TM, d), row),
        out_shape=jax.ShapeDtypeStruct((t, d), F32),
        scratch_shapes=[pltpu.VMEM(a.shape, BF16) for a in (wbn, wbs, wbm, wo)]
        + [pltpu.VMEM((d, 3 * d), BF16)],
        compiler_params=pltpu.CompilerParams(
            dimension_semantics=("arbitrary",), vmem_limit_bytes=VMEM_LIMIT),
        name="merge",
    )(x2, onsa, osgu, omem, wg, wbn, wbs, wbm, wo, g1, b1)


MOE_TM = 1024
MOE_CH = 144
_YS_ROWS = -(-(MOE_TM + N_GROUPS * MOE_CH) // 256) * 256
_YS_TYPICAL = -(-(2 * N_GROUPS * MOE_CH) // 256) * 256
_R_SLOT = 8
_DEST_LANE = 3 * _R_SLOT
_HID = EXPERTS_PER_GROUP * EXPERT_FF


def _moe_kernel(x_ref, tri_ref, wr_ref, br_ref, wg_ref, wu_ref, wd_ref, g_ref, b_ref, out_ref,
                xa_s, tok_s, keyr_s, ys_s, cnt_s, base_s):
    grp = pl.program_id(1)
    tm = x_ref.shape[0]
    d = x_ref.shape[1]

    @pl.when(grp == 0)
    def _route():
        xb = x_ref[...].astype(BF16)
        xa_s[:, 0:d] = xb
        lt = _dot_nt(wr_ref[...], xb) + br_ref[...]
        row = lax.broadcasted_iota(jnp.int32, (_R_SLOT, tm), 0)
        gl = jnp.where(row < N_GROUPS, lt[0:_R_SLOT], NEG)
        gmax = jnp.max(gl, axis=0, keepdims=True)
        gidx = jnp.min(jnp.where(gl == gmax, row, _R_SLOT), axis=0, keepdims=True)
        gprob = 1.0 / jnp.sum(jnp.exp(gl - gmax), axis=0, keepdims=True)
        el = lt[_R_SLOT:2 * _R_SLOT]
        for k in range(1, N_GROUPS):
            el = jnp.where(gidx == k, lt[(k + 1) * _R_SLOT:(k + 2) * _R_SLOT], el)
        ee = jnp.exp(el - jnp.max(el, axis=0, keepdims=True))
        ep = ee / jnp.sum(ee, axis=0, keepdims=True)
        t1 = jnp.max(ep, axis=0, keepdims=True)
        i1 = jnp.min(jnp.where(ep == t1, row, _R_SLOT), axis=0, keepdims=True)
        rest = row != i1
        t2 = jnp.max(jnp.where(rest, ep, -1.0), axis=0, keepdims=True)
        i2 = jnp.min(jnp.where(rest & (ep == t2), row, _R_SLOT), axis=0, keepdims=True)
        den = t1 + t2
        cwf = (jnp.where(row == i1, t1 / den, 0.0) + jnp.where(row == i2, t2 / den, 0.0)) * gprob
        hi = cwf.astype(BF16).astype(F32)
        r1 = cwf - hi
        lo = r1.astype(BF16).astype(F32)
        lo2 = (r1 - lo).astype(BF16).astype(F32)

        onehot = jnp.where(row == gidx, 1.0, 0.0)
        tri = tri_ref[...]
        onehot_b = onehot.astype(BF16)
        ranks, before = [], jnp.zeros((_R_SLOT, 1), F32)
        for b in range(tm // LANES):
            blk = slice(b * LANES, (b + 1) * LANES)
            ranks.append(_dot_nt(onehot_b[:, blk], tri) + before)
            before = before + jnp.sum(onehot[:, blk], axis=1, keepdims=True)
        rank = jnp.concatenate(ranks, axis=1)
        keyr_s[...] = jnp.where(onehot > 0.5, rank, -1.0)
        base = jnp.int32(0)
        basev = jnp.zeros((1, tm), F32)
        for k in range(N_GROUPS):
            n_k = jnp.sum(onehot[k:k + 1, :]).astype(jnp.int32)
            cnt_s[k] = n_k
            base_s[k] = base
            basev = jnp.where(gidx == k, base.astype(F32), basev)
            base = base + ((n_k + MOE_CH - 1) // MOE_CH) * MOE_CH
        base_s[N_GROUPS] = base
        dest =jnp.sum(rank * onehot, axis=0, keepdims=True) + basev
        tok = jnp.concatenate(
            [hi, lo, lo2, jnp.broadcast_to(dest, (_R_SLOT, tm)),
             jnp.zeros((LANES - 4 * _R_SLOT, tm), F32)], axis=0).T
        tok_s[...] = tok
        lane = lax.broadcasted_iota(jnp.int32, tok.shape, 1)
        xa_s[:, d:d + LANES] = jnp.where(lane < _DEST_LANE, tok, 0.0).astype(BF16)
        ys_s[...] = jnp.zeros(ys_s.shape, BF16)

    n_rows = cnt_s[grp]
    row0 = base_s[grp]
    keyr = keyr_s[pl.ds(grp, 1), :]
    half = _HID // 2

    def sweep(k, carry):
        ch = MOE_CH
        r_row = (lax.broadcasted_iota(jnp.int32, (ch, tm), 0) + k * MOE_CH).astype(F32)
        pick = jnp.where(keyr == r_row, 1.0, 0.0).astype(BF16)
        ga = _dot(pick, xa_s[...])
        xg = ga[:, 0:d].astype(BF16)
        cwg = ga[:, d:d + LANES]
        cs = (cwg + pltpu.roll(cwg, LANES - _R_SLOT, 1)
              + pltpu.roll(cwg, LANES - 2 * _R_SLOT, 1))
        experts = range(EXPERTS_PER_GROUP)
        per_half = EXPERTS_PER_GROUP // 2
        hg = [_dot(xg, wg_ref[0, e]) for e in experts]
        hu = [_dot(xg, wu_ref[0, e]) for e in experts]
        y = None
        for h in range(2):
            hid = jnp.concatenate(
                [(((hg[e] * _sigmoid(hg[e])) * hu[e]) * cs[:, e:e + 1]).astype(BF16)
                 for e in range(h * per_half, (h + 1) * per_half)], axis=1)
            term = _dot(hid, wd_ref[0, h * half:(h + 1) * half, :])
            y = term if y is None else y + term
        ys_s[pl.ds(pl.multiple_of(row0 + k * MOE_CH, 16), ch), :] = y.astype(BF16)
        return carry

    lax.fori_loop(0, (n_rows + MOE_CH - 1) // MOE_CH, sweep, 0)

    def fin(k_rows):
        n_q = 4
        q_rows = tm // n_q
        r_col = lax.broadcasted_iota(jnp.int32, (q_rows, k_rows), 1).astype(F32)
        parts = [slice(i * q_rows, (i + 1) * q_rows) for i in range(n_q)]
        puts = [jnp.where(tok_s[rows, _DEST_LANE:_DEST_LANE + 1] == r_col, 1.0, 0.0).astype(BF16)
                for rows in parts]
        fs = [_dot(put, ys_s[0:k_rows, :]) for put in puts]
        for rows, f in zip(parts, fs):
            out_ref[rows, :] = _layer_norm(DN_ALPHA * x_ref[rows, :] + f, g_ref[...], b_ref[...])

    last = grp == N_GROUPS - 1
    used = base_s[N_GROUPS]
    pl.when(last & (used <= _YS_TYPICAL))(functools.partial(fin, _YS_TYPICAL))
    pl.when(last & (used > _YS_TYPICAL))(functools.partial(fin, _YS_ROWS))


def _moe_call(x1, wr, br, wg, wu, wd, g2, b2):
    t, d = x1.shape
    row = lambda i, g: (i, 0)
    c2 = lambda i, g: (0, 0)
    idx = np.arange(LANES)
    tri = jnp.asarray(idx[None, :] < idx[:, None], dtype=BF16)
    return pl.pallas_call(
        _moe_kernel,
        grid=(t // MOE_TM, N_GROUPS),
        in_specs=[
            pl.BlockSpec((MOE_TM, d), row, pipeline_mode=pl.Buffered(1)),
            pl.BlockSpec((LANES, LANES), c2),
            pl.BlockSpec(wr.shape, c2),
            pl.BlockSpec(br.shape, c2),
            pl.BlockSpec((1, EXPERTS_PER_GROUP, d, EXPERT_FF), lambda i, g: (g, 0, 0, 0)),
            pl.BlockSpec((1, EXPERTS_PER_GROUP, d, EXPERT_FF), lambda i, g: (g, 0, 0, 0)),
            pl.BlockSpec((1, _HID, d), lambda i, g: (g, 0, 0)),
            pl.BlockSpec(g2.shape, c2),
            pl.BlockSpec(b2.shape, c2),
        ],
        out_specs=pl.BlockSpec((MOE_TM, d), row),
        out_shape=jax.ShapeDtypeStruct((t, d), F32),
        scratch_shapes=[
            pltpu.VMEM((MOE_TM, d + LANES), BF16),
            pltpu.VMEM((MOE_TM, LANES), F32),
            pltpu.VMEM((_R_SLOT, MOE_TM), F32),
            pltpu.VMEM((_YS_ROWS, d), BF16),
            pltpu.SMEM((N_GROUPS,), jnp.int32),
            pltpu.SMEM((N_GROUPS + 1,), jnp.int32),
        ],
        compiler_params=pltpu.CompilerParams(
            dimension_semantics=("parallel", "arbitrary"), vmem_limit_bytes=MOE_VMEM_LIMIT),
        name="moe",
    )(x1, tri, wr, br, wg, wu, wd, g2, b2)


def _rope_table(seq):
    half = ROPE_DIM // 2
    inv = ROPE_THETA ** (-jnp.arange(0, ROPE_DIM, 2, dtype=F32) / ROPE_DIM)
    ang = jnp.arange(seq, dtype=F32)[:, None] * inv[None, :]
    cos, sin = jnp.cos(ang), jnp.sin(ang)
    rest = HEAD_DIM - ROPE_DIM
    one = jnp.ones((seq, rest), F32)
    zero = jnp.zeros((seq, rest), F32)
    zh = jnp.zeros((seq, half), F32)
    c_h = jnp.concatenate([cos, cos, one], axis=1)
    dn_h = jnp.concatenate([-sin, zh, zero], axis=1)
    up_h = jnp.concatenate([zh, sin, zero], axis=1)
    rep = LANES // HEAD_DIM
    return jnp.concatenate([jnp.tile(c_h, (1, rep)), jnp.tile(dn_h, (1, rep)), jnp.tile(up_h, (1, rep))],
                           axis=1)


def _expand_cmp_weights(pe, w1, w2):
    g_n = NSA_KV_GROUPS
    w1r = w1.astype(BF16).reshape(2, CMP_STRIDE, HEAD_DIM, CMP_HIDDEN)
    w2b = w2.astype(BF16)
    w1e = jnp.concatenate(
        [jnp.stack([w1r if g == e else jnp.zeros_like(w1r) for g in range(g_n)], axis=2)
         .reshape(2, A_W, CMP_HIDDEN) for e in range(g_n)], axis=2)
    w2e = jnp.concatenate(
        [jnp.concatenate([w2b if g == e else jnp.zeros_like(w2b) for e in range(g_n)], axis=1)
         for g in range(g_n)], axis=0)
    per = pe.reshape(2, CMP_STRIDE, 1, HEAD_DIM)
    pee = jnp.broadcast_to(per, (2, CMP_STRIDE, g_n, HEAD_DIM)).reshape(2, A_W)
    pee = jnp.concatenate([pee, jnp.zeros((6, A_W), pe.dtype)], axis=0)
    return pee.astype(F32), w1e.astype(BF16), w2e.astype(BF16)


def kernel(x, mem, w_in, cmp_pe_k, cmp_w1_k, cmp_w2_k, cmp_pe_v, cmp_w1_v, cmp_w2_v, sgu_ln_g, sgu_ln_b,
           sgu_w_s, sgu_b_s, w_mem_kv, w_br_nsa, w_br_sgu, w_br_mem, w_o, ln1_g, ln1_b, w_router_group,
           b_router_group, w_router_expert, b_router_expert, w_exp_gate, w_exp_up, w_exp_down, ln2_g, ln2_b):
    batch, seq, d = x.shape
    t = batch * seq
    assert w_in.shape[0] == DEPTH == 1
    assert seq % PROJ_TM == 0 and seq // CMP_STRIDE == N_A and seq // SLC_BLOCK == N_SEL

    offs = [int(v) for v in np.cumsum(
        [0, Q_W, KV_W, KV_W, KV_W, KV_W, KV_W, KV_W, GATE_W, 2 * SGU_WIDTH, MEM_WIDTH, 3 * d])]
    assert offs[-1] == w_in.shape[2]
    w_t = jnp.transpose(w_in[0])
    seg = lambda i: w_t[offs[i]:offs[i + 1]]
    gate_rows = jnp.pad(seg(7), ((0, LANES - GATE_W), (0, 0)))
    wp_t = jnp.concatenate([seg(0), seg(1), seg(3), seg(5), seg(2), seg(4), seg(6), gate_rows, seg(8), seg(9)],
                           axis=0).astype(BF16)
    w_mt = seg(10).astype(BF16)
    assert wp_t.shape[0] == _PROJ_COLS
    rope = _rope_table(seq)
    tril = jnp.tril(jnp.ones((SGU_CHUNK, SGU_CHUNK), dtype=bool))
    ws = jnp.where(tril[None], sgu_w_s[0], 0.0).astype(BF16)
    bs = jnp.repeat(sgu_b_s[0].T, SGU_WIDTH // SGU_GROUPS, axis=1)

    x2 = x.reshape(t, d)
    (qpad, kc, ksl, kwn, vc, vslt, vwnt, gt, osgu, mq) = _proj_call(
        x2, wp_t, rope, sgu_ln_g[0][None], sgu_ln_b[0][None], ws, bs, seq)

    pek, w1k, w2k = _expand_cmp_weights(cmp_pe_k[0], cmp_w1_k[0], cmp_w2_k[0])
    pev, w1v, w2v = _expand_cmp_weights(cmp_pe_v[0], cmp_w1_v[0], cmp_w2_v[0])
    kcmp, vcmp = _compress_call(kc, vc, pek, pev, w1k, w2k, w1v, w2v, batch)

    nqb = seq // Q_BLOCK
    n_g = NSA_KV_GROUPS
    vcmpt = vcmp.reshape(batch, N_A, KV_W).transpose(0, 2, 1).reshape(batch * KV_W, N_A)
    ci = np.arange(N_A)
    sj = np.arange(N_SEL)
    overlap = ((ci[None, :] * CMP_STRIDE + CMP_LEN - 1 >= sj[:, None] * SLC_BLOCK)
               & (ci[None, :] * CMP_STRIDE <= sj[:, None] * SLC_BLOCK + SLC_BLOCK - 1)
               & (ci[None, :] < (seq - CMP_LEN) // CMP_STRIDE + 1))
    ot = jnp.asarray(overlap, dtype=BF16)
    onsa = _nsa_call(qpad, ksl, kwn, vslt, vwnt, kcmp, vcmpt, gt, ot, batch, seq)

    omem = _memattn_call(mq, mem.reshape(batch * mem.shape[1], d), w_mem_kv[0], batch, seq)

    x1 = _merge_call(x2, onsa, osgu, omem, w_mt, w_br_nsa[0], w_br_sgu[0], w_br_mem[0], w_o[0],
                     ln1_g[0][None], ln1_b[0][None])

    assert EXPERTS_PER_GROUP == _R_SLOT and N_GROUPS <= _R_SLOT
    n_r = _R_SLOT + N_GROUPS * EXPERTS_PER_GROUP
    wr = jnp.concatenate([jnp.pad(w_router_group[0], ((0, 0), (0, _R_SLOT - N_GROUPS))),
                          w_router_expert[0]], axis=1)
    wr = jnp.pad(wr, ((0, 0), (0, LANES - n_r))).astype(BF16).T
    br = jnp.concatenate([jnp.pad(b_router_group[0], (0, _R_SLOT - N_GROUPS)), b_router_expert[0]])
    br = jnp.pad(br, (0, LANES - n_r))[:, None]
    wg = w_exp_gate[0].astype(BF16)
    wu = w_exp_up[0].astype(BF16)
    wd = w_exp_down[0].reshape(N_GROUPS, _HID, d).astype(BF16)
    out = _moe_call(x1, wr, br, wg, wu, wd, ln2_g[0][None], ln2_b[0][None])
    return out.reshape(batch, seq, d)
```

```python
import functools

import numpy as np
import jax
import jax.numpy as jnp
from jax import lax
from jax.experimental import pallas as pl
from jax.experimental.pallas import tpu as pltpu

NSA_HEADS = 8
NSA_KV_GROUPS = 2
NSA_HPG = NSA_HEADS // NSA_KV_GROUPS
HEAD_DIM = 64
CMP_LEN = 32
CMP_STRIDE = 16
CMP_HIDDEN = 256
SLC_BLOCK = 64
SLC_TOPN = 8
WINDOW = 512
Q_BLOCK = 128
N_BAND = WINDOW // Q_BLOCK
ROPE_THETA = 500000.0
ROPE_DIM = HEAD_DIM // 4
SGU_CHUNK = 128
SGU_GROUPS = 8
SGU_WIDTH = 512
MEM_HEADS = 4
MEM_HEAD_DIM = 128
MEM_WIDTH = MEM_HEADS * MEM_HEAD_DIM
N_GROUPS = 4
EXPERTS_PER_GROUP = 8
EXPERT_FF = 256
DEPTH = 1
DN_ALPHA = (2.0 * DEPTH) ** 0.25
LN_EPS = 1e-5
NEG = -1e30
LOG2E = 1.4426950408889634

LANES = 128
Q_W = NSA_HEADS * HEAD_DIM
KV_W = NSA_KV_GROUPS * HEAD_DIM
GATE_W = NSA_HEADS * 3
VMEM_LIMIT = 56 * 1024 * 1024
MOE_VMEM_LIMIT = 60 * 1024 * 1024

BF16 = jnp.bfloat16
F32 = jnp.float32


def _dot(a, b):
    return jnp.dot(a, b, preferred_element_type=F32)


def _dot_nt(a, b):
    return lax.dot_general(a, b, (((1,), (1,)), ((), ())), preferred_element_type=F32)


def _sigmoid(x):
    return 1.0 / (1.0 + jnp.exp(-x))


def _gelu(x):
    return 0.5 * x * (1.0 + lax.erf(x * (2.0 ** -0.5)))


def _layer_norm(x, g, b):
    mu = jnp.mean(x, axis=-1, keepdims=True)
    xc = x - mu
    var = jnp.mean(xc * xc, axis=-1, keepdims=True)
    return xc * lax.rsqrt(var + LN_EPS) * g + b


PROJ_TM = 512
_ROPE_COLS = Q_W + 3 * KV_W
_V_OFF = _ROPE_COLS
_G_OFF = _V_OFF + 3 * KV_W
_SGU_OFF = _G_OFF + LANES
_MQ_OFF = _SGU_OFF + 2 * SGU_WIDTH
_PROJ_COLS = _MQ_OFF + MEM_WIDTH


def _store_strided_rows(val, out_ref, tmp_ref):
    tmp_ref[...] = val
    n = val.shape[0] // CMP_STRIDE
    for l in range(CMP_STRIDE):
        out_ref[:, l * LANES:(l + 1) * LANES] = tmp_ref[pl.ds(l, n, stride=CMP_STRIDE), :].astype(BF16)


def _transpose_weight(wt_ref, w_s):
    for j in range(wt_ref.shape[0] // LANES):
        rows = slice(j * LANES, (j + 1) * LANES)
        w_s[:, rows] = wt_ref[rows, :].astype(F32).T.astype(BF16)


def _proj_kernel(x_ref, wt_ref, rope_ref, lng_ref, lnb_ref, ws_ref, bs_ref,
                 q_ref, kc_ref, ksl_ref, kwn_ref, vc_ref, vsl_ref, vwn_ref, gate_ref, osgu_ref, mq_ref,
                 a_s, w_ref):
    pl.when(pl.program_id(0) == 0)(functools.partial(_transpose_weight, wt_ref, w_ref))
    xb = x_ref[...].astype(BF16)
    lane = lax.broadcasted_iota(jnp.int32, (PROJ_TM, LANES), 1)
    low = lane < HEAD_DIM
    cos = rope_ref[:, 0:LANES]
    s_dn = rope_ref[:, LANES:2 * LANES]
    s_up = rope_ref[:, 2 * LANES:3 * LANES]

    hz = _dot(xb, w_ref[:, _SGU_OFF:_MQ_OFF])
    h = _dot(xb, w_ref[:, 0:_ROPE_COLS])
    k_refs = (kc_ref, ksl_ref, kwn_ref)
    for j in range(_ROPE_COLS // LANES):
        blk = h[:, j * LANES:(j + 1) * LANES]
        r = (blk * cos + pltpu.roll(blk, LANES - ROPE_DIM // 2, 1) * s_dn
             + pltpu.roll(blk, ROPE_DIM // 2, 1) * s_up)
        if j < Q_W // LANES:
            r = r * (HEAD_DIM ** -0.5 * LOG2E)
            sw = pltpu.roll(r, HEAD_DIM, 1)
            g = (2 * j) // NSA_HPG
            if g == 0:
                h0 = jnp.where(low, r, 0.0)
                h1 = jnp.where(low, sw, 0.0)
            else:
                h0 = jnp.where(low, 0.0, sw)
                h1 = jnp.where(low, 0.0, r)
            q_ref[:, (2 * j) * LANES:(2 * j + 1) * LANES] = h0.astype(BF16)
            q_ref[:, (2 * j + 1) * LANES:(2 * j + 2) * LANES] = h1.astype(BF16)
        else:
            if j == Q_W // LANES:
                _store_strided_rows(r, kc_ref, a_s)
            else:
                k_refs[j - Q_W // LANES][...] = r.astype(BF16)

    _store_strided_rows(_dot(xb, w_ref[:, _V_OFF:_V_OFF + LANES]), vc_ref, a_s)
    ones = jnp.ones((V_ROWS - HEAD_DIM, PROJ_TM), BF16)
    for v_ref, row0, blk in ((vsl_ref, _V_OFF + LANES, SEL_CHUNK), (vwn_ref, _V_OFF + 2 * LANES, Q_BLOCK)):
        vt = _dot_nt(wt_ref[row0:row0 + LANES, :], xb).astype(BF16)
        for g in range(NSA_KV_GROUPS):
            for b in range(PROJ_TM // blk):
                v_ref[g, b, 0:HEAD_DIM, :] = vt[g * HEAD_DIM:(g + 1) * HEAD_DIM, b * blk:(b + 1) * blk]
                v_ref[g, b, HEAD_DIM:V_ROWS, :] = ones[:, 0:blk]

    gate_ref[...] = _sigmoid(_dot_nt(wt_ref[_G_OFF:_SGU_OFF, :], xb))

    mq_ref[...] = _dot(xb, w_ref[:, _MQ_OFF:_PROJ_COLS]).astype(BF16)

    z = _gelu(hz)
    u = z[:, 0:SGU_WIDTH]
    v = _layer_norm(z[:, SGU_WIDTH:2 * SGU_WIDTH], lng_ref[...], lnb_ref[...]).astype(BF16)
    lane_c = lax.broadcasted_iota(jnp.int32, (SGU_CHUNK, LANES), 1)
    low_c = lane_c < (SGU_WIDTH // SGU_GROUPS)
    n_ch = PROJ_TM // SGU_CHUNK
    for gp in range(SGU_WIDTH // LANES):
        cols = slice(gp * LANES, (gp + 1) * LANES)
        vcat = jnp.concatenate([v[ci * SGU_CHUNK:(ci + 1) * SGU_CHUNK, cols] for ci in range(n_ch)], axis=1)
        sv0 = _dot(ws_ref[2 * gp], vcat)
        sv1 = _dot(ws_ref[2 * gp + 1], vcat)
        for ci in range(n_ch):
            rows = slice(ci * SGU_CHUNK, (ci + 1) * SGU_CHUNK)
            lanes = slice(ci * LANES, (ci + 1) * LANES)
            sv = jnp.where(low_c, sv0[:, lanes], sv1[:, lanes]) + bs_ref[:, cols]
            osgu_ref[rows, cols] = (u[rows, cols] * sv).astype(BF16)


def _proj_call(x2, wp, rope, lng, lnb, ws, bs, seq):
    t = x2.shape[0]
    d = x2.shape[1]
    nt = t // PROJ_TM
    per_seq = seq // PROJ_TM
    row = lambda i: (i, 0)
    const2 = lambda i: (0, 0)
    assert PROJ_TM % SEL_CHUNK == 0 and PROJ_TM % Q_BLOCK == 0
    n_g = NSA_KV_GROUPS
    batch = t // seq
    vblock = lambda i: (i // per_seq, i % per_seq, 0, 0)

    def v_shape(blk):
        return jax.ShapeDtypeStruct((batch * n_g, seq // blk, V_ROWS, blk), BF16)

    def v_spec(blk):
        return pl.BlockSpec((n_g, PROJ_TM // blk, V_ROWS, blk), vblock)

    a_shape = jax.ShapeDtypeStruct((t // CMP_STRIDE, CMP_STRIDE * KV_W), BF16)
    a_spec = pl.BlockSpec((PROJ_TM // CMP_STRIDE, CMP_STRIDE * KV_W), row)
    k_shape = jax.ShapeDtypeStruct((t, KV_W), BF16)
    k_spec = pl.BlockSpec((PROJ_TM, KV_W), row)
    out_shapes = (
        jax.ShapeDtypeStruct((t, NSA_HEADS * LANES), BF16),
        a_shape, k_shape, k_shape, a_shape,
        v_shape(SEL_CHUNK), v_shape(Q_BLOCK),
        jax.ShapeDtypeStruct((LANES, t), F32),
        jax.ShapeDtypeStruct((t, SGU_WIDTH), BF16),
        jax.ShapeDtypeStruct((t, MEM_WIDTH), BF16),
    )
    out_specs = (
        pl.BlockSpec((PROJ_TM, NSA_HEADS * LANES), row),
        a_spec, k_spec, k_spec, a_spec,
        v_spec(SEL_CHUNK), v_spec(Q_BLOCK),
        pl.BlockSpec((LANES, PROJ_TM), lambda i: (0, i)),
        pl.BlockSpec((PROJ_TM, SGU_WIDTH), row),
        pl.BlockSpec((PROJ_TM, MEM_WIDTH), row),
    )
    return pl.pallas_call(
        _proj_kernel,
        grid=(nt,),
        in_specs=[
            pl.BlockSpec((PROJ_TM, d), row),
            pl.BlockSpec((_PROJ_COLS, d), const2),
            pl.BlockSpec((PROJ_TM, 3 * LANES), lambda i: (i % per_seq, 0)),
            pl.BlockSpec((1, SGU_WIDTH), const2),
            pl.BlockSpec((1, SGU_WIDTH), const2),
            pl.BlockSpec((SGU_GROUPS, SGU_CHUNK, SGU_CHUNK), lambda i: (0, 0, 0)),
            pl.BlockSpec((SGU_CHUNK, SGU_WIDTH), const2),
        ],
        out_specs=out_specs,
        out_shape=out_shapes,
        scratch_shapes=[pltpu.VMEM((PROJ_TM, KV_W), F32), pltpu.VMEM((d, _PROJ_COLS), BF16)],
        compiler_params=pltpu.CompilerParams(
            dimension_semantics=("arbitrary",), vmem_limit_bytes=VMEM_LIMIT),
        name="proj",
    )(x2, wp, rope, lng, lnb, ws, bs)


N_A = 128
A_W = CMP_STRIDE * KV_W


def _compress_kernel(ka_ref, va_ref, pek_ref, pev_ref, w1k_ref, w2k_ref, w1v_ref, w2v_ref,
                     kcmp_ref, vcmp_ref):
    def one(a_ref, pe_ref, w1_ref, w2_ref, out_ref):
        a = a_ref[...].astype(F32)
        top = (a + pe_ref[0:1, :]).astype(BF16)
        bot = (a + pe_ref[1:2, :]).astype(BF16)
        h1 = _dot(top, w1_ref[0])
        h2 = _dot(bot, w1_ref[1])
        pre = h1 + pltpu.roll(h2, N_A - 1, 0)
        act = _gelu(pre).astype(BF16)
        out_ref[...] = _dot(act, w2_ref[...]).astype(BF16)

    one(ka_ref, pek_ref, w1k_ref, w2k_ref, kcmp_ref)
    one(va_ref, pev_ref, w1v_ref, w2v_ref, vcmp_ref)


def _compress_call(ka, va, pek, pev, w1k, w2k, w1v, w2v, batch):
    row = lambda b: (b, 0)
    c2 = lambda b: (0, 0)
    c3 = lambda b: (0, 0, 0)
    hid2 = NSA_KV_GROUPS * CMP_HIDDEN
    return pl.pallas_call(
        _compress_kernel,
        grid=(batch,),
        in_specs=[
            pl.BlockSpec((N_A, A_W), row),
            pl.BlockSpec((N_A, A_W), row),
            pl.BlockSpec((8, A_W), c2),
            pl.BlockSpec((8, A_W), c2),
            pl.BlockSpec((2, A_W, hid2), c3),
            pl.BlockSpec((hid2, KV_W), c2),
            pl.BlockSpec((2, A_W, hid2), c3),
            pl.BlockSpec((hid2, KV_W), c2),
        ],
        out_specs=(pl.BlockSpec((N_A, KV_W), row), pl.BlockSpec((N_A, KV_W), row)),
        out_shape=(jax.ShapeDtypeStruct((batch * N_A, KV_W), BF16),
                   jax.ShapeDtypeStruct((batch * N_A, KV_W), BF16)),
        compiler_params=pltpu.CompilerParams(
            dimension_semantics=("parallel",), vmem_limit_bytes=VMEM_LIMIT),
        name="compress",
    )(ka, va, pek, pev, w1k, w2k, w1v, w2v)


N_SEL = 32
SEL_CHUNK = 256
SEL_SUB = 128
HQ = NSA_HPG * Q_BLOCK
V_ROWS = HEAD_DIM + 16


def _nsa_kernel(*refs):
    ksl_ref = refs[1]
    reach = pl.program_id(1) // (SEL_CHUNK // Q_BLOCK)
    for n in range(ksl_ref.shape[0] // SEL_CHUNK):
        pl.when(reach == n)(functools.partial(_nsa_body, n, *refs))


def _nsa_body(n_later, q_ref, ksl_ref, kwn_ref, vslt_ref, vwnt_ref, kcmp_ref, vcmpt_ref, gt_ref,
              ot_ref, o_ref, m_s, acc_s, ocmp_s, owin_s, cap_s, out_s):
    c = pl.program_id(1)
    lane_hq = lax.broadcasted_iota(jnp.int32, (1, HQ), 1)
    pos_hq = c * Q_BLOCK + (lane_hq & (Q_BLOCK - 1))
    pos_q = c * Q_BLOCK + lax.broadcasted_iota(jnp.int32, (1, Q_BLOCK), 1)
    groups = range(NSA_KV_GROUPS)
    qgs = [jnp.concatenate(
        [q_ref[:, (g * NSA_HPG + hh) * LANES:(g * NSA_HPG + hh + 1) * LANES]
         for hh in range(NSA_HPG)], axis=0) for g in groups]

    kcmp = kcmp_ref[...]
    s_c = [_dot_nt(kcmp, qgs[g]) for g in groups]
    n_idx = lax.broadcasted_iota(jnp.int32, (N_A, HQ), 0)
    valid_c = (n_idx * CMP_STRIDE + (CMP_LEN - 1)) <= pos_hq
    p_c = []
    for g in groups:
        sm_c = jnp.where(valid_c, s_c[g], NEG)
        m_c = jnp.max(sm_c, axis=0, keepdims=True)
        e_c = jnp.where(valid_c, jnp.exp2(sm_c - m_c), 0.0)
        d_c = jnp.sum(e_c, axis=0, keepdims=True)
        p_c.append(e_c / jnp.where(d_c > 0, d_c, 1.0))
    for g in groups:
        ocmp_s[g] = _dot(vcmpt_ref[g * HEAD_DIM:(g + 1) * HEAD_DIM, :], p_c[g].astype(BF16))

    ot = ot_ref[...]
    imp = []
    for g in groups:
        ps = (p_c[g][:, 0:Q_BLOCK] + p_c[g][:, Q_BLOCK:2 * Q_BLOCK]
              + p_c[g][:, 2 * Q_BLOCK:3 * Q_BLOCK] + p_c[g][:, 3 * Q_BLOCK:4 * Q_BLOCK])
        p_hi = ps.astype(BF16)
        r1 = ps - p_hi.astype(F32)
        p_lo = r1.astype(BF16)
        p_lo2 = (r1 - p_lo.astype(F32)).astype(BF16)
        imp.append(_dot(ot, p_hi) + _dot(ot, p_lo) + _dot(ot, p_lo2))

    j_idx = lax.broadcasted_iota(jnp.int32, (N_SEL, Q_BLOCK), 0)
    cur = pos_q // SLC_BLOCK
    future = j_idx > cur
    forced = (j_idx == 0) | (j_idx == cur) | (j_idx == cur - 1)
    imp = [jnp.where(future, NEG, jnp.where(forced, -NEG, imp[g])) for g in groups]
    rank = [jnp.zeros((N_SEL, Q_BLOCK), F32) for g in groups]
    for i in range(N_SEL):
        for g in groups:
            row = imp[g][i:i + 1, :]
            beats = (row > imp[g]) | ((row == imp[g]) & (j_idx > i))
            rank[g] = rank[g] + jnp.where(beats, 1.0, 0.0)
    for g in groups:
        cap_s[g] = jnp.where(rank[g] < float(SLC_TOPN), -NEG, NEG)
        m_s[g] = jnp.full((1, HQ), NEG, F32)
        acc_s[g] = jnp.zeros((V_ROWS, HQ), F32)

    blocks_per_sub = SEL_SUB // SLC_BLOCK
    subs_per_chunk = SEL_CHUNK // SEL_SUB

    def sel_spans(kc):
        return [slice(kc * SEL_CHUNK + a * SEL_SUB, kc * SEL_CHUNK + (a + 1) * SEL_SUB)
                for a in range(subs_per_chunk)]

    def sel_scores(kc):
        return [[_dot_nt(ksl_ref[ks, :], qgs[g]) for g in groups] for ks in sel_spans(kc)]

    def sel_chunk(kc, s=None):
        spans = sel_spans(kc)
        s = sel_scores(kc) if s is None else s
        for a, ks in enumerate(spans):
            kpos = ks.start + lax.broadcasted_iota(jnp.int32, (SEL_SUB, Q_BLOCK), 0)
            causal = kpos <= pos_q
            for g in groups:
                j0 = ks.start // SLC_BLOCK
                cap = jnp.concatenate(
                    [jnp.broadcast_to(cap_s[g, j:j + 1, :], (SLC_BLOCK, Q_BLOCK))
                     for j in range(j0, j0 + blocks_per_sub)], axis=0)
                cap = jnp.where(causal, cap, NEG)
                sm = jnp.minimum(s[a][g], jnp.concatenate([cap] * NSA_HPG, axis=1))
                m_old = m_s[g]
                m_new = jnp.maximum(m_old, jnp.max(sm, axis=0, keepdims=True))
                alpha = jnp.exp2(m_old - m_new)
                e = jnp.exp2((sm - m_new).astype(BF16))
                v_t = vslt_ref[g, kc, :, a * SEL_SUB:(a + 1) * SEL_SUB]
                acc_s[g] = alpha * acc_s[g] + _dot(v_t, e)
                m_s[g] = m_new

    s0 = sel_scores(0)
    _nsa_window(c, qgs, kwn_ref, vwnt_ref, owin_s)
    s_next = sel_scores(1) if n_later else None
    sel_chunk(0, s0)
    for kc in range(1, n_later + 1):
        s_cur = s_next
        if kc < n_later:
            s_next = sel_scores(kc + 1)
        sel_chunk(kc, s_cur)

    for g in groups:
        acc = acc_s[g]
        o_sel = acc[0:HEAD_DIM, :] * (1.0 / acc[HEAD_DIM:HEAD_DIM + 1, :])

        def gate_row(br):
            return jnp.concatenate(
                [gt_ref[(g * NSA_HPG + hh) * 3 + br:(g * NSA_HPG + hh) * 3 + br + 1, :]
                 for hh in range(NSA_HPG)], axis=1)
        o_t = gate_row(0) * ocmp_s[g] + gate_row(1) * o_sel + gate_row(2) * owin_s[g]
        for hh in range(NSA_HPG):
            h = g * NSA_HPG + hh
            out_s[h * HEAD_DIM:(h + 1) * HEAD_DIM, :] = o_t[:, hh * Q_BLOCK:(hh + 1) * Q_BLOCK]

    o_ref[...] = out_s[...].T.astype(BF16)


def _nsa_window(c, qgs, kwn_ref, vwnt_ref, owin_s):
    groups = range(NSA_KV_GROUPS)
    q_i = lax.broadcasted_iota(jnp.int32, (Q_BLOCK, Q_BLOCK), 1)
    k_i = lax.broadcasted_iota(jnp.int32, (Q_BLOCK, Q_BLOCK), 0)
    blks, caps = [], []
    for i in range(N_BAND + 1):
        blk = c - N_BAND + i
        blks.append(jnp.maximum(blk, 0))
        off = (N_BAND - i) * Q_BLOCK
        if off - (Q_BLOCK - 1) >= 0 and off + (Q_BLOCK - 1) < WINDOW:
            caps.append(jnp.where(blk >= 0, -NEG, NEG))
        else:
            diff = off + q_i - k_i
            ok = (diff >= 0) & (diff < WINDOW) & (blk >= 0)
            caps.append(jnp.concatenate([jnp.where(ok, -NEG, NEG)] * NSA_HPG, axis=1))
    k_blocks = [kwn_ref[pl.ds(pl.multiple_of(blks[i] * Q_BLOCK, Q_BLOCK), Q_BLOCK), :]
                for i in range(N_BAND + 1)]
    s_w = [[None] * (N_BAND + 1) for g in groups]
    for i in reversed(range(N_BAND + 1)):
        for g in groups:
            s_w[g][i] = _dot_nt(k_blocks[i], qgs[g])
    m_w = [None for g in groups]
    o_win = [None for g in groups]
    for i in reversed(range(N_BAND + 1)):
        for g in groups:
            sm = jnp.minimum(s_w[g][i], caps[i])
            m_blk = jnp.max(sm, axis=0, keepdims=True)
            m_new = m_blk if m_w[g] is None else jnp.maximum(m_w[g], m_blk)
            pv = _dot(vwnt_ref[g, blks[i]], jnp.exp2((sm - m_new).astype(BF16)))
            o_win[g] = pv if m_w[g] is None else jnp.exp2(m_w[g] - m_new) * o_win[g] + pv
            m_w[g] = m_new
    for g in groups:
        owin_s[g] = o_win[g][0:HEAD_DIM, :] * (1.0 / o_win[g][HEAD_DIM:HEAD_DIM + 1, :])


def _nsa_call(qpad, ksl, kwn, vslt, vwnt, kcmp, vcmpt, gt, ot, batch, seq):
    nqb = seq // Q_BLOCK
    n_g = NSA_KV_GROUPS
    qrow = lambda b, c: (b * nqb + c, 0)
    brow = lambda b, c: (b, 0)
    c2 = lambda b, c: (0, 0)
    return pl.pallas_call(
        _nsa_kernel,
        grid=(batch, nqb),
        in_specs=[
            pl.BlockSpec((Q_BLOCK, NSA_HEADS * LANES), qrow),
            pl.BlockSpec((seq, KV_W), brow),
            pl.BlockSpec((seq, KV_W), brow),
            pl.BlockSpec((n_g, seq // SEL_CHUNK, V_ROWS, SEL_CHUNK), lambda b, c: (b, 0, 0, 0)),
            pl.BlockSpec((n_g, nqb, V_ROWS, Q_BLOCK), lambda b, c: (b, 0, 0, 0)),
            pl.BlockSpec((N_A, KV_W), brow),
            pl.BlockSpec((KV_W, N_A), brow),
            pl.BlockSpec((LANES, Q_BLOCK), lambda b, c: (0, b * nqb + c)),
            pl.BlockSpec((N_SEL, N_A), c2),
        ],
        out_specs=pl.BlockSpec((Q_BLOCK, Q_W), qrow),
        out_shape=jax.ShapeDtypeStruct((batch * seq, Q_W), BF16),
        scratch_shapes=[
            pltpu.VMEM((n_g, 1, HQ), F32),
            pltpu.VMEM((n_g, V_ROWS, HQ), F32),
            pltpu.VMEM((n_g, HEAD_DIM, HQ), F32),
            pltpu.VMEM((n_g, HEAD_DIM, HQ), F32),
            pltpu.VMEM((n_g, N_SEL, Q_BLOCK), F32),
            pltpu.VMEM((Q_W, Q_BLOCK), F32),
        ],
        compiler_params=pltpu.CompilerParams(
            dimension_semantics=("parallel", "arbitrary"), vmem_limit_bytes=VMEM_LIMIT),
        name="nsa",
    )(qpad, ksl, kwn, vslt, vwnt, kcmp, vcmpt, gt, ot)


MEM_TM = 512
MEM_V_ROWS = MEM_HEAD_DIM + 16


def _memattn_kernel(mq_ref, mem_ref, wkv_ref, o_ref, k_s, vt_s, out_s):
    @pl.when(pl.program_id(1) == 0)
    def _():
        kv = _dot(mem_ref[...].astype(BF16), wkv_ref[...].astype(BF16))
        k_s[...] = kv[:, 0:MEM_WIDTH].astype(BF16)
        vt = kv[:, MEM_WIDTH:2 * MEM_WIDTH].T.astype(BF16)
        for h in range(MEM_HEADS):
            vt_s[h, 0:MEM_HEAD_DIM, :] = vt[h * MEM_HEAD_DIM:(h + 1) * MEM_HEAD_DIM, :]
            vt_s[h, MEM_HEAD_DIM:, :] = jnp.ones((MEM_V_ROWS - MEM_HEAD_DIM, vt.shape[1]), BF16)

    heads = range(MEM_HEADS)
    cols = [slice(h * MEM_HEAD_DIM, (h + 1) * MEM_HEAD_DIM) for h in heads]
    s_t = [_dot_nt(k_s[:, cols[h]], mq_ref[:, cols[h]]) for h in heads]
    e = []
    for h in heads:
        m = jnp.max(s_t[h], axis=0, keepdims=True)
        e.append(jnp.exp2((s_t[h] - m) * (MEM_HEAD_DIM ** -0.5 * LOG2E)).astype(BF16))
    o_t = [_dot(vt_s[h], e[h]) for h in heads]
    for h in heads:
        out_s[cols[h], :] = o_t[h][0:MEM_HEAD_DIM, :] * (1.0 / o_t[h][MEM_HEAD_DIM:MEM_HEAD_DIM + 1, :])
    o_ref[...] = out_s[...].T.astype(BF16)


def _memattn_call(mq, mem2, wkv, batch, seq):
    m_len = mem2.shape[0] // batch
    d = mem2.shape[1]
    per = seq // MEM_TM
    return pl.pallas_call(
        _memattn_kernel,
        grid=(batch, per),
        in_specs=[
            pl.BlockSpec((MEM_TM, MEM_WIDTH), lambda b, i: (b * per + i, 0)),
            pl.BlockSpec((m_len, d), lambda b, i: (b, 0)),
            pl.BlockSpec((d, 2 * MEM_WIDTH), lambda b, i: (0, 0)),
        ],
        out_specs=pl.BlockSpec((MEM_TM, MEM_WIDTH), lambda b, i: (b * per + i, 0)),
        out_shape=jax.ShapeDtypeStruct((batch * seq, MEM_WIDTH), BF16),
        scratch_shapes=[
            pltpu.VMEM((m_len, MEM_WIDTH), BF16),
            pltpu.VMEM((MEM_HEADS, MEM_V_ROWS, m_len), BF16),
            pltpu.VMEM((MEM_WIDTH, MEM_TM), F32),
        ],
        compiler_params=pltpu.CompilerParams(
            dimension_semantics=("parallel", "arbitrary"), vmem_limit_bytes=VMEM_LIMIT),
        name="memattn",
    )(mq, mem2, wkv)


MERGE_TM = 512


def _merge_kernel(x_ref, onsa_ref, osgu_ref, omem_ref, wgt_ref, wbn32_ref, wbs32_ref, wbm32_ref, wo32_ref,
                  g_ref, b_ref, out_ref, wbn_ref, wbs_ref, wbm_ref, wo_ref, wg_ref):
    @pl.when(pl.program_id(0) == 0)
    def _():
        for dst, src in ((wbn_ref, wbn32_ref), (wbs_ref, wbs32_ref), (wbm_ref, wbm32_ref),
                         (wo_ref, wo32_ref)):
            dst[...] = src[...].astype(BF16)
        _transpose_weight(wgt_ref, wg_ref)

    d = x_ref.shape[1]
    branches = ((onsa_ref, wbn_ref), (osgu_ref, wbs_ref), (omem_ref, wbm_ref))
    halves = [slice(h * (MERGE_TM // 2), (h + 1) * (MERGE_TM // 2)) for h in range(2)]
    xs = [x_ref[rows, :] for rows in halves]
    xbs = [x.astype(BF16) for x in xs]
    logits = [[_dot(xb, wg_ref[:, br * d:(br + 1) * d]) for br in range(len(branches))] for xb in xbs]
    projs = [[_dot(o_r[rows, :], w_r[...]) for o_r, w_r in branches] for rows in halves]
    ys = []
    for h in range(2):
        y = None
        for br in range(len(branches)):
            term = _sigmoid(logits[h][br]) * projs[h][br]
            y = term if y is None else y + term
        ys.append(y.astype(BF16))
    outs = [_dot(y, wo_ref[...]) for y in ys]
    for h, rows in enumerate(halves):
        out_ref[rows, :] = _layer_norm(DN_ALPHA * xs[h] + outs[h], g_ref[...], b_ref[...])


def _merge_call(x2, onsa, osgu, omem, wg, wbn, wbs, wbm, wo, g1, b1):
    t, d = x2.shape
    row = lambda i: (i, 0)
    c2 = lambda i: (0, 0)
    full = lambda a: pl.BlockSpec(a.shape, c2)
    once = lambda a: pl.BlockSpec(a.shape, c2, pipeline_mode=pl.Buffered(1))
    return pl.pallas_call(
        _merge_kernel,
        grid=(t // MERGE_TM,),
        in_specs=[
            pl.BlockSpec((MERGE_TM, d), row),
            pl.BlockSpec((MERGE_TM, onsa.shape[1]), row),
            pl.BlockSpec((MERGE_TM, osgu.shape[1]), row),
            pl.BlockSpec((MERGE_TM, omem.shape[1]), row),
            pl.BlockSpec((3 * d, d), c2, pipeline_mode=pl.Buffered(1)),
            once(wbn), once(wbs), once(wbm), once(wo), full(g1), full(b1),
        ],
        out_specs=pl.BlockSpec((MERGE_TM, d), row),
        out_shape=jax.ShapeDtypeStruct((t, d), F32),
        scratch_shapes=[pltpu.VMEM(a.shape, BF16) for a in (wbn, wbs, wbm, wo)]
        + [pltpu.VMEM((d, 3 * d), BF16)],
        compiler_params=pltpu.CompilerParams(
            dimension_semantics=("arbitrary",), vmem_limit_bytes=VMEM_LIMIT),
        name="merge",
    )(x2, onsa, osgu, omem, wg, wbn, wbs, wbm, wo, g1, b1)


MOE_TM = 1024
MOE_CH = 144
_YS_ROWS = -(-(MOE_TM + N_GROUPS * MOE_CH) // 256) * 256
_YS_TYPICAL = -(-(2 * N_GROUPS * MOE_CH) // 256) * 256
_R_SLOT = 8
_DEST_LANE = 3 * _R_SLOT
_HID = EXPERTS_PER_GROUP * EXPERT_FF


def _moe_kernel(x_ref, tri_ref, wr_ref, br_ref, wg_ref, wu_ref, wd_ref, g_ref, b_ref, out_ref,
                xa_s, tok_s, keyr_s, ys_s, cnt_s, base_s):
    grp = pl.program_id(1)
    tm = x_ref.shape[0]
    d = x_ref.shape[1]

    @pl.when(grp == 0)
    def _route():
        xb = x_ref[...].astype(BF16)
        xa_s[:, 0:d] = xb
        lt = _dot_nt(wr_ref[...], xb) + br_ref[...]
        row = lax.broadcasted_iota(jnp.int32, (_R_SLOT, tm), 0)
        gl = jnp.where(row < N_GROUPS, lt[0:_R_SLOT], NEG)
        gmax = jnp.max(gl, axis=0, keepdims=True)
        gidx = jnp.min(jnp.where(gl == gmax, row, _R_SLOT), axis=0, keepdims=True)
        gprob = 1.0 / jnp.sum(jnp.exp(gl - gmax), axis=0, keepdims=True)
        el = lt[_R_SLOT:2 * _R_SLOT]
        for k in range(1, N_GROUPS):
            el = jnp.where(gidx == k, lt[(k + 1) * _R_SLOT:(k + 2) * _R_SLOT], el)
        ee = jnp.exp(el - jnp.max(el, axis=0, keepdims=True))
        ep = ee / jnp.sum(ee, axis=0, keepdims=True)
        t1 = jnp.max(ep, axis=0, keepdims=True)
        i1 = jnp.min(jnp.where(ep == t1, row, _R_SLOT), axis=0, keepdims=True)
        rest = row != i1
        t2 = jnp.max(jnp.where(rest, ep, -1.0), axis=0, keepdims=True)
        i2 = jnp.min(jnp.where(rest & (ep == t2), row, _R_SLOT), axis=0, keepdims=True)
        den = t1 + t2
        cwf = (jnp.where(row == i1, t1 / den, 0.0) + jnp.where(row == i2, t2 / den, 0.0)) * gprob
        hi = cwf.astype(BF16).astype(F32)
        r1 = cwf - hi
        lo = r1.astype(BF16).astype(F32)
        lo2 = (r1 - lo).astype(BF16).astype(F32)

        onehot = jnp.where(row == gidx, 1.0, 0.0)
        tri = tri_ref[...]
        onehot_b = onehot.astype(BF16)
        ranks, before = [], jnp.zeros((_R_SLOT, 1), F32)
        for b in range(tm // LANES):
            blk = slice(b * LANES, (b + 1) * LANES)
            ranks.append(_dot_nt(onehot_b[:, blk], tri) + before)
            before = before + jnp.sum(onehot[:, blk], axis=1, keepdims=True)
        rank = jnp.concatenate(ranks, axis=1)
        keyr_s[...] = jnp.where(onehot > 0.5, rank, -1.0)
        base = jnp.int32(0)
        basev = jnp.zeros((1, tm), F32)
        for k in range(N_GROUPS):
            n_k = jnp.sum(onehot[k:k + 1, :]).astype(jnp.int32)
            cnt_s[k] = n_k
            base_s[k] = base
            basev = jnp.where(gidx == k, base.astype(F32), basev)
            base = base + ((n_k + MOE_CH - 1) // MOE_CH) * MOE_CH
        base_s[N_GROUPS] = base
        dest =jnp.sum(rank * onehot, axis=0, keepdims=True) + basev
        tok = jnp.concatenate(
            [hi, lo, lo2, jnp.broadcast_to(dest, (_R_SLOT, tm)),
             jnp.zeros((LANES - 4 * _R_SLOT, tm), F32)], axis=0).T
        tok_s[...] = tok
        lane = lax.broadcasted_iota(jnp.int32, tok.shape, 1)
        xa_s[:, d:d + LANES] = jnp.where(lane < _DEST_LANE, tok, 0.0).astype(BF16)
        ys_s[...] = jnp.zeros(ys_s.shape, BF16)

    n_rows = cnt_s[grp]
    row0 = base_s[grp]
    keyr = keyr_s[pl.ds(grp, 1), :]
    half = _HID // 2

    def sweep(k, carry):
        ch = MOE_CH
        r_row = (lax.broadcasted_iota(jnp.int32, (ch, tm), 0) + k * MOE_CH).astype(F32)
        pick = jnp.where(keyr == r_row, 1.0, 0.0).astype(BF16)
        ga = _dot(pick, xa_s[...])
        xg = ga[:, 0:d].astype(BF16)
        cwg = ga[:, d:d + LANES]
        cs = (cwg + pltpu.roll(cwg, LANES - _R_SLOT, 1)
              + pltpu.roll(cwg, LANES - 2 * _R_SLOT, 1))
        experts = range(EXPERTS_PER_GROUP)
        per_half = EXPERTS_PER_GROUP // 2
        hg = [_dot(xg, wg_ref[0, e]) for e in experts]
        hu = [_dot(xg, wu_ref[0, e]) for e in experts]
        y = None
        for h in range(2):
            hid = jnp.concatenate(
                [(((hg[e] * _sigmoid(hg[e])) * hu[e]) * cs[:, e:e + 1]).astype(BF16)
                 for e in range(h * per_half, (h + 1) * per_half)], axis=1)
            term = _dot(hid, wd_ref[0, h * half:(h + 1) * half, :])
            y = term if y is None else y + term
        ys_s[pl.ds(pl.multiple_of(row0 + k * MOE_CH, 16), ch), :] = y.astype(BF16)
        return carry

    lax.fori_loop(0, (n_rows + MOE_CH - 1) // MOE_CH, sweep, 0)

    def fin(k_rows):
        n_q = 4
        q_rows = tm // n_q
        r_col = lax.broadcasted_iota(jnp.int32, (q_rows, k_rows), 1).astype(F32)
        parts = [slice(i * q_rows, (i + 1) * q_rows) for i in range(n_q)]
        puts = [jnp.where(tok_s[rows, _DEST_LANE:_DEST_LANE + 1] == r_col, 1.0, 0.0).astype(BF16)
                for rows in parts]
        fs = [_dot(put, ys_s[0:k_rows, :]) for put in puts]
        for rows, f in zip(parts, fs):
            out_ref[rows, :] = _layer_norm(DN_ALPHA * x_ref[rows, :] + f, g_ref[...], b_ref[...])

    last = grp == N_GROUPS - 1
    used = base_s[N_GROUPS]
    pl.when(last & (used <= _YS_TYPICAL))(functools.partial(fin, _YS_TYPICAL))
    pl.when(last & (used > _YS_TYPICAL))(functools.partial(fin, _YS_ROWS))


def _moe_call(x1, wr, br, wg, wu, wd, g2, b2):
    t, d = x1.shape
    row = lambda i, g: (i, 0)
    c2 = lambda i, g: (0, 0)
    idx = np.arange(LANES)
    tri = jnp.asarray(idx[None, :] < idx[:, None], dtype=BF16)
    return pl.pallas_call(
        _moe_kernel,
        grid=(t // MOE_TM, N_GROUPS),
        in_specs=[
            pl.BlockSpec((MOE_TM, d), row),
            pl.BlockSpec((LANES, LANES), c2),
            pl.BlockSpec(wr.shape, c2),
            pl.BlockSpec(br.shape, c2),
            pl.BlockSpec((1, EXPERTS_PER_GROUP, d, EXPERT_FF), lambda i, g: (g, 0, 0, 0)),
            pl.BlockSpec((1, EXPERTS_PER_GROUP, d, EXPERT_FF), lambda i, g: (g, 0, 0, 0)),
            pl.BlockSpec((1, _HID, d), lambda i, g: (g, 0, 0)),
            pl.BlockSpec(g2.shape, c2),
            pl.BlockSpec(b2.shape, c2),
        ],
        out_specs=pl.BlockSpec((MOE_TM, d), row),
        out_shape=jax.ShapeDtypeStruct((t, d), F32),
        scratch_shapes=[
            pltpu.VMEM((MOE_TM, d + LANES), BF16),
            pltpu.VMEM((MOE_TM, LANES), F32),
            pltpu.VMEM((_R_SLOT, MOE_TM), F32),
            pltpu.VMEM((_YS_ROWS, d), BF16),
            pltpu.SMEM((N_GROUPS,), jnp.int32),
            pltpu.SMEM((N_GROUPS + 1,), jnp.int32),
        ],
        compiler_params=pltpu.CompilerParams(
            dimension_semantics=("parallel", "arbitrary"), vmem_limit_bytes=MOE_VMEM_LIMIT),
        name="moe",
    )(x1, tri, wr, br, wg, wu, wd, g2, b2)


def _rope_table(seq):
    half = ROPE_DIM // 2
    inv = ROPE_THETA ** (-jnp.arange(0, ROPE_DIM, 2, dtype=F32) / ROPE_DIM)
    ang = jnp.arange(seq, dtype=F32)[:, None] * inv[None, :]
    cos, sin = jnp.cos(ang), jnp.sin(ang)
    rest = HEAD_DIM - ROPE_DIM
    one = jnp.ones((seq, rest), F32)
    zero = jnp.zeros((seq, rest), F32)
    zh = jnp.zeros((seq, half), F32)
    c_h = jnp.concatenate([cos, cos, one], axis=1)
    dn_h = jnp.concatenate([-sin, zh, zero], axis=1)
    up_h = jnp.concatenate([zh, sin, zero], axis=1)
    rep = LANES // HEAD_DIM
    return jnp.concatenate([jnp.tile(c_h, (1, rep)), jnp.tile(dn_h, (1, rep)), jnp.tile(up_h, (1, rep))],
                           axis=1)


def _expand_cmp_weights(pe, w1, w2):
    g_n = NSA_KV_GROUPS
    w1r = w1.astype(BF16).reshape(2, CMP_STRIDE, HEAD_DIM, CMP_HIDDEN)
    w2b = w2.astype(BF16)
    w1e = jnp.concatenate(
        [jnp.stack([w1r if g == e else jnp.zeros_like(w1r) for g in range(g_n)], axis=2)
         .reshape(2, A_W, CMP_HIDDEN) for e in range(g_n)], axis=2)
    w2e = jnp.concatenate(
        [jnp.concatenate([w2b if g == e else jnp.zeros_like(w2b) for e in range(g_n)], axis=1)
         for g in range(g_n)], axis=0)
    per = pe.reshape(2, CMP_STRIDE, 1, HEAD_DIM)
    pee = jnp.broadcast_to(per, (2, CMP_STRIDE, g_n, HEAD_DIM)).reshape(2, A_W)
    pee = jnp.concatenate([pee, jnp.zeros((6, A_W), pe.dtype)], axis=0)
    return pee.astype(F32), w1e.astype(BF16), w2e.astype(BF16)


def kernel(x, mem, w_in, cmp_pe_k, cmp_w1_k, cmp_w2_k, cmp_pe_v, cmp_w1_v, cmp_w2_v, sgu_ln_g, sgu_ln_b,
           sgu_w_s, sgu_b_s, w_mem_kv, w_br_nsa, w_br_sgu, w_br_mem, w_o, ln1_g, ln1_b, w_router_group,
           b_router_group, w_router_expert, b_router_expert, w_exp_gate, w_exp_up, w_exp_down, ln2_g, ln2_b):
    batch, seq, d = x.shape
    t = batch * seq
    assert w_in.shape[0] == DEPTH == 1
    assert seq % PROJ_TM == 0 and seq // CMP_STRIDE == N_A and seq // SLC_BLOCK == N_SEL

    offs = [int(v) for v in np.cumsum(
        [0, Q_W, KV_W, KV_W, KV_W, KV_W, KV_W, KV_W, GATE_W, 2 * SGU_WIDTH, MEM_WIDTH, 3 * d])]
    assert offs[-1] == w_in.shape[2]
    w_t = jnp.transpose(w_in[0])
    seg = lambda i: w_t[offs[i]:offs[i + 1]]
    gate_rows = jnp.pad(seg(7), ((0, LANES - GATE_W), (0, 0)))
    wp_t = jnp.concatenate([seg(0), seg(1), seg(3), seg(5), seg(2), seg(4), seg(6), gate_rows, seg(8), seg(9)],
                           axis=0).astype(BF16)
    w_mt = seg(10).astype(BF16)
    assert wp_t.shape[0] == _PROJ_COLS
    rope = _rope_table(seq)
    tril = jnp.tril(jnp.ones((SGU_CHUNK, SGU_CHUNK), dtype=bool))
    ws = jnp.where(tril[None], sgu_w_s[0], 0.0).astype(BF16)
    bs = jnp.repeat(sgu_b_s[0].T, SGU_WIDTH // SGU_GROUPS, axis=1)

    x2 = x.reshape(t, d)
    (qpad, kc, ksl, kwn, vc, vslt, vwnt, gt, osgu, mq) = _proj_call(
        x2, wp_t, rope, sgu_ln_g[0][None], sgu_ln_b[0][None], ws, bs, seq)

    pek, w1k, w2k = _expand_cmp_weights(cmp_pe_k[0], cmp_w1_k[0], cmp_w2_k[0])
    pev, w1v, w2v = _expand_cmp_weights(cmp_pe_v[0], cmp_w1_v[0], cmp_w2_v[0])
    kcmp, vcmp = _compress_call(kc, vc, pek, pev, w1k, w2k, w1v, w2v, batch)

    nqb = seq // Q_BLOCK
    n_g = NSA_KV_GROUPS
    vcmpt = vcmp.reshape(batch, N_A, KV_W).transpose(0, 2, 1).reshape(batch * KV_W, N_A)
    ci = np.arange(N_A)
    sj = np.arange(N_SEL)
    overlap = ((ci[None, :] * CMP_STRIDE + CMP_LEN - 1 >= sj[:, None] * SLC_BLOCK)
               & (ci[None, :] * CMP_STRIDE <= sj[:, None] * SLC_BLOCK + SLC_BLOCK - 1)
               & (ci[None, :] < (seq - CMP_LEN) // CMP_STRIDE + 1))
    ot = jnp.asarray(overlap, dtype=BF16)
    onsa = _nsa_call(qpad, ksl, kwn, vslt, vwnt, kcmp, vcmpt, gt, ot, batch, seq)

    omem = _memattn_call(mq, mem.reshape(batch * mem.shape[1], d), w_mem_kv[0], batch, seq)

    x1 = _merge_call(x2, onsa, osgu, omem, w_mt, w_br_nsa[0], w_br_sgu[0], w_br_mem[0], w_o[0],
                     ln1_g[0][None], ln1_b[0][None])

    assert EXPERTS_PER_GROUP == _R_SLOT and N_GROUPS <= _R_SLOT
    n_r = _R_SLOT + N_GROUPS * EXPERTS_PER_GROUP
    wr = jnp.concatenate([jnp.pad(w_router_group[0], ((0, 0), (0, _R_SLOT - N_GROUPS))),
                          w_router_expert[0]], axis=1)
    wr = jnp.pad(wr, ((0, 0), (0, LANES - n_r))).astype(BF16).T
    br = jnp.concatenate([jnp.pad(b_router_group[0], (0, _R_SLOT - N_GROUPS)), b_router_expert[0]])
    br = jnp.pad(br, (0, LANES - n_r))[:, None]
    wg = w_exp_gate[0].astype(BF16)
    wu = w_exp_up[0].astype(BF16)
    wd = w_exp_down[0].reshape(N_GROUPS, _HID, d).astype(BF16)
    out = _moe_call(x1, wr, br, wg, wu, wd, ln2_g[0][None], ln2_b[0][None])
    return out.reshape(batch, seq, d)
```

```python
import functools

import numpy as np
import jax
import jax.numpy as jnp
from jax import lax
from jax.experimental import pallas as pl
from jax.experimental.pallas import tpu as pltpu

NSA_HEADS = 8
NSA_KV_GROUPS = 2
NSA_HPG = NSA_HEADS // NSA_KV_GROUPS
HEAD_DIM = 64
CMP_LEN = 32
CMP_STRIDE = 16
CMP_HIDDEN = 256
SLC_BLOCK = 64
SLC_TOPN = 8
WINDOW = 512
Q_BLOCK = 128
N_BAND = WINDOW // Q_BLOCK
ROPE_THETA = 500000.0
ROPE_DIM = HEAD_DIM // 4
SGU_CHUNK = 128
SGU_GROUPS = 8
SGU_WIDTH = 512
MEM_HEADS = 4
MEM_HEAD_DIM = 128
MEM_WIDTH = MEM_HEADS * MEM_HEAD_DIM
N_GROUPS = 4
EXPERTS_PER_GROUP = 8
EXPERT_FF = 256
DEPTH = 1
DN_ALPHA = (2.0 * DEPTH) ** 0.25
LN_EPS = 1e-5
NEG = -1e30
LOG2E = 1.4426950408889634

LANES = 128
Q_W = NSA_HEADS * HEAD_DIM
KV_W = NSA_KV_GROUPS * HEAD_DIM
GATE_W = NSA_HEADS * 3
VMEM_LIMIT = 56 * 1024 * 1024
MOE_VMEM_LIMIT = 60 * 1024 * 1024

BF16 = jnp.bfloat16
F32 = jnp.float32


def _dot(a, b):
    return jnp.dot(a, b, preferred_element_type=F32)


def _dot_nt(a, b):
    return lax.dot_general(a, b, (((1,), (1,)), ((), ())), preferred_element_type=F32)


def _sigmoid(x):
    return 1.0 / (1.0 + jnp.exp(-x))


def _gelu(x):
    return 0.5 * x * (1.0 + lax.erf(x * (2.0 ** -0.5)))


def _layer_norm(x, g, b):
    mu = jnp.mean(x, axis=-1, keepdims=True)
    xc = x - mu
    var = jnp.mean(xc * xc, axis=-1, keepdims=True)
    return xc * lax.rsqrt(var + LN_EPS) * g + b


PROJ_TM = 512
_ROPE_COLS = Q_W + 3 * KV_W
_V_OFF = _ROPE_COLS
_G_OFF = _V_OFF + 3 * KV_W
_SGU_OFF = _G_OFF + LANES
_MQ_OFF = _SGU_OFF + 2 * SGU_WIDTH
_PROJ_COLS = _MQ_OFF + MEM_WIDTH


def _store_strided_rows(val, out_ref, tmp_ref):
    tmp_ref[...] = val
    n = val.shape[0] // CMP_STRIDE
    for l in range(CMP_STRIDE):
        out_ref[:, l * LANES:(l + 1) * LANES] = tmp_ref[pl.ds(l, n, stride=CMP_STRIDE), :].astype(BF16)


def _transpose_weight(wt_ref, w_s):
    for j in range(wt_ref.shape[0] // LANES):
        rows = slice(j * LANES, (j + 1) * LANES)
        w_s[:, rows] = wt_ref[rows, :].astype(F32).T.astype(BF16)


def _proj_kernel(x_ref, wt_ref, rope_ref, lng_ref, lnb_ref, ws_ref, bs_ref,
                 q_ref, kc_ref, ksl_ref, kwn_ref, vc_ref, vsl_ref, vwn_ref, gate_ref, osgu_ref, mq_ref,
                 a_s, w_ref):
    pl.when(pl.program_id(0) == 0)(functools.partial(_transpose_weight, wt_ref, w_ref))
    xb = x_ref[...].astype(BF16)
    lane = lax.broadcasted_iota(jnp.int32, (PROJ_TM, LANES), 1)
    low = lane < HEAD_DIM
    cos = rope_ref[:, 0:LANES]
    s_dn = rope_ref[:, LANES:2 * LANES]
    s_up = rope_ref[:, 2 * LANES:3 * LANES]

    hz = _dot(xb, w_ref[:, _SGU_OFF:_MQ_OFF])
    h = _dot(xb, w_ref[:, 0:_ROPE_COLS])
    k_refs = (kc_ref, ksl_ref, kwn_ref)
    for j in range(_ROPE_COLS // LANES):
        blk = h[:, j * LANES:(j + 1) * LANES]
        r = (blk * cos + pltpu.roll(blk, LANES - ROPE_DIM // 2, 1) * s_dn
             + pltpu.roll(blk, ROPE_DIM // 2, 1) * s_up)
        if j < Q_W // LANES:
            r = r * (HEAD_DIM ** -0.5 * LOG2E)
            sw = pltpu.roll(r, HEAD_DIM, 1)
            g = (2 * j) // NSA_HPG
            if g == 0:
                h0 = jnp.where(low, r, 0.0)
                h1 = jnp.where(low, sw, 0.0)
            else:
                h0 = jnp.where(low, 0.0, sw)
                h1 = jnp.where(low, 0.0, r)
            q_ref[:, (2 * j) * LANES:(2 * j + 1) * LANES] = h0.astype(BF16)
            q_ref[:, (2 * j + 1) * LANES:(2 * j + 2) * LANES] = h1.astype(BF16)
        else:
            if j == Q_W // LANES:
                _store_strided_rows(r, kc_ref, a_s)
            else:
                k_refs[j - Q_W // LANES][...] = r.astype(BF16)

    _store_strided_rows(_dot(xb, w_ref[:, _V_OFF:_V_OFF + LANES]), vc_ref, a_s)
    ones = jnp.ones((V_ROWS - HEAD_DIM, PROJ_TM), BF16)
    for v_ref, row0, blk in ((vsl_ref, _V_OFF + LANES, SEL_CHUNK), (vwn_ref, _V_OFF + 2 * LANES, Q_BLOCK)):
        vt = _dot_nt(wt_ref[row0:row0 + LANES, :], xb).astype(BF16)
        for g in range(NSA_KV_GROUPS):
            for b in range(PROJ_TM // blk):
                v_ref[g, b, 0:HEAD_DIM, :] = vt[g * HEAD_DIM:(g + 1) * HEAD_DIM, b * blk:(b + 1) * blk]
                v_ref[g, b, HEAD_DIM:V_ROWS, :] = ones[:, 0:blk]

    gate_ref[...] = _sigmoid(_dot_nt(wt_ref[_G_OFF:_SGU_OFF, :], xb))

    mq_ref[...] = _dot(xb, w_ref[:, _MQ_OFF:_PROJ_COLS]).astype(BF16)

    z = _gelu(hz)
    u = z[:, 0:SGU_WIDTH]
    v = _layer_norm(z[:, SGU_WIDTH:2 * SGU_WIDTH], lng_ref[...], lnb_ref[...]).astype(BF16)
    lane_c = lax.broadcasted_iota(jnp.int32, (SGU_CHUNK, LANES), 1)
    low_c = lane_c < (SGU_WIDTH // SGU_GROUPS)
    n_ch = PROJ_TM // SGU_CHUNK
    for gp in range(SGU_WIDTH // LANES):
        cols = slice(gp * LANES, (gp + 1) * LANES)
        vcat = jnp.concatenate([v[ci * SGU_CHUNK:(ci + 1) * SGU_CHUNK, cols] for ci in range(n_ch)], axis=1)
        sv0 = _dot(ws_ref[2 * gp], vcat)
        sv1 = _dot(ws_ref[2 * gp + 1], vcat)
        for ci in range(n_ch):
            rows = slice(ci * SGU_CHUNK, (ci + 1) * SGU_CHUNK)
            lanes = slice(ci * LANES, (ci + 1) * LANES)
            sv = jnp.where(low_c, sv0[:, lanes], sv1[:, lanes]) + bs_ref[:, cols]
            osgu_ref[rows, cols] = (u[rows, cols] * sv).astype(BF16)


def _proj_call(x2, wp, rope, lng, lnb, ws, bs, seq):
    t = x2.shape[0]
    d = x2.shape[1]
    nt = t // PROJ_TM
    per_seq = seq // PROJ_TM
    row = lambda i: (i, 0)
    const2 = lambda i: (0, 0)
    assert PROJ_TM % SEL_CHUNK == 0 and PROJ_TM % Q_BLOCK == 0
    n_g = NSA_KV_GROUPS
    batch = t // seq
    vblock = lambda i: (i // per_seq, i % per_seq, 0, 0)

    def v_shape(blk):
        return jax.ShapeDtypeStruct((batch * n_g, seq // blk, V_ROWS, blk), BF16)

    def v_spec(blk):
        return pl.BlockSpec((n_g, PROJ_TM // blk, V_ROWS, blk), vblock)

    a_shape = jax.ShapeDtypeStruct((t // CMP_STRIDE, CMP_STRIDE * KV_W), BF16)
    a_spec = pl.BlockSpec((PROJ_TM // CMP_STRIDE, CMP_STRIDE * KV_W), row)
    k_shape = jax.ShapeDtypeStruct((t, KV_W), BF16)
    k_spec = pl.BlockSpec((PROJ_TM, KV_W), row)
    out_shapes = (
        jax.ShapeDtypeStruct((t, NSA_HEADS * LANES), BF16),
        a_shape, k_shape, k_shape, a_shape,
        v_shape(SEL_CHUNK), v_shape(Q_BLOCK),
        jax.ShapeDtypeStruct((LANES, t), F32),
        jax.ShapeDtypeStruct((t, SGU_WIDTH), BF16),
        jax.ShapeDtypeStruct((t, MEM_WIDTH), BF16),
    )
    out_specs = (
        pl.BlockSpec((PROJ_TM, NSA_HEADS * LANES), row),
        a_spec, k_spec, k_spec, a_spec,
        v_spec(SEL_CHUNK), v_spec(Q_BLOCK),
        pl.BlockSpec((LANES, PROJ_TM), lambda i: (0, i)),
        pl.BlockSpec((PROJ_TM, SGU_WIDTH), row),
        pl.BlockSpec((PROJ_TM, MEM_WIDTH), row),
    )
    return pl.pallas_call(
        _proj_kernel,
        grid=(nt,),
        in_specs=[
            pl.BlockSpec((PROJ_TM, d), row),
            pl.BlockSpec((_PROJ_COLS, d), const2),
            pl.BlockSpec((PROJ_TM, 3 * LANES), lambda i: (i % per_seq, 0)),
            pl.BlockSpec((1, SGU_WIDTH), const2),
            pl.BlockSpec((1, SGU_WIDTH), const2),
            pl.BlockSpec((SGU_GROUPS, SGU_CHUNK, SGU_CHUNK), lambda i: (0, 0, 0)),
            pl.BlockSpec((SGU_CHUNK, SGU_WIDTH), const2),
        ],
        out_specs=out_specs,
        out_shape=out_shapes,
        scratch_shapes=[pltpu.VMEM((PROJ_TM, KV_W), F32), pltpu.VMEM((d, _PROJ_COLS), BF16)],
        compiler_params=pltpu.CompilerParams(
            dimension_semantics=("arbitrary",), vmem_limit_bytes=VMEM_LIMIT),
        name="proj",
    )(x2, wp, rope, lng, lnb, ws, bs)


N_A = 128
A_W = CMP_STRIDE * KV_W


def _compress_kernel(ka_ref, va_ref, pek_ref, pev_ref, w1k_ref, w2k_ref, w1v_ref, w2v_ref,
                     kcmp_ref, vcmp_ref):
    def one(a_ref, pe_ref, w1_ref, w2_ref, out_ref):
        a = a_ref[...].astype(F32)
        top = (a + pe_ref[0:1, :]).astype(BF16)
        bot = (a + pe_ref[1:2, :]).astype(BF16)
        h1 = _dot(top, w1_ref[0])
        h2 = _dot(bot, w1_ref[1])
        pre = h1 + pltpu.roll(h2, N_A - 1, 0)
        act = _gelu(pre).astype(BF16)
        out_ref[...] = _dot(act, w2_ref[...]).astype(BF16)

    one(ka_ref, pek_ref, w1k_ref, w2k_ref, kcmp_ref)
    one(va_ref, pev_ref, w1v_ref, w2v_ref, vcmp_ref)


def _compress_call(ka, va, pek, pev, w1k, w2k, w1v, w2v, batch):
    row = lambda b: (b, 0)
    c2 = lambda b: (0, 0)
    c3 = lambda b: (0, 0, 0)
    hid2 = NSA_KV_GROUPS * CMP_HIDDEN
    return pl.pallas_call(
        _compress_kernel,
        grid=(batch,),
        in_specs=[
            pl.BlockSpec((N_A, A_W), row),
            pl.BlockSpec((N_A, A_W), row),
            pl.BlockSpec((8, A_W), c2),
            pl.BlockSpec((8, A_W), c2),
            pl.BlockSpec((2, A_W, hid2), c3),
            pl.BlockSpec((hid2, KV_W), c2),
            pl.BlockSpec((2, A_W, hid2), c3),
            pl.BlockSpec((hid2, KV_W), c2),
        ],
        out_specs=(pl.BlockSpec((N_A, KV_W), row), pl.BlockSpec((N_A, KV_W), row)),
        out_shape=(jax.ShapeDtypeStruct((batch * N_A, KV_W), BF16),
                   jax.ShapeDtypeStruct((batch * N_A, KV_W), BF16)),
        compiler_params=pltpu.CompilerParams(
            dimension_semantics=("parallel",), vmem_limit_bytes=VMEM_LIMIT),
        name="compress",
    )(ka, va, pek, pev, w1k, w2k, w1v, w2v)


N_SEL = 32
SEL_CHUNK = 256
SEL_SUB = 128
HQ = NSA_HPG * Q_BLOCK
V_ROWS = HEAD_DIM + 16


def _nsa_kernel(*refs):
    ksl_ref = refs[1]
    reach = pl.program_id(1) // (SEL_CHUNK // Q_BLOCK)
    for n in range(ksl_ref.shape[0] // SEL_CHUNK):
        pl.when(reach == n)(functools.partial(_nsa_body, n, *refs))


def _nsa_body(n_later, q_ref, ksl_ref, kwn_ref, vslt_ref, vwnt_ref, kcmp_ref, vcmpt_ref, gt_ref,
              ot_ref, o_ref, m_s, acc_s, ocmp_s, owin_s, cap_s, out_s):
    c = pl.program_id(1)
    lane_hq = lax.broadcasted_iota(jnp.int32, (1, HQ), 1)
    pos_hq = c * Q_BLOCK + (lane_hq & (Q_BLOCK - 1))
    pos_q = c * Q_BLOCK + lax.broadcasted_iota(jnp.int32, (1, Q_BLOCK), 1)
    groups = range(NSA_KV_GROUPS)
    qgs = [jnp.concatenate(
        [q_ref[:, (g * NSA_HPG + hh) * LANES:(g * NSA_HPG + hh + 1) * LANES]
         for hh in range(NSA_HPG)], axis=0) for g in groups]

    kcmp = kcmp_ref[...]
    s_c = [_dot_nt(kcmp, qgs[g]) for g in groups]
    n_idx = lax.broadcasted_iota(jnp.int32, (N_A, HQ), 0)
    valid_c = (n_idx * CMP_STRIDE + (CMP_LEN - 1)) <= pos_hq
    p_c = []
    for g in groups:
        sm_c = jnp.where(valid_c, s_c[g], NEG)
        m_c = jnp.max(sm_c, axis=0, keepdims=True)
        e_c = jnp.where(valid_c, jnp.exp2(sm_c - m_c), 0.0)
        d_c = jnp.sum(e_c, axis=0, keepdims=True)
        p_c.append(e_c / jnp.where(d_c > 0, d_c, 1.0))
    for g in groups:
        ocmp_s[g] = _dot(vcmpt_ref[g * HEAD_DIM:(g + 1) * HEAD_DIM, :], p_c[g].astype(BF16))

    ot = ot_ref[...]
    imp = []
    for g in groups:
        ps = (p_c[g][:, 0:Q_BLOCK] + p_c[g][:, Q_BLOCK:2 * Q_BLOCK]
              + p_c[g][:, 2 * Q_BLOCK:3 * Q_BLOCK] + p_c[g][:, 3 * Q_BLOCK:4 * Q_BLOCK])
        p_hi = ps.astype(BF16)
        r1 = ps - p_hi.astype(F32)
        p_lo = r1.astype(BF16)
        p_lo2 = (r1 - p_lo.astype(F32)).astype(BF16)
        imp.append(_dot(ot, p_hi) + _dot(ot, p_lo) + _dot(ot, p_lo2))

    j_idx = lax.broadcasted_iota(jnp.int32, (N_SEL, Q_BLOCK), 0)
    cur = pos_q // SLC_BLOCK
    future = j_idx > cur
    forced = (j_idx == 0) | (j_idx == cur) | (j_idx == cur - 1)
    imp = [jnp.where(future, NEG, jnp.where(forced, -NEG, imp[g])) for g in groups]
    rank = [jnp.zeros((N_SEL, Q_BLOCK), F32) for g in groups]
    for i in range(N_SEL):
        for g in groups:
            row = imp[g][i:i + 1, :]
            beats = (row > imp[g]) | ((row == imp[g]) & (j_idx > i))
            rank[g] = rank[g] + jnp.where(beats, 1.0, 0.0)
    for g in groups:
        cap_s[g] = jnp.where(rank[g] < float(SLC_TOPN), -NEG, NEG)
        m_s[g] = jnp.full((1, HQ), NEG, F32)
        acc_s[g] = jnp.zeros((V_ROWS, HQ), F32)

    blocks_per_sub = SEL_SUB // SLC_BLOCK
    subs_per_chunk = SEL_CHUNK // SEL_SUB

    def sel_spans(kc):
        return [slice(kc * SEL_CHUNK + a * SEL_SUB, kc * SEL_CHUNK + (a + 1) * SEL_SUB)
                for a in range(subs_per_chunk)]

    def sel_scores(kc):
        return [[_dot_nt(ksl_ref[ks, :], qgs[g]) for g in groups] for ks in sel_spans(kc)]

    def sel_chunk(kc, s=None):
        spans = sel_spans(kc)
        s = sel_scores(kc) if s is None else s
        for a, ks in enumerate(spans):
            kpos = ks.start + lax.broadcasted_iota(jnp.int32, (SEL_SUB, Q_BLOCK), 0)
            causal = kpos <= pos_q
            for g in groups:
                j0 = ks.start // SLC_BLOCK
                cap = jnp.concatenate(
                    [jnp.broadcast_to(cap_s[g, j:j + 1, :], (SLC_BLOCK, Q_BLOCK))
                     for j in range(j0, j0 + blocks_per_sub)], axis=0)
                cap = jnp.where(causal, cap, NEG)
                sm = jnp.minimum(s[a][g], jnp.concatenate([cap] * NSA_HPG, axis=1))
                m_old = m_s[g]
                m_new = jnp.maximum(m_old, jnp.max(sm, axis=0, keepdims=True))
                alpha = jnp.exp2(m_old - m_new)
                e = jnp.exp2((sm - m_new).astype(BF16))
                v_t = vslt_ref[g, kc, :, a * SEL_SUB:(a + 1) * SEL_SUB]
                acc_s[g] = alpha * acc_s[g] + _dot(v_t, e)
                m_s[g] = m_new

    s0 = sel_scores(0)
    _nsa_window(c, qgs, kwn_ref, vwnt_ref, owin_s)
    s_next = sel_scores(1) if n_later else None
    sel_chunk(0, s0)
    for kc in range(1, n_later + 1):
        s_cur = s_next
        if kc < n_later:
            s_next = sel_scores(kc + 1)
        sel_chunk(kc, s_cur)

    for g in groups:
        acc = acc_s[g]
        o_sel = acc[0:HEAD_DIM, :] * (1.0 / acc[HEAD_DIM:HEAD_DIM + 1, :])

        def gate_row(br):
            return jnp.concatenate(
                [gt_ref[(g * NSA_HPG + hh) * 3 + br:(g * NSA_HPG + hh) * 3 + br + 1, :]
                 for hh in range(NSA_HPG)], axis=1)
        o_t = gate_row(0) * ocmp_s[g] + gate_row(1) * o_sel + gate_row(2) * owin_s[g]
        for hh in range(NSA_HPG):
            h = g * NSA_HPG + hh
            out_s[h * HEAD_DIM:(h + 1) * HEAD_DIM, :] = o_t[:, hh * Q_BLOCK:(hh + 1) * Q_BLOCK]

    o_ref[...] = out_s[...].T.astype(BF16)


def _nsa_window(c, qgs, kwn_ref, vwnt_ref, owin_s):
    groups = range(NSA_KV_GROUPS)
    q_i = lax.broadcasted_iota(jnp.int32, (Q_BLOCK, Q_BLOCK), 1)
    k_i = lax.broadcasted_iota(jnp.int32, (Q_BLOCK, Q_BLOCK), 0)
    blks, caps = [], []
    for i in range(N_BAND + 1):
        blk = c - N_BAND + i
        blks.append(jnp.maximum(blk, 0))
        off = (N_BAND - i) * Q_BLOCK
        if off - (Q_BLOCK - 1) >= 0 and off + (Q_BLOCK - 1) < WINDOW:
            caps.append(jnp.where(blk >= 0, -NEG, NEG))
        else:
            diff = off + q_i - k_i
            ok = (diff >= 0) & (diff < WINDOW) & (blk >= 0)
            caps.append(jnp.concatenate([jnp.where(ok, -NEG, NEG)] * NSA_HPG, axis=1))
    k_blocks = [kwn_ref[pl.ds(pl.multiple_of(blks[i] * Q_BLOCK, Q_BLOCK), Q_BLOCK), :]
                for i in range(N_BAND + 1)]
    s_w = [[None] * (N_BAND + 1) for g in groups]
    for i in reversed(range(N_BAND + 1)):
        for g in groups:
            s_w[g][i] = _dot_nt(k_blocks[i], qgs[g])
    m_w = [None for g in groups]
    o_win = [None for g in groups]
    for i in reversed(range(N_BAND + 1)):
        for g in groups:
            sm = jnp.minimum(s_w[g][i], caps[i])
            m_blk = jnp.max(sm, axis=0, keepdims=True)
            m_new = m_blk if m_w[g] is None else jnp.maximum(m_w[g], m_blk)
            pv = _dot(vwnt_ref[g, blks[i]], jnp.exp2((sm - m_new).astype(BF16)))
            o_win[g] = pv if m_w[g] is None else jnp.exp2(m_w[g] - m_new) * o_win[g] + pv
            m_w[g] = m_new
    for g in groups:
        owin_s[g] = o_win[g][0:HEAD_DIM, :] * (1.0 / o_win[g][HEAD_DIM:HEAD_DIM + 1, :])


def _nsa_call(qpad, ksl, kwn, vslt, vwnt, kcmp, vcmpt, gt, ot, batch, seq):
    nqb = seq // Q_BLOCK
    n_g = NSA_KV_GROUPS
    qrow = lambda b, c: (b * nqb + c, 0)
    brow = lambda b, c: (b, 0)
    c2 = lambda b, c: (0, 0)
    return pl.pallas_call(
        _nsa_kernel,
        grid=(batch, nqb),
        in_specs=[
            pl.BlockSpec((Q_BLOCK, NSA_HEADS * LANES), qrow),
            pl.BlockSpec((seq, KV_W), brow),
            pl.BlockSpec((seq, KV_W), brow),
            pl.BlockSpec((n_g, seq // SEL_CHUNK, V_ROWS, SEL_CHUNK), lambda b, c: (b, 0, 0, 0)),
            pl.BlockSpec((n_g, nqb, V_ROWS, Q_BLOCK), lambda b, c: (b, 0, 0, 0)),
            pl.BlockSpec((N_A, KV_W), brow),
            pl.BlockSpec((KV_W, N_A), brow),
            pl.BlockSpec((LANES, Q_BLOCK), lambda b, c: (0, b * nqb + c)),
            pl.BlockSpec((N_SEL, N_A), c2),
        ],
        out_specs=pl.BlockSpec((Q_BLOCK, Q_W), qrow),
        out_shape=jax.ShapeDtypeStruct((batch * seq, Q_W), BF16),
        scratch_shapes=[
            pltpu.VMEM((n_g, 1, HQ), F32),
            pltpu.VMEM((n_g, V_ROWS, HQ), F32),
            pltpu.VMEM((n_g, HEAD_DIM, HQ), F32),
            pltpu.VMEM((n_g, HEAD_DIM, HQ), F32),
            pltpu.VMEM((n_g, N_SEL, Q_BLOCK), F32),
            pltpu.VMEM((Q_W, Q_BLOCK), F32),
        ],
        compiler_params=pltpu.CompilerParams(
            dimension_semantics=("parallel", "arbitrary"), vmem_limit_bytes=VMEM_LIMIT),
        name="nsa",
    )(qpad, ksl, kwn, vslt, vwnt, kcmp, vcmpt, gt, ot)


MEM_TM = 1024
MEM_V_ROWS = MEM_HEAD_DIM + 16


def _memattn_kernel(mq_ref, mem_ref, wkv_ref, o_ref, k_s, vt_s, out_s):
    @pl.when(pl.program_id(1) == 0)
    def _():
        kv = _dot(mem_ref[...].astype(BF16), wkv_ref[...].astype(BF16))
        k_s[...] = kv[:, 0:MEM_WIDTH].astype(BF16)
        vt = kv[:, MEM_WIDTH:2 * MEM_WIDTH].T.astype(BF16)
        for h in range(MEM_HEADS):
            vt_s[h, 0:MEM_HEAD_DIM, :] = vt[h * MEM_HEAD_DIM:(h + 1) * MEM_HEAD_DIM, :]
            vt_s[h, MEM_HEAD_DIM:, :] = jnp.ones((MEM_V_ROWS - MEM_HEAD_DIM, vt.shape[1]), BF16)

    heads = range(MEM_HEADS)
    cols = [slice(h * MEM_HEAD_DIM, (h + 1) * MEM_HEAD_DIM) for h in heads]
    s_t = [_dot_nt(k_s[:, cols[h]], mq_ref[:, cols[h]]) for h in heads]
    e = []
    for h in heads:
        m = jnp.max(s_t[h], axis=0, keepdims=True)
        e.append(jnp.exp2((s_t[h] - m) * (MEM_HEAD_DIM ** -0.5 * LOG2E)).astype(BF16))
    o_t = [_dot(vt_s[h], e[h]) for h in heads]
    for h in heads:
        out_s[cols[h], :] = o_t[h][0:MEM_HEAD_DIM, :] * (1.0 / o_t[h][MEM_HEAD_DIM:MEM_HEAD_DIM + 1, :])
    o_ref[...] = out_s[...].T.astype(BF16)


def _memattn_call(mq, mem2, wkv, batch, seq):
    m_len = mem2.shape[0] // batch
    d = mem2.shape[1]
    per = seq // MEM_TM
    return pl.pallas_call(
        _memattn_kernel,
        grid=(batch, per),
        in_specs=[
            pl.BlockSpec((MEM_TM, MEM_WIDTH), lambda b, i: (b * per + i, 0)),
            pl.BlockSpec((m_len, d), lambda b, i: (b, 0)),
            pl.BlockSpec((d, 2 * MEM_WIDTH), lambda b, i: (0, 0)),
        ],
        out_specs=pl.BlockSpec((MEM_TM, MEM_WIDTH), lambda b, i: (b * per + i, 0)),
        out_shape=jax.ShapeDtypeStruct((batch * seq, MEM_WIDTH), BF16),
        scratch_shapes=[
            pltpu.VMEM((m_len, MEM_WIDTH), BF16),
            pltpu.VMEM((MEM_HEADS, MEM_V_ROWS, m_len), BF16),
            pltpu.VMEM((MEM_WIDTH, MEM_TM), F32),
        ],
        compiler_params=pltpu.CompilerParams(
            dimension_semantics=("parallel", "arbitrary"), vmem_limit_bytes=VMEM_LIMIT),
        name="memattn",
    )(mq, mem2, wkv)


MERGE_TM = 512


def _merge_kernel(x_ref, onsa_ref, osgu_ref, omem_ref, wgt_ref, wbn32_ref, wbs32_ref, wbm32_ref, wo32_ref,
                  g_ref, b_ref, out_ref, wbn_ref, wbs_ref, wbm_ref, wo_ref, wg_ref):
    @pl.when(pl.program_id(0) == 0)
    def _():
        for dst, src in ((wbn_ref, wbn32_ref), (wbs_ref, wbs32_ref), (wbm_ref, wbm32_ref),
                         (wo_ref, wo32_ref)):
            dst[...] = src[...].astype(BF16)
        _transpose_weight(wgt_ref, wg_ref)

    d = x_ref.shape[1]
    branches = ((onsa_ref, wbn_ref), (osgu_ref, wbs_ref), (omem_ref, wbm_ref))
    halves = [slice(h * (MERGE_TM // 2), (h + 1) * (MERGE_TM // 2)) for h in range(2)]
    xs = [x_ref[rows, :] for rows in halves]
    xbs = [x.astype(BF16) for x in xs]
    logits = [[_dot(xb, wg_ref[:, br * d:(br + 1) * d]) for br in range(len(branches))] for xb in xbs]
    projs = [[_dot(o_r[rows, :], w_r[...]) for o_r, w_r in branches] for rows in halves]
    ys = []
    for h in range(2):
        y = None
        for br in range(len(branches)):
            term = _sigmoid(logits[h][br]) * projs[h][br]
            y = term if y is None else y + term
        ys.append(y.astype(BF16))
    outs = [_dot(y, wo_ref[...]) for y in ys]
    for h, rows in enumerate(halves):
        out_ref[rows, :] = _layer_norm(DN_ALPHA * xs[h] + outs[h], g_ref[...], b_ref[...])


def _merge_call(x2, onsa, osgu, omem, wg, wbn, wbs, wbm, wo, g1, b1):
    t, d = x2.shape
    row = lambda i: (i, 0)
    c2 = lambda i: (0, 0)
    full = lambda a: pl.BlockSpec(a.shape, c2)
    once = lambda a: pl.BlockSpec(a.shape, c2, pipeline_mode=pl.Buffered(1))
    return pl.pallas_call(
        _merge_kernel,
        grid=(t // MERGE_TM,),
        in_specs=[
            pl.BlockSpec((MERGE_TM, d), row),
            pl.BlockSpec((MERGE_TM, onsa.shape[1]), row),
            pl.BlockSpec((MERGE_TM, osgu.shape[1]), row),
            pl.BlockSpec((MERGE_TM, omem.shape[1]), row),
            pl.BlockSpec((3 * d, d), c2, pipeline_mode=pl.Buffered(1)),
            once(wbn), once(wbs), once(wbm), once(wo), full(g1), full(b1),
        ],
        out_specs=pl.BlockSpec((MERGE_TM, d), row),
        out_shape=jax.ShapeDtypeStruct((t, d), F32),
        scratch_shapes=[pltpu.VMEM(a.shape, BF16) for a in (wbn, wbs, wbm, wo)]
        + [pltpu.VMEM((d, 3 * d), BF16)],
        compiler_params=pltpu.CompilerParams(
            dimension_semantics=("arbitrary",), vmem_limit_bytes=VMEM_LIMIT),
        name="merge",
    )(x2, onsa, osgu, omem, wg, wbn, wbs, wbm, wo, g1, b1)


MOE_TM = 1024
MOE_CH = 144
_YS_ROWS = -(-(MOE_TM + N_GROUPS * MOE_CH) // 256) * 256
_YS_TYPICAL = -(-(2 * N_GROUPS * MOE_CH) // 256) * 256
_R_SLOT = 8
_DEST_LANE = 3 * _R_SLOT
_HID = EXPERTS_PER_GROUP * EXPERT_FF


def _moe_kernel(x_ref, tri_ref, wr_ref, br_ref, wg_ref, wu_ref, wd_ref, g_ref, b_ref, out_ref,
                xa_s, tok_s, keyr_s, ys_s, cnt_s, base_s):
    grp = pl.program_id(1)
    tm = x_ref.shape[0]
    d = x_ref.shape[1]

    @pl.when(grp == 0)
    def _route():
        xb = x_ref[...].astype(BF16)
        xa_s[:, 0:d] = xb
        lt = _dot_nt(wr_ref[...], xb) + br_ref[...]
        row = lax.broadcasted_iota(jnp.int32, (_R_SLOT, tm), 0)
        gl = jnp.where(row < N_GROUPS, lt[0:_R_SLOT], NEG)
        gmax = jnp.max(gl, axis=0, keepdims=True)
        gidx = jnp.min(jnp.where(gl == gmax, row, _R_SLOT), axis=0, keepdims=True)
        gprob = 1.0 / jnp.sum(jnp.exp(gl - gmax), axis=0, keepdims=True)
        el = lt[_R_SLOT:2 * _R_SLOT]
        for k in range(1, N_GROUPS):
            el = jnp.where(gidx == k, lt[(k + 1) * _R_SLOT:(k + 2) * _R_SLOT], el)
        ee = jnp.exp(el - jnp.max(el, axis=0, keepdims=True))
        ep = ee / jnp.sum(ee, axis=0, keepdims=True)
        t1 = jnp.max(ep, axis=0, keepdims=True)
        i1 = jnp.min(jnp.where(ep == t1, row, _R_SLOT), axis=0, keepdims=True)
        rest = row != i1
        t2 = jnp.max(jnp.where(rest, ep, -1.0), axis=0, keepdims=True)
        i2 = jnp.min(jnp.where(rest & (ep == t2), row, _R_SLOT), axis=0, keepdims=True)
        den = t1 + t2
        cwf = (jnp.where(row == i1, t1 / den, 0.0) + jnp.where(row == i2, t2 / den, 0.0)) * gprob
        hi = cwf.astype(BF16).astype(F32)
        r1 = cwf - hi
        lo = r1.astype(BF16).astype(F32)
        lo2 = (r1 - lo).astype(BF16).astype(F32)

        onehot = jnp.where(row == gidx, 1.0, 0.0)
        tri = tri_ref[...]
        onehot_b = onehot.astype(BF16)
        ranks, before = [], jnp.zeros((_R_SLOT, 1), F32)
        for b in range(tm // LANES):
            blk = slice(b * LANES, (b + 1) * LANES)
            ranks.append(_dot_nt(onehot_b[:, blk], tri) + before)
            before = before + jnp.sum(onehot[:, blk], axis=1, keepdims=True)
        rank = jnp.concatenate(ranks, axis=1)
        keyr_s[...] = jnp.where(onehot > 0.5, rank, -1.0)
        base = jnp.int32(0)
        basev = jnp.zeros((1, tm), F32)
        for k in range(N_GROUPS):
            n_k = jnp.sum(onehot[k:k + 1, :]).astype(jnp.int32)
            cnt_s[k] = n_k
            base_s[k] = base
            basev = jnp.where(gidx == k, base.astype(F32), basev)
            base = base + ((n_k + MOE_CH - 1) // MOE_CH) * MOE_CH
        base_s[N_GROUPS] = base
        dest =jnp.sum(rank * onehot, axis=0, keepdims=True) + basev
        tok = jnp.concatenate(
            [hi, lo, lo2, jnp.broadcast_to(dest, (_R_SLOT, tm)),
             jnp.zeros((LANES - 4 * _R_SLOT, tm), F32)], axis=0).T
        tok_s[...] = tok
        lane = lax.broadcasted_iota(jnp.int32, tok.shape, 1)
        xa_s[:, d:d + LANES] = jnp.where(lane < _DEST_LANE, tok, 0.0).astype(BF16)
        ys_s[...] = jnp.zeros(ys_s.shape, BF16)

    n_rows = cnt_s[grp]
    row0 = base_s[grp]
    keyr = keyr_s[pl.ds(grp, 1), :]
    half = _HID // 2

    def sweep(k, carry):
        ch = MOE_CH
        r_row = (lax.broadcasted_iota(jnp.int32, (ch, tm), 0) + k * MOE_CH).astype(F32)
        pick = jnp.where(keyr == r_row, 1.0, 0.0).astype(BF16)
        ga = _dot(pick, xa_s[...])
        xg = ga[:, 0:d].astype(BF16)
        cwg = ga[:, d:d + LANES]
        cs = (cwg + pltpu.roll(cwg, LANES - _R_SLOT, 1)
              + pltpu.roll(cwg, LANES - 2 * _R_SLOT, 1))
        experts = range(EXPERTS_PER_GROUP)
        per_half = EXPERTS_PER_GROUP // 2
        hg = [_dot(xg, wg_ref[0, e]) for e in experts]
        hu = [_dot(xg, wu_ref[0, e]) for e in experts]
        y = None
        for h in range(2):
            hid = jnp.concatenate(
                [(((hg[e] * _sigmoid(hg[e])) * hu[e]) * cs[:, e:e + 1]).astype(BF16)
                 for e in range(h * per_half, (h + 1) * per_half)], axis=1)
            term = _dot(hid, wd_ref[0, h * half:(h + 1) * half, :])
            y = term if y is None else y + term
        ys_s[pl.ds(pl.multiple_of(row0 + k * MOE_CH, 16), ch), :] = y.astype(BF16)
        return carry

    lax.fori_loop(0, (n_rows + MOE_CH - 1) // MOE_CH, sweep, 0)

    def fin(k_rows):
        n_q = 4
        q_rows = tm // n_q
        r_col = lax.broadcasted_iota(jnp.int32, (q_rows, k_rows), 1).astype(F32)
        parts = [slice(i * q_rows, (i + 1) * q_rows) for i in range(n_q)]
        puts = [jnp.where(tok_s[rows, _DEST_LANE:_DEST_LANE + 1] == r_col, 1.0, 0.0).astype(BF16)
                for rows in parts]
        fs = [_dot(put, ys_s[0:k_rows, :]) for put in puts]
        for rows, f in zip(parts, fs):
            out_ref[rows, :] = _layer_norm(DN_ALPHA * x_ref[rows, :] + f, g_ref[...], b_ref[...])

    last = grp == N_GROUPS - 1
    used = base_s[N_GROUPS]
    pl.when(last & (used <= _YS_TYPICAL))(functools.partial(fin, _YS_TYPICAL))
    pl.when(last & (used > _YS_TYPICAL))(functools.partial(fin, _YS_ROWS))


def _moe_call(x1, wr, br, wg, wu, wd, g2, b2):
    t, d = x1.shape
    row = lambda i, g: (i, 0)
    c2 = lambda i, g: (0, 0)
    idx = np.arange(LANES)
    tri = jnp.asarray(idx[None, :] < idx[:, None], dtype=BF16)
    return pl.pallas_call(
        _moe_kernel,
        grid=(t // MOE_TM, N_GROUPS),
        in_specs=[
            pl.BlockSpec((MOE_TM, d), row),
            pl.BlockSpec((LANES, LANES), c2),
            pl.BlockSpec(wr.shape, c2),
            pl.BlockSpec(br.shape, c2),
            pl.BlockSpec((1, EXPERTS_PER_GROUP, d, EXPERT_FF), lambda i, g: (g, 0, 0, 0)),
            pl.BlockSpec((1, EXPERTS_PER_GROUP, d, EXPERT_FF), lambda i, g: (g, 0, 0, 0)),
            pl.BlockSpec((1, _HID, d), lambda i, g: (g, 0, 0)),
            pl.BlockSpec(g2.shape, c2),
            pl.BlockSpec(b2.shape, c2),
        ],
        out_specs=pl.BlockSpec((MOE_TM, d), row),
        out_shape=jax.ShapeDtypeStruct((t, d), F32),
        scratch_shapes=[
            pltpu.VMEM((MOE_TM, d + LANES), BF16),
            pltpu.VMEM((MOE_TM, LANES), F32),
            pltpu.VMEM((_R_SLOT, MOE_TM), F32),
            pltpu.VMEM((_YS_ROWS, d), BF16),
            pltpu.SMEM((N_GROUPS,), jnp.int32),
            pltpu.SMEM((N_GROUPS + 1,), jnp.int32),
        ],
        compiler_params=pltpu.CompilerParams(
            dimension_semantics=("parallel", "arbitrary"), vmem_limit_bytes=MOE_VMEM_LIMIT),
        name="moe",
    )(x1, tri, wr, br, wg, wu, wd, g2, b2)


def _rope_table(seq):
    half = ROPE_DIM // 2
    inv = ROPE_THETA ** (-jnp.arange(0, ROPE_DIM, 2, dtype=F32) / ROPE_DIM)
    ang = jnp.arange(seq, dtype=F32)[:, None] * inv[None, :]
    cos, sin = jnp.cos(ang), jnp.sin(ang)
    rest = HEAD_DIM - ROPE_DIM
    one = jnp.ones((seq, rest), F32)
    zero = jnp.zeros((seq, rest), F32)
    zh = jnp.zeros((seq, half), F32)
    c_h = jnp.concatenate([cos, cos, one], axis=1)
    dn_h = jnp.concatenate([-sin, zh, zero], axis=1)
    up_h = jnp.concatenate([zh, sin, zero], axis=1)
    rep = LANES // HEAD_DIM
    return jnp.concatenate([jnp.tile(c_h, (1, rep)), jnp.tile(dn_h, (1, rep)), jnp.tile(up_h, (1, rep))],
                           axis=1)


def _expand_cmp_weights(pe, w1, w2):
    g_n = NSA_KV_GROUPS
    w1r = w1.astype(BF16).reshape(2, CMP_STRIDE, HEAD_DIM, CMP_HIDDEN)
    w2b = w2.astype(BF16)
    w1e = jnp.concatenate(
        [jnp.stack([w1r if g == e else jnp.zeros_like(w1r) for g in range(g_n)], axis=2)
         .reshape(2, A_W, CMP_HIDDEN) for e in range(g_n)], axis=2)
    w2e = jnp.concatenate(
        [jnp.concatenate([w2b if g == e else jnp.zeros_like(w2b) for e in range(g_n)], axis=1)
         for g in range(g_n)], axis=0)
    per = pe.reshape(2, CMP_STRIDE, 1, HEAD_DIM)
    pee = jnp.broadcast_to(per, (2, CMP_STRIDE, g_n, HEAD_DIM)).reshape(2, A_W)
    pee = jnp.concatenate([pee, jnp.zeros((6, A_W), pe.dtype)], axis=0)
    return pee.astype(F32), w1e.astype(BF16), w2e.astype(BF16)


def kernel(x, mem, w_in, cmp_pe_k, cmp_w1_k, cmp_w2_k, cmp_pe_v, cmp_w1_v, cmp_w2_v, sgu_ln_g, sgu_ln_b,
           sgu_w_s, sgu_b_s, w_mem_kv, w_br_nsa, w_br_sgu, w_br_mem, w_o, ln1_g, ln1_b, w_router_group,
           b_router_group, w_router_expert, b_router_expert, w_exp_gate, w_exp_up, w_exp_down, ln2_g, ln2_b):
    batch, seq, d = x.shape
    t = batch * seq
    assert w_in.shape[0] == DEPTH == 1
    assert seq % PROJ_TM == 0 and seq // CMP_STRIDE == N_A and seq // SLC_BLOCK == N_SEL

    offs = [int(v) for v in np.cumsum(
        [0, Q_W, KV_W, KV_W, KV_W, KV_W, KV_W, KV_W, GATE_W, 2 * SGU_WIDTH, MEM_WIDTH, 3 * d])]
    assert offs[-1] == w_in.shape[2]
    w_t = jnp.transpose(w_in[0])
    seg = lambda i: w_t[offs[i]:offs[i + 1]]
    gate_rows = jnp.pad(seg(7), ((0, LANES - GATE_W), (0, 0)))
    wp_t = jnp.concatenate([seg(0), seg(1), seg(3), seg(5), seg(2), seg(4), seg(6), gate_rows, seg(8), seg(9)],
                           axis=0).astype(BF16)
    w_mt = seg(10).astype(BF16)
    assert wp_t.shape[0] == _PROJ_COLS
    rope = _rope_table(seq)
    tril = jnp.tril(jnp.ones((SGU_CHUNK, SGU_CHUNK), dtype=bool))
    ws = jnp.where(tril[None], sgu_w_s[0], 0.0).astype(BF16)
    bs = jnp.repeat(sgu_b_s[0].T, SGU_WIDTH // SGU_GROUPS, axis=1)

    x2 = x.reshape(t, d)
    (qpad, kc, ksl, kwn, vc, vslt, vwnt, gt, osgu, mq) = _proj_call(
        x2, wp_t, rope, sgu_ln_g[0][None], sgu_ln_b[0][None], ws, bs, seq)

    pek, w1k, w2k = _expand_cmp_weights(cmp_pe_k[0], cmp_w1_k[0], cmp_w2_k[0])
    pev, w1v, w2v = _expand_cmp_weights(cmp_pe_v[0], cmp_w1_v[0], cmp_w2_v[0])
    kcmp, vcmp = _compress_call(kc, vc, pek, pev, w1k, w2k, w1v, w2v, batch)

    vcmpt = vcmp.reshape(batch, N_A, KV_W).transpose(0, 2, 1).reshape(batch * KV_W, N_A)
    ci = np.arange(N_A)
    sj = np.arange(N_SEL)
    overlap = ((ci[None, :] * CMP_STRIDE + CMP_LEN - 1 >= sj[:, None] * SLC_BLOCK)
               & (ci[None, :] * CMP_STRIDE <= sj[:, None] * SLC_BLOCK + SLC_BLOCK - 1)
               & (ci[None, :] < (seq - CMP_LEN) // CMP_STRIDE + 1))
    ot = jnp.asarray(overlap, dtype=BF16)
    onsa = _nsa_call(qpad, ksl, kwn, vslt, vwnt, kcmp, vcmpt, gt, ot, batch, seq)

    omem = _memattn_call(mq, mem.reshape(batch * mem.shape[1], d), w_mem_kv[0], batch, seq)

    x1 = _merge_call(x2, onsa, osgu, omem, w_mt, w_br_nsa[0], w_br_sgu[0], w_br_mem[0], w_o[0],
                     ln1_g[0][None], ln1_b[0][None])

    assert EXPERTS_PER_GROUP == _R_SLOT and N_GROUPS <= _R_SLOT
    n_r = _R_SLOT + N_GROUPS * EXPERTS_PER_GROUP
    wr = jnp.concatenate([jnp.pad(w_router_group[0], ((0, 0), (0, _R_SLOT - N_GROUPS))),
                          w_router_expert[0]], axis=1)
    wr = jnp.pad(wr, ((0, 0), (0, LANES - n_r))).astype(BF16).T
    br = jnp.concatenate([jnp.pad(b_router_group[0], (0, _R_SLOT - N_GROUPS)), b_router_expert[0]])
    br = jnp.pad(br, (0, LANES - n_r))[:, None]
    wg = w_exp_gate[0].astype(BF16)
    wu = w_exp_up[0].astype(BF16)
    wd = w_exp_down[0].reshape(N_GROUPS, _HID, d).astype(BF16)
    out = _moe_call(x1, wr, br, wg, wu, wd, ln2_g[0][None], ln2_b[0][None])
    return out.reshape(batch, seq, d)
```

```python
import functools

import numpy as np
import jax
import jax.numpy as jnp
from jax import lax
from jax.experimental import pallas as pl
from jax.experimental.pallas import tpu as pltpu

NSA_HEADS = 8
NSA_KV_GROUPS = 2
NSA_HPG = NSA_HEADS // NSA_KV_GROUPS
HEAD_DIM = 64
CMP_LEN = 32
CMP_STRIDE = 16
CMP_HIDDEN = 256
SLC_BLOCK = 64
SLC_TOPN = 8
WINDOW = 512
Q_BLOCK = 128
N_BAND = WINDOW // Q_BLOCK
ROPE_THETA = 500000.0
ROPE_DIM = HEAD_DIM // 4
SGU_CHUNK = 128
SGU_GROUPS = 8
SGU_WIDTH = 512
MEM_HEADS = 4
MEM_HEAD_DIM = 128
MEM_WIDTH = MEM_HEADS * MEM_HEAD_DIM
N_GROUPS = 4
EXPERTS_PER_GROUP = 8
EXPERT_FF = 256
DEPTH = 1
DN_ALPHA = (2.0 * DEPTH) ** 0.25
LN_EPS = 1e-5
NEG = -1e30
LOG2E = 1.4426950408889634

LANES = 128
Q_W = NSA_HEADS * HEAD_DIM
KV_W = NSA_KV_GROUPS * HEAD_DIM
GATE_W = NSA_HEADS * 3
VMEM_LIMIT = 56 * 1024 * 1024
MOE_VMEM_LIMIT = 60 * 1024 * 1024

BF16 = jnp.bfloat16
F32 = jnp.float32


def _dot(a, b):
    return jnp.dot(a, b, preferred_element_type=F32)


def _dot_nt(a, b):
    return lax.dot_general(a, b, (((1,), (1,)), ((), ())), preferred_element_type=F32)


def _sigmoid(x):
    return 1.0 / (1.0 + jnp.exp(-x))


def _gelu(x):
    return 0.5 * x * (1.0 + lax.erf(x * (2.0 ** -0.5)))


def _layer_norm(x, g, b):
    mu = jnp.mean(x, axis=-1, keepdims=True)
    xc = x - mu
    var = jnp.mean(xc * xc, axis=-1, keepdims=True)
    return xc * lax.rsqrt(var + LN_EPS) * g + b


PROJ_TM = 512
_ROPE_COLS = Q_W + 3 * KV_W
_V_OFF = _ROPE_COLS
_G_OFF = _V_OFF + 3 * KV_W
_SGU_OFF = _G_OFF + LANES
_MQ_OFF = _SGU_OFF + 2 * SGU_WIDTH
_PROJ_COLS = _MQ_OFF + MEM_WIDTH


def _store_strided_rows(val, out_ref, tmp_ref):
    tmp_ref[...] = val
    n = val.shape[0] // CMP_STRIDE
    for l in range(CMP_STRIDE):
        out_ref[:, l * LANES:(l + 1) * LANES] = tmp_ref[pl.ds(l, n, stride=CMP_STRIDE), :].astype(BF16)


def _transpose_weight(wt_ref, w_s):
    for j in range(wt_ref.shape[0] // LANES):
        rows = slice(j * LANES, (j + 1) * LANES)
        w_s[:, rows] = wt_ref[rows, :].astype(F32).T.astype(BF16)


def _proj_kernel(x_ref, wt_ref, rope_ref, lng_ref, lnb_ref, ws_ref, bs_ref,
                 q_ref, kc_ref, ksl_ref, kwn_ref, vc_ref, vsl_ref, vwn_ref, gate_ref, osgu_ref, mq_ref,
                 a_s, w_ref):
    pl.when(pl.program_id(0) == 0)(functools.partial(_transpose_weight, wt_ref, w_ref))
    xb = x_ref[...].astype(BF16)
    lane = lax.broadcasted_iota(jnp.int32, (PROJ_TM, LANES), 1)
    low = lane < HEAD_DIM
    cos = rope_ref[:, 0:LANES]
    s_dn = rope_ref[:, LANES:2 * LANES]
    s_up = rope_ref[:, 2 * LANES:3 * LANES]

    hz = _dot(xb, w_ref[:, _SGU_OFF:_MQ_OFF])
    h = _dot(xb, w_ref[:, 0:_ROPE_COLS])
    k_refs = (kc_ref, ksl_ref, kwn_ref)
    for j in range(_ROPE_COLS // LANES):
        blk = h[:, j * LANES:(j + 1) * LANES]
        r = (blk * cos + pltpu.roll(blk, LANES - ROPE_DIM // 2, 1) * s_dn
             + pltpu.roll(blk, ROPE_DIM // 2, 1) * s_up)
        if j < Q_W // LANES:
            r = r * (HEAD_DIM ** -0.5 * LOG2E)
            sw = pltpu.roll(r, HEAD_DIM, 1)
            g = (2 * j) // NSA_HPG
            if g == 0:
                h0 = jnp.where(low, r, 0.0)
                h1 = jnp.where(low, sw, 0.0)
            else:
                h0 = jnp.where(low, 0.0, sw)
                h1 = jnp.where(low, 0.0, r)
            q_ref[:, (2 * j) * LANES:(2 * j + 1) * LANES] = h0.astype(BF16)
            q_ref[:, (2 * j + 1) * LANES:(2 * j + 2) * LANES] = h1.astype(BF16)
        else:
            if j == Q_W // LANES:
                _store_strided_rows(r, kc_ref, a_s)
            else:
                k_refs[j - Q_W // LANES][...] = r.astype(BF16)

    _store_strided_rows(_dot(xb, w_ref[:, _V_OFF:_V_OFF + LANES]), vc_ref, a_s)
    ones = jnp.ones((V_ROWS - HEAD_DIM, PROJ_TM), BF16)
    for v_ref, row0, blk in ((vsl_ref, _V_OFF + LANES, SEL_CHUNK), (vwn_ref, _V_OFF + 2 * LANES, Q_BLOCK)):
        vt = _dot_nt(wt_ref[row0:row0 + LANES, :], xb).astype(BF16)
        for g in range(NSA_KV_GROUPS):
            for b in range(PROJ_TM // blk):
                v_ref[g, b, 0:HEAD_DIM, :] = vt[g * HEAD_DIM:(g + 1) * HEAD_DIM, b * blk:(b + 1) * blk]
                v_ref[g, b, HEAD_DIM:V_ROWS, :] = ones[:, 0:blk]

    gate_ref[...] = _sigmoid(_dot_nt(wt_ref[_G_OFF:_SGU_OFF, :], xb))

    mq_ref[...] = _dot(xb, w_ref[:, _MQ_OFF:_PROJ_COLS]).astype(BF16)

    z = _gelu(hz)
    u = z[:, 0:SGU_WIDTH]
    v = _layer_norm(z[:, SGU_WIDTH:2 * SGU_WIDTH], lng_ref[...], lnb_ref[...]).astype(BF16)
    lane_c = lax.broadcasted_iota(jnp.int32, (SGU_CHUNK, LANES), 1)
    low_c = lane_c < (SGU_WIDTH // SGU_GROUPS)
    n_ch = PROJ_TM // SGU_CHUNK
    for gp in range(SGU_WIDTH // LANES):
        cols = slice(gp * LANES, (gp + 1) * LANES)
        vcat = jnp.concatenate([v[ci * SGU_CHUNK:(ci + 1) * SGU_CHUNK, cols] for ci in range(n_ch)], axis=1)
        sv0 = _dot(ws_ref[2 * gp], vcat)
        sv1 = _dot(ws_ref[2 * gp + 1], vcat)
        for ci in range(n_ch):
            rows = slice(ci * SGU_CHUNK, (ci + 1) * SGU_CHUNK)
            lanes = slice(ci * LANES, (ci + 1) * LANES)
            sv = jnp.where(low_c, sv0[:, lanes], sv1[:, lanes]) + bs_ref[:, cols]
            osgu_ref[rows, cols] = (u[rows, cols] * sv).astype(BF16)


def _proj_call(x2, wp, rope, lng, lnb, ws, bs, seq):
    t = x2.shape[0]
    d = x2.shape[1]
    nt = t // PROJ_TM
    per_seq = seq // PROJ_TM
    row = lambda i: (i, 0)
    const2 = lambda i: (0, 0)
    assert PROJ_TM % SEL_CHUNK == 0 and PROJ_TM % Q_BLOCK == 0
    n_g = NSA_KV_GROUPS
    batch = t // seq
    vblock = lambda i: (i // per_seq, i % per_seq, 0, 0)

    def v_shape(blk):
        return jax.ShapeDtypeStruct((batch * n_g, seq // blk, V_ROWS, blk), BF16)

    def v_spec(blk):
        return pl.BlockSpec((n_g, PROJ_TM // blk, V_ROWS, blk), vblock)

    a_shape = jax.ShapeDtypeStruct((t // CMP_STRIDE, CMP_STRIDE * KV_W), BF16)
    a_spec = pl.BlockSpec((PROJ_TM // CMP_STRIDE, CMP_STRIDE * KV_W), row)
    k_shape = jax.ShapeDtypeStruct((t, KV_W), BF16)
    k_spec = pl.BlockSpec((PROJ_TM, KV_W), row)
    out_shapes = (
        jax.ShapeDtypeStruct((t, NSA_HEADS * LANES), BF16),
        a_shape, k_shape, k_shape, a_shape,
        v_shape(SEL_CHUNK), v_shape(Q_BLOCK),
        jax.ShapeDtypeStruct((LANES, t), F32),
        jax.ShapeDtypeStruct((t, SGU_WIDTH), BF16),
        jax.ShapeDtypeStruct((t, MEM_WIDTH), BF16),
    )
    out_specs = (
        pl.BlockSpec((PROJ_TM, NSA_HEADS * LANES), row),
        a_spec, k_spec, k_spec, a_spec,
        v_spec(SEL_CHUNK), v_spec(Q_BLOCK),
        pl.BlockSpec((LANES, PROJ_TM), lambda i: (0, i)),
        pl.BlockSpec((PROJ_TM, SGU_WIDTH), row),
        pl.BlockSpec((PROJ_TM, MEM_WIDTH), row),
    )
    return pl.pallas_call(
        _proj_kernel,
        grid=(nt,),
        in_specs=[
            pl.BlockSpec((PROJ_TM, d), row),
            pl.BlockSpec((_PROJ_COLS, d), const2),
            pl.BlockSpec((PROJ_TM, 3 * LANES), lambda i: (i % per_seq, 0)),
            pl.BlockSpec((1, SGU_WIDTH), const2),
            pl.BlockSpec((1, SGU_WIDTH), const2),
            pl.BlockSpec((SGU_GROUPS, SGU_CHUNK, SGU_CHUNK), lambda i: (0, 0, 0)),
            pl.BlockSpec((SGU_CHUNK, SGU_WIDTH), const2),
        ],
        out_specs=out_specs,
        out_shape=out_shapes,
        scratch_shapes=[pltpu.VMEM((PROJ_TM, KV_W), F32), pltpu.VMEM((d, _PROJ_COLS), BF16)],
        compiler_params=pltpu.CompilerParams(
            dimension_semantics=("arbitrary",), vmem_limit_bytes=VMEM_LIMIT),
        name="proj",
    )(x2, wp, rope, lng, lnb, ws, bs)


N_A = 128
A_W = CMP_STRIDE * KV_W


def _compress_kernel(ka_ref, va_ref, pek_ref, pev_ref, w1k_ref, w2k_ref, w1v_ref, w2v_ref,
                     kcmp_ref, vcmp_ref):
    def one(a_ref, pe_ref, w1_ref, w2_ref, out_ref):
        a = a_ref[...].astype(F32)
        top = (a + pe_ref[0:1, :]).astype(BF16)
        bot = (a + pe_ref[1:2, :]).astype(BF16)
        h1 = _dot(top, w1_ref[0])
        h2 = _dot(bot, w1_ref[1])
        pre = h1 + pltpu.roll(h2, N_A - 1, 0)
        act = _gelu(pre).astype(BF16)
        out_ref[...] = _dot(act, w2_ref[...]).astype(BF16)

    one(ka_ref, pek_ref, w1k_ref, w2k_ref, kcmp_ref)
    one(va_ref, pev_ref, w1v_ref, w2v_ref, vcmp_ref)


def _compress_call(ka, va, pek, pev, w1k, w2k, w1v, w2v, batch):
    row = lambda b: (b, 0)
    c2 = lambda b: (0, 0)
    c3 = lambda b: (0, 0, 0)
    hid2 = NSA_KV_GROUPS * CMP_HIDDEN
    return pl.pallas_call(
        _compress_kernel,
        grid=(batch,),
        in_specs=[
            pl.BlockSpec((N_A, A_W), row),
            pl.BlockSpec((N_A, A_W), row),
            pl.BlockSpec((8, A_W), c2),
            pl.BlockSpec((8, A_W), c2),
            pl.BlockSpec((2, A_W, hid2), c3),
            pl.BlockSpec((hid2, KV_W), c2),
            pl.BlockSpec((2, A_W, hid2), c3),
            pl.BlockSpec((hid2, KV_W), c2),
        ],
        out_specs=(pl.BlockSpec((N_A, KV_W), row), pl.BlockSpec((N_A, KV_W), row)),
        out_shape=(jax.ShapeDtypeStruct((batch * N_A, KV_W), BF16),
                   jax.ShapeDtypeStruct((batch * N_A, KV_W), BF16)),
        compiler_params=pltpu.CompilerParams(
            dimension_semantics=("parallel",), vmem_limit_bytes=VMEM_LIMIT),
        name="compress",
    )(ka, va, pek, pev, w1k, w2k, w1v, w2v)


N_SEL = 32
SEL_CHUNK = 256
SEL_SUB = 128
HQ = NSA_HPG * Q_BLOCK
V_ROWS = HEAD_DIM + 16


def _nsa_kernel(*refs):
    ksl_ref = refs[1]
    reach = pl.program_id(1) // (SEL_CHUNK // Q_BLOCK)
    for n in range(ksl_ref.shape[0] // SEL_CHUNK):
        pl.when(reach == n)(functools.partial(_nsa_body, n, *refs))


def _nsa_body(n_later, q_ref, ksl_ref, kwn_ref, vslt_ref, vwnt_ref, kcmp_ref, vcmpt_ref, gt_ref,
              ot_ref, o_ref, m_s, acc_s, ocmp_s, owin_s, cap_s, out_s):
    c = pl.program_id(1)
    lane_hq = lax.broadcasted_iota(jnp.int32, (1, HQ), 1)
    pos_hq = c * Q_BLOCK + (lane_hq & (Q_BLOCK - 1))
    pos_q = c * Q_BLOCK + lax.broadcasted_iota(jnp.int32, (1, Q_BLOCK), 1)
    groups = range(NSA_KV_GROUPS)
    qgs = [jnp.concatenate(
        [q_ref[:, (g * NSA_HPG + hh) * LANES:(g * NSA_HPG + hh + 1) * LANES]
         for hh in range(NSA_HPG)], axis=0) for g in groups]

    kcmp = kcmp_ref[...]
    s_c = [_dot_nt(kcmp, qgs[g]) for g in groups]
    n_idx = lax.broadcasted_iota(jnp.int32, (N_A, HQ), 0)
    valid_c = (n_idx * CMP_STRIDE + (CMP_LEN - 1)) <= pos_hq
    p_c = []
    for g in groups:
        sm_c = jnp.where(valid_c, s_c[g], NEG)
        m_c = jnp.max(sm_c, axis=0, keepdims=True)
        e_c = jnp.where(valid_c, jnp.exp2(sm_c - m_c), 0.0)
        d_c = jnp.sum(e_c, axis=0, keepdims=True)
        p_c.append(e_c / jnp.where(d_c > 0, d_c, 1.0))
    for g in groups:
        ocmp_s[g] = _dot(vcmpt_ref[g * HEAD_DIM:(g + 1) * HEAD_DIM, :], p_c[g].astype(BF16))

    ot = ot_ref[...]
    imp = []
    for g in groups:
        ps = (p_c[g][:, 0:Q_BLOCK] + p_c[g][:, Q_BLOCK:2 * Q_BLOCK]
              + p_c[g][:, 2 * Q_BLOCK:3 * Q_BLOCK] + p_c[g][:, 3 * Q_BLOCK:4 * Q_BLOCK])
        p_hi = ps.astype(BF16)
        r1 = ps - p_hi.astype(F32)
        p_lo = r1.astype(BF16)
        p_lo2 = (r1 - p_lo.astype(F32)).astype(BF16)
        imp.append(_dot(ot, p_hi) + _dot(ot, p_lo) + _dot(ot, p_lo2))

    j_idx = lax.broadcasted_iota(jnp.int32, (N_SEL, Q_BLOCK), 0)
    cur = pos_q // SLC_BLOCK
    future = j_idx > cur
    forced = (j_idx == 0) | (j_idx == cur) | (j_idx == cur - 1)
    imp = [jnp.where(future, NEG, jnp.where(forced, -NEG, imp[g])) for g in groups]
    rank = [jnp.zeros((N_SEL, Q_BLOCK), F32) for g in groups]
    for i in range(N_SEL):
        for g in groups:
            row = imp[g][i:i + 1, :]
            beats = (row > imp[g]) | ((row == imp[g]) & (j_idx > i))
            rank[g] = rank[g] + jnp.where(beats, 1.0, 0.0)
    for g in groups:
        cap_s[g] = jnp.where(rank[g] < float(SLC_TOPN), -NEG, NEG)
        m_s[g] = jnp.full((1, HQ), NEG, F32)
        acc_s[g] = jnp.zeros((V_ROWS, HQ), F32)

    blocks_per_sub = SEL_SUB // SLC_BLOCK
    subs_per_chunk = SEL_CHUNK // SEL_SUB

    def sel_spans(kc):
        return [slice(kc * SEL_CHUNK + a * SEL_SUB, kc * SEL_CHUNK + (a + 1) * SEL_SUB)
                for a in range(subs_per_chunk)]

    def sel_scores(kc):
        return [[_dot_nt(ksl_ref[ks, :], qgs[g]) for g in groups] for ks in sel_spans(kc)]

    def sel_chunk(kc, s=None):
        spans = sel_spans(kc)
        s = sel_scores(kc) if s is None else s
        for a, ks in enumerate(spans):
            kpos = ks.start + lax.broadcasted_iota(jnp.int32, (SEL_SUB, Q_BLOCK), 0)
            causal = kpos <= pos_q
            for g in groups:
                j0 = ks.start // SLC_BLOCK
                cap = jnp.concatenate(
                    [jnp.broadcast_to(cap_s[g, j:j + 1, :], (SLC_BLOCK, Q_BLOCK))
                     for j in range(j0, j0 + blocks_per_sub)], axis=0)
                cap = jnp.where(causal, cap, NEG)
                sm = jnp.minimum(s[a][g], jnp.concatenate([cap] * NSA_HPG, axis=1))
                m_old = m_s[g]
                m_new = jnp.maximum(m_old, jnp.max(sm, axis=0, keepdims=True))
                alpha = jnp.exp2(m_old - m_new)
                e = jnp.exp2((sm - m_new).astype(BF16))
                v_t = vslt_ref[g, kc, :, a * SEL_SUB:(a + 1) * SEL_SUB]
                acc_s[g] = alpha * acc_s[g] + _dot(v_t, e)
                m_s[g] = m_new

    s0 = sel_scores(0)
    _nsa_window(c, qgs, kwn_ref, vwnt_ref, owin_s)
    s_next = sel_scores(1) if n_later else None
    sel_chunk(0, s0)
    for kc in range(1, n_later + 1):
        s_cur = s_next
        if kc < n_later:
            s_next = sel_scores(kc + 1)
        sel_chunk(kc, s_cur)

    for g in groups:
        acc = acc_s[g]
        o_sel = acc[0:HEAD_DIM, :] * (1.0 / acc[HEAD_DIM:HEAD_DIM + 1, :])

        def gate_row(br):
            return jnp.concatenate(
                [gt_ref[(g * NSA_HPG + hh) * 3 + br:(g * NSA_HPG + hh) * 3 + br + 1, :]
                 for hh in range(NSA_HPG)], axis=1)
        o_t = gate_row(0) * ocmp_s[g] + gate_row(1) * o_sel + gate_row(2) * owin_s[g]
        for hh in range(NSA_HPG):
            h = g * NSA_HPG + hh
            out_s[h * HEAD_DIM:(h + 1) * HEAD_DIM, :] = o_t[:, hh * Q_BLOCK:(hh + 1) * Q_BLOCK]

    o_ref[...] = out_s[...].T.astype(BF16)


def _nsa_window(c, qgs, kwn_ref, vwnt_ref, owin_s):
    groups = range(NSA_KV_GROUPS)
    q_i = lax.broadcasted_iota(jnp.int32, (Q_BLOCK, Q_BLOCK), 1)
    k_i = lax.broadcasted_iota(jnp.int32, (Q_BLOCK, Q_BLOCK), 0)
    blks, caps = [], []
    for i in range(N_BAND + 1):
        blk = c - N_BAND + i
        blks.append(jnp.maximum(blk, 0))
        off = (N_BAND - i) * Q_BLOCK
        if off - (Q_BLOCK - 1) >= 0 and off + (Q_BLOCK - 1) < WINDOW:
            caps.append(jnp.where(blk >= 0, -NEG, NEG))
        else:
            diff = off + q_i - k_i
            ok = (diff >= 0) & (diff < WINDOW) & (blk >= 0)
            caps.append(jnp.concatenate([jnp.where(ok, -NEG, NEG)] * NSA_HPG, axis=1))
    k_blocks = [kwn_ref[pl.ds(pl.multiple_of(blks[i] * Q_BLOCK, Q_BLOCK), Q_BLOCK), :]
                for i in range(N_BAND + 1)]
    s_w = [[None] * (N_BAND + 1) for g in groups]
    for i in reversed(range(N_BAND + 1)):
        for g in groups:
            s_w[g][i] = _dot_nt(k_blocks[i], qgs[g])
    m_w = [None for g in groups]
    o_win = [None for g in groups]
    for i in reversed(range(N_BAND + 1)):
        for g in groups:
            sm = jnp.minimum(s_w[g][i], caps[i])
            m_blk = jnp.max(sm, axis=0, keepdims=True)
            m_new = m_blk if m_w[g] is None else jnp.maximum(m_w[g], m_blk)
            pv = _dot(vwnt_ref[g, blks[i]], jnp.exp2((sm - m_new).astype(BF16)))
            o_win[g] = pv if m_w[g] is None else jnp.exp2(m_w[g] - m_new) * o_win[g] + pv
            m_w[g] = m_new
    for g in groups:
        owin_s[g] = o_win[g][0:HEAD_DIM, :] * (1.0 / o_win[g][HEAD_DIM:HEAD_DIM + 1, :])


def _nsa_call(qpad, ksl, kwn, vslt, vwnt, kcmp, vcmpt, gt, ot, batch, seq):
    nqb = seq // Q_BLOCK
    n_g = NSA_KV_GROUPS
    qrow = lambda b, c: (b * nqb + c, 0)
    brow = lambda b, c: (b, 0)
    c2 = lambda b, c: (0, 0)
    return pl.pallas_call(
        _nsa_kernel,
        grid=(batch, nqb),
        in_specs=[
            pl.BlockSpec((Q_BLOCK, NSA_HEADS * LANES), qrow),
            pl.BlockSpec((seq, KV_W), brow),
            pl.BlockSpec((seq, KV_W), brow),
            pl.BlockSpec((n_g, seq // SEL_CHUNK, V_ROWS, SEL_CHUNK), lambda b, c: (b, 0, 0, 0)),
            pl.BlockSpec((n_g, nqb, V_ROWS, Q_BLOCK), lambda b, c: (b, 0, 0, 0)),
            pl.BlockSpec((N_A, KV_W), brow),
            pl.BlockSpec((KV_W, N_A), brow),
            pl.BlockSpec((LANES, Q_BLOCK), lambda b, c: (0, b * nqb + c)),
            pl.BlockSpec((N_SEL, N_A), c2),
        ],
        out_specs=pl.BlockSpec((Q_BLOCK, Q_W), qrow),
        out_shape=jax.ShapeDtypeStruct((batch * seq, Q_W), BF16),
        scratch_shapes=[
            pltpu.VMEM((n_g, 1, HQ), F32),
            pltpu.VMEM((n_g, V_ROWS, HQ), F32),
            pltpu.VMEM((n_g, HEAD_DIM, HQ), F32),
            pltpu.VMEM((n_g, HEAD_DIM, HQ), F32),
            pltpu.VMEM((n_g, N_SEL, Q_BLOCK), F32),
            pltpu.VMEM((Q_W, Q_BLOCK), F32),
        ],
        compiler_params=pltpu.CompilerParams(
            dimension_semantics=("parallel", "arbitrary"), vmem_limit_bytes=VMEM_LIMIT),
        name="nsa",
    )(qpad, ksl, kwn, vslt, vwnt, kcmp, vcmpt, gt, ot)


MEM_TM = 1024
MEM_V_ROWS = MEM_HEAD_DIM + 16


def _memattn_kernel(mq_ref, mem_ref, wkv_ref, o_ref, k_s, vt_s, out_s):
    @pl.when(pl.program_id(1) == 0)
    def _():
        kv = _dot(mem_ref[...].astype(BF16), wkv_ref[...].astype(BF16))
        k_s[...] = kv[:, 0:MEM_WIDTH].astype(BF16)
        vt = kv[:, MEM_WIDTH:2 * MEM_WIDTH].T.astype(BF16)
        for h in range(MEM_HEADS):
            vt_s[h, 0:MEM_HEAD_DIM, :] = vt[h * MEM_HEAD_DIM:(h + 1) * MEM_HEAD_DIM, :]
            vt_s[h, MEM_HEAD_DIM:, :] = jnp.ones((MEM_V_ROWS - MEM_HEAD_DIM, vt.shape[1]), BF16)

    heads = range(MEM_HEADS)
    cols = [slice(h * MEM_HEAD_DIM, (h + 1) * MEM_HEAD_DIM) for h in heads]
    s_t = [_dot_nt(k_s[:, cols[h]], mq_ref[:, cols[h]]) for h in heads]
    e = []
    for h in heads:
        m = jnp.max(s_t[h], axis=0, keepdims=True)
        e.append(jnp.exp2((s_t[h] - m) * (MEM_HEAD_DIM ** -0.5 * LOG2E)).astype(BF16))
    o_t = [_dot(vt_s[h], e[h]) for h in heads]
    for h in heads:
        out_s[cols[h], :] = o_t[h][0:MEM_HEAD_DIM, :] * (1.0 / o_t[h][MEM_HEAD_DIM:MEM_HEAD_DIM + 1, :])
    o_ref[...] = out_s[...].T.astype(BF16)


def _memattn_call(mq, mem2, wkv, batch, seq):
    m_len = mem2.shape[0] // batch
    d = mem2.shape[1]
    per = seq // MEM_TM
    return pl.pallas_call(
        _memattn_kernel,
        grid=(batch, per),
        in_specs=[
            pl.BlockSpec((MEM_TM, MEM_WIDTH), lambda b, i: (b * per + i, 0)),
            pl.BlockSpec((m_len, d), lambda b, i: (b, 0)),
            pl.BlockSpec((d, 2 * MEM_WIDTH), lambda b, i: (0, 0)),
        ],
        out_specs=pl.BlockSpec((MEM_TM, MEM_WIDTH), lambda b, i: (b * per + i, 0)),
        out_shape=jax.ShapeDtypeStruct((batch * seq, MEM_WIDTH), BF16),
        scratch_shapes=[
            pltpu.VMEM((m_len, MEM_WIDTH), BF16),
            pltpu.VMEM((MEM_HEADS, MEM_V_ROWS, m_len), BF16),
            pltpu.VMEM((MEM_WIDTH, MEM_TM), F32),
        ],
        compiler_params=pltpu.CompilerParams(
            dimension_semantics=("parallel", "arbitrary"), vmem_limit_bytes=VMEM_LIMIT),
        name="memattn",
    )(mq, mem2, wkv)


MERGE_TM = 512


def _merge_kernel(x_ref, onsa_ref, osgu_ref, omem_ref, wgt_ref, wbn32_ref, wbs32_ref, wbm32_ref, wo32_ref,
                  g_ref, b_ref, out_ref, wbn_ref, wbs_ref, wbm_ref, wo_ref, wg_ref):
    @pl.when(pl.program_id(0) == 0)
    def _():
        for dst, src in ((wbn_ref, wbn32_ref), (wbs_ref, wbs32_ref), (wbm_ref, wbm32_ref),
                         (wo_ref, wo32_ref)):
            dst[...] = src[...].astype(BF16)
        _transpose_weight(wgt_ref, wg_ref)

    d = x_ref.shape[1]
    branches = ((onsa_ref, wbn_ref), (osgu_ref, wbs_ref), (omem_ref, wbm_ref))
    halves = [slice(h * (MERGE_TM // 2), (h + 1) * (MERGE_TM // 2)) for h in range(2)]
    xs = [x_ref[rows, :] for rows in halves]
    xbs = [x.astype(BF16) for x in xs]
    logits = [[_dot(xb, wg_ref[:, br * d:(br + 1) * d]) for br in range(len(branches))] for xb in xbs]
    projs = [[_dot(o_r[rows, :], w_r[...]) for o_r, w_r in branches] for rows in halves]
    ys = []
    for h in range(2):
        y = None
        for br in range(len(branches)):
            term = _sigmoid(logits[h][br]) * projs[h][br]
            y = term if y is None else y + term
        ys.append(y.astype(BF16))
    outs = [_dot(y, wo_ref[...]) for y in ys]
    for h, rows in enumerate(halves):
        out_ref[rows, :] = _layer_norm(DN_ALPHA * xs[h] + outs[h], g_ref[...], b_ref[...])


def _merge_call(x2, onsa, osgu, omem, wg, wbn, wbs, wbm, wo, g1, b1):
    t, d = x2.shape
    row = lambda i: (i, 0)
    c2 = lambda i: (0, 0)
    full = lambda a: pl.BlockSpec(a.shape, c2)
    once = lambda a: pl.BlockSpec(a.shape, c2, pipeline_mode=pl.Buffered(1))
    return pl.pallas_call(
        _merge_kernel,
        grid=(t // MERGE_TM,),
        in_specs=[
            pl.BlockSpec((MERGE_TM, d), row),
            pl.BlockSpec((MERGE_TM, onsa.shape[1]), row),
            pl.BlockSpec((MERGE_TM, osgu.shape[1]), row),
            pl.BlockSpec((MERGE_TM, omem.shape[1]), row),
            pl.BlockSpec((3 * d, d), c2, pipeline_mode=pl.Buffered(1)),
            once(wbn), once(wbs), once(wbm), once(wo), full(g1), full(b1),
        ],
        out_specs=pl.BlockSpec((MERGE_TM, d), row),
        out_shape=jax.ShapeDtypeStruct((t, d), F32),
        scratch_shapes=[pltpu.VMEM(a.shape, BF16) for a in (wbn, wbs, wbm, wo)]
        + [pltpu.VMEM((d, 3 * d), BF16)],
        compiler_params=pltpu.CompilerParams(
            dimension_semantics=("arbitrary",), vmem_limit_bytes=VMEM_LIMIT),
        name="merge",
    )(x2, onsa, osgu, omem, wg, wbn, wbs, wbm, wo, g1, b1)


MOE_TM = 1024
MOE_CH = 144
_YS_ROWS = -(-(MOE_TM + N_GROUPS * MOE_CH) // 256) * 256
_YS_TYPICAL = -(-(2 * N_GROUPS * MOE_CH) // 256) * 256
_R_SLOT = 8
_DEST_LANE = 3 * _R_SLOT
_HID = EXPERTS_PER_GROUP * EXPERT_FF


def _moe_kernel(x_ref, tri_ref, wr_ref, br_ref, wg_ref, wu_ref, wd_ref, g_ref, b_ref, out_ref,
                xa_s, tok_s, keyr_s, ys_s, cnt_s, base_s):
    grp = pl.program_id(1)
    tm = x_ref.shape[0]
    d = x_ref.shape[1]

    @pl.when(grp == 0)
    def _route():
        xb = x_ref[...].astype(BF16)
        xa_s[:, 0:d] = xb
        lt = _dot_nt(wr_ref[...], xb) + br_ref[...]
        row = lax.broadcasted_iota(jnp.int32, (_R_SLOT, tm), 0)
        gl = jnp.where(row < N_GROUPS, lt[0:_R_SLOT], NEG)
        gmax = jnp.max(gl, axis=0, keepdims=True)
        gidx = jnp.min(jnp.where(gl == gmax, row, _R_SLOT), axis=0, keepdims=True)
        gprob = 1.0 / jnp.sum(jnp.exp(gl - gmax), axis=0, keepdims=True)
        el = lt[_R_SLOT:2 * _R_SLOT]
        for k in range(1, N_GROUPS):
            el = jnp.where(gidx == k, lt[(k + 1) * _R_SLOT:(k + 2) * _R_SLOT], el)
        ee = jnp.exp(el - jnp.max(el, axis=0, keepdims=True))
        ep = ee / jnp.sum(ee, axis=0, keepdims=True)
        t1 = jnp.max(ep, axis=0, keepdims=True)
        i1 = jnp.min(jnp.where(ep == t1, row, _R_SLOT), axis=0, keepdims=True)
        rest = row != i1
        t2 = jnp.max(jnp.where(rest, ep, -1.0), axis=0, keepdims=True)
        i2 = jnp.min(jnp.where(rest & (ep == t2), row, _R_SLOT), axis=0, keepdims=True)
        den = t1 + t2
        cwf = (jnp.where(row == i1, t1 / den, 0.0) + jnp.where(row == i2, t2 / den, 0.0)) * gprob
        hi = cwf.astype(BF16).astype(F32)
        r1 = cwf - hi
        lo = r1.astype(BF16).astype(F32)
        lo2 = (r1 - lo).astype(BF16).astype(F32)

        onehot = jnp.where(row == gidx, 1.0, 0.0)
        tri = tri_ref[...]
        onehot_b = onehot.astype(BF16)
        ranks, before = [], jnp.zeros((_R_SLOT, 1), F32)
        for b in range(tm // LANES):
            blk = slice(b * LANES, (b + 1) * LANES)
            ranks.append(_dot_nt(onehot_b[:, blk], tri) + before)
            before = before + jnp.sum(onehot[:, blk], axis=1, keepdims=True)
        rank = jnp.concatenate(ranks, axis=1)
        keyr_s[...] = jnp.where(onehot > 0.5, rank, -1.0)
        base = jnp.int32(0)
        basev = jnp.zeros((1, tm), F32)
        for k in range(N_GROUPS):
            n_k = jnp.sum(onehot[k:k + 1, :]).astype(jnp.int32)
            cnt_s[k] = n_k
            base_s[k] = base
            basev = jnp.where(gidx == k, base.astype(F32), basev)
            base = base + ((n_k + MOE_CH - 1) // MOE_CH) * MOE_CH
        base_s[N_GROUPS] = base
        dest =jnp.sum(rank * onehot, axis=0, keepdims=True) + basev
        tok = jnp.concatenate(
            [hi, lo, lo2, jnp.broadcast_to(dest, (_R_SLOT, tm)),
             jnp.zeros((LANES - 4 * _R_SLOT, tm), F32)], axis=0).T
        tok_s[...] = tok
        lane = lax.broadcasted_iota(jnp.int32, tok.shape, 1)
        xa_s[:, d:d + LANES] = jnp.where(lane < _DEST_LANE, tok, 0.0).astype(BF16)
        ys_s[...] = jnp.zeros(ys_s.shape, BF16)

    n_rows = cnt_s[grp]
    row0 = base_s[grp]
    keyr = keyr_s[pl.ds(grp, 1), :]
    half = _HID // 2

    ch = MOE_CH

    def gather(k):
        r_row = (lax.broadcasted_iota(jnp.int32, (ch, tm), 0) + k * MOE_CH).astype(F32)
        pick = jnp.where(keyr == r_row, 1.0, 0.0).astype(BF16)
        ga = _dot(pick, xa_s[...])
        xg = ga[:, 0:d].astype(BF16)
        cwg = ga[:, d:d + LANES]
        cs = (cwg + pltpu.roll(cwg, LANES - _R_SLOT, 1)
              + pltpu.roll(cwg, LANES - 2 * _R_SLOT, 1))
        return xg, cs

    def experts_ffn(k, xg, cs):
        experts = range(EXPERTS_PER_GROUP)
        per_half = EXPERTS_PER_GROUP // 2
        hg = [_dot(xg, wg_ref[0, e]) for e in experts]
        hu = [_dot(xg, wu_ref[0, e]) for e in experts]
        y = None
        for h in range(2):
            hid = jnp.concatenate(
                [(((hg[e] * _sigmoid(hg[e])) * hu[e]) * cs[:, e:e + 1]).astype(BF16)
                 for e in range(h * per_half, (h + 1) * per_half)], axis=1)
            term = _dot(hid, wd_ref[0, h * half:(h + 1) * half, :])
            y = term if y is None else y + term
        ys_s[pl.ds(pl.multiple_of(row0 + k * MOE_CH, 16), ch), :] = y.astype(BF16)

    def sweep(k, carry):
        experts_ffn(k, *gather(k))
        return carry

    n_sweeps = (n_rows + MOE_CH - 1) // MOE_CH

    @pl.when(n_sweeps == 2)
    def _():
        g0 = gather(0)
        g1 = gather(1)
        experts_ffn(0, *g0)
        experts_ffn(1, *g1)

    @pl.when(n_sweeps != 2)
    def _():
        lax.fori_loop(0, n_sweeps, sweep, 0)

    def fin(k_rows):
        n_q = 4
        q_rows = tm // n_q
        r_col = lax.broadcasted_iota(jnp.int32, (q_rows, k_rows), 1).astype(F32)
        parts = [slice(i * q_rows, (i + 1) * q_rows) for i in range(n_q)]
        puts = [jnp.where(tok_s[rows, _DEST_LANE:_DEST_LANE + 1] == r_col, 1.0, 0.0).astype(BF16)
                for rows in parts]
        fs = [_dot(put, ys_s[0:k_rows, :]) for put in puts]
        for rows, f in zip(parts, fs):
            out_ref[rows, :] = _layer_norm(DN_ALPHA * x_ref[rows, :] + f, g_ref[...], b_ref[...])

    last = grp == N_GROUPS - 1
    used = base_s[N_GROUPS]
    pl.when(last & (used <= _YS_TYPICAL))(functools.partial(fin, _YS_TYPICAL))
    pl.when(last & (used > _YS_TYPICAL))(functools.partial(fin, _YS_ROWS))


def _moe_call(x1, wr, br, wg, wu, wd, g2, b2):
    t, d = x1.shape
    row = lambda i, g: (i, 0)
    c2 = lambda i, g: (0, 0)
    idx = np.arange(LANES)
    tri = jnp.asarray(idx[None, :] < idx[:, None], dtype=BF16)
    return pl.pallas_call(
        _moe_kernel,
        grid=(t // MOE_TM, N_GROUPS),
        in_specs=[
            pl.BlockSpec((MOE_TM, d), row),
            pl.BlockSpec((LANES, LANES), c2),
            pl.BlockSpec(wr.shape, c2),
            pl.BlockSpec(br.shape, c2),
            pl.BlockSpec((1, EXPERTS_PER_GROUP, d, EXPERT_FF), lambda i, g: (g, 0, 0, 0)),
            pl.BlockSpec((1, EXPERTS_PER_GROUP, d, EXPERT_FF), lambda i, g: (g, 0, 0, 0)),
            pl.BlockSpec((1, _HID, d), lambda i, g: (g, 0, 0)),
            pl.BlockSpec(g2.shape, c2),
            pl.BlockSpec(b2.shape, c2),
        ],
        out_specs=pl.BlockSpec((MOE_TM, d), row),
        out_shape=jax.ShapeDtypeStruct((t, d), F32),
        scratch_shapes=[
            pltpu.VMEM((MOE_TM, d + LANES), BF16),
            pltpu.VMEM((MOE_TM, LANES), F32),
            pltpu.VMEM((_R_SLOT, MOE_TM), F32),
            pltpu.VMEM((_YS_ROWS, d), BF16),
            pltpu.SMEM((N_GROUPS,), jnp.int32),
            pltpu.SMEM((N_GROUPS + 1,), jnp.int32),
        ],
        compiler_params=pltpu.CompilerParams(
            dimension_semantics=("parallel", "arbitrary"), vmem_limit_bytes=MOE_VMEM_LIMIT),
        name="moe",
    )(x1, tri, wr, br, wg, wu, wd, g2, b2)


def _rope_table(seq):
    half = ROPE_DIM // 2
    inv = ROPE_THETA ** (-jnp.arange(0, ROPE_DIM, 2, dtype=F32) / ROPE_DIM)
    ang = jnp.arange(seq, dtype=F32)[:, None] * inv[None, :]
    cos, sin = jnp.cos(ang), jnp.sin(ang)
    rest = HEAD_DIM - ROPE_DIM
    one = jnp.ones((seq, rest), F32)
    zero = jnp.zeros((seq, rest), F32)
    zh = jnp.zeros((seq, half), F32)
    c_h = jnp.concatenate([cos, cos, one], axis=1)
    dn_h = jnp.concatenate([-sin, zh, zero], axis=1)
    up_h = jnp.concatenate([zh, sin, zero], axis=1)
    rep = LANES // HEAD_DIM
    return jnp.concatenate([jnp.tile(c_h, (1, rep)), jnp.tile(dn_h, (1, rep)), jnp.tile(up_h, (1, rep))],
                           axis=1)


def _expand_cmp_weights(pe, w1, w2):
    g_n = NSA_KV_GROUPS
    w1r = w1.astype(BF16).reshape(2, CMP_STRIDE, HEAD_DIM, CMP_HIDDEN)
    w2b = w2.astype(BF16)
    w1e = jnp.concatenate(
        [jnp.stack([w1r if g == e else jnp.zeros_like(w1r) for g in range(g_n)], axis=2)
         .reshape(2, A_W, CMP_HIDDEN) for e in range(g_n)], axis=2)
    w2e = jnp.concatenate(
        [jnp.concatenate([w2b if g == e else jnp.zeros_like(w2b) for e in range(g_n)], axis=1)
         for g in range(g_n)], axis=0)
    per = pe.reshape(2, CMP_STRIDE, 1, HEAD_DIM)
    pee = jnp.broadcast_to(per, (2, CMP_STRIDE, g_n, HEAD_DIM)).reshape(2, A_W)
    pee = jnp.concatenate([pee, jnp.zeros((6, A_W), pe.dtype)], axis=0)
    return pee.astype(F32), w1e.astype(BF16), w2e.astype(BF16)


def kernel(x, mem, w_in, cmp_pe_k, cmp_w1_k, cmp_w2_k, cmp_pe_v, cmp_w1_v, cmp_w2_v, sgu_ln_g, sgu_ln_b,
           sgu_w_s, sgu_b_s, w_mem_kv, w_br_nsa, w_br_sgu, w_br_mem, w_o, ln1_g, ln1_b, w_router_group,
           b_router_group, w_router_expert, b_router_expert, w_exp_gate, w_exp_up, w_exp_down, ln2_g, ln2_b):
    batch, seq, d = x.shape
    t = batch * seq
    assert w_in.shape[0] == DEPTH == 1
    assert seq % PROJ_TM == 0 and seq // CMP_STRIDE == N_A and seq // SLC_BLOCK == N_SEL

    offs = [int(v) for v in np.cumsum(
        [0, Q_W, KV_W, KV_W, KV_W, KV_W, KV_W, KV_W, GATE_W, 2 * SGU_WIDTH, MEM_WIDTH, 3 * d])]
    assert offs[-1] == w_in.shape[2]
    w_t = jnp.transpose(w_in[0])
    seg = lambda i: w_t[offs[i]:offs[i + 1]]
    gate_rows = jnp.pad(seg(7), ((0, LANES - GATE_W), (0, 0)))
    wp_t = jnp.concatenate([seg(0), seg(1), seg(3), seg(5), seg(2), seg(4), seg(6), gate_rows, seg(8), seg(9)],
                           axis=0).astype(BF16)
    w_mt = seg(10).astype(BF16)
    assert wp_t.shape[0] == _PROJ_COLS
    rope = _rope_table(seq)
    tril = jnp.tril(jnp.ones((SGU_CHUNK, SGU_CHUNK), dtype=bool))
    ws = jnp.where(tril[None], sgu_w_s[0], 0.0).astype(BF16)
    bs = jnp.repeat(sgu_b_s[0].T, SGU_WIDTH // SGU_GROUPS, axis=1)

    x2 = x.reshape(t, d)
    (qpad, kc, ksl, kwn, vc, vslt, vwnt, gt, osgu, mq) = _proj_call(
        x2, wp_t, rope, sgu_ln_g[0][None], sgu_ln_b[0][None], ws, bs, seq)

    pek, w1k, w2k = _expand_cmp_weights(cmp_pe_k[0], cmp_w1_k[0], cmp_w2_k[0])
    pev, w1v, w2v = _expand_cmp_weights(cmp_pe_v[0], cmp_w1_v[0], cmp_w2_v[0])
    kcmp, vcmp = _compress_call(kc, vc, pek, pev, w1k, w2k, w1v, w2v, batch)

    vcmpt = vcmp.reshape(batch, N_A, KV_W).transpose(0, 2, 1).reshape(batch * KV_W, N_A)
    ci = np.arange(N_A)
    sj = np.arange(N_SEL)
    overlap = ((ci[None, :] * CMP_STRIDE + CMP_LEN - 1 >= sj[:, None] * SLC_BLOCK)
               & (ci[None, :] * CMP_STRIDE <= sj[:, None] * SLC_BLOCK + SLC_BLOCK - 1)
               & (ci[None, :] < (seq - CMP_LEN) // CMP_STRIDE + 1))
    ot = jnp.asarray(overlap, dtype=BF16)
    onsa = _nsa_call(qpad, ksl, kwn, vslt, vwnt, kcmp, vcmpt, gt, ot, batch, seq)

    omem = _memattn_call(mq, mem.reshape(batch * mem.shape[1], d), w_mem_kv[0], batch, seq)

    x1 = _merge_call(x2, onsa, osgu, omem, w_mt, w_br_nsa[0], w_br_sgu[0], w_br_mem[0], w_o[0],
                     ln1_g[0][None], ln1_b[0][None])

    assert EXPERTS_PER_GROUP == _R_SLOT and N_GROUPS <= _R_SLOT
    n_r = _R_SLOT + N_GROUPS * EXPERTS_PER_GROUP
    wr = jnp.concatenate([jnp.pad(w_router_group[0], ((0, 0), (0, _R_SLOT - N_GROUPS))),
                          w_router_expert[0]], axis=1)
    wr = jnp.pad(wr, ((0, 0), (0, LANES - n_r))).astype(BF16).T
    br = jnp.concatenate([jnp.pad(b_router_group[0], (0, _R_SLOT - N_GROUPS)), b_router_expert[0]])
    br = jnp.pad(br, (0, LANES - n_r))[:, None]
    wg = w_exp_gate[0].astype(BF16)
    wu = w_exp_up[0].astype(BF16)
    wd = w_exp_down[0].reshape(N_GROUPS, _HID, d).astype(BF16)
    out = _moe_call(x1, wr, br, wg, wu, wd, ln2_g[0][None], ln2_b[0][None])
    return out.reshape(batch, seq, d)
```

```python
import functools

import numpy as np
import jax
import jax.numpy as jnp
from jax import lax
from jax.experimental import pallas as pl
from jax.experimental.pallas import tpu as pltpu

NSA_HEADS = 8
NSA_KV_GROUPS = 2
NSA_HPG = NSA_HEADS // NSA_KV_GROUPS
HEAD_DIM = 64
CMP_LEN = 32
CMP_STRIDE = 16
CMP_HIDDEN = 256
SLC_BLOCK = 64
SLC_TOPN = 8
WINDOW = 512
Q_BLOCK = 128
N_BAND = WINDOW // Q_BLOCK
ROPE_THETA = 500000.0
ROPE_DIM = HEAD_DIM // 4
SGU_CHUNK = 128
SGU_GROUPS = 8
SGU_WIDTH = 512
MEM_HEADS = 4
MEM_HEAD_DIM = 128
MEM_WIDTH = MEM_HEADS * MEM_HEAD_DIM
N_GROUPS = 4
EXPERTS_PER_GROUP = 8
EXPERT_FF = 256
DEPTH = 1
DN_ALPHA = (2.0 * DEPTH) ** 0.25
LN_EPS = 1e-5
NEG = -1e30
LOG2E = 1.4426950408889634

LANES = 128
Q_W = NSA_HEADS * HEAD_DIM
KV_W = NSA_KV_GROUPS * HEAD_DIM
GATE_W = NSA_HEADS * 3
VMEM_LIMIT = 56 * 1024 * 1024
MOE_VMEM_LIMIT = 60 * 1024 * 1024

BF16 = jnp.bfloat16
F32 = jnp.float32


def _dot(a, b):
    return jnp.dot(a, b, preferred_element_type=F32)


def _dot_nt(a, b):
    return lax.dot_general(a, b, (((1,), (1,)), ((), ())), preferred_element_type=F32)


def _sigmoid(x):
    return 1.0 / (1.0 + jnp.exp(-x))


def _gelu(x):
    return 0.5 * x * (1.0 + lax.erf(x * (2.0 ** -0.5)))


def _layer_norm(x, g, b):
    mu = jnp.mean(x, axis=-1, keepdims=True)
    xc = x - mu
    var = jnp.mean(xc * xc, axis=-1, keepdims=True)
    return xc * lax.rsqrt(var + LN_EPS) * g + b


PROJ_TM = 512
_ROPE_COLS = Q_W + 3 * KV_W
_V_OFF = _ROPE_COLS
_G_OFF = _V_OFF + 3 * KV_W
_SGU_OFF = _G_OFF + LANES
_MQ_OFF = _SGU_OFF + 2 * SGU_WIDTH
_PROJ_COLS = _MQ_OFF + MEM_WIDTH


def _store_strided_rows(val, out_ref, tmp_ref):
    tmp_ref[...] = val
    n = val.shape[0] // CMP_STRIDE
    for l in range(CMP_STRIDE):
        out_ref[:, l * LANES:(l + 1) * LANES] = tmp_ref[pl.ds(l, n, stride=CMP_STRIDE), :].astype(BF16)


def _transpose_weight(wt_ref, w_s):
    for j in range(wt_ref.shape[0] // LANES):
        rows = slice(j * LANES, (j + 1) * LANES)
        w_s[:, rows] = wt_ref[rows, :].astype(F32).T.astype(BF16)


def _proj_kernel(x_ref, wt_ref, rope_ref, lng_ref, lnb_ref, ws_ref, bs_ref,
                 q_ref, kc_ref, ksl_ref, kwn_ref, vc_ref, vsl_ref, vwn_ref, gate_ref, osgu_ref, mq_ref,
                 a_s, w_ref):
    pl.when(pl.program_id(0) == 0)(functools.partial(_transpose_weight, wt_ref, w_ref))
    xb = x_ref[...].astype(BF16)
    lane = lax.broadcasted_iota(jnp.int32, (PROJ_TM, LANES), 1)
    low = lane < HEAD_DIM
    cos = rope_ref[:, 0:LANES]
    s_dn = rope_ref[:, LANES:2 * LANES]
    s_up = rope_ref[:, 2 * LANES:3 * LANES]

    hz = _dot(xb, w_ref[:, _SGU_OFF:_MQ_OFF])
    h = _dot(xb, w_ref[:, 0:_ROPE_COLS])
    k_refs = (kc_ref, ksl_ref, kwn_ref)
    for j in range(_ROPE_COLS // LANES):
        blk = h[:, j * LANES:(j + 1) * LANES]
        r = (blk * cos + pltpu.roll(blk, LANES - ROPE_DIM // 2, 1) * s_dn
             + pltpu.roll(blk, ROPE_DIM // 2, 1) * s_up)
        if j < Q_W // LANES:
            r = r * (HEAD_DIM ** -0.5 * LOG2E)
            sw = pltpu.roll(r, HEAD_DIM, 1)
            g = (2 * j) // NSA_HPG
            if g == 0:
                h0 = jnp.where(low, r, 0.0)
                h1 = jnp.where(low, sw, 0.0)
            else:
                h0 = jnp.where(low, 0.0, sw)
                h1 = jnp.where(low, 0.0, r)
            q_ref[:, (2 * j) * LANES:(2 * j + 1) * LANES] = h0.astype(BF16)
            q_ref[:, (2 * j + 1) * LANES:(2 * j + 2) * LANES] = h1.astype(BF16)
        else:
            if j == Q_W // LANES:
                _store_strided_rows(r, kc_ref, a_s)
            else:
                k_refs[j - Q_W // LANES][...] = r.astype(BF16)

    _store_strided_rows(_dot(xb, w_ref[:, _V_OFF:_V_OFF + LANES]), vc_ref, a_s)
    ones = jnp.ones((V_ROWS - HEAD_DIM, PROJ_TM), BF16)
    for v_ref, row0, blk in ((vsl_ref, _V_OFF + LANES, SEL_CHUNK), (vwn_ref, _V_OFF + 2 * LANES, Q_BLOCK)):
        vt = _dot_nt(wt_ref[row0:row0 + LANES, :], xb).astype(BF16)
        for g in range(NSA_KV_GROUPS):
            for b in range(PROJ_TM // blk):
                v_ref[g, b, 0:HEAD_DIM, :] = vt[g * HEAD_DIM:(g + 1) * HEAD_DIM, b * blk:(b + 1) * blk]
                v_ref[g, b, HEAD_DIM:V_ROWS, :] = ones[:, 0:blk]

    gate_ref[...] = _sigmoid(_dot_nt(wt_ref[_G_OFF:_SGU_OFF, :], xb))

    mq_ref[...] = _dot(xb, w_ref[:, _MQ_OFF:_PROJ_COLS]).astype(BF16)

    z = _gelu(hz)
    u = z[:, 0:SGU_WIDTH]
    v = _layer_norm(z[:, SGU_WIDTH:2 * SGU_WIDTH], lng_ref[...], lnb_ref[...]).astype(BF16)
    lane_c = lax.broadcasted_iota(jnp.int32, (SGU_CHUNK, LANES), 1)
    low_c = lane_c < (SGU_WIDTH // SGU_GROUPS)
    n_ch = PROJ_TM // SGU_CHUNK
    for gp in range(SGU_WIDTH // LANES):
        cols = slice(gp * LANES, (gp + 1) * LANES)
        vcat = jnp.concatenate([v[ci * SGU_CHUNK:(ci + 1) * SGU_CHUNK, cols] for ci in range(n_ch)], axis=1)
        sv0 = _dot(ws_ref[2 * gp], vcat)
        sv1 = _dot(ws_ref[2 * gp + 1], vcat)
        for ci in range(n_ch):
            rows = slice(ci * SGU_CHUNK, (ci + 1) * SGU_CHUNK)
            lanes = slice(ci * LANES, (ci + 1) * LANES)
            sv = jnp.where(low_c, sv0[:, lanes], sv1[:, lanes]) + bs_ref[:, cols]
            osgu_ref[rows, cols] = (u[rows, cols] * sv).astype(BF16)


def _proj_call(x2, wp, rope, lng, lnb, ws, bs, seq):
    t = x2.shape[0]
    d = x2.shape[1]
    nt = t // PROJ_TM
    per_seq = seq // PROJ_TM
    row = lambda i: (i, 0)
    const2 = lambda i: (0, 0)
    assert PROJ_TM % SEL_CHUNK == 0 and PROJ_TM % Q_BLOCK == 0
    n_g = NSA_KV_GROUPS
    batch = t // seq
    vblock = lambda i: (i // per_seq, i % per_seq, 0, 0)

    def v_shape(blk):
        return jax.ShapeDtypeStruct((batch * n_g, seq // blk, V_ROWS, blk), BF16)

    def v_spec(blk):
        return pl.BlockSpec((n_g, PROJ_TM // blk, V_ROWS, blk), vblock)

    a_shape = jax.ShapeDtypeStruct((t // CMP_STRIDE, CMP_STRIDE * KV_W), BF16)
    a_spec = pl.BlockSpec((PROJ_TM // CMP_STRIDE, CMP_STRIDE * KV_W), row)
    k_shape = jax.ShapeDtypeStruct((t, KV_W), BF16)
    k_spec = pl.BlockSpec((PROJ_TM, KV_W), row)
    out_shapes = (
        jax.ShapeDtypeStruct((t, NSA_HEADS * LANES), BF16),
        a_shape, k_shape, k_shape, a_shape,
        v_shape(SEL_CHUNK), v_shape(Q_BLOCK),
        jax.ShapeDtypeStruct((LANES, t), F32),
        jax.ShapeDtypeStruct((t, SGU_WIDTH), BF16),
        jax.ShapeDtypeStruct((t, MEM_WIDTH), BF16),
    )
    out_specs = (
        pl.BlockSpec((PROJ_TM, NSA_HEADS * LANES), row),
        a_spec, k_spec, k_spec, a_spec,
        v_spec(SEL_CHUNK), v_spec(Q_BLOCK),
        pl.BlockSpec((LANES, PROJ_TM), lambda i: (0, i)),
        pl.BlockSpec((PROJ_TM, SGU_WIDTH), row),
        pl.BlockSpec((PROJ_TM, MEM_WIDTH), row),
    )
    return pl.pallas_call(
        _proj_kernel,
        grid=(nt,),
        in_specs=[
            pl.BlockSpec((PROJ_TM, d), row),
            pl.BlockSpec((_PROJ_COLS, d), const2),
            pl.BlockSpec((PROJ_TM, 3 * LANES), lambda i: (i % per_seq, 0)),
            pl.BlockSpec((1, SGU_WIDTH), const2),
            pl.BlockSpec((1, SGU_WIDTH), const2),
            pl.BlockSpec((SGU_GROUPS, SGU_CHUNK, SGU_CHUNK), lambda i: (0, 0, 0)),
            pl.BlockSpec((SGU_CHUNK, SGU_WIDTH), const2),
        ],
        out_specs=out_specs,
        out_shape=out_shapes,
        scratch_shapes=[pltpu.VMEM((PROJ_TM, KV_W), F32), pltpu.VMEM((d, _PROJ_COLS), BF16)],
        compiler_params=pltpu.CompilerParams(
            dimension_semantics=("arbitrary",), vmem_limit_bytes=VMEM_LIMIT),
        name="proj",
    )(x2, wp, rope, lng, lnb, ws, bs)


N_A = 128
A_W = CMP_STRIDE * KV_W


def _compress_kernel(ka_ref, va_ref, pek_ref, pev_ref, w1k_ref, w2k_ref, w1v_ref, w2v_ref,
                     kcmp_ref, vcmp_ref):
    def one(a_ref, pe_ref, w1_ref, w2_ref, out_ref):
        a = a_ref[...].astype(F32)
        top = (a + pe_ref[0:1, :]).astype(BF16)
        bot = (a + pe_ref[1:2, :]).astype(BF16)
        h1 = _dot(top, w1_ref[0])
        h2 = _dot(bot, w1_ref[1])
        pre = h1 + pltpu.roll(h2, N_A - 1, 0)
        act = _gelu(pre).astype(BF16)
        out_ref[...] = _dot(act, w2_ref[...]).astype(BF16)

    one(ka_ref, pek_ref, w1k_ref, w2k_ref, kcmp_ref)
    one(va_ref, pev_ref, w1v_ref, w2v_ref, vcmp_ref)


def _compress_call(ka, va, pek, pev, w1k, w2k, w1v, w2v, batch):
    row = lambda b: (b, 0)
    c2 = lambda b: (0, 0)
    c3 = lambda b: (0, 0, 0)
    hid2 = NSA_KV_GROUPS * CMP_HIDDEN
    return pl.pallas_call(
        _compress_kernel,
        grid=(batch,),
        in_specs=[
            pl.BlockSpec((N_A, A_W), row),
            pl.BlockSpec((N_A, A_W), row),
            pl.BlockSpec((8, A_W), c2),
            pl.BlockSpec((8, A_W), c2),
            pl.BlockSpec((2, A_W, hid2), c3),
            pl.BlockSpec((hid2, KV_W), c2),
            pl.BlockSpec((2, A_W, hid2), c3),
            pl.BlockSpec((hid2, KV_W), c2),
        ],
        out_specs=(pl.BlockSpec((N_A, KV_W), row), pl.BlockSpec((N_A, KV_W), row)),
        out_shape=(jax.ShapeDtypeStruct((batch * N_A, KV_W), BF16),
                   jax.ShapeDtypeStruct((batch * N_A, KV_W), BF16)),
        compiler_params=pltpu.CompilerParams(
            dimension_semantics=("parallel",), vmem_limit_bytes=VMEM_LIMIT),
        name="compress",
    )(ka, va, pek, pev, w1k, w2k, w1v, w2v)


N_SEL = 32
SEL_CHUNK = 256
SEL_SUB = 128
HQ = NSA_HPG * Q_BLOCK
V_ROWS = HEAD_DIM + 16


NSA_QB_PER_STEP = SEL_CHUNK // Q_BLOCK


def _nsa_kernel(*refs):
    ksl_ref = refs[1]
    reach = pl.program_id(1)

    def run(n):
        for part in range(NSA_QB_PER_STEP):
            _nsa_body(n, part, *refs)

    for n in range(ksl_ref.shape[0] // SEL_CHUNK):
        pl.when(reach == n)(functools.partial(run, n))


def _nsa_body(n_later, part, q_all, ksl_ref, kwn_ref, vslt_ref, vwnt_ref, kcmp_ref, vcmpt_ref, gt_all,
              ot_ref, o_all, m_s, acc_s, ocmp_s, owin_s, cap_s, out_s):
    rows = pl.ds(part * Q_BLOCK, Q_BLOCK)
    q_ref, o_ref, gt_ref = q_all.at[rows], o_all.at[rows], gt_all.at[:, rows]
    c = pl.program_id(1) * NSA_QB_PER_STEP + part
    lane_hq = lax.broadcasted_iota(jnp.int32, (1, HQ), 1)
    pos_hq = c * Q_BLOCK + (lane_hq & (Q_BLOCK - 1))
    pos_q = c * Q_BLOCK + lax.broadcasted_iota(jnp.int32, (1, Q_BLOCK), 1)
    groups = range(NSA_KV_GROUPS)
    qgs = [jnp.concatenate(
        [q_ref[:, (g * NSA_HPG + hh) * LANES:(g * NSA_HPG + hh + 1) * LANES]
         for hh in range(NSA_HPG)], axis=0) for g in groups]

    kcmp = kcmp_ref[...]
    s_c = [_dot_nt(kcmp, qgs[g]) for g in groups]
    n_idx = lax.broadcasted_iota(jnp.int32, (N_A, HQ), 0)
    valid_c = (n_idx * CMP_STRIDE + (CMP_LEN - 1)) <= pos_hq
    p_c = []
    for g in groups:
        sm_c = jnp.where(valid_c, s_c[g], NEG)
        m_c = jnp.max(sm_c, axis=0, keepdims=True)
        e_c = jnp.where(valid_c, jnp.exp2(sm_c - m_c), 0.0)
        d_c = jnp.sum(e_c, axis=0, keepdims=True)
        p_c.append(e_c / jnp.where(d_c > 0, d_c, 1.0))
    for g in groups:
        ocmp_s[g] = _dot(vcmpt_ref[g * HEAD_DIM:(g + 1) * HEAD_DIM, :], p_c[g].astype(BF16))

    ot = ot_ref[...]
    imp = []
    for g in groups:
        ps = (p_c[g][:, 0:Q_BLOCK] + p_c[g][:, Q_BLOCK:2 * Q_BLOCK]
              + p_c[g][:, 2 * Q_BLOCK:3 * Q_BLOCK] + p_c[g][:, 3 * Q_BLOCK:4 * Q_BLOCK])
        p_hi = ps.astype(BF16)
        r1 = ps - p_hi.astype(F32)
        p_lo = r1.astype(BF16)
        p_lo2 = (r1 - p_lo.astype(F32)).astype(BF16)
        imp.append(_dot(ot, p_hi) + _dot(ot, p_lo) + _dot(ot, p_lo2))

    j_idx = lax.broadcasted_iota(jnp.int32, (N_SEL, Q_BLOCK), 0)
    cur = pos_q // SLC_BLOCK
    future = j_idx > cur
    forced = (j_idx == 0) | (j_idx == cur) | (j_idx == cur - 1)
    imp = [jnp.where(future, NEG, jnp.where(forced, -NEG, imp[g])) for g in groups]
    rank = [jnp.zeros((N_SEL, Q_BLOCK), F32) for g in groups]
    for i in range(N_SEL):
        for g in groups:
            row = imp[g][i:i + 1, :]
            beats = (row > imp[g]) | ((row == imp[g]) & (j_idx > i))
            rank[g] = rank[g] + jnp.where(beats, 1.0, 0.0)
    for g in groups:
        cap_s[g] = jnp.where(rank[g] < float(SLC_TOPN), -NEG, NEG)
        m_s[g] = jnp.full((1, HQ), NEG, F32)
        acc_s[g] = jnp.zeros((V_ROWS, HQ), F32)

    blocks_per_sub = SEL_SUB // SLC_BLOCK
    subs_per_chunk = SEL_CHUNK // SEL_SUB

    def sel_spans(kc):
        return [slice(kc * SEL_CHUNK + a * SEL_SUB, kc * SEL_CHUNK + (a + 1) * SEL_SUB)
                for a in range(subs_per_chunk)]

    def sel_scores(kc):
        return [[_dot_nt(ksl_ref[ks, :], qgs[g]) for g in groups] for ks in sel_spans(kc)]

    def sel_chunk(kc, s=None):
        spans = sel_spans(kc)
        s = sel_scores(kc) if s is None else s
        for a, ks in enumerate(spans):
            kpos = ks.start + lax.broadcasted_iota(jnp.int32, (SEL_SUB, Q_BLOCK), 0)
            causal = kpos <= pos_q
            for g in groups:
                j0 = ks.start // SLC_BLOCK
                cap = jnp.concatenate(
                    [jnp.broadcast_to(cap_s[g, j:j + 1, :], (SLC_BLOCK, Q_BLOCK))
                     for j in range(j0, j0 + blocks_per_sub)], axis=0)
                cap = jnp.where(causal, cap, NEG)
                sm = jnp.minimum(s[a][g], jnp.concatenate([cap] * NSA_HPG, axis=1))
                m_old = m_s[g]
                m_new = jnp.maximum(m_old, jnp.max(sm, axis=0, keepdims=True))
                alpha = jnp.exp2(m_old - m_new)
                e = jnp.exp2((sm - m_new).astype(BF16))
                v_t = vslt_ref[g, kc, :, a * SEL_SUB:(a + 1) * SEL_SUB]
                acc_s[g] = alpha * acc_s[g] + _dot(v_t, e)
                m_s[g] = m_new

    s0 = sel_scores(0)
    _nsa_window(c, qgs, kwn_ref, vwnt_ref, owin_s)
    s_next = sel_scores(1) if n_later else None
    sel_chunk(0, s0)
    for kc in range(1, n_later + 1):
        s_cur = s_next
        if kc < n_later:
            s_next = sel_scores(kc + 1)
        sel_chunk(kc, s_cur)

    for g in groups:
        acc = acc_s[g]
        o_sel = acc[0:HEAD_DIM, :] * (1.0 / acc[HEAD_DIM:HEAD_DIM + 1, :])

        def gate_row(br):
            return jnp.concatenate(
                [gt_ref[(g * NSA_HPG + hh) * 3 + br:(g * NSA_HPG + hh) * 3 + br + 1, :]
                 for hh in range(NSA_HPG)], axis=1)
        o_t = gate_row(0) * ocmp_s[g] + gate_row(1) * o_sel + gate_row(2) * owin_s[g]
        for hh in range(NSA_HPG):
            h = g * NSA_HPG + hh
            out_s[h * HEAD_DIM:(h + 1) * HEAD_DIM, :] = o_t[:, hh * Q_BLOCK:(hh + 1) * Q_BLOCK]

    o_ref[...] = out_s[...].T.astype(BF16)


def _nsa_window(c, qgs, kwn_ref, vwnt_ref, owin_s):
    groups = range(NSA_KV_GROUPS)
    q_i = lax.broadcasted_iota(jnp.int32, (Q_BLOCK, Q_BLOCK), 1)
    k_i = lax.broadcasted_iota(jnp.int32, (Q_BLOCK, Q_BLOCK), 0)
    blks, caps = [], []
    for i in range(N_BAND + 1):
        blk = c - N_BAND + i
        blks.append(jnp.maximum(blk, 0))
        off = (N_BAND - i) * Q_BLOCK
        if off - (Q_BLOCK - 1) >= 0 and off + (Q_BLOCK - 1) < WINDOW:
            caps.append(jnp.where(blk >= 0, -NEG, NEG))
        else:
            diff = off + q_i - k_i
            ok = (diff >= 0) & (diff < WINDOW) & (blk >= 0)
            caps.append(jnp.concatenate([jnp.where(ok, -NEG, NEG)] * NSA_HPG, axis=1))
    k_blocks = [kwn_ref[pl.ds(pl.multiple_of(blks[i] * Q_BLOCK, Q_BLOCK), Q_BLOCK), :]
                for i in range(N_BAND + 1)]
    s_w = [[None] * (N_BAND + 1) for g in groups]
    for i in reversed(range(N_BAND + 1)):
        for g in groups:
            s_w[g][i] = _dot_nt(k_blocks[i], qgs[g])
    m_w = [None for g in groups]
    o_win = [None for g in groups]
    for i in reversed(range(N_BAND + 1)):
        for g in groups:
            sm = jnp.minimum(s_w[g][i], caps[i])
            m_blk = jnp.max(sm, axis=0, keepdims=True)
            m_new = m_blk if m_w[g] is None else jnp.maximum(m_w[g], m_blk)
            pv = _dot(vwnt_ref[g, blks[i]], jnp.exp2((sm - m_new).astype(BF16)))
            o_win[g] = pv if m_w[g] is None else jnp.exp2(m_w[g] - m_new) * o_win[g] + pv
            m_w[g] = m_new
    for g in groups:
        owin_s[g] = o_win[g][0:HEAD_DIM, :] * (1.0 / o_win[g][HEAD_DIM:HEAD_DIM + 1, :])


def _nsa_call(qpad, ksl, kwn, vslt, vwnt, kcmp, vcmpt, gt, ot, batch, seq):
    nqb = seq // Q_BLOCK
    n_g = NSA_KV_GROUPS
    steps = nqb // NSA_QB_PER_STEP
    q_rows = NSA_QB_PER_STEP * Q_BLOCK
    qrow = lambda b, c: (b * steps + c, 0)
    brow = lambda b, c: (b, 0)
    c2 = lambda b, c: (0, 0)
    return pl.pallas_call(
        _nsa_kernel,
        grid=(batch, steps),
        in_specs=[
            pl.BlockSpec((q_rows, NSA_HEADS * LANES), qrow),
            pl.BlockSpec((seq, KV_W), brow),
            pl.BlockSpec((seq, KV_W), brow),
            pl.BlockSpec((n_g, seq // SEL_CHUNK, V_ROWS, SEL_CHUNK), lambda b, c: (b, 0, 0, 0)),
            pl.BlockSpec((n_g, nqb, V_ROWS, Q_BLOCK), lambda b, c: (b, 0, 0, 0)),
            pl.BlockSpec((N_A, KV_W), brow),
            pl.BlockSpec((KV_W, N_A), brow),
            pl.BlockSpec((LANES, q_rows), lambda b, c: (0, b * steps + c)),
            pl.BlockSpec((N_SEL, N_A), c2),
        ],
        out_specs=pl.BlockSpec((q_rows, Q_W), qrow),
        out_shape=jax.ShapeDtypeStruct((batch * seq, Q_W), BF16),
        scratch_shapes=[
            pltpu.VMEM((n_g, 1, HQ), F32),
            pltpu.VMEM((n_g, V_ROWS, HQ), F32),
            pltpu.VMEM((n_g, HEAD_DIM, HQ), F32),
            pltpu.VMEM((n_g, HEAD_DIM, HQ), F32),
            pltpu.VMEM((n_g, N_SEL, Q_BLOCK), F32),
            pltpu.VMEM((Q_W, Q_BLOCK), F32),
        ],
        compiler_params=pltpu.CompilerParams(
            dimension_semantics=("parallel", "arbitrary"), vmem_limit_bytes=VMEM_LIMIT),
        name="nsa",
    )(qpad, ksl, kwn, vslt, vwnt, kcmp, vcmpt, gt, ot)


MEM_TM = 1024
MEM_V_ROWS = MEM_HEAD_DIM + 16


def _memattn_kernel(mq_ref, mem_ref, wkv_ref, o_ref, k_s, vt_s, out_s):
    @pl.when(pl.program_id(1) == 0)
    def _():
        kv = _dot(mem_ref[...].astype(BF16), wkv_ref[...].astype(BF16))
        k_s[...] = kv[:, 0:MEM_WIDTH].astype(BF16)
        vt = kv[:, MEM_WIDTH:2 * MEM_WIDTH].T.astype(BF16)
        for h in range(MEM_HEADS):
            vt_s[h, 0:MEM_HEAD_DIM, :] = vt[h * MEM_HEAD_DIM:(h + 1) * MEM_HEAD_DIM, :]
            vt_s[h, MEM_HEAD_DIM:, :] = jnp.ones((MEM_V_ROWS - MEM_HEAD_DIM, vt.shape[1]), BF16)

    heads = range(MEM_HEADS)
    cols = [slice(h * MEM_HEAD_DIM, (h + 1) * MEM_HEAD_DIM) for h in heads]
    s_t = [_dot_nt(k_s[:, cols[h]], mq_ref[:, cols[h]]) for h in heads]
    e = []
    for h in heads:
        m = jnp.max(s_t[h], axis=0, keepdims=True)
        e.append(jnp.exp2((s_t[h] - m) * (MEM_HEAD_DIM ** -0.5 * LOG2E)).astype(BF16))
    o_t = [_dot(vt_s[h], e[h]) for h in heads]
    for h in heads:
        out_s[cols[h], :] = o_t[h][0:MEM_HEAD_DIM, :] * (1.0 / o_t[h][MEM_HEAD_DIM:MEM_HEAD_DIM + 1, :])
    o_ref[...] = out_s[...].T.astype(BF16)


def _memattn_call(mq, mem2, wkv, batch, seq):
    m_len = mem2.shape[0] // batch
    d = mem2.shape[1]
    per = seq // MEM_TM
    return pl.pallas_call(
        _memattn_kernel,
        grid=(batch, per),
        in_specs=[
            pl.BlockSpec((MEM_TM, MEM_WIDTH), lambda b, i: (b * per + i, 0)),
            pl.BlockSpec((m_len, d), lambda b, i: (b, 0)),
            pl.BlockSpec((d, 2 * MEM_WIDTH), lambda b, i: (0, 0)),
        ],
        out_specs=pl.BlockSpec((MEM_TM, MEM_WIDTH), lambda b, i: (b * per + i, 0)),
        out_shape=jax.ShapeDtypeStruct((batch * seq, MEM_WIDTH), BF16),
        scratch_shapes=[
            pltpu.VMEM((m_len, MEM_WIDTH), BF16),
            pltpu.VMEM((MEM_HEADS, MEM_V_ROWS, m_len), BF16),
            pltpu.VMEM((MEM_WIDTH, MEM_TM), F32),
        ],
        compiler_params=pltpu.CompilerParams(
            dimension_semantics=("parallel", "arbitrary"), vmem_limit_bytes=VMEM_LIMIT),
        name="memattn",
    )(mq, mem2, wkv)


MERGE_TM = 512


def _merge_kernel(x_ref, onsa_ref, osgu_ref, omem_ref, wgt_ref, wbn32_ref, wbs32_ref, wbm32_ref, wo32_ref,
                  g_ref, b_ref, out_ref, wbn_ref, wbs_ref, wbm_ref, wo_ref, wg_ref):
    @pl.when(pl.program_id(0) == 0)
    def _():
        for dst, src in ((wbn_ref, wbn32_ref), (wbs_ref, wbs32_ref), (wbm_ref, wbm32_ref),
                         (wo_ref, wo32_ref)):
            dst[...] = src[...].astype(BF16)
        _transpose_weight(wgt_ref, wg_ref)

    d = x_ref.shape[1]
    branches = ((onsa_ref, wbn_ref), (osgu_ref, wbs_ref), (omem_ref, wbm_ref))
    halves = [slice(h * (MERGE_TM // 2), (h + 1) * (MERGE_TM // 2)) for h in range(2)]
    xs = [x_ref[rows, :] for rows in halves]
    xbs = [x.astype(BF16) for x in xs]
    logits = [[_dot(xb, wg_ref[:, br * d:(br + 1) * d]) for br in range(len(branches))] for xb in xbs]
    projs = [[_dot(o_r[rows, :], w_r[...]) for o_r, w_r in branches] for rows in halves]
    ys = []
    for h in range(2):
        y = None
        for br in range(len(branches)):
            term = _sigmoid(logits[h][br]) * projs[h][br]
            y = term if y is None else y + term
        ys.append(y.astype(BF16))
    outs = [_dot(y, wo_ref[...]) for y in ys]
    for h, rows in enumerate(halves):
        out_ref[rows, :] = _layer_norm(DN_ALPHA * xs[h] + outs[h], g_ref[...], b_ref[...])


def _merge_call(x2, onsa, osgu, omem, wg, wbn, wbs, wbm, wo, g1, b1):
    t, d = x2.shape
    row = lambda i: (i, 0)
    c2 = lambda i: (0, 0)
    full = lambda a: pl.BlockSpec(a.shape, c2)
    once = lambda a: pl.BlockSpec(a.shape, c2, pipeline_mode=pl.Buffered(1))
    return pl.pallas_call(
        _merge_kernel,
        grid=(t // MERGE_TM,),
        in_specs=[
            pl.BlockSpec((MERGE_TM, d), row),
            pl.BlockSpec((MERGE_TM, onsa.shape[1]), row),
            pl.BlockSpec((MERGE_TM, osgu.shape[1]), row),
            pl.BlockSpec((MERGE_TM, omem.shape[1]), row),
            pl.BlockSpec((3 * d, d), c2, pipeline_mode=pl.Buffered(1)),
            once(wbn), once(wbs), once(wbm), once(wo), full(g1), full(b1),
        ],
        out_specs=pl.BlockSpec((MERGE_TM, d), row),
        out_shape=jax.ShapeDtypeStruct((t, d), F32),
        scratch_shapes=[pltpu.VMEM(a.shape, BF16) for a in (wbn, wbs, wbm, wo)]
        + [pltpu.VMEM((d, 3 * d), BF16)],
        compiler_params=pltpu.CompilerParams(
            dimension_semantics=("arbitrary",), vmem_limit_bytes=VMEM_LIMIT),
        name="merge",
    )(x2, onsa, osgu, omem, wg, wbn, wbs, wbm, wo, g1, b1)


MOE_TM = 1024
MOE_CH = 144
_YS_ROWS = -(-(MOE_TM + N_GROUPS * MOE_CH) // 256) * 256
_YS_TYPICAL = -(-(2 * N_GROUPS * MOE_CH) // 256) * 256
_R_SLOT = 8
_DEST_LANE = 3 * _R_SLOT
_HID = EXPERTS_PER_GROUP * EXPERT_FF


def _moe_kernel(x_ref, tri_ref, wr_ref, br_ref, wg_ref, wu_ref, wd_ref, g_ref, b_ref, out_ref,
                xa_s, tok_s, keyr_s, ys_s, cnt_s, base_s):
    grp = pl.program_id(1)
    tm = x_ref.shape[0]
    d = x_ref.shape[1]

    @pl.when(grp == 0)
    def _route():
        xb = x_ref[...].astype(BF16)
        xa_s[:, 0:d] = xb
        lt = _dot_nt(wr_ref[...], xb) + br_ref[...]
        row = lax.broadcasted_iota(jnp.int32, (_R_SLOT, tm), 0)
        gl = jnp.where(row < N_GROUPS, lt[0:_R_SLOT], NEG)
        gmax = jnp.max(gl, axis=0, keepdims=True)
        gidx = jnp.min(jnp.where(gl == gmax, row, _R_SLOT), axis=0, keepdims=True)
        gprob = 1.0 / jnp.sum(jnp.exp(gl - gmax), axis=0, keepdims=True)
        el = lt[_R_SLOT:2 * _R_SLOT]
        for k in range(1, N_GROUPS):
            el = jnp.where(gidx == k, lt[(k + 1) * _R_SLOT:(k + 2) * _R_SLOT], el)
        ee = jnp.exp(el - jnp.max(el, axis=0, keepdims=True))
        ep = ee / jnp.sum(ee, axis=0, keepdims=True)
        t1 = jnp.max(ep, axis=0, keepdims=True)
        i1 = jnp.min(jnp.where(ep == t1, row, _R_SLOT), axis=0, keepdims=True)
        rest = row != i1
        t2 = jnp.max(jnp.where(rest, ep, -1.0), axis=0, keepdims=True)
        i2 = jnp.min(jnp.where(rest & (ep == t2), row, _R_SLOT), axis=0, keepdims=True)
        den = t1 + t2
        cwf = (jnp.where(row == i1, t1 / den, 0.0) + jnp.where(row == i2, t2 / den, 0.0)) * gprob
        hi = cwf.astype(BF16).astype(F32)
        r1 = cwf - hi
        lo = r1.astype(BF16).astype(F32)
        lo2 = (r1 - lo).astype(BF16).astype(F32)

        onehot = jnp.where(row == gidx, 1.0, 0.0)
        tri = tri_ref[...]
        onehot_b = onehot.astype(BF16)
        ranks, before = [], jnp.zeros((_R_SLOT, 1), F32)
        for b in range(tm // LANES):
            blk = slice(b * LANES, (b + 1) * LANES)
            ranks.append(_dot_nt(onehot_b[:, blk], tri) + before)
            before = before + jnp.sum(onehot[:, blk], axis=1, keepdims=True)
        rank = jnp.concatenate(ranks, axis=1)
        keyr_s[...] = jnp.where(onehot > 0.5, rank, -1.0)
        base = jnp.int32(0)
        basev = jnp.zeros((1, tm), F32)
        for k in range(N_GROUPS):
            n_k = jnp.sum(onehot[k:k + 1, :]).astype(jnp.int32)
            cnt_s[k] = n_k
            base_s[k] = base
            basev = jnp.where(gidx == k, base.astype(F32), basev)
            base = base + ((n_k + MOE_CH - 1) // MOE_CH) * MOE_CH
        base_s[N_GROUPS] = base
        dest =jnp.sum(rank * onehot, axis=0, keepdims=True) + basev
        tok = jnp.concatenate(
            [hi, lo, lo2, jnp.broadcast_to(dest, (_R_SLOT, tm)),
             jnp.zeros((LANES - 4 * _R_SLOT, tm), F32)], axis=0).T
        tok_s[...] = tok
        lane = lax.broadcasted_iota(jnp.int32, tok.shape, 1)
        xa_s[:, d:d + LANES] = jnp.where(lane < _DEST_LANE, tok, 0.0).astype(BF16)
        ys_s[...] = jnp.zeros(ys_s.shape, BF16)

    n_rows = cnt_s[grp]
    row0 = base_s[grp]
    keyr = keyr_s[pl.ds(grp, 1), :]
    half = _HID // 2

    ch = MOE_CH

    def gather(k):
        r_row = (lax.broadcasted_iota(jnp.int32, (ch, tm), 0) + k * MOE_CH).astype(F32)
        pick = jnp.where(keyr == r_row, 1.0, 0.0).astype(BF16)
        ga = _dot(pick, xa_s[...])
        xg = ga[:, 0:d].astype(BF16)
        cwg = ga[:, d:d + LANES]
        cs = (cwg + pltpu.roll(cwg, LANES - _R_SLOT, 1)
              + pltpu.roll(cwg, LANES - 2 * _R_SLOT, 1))
        return xg, cs

    def experts_ffn(k, xg, cs):
        experts = range(EXPERTS_PER_GROUP)
        per_half = EXPERTS_PER_GROUP // 2
        hg = [_dot(xg, wg_ref[0, e]) for e in experts]
        hu = [_dot(xg, wu_ref[0, e]) for e in experts]
        y = None
        for h in range(2):
            hid = jnp.concatenate(
                [(((hg[e] * _sigmoid(hg[e])) * hu[e]) * cs[:, e:e + 1]).astype(BF16)
                 for e in range(h * per_half, (h + 1) * per_half)], axis=1)
            term = _dot(hid, wd_ref[0, h * half:(h + 1) * half, :])
            y = term if y is None else y + term
        ys_s[pl.ds(pl.multiple_of(row0 + k * MOE_CH, 16), ch), :] = y.astype(BF16)

    def sweep(k, carry):
        experts_ffn(k, *gather(k))
        return carry

    n_sweeps = (n_rows + MOE_CH - 1) // MOE_CH

    @pl.when(n_sweeps == 2)
    def _():
        g0 = gather(0)
        g1 = gather(1)
        experts_ffn(0, *g0)
        experts_ffn(1, *g1)

    @pl.when(n_sweeps != 2)
    def _():
        lax.fori_loop(0, n_sweeps, sweep, 0)

    def fin(k_rows):
        n_q = 4
        q_rows = tm // n_q
        r_col = lax.broadcasted_iota(jnp.int32, (q_rows, k_rows), 1).astype(F32)
        parts = [slice(i * q_rows, (i + 1) * q_rows) for i in range(n_q)]
        puts = [jnp.where(tok_s[rows, _DEST_LANE:_DEST_LANE + 1] == r_col, 1.0, 0.0).astype(BF16)
                for rows in parts]
        fs = [_dot(put, ys_s[0:k_rows, :]) for put in puts]
        for rows, f in zip(parts, fs):
            out_ref[rows, :] = _layer_norm(DN_ALPHA * x_ref[rows, :] + f, g_ref[...], b_ref[...])

    last = grp == N_GROUPS - 1
    used = base_s[N_GROUPS]
    pl.when(last & (used <= _YS_TYPICAL))(functools.partial(fin, _YS_TYPICAL))
    pl.when(last & (used > _YS_TYPICAL))(functools.partial(fin, _YS_ROWS))


def _moe_call(x1, wr, br, wg, wu, wd, g2, b2):
    t, d = x1.shape
    row = lambda i, g: (i, 0)
    c2 = lambda i, g: (0, 0)
    idx = np.arange(LANES)
    tri = jnp.asarray(idx[None, :] < idx[:, None], dtype=BF16)
    return pl.pallas_call(
        _moe_kernel,
        grid=(t // MOE_TM, N_GROUPS),
        in_specs=[
            pl.BlockSpec((MOE_TM, d), row),
            pl.BlockSpec((LANES, LANES), c2),
            pl.BlockSpec(wr.shape, c2),
            pl.BlockSpec(br.shape, c2),
            pl.BlockSpec((1, EXPERTS_PER_GROUP, d, EXPERT_FF), lambda i, g: (g, 0, 0, 0)),
            pl.BlockSpec((1, EXPERTS_PER_GROUP, d, EXPERT_FF), lambda i, g: (g, 0, 0, 0)),
            pl.BlockSpec((1, _HID, d), lambda i, g: (g, 0, 0)),
            pl.BlockSpec(g2.shape, c2),
            pl.BlockSpec(b2.shape, c2),
        ],
        out_specs=pl.BlockSpec((MOE_TM, d), row),
        out_shape=jax.ShapeDtypeStruct((t, d), F32),
        scratch_shapes=[
            pltpu.VMEM((MOE_TM, d + LANES), BF16),
            pltpu.VMEM((MOE_TM, LANES), F32),
            pltpu.VMEM((_R_SLOT, MOE_TM), F32),
            pltpu.VMEM((_YS_ROWS, d), BF16),
            pltpu.SMEM((N_GROUPS,), jnp.int32),
            pltpu.SMEM((N_GROUPS + 1,), jnp.int32),
        ],
        compiler_params=pltpu.CompilerParams(
            dimension_semantics=("parallel", "arbitrary"), vmem_limit_bytes=MOE_VMEM_LIMIT),
        name="moe",
    )(x1, tri, wr, br, wg, wu, wd, g2, b2)


def _rope_table(seq):
    half = ROPE_DIM // 2
    inv = ROPE_THETA ** (-jnp.arange(0, ROPE_DIM, 2, dtype=F32) / ROPE_DIM)
    ang = jnp.arange(seq, dtype=F32)[:, None] * inv[None, :]
    cos, sin = jnp.cos(ang), jnp.sin(ang)
    rest = HEAD_DIM - ROPE_DIM
    one = jnp.ones((seq, rest), F32)
    zero = jnp.zeros((seq, rest), F32)
    zh = jnp.zeros((seq, half), F32)
    c_h = jnp.concatenate([cos, cos, one], axis=1)
    dn_h = jnp.concatenate([-sin, zh, zero], axis=1)
    up_h = jnp.concatenate([zh, sin, zero], axis=1)
    rep = LANES // HEAD_DIM
    return jnp.concatenate([jnp.tile(c_h, (1, rep)), jnp.tile(dn_h, (1, rep)), jnp.tile(up_h, (1, rep))],
                           axis=1)


def _expand_cmp_weights(pe, w1, w2):
    g_n = NSA_KV_GROUPS
    w1r = w1.astype(BF16).reshape(2, CMP_STRIDE, HEAD_DIM, CMP_HIDDEN)
    w2b = w2.astype(BF16)
    w1e = jnp.concatenate(
        [jnp.stack([w1r if g == e else jnp.zeros_like(w1r) for g in range(g_n)], axis=2)
         .reshape(2, A_W, CMP_HIDDEN) for e in range(g_n)], axis=2)
    w2e = jnp.concatenate(
        [jnp.concatenate([w2b if g == e else jnp.zeros_like(w2b) for e in range(g_n)], axis=1)
         for g in range(g_n)], axis=0)
    per = pe.reshape(2, CMP_STRIDE, 1, HEAD_DIM)
    pee = jnp.broadcast_to(per, (2, CMP_STRIDE, g_n, HEAD_DIM)).reshape(2, A_W)
    pee = jnp.concatenate([pee, jnp.zeros((6, A_W), pe.dtype)], axis=0)
    return pee.astype(F32), w1e.astype(BF16), w2e.astype(BF16)


def kernel(x, mem, w_in, cmp_pe_k, cmp_w1_k, cmp_w2_k, cmp_pe_v, cmp_w1_v, cmp_w2_v, sgu_ln_g, sgu_ln_b,
           sgu_w_s, sgu_b_s, w_mem_kv, w_br_nsa, w_br_sgu, w_br_mem, w_o, ln1_g, ln1_b, w_router_group,
           b_router_group, w_router_expert, b_router_expert, w_exp_gate, w_exp_up, w_exp_down, ln2_g, ln2_b):
    batch, seq, d = x.shape
    t = batch * seq
    assert w_in.shape[0] == DEPTH == 1
    assert seq % PROJ_TM == 0 and seq // CMP_STRIDE == N_A and seq // SLC_BLOCK == N_SEL

    offs = [int(v) for v in np.cumsum(
        [0, Q_W, KV_W, KV_W, KV_W, KV_W, KV_W, KV_W, GATE_W, 2 * SGU_WIDTH, MEM_WIDTH, 3 * d])]
    assert offs[-1] == w_in.shape[2]
    w_t = jnp.transpose(w_in[0])
    seg = lambda i: w_t[offs[i]:offs[i + 1]]
    gate_rows = jnp.pad(seg(7), ((0, LANES - GATE_W), (0, 0)))
    wp_t = jnp.concatenate([seg(0), seg(1), seg(3), seg(5), seg(2), seg(4), seg(6), gate_rows, seg(8), seg(9)],
                           axis=0).astype(BF16)
    w_mt = seg(10).astype(BF16)
    assert wp_t.shape[0] == _PROJ_COLS
    rope = _rope_table(seq)
    tril = jnp.tril(jnp.ones((SGU_CHUNK, SGU_CHUNK), dtype=bool))
    ws = jnp.where(tril[None], sgu_w_s[0], 0.0).astype(BF16)
    bs = jnp.repeat(sgu_b_s[0].T, SGU_WIDTH // SGU_GROUPS, axis=1)

    x2 = x.reshape(t, d)
    (qpad, kc, ksl, kwn, vc, vslt, vwnt, gt, osgu, mq) = _proj_call(
        x2, wp_t, rope, sgu_ln_g[0][None], sgu_ln_b[0][None], ws, bs, seq)

    pek, w1k, w2k = _expand_cmp_weights(cmp_pe_k[0], cmp_w1_k[0], cmp_w2_k[0])
    pev, w1v, w2v = _expand_cmp_weights(cmp_pe_v[0], cmp_w1_v[0], cmp_w2_v[0])
    kcmp, vcmp = _compress_call(kc, vc, pek, pev, w1k, w2k, w1v, w2v, batch)

    vcmpt = vcmp.reshape(batch, N_A, KV_W).transpose(0, 2, 1).reshape(batch * KV_W, N_A)
    ci = np.arange(N_A)
    sj = np.arange(N_SEL)
    overlap = ((ci[None, :] * CMP_STRIDE + CMP_LEN - 1 >= sj[:, None] * SLC_BLOCK)
               & (ci[None, :] * CMP_STRIDE <= sj[:, None] * SLC_BLOCK + SLC_BLOCK - 1)
               & (ci[None, :] < (seq - CMP_LEN) // CMP_STRIDE + 1))
    ot = jnp.asarray(overlap, dtype=BF16)
    onsa = _nsa_call(qpad, ksl, kwn, vslt, vwnt, kcmp, vcmpt, gt, ot, batch, seq)

    omem = _memattn_call(mq, mem.reshape(batch * mem.shape[1], d), w_mem_kv[0], batch, seq)

    x1 = _merge_call(x2, onsa, osgu, omem, w_mt, w_br_nsa[0], w_br_sgu[0], w_br_mem[0], w_o[0],
                     ln1_g[0][None], ln1_b[0][None])

    assert EXPERTS_PER_GROUP == _R_SLOT and N_GROUPS <= _R_SLOT
    n_r = _R_SLOT + N_GROUPS * EXPERTS_PER_GROUP
    wr = jnp.concatenate([jnp.pad(w_router_group[0], ((0, 0), (0, _R_SLOT - N_GROUPS))),
                          w_router_expert[0]], axis=1)
    wr = jnp.pad(wr, ((0, 0), (0, LANES - n_r))).astype(BF16).T
    br = jnp.concatenate([jnp.pad(b_router_group[0], (0, _R_SLOT - N_GROUPS)), b_router_expert[0]])
    br = jnp.pad(br, (0, LANES - n_r))[:, None]
    wg = w_exp_gate[0].astype(BF16)
    wu = w_exp_up[0].astype(BF16)
    wd = w_exp_down[0].reshape(N_GROUPS, _HID, d).astype(BF16)
    out = _moe_call(x1, wr, br, wg, wu, wd, ln2_g[0][None], ln2_b[0][None])
    return out.reshape(batch, seq, d)
```

```python
import functools

import numpy as np
import jax
import jax.numpy as jnp
from jax import lax
from jax.experimental import pallas as pl
from jax.experimental.pallas import tpu as pltpu

NSA_HEADS = 8
NSA_KV_GROUPS = 2
NSA_HPG = NSA_HEADS // NSA_KV_GROUPS
HEAD_DIM = 64
CMP_LEN = 32
CMP_STRIDE = 16
CMP_HIDDEN = 256
SLC_BLOCK = 64
SLC_TOPN = 8
WINDOW = 512
Q_BLOCK = 128
N_BAND = WINDOW // Q_BLOCK
ROPE_THETA = 500000.0
ROPE_DIM = HEAD_DIM // 4
SGU_CHUNK = 128
SGU_GROUPS = 8
SGU_WIDTH = 512
MEM_HEADS = 4
MEM_HEAD_DIM = 128
MEM_WIDTH = MEM_HEADS * MEM_HEAD_DIM
N_GROUPS = 4
EXPERTS_PER_GROUP = 8
EXPERT_FF = 256
DEPTH = 1
DN_ALPHA = (2.0 * DEPTH) ** 0.25
LN_EPS = 1e-5
NEG = -1e30
LOG2E = 1.4426950408889634

LANES = 128
Q_W = NSA_HEADS * HEAD_DIM
KV_W = NSA_KV_GROUPS * HEAD_DIM
GATE_W = NSA_HEADS * 3
VMEM_LIMIT = 56 * 1024 * 1024
MOE_VMEM_LIMIT = 60 * 1024 * 1024

BF16 = jnp.bfloat16
F32 = jnp.float32


def _dot(a, b):
    return jnp.dot(a, b, preferred_element_type=F32)


def _dot_nt(a, b):
    return lax.dot_general(a, b, (((1,), (1,)), ((), ())), preferred_element_type=F32)


def _sigmoid(x):
    return 1.0 / (1.0 + jnp.exp(-x))


def _gelu(x):
    return 0.5 * x * (1.0 + lax.erf(x * (2.0 ** -0.5)))


def _layer_norm(x, g, b):
    mu = jnp.mean(x, axis=-1, keepdims=True)
    xc = x - mu
    var = jnp.mean(xc * xc, axis=-1, keepdims=True)
    return xc * lax.rsqrt(var + LN_EPS) * g + b


PROJ_TM = 512
_ROPE_COLS = Q_W + 3 * KV_W
_V_OFF = _ROPE_COLS
_G_OFF = _V_OFF + 3 * KV_W
_SGU_OFF = _G_OFF + LANES
_MQ_OFF = _SGU_OFF + 2 * SGU_WIDTH
_PROJ_COLS = _MQ_OFF + MEM_WIDTH


def _store_strided_rows(val, out_ref, tmp_ref):
    tmp_ref[...] = val
    n = val.shape[0] // CMP_STRIDE
    for l in range(CMP_STRIDE):
        out_ref[:, l * LANES:(l + 1) * LANES] = tmp_ref[pl.ds(l, n, stride=CMP_STRIDE), :].astype(BF16)


def _transpose_weight(wt_ref, w_s):
    for j in range(wt_ref.shape[0] // LANES):
        rows = slice(j * LANES, (j + 1) * LANES)
        w_s[:, rows] = wt_ref[rows, :].astype(F32).T.astype(BF16)


def _proj_kernel(x_ref, wt_ref, rope_ref, lng_ref, lnb_ref, ws_ref, bs_ref,
                 q_ref, kc_ref, ksl_ref, kwn_ref, vc_ref, vsl_ref, vwn_ref, gate_ref, osgu_ref, mq_ref,
                 a_s, w_ref):
    pl.when(pl.program_id(0) == 0)(functools.partial(_transpose_weight, wt_ref, w_ref))
    xb = x_ref[...].astype(BF16)
    lane = lax.broadcasted_iota(jnp.int32, (PROJ_TM, LANES), 1)
    low = lane < HEAD_DIM
    cos = rope_ref[:, 0:LANES]
    s_dn = rope_ref[:, LANES:2 * LANES]
    s_up = rope_ref[:, 2 * LANES:3 * LANES]

    hz = _dot(xb, w_ref[:, _SGU_OFF:_MQ_OFF])
    h = _dot(xb, w_ref[:, 0:_ROPE_COLS])
    k_refs = (kc_ref, ksl_ref, kwn_ref)
    for j in range(_ROPE_COLS // LANES):
        blk = h[:, j * LANES:(j + 1) * LANES]
        r = (blk * cos + pltpu.roll(blk, LANES - ROPE_DIM // 2, 1) * s_dn
             + pltpu.roll(blk, ROPE_DIM // 2, 1) * s_up)
        if j < Q_W // LANES:
            r = r * (HEAD_DIM ** -0.5 * LOG2E)
            sw = pltpu.roll(r, HEAD_DIM, 1)
            g = (2 * j) // NSA_HPG
            if g == 0:
                h0 = jnp.where(low, r, 0.0)
                h1 = jnp.where(low, sw, 0.0)
            else:
                h0 = jnp.where(low, 0.0, sw)
                h1 = jnp.where(low, 0.0, r)
            q_ref[:, (2 * j) * LANES:(2 * j + 1) * LANES] = h0.astype(BF16)
            q_ref[:, (2 * j + 1) * LANES:(2 * j + 2) * LANES] = h1.astype(BF16)
        else:
            if j == Q_W // LANES:
                _store_strided_rows(r, kc_ref, a_s)
            else:
                k_refs[j - Q_W // LANES][...] = r.astype(BF16)

    _store_strided_rows(_dot(xb, w_ref[:, _V_OFF:_V_OFF + LANES]), vc_ref, a_s)
    ones = jnp.ones((V_ROWS - HEAD_DIM, PROJ_TM), BF16)
    for v_ref, row0, blk in ((vsl_ref, _V_OFF + LANES, SEL_CHUNK), (vwn_ref, _V_OFF + 2 * LANES, Q_BLOCK)):
        vt = _dot_nt(wt_ref[row0:row0 + LANES, :], xb).astype(BF16)
        for g in range(NSA_KV_GROUPS):
            for b in range(PROJ_TM // blk):
                v_ref[g, b, 0:HEAD_DIM, :] = vt[g * HEAD_DIM:(g + 1) * HEAD_DIM, b * blk:(b + 1) * blk]
                v_ref[g, b, HEAD_DIM:V_ROWS, :] = ones[:, 0:blk]

    gate_ref[...] = _sigmoid(_dot_nt(wt_ref[_G_OFF:_SGU_OFF, :], xb))

    mq_ref[...] = _dot(xb, w_ref[:, _MQ_OFF:_PROJ_COLS]).astype(BF16)

    z = _gelu(hz)
    u = z[:, 0:SGU_WIDTH]
    v = _layer_norm(z[:, SGU_WIDTH:2 * SGU_WIDTH], lng_ref[...], lnb_ref[...]).astype(BF16)
    lane_c = lax.broadcasted_iota(jnp.int32, (SGU_CHUNK, LANES), 1)
    low_c = lane_c < (SGU_WIDTH // SGU_GROUPS)
    n_ch = PROJ_TM // SGU_CHUNK
    for gp in range(SGU_WIDTH // LANES):
        cols = slice(gp * LANES, (gp + 1) * LANES)
        vcat = jnp.concatenate([v[ci * SGU_CHUNK:(ci + 1) * SGU_CHUNK, cols] for ci in range(n_ch)], axis=1)
        sv0 = _dot(ws_ref[2 * gp], vcat)
        sv1 = _dot(ws_ref[2 * gp + 1], vcat)
        for ci in range(n_ch):
            rows = slice(ci * SGU_CHUNK, (ci + 1) * SGU_CHUNK)
            lanes = slice(ci * LANES, (ci + 1) * LANES)
            sv = jnp.where(low_c, sv0[:, lanes], sv1[:, lanes]) + bs_ref[:, cols]
            osgu_ref[rows, cols] = (u[rows, cols] * sv).astype(BF16)


def _proj_call(x2, wp, rope, lng, lnb, ws, bs, seq):
    t = x2.shape[0]
    d = x2.shape[1]
    nt = t // PROJ_TM
    per_seq = seq // PROJ_TM
    row = lambda i: (i, 0)
    const2 = lambda i: (0, 0)
    assert PROJ_TM % SEL_CHUNK == 0 and PROJ_TM % Q_BLOCK == 0
    n_g = NSA_KV_GROUPS
    batch = t // seq
    vblock = lambda i: (i // per_seq, i % per_seq, 0, 0)

    def v_shape(blk):
        return jax.ShapeDtypeStruct((batch * n_g, seq // blk, V_ROWS, blk), BF16)

    def v_spec(blk):
        return pl.BlockSpec((n_g, PROJ_TM // blk, V_ROWS, blk), vblock)

    a_shape = jax.ShapeDtypeStruct((t // CMP_STRIDE, CMP_STRIDE * KV_W), BF16)
    a_spec = pl.BlockSpec((PROJ_TM // CMP_STRIDE, CMP_STRIDE * KV_W), row)
    k_shape = jax.ShapeDtypeStruct((t, KV_W), BF16)
    k_spec = pl.BlockSpec((PROJ_TM, KV_W), row)
    out_shapes = (
        jax.ShapeDtypeStruct((t, NSA_HEADS * LANES), BF16),
        a_shape, k_shape, k_shape, a_shape,
        v_shape(SEL_CHUNK), v_shape(Q_BLOCK),
        jax.ShapeDtypeStruct((LANES, t), F32),
        jax.ShapeDtypeStruct((t, SGU_WIDTH), BF16),
        jax.ShapeDtypeStruct((t, MEM_WIDTH), BF16),
    )
    out_specs = (
        pl.BlockSpec((PROJ_TM, NSA_HEADS * LANES), row),
        a_spec, k_spec, k_spec, a_spec,
        v_spec(SEL_CHUNK), v_spec(Q_BLOCK),
        pl.BlockSpec((LANES, PROJ_TM), lambda i: (0, i)),
        pl.BlockSpec((PROJ_TM, SGU_WIDTH), row),
        pl.BlockSpec((PROJ_TM, MEM_WIDTH), row),
    )
    return pl.pallas_call(
        _proj_kernel,
        grid=(nt,),
        in_specs=[
            pl.BlockSpec((PROJ_TM, d), row),
            pl.BlockSpec((_PROJ_COLS, d), const2),
            pl.BlockSpec((PROJ_TM, 3 * LANES), lambda i: (i % per_seq, 0)),
            pl.BlockSpec((1, SGU_WIDTH), const2),
            pl.BlockSpec((1, SGU_WIDTH), const2),
            pl.BlockSpec((SGU_GROUPS, SGU_CHUNK, SGU_CHUNK), lambda i: (0, 0, 0)),
            pl.BlockSpec((SGU_CHUNK, SGU_WIDTH), const2),
        ],
        out_specs=out_specs,
        out_shape=out_shapes,
        scratch_shapes=[pltpu.VMEM((PROJ_TM, KV_W), F32), pltpu.VMEM((d, _PROJ_COLS), BF16)],
        compiler_params=pltpu.CompilerParams(
            dimension_semantics=("arbitrary",), vmem_limit_bytes=VMEM_LIMIT),
        name="proj",
    )(x2, wp, rope, lng, lnb, ws, bs)


N_A = 128
A_W = CMP_STRIDE * KV_W


def _compress_kernel(ka_ref, va_ref, pek_ref, pev_ref, w1k_ref, w2k_ref, w1v_ref, w2v_ref,
                     kcmp_ref, vcmp_ref):
    sides = ((ka_ref, pek_ref, w1k_ref, w2k_ref, kcmp_ref), (va_ref, pev_ref, w1v_ref, w2v_ref, vcmp_ref))
    hs = []
    for a_ref, pe_ref, w1_ref, _, _ in sides:
        a = a_ref[...].astype(F32)
        top = (a + pe_ref[0:1, :]).astype(BF16)
        bot = (a + pe_ref[1:2, :]).astype(BF16)
        hs.append((_dot(top, w1_ref[0]), _dot(bot, w1_ref[1])))
    acts = [_gelu(h1 + pltpu.roll(h2, N_A - 1, 0)).astype(BF16) for h1, h2 in hs]
    for act, (_, _, _, w2_ref, out_ref) in zip(acts, sides):
        out_ref[...] = _dot(act, w2_ref[...]).astype(BF16)


def _compress_call(ka, va, pek, pev, w1k, w2k, w1v, w2v, batch):
    row = lambda b: (b, 0)
    c2 = lambda b: (0, 0)
    c3 = lambda b: (0, 0, 0)
    hid2 = NSA_KV_GROUPS * CMP_HIDDEN
    return pl.pallas_call(
        _compress_kernel,
        grid=(batch,),
        in_specs=[
            pl.BlockSpec((N_A, A_W), row),
            pl.BlockSpec((N_A, A_W), row),
            pl.BlockSpec((8, A_W), c2),
            pl.BlockSpec((8, A_W), c2),
            pl.BlockSpec((2, A_W, hid2), c3),
            pl.BlockSpec((hid2, KV_W), c2),
            pl.BlockSpec((2, A_W, hid2), c3),
            pl.BlockSpec((hid2, KV_W), c2),
        ],
        out_specs=(pl.BlockSpec((N_A, KV_W), row), pl.BlockSpec((N_A, KV_W), row)),
        out_shape=(jax.ShapeDtypeStruct((batch * N_A, KV_W), BF16),
                   jax.ShapeDtypeStruct((batch * N_A, KV_W), BF16)),
        compiler_params=pltpu.CompilerParams(
            dimension_semantics=("parallel",), vmem_limit_bytes=VMEM_LIMIT),
        name="compress",
    )(ka, va, pek, pev, w1k, w2k, w1v, w2v)


N_SEL = 32
SEL_CHUNK = 256
SEL_SUB = 128
HQ = NSA_HPG * Q_BLOCK
V_ROWS = HEAD_DIM + 16


def _nsa_kernel(*refs):
    ksl_ref = refs[1]
    reach = pl.program_id(1) // (SEL_CHUNK // Q_BLOCK)
    for n in range(ksl_ref.shape[0] // SEL_CHUNK):
        pl.when(reach == n)(functools.partial(_nsa_body, n, *refs))


def _nsa_body(n_later, q_ref, ksl_ref, kwn_ref, vslt_ref, vwnt_ref, kcmp_ref, vcmpt_ref, gt_ref,
              ot_ref, o_ref, m_s, acc_s, ocmp_s, owin_s, cap_s, out_s):
    c = pl.program_id(1)
    lane_hq = lax.broadcasted_iota(jnp.int32, (1, HQ), 1)
    pos_hq = c * Q_BLOCK + (lane_hq & (Q_BLOCK - 1))
    pos_q = c * Q_BLOCK + lax.broadcasted_iota(jnp.int32, (1, Q_BLOCK), 1)
    groups = range(NSA_KV_GROUPS)
    qgs = [jnp.concatenate(
        [q_ref[:, (g * NSA_HPG + hh) * LANES:(g * NSA_HPG + hh + 1) * LANES]
         for hh in range(NSA_HPG)], axis=0) for g in groups]

    kcmp = kcmp_ref[...]
    s_c = [_dot_nt(kcmp, qgs[g]) for g in groups]
    n_idx = lax.broadcasted_iota(jnp.int32, (N_A, HQ), 0)
    valid_c = (n_idx * CMP_STRIDE + (CMP_LEN - 1)) <= pos_hq
    p_c = []
    for g in groups:
        sm_c = jnp.where(valid_c, s_c[g], NEG)
        m_c = jnp.max(sm_c, axis=0, keepdims=True)
        e_c = jnp.where(valid_c, jnp.exp2(sm_c - m_c), 0.0)
        d_c = jnp.sum(e_c, axis=0, keepdims=True)
        p_c.append(e_c / jnp.where(d_c > 0, d_c, 1.0))
    for g in groups:
        ocmp_s[g] = _dot(vcmpt_ref[g * HEAD_DIM:(g + 1) * HEAD_DIM, :], p_c[g].astype(BF16))

    ot = ot_ref[...]
    imp = []
    for g in groups:
        ps = (p_c[g][:, 0:Q_BLOCK] + p_c[g][:, Q_BLOCK:2 * Q_BLOCK]
              + p_c[g][:, 2 * Q_BLOCK:3 * Q_BLOCK] + p_c[g][:, 3 * Q_BLOCK:4 * Q_BLOCK])
        p_hi = ps.astype(BF16)
        r1 = ps - p_hi.astype(F32)
        p_lo = r1.astype(BF16)
        p_lo2 = (r1 - p_lo.astype(F32)).astype(BF16)
        imp.append(_dot(ot, p_hi) + _dot(ot, p_lo) + _dot(ot, p_lo2))

    j_idx = lax.broadcasted_iota(jnp.int32, (N_SEL, Q_BLOCK), 0)
    cur = pos_q // SLC_BLOCK
    future = j_idx > cur
    forced = (j_idx == 0) | (j_idx == cur) | (j_idx == cur - 1)
    imp = [jnp.where(future, NEG, jnp.where(forced, -NEG, imp[g])) for g in groups]
    rank = [jnp.zeros((N_SEL, Q_BLOCK), F32) for g in groups]
    for i in range(N_SEL):
        for g in groups:
            row = imp[g][i:i + 1, :]
            beats = (row > imp[g]) | ((row == imp[g]) & (j_idx > i))
            rank[g] = rank[g] + jnp.where(beats, 1.0, 0.0)
    for g in groups:
        cap_s[g] = jnp.where(rank[g] < float(SLC_TOPN), -NEG, NEG)
        m_s[g] = jnp.full((1, HQ), NEG, F32)
        acc_s[g] = jnp.zeros((V_ROWS, HQ), F32)

    blocks_per_sub = SEL_SUB // SLC_BLOCK
    subs_per_chunk = SEL_CHUNK // SEL_SUB

    def sel_spans(kc):
        return [slice(kc * SEL_CHUNK + a * SEL_SUB, kc * SEL_CHUNK + (a + 1) * SEL_SUB)
                for a in range(subs_per_chunk)]

    def sel_scores(kc):
        return [[_dot_nt(ksl_ref[ks, :], qgs[g]) for g in groups] for ks in sel_spans(kc)]

    def sel_chunk(kc, s=None):
        spans = sel_spans(kc)
        s = sel_scores(kc) if s is None else s
        for a, ks in enumerate(spans):
            kpos = ks.start + lax.broadcasted_iota(jnp.int32, (SEL_SUB, Q_BLOCK), 0)
            causal = kpos <= pos_q
            for g in groups:
                j0 = ks.start // SLC_BLOCK
                cap = jnp.concatenate(
                    [jnp.broadcast_to(cap_s[g, j:j + 1, :], (SLC_BLOCK, Q_BLOCK))
                     for j in range(j0, j0 + blocks_per_sub)], axis=0)
                cap = jnp.where(causal, cap, NEG)
                sm = jnp.minimum(s[a][g], jnp.concatenate([cap] * NSA_HPG, axis=1))
                m_old = m_s[g]
                m_new = jnp.maximum(m_old, jnp.max(sm, axis=0, keepdims=True))
                alpha = jnp.exp2(m_old - m_new)
                e = jnp.exp2((sm - m_new).astype(BF16))
                v_t = vslt_ref[g, kc, :, a * SEL_SUB:(a + 1) * SEL_SUB]
                acc_s[g] = alpha * acc_s[g] + _dot(v_t, e)
                m_s[g] = m_new

    s0 = sel_scores(0)
    _nsa_window(c, qgs, kwn_ref, vwnt_ref, owin_s)
    s_next = sel_scores(1) if n_later else None
    sel_chunk(0, s0)
    for kc in range(1, n_later + 1):
        s_cur = s_next
        if kc < n_later:
            s_next = sel_scores(kc + 1)
        sel_chunk(kc, s_cur)

    for g in groups:
        acc = acc_s[g]
        o_sel = acc[0:HEAD_DIM, :] * (1.0 / acc[HEAD_DIM:HEAD_DIM + 1, :])

        def gate_row(br):
            return jnp.concatenate(
                [gt_ref[(g * NSA_HPG + hh) * 3 + br:(g * NSA_HPG + hh) * 3 + br + 1, :]
                 for hh in range(NSA_HPG)], axis=1)
        o_t = gate_row(0) * ocmp_s[g] + gate_row(1) * o_sel + gate_row(2) * owin_s[g]
        for hh in range(NSA_HPG):
            h = g * NSA_HPG + hh
            out_s[h * HEAD_DIM:(h + 1) * HEAD_DIM, :] = o_t[:, hh * Q_BLOCK:(hh + 1) * Q_BLOCK]

    o_ref[...] = out_s[...].T.astype(BF16)


def _nsa_window(c, qgs, kwn_ref, vwnt_ref, owin_s):
    groups = range(NSA_KV_GROUPS)
    q_i = lax.broadcasted_iota(jnp.int32, (Q_BLOCK, Q_BLOCK), 1)
    k_i = lax.broadcasted_iota(jnp.int32, (Q_BLOCK, Q_BLOCK), 0)
    blks, caps = [], []
    for i in range(N_BAND + 1):
        blk = c - N_BAND + i
        blks.append(jnp.maximum(blk, 0))
        off = (N_BAND - i) * Q_BLOCK
        if off - (Q_BLOCK - 1) >= 0 and off + (Q_BLOCK - 1) < WINDOW:
            caps.append(jnp.where(blk >= 0, -NEG, NEG))
        else:
            diff = off + q_i - k_i
            ok = (diff >= 0) & (diff < WINDOW) & (blk >= 0)
            caps.append(jnp.concatenate([jnp.where(ok, -NEG, NEG)] * NSA_HPG, axis=1))
    k_blocks = [kwn_ref[pl.ds(pl.multiple_of(blks[i] * Q_BLOCK, Q_BLOCK), Q_BLOCK), :]
                for i in range(N_BAND + 1)]
    s_w = [[None] * (N_BAND + 1) for g in groups]
    for i in reversed(range(N_BAND + 1)):
        for g in groups:
            s_w[g][i] = _dot_nt(k_blocks[i], qgs[g])
    m_w = [None for g in groups]
    o_win = [None for g in groups]
    for i in reversed(range(N_BAND + 1)):
        for g in groups:
            sm = jnp.minimum(s_w[g][i], caps[i])
            m_blk = jnp.max(sm, axis=0, keepdims=True)
            m_new = m_blk if m_w[g] is None else jnp.maximum(m_w[g], m_blk)
            pv = _dot(vwnt_ref[g, blks[i]], jnp.exp2((sm - m_new).astype(BF16)))
            o_win[g] = pv if m_w[g] is None else jnp.exp2(m_w[g] - m_new) * o_win[g] + pv
            m_w[g] = m_new
    for g in groups:
        owin_s[g] = o_win[g][0:HEAD_DIM, :] * (1.0 / o_win[g][HEAD_DIM:HEAD_DIM + 1, :])


def _nsa_call(qpad, ksl, kwn, vslt, vwnt, kcmp, vcmpt, gt, ot, batch, seq):
    nqb = seq // Q_BLOCK
    n_g = NSA_KV_GROUPS
    qrow = lambda b, c: (b * nqb + c, 0)
    brow = lambda b, c: (b, 0)
    c2 = lambda b, c: (0, 0)
    return pl.pallas_call(
        _nsa_kernel,
        grid=(batch, nqb),
        in_specs=[
            pl.BlockSpec((Q_BLOCK, NSA_HEADS * LANES), qrow),
            pl.BlockSpec((seq, KV_W), brow),
            pl.BlockSpec((seq, KV_W), brow),
            pl.BlockSpec((n_g, seq // SEL_CHUNK, V_ROWS, SEL_CHUNK), lambda b, c: (b, 0, 0, 0)),
            pl.BlockSpec((n_g, nqb, V_ROWS, Q_BLOCK), lambda b, c: (b, 0, 0, 0)),
            pl.BlockSpec((N_A, KV_W), brow),
            pl.BlockSpec((KV_W, N_A), brow),
            pl.BlockSpec((LANES, Q_BLOCK), lambda b, c: (0, b * nqb + c)),
            pl.BlockSpec((N_SEL, N_A), c2),
        ],
        out_specs=pl.BlockSpec((Q_BLOCK, Q_W), qrow),
        out_shape=jax.ShapeDtypeStruct((batch * seq, Q_W), BF16),
        scratch_shapes=[
            pltpu.VMEM((n_g, 1, HQ), F32),
            pltpu.VMEM((n_g, V_ROWS, HQ), F32),
            pltpu.VMEM((n_g, HEAD_DIM, HQ), F32),
            pltpu.VMEM((n_g, HEAD_DIM, HQ), F32),
            pltpu.VMEM((n_g, N_SEL, Q_BLOCK), F32),
            pltpu.VMEM((Q_W, Q_BLOCK), F32),
        ],
        compiler_params=pltpu.CompilerParams(
            dimension_semantics=("parallel", "arbitrary"), vmem_limit_bytes=VMEM_LIMIT),
        name="nsa",
    )(qpad, ksl, kwn, vslt, vwnt, kcmp, vcmpt, gt, ot)


MEM_TM = 1024
MEM_V_ROWS = MEM_HEAD_DIM + 16


def _memattn_kernel(mq_ref, mem_ref, wkv_ref, o_ref, k_s, vt_s, out_s):
    @pl.when(pl.program_id(1) == 0)
    def _():
        kv = _dot(mem_ref[...].astype(BF16), wkv_ref[...].astype(BF16))
        k_s[...] = kv[:, 0:MEM_WIDTH].astype(BF16)
        vt = kv[:, MEM_WIDTH:2 * MEM_WIDTH].T.astype(BF16)
        for h in range(MEM_HEADS):
            vt_s[h, 0:MEM_HEAD_DIM, :] = vt[h * MEM_HEAD_DIM:(h + 1) * MEM_HEAD_DIM, :]
            vt_s[h, MEM_HEAD_DIM:, :] = jnp.ones((MEM_V_ROWS - MEM_HEAD_DIM, vt.shape[1]), BF16)

    heads = range(MEM_HEADS)
    cols = [slice(h * MEM_HEAD_DIM, (h + 1) * MEM_HEAD_DIM) for h in heads]
    s_t = [_dot_nt(k_s[:, cols[h]], mq_ref[:, cols[h]]) for h in heads]
    e = []
    for h in heads:
        m = jnp.max(s_t[h], axis=0, keepdims=True)
        e.append(jnp.exp2((s_t[h] - m) * (MEM_HEAD_DIM ** -0.5 * LOG2E)).astype(BF16))
    o_t = [_dot(vt_s[h], e[h]) for h in heads]
    for h in heads:
        out_s[cols[h], :] = o_t[h][0:MEM_HEAD_DIM, :] * (1.0 / o_t[h][MEM_HEAD_DIM:MEM_HEAD_DIM + 1, :])
    o_ref[...] = out_s[...].T.astype(BF16)


def _memattn_call(mq, mem2, wkv, batch, seq):
    m_len = mem2.shape[0] // batch
    d = mem2.shape[1]
    per = seq // MEM_TM
    return pl.pallas_call(
        _memattn_kernel,
        grid=(batch, per),
        in_specs=[
            pl.BlockSpec((MEM_TM, MEM_WIDTH), lambda b, i: (b * per + i, 0)),
            pl.BlockSpec((m_len, d), lambda b, i: (b, 0)),
            pl.BlockSpec((d, 2 * MEM_WIDTH), lambda b, i: (0, 0)),
        ],
        out_specs=pl.BlockSpec((MEM_TM, MEM_WIDTH), lambda b, i: (b * per + i, 0)),
        out_shape=jax.ShapeDtypeStruct((batch * seq, MEM_WIDTH), BF16),
        scratch_shapes=[
            pltpu.VMEM((m_len, MEM_WIDTH), BF16),
            pltpu.VMEM((MEM_HEADS, MEM_V_ROWS, m_len), BF16),
            pltpu.VMEM((MEM_WIDTH, MEM_TM), F32),
        ],
        compiler_params=pltpu.CompilerParams(
            dimension_semantics=("parallel", "arbitrary"), vmem_limit_bytes=VMEM_LIMIT),
        name="memattn",
    )(mq, mem2, wkv)


MERGE_TM = 512


def _merge_kernel(x_ref, onsa_ref, osgu_ref, omem_ref, wgt_ref, wbn32_ref, wbs32_ref, wbm32_ref, wo32_ref,
                  g_ref, b_ref, out_ref, wbn_ref, wbs_ref, wbm_ref, wo_ref, wg_ref):
    @pl.when(pl.program_id(0) == 0)
    def _():
        for dst, src in ((wbn_ref, wbn32_ref), (wbs_ref, wbs32_ref), (wbm_ref, wbm32_ref),
                         (wo_ref, wo32_ref)):
            dst[...] = src[...].astype(BF16)
        _transpose_weight(wgt_ref, wg_ref)

    d = x_ref.shape[1]
    branches = ((onsa_ref, wbn_ref), (osgu_ref, wbs_ref), (omem_ref, wbm_ref))
    halves = [slice(h * (MERGE_TM // 2), (h + 1) * (MERGE_TM // 2)) for h in range(2)]
    xs = [x_ref[rows, :] for rows in halves]
    xbs = [x.astype(BF16) for x in xs]
    logits = [[_dot(xb, wg_ref[:, br * d:(br + 1) * d]) for br in range(len(branches))] for xb in xbs]
    projs = [[_dot(o_r[rows, :], w_r[...]) for o_r, w_r in branches] for rows in halves]
    ys = []
    for h in range(2):
        y = None
        for br in range(len(branches)):
            term = _sigmoid(logits[h][br]) * projs[h][br]
            y = term if y is None else y + term
        ys.append(y.astype(BF16))
    outs = [_dot(y, wo_ref[...]) for y in ys]
    for h, rows in enumerate(halves):
        out_ref[rows, :] = _layer_norm(DN_ALPHA * xs[h] + outs[h], g_ref[...], b_ref[...])


def _merge_call(x2, onsa, osgu, omem, wg, wbn, wbs, wbm, wo, g1, b1):
    t, d = x2.shape
    row = lambda i: (i, 0)
    c2 = lambda i: (0, 0)
    full = lambda a: pl.BlockSpec(a.shape, c2)
    once = lambda a: pl.BlockSpec(a.shape, c2, pipeline_mode=pl.Buffered(1))
    return pl.pallas_call(
        _merge_kernel,
        grid=(t // MERGE_TM,),
        in_specs=[
            pl.BlockSpec((MERGE_TM, d), row),
            pl.BlockSpec((MERGE_TM, onsa.shape[1]), row),
            pl.BlockSpec((MERGE_TM, osgu.shape[1]), row),
            pl.BlockSpec((MERGE_TM, omem.shape[1]), row),
            pl.BlockSpec((3 * d, d), c2, pipeline_mode=pl.Buffered(1)),
            once(wbn), once(wbs), once(wbm), once(wo), full(g1), full(b1),
        ],
        out_specs=pl.BlockSpec((MERGE_TM, d), row),
        out_shape=jax.ShapeDtypeStruct((t, d), F32),
        scratch_shapes=[pltpu.VMEM(a.shape, BF16) for a in (wbn, wbs, wbm, wo)]
        + [pltpu.VMEM((d, 3 * d), BF16)],
        compiler_params=pltpu.CompilerParams(
            dimension_semantics=("arbitrary",), vmem_limit_bytes=VMEM_LIMIT),
        name="merge",
    )(x2, onsa, osgu, omem, wg, wbn, wbs, wbm, wo, g1, b1)


MOE_TM = 1024
MOE_CH = 144
_YS_ROWS = -(-(MOE_TM + N_GROUPS * MOE_CH) // 256) * 256
_YS_TYPICAL = -(-(2 * N_GROUPS * MOE_CH) // 256) * 256
_R_SLOT = 8
_DEST_LANE = 3 * _R_SLOT
_HID = EXPERTS_PER_GROUP * EXPERT_FF


def _moe_kernel(x_ref, tri_ref, wr_ref, br_ref, wg_ref, wu_ref, wd_ref, g_ref, b_ref, out_ref,
                xa_s, tok_s, keyr_s, ys_s, cnt_s, base_s):
    grp = pl.program_id(1)
    tm = x_ref.shape[0]
    d = x_ref.shape[1]

    @pl.when(grp == 0)
    def _route():
        xb = x_ref[...].astype(BF16)
        xa_s[:, 0:d] = xb
        lt = _dot_nt(wr_ref[...], xb) + br_ref[...]
        row = lax.broadcasted_iota(jnp.int32, (_R_SLOT, tm), 0)
        gl = jnp.where(row < N_GROUPS, lt[0:_R_SLOT], NEG)
        gmax = jnp.max(gl, axis=0, keepdims=True)
        gidx = jnp.min(jnp.where(gl == gmax, row, _R_SLOT), axis=0, keepdims=True)
        gprob = 1.0 / jnp.sum(jnp.exp(gl - gmax), axis=0, keepdims=True)
        el = lt[_R_SLOT:2 * _R_SLOT]
        for k in range(1, N_GROUPS):
            el = jnp.where(gidx == k, lt[(k + 1) * _R_SLOT:(k + 2) * _R_SLOT], el)
        ee = jnp.exp(el - jnp.max(el, axis=0, keepdims=True))
        ep = ee / jnp.sum(ee, axis=0, keepdims=True)
        t1 = jnp.max(ep, axis=0, keepdims=True)
        i1 = jnp.min(jnp.where(ep == t1, row, _R_SLOT), axis=0, keepdims=True)
        rest = row != i1
        t2 = jnp.max(jnp.where(rest, ep, -1.0), axis=0, keepdims=True)
        i2 = jnp.min(jnp.where(rest & (ep == t2), row, _R_SLOT), axis=0, keepdims=True)
        den = t1 + t2
        cwf = (jnp.where(row == i1, t1 / den, 0.0) + jnp.where(row == i2, t2 / den, 0.0)) * gprob
        hi = cwf.astype(BF16).astype(F32)
        r1 = cwf - hi
        lo = r1.astype(BF16).astype(F32)
        lo2 = (r1 - lo).astype(BF16).astype(F32)

        onehot = jnp.where(row == gidx, 1.0, 0.0)
        tri = tri_ref[...]
        onehot_b = onehot.astype(BF16)
        ranks, before = [], jnp.zeros((_R_SLOT, 1), F32)
        for b in range(tm // LANES):
            blk = slice(b * LANES, (b + 1) * LANES)
            ranks.append(_dot_nt(onehot_b[:, blk], tri) + before)
            before = before + jnp.sum(onehot[:, blk], axis=1, keepdims=True)
        rank = jnp.concatenate(ranks, axis=1)
        keyr_s[...] = jnp.where(onehot > 0.5, rank, -1.0)
        base = jnp.int32(0)
        basev = jnp.zeros((1, tm), F32)
        for k in range(N_GROUPS):
            n_k = jnp.sum(onehot[k:k + 1, :]).astype(jnp.int32)
            cnt_s[k] = n_k
            base_s[k] = base
            basev = jnp.where(gidx == k, base.astype(F32), basev)
            base = base + ((n_k + MOE_CH - 1) // MOE_CH) * MOE_CH
        base_s[N_GROUPS] = base
        dest =jnp.sum(rank * onehot, axis=0, keepdims=True) + basev
        tok = jnp.concatenate(
            [hi, lo, lo2, jnp.broadcast_to(dest, (_R_SLOT, tm)),
             jnp.zeros((LANES - 4 * _R_SLOT, tm), F32)], axis=0).T
        tok_s[...] = tok
        lane = lax.broadcasted_iota(jnp.int32, tok.shape, 1)
        xa_s[:, d:d + LANES] = jnp.where(lane < _DEST_LANE, tok, 0.0).astype(BF16)
        ys_s[...] = jnp.zeros(ys_s.shape, BF16)

    n_rows = cnt_s[grp]
    row0 = base_s[grp]
    keyr = keyr_s[pl.ds(grp, 1), :]
    half = _HID // 2

    ch = MOE_CH

    def gather(k):
        r_row = (lax.broadcasted_iota(jnp.int32, (ch, tm), 0) + k * MOE_CH).astype(F32)
        pick = jnp.where(keyr == r_row, 1.0, 0.0).astype(BF16)
        ga = _dot(pick, xa_s[...])
        xg = ga[:, 0:d].astype(BF16)
        cwg = ga[:, d:d + LANES]
        cs = (cwg + pltpu.roll(cwg, LANES - _R_SLOT, 1)
              + pltpu.roll(cwg, LANES - 2 * _R_SLOT, 1))
        return xg, cs

    def experts_ffn(k, xg, cs):
        experts = range(EXPERTS_PER_GROUP)
        per_half = EXPERTS_PER_GROUP // 2
        hg = [_dot(xg, wg_ref[0, e]) for e in experts]
        hu = [_dot(xg, wu_ref[0, e]) for e in experts]
        y = None
        for h in range(2):
            hid = jnp.concatenate(
                [(((hg[e] * _sigmoid(hg[e])) * hu[e]) * cs[:, e:e + 1]).astype(BF16)
                 for e in range(h * per_half, (h + 1) * per_half)], axis=1)
            term = _dot(hid, wd_ref[0, h * half:(h + 1) * half, :])
            y = term if y is None else y + term
        ys_s[pl.ds(pl.multiple_of(row0 + k * MOE_CH, 16), ch), :] = y.astype(BF16)

    def sweep(k, carry):
        experts_ffn(k, *gather(k))
        return carry

    n_sweeps = (n_rows + MOE_CH - 1) // MOE_CH

    @pl.when(n_sweeps == 2)
    def _():
        g0 = gather(0)
        g1 = gather(1)
        experts_ffn(0, *g0)
        experts_ffn(1, *g1)

    @pl.when(n_sweeps != 2)
    def _():
        lax.fori_loop(0, n_sweeps, sweep, 0)

    def fin(k_rows):
        n_q = 4
        q_rows = tm // n_q
        r_col = lax.broadcasted_iota(jnp.int32, (q_rows, k_rows), 1).astype(F32)
        parts = [slice(i * q_rows, (i + 1) * q_rows) for i in range(n_q)]
        puts = [jnp.where(tok_s[rows, _DEST_LANE:_DEST_LANE + 1] == r_col, 1.0, 0.0).astype(BF16)
                for rows in parts]
        fs = [_dot(put, ys_s[0:k_rows, :]) for put in puts]
        for rows, f in zip(parts, fs):
            out_ref[rows, :] = _layer_norm(DN_ALPHA * x_ref[rows, :] + f, g_ref[...], b_ref[...])

    last = grp == N_GROUPS - 1
    used = base_s[N_GROUPS]
    pl.when(last & (used <= _YS_TYPICAL))(functools.partial(fin, _YS_TYPICAL))
    pl.when(last & (used > _YS_TYPICAL))(functools.partial(fin, _YS_ROWS))


def _moe_call(x1, wr, br, wg, wu, wd, g2, b2):
    t, d = x1.shape
    row = lambda i, g: (i, 0)
    c2 = lambda i, g: (0, 0)
    idx = np.arange(LANES)
    tri = jnp.asarray(idx[None, :] < idx[:, None], dtype=BF16)
    return pl.pallas_call(
        _moe_kernel,
        grid=(t // MOE_TM, N_GROUPS),
        in_specs=[
            pl.BlockSpec((MOE_TM, d), row),
            pl.BlockSpec((LANES, LANES), c2),
            pl.BlockSpec(wr.shape, c2),
            pl.BlockSpec(br.shape, c2),
            pl.BlockSpec((1, EXPERTS_PER_GROUP, d, EXPERT_FF), lambda i, g: (g, 0, 0, 0)),
            pl.BlockSpec((1, EXPERTS_PER_GROUP, d, EXPERT_FF), lambda i, g: (g, 0, 0, 0)),
            pl.BlockSpec((1, _HID, d), lambda i, g: (g, 0, 0)),
            pl.BlockSpec(g2.shape, c2),
            pl.BlockSpec(b2.shape, c2),
        ],
        out_specs=pl.BlockSpec((MOE_TM, d), row),
        out_shape=jax.ShapeDtypeStruct((t, d), F32),
        scratch_shapes=[
            pltpu.VMEM((MOE_TM, d + LANES), BF16),
            pltpu.VMEM((MOE_TM, LANES), F32),
            pltpu.VMEM((_R_SLOT, MOE_TM), F32),
            pltpu.VMEM((_YS_ROWS, d), BF16),
            pltpu.SMEM((N_GROUPS,), jnp.int32),
            pltpu.SMEM((N_GROUPS + 1,), jnp.int32),
        ],
        compiler_params=pltpu.CompilerParams(
            dimension_semantics=("parallel", "arbitrary"), vmem_limit_bytes=MOE_VMEM_LIMIT),
        name="moe",
    )(x1, tri, wr, br, wg, wu, wd, g2, b2)


def _rope_table(seq):
    half = ROPE_DIM // 2
    inv = (ROPE_THETA ** (-np.arange(0, ROPE_DIM, 2, dtype=np.float32) / ROPE_DIM)).astype(np.float32)
    ang = np.arange(seq, dtype=np.float32)[:, None] * inv[None, :]
    cos, sin = np.cos(ang), np.sin(ang)
    rest = HEAD_DIM - ROPE_DIM
    one = np.ones((seq, rest), np.float32)
    zero = np.zeros((seq, rest), np.float32)
    zh = np.zeros((seq, half), np.float32)
    c_h = np.concatenate([cos, cos, one], axis=1)
    dn_h = np.concatenate([-sin, zh, zero], axis=1)
    up_h = np.concatenate([zh, sin, zero], axis=1)
    rep = LANES // HEAD_DIM
    table = np.concatenate([np.tile(c_h, (1, rep)), np.tile(dn_h, (1, rep)), np.tile(up_h, (1, rep))], axis=1)
    return jnp.asarray(table, dtype=F32)


def _expand_cmp_weights(pe, w1, w2):
    g_n = NSA_KV_GROUPS
    w1r = w1.astype(BF16).reshape(2, CMP_STRIDE, HEAD_DIM, CMP_HIDDEN)
    w2b = w2.astype(BF16)
    w1e = jnp.concatenate(
        [jnp.stack([w1r if g == e else jnp.zeros_like(w1r) for g in range(g_n)], axis=2)
         .reshape(2, A_W, CMP_HIDDEN) for e in range(g_n)], axis=2)
    w2e = jnp.concatenate(
        [jnp.concatenate([w2b if g == e else jnp.zeros_like(w2b) for e in range(g_n)], axis=1)
         for g in range(g_n)], axis=0)
    per = pe.reshape(2, CMP_STRIDE, 1, HEAD_DIM)
    pee = jnp.broadcast_to(per, (2, CMP_STRIDE, g_n, HEAD_DIM)).reshape(2, A_W)
    pee = jnp.concatenate([pee, jnp.zeros((6, A_W), pe.dtype)], axis=0)
    return pee.astype(F32), w1e.astype(BF16), w2e.astype(BF16)


def kernel(x, mem, w_in, cmp_pe_k, cmp_w1_k, cmp_w2_k, cmp_pe_v, cmp_w1_v, cmp_w2_v, sgu_ln_g, sgu_ln_b,
           sgu_w_s, sgu_b_s, w_mem_kv, w_br_nsa, w_br_sgu, w_br_mem, w_o, ln1_g, ln1_b, w_router_group,
           b_router_group, w_router_expert, b_router_expert, w_exp_gate, w_exp_up, w_exp_down, ln2_g, ln2_b):
    batch, seq, d = x.shape
    t = batch * seq
    assert w_in.shape[0] == DEPTH == 1
    assert seq % PROJ_TM == 0 and seq // CMP_STRIDE == N_A and seq // SLC_BLOCK == N_SEL

    offs = [int(v) for v in np.cumsum(
        [0, Q_W, KV_W, KV_W, KV_W, KV_W, KV_W, KV_W, GATE_W, 2 * SGU_WIDTH, MEM_WIDTH, 3 * d])]
    assert offs[-1] == w_in.shape[2]
    w_t = jnp.transpose(w_in[0])
    seg = lambda i: w_t[offs[i]:offs[i + 1]]
    gate_rows = jnp.pad(seg(7), ((0, LANES - GATE_W), (0, 0)))
    wp_t = jnp.concatenate([seg(0), seg(1), seg(3), seg(5), seg(2), seg(4), seg(6), gate_rows, seg(8), seg(9)],
                           axis=0).astype(BF16)
    w_mt = seg(10).astype(BF16)
    assert wp_t.shape[0] == _PROJ_COLS
    rope = _rope_table(seq)
    tril = jnp.tril(jnp.ones((SGU_CHUNK, SGU_CHUNK), dtype=bool))
    ws = jnp.where(tril[None], sgu_w_s[0], 0.0).astype(BF16)
    bs = jnp.repeat(sgu_b_s[0].T, SGU_WIDTH // SGU_GROUPS, axis=1)

    x2 = x.reshape(t, d)
    (qpad, kc, ksl, kwn, vc, vslt, vwnt, gt, osgu, mq) = _proj_call(
        x2, wp_t, rope, sgu_ln_g[0][None], sgu_ln_b[0][None], ws, bs, seq)

    pek, w1k, w2k = _expand_cmp_weights(cmp_pe_k[0], cmp_w1_k[0], cmp_w2_k[0])
    pev, w1v, w2v = _expand_cmp_weights(cmp_pe_v[0], cmp_w1_v[0], cmp_w2_v[0])
    kcmp, vcmp = _compress_call(kc, vc, pek, pev, w1k, w2k, w1v, w2v, batch)

    vcmpt = vcmp.reshape(batch, N_A, KV_W).transpose(0, 2, 1).reshape(batch * KV_W, N_A)
    ci = np.arange(N_A)
    sj = np.arange(N_SEL)
    overlap = ((ci[None, :] * CMP_STRIDE + CMP_LEN - 1 >= sj[:, None] * SLC_BLOCK)
               & (ci[None, :] * CMP_STRIDE <= sj[:, None] * SLC_BLOCK + SLC_BLOCK - 1)
               & (ci[None, :] < (seq - CMP_LEN) // CMP_STRIDE + 1))
    ot = jnp.asarray(overlap, dtype=BF16)
    onsa = _nsa_call(qpad, ksl, kwn, vslt, vwnt, kcmp, vcmpt, gt, ot, batch, seq)

    omem = _memattn_call(mq, mem.reshape(batch * mem.shape[1], d), w_mem_kv[0], batch, seq)

    x1 = _merge_call(x2, onsa, osgu, omem, w_mt, w_br_nsa[0], w_br_sgu[0], w_br_mem[0], w_o[0],
                     ln1_g[0][None], ln1_b[0][None])

    assert EXPERTS_PER_GROUP == _R_SLOT and N_GROUPS <= _R_SLOT
    n_r = _R_SLOT + N_GROUPS * EXPERTS_PER_GROUP
    wr = jnp.concatenate([jnp.pad(w_router_group[0], ((0, 0), (0, _R_SLOT - N_GROUPS))),
                          w_router_expert[0]], axis=1)
    wr = jnp.pad(wr, ((0, 0), (0, LANES - n_r))).astype(BF16).T
    br = jnp.concatenate([jnp.pad(b_router_group[0], (0, _R_SLOT - N_GROUPS)), b_router_expert[0]])
    br = jnp.pad(br, (0, LANES - n_r))[:, None]
    wg = w_exp_gate[0].astype(BF16)
    wu = w_exp_up[0].astype(BF16)
    wd = w_exp_down[0].reshape(N_GROUPS, _HID, d).astype(BF16)
    out = _moe_call(x1, wr, br, wg, wu, wd, ln2_g[0][None], ln2_b[0][None])
    return out.reshape(batch, seq, d)
```
